```python
import jax, jax.numpy as jnp
from jax import lax
import numpy as np

D_MODEL = 2048
BATCH = 8
SEQ = 8192
DEPTH = 1

FOX_HEADS = 8
FOX_HEAD_DIM = 128
FOX_WIDTH = FOX_HEADS * FOX_HEAD_DIM
SWA_Q_HEADS = 16
SWA_KV_HEADS = 4
SWA_HEAD_DIM = 64
SWA_GROUP = SWA_Q_HEADS // SWA_KV_HEADS
SWA_WIDTH = SWA_Q_HEADS * SWA_HEAD_DIM
SWA_KV_WIDTH = SWA_KV_HEADS * SWA_HEAD_DIM
WINDOW = 128
Q_BLOCK = 128
LN_EPS = 1e-5
NEG_INF = -1e30
DEEPNORM_ALPHA = (2.0 * DEPTH) ** 0.25
DEEPNORM_BETA = (8.0 * DEPTH) ** -0.25

_SPLIT_SIZES = (FOX_WIDTH, FOX_WIDTH, FOX_WIDTH, FOX_HEADS,
                SWA_WIDTH, SWA_KV_WIDTH, SWA_KV_WIDTH,
                FOX_WIDTH, SWA_WIDTH,
                D_MODEL, D_MODEL)
IN_WIDTH = sum(_SPLIT_SIZES)
SPLIT_IDX = tuple(int(v) for v in np.cumsum(_SPLIT_SIZES)[:-1])

kernel_name = "hybrid_fox_swa_gated_deepnorm_adaln"


def _layer_norm(x):
    xf = x.astype(jnp.float32)
    mu = jnp.mean(xf, axis=-1, keepdims=True)
    var = jnp.mean(jnp.square(xf - mu), axis=-1, keepdims=True)
    return ((xf - mu) * lax.rsqrt(var + LN_EPS)).astype(x.dtype)


def _forgetting_attention(q, k, v, fgate_logit, b_f):
    B, S, H, Dh = q.shape
    log_f = jax.nn.log_sigmoid(fgate_logit.astype(jnp.float32) + b_f.astype(jnp.float32))
    cum = jnp.cumsum(log_f, axis=1)
    cum_k = cum.transpose(0, 2, 1)[:, :, None, :]
    nb = S // Q_BLOCK
    qb = q.reshape(B, nb, Q_BLOCK, H, Dh).transpose(1, 0, 2, 3, 4)
    cqb = cum.reshape(B, nb, Q_BLOCK, H).transpose(1, 0, 2, 3)
    qpos = jnp.arange(S, dtype=jnp.int32).reshape(nb, Q_BLOCK)
    kpos = jnp.arange(S, dtype=jnp.int32)
    scale = Dh ** -0.5

    def block(args):
        qi, ci, pi = args
        s = jnp.einsum('bqhd,bkhd->bhqk', qi, k).astype(jnp.float32) * scale
        s = s + ci.transpose(0, 2, 1)[..., None] - cum_k
        mask = kpos[None, :] <= pi[:, None]
        s = jnp.where(mask[None, None], s, NEG_INF)
        p = jax.nn.softmax(s, axis=-1).astype(v.dtype)
        return jnp.einsum('bhqk,bkhd->bqhd', p, v)

    out = lax.map(block, (qb, cqb, qpos))
    return out.transpose(1, 0, 2, 3, 4).reshape(B, S, H * Dh)


def _sliding_window_attention(q, k, v, sinks):
    B, S, HQ, d = q.shape
    nb = S // WINDOW
    qb = q.reshape(B, nb, WINDOW, SWA_KV_HEADS, SWA_GROUP, d)
    kb = k.reshape(B, nb, WINDOW, SWA_KV_HEADS, d)
    vb = v.reshape(B, nb, WINDOW, SWA_KV_HEADS, d)
    pad = ((0, 0), (1, 0), (0, 0), (0, 0), (0, 0))
    kk = jnp.concatenate([jnp.pad(kb, pad)[:, :-1], kb], axis=2)
    vv = jnp.concatenate([jnp.pad(vb, pad)[:, :-1], vb], axis=2)
    s = jnp.einsum('bnqhgd,bnkhd->bnhgqk', qb, kk).astype(jnp.float32) * (d ** -0.5)
    i = jnp.arange(WINDOW, dtype=jnp.int32)[:, None]
    j = jnp.arange(2 * WINDOW, dtype=jnp.int32)[None, :]
    dist = (i - j + WINDOW)
    n = jnp.arange(nb, dtype=jnp.int32)[:, None, None]
    kpos_abs = n * WINDOW - WINDOW + j[None]
    valid = (dist[None] >= 0) & (dist[None] < WINDOW) & (kpos_abs >= 0)
    slopes = 2.0 ** (-8.0 * (jnp.arange(HQ, dtype=jnp.float32) + 1.0) / HQ)
    slopes = slopes.reshape(SWA_KV_HEADS, SWA_GROUP)
    s = s - slopes[:, :, None, None] * dist.astype(jnp.float32)
    s = jnp.where(valid[None, :, None, None], s, NEG_INF)
    sink = jnp.broadcast_to(sinks.astype(jnp.float32).reshape(SWA_KV_HEADS, SWA_GROUP)[:, :, None, None],
                            s.shape[:-1] + (1,))
    p = jax.nn.softmax(jnp.concatenate([s, sink], axis=-1), axis=-1)[..., :-1].astype(v.dtype)
    out = jnp.einsum('bnhgqk,bnkhd->bnqhgd', p, vv)
    return out.reshape(B, S, HQ * d)


def _fwd_setup_inputs(seed: int = 0) -> dict:
    key = jax.random.key(seed)
    ks = jax.random.split(key, 13)
    f32 = jnp.float32
    nrm = lambda k, shape, s: jax.random.normal(k, shape, f32) * s
    x = nrm(ks[0], (BATCH, SEQ, D_MODEL), 1.0)
    c = nrm(ks[1], (BATCH, D_MODEL), 1.0)
    w_ada = nrm(ks[2], (DEPTH, D_MODEL, 3 * D_MODEL), 0.5 * D_MODEL ** -0.5)
    b_ada = nrm(ks[3], (DEPTH, 3 * D_MODEL), 0.02)
    w_in = nrm(ks[4], (DEPTH, D_MODEL, IN_WIDTH), D_MODEL ** -0.5)
    b_f = jnp.linspace(2.0, 7.0, FOX_HEADS, dtype=f32)[None, :] + nrm(ks[5], (DEPTH, FOX_HEADS), 0.1)
    attn_sinks = nrm(ks[6], (DEPTH, SWA_Q_HEADS), 1.0)
    w_br_fox = nrm(ks[7], (DEPTH, FOX_WIDTH, D_MODEL), DEEPNORM_BETA * FOX_WIDTH ** -0.5)
    w_br_swa = nrm(ks[8], (DEPTH, SWA_WIDTH, D_MODEL), DEEPNORM_BETA * SWA_WIDTH ** -0.5)
    w_out = nrm(ks[9], (DEPTH, D_MODEL, D_MODEL), DEEPNORM_BETA * D_MODEL ** -0.5)
    ln_g = 1.0 + nrm(ks[10], (DEPTH, D_MODEL), 0.02)
    ln_b = nrm(ks[11], (DEPTH, D_MODEL), 0.02)
    return {"x": x, "c": c, "w_ada": w_ada, "b_ada": b_ada, "w_in": w_in, "b_f": b_f,
            "attn_sinks": attn_sinks, "w_br_fox": w_br_fox, "w_br_swa": w_br_swa,
            "w_out": w_out, "ln_g": ln_g, "ln_b": ln_b}


def _fwd_reference(x, c, w_ada, b_ada, w_in, b_f, attn_sinks, w_br_fox, w_br_swa, w_out, ln_g, ln_b):
    B, S, D = x.shape
    for l in range(DEPTH):
        ada = c @ w_ada[l] + b_ada[l]
        shift, scale, gate = jnp.split(ada, 3, axis=-1)
        h = _layer_norm(x) * (1.0 + scale[:, None, :]) + shift[:, None, :]
        proj = h @ w_in[l]
        (fq, fk, fv, flog, sq, sk, sv, g_fox, g_swa, m_fox, m_swa) = jnp.split(proj, SPLIT_IDX, axis=-1)
        o_fox = _forgetting_attention(fq.reshape(B, S, FOX_HEADS, FOX_HEAD_DIM),
                                      fk.reshape(B, S, FOX_HEADS, FOX_HEAD_DIM),
                                      fv.reshape(B, S, FOX_HEADS, FOX_HEAD_DIM),
                                      flog, b_f[l])
        o_swa = _sliding_window_attention(sq.reshape(B, S, SWA_Q_HEADS, SWA_HEAD_DIM),
                                          sk.reshape(B, S, SWA_KV_HEADS, SWA_HEAD_DIM),
                                          sv.reshape(B, S, SWA_KV_HEADS, SWA_HEAD_DIM),
                                          attn_sinks[l])
        y_fox = (o_fox * jax.nn.silu(g_fox)) @ w_br_fox[l]
        y_swa = (o_swa * jax.nn.silu(g_swa)) @ w_br_swa[l]
        merged = jax.nn.sigmoid(m_fox) * y_fox + jax.nn.sigmoid(m_swa) * y_swa
        sub = merged @ w_out[l]
        z = DEEPNORM_ALPHA * x + gate[:, None, :] * sub
        x = _layer_norm(z) * ln_g[l] + ln_b[l]
    return x


import jax as _jax
import jax.numpy as _jnp

TWIN_FORMAT = 'train_step'
FWD_PARAMS = ['x', 'c', 'w_ada', 'b_ada', 'w_in', 'b_f', 'attn_sinks', 'w_br_fox', 'w_br_swa', 'w_out', 'ln_g', 'ln_b']
TWIN_WEIGHTS = ['w_ada', 'b_ada', 'w_in', 'b_f', 'attn_sinks', 'w_br_fox', 'w_br_swa', 'w_out', 'ln_g', 'ln_b']
TWIN_DIFF_INPUT = 'x'
TWIN_INPUTS = ['x', 'c', 'w_ada', 'b_ada', 'w_in', 'b_f', 'attn_sinks', 'w_br_fox', 'w_br_swa', 'w_out', 'ln_g', 'ln_b', 'loss_target', 'm_w_ada', 'm_b_ada', 'm_w_in', 'm_b_f', 'm_attn_sinks', 'm_w_br_fox', 'm_w_br_swa', 'm_w_out', 'm_ln_g', 'm_ln_b', 'v_w_ada', 'v_b_ada', 'v_w_in', 'v_b_f', 'v_attn_sinks', 'v_w_br_fox', 'v_w_br_swa', 'v_w_out', 'v_ln_g', 'v_ln_b']
TWIN_OUTPUTS = ['loss', 'grad_x', 'grad_w_ada', 'grad_b_ada', 'grad_w_in', 'grad_b_f', 'grad_attn_sinks', 'grad_w_br_fox', 'grad_w_br_swa', 'grad_w_out', 'grad_ln_g', 'grad_ln_b', 'delta_w_ada', 'delta_b_ada', 'delta_w_in', 'delta_b_f', 'delta_attn_sinks', 'delta_w_br_fox', 'delta_w_br_swa', 'delta_w_out', 'delta_ln_g', 'delta_ln_b', 'new_m_w_ada', 'new_m_b_ada', 'new_m_w_in', 'new_m_b_f', 'new_m_attn_sinks', 'new_m_w_br_fox', 'new_m_w_br_swa', 'new_m_w_out', 'new_m_ln_g', 'new_m_ln_b', 'new_v_w_ada', 'new_v_b_ada', 'new_v_w_in', 'new_v_b_f', 'new_v_attn_sinks', 'new_v_w_br_fox', 'new_v_w_br_swa', 'new_v_w_out', 'new_v_ln_g', 'new_v_ln_b']
TWIN_LEAF_KINDS = {'loss': 'loss', 'grad_x': 'grad_x', 'grad_w_ada': 'grad_w', 'grad_b_ada': 'grad_w', 'grad_w_in': 'grad_w', 'grad_b_f': 'grad_w', 'grad_attn_sinks': 'grad_w', 'grad_w_br_fox': 'grad_w', 'grad_w_br_swa': 'grad_w', 'grad_w_out': 'grad_w', 'grad_ln_g': 'grad_w', 'grad_ln_b': 'grad_w', 'delta_w_ada': 'delta_w', 'delta_b_ada': 'delta_w', 'delta_w_in': 'delta_w', 'delta_b_f': 'delta_w', 'delta_attn_sinks': 'delta_w', 'delta_w_br_fox': 'delta_w', 'delta_w_br_swa': 'delta_w', 'delta_w_out': 'delta_w', 'delta_ln_g': 'delta_w', 'delta_ln_b': 'delta_w', 'new_m_w_ada': 'new_m', 'new_m_b_ada': 'new_m', 'new_m_w_in': 'new_m', 'new_m_b_f': 'new_m', 'new_m_attn_sinks': 'new_m', 'new_m_w_br_fox': 'new_m', 'new_m_w_br_swa': 'new_m', 'new_m_w_out': 'new_m', 'new_m_ln_g': 'new_m', 'new_m_ln_b': 'new_m', 'new_v_w_ada': 'new_v', 'new_v_b_ada': 'new_v', 'new_v_w_in': 'new_v', 'new_v_b_f': 'new_v', 'new_v_attn_sinks': 'new_v', 'new_v_w_br_fox': 'new_v', 'new_v_w_br_swa': 'new_v', 'new_v_w_out': 'new_v', 'new_v_ln_g': 'new_v', 'new_v_ln_b': 'new_v'}


def _forward(args):
    return _fwd_reference(*[args[k] for k in FWD_PARAMS])


def _output_shape():
    def fwd():
        inp = _fwd_setup_inputs(0)
        return _fwd_reference(*[inp[k] for k in FWD_PARAMS])
    out = _jax.eval_shape(fwd)
    return out.shape, out.dtype

N_MICROBATCH = 1
ADAM_LR = 0.001
ADAM_B1 = 0.9
ADAM_B2 = 0.999
ADAM_EPS = 1e-08
ADAM_WD = 0.01
ADAM_STEP = 10
PER_EXAMPLE_BATCH_AXIS = {'x': 0, 'c': 0, 'loss_target': 0}
SHARED_INPUTS = []
_WEIGHT_DTYPES = {'w_ada': _jnp.float32, 'b_ada': _jnp.float32, 'w_in': _jnp.float32, 'b_f': _jnp.float32, 'attn_sinks': _jnp.float32, 'w_br_fox': _jnp.float32, 'w_br_swa': _jnp.float32, 'w_out': _jnp.float32, 'ln_g': _jnp.float32, 'ln_b': _jnp.float32}
MOMENT_SCALE = {'w_ada': 1.218971e-02, 'b_ada': 1.274407e-02, 'w_in': 5.259497e-03, 'b_f': 3.960772e-02, 'attn_sinks': 7.665484e-03, 'w_br_fox': 8.349635e-03, 'w_br_swa': 8.183222e-03, 'w_out': 1.161091e-02, 'ln_g': 3.196762e+01, 'ln_b': 4.824007e-01}


def _to_microbatches(a, axis):
    t = _jnp.moveaxis(a, axis, 0)
    t = t.reshape((N_MICROBATCH, t.shape[0] // N_MICROBATCH) + t.shape[1:])
    return _jnp.moveaxis(t, 1, axis + 1)


def setup_inputs(seed: int = 0) -> dict:
    inp = _fwd_setup_inputs(seed)
    key = _jax.random.fold_in(_jax.random.key(seed), 7919)
    shape, _ = _output_shape()
    out = dict(inp)
    out["loss_target"] = _jax.random.normal(_jax.random.fold_in(key, 0), shape, _jnp.float32)
    for i, name in enumerate(TWIN_WEIGHTS):
        w = inp[name].astype(_jnp.float32)
        if MOMENT_SCALE is None:
            s = _jnp.sqrt(_jnp.mean(_jnp.square(w)) + 1e-30)
        else:
            s = MOMENT_SCALE[name]
        km, kv = _jax.random.split(_jax.random.fold_in(key, i + 1))
        out[name] = w
        out["m_" + name] = s * _jax.random.normal(km, w.shape, _jnp.float32)
        out["v_" + name] = (s * s) * _jax.random.uniform(kv, w.shape, _jnp.float32, 0.5, 1.5)
    if N_MICROBATCH > 1:
        for name, axis in PER_EXAMPLE_BATCH_AXIS.items():
            out[name] = _to_microbatches(out[name], axis)
    return {'x': out['x'], 'c': out['c'], 'w_ada': out['w_ada'], 'b_ada': out['b_ada'], 'w_in': out['w_in'], 'b_f': out['b_f'], 'attn_sinks': out['attn_sinks'], 'w_br_fox': out['w_br_fox'], 'w_br_swa': out['w_br_swa'], 'w_out': out['w_out'], 'ln_g': out['ln_g'], 'ln_b': out['ln_b'], 'loss_target': out['loss_target'], 'm_w_ada': out['m_w_ada'], 'm_b_ada': out['m_b_ada'], 'm_w_in': out['m_w_in'], 'm_b_f': out['m_b_f'], 'm_attn_sinks': out['m_attn_sinks'], 'm_w_br_fox': out['m_w_br_fox'], 'm_w_br_swa': out['m_w_br_swa'], 'm_w_out': out['m_w_out'], 'm_ln_g': out['m_ln_g'], 'm_ln_b': out['m_ln_b'], 'v_w_ada': out['v_w_ada'], 'v_b_ada': out['v_b_ada'], 'v_w_in': out['v_w_in'], 'v_b_f': out['v_b_f'], 'v_attn_sinks': out['v_attn_sinks'], 'v_w_br_fox': out['v_w_br_fox'], 'v_w_br_swa': out['v_w_br_swa'], 'v_w_out': out['v_w_out'], 'v_ln_g': out['v_ln_g'], 'v_ln_b': out['v_ln_b']}


def _loss(weights, diff, rest, loss_target):
    with _jax.named_scope("forward"):
        args = {**rest, TWIN_DIFF_INPUT: diff, **{k: w.astype(_WEIGHT_DTYPES[k]) for k, w in weights.items()}}
        y = _forward(args)
    with _jax.named_scope("loss_head"):
        err = _jnp.square(y.astype(_jnp.float32) - loss_target)
        return 0.5 * _jnp.sum(_jnp.mean(err, axis=-1)) if err.ndim else 0.5 * err


def _adamw(w, g, m, v):
    m = ADAM_B1 * m + (1.0 - ADAM_B1) * g
    v = ADAM_B2 * v + (1.0 - ADAM_B2) * _jnp.square(g)
    m_hat = m / (1.0 - ADAM_B1 ** ADAM_STEP)
    v_hat = v / (1.0 - ADAM_B2 ** ADAM_STEP)
    delta = -ADAM_LR * (m_hat / (_jnp.sqrt(v_hat) + ADAM_EPS) + ADAM_WD * w)
    return delta, m, v


def reference(x, c, w_ada, b_ada, w_in, b_f, attn_sinks, w_br_fox, w_br_swa, w_out, ln_g, ln_b, loss_target, m_w_ada, m_b_ada, m_w_in, m_b_f, m_attn_sinks, m_w_br_fox, m_w_br_swa, m_w_out, m_ln_g, m_ln_b, v_w_ada, v_b_ada, v_w_in, v_b_f, v_attn_sinks, v_w_br_fox, v_w_br_swa, v_w_out, v_ln_g, v_ln_b):
    given = dict(x=x, c=c, w_ada=w_ada, b_ada=b_ada, w_in=w_in, b_f=b_f, attn_sinks=attn_sinks, w_br_fox=w_br_fox, w_br_swa=w_br_swa, w_out=w_out, ln_g=ln_g, ln_b=ln_b, loss_target=loss_target, m_w_ada=m_w_ada, m_b_ada=m_b_ada, m_w_in=m_w_in, m_b_f=m_b_f, m_attn_sinks=m_attn_sinks, m_w_br_fox=m_w_br_fox, m_w_br_swa=m_w_br_swa, m_w_out=m_w_out, m_ln_g=m_ln_g, m_ln_b=m_ln_b, v_w_ada=v_w_ada, v_b_ada=v_b_ada, v_w_in=v_w_in, v_b_f=v_b_f, v_attn_sinks=v_attn_sinks, v_w_br_fox=v_w_br_fox, v_w_br_swa=v_w_br_swa, v_w_out=v_w_out, v_ln_g=v_ln_g, v_ln_b=v_ln_b)
    weights = {n: given[n] for n in TWIN_WEIGHTS}
    shared = {n: given[n] for n in SHARED_INPUTS}
    per_example = {n: given[n] for n in ['x', 'c']}
    grad_fn = _jax.value_and_grad(_loss, argnums=(0, 1))

    def one_microbatch(ex, loss_target):
        ex = dict(ex)
        diff = ex.pop(TWIN_DIFF_INPUT)
        return grad_fn(weights, diff, {**shared, **ex}, loss_target)

    if N_MICROBATCH == 1:
        loss, (grad_w, grad_x) = one_microbatch(per_example, given["loss_target"])
    else:
        def body(carry, xs):
            loss_sum, grad_sum = carry
            l_k, (gw_k, gx_k) = one_microbatch(xs[0], xs[1])
            with _jax.named_scope("update"):
                return (loss_sum + l_k, _jax.tree.map(_jnp.add, grad_sum, gw_k)), gx_k

        init = (_jnp.zeros((), _jnp.float32), _jax.tree.map(_jnp.zeros_like, weights))
        (loss, grad_w), grad_x = _jax.lax.scan(body, init, (per_example, given["loss_target"]))
    with _jax.named_scope("update"):
        delta_w, new_m, new_v = {}, {}, {}
        for n in TWIN_WEIGHTS:
            delta_w[n], new_m[n], new_v[n] = _adamw(weights[n], grad_w[n], given["m_" + n], given["v_" + n])
    return (loss, grad_x, *[grad_w[n] for n in TWIN_WEIGHTS], *[delta_w[n] for n in TWIN_WEIGHTS],
            *[new_m[n] for n in TWIN_WEIGHTS], *[new_v[n] for n in TWIN_WEIGHTS])
```

```python
import numpy as np
import jax
import jax.numpy as jnp
from jax import lax
from jax.experimental import pallas as pl
from jax.experimental.pallas import tpu as pltpu

F32 = jnp.float32
BF16 = jnp.bfloat16
N_DEV = 8
MESH = pl.DeviceIdType.MESH

FOX_H, FOX_DH, FOX_W = 8, 128, 1024
SWA_HQ, SWA_HKV, SWA_DH, SWA_G = 16, 4, 64, 4
SWA_W, SWA_KVW, WINDOW = 1024, 256, 128
LN_EPS = 1e-5
NEG = -1e30
DEPTH = 1
ALPHA = (2.0 * DEPTH) ** 0.25
FOX_SCALE = FOX_DH ** -0.5
SWA_SCALE = SWA_DH ** -0.5
SLOPES = [2.0 ** (-8.0 * (h + 1.0) / SWA_HQ) for h in range(SWA_HQ)]

ADAM_LR, ADAM_B1, ADAM_B2, ADAM_EPS, ADAM_WD, ADAM_STEP = 0.001, 0.9, 0.999, 1e-08, 0.01, 10

N_FLOG = 8
FLOG_PAD = 512
OFF_FQ, OFF_FK, OFF_FV, OFF_FLOG = 0, 1024, 2048, 3072
OFF_SQ = OFF_FLOG + FLOG_PAD
OFF_SK = OFF_SQ + SWA_W
OFF_SV = OFF_SK + SWA_KVW
OFF_GF = OFF_SV + SWA_KVW
OFF_GS = OFF_GF + FOX_W
OFF_MF = OFF_GS + SWA_W
REAL_FLOG_END = OFF_FLOG + N_FLOG

ATT_BLK = 512
VMEM_LIMIT = 52 * 1024 * 1024


def _pcall(body, **kw):
    return pl.pallas_call(body, **kw)


def _cp(*sem):
    return pltpu.CompilerParams(dimension_semantics=sem, vmem_limit_bytes=VMEM_LIMIT)


def _sigmoid(x):
    return 1.0 / (1.0 + jnp.exp(-x))


def _all_gather(x, name, space):
    m_per, n = x.shape

    def body(x_ref, out_ref, send_sems, recv_sems, local_sem):
        mx, my, mc = lax.axis_index("x"), lax.axis_index("y"), lax.axis_index("c")
        me, sibling = (mx, my, mc), (mx, my, 1 - mc)
        chips = [(1 - mx, my), (mx, 1 - my), (1 - mx, 1 - my)]

        def rows(px, py, pc):
            return out_ref.at[pl.ds((4 * px + 2 * py + pc) * m_per, m_per), :]

        def copy(k, block, to, src=None):
            return pltpu.make_async_remote_copy(
                src_ref=rows(*block) if src is None else src,
                dst_ref=rows(*block),
                send_sem=send_sems.at[k],
                recv_sem=recv_sems.at[k],
                device_id=to,
                device_id_type=MESH,
            )

        mine = pltpu.make_async_copy(x_ref, rows(*me), local_sem)
        mine.start()
        first = [copy(0, me, sibling, src=x_ref)]
        first += [copy(1 + j, me, (*chip, mc), src=x_ref) for j, chip in enumerate(chips)]
        for cp in first:
            cp.start()
        passed = [copy(4 + j, (*chip, mc), sibling) for j, chip in enumerate(chips)]
        for j, chip in enumerate(chips):
            copy(1 + j, (*chip, mc), me).wait_recv()
            passed[j].start()
        copy(0, sibling, me).wait_recv()
        for j, chip in enumerate(chips):
            copy(4 + j, (*chip, 1 - mc), me).wait_recv()
        for cp in first + passed:
            cp.wait_send()
        mine.wait()

    return _pcall(
        body,
        name=name,
        out_shape=jax.ShapeDtypeStruct((N_DEV * m_per, n), x.dtype),
        in_specs=[pl.BlockSpec(memory_space=space)],
        out_specs=pl.BlockSpec(memory_space=space),
        scratch_shapes=[pltpu.SemaphoreType.DMA((7,)), pltpu.SemaphoreType.DMA((7,)), pltpu.SemaphoreType.DMA],
    )(x)


def _grad_exchange(blocks, name):
    n = len(blocks)

    def body(*refs):
        ins, outs = refs[:n], refs[n : 2 * n]
        send_sems, recv_sems, local_sems = refs[2 * n :]
        mx, my, mc = lax.axis_index("x"), lax.axis_index("y"), lax.axis_index("c")
        me = 4 * mx + 2 * my + mc
        copies, local = [], []
        for a in range(n):
            own = pltpu.make_async_copy(ins[a].at[me], outs[a].at[0], local_sems.at[a])
            own.start()
            local.append(own)
            for d in range(1, N_DEV):
                px = 1 - mx if (d >> 2) & 1 else mx
                py = 1 - my if (d >> 1) & 1 else my
                pc = 1 - mc if d & 1 else mc
                cp = pltpu.make_async_remote_copy(
                    src_ref=ins[a].at[4 * px + 2 * py + pc],
                    dst_ref=outs[a].at[d],
                    send_sem=send_sems.at[a * 7 + d - 1],
                    recv_sem=recv_sems.at[a * 7 + d - 1],
                    device_id=(px, py, pc),
                    device_id_type=MESH,
                )
                cp.start()
                copies.append(cp)
        for cp in copies:
            cp.wait_recv()
        for cp in copies:
            cp.wait_send()
        for own in local:
            own.wait()

    hbm = pl.BlockSpec(memory_space=pltpu.HBM)
    return _pcall(
        body,
        name=name,
        out_shape=[jax.ShapeDtypeStruct(b.shape, b.dtype) for b in blocks],
        in_specs=[hbm] * n,
        out_specs=[hbm] * n,
        scratch_shapes=[
            pltpu.SemaphoreType.DMA((7 * n,)),
            pltpu.SemaphoreType.DMA((7 * n,)),
            pltpu.SemaphoreType.DMA((n,)),
        ],
    )(*blocks)


def _gather_rows(v, name):
    n = v.shape[1]
    g = _all_gather(jnp.broadcast_to(v, (8, n)), name, pltpu.VMEM)
    return g.reshape(N_DEV, 8, n)[:, 0, :]


def _adamw(w, g, m, v):
    m = ADAM_B1 * m + (1.0 - ADAM_B1) * g
    v = ADAM_B2 * v + (1.0 - ADAM_B2) * (g * g)
    m_hat = m / (1.0 - ADAM_B1**ADAM_STEP)
    v_hat = v / (1.0 - ADAM_B2**ADAM_STEP)
    delta = -ADAM_LR * (m_hat / (jnp.sqrt(v_hat) + ADAM_EPS) + ADAM_WD * w)
    return delta, m, v


def _sum_adam(recv, w, m, v, name):
    _, r_tot, c = recv.shape
    c_pad = -(-c // 128) * 128
    tr = r_tot
    while 8 * tr * c_pad * 4 > 6 * 1024 * 1024 and tr % 16 == 0:
        tr //= 2

    def body(r_ref, w_ref, m_ref, v_ref, g_ref, d_ref, nm_ref, nv_ref):
        g = r_ref[0]
        for k in range(1, N_DEV):
            g = g + r_ref[k]
        d, nm, nv = _adamw(w_ref[...], g, m_ref[...], v_ref[...])
        g_ref[...] = g
        d_ref[...] = d
        nm_ref[...] = nm
        nv_ref[...] = nv

    blk = pl.BlockSpec((tr, c), lambda i: (i, 0))
    return _pcall(
        body,
        name=name,
        grid=(r_tot // tr,),
        out_shape=[jax.ShapeDtypeStruct((r_tot, c), F32)] * 4,
        in_specs=[pl.BlockSpec((N_DEV, tr, c), lambda i: (0, i, 0)), blk, blk, blk],
        out_specs=[blk] * 4,
        compiler_params=_cp("parallel"),
    )(recv, w, m, v)


def _wada_adam(c_t, dada_cols, w, m, v):
    d_model, c = w.shape
    tr = min(256, d_model)

    def body(ct_ref, da_ref, w_ref, m_ref, v_ref, g_ref, d_ref, nm_ref, nv_ref):
        g = jnp.dot(ct_ref[...].astype(BF16), da_ref[...].astype(BF16), preferred_element_type=F32)
        d, nm, nv = _adamw(w_ref[...], g, m_ref[...], v_ref[...])
        g_ref[...] = g
        d_ref[...] = d
        nm_ref[...] = nm
        nv_ref[...] = nv

    blk = pl.BlockSpec((tr, c), lambda i: (i, 0))
    return _pcall(
        body,
        name="wada_adam",
        grid=(d_model // tr,),
        out_shape=[jax.ShapeDtypeStruct((d_model, c), F32)] * 4,
        in_specs=[pl.BlockSpec((tr, N_DEV), lambda i: (i, 0)), pl.BlockSpec((N_DEV, c), lambda i: (0, 0)), blk, blk, blk],
        out_specs=[blk] * 4,
        compiler_params=_cp("parallel"),
    )(c_t, dada_cols, w, m, v)


def _small_adam(gathered, w, m, v):
    p = w.shape[1]

    def body(a_ref, w_ref, m_ref, v_ref, g_ref, d_ref, nm_ref, nv_ref):
        g = a_ref[0:1, :]
        for k in range(1, N_DEV):
            g = g + a_ref[k : k + 1, :]
        d, nm, nv = _adamw(w_ref[...], g, m_ref[...], v_ref[...])
        g_ref[...] = g
        d_ref[...] = d
        nm_ref[...] = nm
        nv_ref[...] = nv

    return _pcall(
        body,
        name="small_adam",
        out_shape=[jax.ShapeDtypeStruct((1, p), F32)] * 4,
    )(gathered, w, m, v)


def _ada_fwd(c_all, w_ada, b_cols):
    c = w_ada.shape[1]

    def body(c_ref, w_ref, b_ref, o_ref):
        o_ref[...] = jnp.dot(c_ref[...].astype(BF16), w_ref[...].astype(BF16), preferred_element_type=F32) + b_ref[...]

    return _pcall(
        body,
        name="ada_fwd",
        out_shape=jax.ShapeDtypeStruct((N_DEV, c), F32),
        compiler_params=_cp(),
    )(c_all, w_ada, b_cols)


def _ln_mod(x, ada):
    s_len, d = x.shape
    tm = min(512, s_len)

    def body(x_ref, sh_ref, sc_ref, h_ref):
        xv = x_ref[...]
        mu = jnp.mean(xv, axis=-1, keepdims=True)
        xc = xv - mu
        var = jnp.mean(xc * xc, axis=-1, keepdims=True)
        xhat = xc * lax.rsqrt(var + LN_EPS)
        h_ref[...] = (xhat * (1.0 + sc_ref[...]) + sh_ref[...]).astype(BF16)

    return _pcall(
        body,
        name="ln_mod",
        grid=(s_len // tm,),
        out_shape=jax.ShapeDtypeStruct((s_len, d), BF16),
        in_specs=[
            pl.BlockSpec((tm, d), lambda i: (i, 0)),
            pl.BlockSpec((1, d), lambda i: (0, 0)),
            pl.BlockSpec((1, d), lambda i: (0, 1)),
        ],
        out_specs=pl.BlockSpec((tm, d), lambda i: (i, 0)),
        compiler_params=_cp("parallel"),
    )(x, ada, ada)


def _mm_cols(a, b, col_off, n_cols, out_dtype, name):
    m, k = a.shape
    tm, tn = min(1024, m), 512
    off = col_off // tn

    def body(a_ref, b_ref, o_ref):
        o_ref[...] = jnp.dot(a_ref[...], b_ref[...], preferred_element_type=F32).astype(out_dtype)

    return _pcall(
        body,
        name=name,
        grid=(m // tm, n_cols // tn),
        out_shape=jax.ShapeDtypeStruct((m, n_cols), out_dtype),
        in_specs=[pl.BlockSpec((tm, k), lambda i, j: (i, 0)), pl.BlockSpec((k, tn), lambda i, j: (0, off + j))],
        out_specs=pl.BlockSpec((tm, tn), lambda i, j: (i, j)),
        compiler_params=_cp("parallel", "parallel"),
    )(a, b)


def _mm_acc(a, b, name):
    m, k = a.shape
    n = b.shape[1]
    tm, tn, tk = min(1024, m), min(1024, n), min(512, k)

    def body(a_ref, b_ref, o_ref):
        kk = pl.program_id(2)
        part = jnp.dot(a_ref[...], b_ref[...], preferred_element_type=F32)

        @pl.when(kk == 0)
        def _():
            o_ref[...] = part

        @pl.when(kk > 0)
        def _():
            o_ref[...] += part

    return _pcall(
        body,
        name=name,
        grid=(m // tm, n // tn, k // tk),
        out_shape=jax.ShapeDtypeStruct((m, n), F32),
        in_specs=[pl.BlockSpec((tm, tk), lambda i, j, kk: (i, kk)), pl.BlockSpec((tk, tn), lambda i, j, kk: (kk, j))],
        out_specs=pl.BlockSpec((tm, tn), lambda i, j, kk: (i, j)),
        compiler_params=_cp("parallel", "parallel", "arbitrary"),
    )(a, b)


def _split3(a):
    hi = a.astype(BF16)
    r1 = a - hi.astype(F32)
    mid = r1.astype(BF16)
    lo = (r1 - mid.astype(F32)).astype(BF16)
    return hi, mid, lo


def _dot_ones(a, tri):
    return sum(jnp.dot(t, tri, preferred_element_type=F32) for t in _split3(a))


def _log_sigmoid(x):
    return jnp.minimum(x, 0.0) - jnp.log1p(jnp.exp(-jnp.abs(x)))


def _fox_cum(flog_t, bf_col):
    s_len = flog_t.shape[1]

    def body(fl_ref, bf_ref, cum_ref):
        r = lax.broadcasted_iota(jnp.int32, (128, 128), 0)
        c = lax.broadcasted_iota(jnp.int32, (128, 128), 1)
        upper = (r <= c).astype(BF16)

        def step(t, carry):
            sl = pl.ds(pl.multiple_of(t * 128, 128), 128)
            lf = _log_sigmoid(fl_ref[:, sl] + bf_ref[...])
            cs = _dot_ones(lf, upper) + carry
            cum_ref[:, sl] = cs
            return cs[:, 127:128]

        lax.fori_loop(0, s_len // 128, step, jnp.zeros((FOX_H, 1), F32))

    return _pcall(body, name="fox_cum", out_shape=jax.ShapeDtypeStruct((FOX_H, s_len), F32))(flog_t, bf_col)


def _fox_gate_bwd(drow, dcol, flog_t, bf_col):
    s_len = flog_t.shape[1]
    n = s_len // 128

    def body(dr_ref, dc_ref, fl_ref, bf_ref, dfl_ref, dbf_ref):
        r = lax.broadcasted_iota(jnp.int32, (128, 128), 0)
        c = lax.broadcasted_iota(jnp.int32, (128, 128), 1)
        lower = (r >= c).astype(BF16)

        def step(t, carry):
            run, tot = carry
            sl = pl.ds(pl.multiple_of((n - 1 - t) * 128, 128), 128)
            rc = _dot_ones(dr_ref[:, sl] - dc_ref[:, sl], lower) + run
            dfl = rc * _sigmoid(-(fl_ref[:, sl] + bf_ref[...]))
            dfl_ref[:, sl] = dfl
            return rc[:, 0:1], tot + jnp.sum(dfl, axis=1, keepdims=True)

        zero = jnp.zeros((FOX_H, 1), F32)
        _, tot = lax.fori_loop(0, n, step, (zero, zero))
        dbf_ref[...] = jnp.broadcast_to(tot, (FOX_H, 128))

    return _pcall(
        body,
        name="fox_gate_bwd",
        out_shape=[jax.ShapeDtypeStruct((FOX_H, s_len), F32), jax.ShapeDtypeStruct((FOX_H, 128), F32)],
    )(drow, dcol, flog_t, bf_col)


def _tri_tables(n, kv_major):
    if kv_major:
        pairs = [(i, j) for j in range(n) for i in range(j, n)]
    else:
        pairs = [(i, j) for i in range(n) for j in range(i + 1)]
    qi = np.array([p[0] for p in pairs], np.int32)
    kj = np.array([p[1] for p in pairs], np.int32)
    return jnp.asarray(qi), jnp.asarray(kj)


def _diag_mask(blk):
    r = lax.broadcasted_iota(jnp.int32, (blk, blk), 0)
    c = lax.broadcasted_iota(jnp.int32, (blk, blk), 1)
    return r >= c


_NT = (((1,), (1,)), ((), ()))
_TN = (((0,), (0,)), ((), ()))


def _fox_fwd(qkv, cum_row):
    s_len = qkv.shape[0]
    blk = min(ATT_BLK, s_len)
    nb = s_len // blk
    qi, kj = _tri_tables(nb, kv_major=False)

    def body(qi_ref, kj_ref, q_ref, k_ref, v_ref, c_ref, o_ref, lse_ref, m_s, l_s, acc_s):
        t = pl.program_id(1)
        i, j = qi_ref[t], kj_ref[t]

        @pl.when(j == 0)
        def _():
            m_s[...] = jnp.full(m_s.shape, NEG, F32)
            l_s[...] = jnp.zeros(l_s.shape, F32)
            acc_s[...] = jnp.zeros(acc_s.shape, F32)

        def update(masked):
            s = lax.dot_general(q_ref[...], k_ref[...], _NT, preferred_element_type=F32) * FOX_SCALE - c_ref[...]
            if masked:
                s = jnp.where(_diag_mask(blk), s, NEG)
            m_prev = m_s[...]
            m_new = jnp.maximum(m_prev, jnp.max(s, axis=1, keepdims=True))
            a = jnp.exp(m_prev - m_new)
            p = jnp.exp(s - m_new)
            l_s[...] = a * l_s[...] + jnp.sum(p, axis=1, keepdims=True)
            acc_s[...] = a * acc_s[...] + jnp.dot(p.astype(BF16), v_ref[...], preferred_element_type=F32)
            m_s[...] = m_new

        @pl.when(j < i)
        def _():
            update(False)

        @pl.when(j == i)
        def _():
            update(True)
            o_ref[...] = acc_s[...] / l_s[...]
            lse_ref[...] = m_s[...] + jnp.log(l_s[...])

    grid_spec = pltpu.PrefetchScalarGridSpec(
        num_scalar_prefetch=2,
        grid=(FOX_H, qi.shape[0]),
        in_specs=[
            pl.BlockSpec((blk, FOX_DH), lambda h, t, qi, kj: (qi[t], h)),
            pl.BlockSpec((blk, FOX_DH), lambda h, t, qi, kj: (kj[t], FOX_H + h)),
            pl.BlockSpec((blk, FOX_DH), lambda h, t, qi, kj: (kj[t], 2 * FOX_H + h)),
            pl.BlockSpec((None, 1, blk), lambda h, t, qi, kj: (h, 0, kj[t])),
        ],
        out_specs=[
            pl.BlockSpec((blk, FOX_DH), lambda h, t, qi, kj: (qi[t], h)),
            pl.BlockSpec((None, blk, 1), lambda h, t, qi, kj: (h, qi[t], 0)),
        ],
        scratch_shapes=[pltpu.VMEM((blk, 1), F32), pltpu.VMEM((blk, 1), F32), pltpu.VMEM((blk, FOX_DH), F32)],
    )
    return _pcall(
        body,
        name="fox_fwd",
        grid_spec=grid_spec,
        out_shape=[jax.ShapeDtypeStruct((s_len, FOX_W), F32), jax.ShapeDtypeStruct((FOX_H, s_len, 1), F32)],
        compiler_params=_cp("parallel", "arbitrary"),
    )(qi, kj, qkv, qkv, qkv, cum_row)


def _fox_bwd(qkv, cum_row, o, lse, do):
    s_len = qkv.shape[0]
    blk = min(ATT_BLK, s_len)
    nb = s_len // blk
    qi, kj = _tri_tables(nb, kv_major=True)

    def body(qi_ref, kj_ref, q_ref, k_ref, v_ref, c_ref, o_ref, lse_ref, do_ref, dq_ref, dk_ref, dv_ref, dc_ref, dr_ref, dk_s, dv_s, dc_s, dq_s, dr_s):
        t = pl.program_id(1)
        i, j = qi_ref[t], kj_ref[t]
        rows = pl.ds(pl.multiple_of(i * blk, blk), blk)

        def step(diag):
            q, k, v = q_ref[...], k_ref[...], v_ref[...]
            do32 = do_ref[...]
            dob = do32.astype(BF16)
            delta = jnp.sum(do32 * o_ref[...], axis=1, keepdims=True)
            s = lax.dot_general(q, k, _NT, preferred_element_type=F32) * FOX_SCALE - c_ref[...]
            p = jnp.exp(s - lse_ref[...])
            if diag:
                p = jnp.where(_diag_mask(blk), p, 0.0)
            dp = lax.dot_general(dob, v, _NT, preferred_element_type=F32)
            ds = p * (dp - delta)
            dsb = ds.astype(BF16)
            dv_c = lax.dot_general(p.astype(BF16), dob, _TN, preferred_element_type=F32)
            dk_c = lax.dot_general(dsb, q, _TN, preferred_element_type=F32)
            dq_c = jnp.dot(dsb, k, preferred_element_type=F32)
            dc_c = jnp.sum(ds, axis=0, keepdims=True)
            if diag:
                dk_s[...] = dk_c
                dv_s[...] = dv_c
                dc_s[...] = dc_c
            else:
                dk_s[...] += dk_c
                dv_s[...] += dv_c
                dc_s[...] += dc_c
            dq_s[...] = dq_c
            dr_s[...] = jnp.sum(ds, axis=1, keepdims=True)

        @pl.when(i == j)
        def _():
            step(True)

        @pl.when(i > j)
        def _():
            step(False)

        @pl.when((j == 0) & (i == 0))
        def _():
            dq_ref[rows, :] = dq_s[...] * FOX_SCALE

        @pl.when((j == 0) & (i > 0))
        def _():
            dq_ref[rows, :] = dq_s[...]

        @pl.when((j > 0) & (i == j))
        def _():
            dq_ref[rows, :] = (dq_ref[rows, :] + dq_s[...]) * FOX_SCALE

        @pl.when((j > 0) & (i > j))
        def _():
            dq_ref[rows, :] += dq_s[...]

        @pl.when(j == 0)
        def _():
            dr_ref[rows, :] = dr_s[...]

        @pl.when(j > 0)
        def _():
            dr_ref[rows, :] += dr_s[...]

        @pl.when(i == nb - 1)
        def _():
            dk_ref[...] = (dk_s[...] * FOX_SCALE).astype(BF16)
            dv_ref[...] = dv_s[...].astype(BF16)
            dc_ref[...] = dc_s[...]

    grid_spec = pltpu.PrefetchScalarGridSpec(
        num_scalar_prefetch=2,
        grid=(FOX_H, qi.shape[0]),
        in_specs=[
            pl.BlockSpec((blk, FOX_DH), lambda h, t, qi, kj: (qi[t], h)),
            pl.BlockSpec((blk, FOX_DH), lambda h, t, qi, kj: (kj[t], FOX_H + h)),
            pl.BlockSpec((blk, FOX_DH), lambda h, t, qi, kj: (kj[t], 2 * FOX_H + h)),
            pl.BlockSpec((None, 1, blk), lambda h, t, qi, kj: (h, 0, kj[t])),
            pl.BlockSpec((blk, FOX_DH), lambda h, t, qi, kj: (qi[t], h)),
            pl.BlockSpec((None, blk, 1), lambda h, t, qi, kj: (h, qi[t], 0)),
            pl.BlockSpec((blk, FOX_DH), lambda h, t, qi, kj: (qi[t], h)),
        ],
        out_specs=[
            pl.BlockSpec((s_len, FOX_DH), lambda h, t, qi, kj: (0, h)),
            pl.BlockSpec((blk, FOX_DH), lambda h, t, qi, kj: (kj[t], h)),
            pl.BlockSpec((blk, FOX_DH), lambda h, t, qi, kj: (kj[t], h)),
            pl.BlockSpec((None, 1, blk), lambda h, t, qi, kj: (h, 0, kj[t])),
            pl.BlockSpec((None, s_len, 1), lambda h, t, qi, kj: (h, 0, 0)),
        ],
        scratch_shapes=[
            pltpu.VMEM((blk, FOX_DH), F32),
            pltpu.VMEM((blk, FOX_DH), F32),
            pltpu.VMEM((1, blk), F32),
            pltpu.VMEM((blk, FOX_DH), F32),
            pltpu.VMEM((blk, 1), F32),
        ],
    )
    return _pcall(
        body,
        name="fox_bwd",
        grid_spec=grid_spec,
        out_shape=[
            jax.ShapeDtypeStruct((s_len, FOX_W), F32),
            jax.ShapeDtypeStruct((s_len, FOX_W), BF16),
            jax.ShapeDtypeStruct((s_len, FOX_W), BF16),
            jax.ShapeDtypeStruct((FOX_H, 1, s_len), F32),
            jax.ShapeDtypeStruct((FOX_H, s_len, 1), F32),
        ],
        compiler_params=_cp("parallel", "arbitrary"),
    )(qi, kj, qkv, qkv, qkv, cum_row, o, lse, do)


def _swa_probs(i, q_ref, kk, sinks_ref, h):
    r = lax.broadcasted_iota(jnp.int32, (WINDOW, 2 * WINDOW), 0)
    c = lax.broadcasted_iota(jnp.int32, (WINDOW, 2 * WINDOW), 1)
    dist = r - c + WINDOW
    valid = (dist >= 0) & (dist < WINDOW) & ((c >= WINDOW) | (i > 0))
    g = h // SWA_G
    q = q_ref[:, h * SWA_DH : (h + 1) * SWA_DH]
    k = kk[:, g * SWA_DH : (g + 1) * SWA_DH]
    s = lax.dot_general(q, k, _NT, preferred_element_type=F32) * SWA_SCALE - SLOPES[h] * dist.astype(F32)
    s = jnp.where(valid, s, NEG)
    sink = sinks_ref[h]
    m = jnp.maximum(jnp.max(s, axis=1, keepdims=True), sink)
    e = jnp.exp(s - m)
    e_sink = jnp.exp(sink - m)
    inv = 1.0 / (jnp.sum(e, axis=1, keepdims=True) + e_sink)
    return q, k, e * inv, e_sink * inv


def _swa_specs(col_q, col_k, col_v, rev, nb):
    def blk(t):
        return nb - 1 - t if rev else t

    return [
        pl.BlockSpec((WINDOW, SWA_W), lambda t: (blk(t), col_q)),
        pl.BlockSpec((WINDOW, SWA_KVW), lambda t: (jnp.maximum(blk(t) - 1, 0), col_k)),
        pl.BlockSpec((WINDOW, SWA_KVW), lambda t: (blk(t), col_k)),
        pl.BlockSpec((WINDOW, SWA_KVW), lambda t: (jnp.maximum(blk(t) - 1, 0), col_v)),
        pl.BlockSpec((WINDOW, SWA_KVW), lambda t: (blk(t), col_v)),
    ]


def _swa_fwd(qkv, sinks):
    s_len = qkv.shape[0]
    nb = s_len // WINDOW

    def body(q_ref, kp_ref, kc_ref, vp_ref, vc_ref, sinks_ref, o_ref):
        i = pl.program_id(0)
        kk = jnp.concatenate([kp_ref[...], kc_ref[...]], axis=0)
        vv = jnp.concatenate([vp_ref[...], vc_ref[...]], axis=0)
        for h in range(SWA_HQ):
            g = h // SWA_G
            _, _, p, _ = _swa_probs(i, q_ref, kk, sinks_ref, h)
            v = vv[:, g * SWA_DH : (g + 1) * SWA_DH]
            o_ref[:, h * SWA_DH : (h + 1) * SWA_DH] = jnp.dot(p.astype(BF16), v, preferred_element_type=F32)

    return _pcall(
        body,
        name="swa_fwd",
        grid=(nb,),
        out_shape=jax.ShapeDtypeStruct((s_len, SWA_W), F32),
        in_specs=_swa_specs(0, 4, 5, False, nb) + [pl.BlockSpec(memory_space=pltpu.SMEM)],
        out_specs=pl.BlockSpec((WINDOW, SWA_W), lambda t: (t, 0)),
        compiler_params=_cp("parallel"),
    )(qkv, qkv, qkv, qkv, qkv, sinks)


def _swa_bwd(qkv, sinks, do):
    s_len = qkv.shape[0]
    nb = s_len // WINDOW

    def body(q_ref, kp_ref, kc_ref, vp_ref, vc_ref, sinks_ref, do_ref, dq_ref, dk_ref, dv_ref, dsink_ref, ck_s, cv_s, dkk_s, dvv_s):
        t = pl.program_id(0)
        i = nb - 1 - t

        @pl.when(t == 0)
        def _():
            ck_s[...] = jnp.zeros(ck_s.shape, F32)
            cv_s[...] = jnp.zeros(cv_s.shape, F32)
            dsink_ref[...] = jnp.zeros(dsink_ref.shape, F32)

        kk = jnp.concatenate([kp_ref[...], kc_ref[...]], axis=0)
        vv = jnp.concatenate([vp_ref[...], vc_ref[...]], axis=0)
        lane = lax.broadcasted_iota(jnp.int32, (1, 128), 1)
        dsink = jnp.zeros((1, 128), F32)
        for g in range(SWA_HKV):
            cols = slice(g * SWA_DH, (g + 1) * SWA_DH)
            v = vv[:, cols]
            dkk = jnp.zeros((2 * WINDOW, SWA_DH), F32)
            dvv = jnp.zeros((2 * WINDOW, SWA_DH), F32)
            for h in range(g * SWA_G, (g + 1) * SWA_G):
                hc = slice(h * SWA_DH, (h + 1) * SWA_DH)
                q, k, p, p_sink = _swa_probs(i, q_ref, kk, sinks_ref, h)
                dob = do_ref[:, hc].astype(BF16)
                dp = lax.dot_general(dob, v, _NT, preferred_element_type=F32)
                delta = jnp.sum(p * dp, axis=1, keepdims=True)
                dsb = (p * (dp - delta)).astype(BF16)
                dq_ref[:, hc] = (jnp.dot(dsb, k, preferred_element_type=F32) * SWA_SCALE).astype(BF16)
                dkk = dkk + lax.dot_general(dsb, q, _TN, preferred_element_type=F32)
                dvv = dvv + lax.dot_general(p.astype(BF16), dob, _TN, preferred_element_type=F32)
                dsink = dsink + jnp.where(lane == h, -jnp.sum(p_sink * delta, axis=0, keepdims=True), 0.0)
            dkk_s[:, cols] = dkk * SWA_SCALE
            dvv_s[:, cols] = dvv
        dk_ref[...] = (dkk_s[WINDOW:, :] + ck_s[...]).astype(BF16)
        dv_ref[...] = (dvv_s[WINDOW:, :] + cv_s[...]).astype(BF16)
        ck_s[...] = dkk_s[:WINDOW, :]
        cv_s[...] = dvv_s[:WINDOW, :]
        dsink_ref[...] += dsink

    row = lambda t: (nb - 1 - t, 0)
    return _pcall(
        body,
        name="swa_bwd",
        grid=(nb,),
        out_shape=[
            jax.ShapeDtypeStruct((s_len, SWA_W), BF16),
            jax.ShapeDtypeStruct((s_len, SWA_KVW), BF16),
            jax.ShapeDtypeStruct((s_len, SWA_KVW), BF16),
            jax.ShapeDtypeStruct((1, 128), F32),
        ],
        in_specs=_swa_specs(0, 4, 5, True, nb)
        + [pl.BlockSpec(memory_space=pltpu.SMEM), pl.BlockSpec((WINDOW, SWA_W), row)],
        out_specs=[
            pl.BlockSpec((WINDOW, SWA_W), row),
            pl.BlockSpec((WINDOW, SWA_KVW), row),
            pl.BlockSpec((WINDOW, SWA_KVW), row),
            pl.BlockSpec((1, 128), lambda t: (0, 0)),
        ],
        scratch_shapes=[
            pltpu.VMEM((WINDOW, SWA_KVW), F32),
            pltpu.VMEM((WINDOW, SWA_KVW), F32),
            pltpu.VMEM((2 * WINDOW, SWA_KVW), F32),
            pltpu.VMEM((2 * WINDOW, SWA_KVW), F32),
        ],
        compiler_params=_cp("arbitrary"),
    )(qkv, qkv, qkv, qkv, qkv, sinks, do)


def _branch_fwd(o, gates, g_blk, w_b, name):
    s_len, wd = o.shape
    d = w_b.shape[1]
    tm = min(512, s_len)

    def body(o_ref, g_ref, w_ref, y_ref, a_ref):
        g = g_ref[...]
        a = (o_ref[...] * (g * _sigmoid(g))).astype(BF16)
        a_ref[...] = a
        y_ref[...] = jnp.dot(a, w_ref[...], preferred_element_type=F32)

    return _pcall(
        body,
        name=name,
        grid=(s_len // tm,),
        out_shape=[jax.ShapeDtypeStruct((s_len, d), F32), jax.ShapeDtypeStruct((s_len, wd), BF16)],
        in_specs=[
            pl.BlockSpec((tm, wd), lambda i: (i, 0)),
            pl.BlockSpec((tm, wd), lambda i: (i, g_blk)),
            pl.BlockSpec((wd, d), lambda i: (0, 0)),
        ],
        out_specs=[pl.BlockSpec((tm, d), lambda i: (i, 0)), pl.BlockSpec((tm, wd), lambda i: (i, 0))],
        compiler_params=_cp("parallel"),
    )(o, gates, w_b)


def _out_stage(gates, mf_blk, y_fox, y_swa, w_out, x, ada, ln_g, ln_b, target):
    s_len, d = x.shape
    tm = min(128, s_len)
    n_steps = s_len // tm

    def body(mf_ref, ms_ref, yf_ref, ys_ref, w_ref, x_ref, gate_ref, lg_ref, lb_ref, t_ref, mg_ref, dza_ref, dsub_ref, red_ref):
        i = pl.program_id(0)
        merged = _sigmoid(mf_ref[...]) * yf_ref[...] + _sigmoid(ms_ref[...]) * ys_ref[...]
        mb = merged.astype(BF16)
        mg_ref[...] = mb
        sub = jnp.dot(mb, w_ref[...], preferred_element_type=F32)
        gate = gate_ref[...]
        z = ALPHA * x_ref[...] + gate * sub
        mu = jnp.mean(z, axis=-1, keepdims=True)
        zc = z - mu
        var = jnp.mean(zc * zc, axis=-1, keepdims=True)
        rstd = lax.rsqrt(var + LN_EPS)
        zhat = zc * rstd
        err = zhat * lg_ref[...] + lb_ref[...] - t_ref[...]
        dout = err * (1.0 / d)
        dzhat = dout * lg_ref[...]
        dz = rstd * (dzhat - jnp.mean(dzhat, axis=-1, keepdims=True) - zhat * jnp.mean(dzhat * zhat, axis=-1, keepdims=True))
        dza_ref[...] = ALPHA * dz
        dsub_ref[...] = (gate * dz).astype(BF16)
        part = jnp.concatenate(
            [
                jnp.sum(dz * sub, axis=0, keepdims=True),
                jnp.sum(dout * zhat, axis=0, keepdims=True),
                jnp.sum(dout, axis=0, keepdims=True),
                jnp.sum(err * err, axis=0, keepdims=True),
                jnp.zeros((4, d), F32),
            ],
            axis=0,
        )

        @pl.when(i == 0)
        def _():
            red_ref[...] = part

        @pl.when(i > 0)
        def _():
            red_ref[...] += part

        @pl.when(i == n_steps - 1)
        def _():
            red_ref[4:5, :] = jnp.broadcast_to(jnp.sum(red_ref[3:4, :], axis=1, keepdims=True), (1, d))

    row = pl.BlockSpec((tm, d), lambda i: (i, 0))
    vec = pl.BlockSpec((1, d), lambda i: (0, 0))
    return _pcall(
        body,
        name="out_stage",
        grid=(n_steps,),
        out_shape=[
            jax.ShapeDtypeStruct((s_len, d), BF16),
            jax.ShapeDtypeStruct((s_len, d), F32),
            jax.ShapeDtypeStruct((s_len, d), BF16),
            jax.ShapeDtypeStruct((8, d), F32),
        ],
        in_specs=[
            pl.BlockSpec((tm, d), lambda i: (i, mf_blk)),
            pl.BlockSpec((tm, d), lambda i: (i, mf_blk + 1)),
            row,
            row,
            pl.BlockSpec((d, d), lambda i: (0, 0)),
            row,
            pl.BlockSpec((1, d), lambda i: (0, 2)),
            vec,
            vec,
            row,
        ],
        out_specs=[row, row, row, pl.BlockSpec((8, d), lambda i: (0, 0))],
        compiler_params=_cp("arbitrary"),
    )(gates, gates, y_fox, y_swa, w_out, x, ada, ln_g, ln_b, target)


def _merge_bwd(dsub, w_out_t, gates, mf_blk, y_fox, y_swa):
    s_len, d = dsub.shape
    tm = min(128, s_len)

    def body(ds_ref, w_ref, mf_ref, ms_ref, yf_ref, ys_ref, dmf_ref, dms_ref, dyf_ref, dys_ref):
        dm = jnp.dot(ds_ref[...], w_ref[...], preferred_element_type=F32)
        sf, ss = _sigmoid(mf_ref[...]), _sigmoid(ms_ref[...])
        dmf_ref[...] = (dm * yf_ref[...] * (sf * (1.0 - sf))).astype(BF16)
        dms_ref[...] = (dm * ys_ref[...] * (ss * (1.0 - ss))).astype(BF16)
        dyf_ref[...] = (dm * sf).astype(BF16)
        dys_ref[...] = (dm * ss).astype(BF16)

    row = pl.BlockSpec((tm, d), lambda i: (i, 0))
    return _pcall(
        body,
        name="merge_bwd",
        grid=(s_len // tm,),
        out_shape=[jax.ShapeDtypeStruct((s_len, d), BF16)] * 4,
        in_specs=[
            row,
            pl.BlockSpec((d, d), lambda i: (0, 0)),
            pl.BlockSpec((tm, d), lambda i: (i, mf_blk)),
            pl.BlockSpec((tm, d), lambda i: (i, mf_blk + 1)),
            row,
            row,
        ],
        out_specs=[row] * 4,
        compiler_params=_cp("parallel"),
    )(dsub, w_out_t, gates, gates, y_fox, y_swa)


def _branch_bwd(dy, w_b_t, o, gates, g_blk, name):
    s_len, d = dy.shape
    wd = w_b_t.shape[1]
    tm = min(512, s_len)

    def body(dy_ref, w_ref, o_ref, g_ref, do_ref, dg_ref):
        da = jnp.dot(dy_ref[...], w_ref[...], preferred_element_type=F32)
        g = g_ref[...]
        sg = _sigmoid(g)
        do_ref[...] = da * (g * sg)
        dg_ref[...] = (da * o_ref[...] * (sg * (1.0 + g * (1.0 - sg)))).astype(BF16)

    return _pcall(
        body,
        name=name,
        grid=(s_len // tm,),
        out_shape=[jax.ShapeDtypeStruct((s_len, wd), F32), jax.ShapeDtypeStruct((s_len, wd), BF16)],
        in_specs=[
            pl.BlockSpec((tm, d), lambda i: (i, 0)),
            pl.BlockSpec((d, wd), lambda i: (0, 0)),
            pl.BlockSpec((tm, wd), lambda i: (i, 0)),
            pl.BlockSpec((tm, wd), lambda i: (i, g_blk)),
        ],
        out_specs=[pl.BlockSpec((tm, wd), lambda i: (i, 0))] * 2,
        compiler_params=_cp("parallel"),
    )(dy, w_b_t, o, gates)


def _in_bwd(dproj, w_in_t, x, ada, dza):
    s_len, d = x.shape
    k_tot = dproj.shape[1]
    tm, tk = min(256, s_len), 1024
    nk = k_tot // tk

    def body(dp_ref, w_ref, x_ref, sc_ref, dza_ref, gx_ref, red_ref, acc_s):
        i, kk = pl.program_id(0), pl.program_id(1)
        part = jnp.dot(dp_ref[...], w_ref[...], preferred_element_type=F32)

        @pl.when(kk == 0)
        def _():
            acc_s[...] = part

        @pl.when(kk > 0)
        def _():
            acc_s[...] += part

        @pl.when(kk == nk - 1)
        def _():
            dh = acc_s[...]
            xv = x_ref[...]
            mu = jnp.mean(xv, axis=-1, keepdims=True)
            xc = xv - mu
            var = jnp.mean(xc * xc, axis=-1, keepdims=True)
            rstd = lax.rsqrt(var + LN_EPS)
            xhat = xc * rstd
            dxhat = dh * (1.0 + sc_ref[...])
            dx = rstd * (dxhat - jnp.mean(dxhat, axis=-1, keepdims=True) - xhat * jnp.mean(dxhat * xhat, axis=-1, keepdims=True))
            gx_ref[...] = dza_ref[...] + dx
            part_r = jnp.concatenate(
                [jnp.sum(dh, axis=0, keepdims=True), jnp.sum(dh * xhat, axis=0, keepdims=True), jnp.zeros((6, d), F32)], axis=0
            )

            @pl.when(i == 0)
            def _():
                red_ref[...] = part_r

            @pl.when(i > 0)
            def _():
                red_ref[...] += part_r

    row = pl.BlockSpec((tm, d), lambda i, kk: (i, 0))
    return _pcall(
        body,
        name="in_bwd",
        grid=(s_len // tm, nk),
        out_shape=[jax.ShapeDtypeStruct((s_len, d), F32), jax.ShapeDtypeStruct((8, d), F32)],
        in_specs=[
            pl.BlockSpec((tm, tk), lambda i, kk: (i, kk)),
            pl.BlockSpec((tk, d), lambda i, kk: (kk, 0)),
            row,
            pl.BlockSpec((1, d), lambda i, kk: (0, 1)),
            row,
        ],
        out_specs=[row, pl.BlockSpec((8, d), lambda i, kk: (0, 0))],
        scratch_shapes=[pltpu.VMEM((tm, d), F32)],
        compiler_params=_cp("arbitrary", "arbitrary"),
    )(dproj, w_in_t, x, ada, dza)


def _pad_lanes(v, n):
    return jnp.pad(v, ((0, 0), (0, n - v.shape[1])))


def kernel(x, c, w_ada, b_ada, w_in, b_f, attn_sinks, w_br_fox, w_br_swa, w_out, ln_g, ln_b, loss_target, m_w_ada, m_b_ada, m_w_in, m_b_f, m_attn_sinks, m_w_br_fox, m_w_br_swa, m_w_out, m_ln_g, m_ln_b, v_w_ada, v_b_ada, v_w_in, v_b_f, v_attn_sinks, v_w_br_fox, v_w_br_swa, v_w_out, v_ln_g, v_ln_b):
    x2, tgt = x[0], loss_target[0]
    s_len, d = x2.shape
    me = 4 * lax.axis_index("x") + 2 * lax.axis_index("y") + lax.axis_index("c")
    off_ms = OFF_MF + d
    in_pad = off_ms + d
    c_ada = w_ada.shape[2]
    c_in = w_in.shape[2]
    c_br = w_br_fox.shape[2]

    w_in_g = _all_gather(w_in[0].astype(BF16), "ag_w_in", pltpu.HBM)
    w_in_full = w_in_g.reshape(N_DEV, d, c_in).transpose(1, 0, 2).reshape(d, N_DEV * c_in)
    w_in_pad = jnp.concatenate(
        [w_in_full[:, :REAL_FLOG_END], jnp.zeros((d, FLOG_PAD - N_FLOG), BF16), w_in_full[:, REAL_FLOG_END:]], axis=1
    )
    w_in_t = w_in_pad.T
    w_bf = _all_gather(w_br_fox[0].astype(BF16), "ag_w_br_fox", pltpu.HBM)
    w_bf = w_bf.reshape(N_DEV, FOX_W, c_br).transpose(1, 0, 2).reshape(FOX_W, d)
    w_bs = _all_gather(w_br_swa[0].astype(BF16), "ag_w_br_swa", pltpu.HBM)
    w_bs = w_bs.reshape(N_DEV, SWA_W, c_br).transpose(1, 0, 2).reshape(SWA_W, d)
    w_o = _all_gather(w_out[0].astype(BF16), "ag_w_out", pltpu.HBM)

    c_all = _gather_rows(c, "ag_c")
    b_cols = lax.dynamic_slice(b_ada, (0, me * c_ada), (1, c_ada))
    ada_cols = _ada_fwd(c_all, w_ada[0], b_cols)
    ada_g = _all_gather(ada_cols, "ag_ada", pltpu.VMEM).reshape(N_DEV, N_DEV, c_ada)
    ada = lax.dynamic_index_in_dim(ada_g, me, axis=1, keepdims=False).reshape(1, N_DEV * c_ada)

    h = _ln_mod(x2, ada)
    qkv_fox = _mm_cols(h, w_in_pad, OFF_FQ, 3 * FOX_W, BF16, "proj_fox")
    flog = _mm_cols(h, w_in_pad, OFF_FLOG, FLOG_PAD, F32, "proj_flog")
    qkv_swa = _mm_cols(h, w_in_pad, OFF_SQ, SWA_W + 2 * SWA_KVW, BF16, "proj_swa")
    gates = _mm_cols(h, w_in_pad, OFF_GF, in_pad - OFF_GF, F32, "proj_gates")
    mf_blk = (OFF_MF - OFF_GF) // d

    flog_t = flog[:, :N_FLOG].T
    bf_col = b_f.reshape(FOX_H, 1)
    cum_row = _fox_cum(flog_t, bf_col).reshape(FOX_H, 1, s_len)
    o_fox, lse = _fox_fwd(qkv_fox, cum_row)
    sinks = attn_sinks.reshape(SWA_HQ)
    o_swa = _swa_fwd(qkv_swa, sinks)

    y_fox, a_fox = _branch_fwd(o_fox, gates, 0, w_bf, "branch_fox")
    y_swa, a_swa = _branch_fwd(o_swa, gates, 1, w_bs, "branch_swa")
    merged, dza, dsub, red = _out_stage(gates, mf_blk, y_fox, y_swa, w_o, x2, ada, ln_g, ln_b, tgt)
    loss = lax.psum(0.5 * red[4, 0] / d, ("x", "y", "c"))

    dmf, dms, dy_fox, dy_swa = _merge_bwd(dsub, w_o.T, gates, mf_blk, y_fox, y_swa)
    do_fox, dg_fox = _branch_bwd(dy_fox, w_bf.T, o_fox, gates, 0, "branch_fox_bwd")
    do_swa, dg_swa = _branch_bwd(dy_swa, w_bs.T, o_swa, gates, 1, "branch_swa_bwd")
    dq_f, dk_f, dv_f, dcol, drow = _fox_bwd(qkv_fox, cum_row, o_fox, lse, do_fox)
    dflog_t, dbf = _fox_gate_bwd(drow.reshape(FOX_H, s_len), dcol.reshape(FOX_H, s_len), flog_t, bf_col)
    dq_s, dk_s, dv_s, dsink = _swa_bwd(qkv_swa, sinks, do_swa)
    dflog = _pad_lanes(dflog_t.T, FLOG_PAD).astype(BF16)
    dproj = jnp.concatenate([dq_f.astype(BF16), dk_f, dv_f, dflog, dq_s, dk_s, dv_s, dg_fox, dg_swa, dmf, dms], axis=1)
    grad_x, red2 = _in_bwd(dproj, w_in_t, x2, ada, dza)

    g_w_in = _mm_acc(h.T, dproj, "grad_w_in")
    g_w_in = jnp.concatenate([g_w_in[:, :REAL_FLOG_END], g_w_in[:, OFF_SQ:]], axis=1)
    g_w_bf = _mm_acc(a_fox.T, dy_fox, "grad_w_br_fox")
    g_w_bs = _mm_acc(a_swa.T, dy_swa, "grad_w_br_swa")
    g_w_o = _mm_acc(merged.T, dsub, "grad_w_out")

    r_in, r_bf, r_bs, r_o = _grad_exchange(
        [
            g_w_in.reshape(d, N_DEV, c_in).transpose(1, 0, 2),
            g_w_bf.reshape(FOX_W, N_DEV, c_br).transpose(1, 0, 2),
            g_w_bs.reshape(SWA_W, N_DEV, c_br).transpose(1, 0, 2),
            g_w_o.reshape(N_DEV, d // N_DEV, d),
        ],
        "grad_exchange",
    )
    out_w_in = _sum_adam(r_in, w_in[0], m_w_in[0], v_w_in[0], "adam_w_in")
    out_w_bf = _sum_adam(r_bf, w_br_fox[0], m_w_br_fox[0], v_w_br_fox[0], "adam_w_br_fox")
    out_w_bs = _sum_adam(r_bs, w_br_swa[0], m_w_br_swa[0], v_w_br_swa[0], "adam_w_br_swa")
    out_w_o = _sum_adam(r_o, w_out[0], m_w_out[0], v_w_out[0], "adam_w_out")

    packed = jnp.concatenate([red2[0:1], red2[1:2], red[0:1], _pad_lanes(dbf[:, 0].reshape(1, FOX_H), 128), dsink, red[1:2], red[2:3]], axis=1)
    gathered = _gather_rows(packed, "ag_small")
    pack = lambda a, b, cc, dd, e: jnp.concatenate([a, _pad_lanes(b, 128), _pad_lanes(cc, 128), dd, e], axis=1)
    small = _small_adam(
        gathered,
        pack(b_ada, b_f, attn_sinks, ln_g, ln_b),
        pack(m_b_ada, m_b_f, m_attn_sinks, m_ln_g, m_ln_b),
        pack(v_b_ada, v_b_f, v_attn_sinks, v_ln_g, v_ln_b),
    )
    dada_cols = lax.dynamic_slice(gathered, (0, me * c_ada), (N_DEV, c_ada))
    out_w_ada = _wada_adam(c_all.T, dada_cols, w_ada[0], m_w_ada[0], v_w_ada[0])

    o1, o2, o3 = 3 * d, 3 * d + 128, 3 * d + 256

    def unpack(p):
        return p[:, :o1], p[:, o1 : o1 + FOX_H], p[:, o2 : o2 + SWA_HQ], p[:, o3 : o3 + d], p[:, o3 + d : o3 + 2 * d]

    kinds = []
    for k in range(4):
        b_ada_k, b_f_k, sinks_k, ln_g_k, ln_b_k = unpack(small[k])
        kinds.append(
            [out_w_ada[k][None], b_ada_k, out_w_in[k][None], b_f_k, sinks_k, out_w_bf[k][None], out_w_bs[k][None], out_w_o[k][None], ln_g_k, ln_b_k]
        )
    return (loss, grad_x[None], *kinds[0], *kinds[1], *kinds[2], *kinds[3])
```

```python
import numpy as np
import jax
import jax.numpy as jnp
from jax import lax
from jax.experimental import pallas as pl
from jax.experimental.pallas import tpu as pltpu

F32 = jnp.float32
BF16 = jnp.bfloat16
N_DEV = 8
MESH = pl.DeviceIdType.MESH

FOX_H, FOX_DH, FOX_W = 8, 128, 1024
SWA_HQ, SWA_HKV, SWA_DH, SWA_G = 16, 4, 64, 4
SWA_W, SWA_KVW, WINDOW = 1024, 256, 128
LN_EPS = 1e-5
NEG = -1e30
DEPTH = 1
ALPHA = (2.0 * DEPTH) ** 0.25
FOX_SCALE = FOX_DH ** -0.5
SWA_SCALE = SWA_DH ** -0.5
SLOPES = [2.0 ** (-8.0 * (h + 1.0) / SWA_HQ) for h in range(SWA_HQ)]

ADAM_LR, ADAM_B1, ADAM_B2, ADAM_EPS, ADAM_WD, ADAM_STEP = 0.001, 0.9, 0.999, 1e-08, 0.01, 10

N_FLOG = 8
FLOG_PAD = 512
OFF_FQ, OFF_FK, OFF_FV, OFF_FLOG = 0, 1024, 2048, 3072
OFF_SQ = OFF_FLOG + FLOG_PAD
OFF_SK = OFF_SQ + SWA_W
OFF_SV = OFF_SK + SWA_KVW
OFF_GF = OFF_SV + SWA_KVW
OFF_GS = OFF_GF + FOX_W
OFF_MF = OFF_GS + SWA_W
REAL_FLOG_END = OFF_FLOG + N_FLOG

ATT_BLK = 512
VMEM_LIMIT = 52 * 1024 * 1024


def _pcall(body, **kw):
    return pl.pallas_call(body, **kw)


def _cp(*sem):
    return pltpu.CompilerParams(dimension_semantics=sem, vmem_limit_bytes=VMEM_LIMIT)


def _sigmoid(x):
    return 1.0 / (1.0 + jnp.exp(-x))


def _all_gather(x, name, space):
    m_per, n = x.shape

    def body(x_ref, out_ref, send_sems, recv_sems, local_sem):
        mx, my, mc = lax.axis_index("x"), lax.axis_index("y"), lax.axis_index("c")
        me, sibling = (mx, my, mc), (mx, my, 1 - mc)
        chips = [(1 - mx, my), (mx, 1 - my), (1 - mx, 1 - my)]

        def rows(px, py, pc):
            return out_ref.at[pl.ds((4 * px + 2 * py + pc) * m_per, m_per), :]

        def copy(k, block, to, src=None):
            return pltpu.make_async_remote_copy(
                src_ref=rows(*block) if src is None else src,
                dst_ref=rows(*block),
                send_sem=send_sems.at[k],
                recv_sem=recv_sems.at[k],
                device_id=to,
                device_id_type=MESH,
            )

        mine = pltpu.make_async_copy(x_ref, rows(*me), local_sem)
        mine.start()
        first = [copy(0, me, sibling, src=x_ref)]
        first += [copy(1 + j, me, (*chip, mc), src=x_ref) for j, chip in enumerate(chips)]
        for cp in first:
            cp.start()
        passed = [copy(4 + j, (*chip, mc), sibling) for j, chip in enumerate(chips)]
        for j, chip in enumerate(chips):
            copy(1 + j, (*chip, mc), me).wait_recv()
            passed[j].start()
        copy(0, sibling, me).wait_recv()
        for j, chip in enumerate(chips):
            copy(4 + j, (*chip, 1 - mc), me).wait_recv()
        for cp in first + passed:
            cp.wait_send()
        mine.wait()

    return _pcall(
        body,
        name=name,
        out_shape=jax.ShapeDtypeStruct((N_DEV * m_per, n), x.dtype),
        in_specs=[pl.BlockSpec(memory_space=space)],
        out_specs=pl.BlockSpec(memory_space=space),
        scratch_shapes=[pltpu.SemaphoreType.DMA((7,)), pltpu.SemaphoreType.DMA((7,)), pltpu.SemaphoreType.DMA],
    )(x)


def _grad_exchange(blocks, name):
    n = len(blocks)

    def body(*refs):
        ins, outs = refs[:n], refs[n : 2 * n]
        send_sems, recv_sems, local_sems = refs[2 * n :]
        mx, my, mc = lax.axis_index("x"), lax.axis_index("y"), lax.axis_index("c")
        me = 4 * mx + 2 * my + mc
        copies, local = [], []
        for a in range(n):
            own = pltpu.make_async_copy(ins[a].at[me], outs[a].at[0], local_sems.at[a])
            own.start()
            local.append(own)
            for d in range(1, N_DEV):
                px = 1 - mx if (d >> 2) & 1 else mx
                py = 1 - my if (d >> 1) & 1 else my
                pc = 1 - mc if d & 1 else mc
                cp = pltpu.make_async_remote_copy(
                    src_ref=ins[a].at[4 * px + 2 * py + pc],
                    dst_ref=outs[a].at[d],
                    send_sem=send_sems.at[a * 7 + d - 1],
                    recv_sem=recv_sems.at[a * 7 + d - 1],
                    device_id=(px, py, pc),
                    device_id_type=MESH,
                )
                cp.start()
                copies.append(cp)
        for cp in copies:
            cp.wait_recv()
        for cp in copies:
            cp.wait_send()
        for own in local:
            own.wait()

    hbm = pl.BlockSpec(memory_space=pltpu.HBM)
    return _pcall(
        body,
        name=name,
        out_shape=[jax.ShapeDtypeStruct(b.shape, b.dtype) for b in blocks],
        in_specs=[hbm] * n,
        out_specs=[hbm] * n,
        scratch_shapes=[
            pltpu.SemaphoreType.DMA((7 * n,)),
            pltpu.SemaphoreType.DMA((7 * n,)),
            pltpu.SemaphoreType.DMA((n,)),
        ],
    )(*blocks)


def _gather_rows(v, name):
    n = v.shape[1]
    g = _all_gather(jnp.broadcast_to(v, (8, n)), name, pltpu.VMEM)
    return g.reshape(N_DEV, 8, n)[:, 0, :]


def _adamw(w, g, m, v):
    m = ADAM_B1 * m + (1.0 - ADAM_B1) * g
    v = ADAM_B2 * v + (1.0 - ADAM_B2) * (g * g)
    m_hat = m / (1.0 - ADAM_B1**ADAM_STEP)
    v_hat = v / (1.0 - ADAM_B2**ADAM_STEP)
    delta = -ADAM_LR * (m_hat / (jnp.sqrt(v_hat) + ADAM_EPS) + ADAM_WD * w)
    return delta, m, v


def _sum_adam(recv, w, m, v, name):
    _, r_tot, c = recv.shape
    c_pad = -(-c // 128) * 128
    tr = r_tot
    while 8 * tr * c_pad * 4 > 6 * 1024 * 1024 and tr % 32 == 0:
        tr //= 2

    def body(r_ref, w_ref, m_ref, v_ref, g_ref, d_ref, nm_ref, nv_ref):
        g = r_ref[0].astype(F32)
        for k in range(1, N_DEV):
            g = g + r_ref[k].astype(F32)
        d, nm, nv = _adamw(w_ref[...], g, m_ref[...], v_ref[...])
        g_ref[...] = g
        d_ref[...] = d
        nm_ref[...] = nm
        nv_ref[...] = nv

    blk = pl.BlockSpec((tr, c), lambda i: (i, 0))
    return _pcall(
        body,
        name=name,
        grid=(r_tot // tr,),
        out_shape=[jax.ShapeDtypeStruct((r_tot, c), F32)] * 4,
        in_specs=[pl.BlockSpec((N_DEV, tr, c), lambda i: (0, i, 0)), blk, blk, blk],
        out_specs=[blk] * 4,
        compiler_params=_cp("parallel"),
    )(recv, w, m, v)


def _wada_adam(c_t, dada_cols, w, m, v):
    d_model, c = w.shape
    tr = min(256, d_model)

    def body(ct_ref, da_ref, w_ref, m_ref, v_ref, g_ref, d_ref, nm_ref, nv_ref):
        g = jnp.dot(ct_ref[...].astype(BF16), da_ref[...].astype(BF16), preferred_element_type=F32)
        d, nm, nv = _adamw(w_ref[...], g, m_ref[...], v_ref[...])
        g_ref[...] = g
        d_ref[...] = d
        nm_ref[...] = nm
        nv_ref[...] = nv

    blk = pl.BlockSpec((tr, c), lambda i: (i, 0))
    return _pcall(
        body,
        name="wada_adam",
        grid=(d_model // tr,),
        out_shape=[jax.ShapeDtypeStruct((d_model, c), F32)] * 4,
        in_specs=[pl.BlockSpec((tr, N_DEV), lambda i: (i, 0)), pl.BlockSpec((N_DEV, c), lambda i: (0, 0)), blk, blk, blk],
        out_specs=[blk] * 4,
        compiler_params=_cp("parallel"),
    )(c_t, dada_cols, w, m, v)


def _small_adam(gathered, w, m, v):
    p = w.shape[1]

    def body(a_ref, w_ref, m_ref, v_ref, g_ref, d_ref, nm_ref, nv_ref):
        g = a_ref[0:1, :]
        for k in range(1, N_DEV):
            g = g + a_ref[k : k + 1, :]
        d, nm, nv = _adamw(w_ref[...], g, m_ref[...], v_ref[...])
        g_ref[...] = g
        d_ref[...] = d
        nm_ref[...] = nm
        nv_ref[...] = nv

    return _pcall(
        body,
        name="small_adam",
        out_shape=[jax.ShapeDtypeStruct((1, p), F32)] * 4,
    )(gathered, w, m, v)


def _ada_fwd(c_all, w_ada, b_cols):
    c = w_ada.shape[1]

    def body(c_ref, w_ref, b_ref, o_ref):
        o_ref[...] = jnp.dot(c_ref[...].astype(BF16), w_ref[...].astype(BF16), preferred_element_type=F32) + b_ref[...]

    return _pcall(
        body,
        name="ada_fwd",
        out_shape=jax.ShapeDtypeStruct((N_DEV, c), F32),
        compiler_params=_cp(),
    )(c_all, w_ada, b_cols)


def _ln_mod(x, ada):
    s_len, d = x.shape
    tm = min(512, s_len)

    def body(x_ref, sh_ref, sc_ref, h_ref):
        xv = x_ref[...]
        mu = jnp.mean(xv, axis=-1, keepdims=True)
        xc = xv - mu
        var = jnp.mean(xc * xc, axis=-1, keepdims=True)
        xhat = xc * lax.rsqrt(var + LN_EPS)
        h_ref[...] = (xhat * (1.0 + sc_ref[...]) + sh_ref[...]).astype(BF16)

    return _pcall(
        body,
        name="ln_mod",
        grid=(s_len // tm,),
        out_shape=jax.ShapeDtypeStruct((s_len, d), BF16),
        in_specs=[
            pl.BlockSpec((tm, d), lambda i: (i, 0)),
            pl.BlockSpec((1, d), lambda i: (0, 0)),
            pl.BlockSpec((1, d), lambda i: (0, 1)),
        ],
        out_specs=pl.BlockSpec((tm, d), lambda i: (i, 0)),
        compiler_params=_cp("parallel"),
    )(x, ada, ada)


def _mm_cols(a, b, col_off, n_cols, out_dtype, name):
    m, k = a.shape
    tm, tn = min(1024, m), 512
    off = col_off // tn

    def body(a_ref, b_ref, o_ref):
        o_ref[...] = jnp.dot(a_ref[...], b_ref[...], preferred_element_type=F32).astype(out_dtype)

    return _pcall(
        body,
        name=name,
        grid=(m // tm, n_cols // tn),
        out_shape=jax.ShapeDtypeStruct((m, n_cols), out_dtype),
        in_specs=[pl.BlockSpec((tm, k), lambda i, j: (i, 0)), pl.BlockSpec((k, tn), lambda i, j: (0, off + j))],
        out_specs=pl.BlockSpec((tm, tn), lambda i, j: (i, j)),
        compiler_params=_cp("parallel", "parallel"),
    )(a, b)


def _mm_acc(a, b, name):
    m, k = a.shape
    n = b.shape[1]
    tm, tn, tk = min(1024, m), min(1024, n), min(512, k)
    nk = k // tk

    def body(a_ref, b_ref, o_ref, acc_s):
        kk = pl.program_id(2)
        part = jnp.dot(a_ref[...], b_ref[...], preferred_element_type=F32)

        @pl.when(kk == 0)
        def _():
            acc_s[...] = part

        @pl.when(kk > 0)
        def _():
            acc_s[...] += part

        @pl.when(kk == nk - 1)
        def _():
            o_ref[...] = acc_s[...].astype(BF16)

    return _pcall(
        body,
        name=name,
        grid=(m // tm, n // tn, nk),
        out_shape=jax.ShapeDtypeStruct((m, n), BF16),
        in_specs=[pl.BlockSpec((tm, tk), lambda i, j, kk: (i, kk)), pl.BlockSpec((tk, tn), lambda i, j, kk: (kk, j))],
        out_specs=pl.BlockSpec((tm, tn), lambda i, j, kk: (i, j)),
        scratch_shapes=[pltpu.VMEM((tm, tn), F32)],
        compiler_params=_cp("parallel", "parallel", "arbitrary"),
    )(a, b)


def _split3(a):
    hi = a.astype(BF16)
    r1 = a - hi.astype(F32)
    mid = r1.astype(BF16)
    lo = (r1 - mid.astype(F32)).astype(BF16)
    return hi, mid, lo


def _dot_ones(a, tri):
    return sum(jnp.dot(t, tri, preferred_element_type=F32) for t in _split3(a))


def _log_sigmoid(x):
    return jnp.minimum(x, 0.0) - jnp.log1p(jnp.exp(-jnp.abs(x)))


def _fox_cum(flog_t, bf_col):
    s_len = flog_t.shape[1]

    def body(fl_ref, bf_ref, cum_ref):
        r = lax.broadcasted_iota(jnp.int32, (128, 128), 0)
        c = lax.broadcasted_iota(jnp.int32, (128, 128), 1)
        upper = (r <= c).astype(BF16)

        def step(t, carry):
            sl = pl.ds(pl.multiple_of(t * 128, 128), 128)
            lf = _log_sigmoid(fl_ref[:, sl] + bf_ref[...])
            cs = _dot_ones(lf, upper) + carry
            cum_ref[:, sl] = cs
            return cs[:, 127:128]

        lax.fori_loop(0, s_len // 128, step, jnp.zeros((FOX_H, 1), F32))

    return _pcall(body, name="fox_cum", out_shape=jax.ShapeDtypeStruct((FOX_H, s_len), F32))(flog_t, bf_col)


def _fox_gate_bwd(drow, dcol, flog_t, bf_col):
    s_len = flog_t.shape[1]
    n = s_len // 128

    def body(dr_ref, dc_ref, fl_ref, bf_ref, dfl_ref, dbf_ref):
        r = lax.broadcasted_iota(jnp.int32, (128, 128), 0)
        c = lax.broadcasted_iota(jnp.int32, (128, 128), 1)
        lower = (r >= c).astype(BF16)

        def step(t, carry):
            run, tot = carry
            sl = pl.ds(pl.multiple_of((n - 1 - t) * 128, 128), 128)
            rc = _dot_ones(dr_ref[:, sl] - dc_ref[:, sl], lower) + run
            dfl = rc * _sigmoid(-(fl_ref[:, sl] + bf_ref[...]))
            dfl_ref[:, sl] = dfl
            return rc[:, 0:1], tot + jnp.sum(dfl, axis=1, keepdims=True)

        zero = jnp.zeros((FOX_H, 1), F32)
        _, tot = lax.fori_loop(0, n, step, (zero, zero))
        dbf_ref[...] = jnp.broadcast_to(tot, (FOX_H, 128))

    return _pcall(
        body,
        name="fox_gate_bwd",
        out_shape=[jax.ShapeDtypeStruct((FOX_H, s_len), F32), jax.ShapeDtypeStruct((FOX_H, 128), F32)],
    )(drow, dcol, flog_t, bf_col)


def _diag_mask(blk, transposed=False):
    r = lax.broadcasted_iota(jnp.int32, (blk, blk), 0)
    c = lax.broadcasted_iota(jnp.int32, (blk, blk), 1)
    return c >= r if transposed else r >= c


_NT = (((1,), (1,)), ((), ()))
_TN = (((0,), (0,)), ((), ()))


def _fox_fwd(qkv, cum_row):
    s_len = qkv.shape[0]
    blk = min(ATT_BLK, s_len)
    nb = s_len // blk

    def body(q_ref, k_ref, v_ref, c_ref, o_ref, lse_ref, m_s, l_s, acc_s):
        i = pl.program_id(1)
        m_s[...] = jnp.full(m_s.shape, NEG, F32)
        l_s[...] = jnp.zeros(l_s.shape, F32)
        acc_s[...] = jnp.zeros(acc_s.shape, F32)

        def tile(j, masked):
            cols = pl.ds(pl.multiple_of(j * blk, blk), blk)
            s = lax.dot_general(q_ref[...], k_ref[cols, :], _NT, preferred_element_type=F32) * FOX_SCALE - c_ref[:, cols]
            if masked:
                s = jnp.where(_diag_mask(blk), s, NEG)
            m_prev = m_s[...]
            m_new = jnp.maximum(m_prev, jnp.max(s, axis=1, keepdims=True))
            a = jnp.exp(m_prev - m_new)
            p = jnp.exp(s - m_new)
            l_s[...] = a * l_s[...] + jnp.sum(p, axis=1, keepdims=True)
            acc_s[...] = a * acc_s[...] + jnp.dot(p.astype(BF16), v_ref[cols, :], preferred_element_type=F32)
            m_s[...] = m_new

        def below(j, carry):
            tile(j, False)
            return carry

        lax.fori_loop(0, i, below, 0)
        tile(i, True)
        o_ref[...] = acc_s[...] / l_s[...]
        lse_ref[...] = m_s[...] + jnp.log(l_s[...])

    return _pcall(
        body,
        name="fox_fwd",
        grid=(FOX_H, nb),
        out_shape=[jax.ShapeDtypeStruct((s_len, FOX_W), F32), jax.ShapeDtypeStruct((FOX_H, s_len, 1), F32)],
        in_specs=[
            pl.BlockSpec((blk, FOX_DH), lambda h, i: (i, h)),
            pl.BlockSpec((s_len, FOX_DH), lambda h, i: (0, FOX_H + h)),
            pl.BlockSpec((s_len, FOX_DH), lambda h, i: (0, 2 * FOX_H + h)),
            pl.BlockSpec((None, 1, s_len), lambda h, i: (h, 0, 0)),
        ],
        out_specs=[
            pl.BlockSpec((blk, FOX_DH), lambda h, i: (i, h)),
            pl.BlockSpec((None, blk, 1), lambda h, i: (h, i, 0)),
        ],
        scratch_shapes=[pltpu.VMEM((blk, 1), F32), pltpu.VMEM((blk, 1), F32), pltpu.VMEM((blk, FOX_DH), F32)],
        compiler_params=_cp("parallel", "arbitrary"),
    )(qkv, qkv, qkv, cum_row)


def _fox_bwd(qkv, cum_col, lse_row, delta_row, do):
    s_len = qkv.shape[0]
    blk = min(ATT_BLK, s_len)
    nb = s_len // blk

    def body(q_ref, k_ref, v_ref, c_ref, lse_ref, dl_ref, do_ref, dq_ref, dk_ref, dv_ref, dc_ref, dr_ref, dk_s, dv_s, dc_s, cb_s):
        j = pl.program_id(1)

        @pl.when(j == 0)
        def _():
            dq_ref[...] = jnp.zeros(dq_ref.shape, F32)
            dr_ref[...] = jnp.zeros(dr_ref.shape, F32)

        dk_s[...] = jnp.zeros(dk_s.shape, F32)
        dv_s[...] = jnp.zeros(dv_s.shape, F32)
        dc_s[...] = jnp.zeros(dc_s.shape, F32)
        cb_s[...] = jnp.broadcast_to(c_ref[...], cb_s.shape)

        def tile(i, diag):
            rows = pl.ds(pl.multiple_of(i * blk, blk), blk)
            q, dob = q_ref[rows, :], do_ref[rows, :]
            k, v = k_ref[...], v_ref[...]
            s_t = lax.dot_general(k, q, _NT, preferred_element_type=F32) * FOX_SCALE - cb_s[...]
            p_t = jnp.exp(s_t - lse_ref[:, rows])
            if diag:
                p_t = jnp.where(_diag_mask(blk, transposed=True), p_t, 0.0)
            dp_t = lax.dot_general(v, dob, _NT, preferred_element_type=F32)
            ds_t = p_t * (dp_t - dl_ref[:, rows])
            dsb = ds_t.astype(BF16)
            dv_s[...] += jnp.dot(p_t.astype(BF16), dob, preferred_element_type=F32)
            dk_s[...] += jnp.dot(dsb, q, preferred_element_type=F32)
            dq_c = lax.dot_general(dsb, k, _TN, preferred_element_type=F32)
            part = ds_t[:, 0:128]
            for t in range(1, blk // 128):
                part = part + ds_t[:, t * 128 : (t + 1) * 128]
            dc_s[...] += part
            dr_ref[:, rows] += jnp.sum(ds_t, axis=0, keepdims=True)
            if diag:
                dq_ref[rows, :] = (dq_ref[rows, :] + dq_c) * FOX_SCALE
            else:
                dq_ref[rows, :] += dq_c

        tile(j, True)

        def below(i, carry):
            tile(i, False)
            return carry

        lax.fori_loop(j + 1, nb, below, 0)
        dk_ref[...] = (dk_s[...] * FOX_SCALE).astype(BF16)
        dv_ref[...] = dv_s[...].astype(BF16)
        dc_ref[...] = jnp.sum(dc_s[...], axis=1, keepdims=True)

    head = lambda h, j: (0, h)
    row = pl.BlockSpec((None, 1, s_len), lambda h, j: (h, 0, 0))
    return _pcall(
        body,
        name="fox_bwd",
        grid=(FOX_H, nb),
        out_shape=[
            jax.ShapeDtypeStruct((s_len, FOX_W), F32),
            jax.ShapeDtypeStruct((s_len, FOX_W), BF16),
            jax.ShapeDtypeStruct((s_len, FOX_W), BF16),
            jax.ShapeDtypeStruct((FOX_H, s_len, 1), F32),
            jax.ShapeDtypeStruct((FOX_H, 1, s_len), F32),
        ],
        in_specs=[
            pl.BlockSpec((s_len, FOX_DH), head),
            pl.BlockSpec((blk, FOX_DH), lambda h, j: (j, FOX_H + h)),
            pl.BlockSpec((blk, FOX_DH), lambda h, j: (j, 2 * FOX_H + h)),
            pl.BlockSpec((None, blk, 1), lambda h, j: (h, j, 0)),
            row,
            row,
            pl.BlockSpec((s_len, FOX_DH), head),
        ],
        out_specs=[
            pl.BlockSpec((s_len, FOX_DH), head),
            pl.BlockSpec((blk, FOX_DH), lambda h, j: (j, h)),
            pl.BlockSpec((blk, FOX_DH), lambda h, j: (j, h)),
            pl.BlockSpec((None, blk, 1), lambda h, j: (h, j, 0)),
            row,
        ],
        scratch_shapes=[
            pltpu.VMEM((blk, FOX_DH), F32),
            pltpu.VMEM((blk, FOX_DH), F32),
            pltpu.VMEM((blk, 128), F32),
            pltpu.VMEM((blk, blk), F32),
        ],
        compiler_params=_cp("parallel", "arbitrary"),
    )(qkv, qkv, qkv, cum_col, lse_row, delta_row, do)


def _swa_probs(i, q_ref, kk, sinks_ref, h):
    r = lax.broadcasted_iota(jnp.int32, (WINDOW, 2 * WINDOW), 0)
    c = lax.broadcasted_iota(jnp.int32, (WINDOW, 2 * WINDOW), 1)
    dist = r - c + WINDOW
    valid = (dist >= 0) & (dist < WINDOW) & ((c >= WINDOW) | (i > 0))
    g = h // SWA_G
    q = q_ref[:, h * SWA_DH : (h + 1) * SWA_DH]
    k = kk[:, g * SWA_DH : (g + 1) * SWA_DH]
    s = lax.dot_general(q, k, _NT, preferred_element_type=F32) * SWA_SCALE - SLOPES[h] * dist.astype(F32)
    s = jnp.where(valid, s, NEG)
    sink = sinks_ref[h]
    m = jnp.maximum(jnp.max(s, axis=1, keepdims=True), sink)
    e = jnp.exp(s - m)
    e_sink = jnp.exp(sink - m)
    inv = 1.0 / (jnp.sum(e, axis=1, keepdims=True) + e_sink)
    return q, k, e * inv, e_sink * inv


def _swa_specs(col_q, col_k, col_v, rev, nb):
    def blk(t):
        return nb - 1 - t if rev else t

    return [
        pl.BlockSpec((WINDOW, SWA_W), lambda t: (blk(t), col_q)),
        pl.BlockSpec((WINDOW, SWA_KVW), lambda t: (jnp.maximum(blk(t) - 1, 0), col_k)),
        pl.BlockSpec((WINDOW, SWA_KVW), lambda t: (blk(t), col_k)),
        pl.BlockSpec((WINDOW, SWA_KVW), lambda t: (jnp.maximum(blk(t) - 1, 0), col_v)),
        pl.BlockSpec((WINDOW, SWA_KVW), lambda t: (blk(t), col_v)),
    ]


def _swa_fwd(qkv, sinks):
    s_len = qkv.shape[0]
    nb = s_len // WINDOW

    def body(q_ref, kp_ref, kc_ref, vp_ref, vc_ref, sinks_ref, o_ref):
        i = pl.program_id(0)
        kk = jnp.concatenate([kp_ref[...], kc_ref[...]], axis=0)
        vv = jnp.concatenate([vp_ref[...], vc_ref[...]], axis=0)
        for h in range(SWA_HQ):
            g = h // SWA_G
            _, _, p, _ = _swa_probs(i, q_ref, kk, sinks_ref, h)
            v = vv[:, g * SWA_DH : (g + 1) * SWA_DH]
            o_ref[:, h * SWA_DH : (h + 1) * SWA_DH] = jnp.dot(p.astype(BF16), v, preferred_element_type=F32)

    return _pcall(
        body,
        name="swa_fwd",
        grid=(nb,),
        out_shape=jax.ShapeDtypeStruct((s_len, SWA_W), F32),
        in_specs=_swa_specs(0, 4, 5, False, nb) + [pl.BlockSpec(memory_space=pltpu.SMEM)],
        out_specs=pl.BlockSpec((WINDOW, SWA_W), lambda t: (t, 0)),
        compiler_params=_cp("parallel"),
    )(qkv, qkv, qkv, qkv, qkv, sinks)


def _swa_bwd(qkv, sinks, do):
    s_len = qkv.shape[0]
    nb = s_len // WINDOW

    def body(q_ref, kp_ref, kc_ref, vp_ref, vc_ref, sinks_ref, do_ref, dq_ref, dk_ref, dv_ref, dsink_ref, ck_s, cv_s, dkk_s, dvv_s):
        t = pl.program_id(0)
        i = nb - 1 - t

        @pl.when(t == 0)
        def _():
            ck_s[...] = jnp.zeros(ck_s.shape, F32)
            cv_s[...] = jnp.zeros(cv_s.shape, F32)
            dsink_ref[...] = jnp.zeros(dsink_ref.shape, F32)

        kk = jnp.concatenate([kp_ref[...], kc_ref[...]], axis=0)
        vv = jnp.concatenate([vp_ref[...], vc_ref[...]], axis=0)
        lane = lax.broadcasted_iota(jnp.int32, (1, 128), 1)
        dsink = jnp.zeros((1, 128), F32)
        for g in range(SWA_HKV):
            cols = slice(g * SWA_DH, (g + 1) * SWA_DH)
            v = vv[:, cols]
            dkk = jnp.zeros((2 * WINDOW, SWA_DH), F32)
            dvv = jnp.zeros((2 * WINDOW, SWA_DH), F32)
            for h in range(g * SWA_G, (g + 1) * SWA_G):
                hc = slice(h * SWA_DH, (h + 1) * SWA_DH)
                q, k, p, p_sink = _swa_probs(i, q_ref, kk, sinks_ref, h)
                dob = do_ref[:, hc]
                dp = lax.dot_general(dob, v, _NT, preferred_element_type=F32)
                delta = jnp.sum(p * dp, axis=1, keepdims=True)
                dsb = (p * (dp - delta)).astype(BF16)
                dq_ref[:, hc] = (jnp.dot(dsb, k, preferred_element_type=F32) * SWA_SCALE).astype(BF16)
                dkk = dkk + lax.dot_general(dsb, q, _TN, preferred_element_type=F32)
                dvv = dvv + lax.dot_general(p.astype(BF16), dob, _TN, preferred_element_type=F32)
                dsink = dsink + jnp.where(lane == h, -jnp.sum(p_sink * delta, axis=0, keepdims=True), 0.0)
            dkk_s[:, cols] = dkk * SWA_SCALE
            dvv_s[:, cols] = dvv
        dk_ref[...] = (dkk_s[WINDOW:, :] + ck_s[...]).astype(BF16)
        dv_ref[...] = (dvv_s[WINDOW:, :] + cv_s[...]).astype(BF16)
        ck_s[...] = dkk_s[:WINDOW, :]
        cv_s[...] = dvv_s[:WINDOW, :]
        dsink_ref[...] += dsink

    row = lambda t: (nb - 1 - t, 0)
    return _pcall(
        body,
        name="swa_bwd",
        grid=(nb,),
        out_shape=[
            jax.ShapeDtypeStruct((s_len, SWA_W), BF16),
            jax.ShapeDtypeStruct((s_len, SWA_KVW), BF16),
            jax.ShapeDtypeStruct((s_len, SWA_KVW), BF16),
            jax.ShapeDtypeStruct((1, 128), F32),
        ],
        in_specs=_swa_specs(0, 4, 5, True, nb)
        + [pl.BlockSpec(memory_space=pltpu.SMEM), pl.BlockSpec((WINDOW, SWA_W), row)],
        out_specs=[
            pl.BlockSpec((WINDOW, SWA_W), row),
            pl.BlockSpec((WINDOW, SWA_KVW), row),
            pl.BlockSpec((WINDOW, SWA_KVW), row),
            pl.BlockSpec((1, 128), lambda t: (0, 0)),
        ],
        scratch_shapes=[
            pltpu.VMEM((WINDOW, SWA_KVW), F32),
            pltpu.VMEM((WINDOW, SWA_KVW), F32),
            pltpu.VMEM((2 * WINDOW, SWA_KVW), F32),
            pltpu.VMEM((2 * WINDOW, SWA_KVW), F32),
        ],
        compiler_params=_cp("arbitrary"),
    )(qkv, qkv, qkv, qkv, qkv, sinks, do)


def _branch_fwd(o, gates, g_blk, w_b, name):
    s_len, wd = o.shape
    d = w_b.shape[1]
    tm = min(512, s_len)

    def body(o_ref, g_ref, w_ref, y_ref, a_ref):
        g = g_ref[...]
        a = (o_ref[...] * (g * _sigmoid(g))).astype(BF16)
        a_ref[...] = a
        y_ref[...] = jnp.dot(a, w_ref[...], preferred_element_type=F32)

    return _pcall(
        body,
        name=name,
        grid=(s_len // tm,),
        out_shape=[jax.ShapeDtypeStruct((s_len, d), F32), jax.ShapeDtypeStruct((s_len, wd), BF16)],
        in_specs=[
            pl.BlockSpec((tm, wd), lambda i: (i, 0)),
            pl.BlockSpec((tm, wd), lambda i: (i, g_blk)),
            pl.BlockSpec((wd, d), lambda i: (0, 0)),
        ],
        out_specs=[pl.BlockSpec((tm, d), lambda i: (i, 0)), pl.BlockSpec((tm, wd), lambda i: (i, 0))],
        compiler_params=_cp("parallel"),
    )(o, gates, w_b)


def _out_stage(gates, mf_blk, y_fox, y_swa, w_out, x, ada, ln_g, ln_b, target):
    s_len, d = x.shape
    tm = min(128, s_len)
    n_steps = s_len // tm

    def body(mf_ref, ms_ref, yf_ref, ys_ref, w_ref, x_ref, gate_ref, lg_ref, lb_ref, t_ref, mg_ref, dza_ref, dsub_ref, red_ref):
        i = pl.program_id(0)
        merged = _sigmoid(mf_ref[...]) * yf_ref[...] + _sigmoid(ms_ref[...]) * ys_ref[...]
        mb = merged.astype(BF16)
        mg_ref[...] = mb
        sub = jnp.dot(mb, w_ref[...], preferred_element_type=F32)
        gate = gate_ref[...]
        z = ALPHA * x_ref[...] + gate * sub
        mu = jnp.mean(z, axis=-1, keepdims=True)
        zc = z - mu
        var = jnp.mean(zc * zc, axis=-1, keepdims=True)
        rstd = lax.rsqrt(var + LN_EPS)
        zhat = zc * rstd
        err = zhat * lg_ref[...] + lb_ref[...] - t_ref[...]
        dout = err * (1.0 / d)
        dzhat = dout * lg_ref[...]
        dz = rstd * (dzhat - jnp.mean(dzhat, axis=-1, keepdims=True) - zhat * jnp.mean(dzhat * zhat, axis=-1, keepdims=True))
        dza_ref[...] = ALPHA * dz
        dsub_ref[...] = (gate * dz).astype(BF16)
        part = jnp.concatenate(
            [
                jnp.sum(dz * sub, axis=0, keepdims=True),
                jnp.sum(dout * zhat, axis=0, keepdims=True),
                jnp.sum(dout, axis=0, keepdims=True),
                jnp.sum(err * err, axis=0, keepdims=True),
                jnp.zeros((4, d), F32),
            ],
            axis=0,
        )

        @pl.when(i == 0)
        def _():
            red_ref[...] = part

        @pl.when(i > 0)
        def _():
            red_ref[...] += part

        @pl.when(i == n_steps - 1)
        def _():
            red_ref[4:5, :] = jnp.broadcast_to(jnp.sum(red_ref[3:4, :], axis=1, keepdims=True), (1, d))

    row = pl.BlockSpec((tm, d), lambda i: (i, 0))
    vec = pl.BlockSpec((1, d), lambda i: (0, 0))
    return _pcall(
        body,
        name="out_stage",
        grid=(n_steps,),
        out_shape=[
            jax.ShapeDtypeStruct((s_len, d), BF16),
            jax.ShapeDtypeStruct((s_len, d), F32),
            jax.ShapeDtypeStruct((s_len, d), BF16),
            jax.ShapeDtypeStruct((8, d), F32),
        ],
        in_specs=[
            pl.BlockSpec((tm, d), lambda i: (i, mf_blk)),
            pl.BlockSpec((tm, d), lambda i: (i, mf_blk + 1)),
            row,
            row,
            pl.BlockSpec((d, d), lambda i: (0, 0)),
            row,
            pl.BlockSpec((1, d), lambda i: (0, 2)),
            vec,
            vec,
            row,
        ],
        out_specs=[row, row, row, pl.BlockSpec((8, d), lambda i: (0, 0))],
        compiler_params=_cp("arbitrary"),
    )(gates, gates, y_fox, y_swa, w_out, x, ada, ln_g, ln_b, target)


def _merge_bwd(dsub, w_out, gates, mf_blk, y_fox, y_swa):
    s_len, d = dsub.shape
    tm = min(128, s_len)

    def body(ds_ref, w_ref, mf_ref, ms_ref, yf_ref, ys_ref, dmf_ref, dms_ref, dyf_ref, dys_ref):
        dm = lax.dot_general(ds_ref[...], w_ref[...], _NT, preferred_element_type=F32)
        sf, ss = _sigmoid(mf_ref[...]), _sigmoid(ms_ref[...])
        dmf_ref[...] = (dm * yf_ref[...] * (sf * (1.0 - sf))).astype(BF16)
        dms_ref[...] = (dm * ys_ref[...] * (ss * (1.0 - ss))).astype(BF16)
        dyf_ref[...] = (dm * sf).astype(BF16)
        dys_ref[...] = (dm * ss).astype(BF16)

    row = pl.BlockSpec((tm, d), lambda i: (i, 0))
    return _pcall(
        body,
        name="merge_bwd",
        grid=(s_len // tm,),
        out_shape=[jax.ShapeDtypeStruct((s_len, d), BF16)] * 4,
        in_specs=[
            row,
            pl.BlockSpec((d, d), lambda i: (0, 0)),
            pl.BlockSpec((tm, d), lambda i: (i, mf_blk)),
            pl.BlockSpec((tm, d), lambda i: (i, mf_blk + 1)),
            row,
            row,
        ],
        out_specs=[row] * 4,
        compiler_params=_cp("parallel"),
    )(dsub, w_out, gates, gates, y_fox, y_swa)


def _branch_bwd(dy, w_b, o, gates, g_blk, name, n_heads):
    s_len, d = dy.shape
    wd = w_b.shape[0]
    tm = min(512, s_len)

    def body(dy_ref, w_ref, o_ref, g_ref, do_ref, dg_ref, *rest):
        da = lax.dot_general(dy_ref[...], w_ref[...], _NT, preferred_element_type=F32)
        g = g_ref[...]
        sg = _sigmoid(g)
        do = da * (g * sg)
        do_ref[...] = do.astype(BF16)
        o = o_ref[...]
        dg_ref[...] = (da * o * (sg * (1.0 + g * (1.0 - sg)))).astype(BF16)
        if n_heads:
            prod = do * o
            lane = lax.broadcasted_iota(jnp.int32, (1, 128), 1)
            delta = jnp.zeros((tm, 128), F32)
            for h in range(n_heads):
                dh = jnp.sum(prod[:, h * 128 : (h + 1) * 128], axis=1, keepdims=True)
                delta = delta + jnp.where(lane == h, dh, 0.0)
            rest[0][...] = delta

    out_shape = [jax.ShapeDtypeStruct((s_len, wd), BF16), jax.ShapeDtypeStruct((s_len, wd), BF16)]
    out_specs = [pl.BlockSpec((tm, wd), lambda i: (i, 0))] * 2
    if n_heads:
        out_shape.append(jax.ShapeDtypeStruct((s_len, 128), F32))
        out_specs.append(pl.BlockSpec((tm, 128), lambda i: (i, 0)))
    return _pcall(
        body,
        name=name,
        grid=(s_len // tm,),
        out_shape=out_shape,
        in_specs=[
            pl.BlockSpec((tm, d), lambda i: (i, 0)),
            pl.BlockSpec((wd, d), lambda i: (0, 0)),
            pl.BlockSpec((tm, wd), lambda i: (i, 0)),
            pl.BlockSpec((tm, wd), lambda i: (i, g_blk)),
        ],
        out_specs=out_specs,
        compiler_params=_cp("parallel"),
    )(dy, w_b, o, gates)


def _in_bwd(dproj, w_in, x, ada, dza):
    s_len, d = x.shape
    k_tot = dproj.shape[1]
    tm, tk = min(256, s_len), 1024
    nk = k_tot // tk

    def body(dp_ref, w_ref, x_ref, sc_ref, dza_ref, gx_ref, red_ref, acc_s):
        i, kk = pl.program_id(0), pl.program_id(1)
        part = lax.dot_general(dp_ref[...], w_ref[...], _NT, preferred_element_type=F32)

        @pl.when(kk == 0)
        def _():
            acc_s[...] = part

        @pl.when(kk > 0)
        def _():
            acc_s[...] += part

        @pl.when(kk == nk - 1)
        def _():
            dh = acc_s[...]
            xv = x_ref[...]
            mu = jnp.mean(xv, axis=-1, keepdims=True)
            xc = xv - mu
            var = jnp.mean(xc * xc, axis=-1, keepdims=True)
            rstd = lax.rsqrt(var + LN_EPS)
            xhat = xc * rstd
            dxhat = dh * (1.0 + sc_ref[...])
            dx = rstd * (dxhat - jnp.mean(dxhat, axis=-1, keepdims=True) - xhat * jnp.mean(dxhat * xhat, axis=-1, keepdims=True))
            gx_ref[...] = dza_ref[...] + dx
            part_r = jnp.concatenate(
                [jnp.sum(dh, axis=0, keepdims=True), jnp.sum(dh * xhat, axis=0, keepdims=True), jnp.zeros((6, d), F32)], axis=0
            )

            @pl.when(i == 0)
            def _():
                red_ref[...] = part_r

            @pl.when(i > 0)
            def _():
                red_ref[...] += part_r

    row = pl.BlockSpec((tm, d), lambda i, kk: (i, 0))
    return _pcall(
        body,
        name="in_bwd",
        grid=(s_len // tm, nk),
        out_shape=[jax.ShapeDtypeStruct((s_len, d), F32), jax.ShapeDtypeStruct((8, d), F32)],
        in_specs=[
            pl.BlockSpec((tm, tk), lambda i, kk: (i, kk)),
            pl.BlockSpec((d, tk), lambda i, kk: (0, kk)),
            row,
            pl.BlockSpec((1, d), lambda i, kk: (0, 1)),
            row,
        ],
        out_specs=[row, pl.BlockSpec((8, d), lambda i, kk: (0, 0))],
        scratch_shapes=[pltpu.VMEM((tm, d), F32)],
        compiler_params=_cp("arbitrary", "arbitrary"),
    )(dproj, w_in, x, ada, dza)


def _pad_lanes(v, n):
    return jnp.pad(v, ((0, 0), (0, n - v.shape[1])))


def kernel(x, c, w_ada, b_ada, w_in, b_f, attn_sinks, w_br_fox, w_br_swa, w_out, ln_g, ln_b, loss_target, m_w_ada, m_b_ada, m_w_in, m_b_f, m_attn_sinks, m_w_br_fox, m_w_br_swa, m_w_out, m_ln_g, m_ln_b, v_w_ada, v_b_ada, v_w_in, v_b_f, v_attn_sinks, v_w_br_fox, v_w_br_swa, v_w_out, v_ln_g, v_ln_b):
    x2, tgt = x[0], loss_target[0]
    s_len, d = x2.shape
    me = 4 * lax.axis_index("x") + 2 * lax.axis_index("y") + lax.axis_index("c")
    off_ms = OFF_MF + d
    in_pad = off_ms + d
    c_ada = w_ada.shape[2]
    c_in = w_in.shape[2]
    c_br = w_br_fox.shape[2]

    w_in_g = _all_gather(w_in[0].astype(BF16), "ag_w_in", pltpu.HBM)
    w_in_full = w_in_g.reshape(N_DEV, d, c_in).transpose(1, 0, 2).reshape(d, N_DEV * c_in)
    w_in_pad = jnp.concatenate(
        [w_in_full[:, :REAL_FLOG_END], jnp.zeros((d, FLOG_PAD - N_FLOG), BF16), w_in_full[:, REAL_FLOG_END:]], axis=1
    )
    w_bf = _all_gather(w_br_fox[0].astype(BF16), "ag_w_br_fox", pltpu.HBM)
    w_bf = w_bf.reshape(N_DEV, FOX_W, c_br).transpose(1, 0, 2).reshape(FOX_W, d)
    w_bs = _all_gather(w_br_swa[0].astype(BF16), "ag_w_br_swa", pltpu.HBM)
    w_bs = w_bs.reshape(N_DEV, SWA_W, c_br).transpose(1, 0, 2).reshape(SWA_W, d)
    w_o = _all_gather(w_out[0].astype(BF16), "ag_w_out", pltpu.HBM)

    c_all = _gather_rows(c, "ag_c")
    b_cols = lax.dynamic_slice(b_ada, (0, me * c_ada), (1, c_ada))
    ada_cols = _ada_fwd(c_all, w_ada[0], b_cols)
    ada_g = _all_gather(ada_cols, "ag_ada", pltpu.VMEM).reshape(N_DEV, N_DEV, c_ada)
    ada = lax.dynamic_index_in_dim(ada_g, me, axis=1, keepdims=False).reshape(1, N_DEV * c_ada)

    h = _ln_mod(x2, ada)
    qkv_fox = _mm_cols(h, w_in_pad, OFF_FQ, 3 * FOX_W, BF16, "proj_fox")
    flog = _mm_cols(h, w_in_pad, OFF_FLOG, FLOG_PAD, F32, "proj_flog")
    qkv_swa = _mm_cols(h, w_in_pad, OFF_SQ, SWA_W + 2 * SWA_KVW, BF16, "proj_swa")
    gates = _mm_cols(h, w_in_pad, OFF_GF, in_pad - OFF_GF, F32, "proj_gates")
    mf_blk = (OFF_MF - OFF_GF) // d

    flog_t = flog[:, :N_FLOG].T
    bf_col = b_f.reshape(FOX_H, 1)
    cum = _fox_cum(flog_t, bf_col)
    cum_row = cum.reshape(FOX_H, 1, s_len)
    o_fox, lse = _fox_fwd(qkv_fox, cum_row)
    sinks = attn_sinks.reshape(SWA_HQ)
    o_swa = _swa_fwd(qkv_swa, sinks)

    y_fox, a_fox = _branch_fwd(o_fox, gates, 0, w_bf, "branch_fox")
    y_swa, a_swa = _branch_fwd(o_swa, gates, 1, w_bs, "branch_swa")
    merged, dza, dsub, red = _out_stage(gates, mf_blk, y_fox, y_swa, w_o, x2, ada, ln_g, ln_b, tgt)
    loss = lax.psum(0.5 * red[4, 0] / d, ("x", "y", "c"))

    dmf, dms, dy_fox, dy_swa = _merge_bwd(dsub, w_o, gates, mf_blk, y_fox, y_swa)
    do_fox, dg_fox, delta = _branch_bwd(dy_fox, w_bf, o_fox, gates, 0, "branch_fox_bwd", FOX_H)
    do_swa, dg_swa = _branch_bwd(dy_swa, w_bs, o_swa, gates, 1, "branch_swa_bwd", 0)
    delta_row = delta[:, :FOX_H].T.reshape(FOX_H, 1, s_len)
    dq_f, dk_f, dv_f, dcol, drow = _fox_bwd(
        qkv_fox, cum.reshape(FOX_H, s_len, 1), lse.reshape(FOX_H, 1, s_len), delta_row, do_fox
    )
    dflog_t, dbf = _fox_gate_bwd(drow.reshape(FOX_H, s_len), dcol.reshape(FOX_H, s_len), flog_t, bf_col)
    dq_s, dk_s, dv_s, dsink = _swa_bwd(qkv_swa, sinks, do_swa)
    dflog = _pad_lanes(dflog_t.T, FLOG_PAD).astype(BF16)
    dproj = jnp.concatenate([dq_f.astype(BF16), dk_f, dv_f, dflog, dq_s, dk_s, dv_s, dg_fox, dg_swa, dmf, dms], axis=1)
    grad_x, red2 = _in_bwd(dproj, w_in_pad, x2, ada, dza)

    g_w_in = _mm_acc(h.T, dproj, "grad_w_in")
    g_w_in = jnp.concatenate([g_w_in[:, :REAL_FLOG_END], g_w_in[:, OFF_SQ:]], axis=1)
    g_w_bf = _mm_acc(a_fox.T, dy_fox, "grad_w_br_fox")
    g_w_bs = _mm_acc(a_swa.T, dy_swa, "grad_w_br_swa")
    g_w_o = _mm_acc(merged.T, dsub, "grad_w_out")

    r_in, r_bf, r_bs, r_o = _grad_exchange(
        [
            g_w_in.reshape(d, N_DEV, c_in).transpose(1, 0, 2),
            g_w_bf.reshape(FOX_W, N_DEV, c_br).transpose(1, 0, 2),
            g_w_bs.reshape(SWA_W, N_DEV, c_br).transpose(1, 0, 2),
            g_w_o.reshape(N_DEV, d // N_DEV, d),
        ],
        "grad_exchange",
    )
    out_w_in = _sum_adam(r_in, w_in[0], m_w_in[0], v_w_in[0], "adam_w_in")
    out_w_bf = _sum_adam(r_bf, w_br_fox[0], m_w_br_fox[0], v_w_br_fox[0], "adam_w_br_fox")
    out_w_bs = _sum_adam(r_bs, w_br_swa[0], m_w_br_swa[0], v_w_br_swa[0], "adam_w_br_swa")
    out_w_o = _sum_adam(r_o, w_out[0], m_w_out[0], v_w_out[0], "adam_w_out")

    packed = jnp.concatenate([red2[0:1], red2[1:2], red[0:1], _pad_lanes(dbf[:, 0].reshape(1, FOX_H), 128), dsink, red[1:2], red[2:3]], axis=1)
    gathered = _gather_rows(packed, "ag_small")
    pack = lambda a, b, cc, dd, e: jnp.concatenate([a, _pad_lanes(b, 128), _pad_lanes(cc, 128), dd, e], axis=1)
    small = _small_adam(
        gathered,
        pack(b_ada, b_f, attn_sinks, ln_g, ln_b),
        pack(m_b_ada, m_b_f, m_attn_sinks, m_ln_g, m_ln_b),
        pack(v_b_ada, v_b_f, v_attn_sinks, v_ln_g, v_ln_b),
    )
    dada_cols = lax.dynamic_slice(gathered, (0, me * c_ada), (N_DEV, c_ada))
    out_w_ada = _wada_adam(c_all.T, dada_cols, w_ada[0], m_w_ada[0], v_w_ada[0])

    o1, o2, o3 = 3 * d, 3 * d + 128, 3 * d + 256

    def unpack(p):
        return p[:, :o1], p[:, o1 : o1 + FOX_H], p[:, o2 : o2 + SWA_HQ], p[:, o3 : o3 + d], p[:, o3 + d : o3 + 2 * d]

    kinds = []
    for k in range(4):
        b_ada_k, b_f_k, sinks_k, ln_g_k, ln_b_k = unpack(small[k])
        kinds.append(
            [out_w_ada[k][None], b_ada_k, out_w_in[k][None], b_f_k, sinks_k, out_w_bf[k][None], out_w_bs[k][None], out_w_o[k][None], ln_g_k, ln_b_k]
        )
    return (loss, grad_x[None], *kinds[0], *kinds[1], *kinds[2], *kinds[3])
```

```python
import numpy as np
import jax
import jax.numpy as jnp
from jax import lax
from jax.experimental import pallas as pl
from jax.experimental.pallas import tpu as pltpu

F32 = jnp.float32
BF16 = jnp.bfloat16
N_DEV = 8
MESH = pl.DeviceIdType.MESH

FOX_H, FOX_DH, FOX_W = 8, 128, 1024
SWA_HQ, SWA_HKV, SWA_DH, SWA_G = 16, 4, 64, 4
SWA_W, SWA_KVW, WINDOW = 1024, 256, 128
LN_EPS = 1e-5
NEG = -1e30
DEPTH = 1
ALPHA = (2.0 * DEPTH) ** 0.25
FOX_SCALE = FOX_DH ** -0.5
SWA_SCALE = SWA_DH ** -0.5
SLOPES = [2.0 ** (-8.0 * (h + 1.0) / SWA_HQ) for h in range(SWA_HQ)]

ADAM_LR, ADAM_B1, ADAM_B2, ADAM_EPS, ADAM_WD, ADAM_STEP = 0.001, 0.9, 0.999, 1e-08, 0.01, 10

N_FLOG = 8
FLOG_PAD = 512
OFF_FQ, OFF_FK, OFF_FV, OFF_FLOG = 0, 1024, 2048, 3072
OFF_SQ = OFF_FLOG + FLOG_PAD
OFF_SK = OFF_SQ + SWA_W
OFF_SV = OFF_SK + SWA_KVW
OFF_GF = OFF_SV + SWA_KVW
OFF_GS = OFF_GF + FOX_W
OFF_MF = OFF_GS + SWA_W
REAL_FLOG_END = OFF_FLOG + N_FLOG

ATT_BLK = 512
VMEM_LIMIT = 52 * 1024 * 1024


def _pcall(body, **kw):
    return pl.pallas_call(body, **kw)


def _cp(*sem):
    return pltpu.CompilerParams(dimension_semantics=sem, vmem_limit_bytes=VMEM_LIMIT)


def _sigmoid(x):
    return 1.0 / (1.0 + jnp.exp(-x))


def _all_gather(x, name, space):
    m_per, n = x.shape

    def body(x_ref, out_ref, send_sems, recv_sems, local_sem):
        mx, my, mc = lax.axis_index("x"), lax.axis_index("y"), lax.axis_index("c")
        me, sibling = (mx, my, mc), (mx, my, 1 - mc)
        chips = [(1 - mx, my), (mx, 1 - my), (1 - mx, 1 - my)]

        def rows(px, py, pc):
            return out_ref.at[pl.ds((4 * px + 2 * py + pc) * m_per, m_per), :]

        def copy(k, block, to, src=None):
            return pltpu.make_async_remote_copy(
                src_ref=rows(*block) if src is None else src,
                dst_ref=rows(*block),
                send_sem=send_sems.at[k],
                recv_sem=recv_sems.at[k],
                device_id=to,
                device_id_type=MESH,
            )

        mine = pltpu.make_async_copy(x_ref, rows(*me), local_sem)
        mine.start()
        first = [copy(0, me, sibling, src=x_ref)]
        first += [copy(1 + j, me, (*chip, mc), src=x_ref) for j, chip in enumerate(chips)]
        for cp in first:
            cp.start()
        passed = [copy(4 + j, (*chip, mc), sibling) for j, chip in enumerate(chips)]
        for j, chip in enumerate(chips):
            copy(1 + j, (*chip, mc), me).wait_recv()
            passed[j].start()
        copy(0, sibling, me).wait_recv()
        for j, chip in enumerate(chips):
            copy(4 + j, (*chip, 1 - mc), me).wait_recv()
        for cp in first + passed:
            cp.wait_send()
        mine.wait()

    return _pcall(
        body,
        name=name,
        out_shape=jax.ShapeDtypeStruct((N_DEV * m_per, n), x.dtype),
        in_specs=[pl.BlockSpec(memory_space=space)],
        out_specs=pl.BlockSpec(memory_space=space),
        scratch_shapes=[pltpu.SemaphoreType.DMA((7,)), pltpu.SemaphoreType.DMA((7,)), pltpu.SemaphoreType.DMA],
    )(x)


def _grad_exchange(blocks, name):
    n = len(blocks)

    def body(*refs):
        ins, outs = refs[:n], refs[n : 2 * n]
        send_sems, recv_sems, local_sems = refs[2 * n :]
        mx, my, mc = lax.axis_index("x"), lax.axis_index("y"), lax.axis_index("c")
        me = 4 * mx + 2 * my + mc
        copies, local = [], []
        for a in range(n):
            own = pltpu.make_async_copy(ins[a].at[me], outs[a].at[0], local_sems.at[a])
            own.start()
            local.append(own)
            for d in range(1, N_DEV):
                px = 1 - mx if (d >> 2) & 1 else mx
                py = 1 - my if (d >> 1) & 1 else my
                pc = 1 - mc if d & 1 else mc
                cp = pltpu.make_async_remote_copy(
                    src_ref=ins[a].at[4 * px + 2 * py + pc],
                    dst_ref=outs[a].at[d],
                    send_sem=send_sems.at[a * 7 + d - 1],
                    recv_sem=recv_sems.at[a * 7 + d - 1],
                    device_id=(px, py, pc),
                    device_id_type=MESH,
                )
                cp.start()
                copies.append(cp)
        for cp in copies:
            cp.wait_recv()
        for cp in copies:
            cp.wait_send()
        for own in local:
            own.wait()

    hbm = pl.BlockSpec(memory_space=pltpu.HBM)
    return _pcall(
        body,
        name=name,
        out_shape=[jax.ShapeDtypeStruct(b.shape, b.dtype) for b in blocks],
        in_specs=[hbm] * n,
        out_specs=[hbm] * n,
        scratch_shapes=[
            pltpu.SemaphoreType.DMA((7 * n,)),
            pltpu.SemaphoreType.DMA((7 * n,)),
            pltpu.SemaphoreType.DMA((n,)),
        ],
    )(*blocks)


def _gather_rows(v, name):
    n = v.shape[1]
    g = _all_gather(jnp.broadcast_to(v, (8, n)), name, pltpu.VMEM)
    return g.reshape(N_DEV, 8, n)[:, 0, :]


def _adamw(w, g, m, v):
    m = ADAM_B1 * m + (1.0 - ADAM_B1) * g
    v = ADAM_B2 * v + (1.0 - ADAM_B2) * (g * g)
    m_hat = m / (1.0 - ADAM_B1**ADAM_STEP)
    v_hat = v / (1.0 - ADAM_B2**ADAM_STEP)
    delta = -ADAM_LR * (m_hat / (jnp.sqrt(v_hat) + ADAM_EPS) + ADAM_WD * w)
    return delta, m, v


def _sum_adam(recv, w, m, v, name):
    _, r_tot, c = recv.shape
    c_pad = -(-c // 128) * 128
    tr = r_tot
    while 8 * tr * c_pad * 4 > 6 * 1024 * 1024 and tr % 32 == 0:
        tr //= 2

    def body(r_ref, w_ref, m_ref, v_ref, g_ref, d_ref, nm_ref, nv_ref):
        g = r_ref[0].astype(F32)
        for k in range(1, N_DEV):
            g = g + r_ref[k].astype(F32)
        d, nm, nv = _adamw(w_ref[...], g, m_ref[...], v_ref[...])
        g_ref[...] = g
        d_ref[...] = d
        nm_ref[...] = nm
        nv_ref[...] = nv

    blk = pl.BlockSpec((tr, c), lambda i: (i, 0))
    return _pcall(
        body,
        name=name,
        grid=(r_tot // tr,),
        out_shape=[jax.ShapeDtypeStruct((r_tot, c), F32)] * 4,
        in_specs=[pl.BlockSpec((N_DEV, tr, c), lambda i: (0, i, 0)), blk, blk, blk],
        out_specs=[blk] * 4,
        compiler_params=_cp("parallel"),
    )(recv, w, m, v)


def _wada_adam(c_t, dada_cols, w, m, v):
    d_model, c = w.shape
    tr = min(256, d_model)

    def body(ct_ref, da_ref, w_ref, m_ref, v_ref, g_ref, d_ref, nm_ref, nv_ref):
        g = jnp.dot(ct_ref[...].astype(BF16), da_ref[...].astype(BF16), preferred_element_type=F32)
        d, nm, nv = _adamw(w_ref[...], g, m_ref[...], v_ref[...])
        g_ref[...] = g
        d_ref[...] = d
        nm_ref[...] = nm
        nv_ref[...] = nv

    blk = pl.BlockSpec((tr, c), lambda i: (i, 0))
    return _pcall(
        body,
        name="wada_adam",
        grid=(d_model // tr,),
        out_shape=[jax.ShapeDtypeStruct((d_model, c), F32)] * 4,
        in_specs=[pl.BlockSpec((tr, N_DEV), lambda i: (i, 0)), pl.BlockSpec((N_DEV, c), lambda i: (0, 0)), blk, blk, blk],
        out_specs=[blk] * 4,
        compiler_params=_cp("parallel"),
    )(c_t, dada_cols, w, m, v)


def _small_adam(gathered, w, m, v):
    p = w.shape[1]

    def body(a_ref, w_ref, m_ref, v_ref, g_ref, d_ref, nm_ref, nv_ref):
        g = a_ref[0:1, :]
        for k in range(1, N_DEV):
            g = g + a_ref[k : k + 1, :]
        d, nm, nv = _adamw(w_ref[...], g, m_ref[...], v_ref[...])
        g_ref[...] = g
        d_ref[...] = d
        nm_ref[...] = nm
        nv_ref[...] = nv

    return _pcall(
        body,
        name="small_adam",
        out_shape=[jax.ShapeDtypeStruct((1, p), F32)] * 4,
    )(gathered, w, m, v)


def _ada_fwd(c_all, w_ada, b_cols):
    c = w_ada.shape[1]

    def body(c_ref, w_ref, b_ref, o_ref):
        o_ref[...] = jnp.dot(c_ref[...].astype(BF16), w_ref[...].astype(BF16), preferred_element_type=F32) + b_ref[...]

    return _pcall(
        body,
        name="ada_fwd",
        out_shape=jax.ShapeDtypeStruct((N_DEV, c), F32),
        compiler_params=_cp(),
    )(c_all, w_ada, b_cols)


def _ln_mod(x, ada):
    s_len, d = x.shape
    tm = min(512, s_len)

    def body(x_ref, sh_ref, sc_ref, h_ref):
        xv = x_ref[...]
        mu = jnp.mean(xv, axis=-1, keepdims=True)
        xc = xv - mu
        var = jnp.mean(xc * xc, axis=-1, keepdims=True)
        xhat = xc * lax.rsqrt(var + LN_EPS)
        h_ref[...] = (xhat * (1.0 + sc_ref[...]) + sh_ref[...]).astype(BF16)

    return _pcall(
        body,
        name="ln_mod",
        grid=(s_len // tm,),
        out_shape=jax.ShapeDtypeStruct((s_len, d), BF16),
        in_specs=[
            pl.BlockSpec((tm, d), lambda i: (i, 0)),
            pl.BlockSpec((1, d), lambda i: (0, 0)),
            pl.BlockSpec((1, d), lambda i: (0, 1)),
        ],
        out_specs=pl.BlockSpec((tm, d), lambda i: (i, 0)),
        compiler_params=_cp("parallel"),
    )(x, ada, ada)


def _mm_cols(a, b, col_off, n_cols, out_dtype, name):
    m, k = a.shape
    tm, tn = min(1024, m), 512
    off = col_off // tn

    def body(a_ref, b_ref, o_ref):
        o_ref[...] = jnp.dot(a_ref[...], b_ref[...], preferred_element_type=F32).astype(out_dtype)

    return _pcall(
        body,
        name=name,
        grid=(m // tm, n_cols // tn),
        out_shape=jax.ShapeDtypeStruct((m, n_cols), out_dtype),
        in_specs=[pl.BlockSpec((tm, k), lambda i, j: (i, 0)), pl.BlockSpec((k, tn), lambda i, j: (0, off + j))],
        out_specs=pl.BlockSpec((tm, tn), lambda i, j: (i, j)),
        compiler_params=_cp("parallel", "parallel"),
    )(a, b)


def _mm_acc(a, b, name):
    m, k = a.shape
    n = b.shape[1]
    tm, tn, tk = min(1024, m), min(1024, n), min(1024, k)
    nk = k // tk

    def body(a_ref, b_ref, o_ref, acc_s):
        kk = pl.program_id(2)
        part = jnp.dot(a_ref[...], b_ref[...], preferred_element_type=F32)

        @pl.when(kk == 0)
        def _():
            acc_s[...] = part

        @pl.when(kk > 0)
        def _():
            acc_s[...] += part

        @pl.when(kk == nk - 1)
        def _():
            o_ref[...] = acc_s[...].astype(BF16)

    return _pcall(
        body,
        name=name,
        grid=(m // tm, n // tn, nk),
        out_shape=jax.ShapeDtypeStruct((m, n), BF16),
        in_specs=[pl.BlockSpec((tm, tk), lambda i, j, kk: (i, kk)), pl.BlockSpec((tk, tn), lambda i, j, kk: (kk, j))],
        out_specs=pl.BlockSpec((tm, tn), lambda i, j, kk: (i, j)),
        scratch_shapes=[pltpu.VMEM((tm, tn), F32)],
        compiler_params=_cp("parallel", "parallel", "arbitrary"),
    )(a, b)


def _split3(a):
    hi = a.astype(BF16)
    r1 = a - hi.astype(F32)
    mid = r1.astype(BF16)
    lo = (r1 - mid.astype(F32)).astype(BF16)
    return hi, mid, lo


def _dot_ones(a, tri):
    return sum(jnp.dot(t, tri, preferred_element_type=F32) for t in _split3(a))


def _log_sigmoid(x):
    return jnp.minimum(x, 0.0) - jnp.log1p(jnp.exp(-jnp.abs(x)))


def _fox_cum(flog_t, bf_col):
    s_len = flog_t.shape[1]

    def body(fl_ref, bf_ref, cum_ref):
        r = lax.broadcasted_iota(jnp.int32, (128, 128), 0)
        c = lax.broadcasted_iota(jnp.int32, (128, 128), 1)
        upper = (r <= c).astype(BF16)

        def step(t, carry):
            sl = pl.ds(pl.multiple_of(t * 128, 128), 128)
            lf = _log_sigmoid(fl_ref[:, sl] + bf_ref[...])
            cs = _dot_ones(lf, upper) + carry
            cum_ref[:, sl] = cs
            return cs[:, 127:128]

        lax.fori_loop(0, s_len // 128, step, jnp.zeros((FOX_H, 1), F32))

    return _pcall(body, name="fox_cum", out_shape=jax.ShapeDtypeStruct((FOX_H, s_len), F32))(flog_t, bf_col)


def _fox_gate_bwd(drow, dcol, flog_t, bf_col):
    s_len = flog_t.shape[1]
    n = s_len // 128

    def body(dr_ref, dc_ref, fl_ref, bf_ref, dfl_ref, dbf_ref):
        r = lax.broadcasted_iota(jnp.int32, (128, 128), 0)
        c = lax.broadcasted_iota(jnp.int32, (128, 128), 1)
        lower = (r >= c).astype(BF16)

        def step(t, carry):
            run, tot = carry
            sl = pl.ds(pl.multiple_of((n - 1 - t) * 128, 128), 128)
            rc = _dot_ones(dr_ref[:, sl] - dc_ref[:, sl], lower) + run
            dfl = rc * _sigmoid(-(fl_ref[:, sl] + bf_ref[...]))
            dfl_ref[:, sl] = dfl
            return rc[:, 0:1], tot + jnp.sum(dfl, axis=1, keepdims=True)

        zero = jnp.zeros((FOX_H, 1), F32)
        _, tot = lax.fori_loop(0, n, step, (zero, zero))
        dbf_ref[...] = jnp.broadcast_to(tot, (FOX_H, 128))

    return _pcall(
        body,
        name="fox_gate_bwd",
        out_shape=[jax.ShapeDtypeStruct((FOX_H, s_len), F32), jax.ShapeDtypeStruct((FOX_H, 128), F32)],
    )(drow, dcol, flog_t, bf_col)


def _diag_mask(blk, transposed=False):
    r = lax.broadcasted_iota(jnp.int32, (blk, blk), 0)
    c = lax.broadcasted_iota(jnp.int32, (blk, blk), 1)
    return c >= r if transposed else r >= c


_NT = (((1,), (1,)), ((), ()))
_TN = (((0,), (0,)), ((), ()))


def _fox_fwd(qkv, cum_row):
    s_len = qkv.shape[0]
    blk = min(ATT_BLK, s_len)
    nb = s_len // blk
    log2e = 1.4426950408889634

    def body(q_ref, k_ref, v_ref, c_ref, o_ref, lse_ref, mx_s, acc_s):
        i = pl.program_id(1)

        def logits(j, masked):
            cols = pl.ds(pl.multiple_of(j * blk, blk), blk)
            u = lax.dot_general(q_ref[...], k_ref[cols, :], _NT, preferred_element_type=F32) - c_ref[:, cols] * (1.0 / FOX_SCALE)
            if masked:
                u = jnp.where(_diag_mask(blk), u, NEG)
            return u, cols

        def lane_max(j, masked):
            u, _ = logits(j, masked)
            part = u[:, 0:128]
            for t in range(1, blk // 128):
                part = jnp.maximum(part, u[:, t * 128 : (t + 1) * 128])
            mx_s[...] = jnp.maximum(mx_s[...], part)

        mx_s[...] = jnp.full(mx_s.shape, NEG, F32)
        lax.fori_loop(0, i, lambda j, c: (lane_max(j, False), c)[1], 0)
        lane_max(i, True)
        m = jnp.max(mx_s[...], axis=1, keepdims=True)

        ones_col = (lax.broadcasted_iota(jnp.int32, (blk, 128), 1) == 0).astype(BF16)

        def weigh(j, masked):
            u, cols = logits(j, masked)
            p = jnp.exp2((u - m) * (FOX_SCALE * log2e))
            v1 = jnp.concatenate([v_ref[cols, :], ones_col], axis=1)
            acc_s[...] += jnp.dot(p.astype(BF16), v1, preferred_element_type=F32)

        acc_s[...] = jnp.zeros(acc_s.shape, F32)
        lax.fori_loop(0, i, lambda j, c: (weigh(j, False), c)[1], 0)
        weigh(i, True)
        l = acc_s[:, FOX_DH : FOX_DH + 1]
        o_ref[...] = acc_s[:, :FOX_DH] / l
        lse_ref[...] = m * FOX_SCALE + jnp.log(l)

    return _pcall(
        body,
        name="fox_fwd",
        grid=(FOX_H, nb),
        out_shape=[jax.ShapeDtypeStruct((s_len, FOX_W), F32), jax.ShapeDtypeStruct((FOX_H, s_len, 1), F32)],
        in_specs=[
            pl.BlockSpec((blk, FOX_DH), lambda h, i: (i, h)),
            pl.BlockSpec((s_len, FOX_DH), lambda h, i: (0, FOX_H + h)),
            pl.BlockSpec((s_len, FOX_DH), lambda h, i: (0, 2 * FOX_H + h)),
            pl.BlockSpec((None, 1, s_len), lambda h, i: (h, 0, 0)),
        ],
        out_specs=[
            pl.BlockSpec((blk, FOX_DH), lambda h, i: (i, h)),
            pl.BlockSpec((None, blk, 1), lambda h, i: (h, i, 0)),
        ],
        scratch_shapes=[pltpu.VMEM((blk, 128), F32), pltpu.VMEM((blk, 2 * FOX_DH), F32)],
        compiler_params=_cp("parallel", "arbitrary"),
    )(qkv, qkv, qkv, cum_row)


def _fox_bwd(qkv, cum_col, lse_row, delta_row, do):
    s_len = qkv.shape[0]
    blk = min(ATT_BLK, s_len)
    nb = s_len // blk

    def body(q_ref, k_ref, v_ref, c_ref, lse_ref, dl_ref, do_ref, dq_ref, dk_ref, dv_ref, dc_ref, dr_ref, dk_s, dv_s, dc_s, cb_s):
        j = pl.program_id(1)

        @pl.when(j == 0)
        def _():
            dq_ref[...] = jnp.zeros(dq_ref.shape, F32)
            dr_ref[...] = jnp.zeros(dr_ref.shape, F32)

        dk_s[...] = jnp.zeros(dk_s.shape, F32)
        dv_s[...] = jnp.zeros(dv_s.shape, F32)
        dc_s[...] = jnp.zeros(dc_s.shape, F32)
        cb_s[...] = jnp.broadcast_to(c_ref[...], cb_s.shape)

        def tile(i, diag):
            rows = pl.ds(pl.multiple_of(i * blk, blk), blk)
            q, dob = q_ref[rows, :], do_ref[rows, :]
            k, v = k_ref[...], v_ref[...]
            s_t = lax.dot_general(k, q, _NT, preferred_element_type=F32) * FOX_SCALE - cb_s[...]
            p_t = jnp.exp(s_t - lse_ref[:, rows])
            if diag:
                p_t = jnp.where(_diag_mask(blk, transposed=True), p_t, 0.0)
            dp_t = lax.dot_general(v, dob, _NT, preferred_element_type=F32)
            ds_t = p_t * (dp_t - dl_ref[:, rows])
            dsb = ds_t.astype(BF16)
            dv_s[...] += jnp.dot(p_t.astype(BF16), dob, preferred_element_type=F32)
            dk_s[...] += jnp.dot(dsb, q, preferred_element_type=F32)
            dq_c = lax.dot_general(dsb, k, _TN, preferred_element_type=F32)
            part = ds_t[:, 0:128]
            for t in range(1, blk // 128):
                part = part + ds_t[:, t * 128 : (t + 1) * 128]
            dc_s[...] += part
            dr_ref[:, rows] += jnp.sum(ds_t, axis=0, keepdims=True)
            if diag:
                dq_ref[rows, :] = (dq_ref[rows, :] + dq_c) * FOX_SCALE
            else:
                dq_ref[rows, :] += dq_c

        tile(j, True)

        def below(i, carry):
            tile(i, False)
            return carry

        lax.fori_loop(j + 1, nb, below, 0)
        dk_ref[...] = (dk_s[...] * FOX_SCALE).astype(BF16)
        dv_ref[...] = dv_s[...].astype(BF16)
        dc_ref[...] = jnp.sum(dc_s[...], axis=1, keepdims=True)

    head = lambda h, j: (0, h)
    row = pl.BlockSpec((None, 1, s_len), lambda h, j: (h, 0, 0))
    return _pcall(
        body,
        name="fox_bwd",
        grid=(FOX_H, nb),
        out_shape=[
            jax.ShapeDtypeStruct((s_len, FOX_W), F32),
            jax.ShapeDtypeStruct((s_len, FOX_W), BF16),
            jax.ShapeDtypeStruct((s_len, FOX_W), BF16),
            jax.ShapeDtypeStruct((FOX_H, s_len, 1), F32),
            jax.ShapeDtypeStruct((FOX_H, 1, s_len), F32),
        ],
        in_specs=[
            pl.BlockSpec((s_len, FOX_DH), head),
            pl.BlockSpec((blk, FOX_DH), lambda h, j: (j, FOX_H + h)),
            pl.BlockSpec((blk, FOX_DH), lambda h, j: (j, 2 * FOX_H + h)),
            pl.BlockSpec((None, blk, 1), lambda h, j: (h, j, 0)),
            row,
            row,
            pl.BlockSpec((s_len, FOX_DH), head),
        ],
        out_specs=[
            pl.BlockSpec((s_len, FOX_DH), head),
            pl.BlockSpec((blk, FOX_DH), lambda h, j: (j, h)),
            pl.BlockSpec((blk, FOX_DH), lambda h, j: (j, h)),
            pl.BlockSpec((None, blk, 1), lambda h, j: (h, j, 0)),
            row,
        ],
        scratch_shapes=[
            pltpu.VMEM((blk, FOX_DH), F32),
            pltpu.VMEM((blk, FOX_DH), F32),
            pltpu.VMEM((blk, 128), F32),
            pltpu.VMEM((blk, blk), F32),
        ],
        compiler_params=_cp("parallel", "arbitrary"),
    )(qkv, qkv, qkv, cum_col, lse_row, delta_row, do)


def _swa_probs(i, q_ref, kk, sinks_ref, h):
    r = lax.broadcasted_iota(jnp.int32, (WINDOW, 2 * WINDOW), 0)
    c = lax.broadcasted_iota(jnp.int32, (WINDOW, 2 * WINDOW), 1)
    dist = r - c + WINDOW
    valid = (dist >= 0) & (dist < WINDOW) & ((c >= WINDOW) | (i > 0))
    g = h // SWA_G
    q = q_ref[:, h * SWA_DH : (h + 1) * SWA_DH]
    k = kk[:, g * SWA_DH : (g + 1) * SWA_DH]
    s = lax.dot_general(q, k, _NT, preferred_element_type=F32) * SWA_SCALE - SLOPES[h] * dist.astype(F32)
    s = jnp.where(valid, s, NEG)
    sink = sinks_ref[h]
    m = jnp.maximum(jnp.max(s, axis=1, keepdims=True), sink)
    e = jnp.exp(s - m)
    e_sink = jnp.exp(sink - m)
    inv = 1.0 / (jnp.sum(e, axis=1, keepdims=True) + e_sink)
    return q, k, e * inv, e_sink * inv


def _swa_specs(col_q, col_k, col_v, rev, nb):
    def blk(t):
        return nb - 1 - t if rev else t

    return [
        pl.BlockSpec((WINDOW, SWA_W), lambda t: (blk(t), col_q)),
        pl.BlockSpec((WINDOW, SWA_KVW), lambda t: (jnp.maximum(blk(t) - 1, 0), col_k)),
        pl.BlockSpec((WINDOW, SWA_KVW), lambda t: (blk(t), col_k)),
        pl.BlockSpec((WINDOW, SWA_KVW), lambda t: (jnp.maximum(blk(t) - 1, 0), col_v)),
        pl.BlockSpec((WINDOW, SWA_KVW), lambda t: (blk(t), col_v)),
    ]


def _swa_fwd(qkv, sinks):
    s_len = qkv.shape[0]
    nb = s_len // WINDOW

    def body(q_ref, kp_ref, kc_ref, vp_ref, vc_ref, sinks_ref, o_ref):
        i = pl.program_id(0)
        kk = jnp.concatenate([kp_ref[...], kc_ref[...]], axis=0)
        vv = jnp.concatenate([vp_ref[...], vc_ref[...]], axis=0)
        for h in range(SWA_HQ):
            g = h // SWA_G
            _, _, p, _ = _swa_probs(i, q_ref, kk, sinks_ref, h)
            v = vv[:, g * SWA_DH : (g + 1) * SWA_DH]
            o_ref[:, h * SWA_DH : (h + 1) * SWA_DH] = jnp.dot(p.astype(BF16), v, preferred_element_type=F32)

    return _pcall(
        body,
        name="swa_fwd",
        grid=(nb,),
        out_shape=jax.ShapeDtypeStruct((s_len, SWA_W), F32),
        in_specs=_swa_specs(0, 4, 5, False, nb) + [pl.BlockSpec(memory_space=pltpu.SMEM)],
        out_specs=pl.BlockSpec((WINDOW, SWA_W), lambda t: (t, 0)),
        compiler_params=_cp("parallel"),
    )(qkv, qkv, qkv, qkv, qkv, sinks)


def _swa_bwd(qkv, sinks, do):
    s_len = qkv.shape[0]
    nb = s_len // WINDOW

    def body(q_ref, kp_ref, kc_ref, vp_ref, vc_ref, sinks_ref, do_ref, dq_ref, dk_ref, dv_ref, dsink_ref, ck_s, cv_s, dkk_s, dvv_s):
        t = pl.program_id(0)
        i = nb - 1 - t

        @pl.when(t == 0)
        def _():
            ck_s[...] = jnp.zeros(ck_s.shape, F32)
            cv_s[...] = jnp.zeros(cv_s.shape, F32)
            dsink_ref[...] = jnp.zeros(dsink_ref.shape, F32)

        kk = jnp.concatenate([kp_ref[...], kc_ref[...]], axis=0)
        vv = jnp.concatenate([vp_ref[...], vc_ref[...]], axis=0)
        lane = lax.broadcasted_iota(jnp.int32, (1, 128), 1)
        dsink = jnp.zeros((1, 128), F32)
        for g in range(SWA_HKV):
            cols = slice(g * SWA_DH, (g + 1) * SWA_DH)
            v = vv[:, cols]
            dkk = jnp.zeros((2 * WINDOW, SWA_DH), F32)
            dvv = jnp.zeros((2 * WINDOW, SWA_DH), F32)
            for h in range(g * SWA_G, (g + 1) * SWA_G):
                hc = slice(h * SWA_DH, (h + 1) * SWA_DH)
                q, k, p, p_sink = _swa_probs(i, q_ref, kk, sinks_ref, h)
                dob = do_ref[:, hc]
                dp = lax.dot_general(dob, v, _NT, preferred_element_type=F32)
                delta = jnp.sum(p * dp, axis=1, keepdims=True)
                dsb = (p * (dp - delta)).astype(BF16)
                dq_ref[:, hc] = (jnp.dot(dsb, k, preferred_element_type=F32) * SWA_SCALE).astype(BF16)
                dkk = dkk + lax.dot_general(dsb, q, _TN, preferred_element_type=F32)
                dvv = dvv + lax.dot_general(p.astype(BF16), dob, _TN, preferred_element_type=F32)
                dsink = dsink + jnp.where(lane == h, -jnp.sum(p_sink * delta, axis=0, keepdims=True), 0.0)
            dkk_s[:, cols] = dkk * SWA_SCALE
            dvv_s[:, cols] = dvv
        dk_ref[...] = (dkk_s[WINDOW:, :] + ck_s[...]).astype(BF16)
        dv_ref[...] = (dvv_s[WINDOW:, :] + cv_s[...]).astype(BF16)
        ck_s[...] = dkk_s[:WINDOW, :]
        cv_s[...] = dvv_s[:WINDOW, :]
        dsink_ref[...] += dsink

    row = lambda t: (nb - 1 - t, 0)
    return _pcall(
        body,
        name="swa_bwd",
        grid=(nb,),
        out_shape=[
            jax.ShapeDtypeStruct((s_len, SWA_W), BF16),
            jax.ShapeDtypeStruct((s_len, SWA_KVW), BF16),
            jax.ShapeDtypeStruct((s_len, SWA_KVW), BF16),
            jax.ShapeDtypeStruct((1, 128), F32),
        ],
        in_specs=_swa_specs(0, 4, 5, True, nb)
        + [pl.BlockSpec(memory_space=pltpu.SMEM), pl.BlockSpec((WINDOW, SWA_W), row)],
        out_specs=[
            pl.BlockSpec((WINDOW, SWA_W), row),
            pl.BlockSpec((WINDOW, SWA_KVW), row),
            pl.BlockSpec((WINDOW, SWA_KVW), row),
            pl.BlockSpec((1, 128), lambda t: (0, 0)),
        ],
        scratch_shapes=[
            pltpu.VMEM((WINDOW, SWA_KVW), F32),
            pltpu.VMEM((WINDOW, SWA_KVW), F32),
            pltpu.VMEM((2 * WINDOW, SWA_KVW), F32),
            pltpu.VMEM((2 * WINDOW, SWA_KVW), F32),
        ],
        compiler_params=_cp("arbitrary"),
    )(qkv, qkv, qkv, qkv, qkv, sinks, do)


def _branch_fwd(o, gates, g_blk, w_b, name):
    s_len, wd = o.shape
    d = w_b.shape[1]
    tm = min(512, s_len)

    def body(o_ref, g_ref, w_ref, y_ref, a_ref):
        g = g_ref[...]
        a = (o_ref[...] * (g * _sigmoid(g))).astype(BF16)
        a_ref[...] = a
        y_ref[...] = jnp.dot(a, w_ref[...], preferred_element_type=F32)

    return _pcall(
        body,
        name=name,
        grid=(s_len // tm,),
        out_shape=[jax.ShapeDtypeStruct((s_len, d), F32), jax.ShapeDtypeStruct((s_len, wd), BF16)],
        in_specs=[
            pl.BlockSpec((tm, wd), lambda i: (i, 0)),
            pl.BlockSpec((tm, wd), lambda i: (i, g_blk)),
            pl.BlockSpec((wd, d), lambda i: (0, 0)),
        ],
        out_specs=[pl.BlockSpec((tm, d), lambda i: (i, 0)), pl.BlockSpec((tm, wd), lambda i: (i, 0))],
        compiler_params=_cp("parallel"),
    )(o, gates, w_b)


def _out_stage(gates, mf_blk, y_fox, y_swa, w_out, x, ada, ln_g, ln_b, target):
    s_len, d = x.shape
    tm = min(128, s_len)
    n_steps = s_len // tm

    def body(mf_ref, ms_ref, yf_ref, ys_ref, w_ref, x_ref, gate_ref, lg_ref, lb_ref, t_ref, mg_ref, dza_ref, dsub_ref, red_ref):
        i = pl.program_id(0)
        merged = _sigmoid(mf_ref[...]) * yf_ref[...] + _sigmoid(ms_ref[...]) * ys_ref[...]
        mb = merged.astype(BF16)
        mg_ref[...] = mb
        sub = jnp.dot(mb, w_ref[...], preferred_element_type=F32)
        gate = gate_ref[...]
        z = ALPHA * x_ref[...] + gate * sub
        mu = jnp.mean(z, axis=-1, keepdims=True)
        zc = z - mu
        var = jnp.mean(zc * zc, axis=-1, keepdims=True)
        rstd = lax.rsqrt(var + LN_EPS)
        zhat = zc * rstd
        err = zhat * lg_ref[...] + lb_ref[...] - t_ref[...]
        dout = err * (1.0 / d)
        dzhat = dout * lg_ref[...]
        dz = rstd * (dzhat - jnp.mean(dzhat, axis=-1, keepdims=True) - zhat * jnp.mean(dzhat * zhat, axis=-1, keepdims=True))
        dza_ref[...] = ALPHA * dz
        dsub_ref[...] = (gate * dz).astype(BF16)
        part = jnp.concatenate(
            [
                jnp.sum(dz * sub, axis=0, keepdims=True),
                jnp.sum(dout * zhat, axis=0, keepdims=True),
                jnp.sum(dout, axis=0, keepdims=True),
                jnp.sum(err * err, axis=0, keepdims=True),
                jnp.zeros((4, d), F32),
            ],
            axis=0,
        )

        @pl.when(i == 0)
        def _():
            red_ref[...] = part

        @pl.when(i > 0)
        def _():
            red_ref[...] += part

        @pl.when(i == n_steps - 1)
        def _():
            red_ref[4:5, :] = jnp.broadcast_to(jnp.sum(red_ref[3:4, :], axis=1, keepdims=True), (1, d))

    row = pl.BlockSpec((tm, d), lambda i: (i, 0))
    vec = pl.BlockSpec((1, d), lambda i: (0, 0))
    return _pcall(
        body,
        name="out_stage",
        grid=(n_steps,),
        out_shape=[
            jax.ShapeDtypeStruct((s_len, d), BF16),
            jax.ShapeDtypeStruct((s_len, d), F32),
            jax.ShapeDtypeStruct((s_len, d), BF16),
            jax.ShapeDtypeStruct((8, d), F32),
        ],
        in_specs=[
            pl.BlockSpec((tm, d), lambda i: (i, mf_blk)),
            pl.BlockSpec((tm, d), lambda i: (i, mf_blk + 1)),
            row,
            row,
            pl.BlockSpec((d, d), lambda i: (0, 0)),
            row,
            pl.BlockSpec((1, d), lambda i: (0, 2)),
            vec,
            vec,
            row,
        ],
        out_specs=[row, row, row, pl.BlockSpec((8, d), lambda i: (0, 0))],
        compiler_params=_cp("arbitrary"),
    )(gates, gates, y_fox, y_swa, w_out, x, ada, ln_g, ln_b, target)


def _merge_bwd(dsub, w_out, gates, mf_blk, y_fox, y_swa):
    s_len, d = dsub.shape
    tm = min(128, s_len)

    def body(ds_ref, w_ref, mf_ref, ms_ref, yf_ref, ys_ref, dmf_ref, dms_ref, dyf_ref, dys_ref):
        dm = lax.dot_general(ds_ref[...], w_ref[...], _NT, preferred_element_type=F32)
        sf, ss = _sigmoid(mf_ref[...]), _sigmoid(ms_ref[...])
        dmf_ref[...] = (dm * yf_ref[...] * (sf * (1.0 - sf))).astype(BF16)
        dms_ref[...] = (dm * ys_ref[...] * (ss * (1.0 - ss))).astype(BF16)
        dyf_ref[...] = (dm * sf).astype(BF16)
        dys_ref[...] = (dm * ss).astype(BF16)

    row = pl.BlockSpec((tm, d), lambda i: (i, 0))
    return _pcall(
        body,
        name="merge_bwd",
        grid=(s_len // tm,),
        out_shape=[jax.ShapeDtypeStruct((s_len, d), BF16)] * 4,
        in_specs=[
            row,
            pl.BlockSpec((d, d), lambda i: (0, 0)),
            pl.BlockSpec((tm, d), lambda i: (i, mf_blk)),
            pl.BlockSpec((tm, d), lambda i: (i, mf_blk + 1)),
            row,
            row,
        ],
        out_specs=[row] * 4,
        compiler_params=_cp("parallel"),
    )(dsub, w_out, gates, gates, y_fox, y_swa)


def _branch_bwd(dy, w_b, o, gates, g_blk, name, n_heads):
    s_len, d = dy.shape
    wd = w_b.shape[0]
    tm = min(512, s_len)

    def body(dy_ref, w_ref, o_ref, g_ref, do_ref, dg_ref, *rest):
        da = lax.dot_general(dy_ref[...], w_ref[...], _NT, preferred_element_type=F32)
        g = g_ref[...]
        sg = _sigmoid(g)
        do = da * (g * sg)
        do_ref[...] = do.astype(BF16)
        o = o_ref[...]
        dg_ref[...] = (da * o * (sg * (1.0 + g * (1.0 - sg)))).astype(BF16)
        if n_heads:
            prod = do * o
            lane = lax.broadcasted_iota(jnp.int32, (1, 128), 1)
            delta = jnp.zeros((tm, 128), F32)
            for h in range(n_heads):
                dh = jnp.sum(prod[:, h * 128 : (h + 1) * 128], axis=1, keepdims=True)
                delta = delta + jnp.where(lane == h, dh, 0.0)
            rest[0][...] = delta

    out_shape = [jax.ShapeDtypeStruct((s_len, wd), BF16), jax.ShapeDtypeStruct((s_len, wd), BF16)]
    out_specs = [pl.BlockSpec((tm, wd), lambda i: (i, 0))] * 2
    if n_heads:
        out_shape.append(jax.ShapeDtypeStruct((s_len, 128), F32))
        out_specs.append(pl.BlockSpec((tm, 128), lambda i: (i, 0)))
    return _pcall(
        body,
        name=name,
        grid=(s_len // tm,),
        out_shape=out_shape,
        in_specs=[
            pl.BlockSpec((tm, d), lambda i: (i, 0)),
            pl.BlockSpec((wd, d), lambda i: (0, 0)),
            pl.BlockSpec((tm, wd), lambda i: (i, 0)),
            pl.BlockSpec((tm, wd), lambda i: (i, g_blk)),
        ],
        out_specs=out_specs,
        compiler_params=_cp("parallel"),
    )(dy, w_b, o, gates)


def _in_bwd(dproj, w_in, x, ada, dza):
    s_len, d = x.shape
    k_tot = dproj.shape[1]
    tm, tk = min(512, s_len), 1024
    nk = k_tot // tk

    def body(dp_ref, w_ref, x_ref, sc_ref, dza_ref, gx_ref, red_ref, acc_s):
        i, kk = pl.program_id(0), pl.program_id(1)
        part = lax.dot_general(dp_ref[...], w_ref[...], _NT, preferred_element_type=F32)

        @pl.when(kk == 0)
        def _():
            acc_s[...] = part

        @pl.when(kk > 0)
        def _():
            acc_s[...] += part

        @pl.when(kk == nk - 1)
        def _():
            dh = acc_s[...]
            xv = x_ref[...]
            mu = jnp.mean(xv, axis=-1, keepdims=True)
            xc = xv - mu
            var = jnp.mean(xc * xc, axis=-1, keepdims=True)
            rstd = lax.rsqrt(var + LN_EPS)
            xhat = xc * rstd
            dxhat = dh * (1.0 + sc_ref[...])
            dx = rstd * (dxhat - jnp.mean(dxhat, axis=-1, keepdims=True) - xhat * jnp.mean(dxhat * xhat, axis=-1, keepdims=True))
            gx_ref[...] = dza_ref[...] + dx
            part_r = jnp.concatenate(
                [jnp.sum(dh, axis=0, keepdims=True), jnp.sum(dh * xhat, axis=0, keepdims=True), jnp.zeros((6, d), F32)], axis=0
            )

            @pl.when(i == 0)
            def _():
                red_ref[...] = part_r

            @pl.when(i > 0)
            def _():
                red_ref[...] += part_r

    row = pl.BlockSpec((tm, d), lambda i, kk: (i, 0))
    return _pcall(
        body,
        name="in_bwd",
        grid=(s_len // tm, nk),
        out_shape=[jax.ShapeDtypeStruct((s_len, d), F32), jax.ShapeDtypeStruct((8, d), F32)],
        in_specs=[
            pl.BlockSpec((tm, tk), lambda i, kk: (i, kk)),
            pl.BlockSpec((d, tk), lambda i, kk: (0, kk)),
            row,
            pl.BlockSpec((1, d), lambda i, kk: (0, 1)),
            row,
        ],
        out_specs=[row, pl.BlockSpec((8, d), lambda i, kk: (0, 0))],
        scratch_shapes=[pltpu.VMEM((tm, d), F32)],
        compiler_params=_cp("arbitrary", "arbitrary"),
    )(dproj, w_in, x, ada, dza)


def _pad_lanes(v, n):
    return jnp.pad(v, ((0, 0), (0, n - v.shape[1])))


def kernel(x, c, w_ada, b_ada, w_in, b_f, attn_sinks, w_br_fox, w_br_swa, w_out, ln_g, ln_b, loss_target, m_w_ada, m_b_ada, m_w_in, m_b_f, m_attn_sinks, m_w_br_fox, m_w_br_swa, m_w_out, m_ln_g, m_ln_b, v_w_ada, v_b_ada, v_w_in, v_b_f, v_attn_sinks, v_w_br_fox, v_w_br_swa, v_w_out, v_ln_g, v_ln_b):
    x2, tgt = x[0], loss_target[0]
    s_len, d = x2.shape
    me = 4 * lax.axis_index("x") + 2 * lax.axis_index("y") + lax.axis_index("c")
    off_ms = OFF_MF + d
    in_pad = off_ms + d
    c_ada = w_ada.shape[2]
    c_in = w_in.shape[2]
    c_br = w_br_fox.shape[2]

    w_in_g = _all_gather(w_in[0].astype(BF16), "ag_w_in", pltpu.HBM)
    w_in_full = w_in_g.reshape(N_DEV, d, c_in).transpose(1, 0, 2).reshape(d, N_DEV * c_in)
    w_in_pad = jnp.concatenate(
        [w_in_full[:, :REAL_FLOG_END], jnp.zeros((d, FLOG_PAD - N_FLOG), BF16), w_in_full[:, REAL_FLOG_END:]], axis=1
    )
    w_bf = _all_gather(w_br_fox[0].astype(BF16), "ag_w_br_fox", pltpu.HBM)
    w_bf = w_bf.reshape(N_DEV, FOX_W, c_br).transpose(1, 0, 2).reshape(FOX_W, d)
    w_bs = _all_gather(w_br_swa[0].astype(BF16), "ag_w_br_swa", pltpu.HBM)
    w_bs = w_bs.reshape(N_DEV, SWA_W, c_br).transpose(1, 0, 2).reshape(SWA_W, d)
    w_o = _all_gather(w_out[0].astype(BF16), "ag_w_out", pltpu.HBM)

    c_all = _gather_rows(c, "ag_c")
    b_cols = lax.dynamic_slice(b_ada, (0, me * c_ada), (1, c_ada))
    ada_cols = _ada_fwd(c_all, w_ada[0], b_cols)
    ada_g = _all_gather(ada_cols, "ag_ada", pltpu.VMEM).reshape(N_DEV, N_DEV, c_ada)
    ada = lax.dynamic_index_in_dim(ada_g, me, axis=1, keepdims=False).reshape(1, N_DEV * c_ada)

    h = _ln_mod(x2, ada)
    qkv_fox = _mm_cols(h, w_in_pad, OFF_FQ, 3 * FOX_W, BF16, "proj_fox")
    flog = _mm_cols(h, w_in_pad, OFF_FLOG, FLOG_PAD, F32, "proj_flog")
    qkv_swa = _mm_cols(h, w_in_pad, OFF_SQ, SWA_W + 2 * SWA_KVW, BF16, "proj_swa")
    gates = _mm_cols(h, w_in_pad, OFF_GF, in_pad - OFF_GF, F32, "proj_gates")
    mf_blk = (OFF_MF - OFF_GF) // d

    flog_t = flog[:, :N_FLOG].T
    bf_col = b_f.reshape(FOX_H, 1)
    cum = _fox_cum(flog_t, bf_col)
    cum_row = cum.reshape(FOX_H, 1, s_len)
    o_fox, lse = _fox_fwd(qkv_fox, cum_row)
    sinks = attn_sinks.reshape(SWA_HQ)
    o_swa = _swa_fwd(qkv_swa, sinks)

    y_fox, a_fox = _branch_fwd(o_fox, gates, 0, w_bf, "branch_fox")
    y_swa, a_swa = _branch_fwd(o_swa, gates, 1, w_bs, "branch_swa")
    merged, dza, dsub, red = _out_stage(gates, mf_blk, y_fox, y_swa, w_o, x2, ada, ln_g, ln_b, tgt)
    loss = lax.psum(0.5 * red[4, 0] / d, ("x", "y", "c"))

    dmf, dms, dy_fox, dy_swa = _merge_bwd(dsub, w_o, gates, mf_blk, y_fox, y_swa)
    do_fox, dg_fox, delta = _branch_bwd(dy_fox, w_bf, o_fox, gates, 0, "branch_fox_bwd", FOX_H)
    do_swa, dg_swa = _branch_bwd(dy_swa, w_bs, o_swa, gates, 1, "branch_swa_bwd", 0)
    delta_row = delta[:, :FOX_H].T.reshape(FOX_H, 1, s_len)
    dq_f, dk_f, dv_f, dcol, drow = _fox_bwd(
        qkv_fox, cum.reshape(FOX_H, s_len, 1), lse.reshape(FOX_H, 1, s_len), delta_row, do_fox
    )
    dflog_t, dbf = _fox_gate_bwd(drow.reshape(FOX_H, s_len), dcol.reshape(FOX_H, s_len), flog_t, bf_col)
    dq_s, dk_s, dv_s, dsink = _swa_bwd(qkv_swa, sinks, do_swa)
    dflog = _pad_lanes(dflog_t.T, FLOG_PAD).astype(BF16)
    dproj = jnp.concatenate([dq_f.astype(BF16), dk_f, dv_f, dflog, dq_s, dk_s, dv_s, dg_fox, dg_swa, dmf, dms], axis=1)
    grad_x, red2 = _in_bwd(dproj, w_in_pad, x2, ada, dza)

    g_w_in = _mm_acc(h.T, dproj, "grad_w_in")
    g_w_in = jnp.concatenate([g_w_in[:, :REAL_FLOG_END], g_w_in[:, OFF_SQ:]], axis=1)
    g_w_bf = _mm_acc(a_fox.T, dy_fox, "grad_w_br_fox")
    g_w_bs = _mm_acc(a_swa.T, dy_swa, "grad_w_br_swa")
    g_w_o = _mm_acc(merged.T, dsub, "grad_w_out")

    r_in, r_bf, r_bs, r_o = _grad_exchange(
        [
            g_w_in.reshape(d, N_DEV, c_in).transpose(1, 0, 2),
            g_w_bf.reshape(FOX_W, N_DEV, c_br).transpose(1, 0, 2),
            g_w_bs.reshape(SWA_W, N_DEV, c_br).transpose(1, 0, 2),
            g_w_o.reshape(N_DEV, d // N_DEV, d),
        ],
        "grad_exchange",
    )
    out_w_in = _sum_adam(r_in, w_in[0], m_w_in[0], v_w_in[0], "adam_w_in")
    out_w_bf = _sum_adam(r_bf, w_br_fox[0], m_w_br_fox[0], v_w_br_fox[0], "adam_w_br_fox")
    out_w_bs = _sum_adam(r_bs, w_br_swa[0], m_w_br_swa[0], v_w_br_swa[0], "adam_w_br_swa")
    out_w_o = _sum_adam(r_o, w_out[0], m_w_out[0], v_w_out[0], "adam_w_out")

    packed = jnp.concatenate([red2[0:1], red2[1:2], red[0:1], _pad_lanes(dbf[:, 0].reshape(1, FOX_H), 128), dsink, red[1:2], red[2:3]], axis=1)
    gathered = _gather_rows(packed, "ag_small")
    pack = lambda a, b, cc, dd, e: jnp.concatenate([a, _pad_lanes(b, 128), _pad_lanes(cc, 128), dd, e], axis=1)
    small = _small_adam(
        gathered,
        pack(b_ada, b_f, attn_sinks, ln_g, ln_b),
        pack(m_b_ada, m_b_f, m_attn_sinks, m_ln_g, m_ln_b),
        pack(v_b_ada, v_b_f, v_attn_sinks, v_ln_g, v_ln_b),
    )
    dada_cols = lax.dynamic_slice(gathered, (0, me * c_ada), (N_DEV, c_ada))
    out_w_ada = _wada_adam(c_all.T, dada_cols, w_ada[0], m_w_ada[0], v_w_ada[0])

    o1, o2, o3 = 3 * d, 3 * d + 128, 3 * d + 256

    def unpack(p):
        return p[:, :o1], p[:, o1 : o1 + FOX_H], p[:, o2 : o2 + SWA_HQ], p[:, o3 : o3 + d], p[:, o3 + d : o3 + 2 * d]

    kinds = []
    for k in range(4):
        b_ada_k, b_f_k, sinks_k, ln_g_k, ln_b_k = unpack(small[k])
        kinds.append(
            [out_w_ada[k][None], b_ada_k, out_w_in[k][None], b_f_k, sinks_k, out_w_bf[k][None], out_w_bs[k][None], out_w_o[k][None], ln_g_k, ln_b_k]
        )
    return (loss, grad_x[None], *kinds[0], *kinds[1], *kinds[2], *kinds[3])
```

```python
import numpy as np
import jax
import jax.numpy as jnp
from jax import lax
from jax.experimental import pallas as pl
from jax.experimental.pallas import tpu as pltpu

F32 = jnp.float32
BF16 = jnp.bfloat16
N_DEV = 8
MESH = pl.DeviceIdType.MESH

FOX_H, FOX_DH, FOX_W = 8, 128, 1024
SWA_HQ, SWA_HKV, SWA_DH, SWA_G = 16, 4, 64, 4
SWA_W, SWA_KVW, WINDOW = 1024, 256, 128
LN_EPS = 1e-5
NEG = -1e30
DEPTH = 1
ALPHA = (2.0 * DEPTH) ** 0.25
FOX_SCALE = FOX_DH ** -0.5
SWA_SCALE = SWA_DH ** -0.5
SLOPES = [2.0 ** (-8.0 * (h + 1.0) / SWA_HQ) for h in range(SWA_HQ)]

ADAM_LR, ADAM_B1, ADAM_B2, ADAM_EPS, ADAM_WD, ADAM_STEP = 0.001, 0.9, 0.999, 1e-08, 0.01, 10

N_FLOG = 8
FLOG_PAD = 512
OFF_FQ, OFF_FK, OFF_FV, OFF_FLOG = 0, 1024, 2048, 3072
OFF_SQ = OFF_FLOG + FLOG_PAD
OFF_SK = OFF_SQ + SWA_W
OFF_SV = OFF_SK + SWA_KVW
OFF_GF = OFF_SV + SWA_KVW
OFF_GS = OFF_GF + FOX_W
OFF_MF = OFF_GS + SWA_W
REAL_FLOG_END = OFF_FLOG + N_FLOG

ATT_BLK = 512
VMEM_LIMIT = 52 * 1024 * 1024


def _pcall(body, **kw):
    return pl.pallas_call(body, **kw)


def _cp(*sem):
    return pltpu.CompilerParams(dimension_semantics=sem, vmem_limit_bytes=VMEM_LIMIT)


def _sigmoid(x):
    return 1.0 / (1.0 + jnp.exp(-x))


def _all_gather(x, name, space):
    m_per, n = x.shape

    def body(x_ref, out_ref, send_sems, recv_sems, local_sem):
        mx, my, mc = lax.axis_index("x"), lax.axis_index("y"), lax.axis_index("c")
        me, sibling = (mx, my, mc), (mx, my, 1 - mc)
        chips = [(1 - mx, my), (mx, 1 - my), (1 - mx, 1 - my)]

        def rows(px, py, pc):
            return out_ref.at[pl.ds((4 * px + 2 * py + pc) * m_per, m_per), :]

        def copy(k, block, to, src=None):
            return pltpu.make_async_remote_copy(
                src_ref=rows(*block) if src is None else src,
                dst_ref=rows(*block),
                send_sem=send_sems.at[k],
                recv_sem=recv_sems.at[k],
                device_id=to,
                device_id_type=MESH,
            )

        mine = pltpu.make_async_copy(x_ref, rows(*me), local_sem)
        mine.start()
        first = [copy(0, me, sibling, src=x_ref)]
        first += [copy(1 + j, me, (*chip, mc), src=x_ref) for j, chip in enumerate(chips)]
        for cp in first:
            cp.start()
        passed = [copy(4 + j, (*chip, mc), sibling) for j, chip in enumerate(chips)]
        for j, chip in enumerate(chips):
            copy(1 + j, (*chip, mc), me).wait_recv()
            passed[j].start()
        copy(0, sibling, me).wait_recv()
        for j, chip in enumerate(chips):
            copy(4 + j, (*chip, 1 - mc), me).wait_recv()
        for cp in first + passed:
            cp.wait_send()
        mine.wait()

    return _pcall(
        body,
        name=name,
        out_shape=jax.ShapeDtypeStruct((N_DEV * m_per, n), x.dtype),
        in_specs=[pl.BlockSpec(memory_space=space)],
        out_specs=pl.BlockSpec(memory_space=space),
        scratch_shapes=[pltpu.SemaphoreType.DMA((7,)), pltpu.SemaphoreType.DMA((7,)), pltpu.SemaphoreType.DMA],
    )(x)


def _peer(d, mx, my, mc):
    return (1 - mx if (d >> 2) & 1 else mx, 1 - my if (d >> 1) & 1 else my, 1 - mc if d & 1 else mc)


def _rider_copies(kind, ins, outs, send_sems, recv_sems, local_sems):
    mx, my, mc = lax.axis_index("x"), lax.axis_index("y"), lax.axis_index("c")
    me = 4 * mx + 2 * my + mc
    remote, local = [], []
    for a in range(len(ins)):
        if kind == "gather":
            m_per = ins[a].shape[0]
            mine = outs[a].at[pl.ds(me * m_per, m_per), :]
            local.append(pltpu.make_async_copy(ins[a], mine, local_sems.at[a]))
        else:
            local.append(pltpu.make_async_copy(ins[a].at[me], outs[a].at[0], local_sems.at[a]))
        for d in range(1, N_DEV):
            px, py, pc = _peer(d, mx, my, mc)
            if kind == "gather":
                src, dst = ins[a], mine
            else:
                src, dst = ins[a].at[4 * px + 2 * py + pc], outs[a].at[d]
            remote.append(
                pltpu.make_async_remote_copy(
                    src_ref=src,
                    dst_ref=dst,
                    send_sem=send_sems.at[a * 7 + d - 1],
                    recv_sem=recv_sems.at[a * 7 + d - 1],
                    device_id=(px, py, pc),
                    device_id_type=MESH,
                )
            )
    return remote, local


def _rider_start(*args):
    remote, local = _rider_copies(*args)
    for cp in local + remote:
        cp.start()


def _rider_wait(*args):
    remote, local = _rider_copies(*args)
    for cp in remote:
        cp.wait_recv()
    for cp in remote:
        cp.wait_send()
    for cp in local:
        cp.wait()


def _rider_scratch(n):
    return [pltpu.SemaphoreType.DMA((7 * n,)), pltpu.SemaphoreType.DMA((7 * n,)), pltpu.SemaphoreType.DMA((n,))]


def _gather_rows(v, name):
    n = v.shape[1]
    g = _all_gather(jnp.broadcast_to(v, (8, n)), name, pltpu.VMEM)
    return g.reshape(N_DEV, 8, n)[:, 0, :]


def _adamw(w, g, m, v):
    m = ADAM_B1 * m + (1.0 - ADAM_B1) * g
    v = ADAM_B2 * v + (1.0 - ADAM_B2) * (g * g)
    m_hat = m / (1.0 - ADAM_B1**ADAM_STEP)
    v_hat = v / (1.0 - ADAM_B2**ADAM_STEP)
    delta = -ADAM_LR * (m_hat / (jnp.sqrt(v_hat) + ADAM_EPS) + ADAM_WD * w)
    return delta, m, v


def _sum_adam(recv, w, m, v, name):
    _, r_tot, c = recv.shape
    c_pad = -(-c // 128) * 128
    tr = r_tot
    while 8 * tr * c_pad * 4 > 6 * 1024 * 1024 and tr % 32 == 0:
        tr //= 2

    def body(r_ref, w_ref, m_ref, v_ref, g_ref, d_ref, nm_ref, nv_ref):
        g = r_ref[0].astype(F32)
        for k in range(1, N_DEV):
            g = g + r_ref[k].astype(F32)
        d, nm, nv = _adamw(w_ref[...], g, m_ref[...], v_ref[...])
        g_ref[...] = g
        d_ref[...] = d
        nm_ref[...] = nm
        nv_ref[...] = nv

    blk = pl.BlockSpec((tr, c), lambda i: (i, 0))
    return _pcall(
        body,
        name=name,
        grid=(r_tot // tr,),
        out_shape=[jax.ShapeDtypeStruct((r_tot, c), F32)] * 4,
        in_specs=[pl.BlockSpec((N_DEV, tr, c), lambda i: (0, i, 0)), blk, blk, blk],
        out_specs=[blk] * 4,
        compiler_params=_cp("parallel"),
    )(recv, w, m, v)


def _wada_adam(c_t, dada_cols, w, m, v):
    d_model, c = w.shape
    tr = min(256, d_model)

    def body(ct_ref, da_ref, w_ref, m_ref, v_ref, g_ref, d_ref, nm_ref, nv_ref):
        g = jnp.dot(ct_ref[...].astype(BF16), da_ref[...].astype(BF16), preferred_element_type=F32)
        d, nm, nv = _adamw(w_ref[...], g, m_ref[...], v_ref[...])
        g_ref[...] = g
        d_ref[...] = d
        nm_ref[...] = nm
        nv_ref[...] = nv

    blk = pl.BlockSpec((tr, c), lambda i: (i, 0))
    return _pcall(
        body,
        name="wada_adam",
        grid=(d_model // tr,),
        out_shape=[jax.ShapeDtypeStruct((d_model, c), F32)] * 4,
        in_specs=[pl.BlockSpec((tr, N_DEV), lambda i: (i, 0)), pl.BlockSpec((N_DEV, c), lambda i: (0, 0)), blk, blk, blk],
        out_specs=[blk] * 4,
        compiler_params=_cp("parallel"),
    )(c_t, dada_cols, w, m, v)


def _small_adam(gathered, w, m, v):
    p = w.shape[1]

    def body(a_ref, w_ref, m_ref, v_ref, g_ref, d_ref, nm_ref, nv_ref):
        g = a_ref[0:1, :]
        for k in range(1, N_DEV):
            g = g + a_ref[k : k + 1, :]
        d, nm, nv = _adamw(w_ref[...], g, m_ref[...], v_ref[...])
        g_ref[...] = g
        d_ref[...] = d
        nm_ref[...] = nm
        nv_ref[...] = nv

    return _pcall(
        body,
        name="small_adam",
        out_shape=[jax.ShapeDtypeStruct((1, p), F32)] * 4,
    )(gathered, w, m, v)


def _ada_fwd(c_all, w_ada, b_cols):
    c = w_ada.shape[1]

    def body(c_ref, w_ref, b_ref, o_ref):
        o_ref[...] = jnp.dot(c_ref[...].astype(BF16), w_ref[...].astype(BF16), preferred_element_type=F32) + b_ref[...]

    return _pcall(
        body,
        name="ada_fwd",
        out_shape=jax.ShapeDtypeStruct((N_DEV, c), F32),
        compiler_params=_cp(),
    )(c_all, w_ada, b_cols)


def _ln_mod(x, ada):
    s_len, d = x.shape
    tm = min(512, s_len)

    def body(x_ref, sh_ref, sc_ref, h_ref):
        xv = x_ref[...]
        mu = jnp.mean(xv, axis=-1, keepdims=True)
        xc = xv - mu
        var = jnp.mean(xc * xc, axis=-1, keepdims=True)
        xhat = xc * lax.rsqrt(var + LN_EPS)
        h_ref[...] = (xhat * (1.0 + sc_ref[...]) + sh_ref[...]).astype(BF16)

    return _pcall(
        body,
        name="ln_mod",
        grid=(s_len // tm,),
        out_shape=jax.ShapeDtypeStruct((s_len, d), BF16),
        in_specs=[
            pl.BlockSpec((tm, d), lambda i: (i, 0)),
            pl.BlockSpec((1, d), lambda i: (0, 0)),
            pl.BlockSpec((1, d), lambda i: (0, 1)),
        ],
        out_specs=pl.BlockSpec((tm, d), lambda i: (i, 0)),
        compiler_params=_cp("parallel"),
    )(x, ada, ada)


def _mm_cols(a, b, col_off, n_cols, out_dtype, name, ride=()):
    m, k = a.shape
    tm, tn = min(1024, m), 512
    off = col_off // tn
    ni, nj = m // tm, n_cols // tn
    n = len(ride)

    def body(a_ref, b_ref, *rest):
        ins, o_ref, outs, sems = rest[:n], rest[n], rest[n + 1 : 2 * n + 1], rest[2 * n + 1 :]
        i, j = pl.program_id(0), pl.program_id(1)
        if n:

            @pl.when((i == 0) & (j == 0))
            def _():
                _rider_start("gather", ins, outs, *sems)

        o_ref[...] = jnp.dot(a_ref[...], b_ref[...], preferred_element_type=F32).astype(out_dtype)
        if n:

            @pl.when((i == ni - 1) & (j == nj - 1))
            def _():
                _rider_wait("gather", ins, outs, *sems)

    hbm = pl.BlockSpec(memory_space=pltpu.HBM)
    out = _pcall(
        body,
        name=name,
        grid=(ni, nj),
        out_shape=[jax.ShapeDtypeStruct((m, n_cols), out_dtype)]
        + [jax.ShapeDtypeStruct((N_DEV * r.shape[0], r.shape[1]), r.dtype) for r in ride],
        in_specs=[pl.BlockSpec((tm, k), lambda i, j: (i, 0)), pl.BlockSpec((k, tn), lambda i, j: (0, off + j))] + [hbm] * n,
        out_specs=[pl.BlockSpec((tm, tn), lambda i, j: (i, j))] + [hbm] * n,
        scratch_shapes=_rider_scratch(n) if n else [],
        compiler_params=_cp("arbitrary", "arbitrary") if n else _cp("parallel", "parallel"),
    )(a, b, *ride)
    return out if n else out[0]


def _mm_acc(a, b, name):
    m, k = a.shape
    n = b.shape[1]
    tm, tn, tk = min(1024, m), min(1024, n), min(1024, k)
    nk = k // tk

    def body(a_ref, b_ref, o_ref, acc_s):
        kk = pl.program_id(2)
        part = jnp.dot(a_ref[...], b_ref[...], preferred_element_type=F32)

        @pl.when(kk == 0)
        def _():
            acc_s[...] = part

        @pl.when(kk > 0)
        def _():
            acc_s[...] += part

        @pl.when(kk == nk - 1)
        def _():
            o_ref[...] = acc_s[...].astype(BF16)

    return _pcall(
        body,
        name=name,
        grid=(m // tm, n // tn, nk),
        out_shape=jax.ShapeDtypeStruct((m, n), BF16),
        in_specs=[pl.BlockSpec((tm, tk), lambda i, j, kk: (i, kk)), pl.BlockSpec((tk, tn), lambda i, j, kk: (kk, j))],
        out_specs=pl.BlockSpec((tm, tn), lambda i, j, kk: (i, j)),
        scratch_shapes=[pltpu.VMEM((tm, tn), F32)],
        compiler_params=_cp("parallel", "parallel", "arbitrary"),
    )(a, b)


def _split3(a):
    hi = a.astype(BF16)
    r1 = a - hi.astype(F32)
    mid = r1.astype(BF16)
    lo = (r1 - mid.astype(F32)).astype(BF16)
    return hi, mid, lo


def _dot_ones(a, tri):
    return sum(jnp.dot(t, tri, preferred_element_type=F32) for t in _split3(a))


def _log_sigmoid(x):
    return jnp.minimum(x, 0.0) - jnp.log1p(jnp.exp(-jnp.abs(x)))


def _fox_cum(flog_t, bf_col):
    s_len = flog_t.shape[1]

    def body(fl_ref, bf_ref, cum_ref):
        r = lax.broadcasted_iota(jnp.int32, (128, 128), 0)
        c = lax.broadcasted_iota(jnp.int32, (128, 128), 1)
        upper = (r <= c).astype(BF16)

        def step(t, carry):
            sl = pl.ds(pl.multiple_of(t * 128, 128), 128)
            lf = _log_sigmoid(fl_ref[:, sl] + bf_ref[...])
            cs = _dot_ones(lf, upper) + carry
            cum_ref[:, sl] = cs
            return cs[:, 127:128]

        lax.fori_loop(0, s_len // 128, step, jnp.zeros((FOX_H, 1), F32))

    return _pcall(body, name="fox_cum", out_shape=jax.ShapeDtypeStruct((FOX_H, s_len), F32))(flog_t, bf_col)


def _fox_gate_bwd(drow, dcol, flog_t, bf_col):
    s_len = flog_t.shape[1]
    n = s_len // 128

    def body(dr_ref, dc_ref, fl_ref, bf_ref, dfl_ref, dbf_ref):
        r = lax.broadcasted_iota(jnp.int32, (128, 128), 0)
        c = lax.broadcasted_iota(jnp.int32, (128, 128), 1)
        lower = (r >= c).astype(BF16)

        def step(t, carry):
            run, tot = carry
            sl = pl.ds(pl.multiple_of((n - 1 - t) * 128, 128), 128)
            rc = _dot_ones(dr_ref[:, sl] - dc_ref[:, sl], lower) + run
            dfl = rc * _sigmoid(-(fl_ref[:, sl] + bf_ref[...]))
            dfl_ref[:, sl] = dfl
            return rc[:, 0:1], tot + jnp.sum(dfl, axis=1, keepdims=True)

        zero = jnp.zeros((FOX_H, 1), F32)
        _, tot = lax.fori_loop(0, n, step, (zero, zero))
        dbf_ref[...] = jnp.broadcast_to(tot, (FOX_H, 128))

    return _pcall(
        body,
        name="fox_gate_bwd",
        out_shape=[jax.ShapeDtypeStruct((FOX_H, s_len), F32), jax.ShapeDtypeStruct((FOX_H, 128), F32)],
    )(drow, dcol, flog_t, bf_col)


def _diag_mask(blk, transposed=False):
    r = lax.broadcasted_iota(jnp.int32, (blk, blk), 0)
    c = lax.broadcasted_iota(jnp.int32, (blk, blk), 1)
    return c >= r if transposed else r >= c


_NT = (((1,), (1,)), ((), ()))
_TN = (((0,), (0,)), ((), ()))


def _fox_fwd(qkv, cum_row):
    s_len = qkv.shape[0]
    blk = min(ATT_BLK, s_len)
    nb = s_len // blk
    log2e = 1.4426950408889634

    def body(q_ref, k_ref, v_ref, c_ref, o_ref, lse_ref, mx_s, acc_s):
        i = pl.program_id(1)

        def logits(j, masked):
            cols = pl.ds(pl.multiple_of(j * blk, blk), blk)
            u = lax.dot_general(q_ref[...], k_ref[cols, :], _NT, preferred_element_type=F32) - c_ref[:, cols] * (1.0 / FOX_SCALE)
            if masked:
                u = jnp.where(_diag_mask(blk), u, NEG)
            return u, cols

        def lane_max(j, masked):
            u, _ = logits(j, masked)
            part = u[:, 0:128]
            for t in range(1, blk // 128):
                part = jnp.maximum(part, u[:, t * 128 : (t + 1) * 128])
            mx_s[...] = jnp.maximum(mx_s[...], part)

        mx_s[...] = jnp.full(mx_s.shape, NEG, F32)
        lax.fori_loop(0, i, lambda j, c: (lane_max(j, False), c)[1], 0)
        lane_max(i, True)
        m = jnp.max(mx_s[...], axis=1, keepdims=True)

        ones_col = (lax.broadcasted_iota(jnp.int32, (blk, 128), 1) == 0).astype(BF16)

        def weigh(j, masked):
            u, cols = logits(j, masked)
            p = jnp.exp2((u - m) * (FOX_SCALE * log2e))
            v1 = jnp.concatenate([v_ref[cols, :], ones_col], axis=1)
            acc_s[...] += jnp.dot(p.astype(BF16), v1, preferred_element_type=F32)

        acc_s[...] = jnp.zeros(acc_s.shape, F32)
        lax.fori_loop(0, i, lambda j, c: (weigh(j, False), c)[1], 0)
        weigh(i, True)
        l = acc_s[:, FOX_DH : FOX_DH + 1]
        o_ref[...] = acc_s[:, :FOX_DH] / l
        lse_ref[...] = m * FOX_SCALE + jnp.log(l)

    return _pcall(
        body,
        name="fox_fwd",
        grid=(FOX_H, nb),
        out_shape=[jax.ShapeDtypeStruct((s_len, FOX_W), F32), jax.ShapeDtypeStruct((FOX_H, s_len, 1), F32)],
        in_specs=[
            pl.BlockSpec((blk, FOX_DH), lambda h, i: (i, h)),
            pl.BlockSpec((s_len, FOX_DH), lambda h, i: (0, FOX_H + h)),
            pl.BlockSpec((s_len, FOX_DH), lambda h, i: (0, 2 * FOX_H + h)),
            pl.BlockSpec((None, 1, s_len), lambda h, i: (h, 0, 0)),
        ],
        out_specs=[
            pl.BlockSpec((blk, FOX_DH), lambda h, i: (i, h)),
            pl.BlockSpec((None, blk, 1), lambda h, i: (h, i, 0)),
        ],
        scratch_shapes=[pltpu.VMEM((blk, 128), F32), pltpu.VMEM((blk, 2 * FOX_DH), F32)],
        compiler_params=_cp("parallel", "arbitrary"),
    )(qkv, qkv, qkv, cum_row)


def _fox_bwd(qkv, cum_col, lse_row, delta_row, do):
    s_len = qkv.shape[0]
    blk = min(ATT_BLK, s_len)
    nb = s_len // blk

    def body(q_ref, k_ref, v_ref, c_ref, lse_ref, dl_ref, do_ref, dq_ref, dk_ref, dv_ref, dc_ref, dr_ref, dk_s, dv_s, dc_s, cb_s):
        j = pl.program_id(1)

        @pl.when(j == 0)
        def _():
            dq_ref[...] = jnp.zeros(dq_ref.shape, F32)
            dr_ref[...] = jnp.zeros(dr_ref.shape, F32)

        dk_s[...] = jnp.zeros(dk_s.shape, F32)
        dv_s[...] = jnp.zeros(dv_s.shape, F32)
        dc_s[...] = jnp.zeros(dc_s.shape, F32)
        cb_s[...] = jnp.broadcast_to(c_ref[...], cb_s.shape)

        def tile(i, diag):
            rows = pl.ds(pl.multiple_of(i * blk, blk), blk)
            q, dob = q_ref[rows, :], do_ref[rows, :]
            k, v = k_ref[...], v_ref[...]
            s_t = lax.dot_general(k, q, _NT, preferred_element_type=F32) * FOX_SCALE - cb_s[...]
            p_t = jnp.exp(s_t - lse_ref[:, rows])
            if diag:
                p_t = jnp.where(_diag_mask(blk, transposed=True), p_t, 0.0)
            dp_t = lax.dot_general(v, dob, _NT, preferred_element_type=F32)
            ds_t = p_t * (dp_t - dl_ref[:, rows])
            dsb = ds_t.astype(BF16)
            dv_s[...] += jnp.dot(p_t.astype(BF16), dob, preferred_element_type=F32)
            dk_s[...] += jnp.dot(dsb, q, preferred_element_type=F32)
            dq_c = lax.dot_general(dsb, k, _TN, preferred_element_type=F32)
            part = ds_t[:, 0:128]
            for t in range(1, blk // 128):
                part = part + ds_t[:, t * 128 : (t + 1) * 128]
            dc_s[...] += part
            dr_ref[:, rows] += jnp.sum(ds_t, axis=0, keepdims=True)
            if diag:
                dq_ref[rows, :] = (dq_ref[rows, :] + dq_c) * FOX_SCALE
            else:
                dq_ref[rows, :] += dq_c

        tile(j, True)

        def below(i, carry):
            tile(i, False)
            return carry

        lax.fori_loop(j + 1, nb, below, 0)
        dk_ref[...] = (dk_s[...] * FOX_SCALE).astype(BF16)
        dv_ref[...] = dv_s[...].astype(BF16)
        dc_ref[...] = jnp.sum(dc_s[...], axis=1, keepdims=True)

    head = lambda h, j: (0, h)
    row = pl.BlockSpec((None, 1, s_len), lambda h, j: (h, 0, 0))
    return _pcall(
        body,
        name="fox_bwd",
        grid=(FOX_H, nb),
        out_shape=[
            jax.ShapeDtypeStruct((s_len, FOX_W), F32),
            jax.ShapeDtypeStruct((s_len, FOX_W), BF16),
            jax.ShapeDtypeStruct((s_len, FOX_W), BF16),
            jax.ShapeDtypeStruct((FOX_H, s_len, 1), F32),
            jax.ShapeDtypeStruct((FOX_H, 1, s_len), F32),
        ],
        in_specs=[
            pl.BlockSpec((s_len, FOX_DH), head),
            pl.BlockSpec((blk, FOX_DH), lambda h, j: (j, FOX_H + h)),
            pl.BlockSpec((blk, FOX_DH), lambda h, j: (j, 2 * FOX_H + h)),
            pl.BlockSpec((None, blk, 1), lambda h, j: (h, j, 0)),
            row,
            row,
            pl.BlockSpec((s_len, FOX_DH), head),
        ],
        out_specs=[
            pl.BlockSpec((s_len, FOX_DH), head),
            pl.BlockSpec((blk, FOX_DH), lambda h, j: (j, h)),
            pl.BlockSpec((blk, FOX_DH), lambda h, j: (j, h)),
            pl.BlockSpec((None, blk, 1), lambda h, j: (h, j, 0)),
            row,
        ],
        scratch_shapes=[
            pltpu.VMEM((blk, FOX_DH), F32),
            pltpu.VMEM((blk, FOX_DH), F32),
            pltpu.VMEM((blk, 128), F32),
            pltpu.VMEM((blk, blk), F32),
        ],
        compiler_params=_cp("parallel", "arbitrary"),
    )(qkv, qkv, qkv, cum_col, lse_row, delta_row, do)


def _swa_group(i, q_ref, kk, sinks_ref, g):
    rows = SWA_G * WINDOW
    r = lax.broadcasted_iota(jnp.int32, (rows, 2 * WINDOW), 0)
    c = lax.broadcasted_iota(jnp.int32, (rows, 2 * WINDOW), 1)
    dist = (r & (WINDOW - 1)) - c + WINDOW
    valid = (dist >= 0) & (dist < WINDOW) & ((c >= WINDOW) | (i > 0))
    head = lax.broadcasted_iota(jnp.int32, (rows, 1), 0) // WINDOW
    slope = jnp.zeros((rows, 1), F32)
    sink = jnp.zeros((rows, 1), F32)
    for t in range(SWA_G):
        h = g * SWA_G + t
        slope = jnp.where(head == t, SLOPES[h], slope)
        sink = jnp.where(head == t, sinks_ref[h], sink)
    q = jnp.concatenate([q_ref[:, (g * SWA_G + t) * SWA_DH : (g * SWA_G + t + 1) * SWA_DH] for t in range(SWA_G)], axis=0)
    k = kk[:, g * SWA_DH : (g + 1) * SWA_DH]
    s = lax.dot_general(q, k, _NT, preferred_element_type=F32) * SWA_SCALE - slope * dist.astype(F32)
    s = jnp.where(valid, s, NEG)
    m = jnp.maximum(jnp.max(s, axis=1, keepdims=True), sink)
    e = jnp.exp(s - m)
    e_sink = jnp.exp(sink - m)
    inv = 1.0 / (jnp.sum(e, axis=1, keepdims=True) + e_sink)
    return q, k, e * inv, e_sink * inv


def _swa_specs(col_q, col_k, col_v, rev, nb):
    def blk(t):
        return nb - 1 - t if rev else t

    return [
        pl.BlockSpec((WINDOW, SWA_W), lambda t: (blk(t), col_q)),
        pl.BlockSpec((WINDOW, SWA_KVW), lambda t: (jnp.maximum(blk(t) - 1, 0), col_k)),
        pl.BlockSpec((WINDOW, SWA_KVW), lambda t: (blk(t), col_k)),
        pl.BlockSpec((WINDOW, SWA_KVW), lambda t: (jnp.maximum(blk(t) - 1, 0), col_v)),
        pl.BlockSpec((WINDOW, SWA_KVW), lambda t: (blk(t), col_v)),
    ]


def _swa_fwd(qkv, sinks):
    s_len = qkv.shape[0]
    nb = s_len // WINDOW

    def body(q_ref, kp_ref, kc_ref, vp_ref, vc_ref, sinks_ref, o_ref):
        i = pl.program_id(0)
        kk = jnp.concatenate([kp_ref[...], kc_ref[...]], axis=0)
        vv = jnp.concatenate([vp_ref[...], vc_ref[...]], axis=0)
        for g in range(SWA_HKV):
            _, _, p, _ = _swa_group(i, q_ref, kk, sinks_ref, g)
            o = jnp.dot(p.astype(BF16), vv[:, g * SWA_DH : (g + 1) * SWA_DH], preferred_element_type=F32)
            for t in range(SWA_G):
                h = g * SWA_G + t
                o_ref[:, h * SWA_DH : (h + 1) * SWA_DH] = o[t * WINDOW : (t + 1) * WINDOW, :]

    return _pcall(
        body,
        name="swa_fwd",
        grid=(nb,),
        out_shape=jax.ShapeDtypeStruct((s_len, SWA_W), F32),
        in_specs=_swa_specs(0, 4, 5, False, nb) + [pl.BlockSpec(memory_space=pltpu.SMEM)],
        out_specs=pl.BlockSpec((WINDOW, SWA_W), lambda t: (t, 0)),
        compiler_params=_cp("parallel"),
    )(qkv, qkv, qkv, qkv, qkv, sinks)


def _swa_bwd(qkv, sinks, do):
    s_len = qkv.shape[0]
    nb = s_len // WINDOW

    def body(q_ref, kp_ref, kc_ref, vp_ref, vc_ref, sinks_ref, do_ref, dq_ref, dk_ref, dv_ref, dsink_ref, ck_s, cv_s, dkk_s, dvv_s):
        t = pl.program_id(0)
        i = nb - 1 - t

        @pl.when(t == 0)
        def _():
            ck_s[...] = jnp.zeros(ck_s.shape, F32)
            cv_s[...] = jnp.zeros(cv_s.shape, F32)
            dsink_ref[...] = jnp.zeros(dsink_ref.shape, F32)

        kk = jnp.concatenate([kp_ref[...], kc_ref[...]], axis=0)
        vv = jnp.concatenate([vp_ref[...], vc_ref[...]], axis=0)
        lane = lax.broadcasted_iota(jnp.int32, (1, 128), 1)
        dsink = jnp.zeros((1, 128), F32)
        for g in range(SWA_HKV):
            cols = slice(g * SWA_DH, (g + 1) * SWA_DH)
            q, k, p, p_sink = _swa_group(i, q_ref, kk, sinks_ref, g)
            dob = jnp.concatenate([do_ref[:, (g * SWA_G + t) * SWA_DH : (g * SWA_G + t + 1) * SWA_DH] for t in range(SWA_G)], axis=0)
            dp = lax.dot_general(dob, vv[:, cols], _NT, preferred_element_type=F32)
            delta = jnp.sum(p * dp, axis=1, keepdims=True)
            dsb = (p * (dp - delta)).astype(BF16)
            dq = (jnp.dot(dsb, k, preferred_element_type=F32) * SWA_SCALE).astype(BF16)
            ps_d = p_sink * delta
            for t in range(SWA_G):
                h = g * SWA_G + t
                dq_ref[:, h * SWA_DH : (h + 1) * SWA_DH] = dq[t * WINDOW : (t + 1) * WINDOW, :]
                dsink = dsink + jnp.where(lane == h, -jnp.sum(ps_d[t * WINDOW : (t + 1) * WINDOW, :], axis=0, keepdims=True), 0.0)
            dkk_s[:, cols] = lax.dot_general(dsb, q, _TN, preferred_element_type=F32) * SWA_SCALE
            dvv_s[:, cols] = lax.dot_general(p.astype(BF16), dob, _TN, preferred_element_type=F32)
        dk_ref[...] = (dkk_s[WINDOW:, :] + ck_s[...]).astype(BF16)
        dv_ref[...] = (dvv_s[WINDOW:, :] + cv_s[...]).astype(BF16)
        ck_s[...] = dkk_s[:WINDOW, :]
        cv_s[...] = dvv_s[:WINDOW, :]
        dsink_ref[...] += dsink

    row = lambda t: (nb - 1 - t, 0)
    return _pcall(
        body,
        name="swa_bwd",
        grid=(nb,),
        out_shape=[
            jax.ShapeDtypeStruct((s_len, SWA_W), BF16),
            jax.ShapeDtypeStruct((s_len, SWA_KVW), BF16),
            jax.ShapeDtypeStruct((s_len, SWA_KVW), BF16),
            jax.ShapeDtypeStruct((1, 128), F32),
        ],
        in_specs=_swa_specs(0, 4, 5, True, nb)
        + [pl.BlockSpec(memory_space=pltpu.SMEM), pl.BlockSpec((WINDOW, SWA_W), row)],
        out_specs=[
            pl.BlockSpec((WINDOW, SWA_W), row),
            pl.BlockSpec((WINDOW, SWA_KVW), row),
            pl.BlockSpec((WINDOW, SWA_KVW), row),
            pl.BlockSpec((1, 128), lambda t: (0, 0)),
        ],
        scratch_shapes=[
            pltpu.VMEM((WINDOW, SWA_KVW), F32),
            pltpu.VMEM((WINDOW, SWA_KVW), F32),
            pltpu.VMEM((2 * WINDOW, SWA_KVW), F32),
            pltpu.VMEM((2 * WINDOW, SWA_KVW), F32),
        ],
        compiler_params=_cp("arbitrary"),
    )(qkv, qkv, qkv, qkv, qkv, sinks, do)


def _branch_fwd(o, gates, g_blk, w_b, name):
    s_len, wd = o.shape
    d = w_b.shape[1]
    tm = min(512, s_len)

    def body(o_ref, g_ref, w_ref, y_ref, a_ref):
        g = g_ref[...]
        a = (o_ref[...] * (g * _sigmoid(g))).astype(BF16)
        a_ref[...] = a
        y_ref[...] = jnp.dot(a, w_ref[...], preferred_element_type=F32)

    return _pcall(
        body,
        name=name,
        grid=(s_len // tm,),
        out_shape=[jax.ShapeDtypeStruct((s_len, d), F32), jax.ShapeDtypeStruct((s_len, wd), BF16)],
        in_specs=[
            pl.BlockSpec((tm, wd), lambda i: (i, 0)),
            pl.BlockSpec((tm, wd), lambda i: (i, g_blk)),
            pl.BlockSpec((wd, d), lambda i: (0, 0)),
        ],
        out_specs=[pl.BlockSpec((tm, d), lambda i: (i, 0)), pl.BlockSpec((tm, wd), lambda i: (i, 0))],
        compiler_params=_cp("parallel"),
    )(o, gates, w_b)


def _out_stage(gates, mf_blk, y_fox, y_swa, w_out, x, ada, ln_g, ln_b, target):
    s_len, d = x.shape
    tm = min(128, s_len)
    n_steps = s_len // tm

    def body(mf_ref, ms_ref, yf_ref, ys_ref, w_ref, x_ref, gate_ref, lg_ref, lb_ref, t_ref, mg_ref, dza_ref, dsub_ref, red_ref):
        i = pl.program_id(0)
        merged = _sigmoid(mf_ref[...]) * yf_ref[...] + _sigmoid(ms_ref[...]) * ys_ref[...]
        mb = merged.astype(BF16)
        mg_ref[...] = mb
        sub = jnp.dot(mb, w_ref[...], preferred_element_type=F32)
        gate = gate_ref[...]
        z = ALPHA * x_ref[...] + gate * sub
        mu = jnp.mean(z, axis=-1, keepdims=True)
        zc = z - mu
        var = jnp.mean(zc * zc, axis=-1, keepdims=True)
        rstd = lax.rsqrt(var + LN_EPS)
        zhat = zc * rstd
        err = zhat * lg_ref[...] + lb_ref[...] - t_ref[...]
        dout = err * (1.0 / d)
        dzhat = dout * lg_ref[...]
        dz = rstd * (dzhat - jnp.mean(dzhat, axis=-1, keepdims=True) - zhat * jnp.mean(dzhat * zhat, axis=-1, keepdims=True))
        dza_ref[...] = ALPHA * dz
        dsub_ref[...] = (gate * dz).astype(BF16)
        part = jnp.concatenate(
            [
                jnp.sum(dz * sub, axis=0, keepdims=True),
                jnp.sum(dout * zhat, axis=0, keepdims=True),
                jnp.sum(dout, axis=0, keepdims=True),
                jnp.sum(err * err, axis=0, keepdims=True),
                jnp.zeros((4, d), F32),
            ],
            axis=0,
        )

        @pl.when(i == 0)
        def _():
            red_ref[...] = part

        @pl.when(i > 0)
        def _():
            red_ref[...] += part

        @pl.when(i == n_steps - 1)
        def _():
            red_ref[4:5, :] = jnp.broadcast_to(jnp.sum(red_ref[3:4, :], axis=1, keepdims=True), (1, d))

    row = pl.BlockSpec((tm, d), lambda i: (i, 0))
    vec = pl.BlockSpec((1, d), lambda i: (0, 0))
    return _pcall(
        body,
        name="out_stage",
        grid=(n_steps,),
        out_shape=[
            jax.ShapeDtypeStruct((s_len, d), BF16),
            jax.ShapeDtypeStruct((s_len, d), F32),
            jax.ShapeDtypeStruct((s_len, d), BF16),
            jax.ShapeDtypeStruct((8, d), F32),
        ],
        in_specs=[
            pl.BlockSpec((tm, d), lambda i: (i, mf_blk)),
            pl.BlockSpec((tm, d), lambda i: (i, mf_blk + 1)),
            row,
            row,
            pl.BlockSpec((d, d), lambda i: (0, 0)),
            row,
            pl.BlockSpec((1, d), lambda i: (0, 2)),
            vec,
            vec,
            row,
        ],
        out_specs=[row, row, row, pl.BlockSpec((8, d), lambda i: (0, 0))],
        compiler_params=_cp("arbitrary"),
    )(gates, gates, y_fox, y_swa, w_out, x, ada, ln_g, ln_b, target)


def _merge_bwd(dsub, w_out, gates, mf_blk, y_fox, y_swa):
    s_len, d = dsub.shape
    tm = min(128, s_len)

    def body(ds_ref, w_ref, mf_ref, ms_ref, yf_ref, ys_ref, dmf_ref, dms_ref, dyf_ref, dys_ref):
        dm = lax.dot_general(ds_ref[...], w_ref[...], _NT, preferred_element_type=F32)
        sf, ss = _sigmoid(mf_ref[...]), _sigmoid(ms_ref[...])
        dmf_ref[...] = (dm * yf_ref[...] * (sf * (1.0 - sf))).astype(BF16)
        dms_ref[...] = (dm * ys_ref[...] * (ss * (1.0 - ss))).astype(BF16)
        dyf_ref[...] = (dm * sf).astype(BF16)
        dys_ref[...] = (dm * ss).astype(BF16)

    row = pl.BlockSpec((tm, d), lambda i: (i, 0))
    return _pcall(
        body,
        name="merge_bwd",
        grid=(s_len // tm,),
        out_shape=[jax.ShapeDtypeStruct((s_len, d), BF16)] * 4,
        in_specs=[
            row,
            pl.BlockSpec((d, d), lambda i: (0, 0)),
            pl.BlockSpec((tm, d), lambda i: (i, mf_blk)),
            pl.BlockSpec((tm, d), lambda i: (i, mf_blk + 1)),
            row,
            row,
        ],
        out_specs=[row] * 4,
        compiler_params=_cp("parallel"),
    )(dsub, w_out, gates, gates, y_fox, y_swa)


def _branch_bwd(dy, w_b, o, gates, g_blk, name, n_heads):
    s_len, d = dy.shape
    wd = w_b.shape[0]
    tm = min(512, s_len)

    def body(dy_ref, w_ref, o_ref, g_ref, do_ref, dg_ref, *rest):
        da = lax.dot_general(dy_ref[...], w_ref[...], _NT, preferred_element_type=F32)
        g = g_ref[...]
        sg = _sigmoid(g)
        do = da * (g * sg)
        do_ref[...] = do.astype(BF16)
        o = o_ref[...]
        dg_ref[...] = (da * o * (sg * (1.0 + g * (1.0 - sg)))).astype(BF16)
        if n_heads:
            prod = do * o
            lane = lax.broadcasted_iota(jnp.int32, (1, 128), 1)
            delta = jnp.zeros((tm, 128), F32)
            for h in range(n_heads):
                dh = jnp.sum(prod[:, h * 128 : (h + 1) * 128], axis=1, keepdims=True)
                delta = delta + jnp.where(lane == h, dh, 0.0)
            rest[0][...] = delta

    out_shape = [jax.ShapeDtypeStruct((s_len, wd), BF16), jax.ShapeDtypeStruct((s_len, wd), BF16)]
    out_specs = [pl.BlockSpec((tm, wd), lambda i: (i, 0))] * 2
    if n_heads:
        out_shape.append(jax.ShapeDtypeStruct((s_len, 128), F32))
        out_specs.append(pl.BlockSpec((tm, 128), lambda i: (i, 0)))
    return _pcall(
        body,
        name=name,
        grid=(s_len // tm,),
        out_shape=out_shape,
        in_specs=[
            pl.BlockSpec((tm, d), lambda i: (i, 0)),
            pl.BlockSpec((wd, d), lambda i: (0, 0)),
            pl.BlockSpec((tm, wd), lambda i: (i, 0)),
            pl.BlockSpec((tm, wd), lambda i: (i, g_blk)),
        ],
        out_specs=out_specs,
        compiler_params=_cp("parallel"),
    )(dy, w_b, o, gates)


def _in_bwd(dproj, w_in, x, ada, dza, ride):
    s_len, d = x.shape
    k_tot = dproj.shape[1]
    tm, tk = min(512, s_len), 1024
    ni, nk = s_len // tm, k_tot // tk
    n = len(ride)

    def body(dp_ref, w_ref, x_ref, sc_ref, dza_ref, *rest):
        ins, (gx_ref, red_ref), outs = rest[:n], rest[n : n + 2], rest[n + 2 : 2 * n + 2]
        sems, acc_s = rest[2 * n + 2 : 2 * n + 5], rest[2 * n + 5]
        i, kk = pl.program_id(0), pl.program_id(1)

        @pl.when((i == 0) & (kk == 0))
        def _():
            _rider_start("exchange", ins, outs, *sems)

        @pl.when((i == ni - 1) & (kk == nk - 1))
        def _():
            _rider_wait("exchange", ins, outs, *sems)

        part = lax.dot_general(dp_ref[...], w_ref[...], _NT, preferred_element_type=F32)

        @pl.when(kk == 0)
        def _():
            acc_s[...] = part

        @pl.when(kk > 0)
        def _():
            acc_s[...] += part

        @pl.when(kk == nk - 1)
        def _():
            dh = acc_s[...]
            xv = x_ref[...]
            mu = jnp.mean(xv, axis=-1, keepdims=True)
            xc = xv - mu
            var = jnp.mean(xc * xc, axis=-1, keepdims=True)
            rstd = lax.rsqrt(var + LN_EPS)
            xhat = xc * rstd
            dxhat = dh * (1.0 + sc_ref[...])
            dx = rstd * (dxhat - jnp.mean(dxhat, axis=-1, keepdims=True) - xhat * jnp.mean(dxhat * xhat, axis=-1, keepdims=True))
            gx_ref[...] = dza_ref[...] + dx
            part_r = jnp.concatenate(
                [jnp.sum(dh, axis=0, keepdims=True), jnp.sum(dh * xhat, axis=0, keepdims=True), jnp.zeros((6, d), F32)], axis=0
            )

            @pl.when(i == 0)
            def _():
                red_ref[...] = part_r

            @pl.when(i > 0)
            def _():
                red_ref[...] += part_r

    row = pl.BlockSpec((tm, d), lambda i, kk: (i, 0))
    hbm = pl.BlockSpec(memory_space=pltpu.HBM)
    return _pcall(
        body,
        name="in_bwd",
        grid=(ni, nk),
        out_shape=[jax.ShapeDtypeStruct((s_len, d), F32), jax.ShapeDtypeStruct((8, d), F32)]
        + [jax.ShapeDtypeStruct(r.shape, r.dtype) for r in ride],
        in_specs=[
            pl.BlockSpec((tm, tk), lambda i, kk: (i, kk)),
            pl.BlockSpec((d, tk), lambda i, kk: (0, kk)),
            row,
            pl.BlockSpec((1, d), lambda i, kk: (0, 1)),
            row,
        ]
        + [hbm] * n,
        out_specs=[row, pl.BlockSpec((8, d), lambda i, kk: (0, 0))] + [hbm] * n,
        scratch_shapes=_rider_scratch(n) + [pltpu.VMEM((tm, d), F32)],
        compiler_params=_cp("arbitrary", "arbitrary"),
    )(dproj, w_in, x, ada, dza, *ride)


def _pad_lanes(v, n):
    return jnp.pad(v, ((0, 0), (0, n - v.shape[1])))


def kernel(x, c, w_ada, b_ada, w_in, b_f, attn_sinks, w_br_fox, w_br_swa, w_out, ln_g, ln_b, loss_target, m_w_ada, m_b_ada, m_w_in, m_b_f, m_attn_sinks, m_w_br_fox, m_w_br_swa, m_w_out, m_ln_g, m_ln_b, v_w_ada, v_b_ada, v_w_in, v_b_f, v_attn_sinks, v_w_br_fox, v_w_br_swa, v_w_out, v_ln_g, v_ln_b):
    x2, tgt = x[0], loss_target[0]
    s_len, d = x2.shape
    me = 4 * lax.axis_index("x") + 2 * lax.axis_index("y") + lax.axis_index("c")
    off_ms = OFF_MF + d
    in_pad = off_ms + d
    c_ada = w_ada.shape[2]
    c_in = w_in.shape[2]
    c_br = w_br_fox.shape[2]

    w_in_g = _all_gather(w_in[0].astype(BF16), "ag_w_in", pltpu.HBM)
    w_in_full = w_in_g.reshape(N_DEV, d, c_in).transpose(1, 0, 2).reshape(d, N_DEV * c_in)
    w_in_pad = jnp.concatenate(
        [w_in_full[:, :REAL_FLOG_END], jnp.zeros((d, FLOG_PAD - N_FLOG), BF16), w_in_full[:, REAL_FLOG_END:]], axis=1
    )

    c_all = _gather_rows(c, "ag_c")
    b_cols = lax.dynamic_slice(b_ada, (0, me * c_ada), (1, c_ada))
    ada_cols = _ada_fwd(c_all, w_ada[0], b_cols)
    ada_g = _all_gather(ada_cols, "ag_ada", pltpu.VMEM).reshape(N_DEV, N_DEV, c_ada)
    ada = lax.dynamic_index_in_dim(ada_g, me, axis=1, keepdims=False).reshape(1, N_DEV * c_ada)

    h = _ln_mod(x2, ada)
    qkv_fox = _mm_cols(h, w_in_pad, OFF_FQ, 3 * FOX_W, BF16, "proj_fox")
    flog = _mm_cols(h, w_in_pad, OFF_FLOG, FLOG_PAD, F32, "proj_flog")
    qkv_swa = _mm_cols(h, w_in_pad, OFF_SQ, SWA_W + 2 * SWA_KVW, BF16, "proj_swa")
    gates, w_bf, w_bs, w_o = _mm_cols(
        h, w_in_pad, OFF_GF, in_pad - OFF_GF, F32, "proj_gates",
        ride=(w_br_fox[0].astype(BF16), w_br_swa[0].astype(BF16), w_out[0].astype(BF16)),
    )
    w_bf = w_bf.reshape(N_DEV, FOX_W, c_br).transpose(1, 0, 2).reshape(FOX_W, d)
    w_bs = w_bs.reshape(N_DEV, SWA_W, c_br).transpose(1, 0, 2).reshape(SWA_W, d)
    mf_blk = (OFF_MF - OFF_GF) // d

    flog_t = flog[:, :N_FLOG].T
    bf_col = b_f.reshape(FOX_H, 1)
    cum = _fox_cum(flog_t, bf_col)
    cum_row = cum.reshape(FOX_H, 1, s_len)
    o_fox, lse = _fox_fwd(qkv_fox, cum_row)
    sinks = attn_sinks.reshape(SWA_HQ)
    o_swa = _swa_fwd(qkv_swa, sinks)

    y_fox, a_fox = _branch_fwd(o_fox, gates, 0, w_bf, "branch_fox")
    y_swa, a_swa = _branch_fwd(o_swa, gates, 1, w_bs, "branch_swa")
    merged, dza, dsub, red = _out_stage(gates, mf_blk, y_fox, y_swa, w_o, x2, ada, ln_g, ln_b, tgt)
    loss = lax.psum(0.5 * red[4, 0] / d, ("x", "y", "c"))

    dmf, dms, dy_fox, dy_swa = _merge_bwd(dsub, w_o, gates, mf_blk, y_fox, y_swa)
    do_fox, dg_fox, delta = _branch_bwd(dy_fox, w_bf, o_fox, gates, 0, "branch_fox_bwd", FOX_H)
    do_swa, dg_swa = _branch_bwd(dy_swa, w_bs, o_swa, gates, 1, "branch_swa_bwd", 0)
    delta_row = delta[:, :FOX_H].T.reshape(FOX_H, 1, s_len)
    dq_f, dk_f, dv_f, dcol, drow = _fox_bwd(
        qkv_fox, cum.reshape(FOX_H, s_len, 1), lse.reshape(FOX_H, 1, s_len), delta_row, do_fox
    )
    dflog_t, dbf = _fox_gate_bwd(drow.reshape(FOX_H, s_len), dcol.reshape(FOX_H, s_len), flog_t, bf_col)
    dq_s, dk_s, dv_s, dsink = _swa_bwd(qkv_swa, sinks, do_swa)
    dflog = _pad_lanes(dflog_t.T, FLOG_PAD).astype(BF16)
    dproj = jnp.concatenate([dq_f.astype(BF16), dk_f, dv_f, dflog, dq_s, dk_s, dv_s, dg_fox, dg_swa, dmf, dms], axis=1)
    g_w_in = _mm_acc(h.T, dproj, "grad_w_in")
    g_w_in = jnp.concatenate([g_w_in[:, :REAL_FLOG_END], g_w_in[:, OFF_SQ:]], axis=1)
    g_w_bf = _mm_acc(a_fox.T, dy_fox, "grad_w_br_fox")
    g_w_bs = _mm_acc(a_swa.T, dy_swa, "grad_w_br_swa")
    g_w_o = _mm_acc(merged.T, dsub, "grad_w_out")

    grad_x, red2, r_in, r_bf, r_bs, r_o = _in_bwd(
        dproj, w_in_pad, x2, ada, dza,
        ride=(
            g_w_in.reshape(d, N_DEV, c_in).transpose(1, 0, 2),
            g_w_bf.reshape(FOX_W, N_DEV, c_br).transpose(1, 0, 2),
            g_w_bs.reshape(SWA_W, N_DEV, c_br).transpose(1, 0, 2),
            g_w_o.reshape(N_DEV, d // N_DEV, d),
        ),
    )
    out_w_in = _sum_adam(r_in, w_in[0], m_w_in[0], v_w_in[0], "adam_w_in")
    out_w_bf = _sum_adam(r_bf, w_br_fox[0], m_w_br_fox[0], v_w_br_fox[0], "adam_w_br_fox")
    out_w_bs = _sum_adam(r_bs, w_br_swa[0], m_w_br_swa[0], v_w_br_swa[0], "adam_w_br_swa")
    out_w_o = _sum_adam(r_o, w_out[0], m_w_out[0], v_w_out[0], "adam_w_out")

    packed = jnp.concatenate([red2[0:1], red2[1:2], red[0:1], _pad_lanes(dbf[:, 0].reshape(1, FOX_H), 128), dsink, red[1:2], red[2:3]], axis=1)
    gathered = _gather_rows(packed, "ag_small")
    pack = lambda a, b, cc, dd, e: jnp.concatenate([a, _pad_lanes(b, 128), _pad_lanes(cc, 128), dd, e], axis=1)
    small = _small_adam(
        gathered,
        pack(b_ada, b_f, attn_sinks, ln_g, ln_b),
        pack(m_b_ada, m_b_f, m_attn_sinks, m_ln_g, m_ln_b),
        pack(v_b_ada, v_b_f, v_attn_sinks, v_ln_g, v_ln_b),
    )
    dada_cols = lax.dynamic_slice(gathered, (0, me * c_ada), (N_DEV, c_ada))
    out_w_ada = _wada_adam(c_all.T, dada_cols, w_ada[0], m_w_ada[0], v_w_ada[0])

    o1, o2, o3 = 3 * d, 3 * d + 128, 3 * d + 256

    def unpack(p):
        return p[:, :o1], p[:, o1 : o1 + FOX_H], p[:, o2 : o2 + SWA_HQ], p[:, o3 : o3 + d], p[:, o3 + d : o3 + 2 * d]

    kinds = []
    for k in range(4):
        b_ada_k, b_f_k, sinks_k, ln_g_k, ln_b_k = unpack(small[k])
        kinds.append(
            [out_w_ada[k][None], b_ada_k, out_w_in[k][None], b_f_k, sinks_k, out_w_bf[k][None], out_w_bs[k][None], out_w_o[k][None], ln_g_k, ln_b_k]
        )
    return (loss, grad_x[None], *kinds[0], *kinds[1], *kinds[2], *kinds[3])
```

```python
import numpy as np
import jax
import jax.numpy as jnp
from jax import lax
from jax.experimental import pallas as pl
from jax.experimental.pallas import tpu as pltpu

F32 = jnp.float32
BF16 = jnp.bfloat16
N_DEV = 8
MESH = pl.DeviceIdType.MESH

FOX_H, FOX_DH, FOX_W = 8, 128, 1024
SWA_HQ, SWA_HKV, SWA_DH, SWA_G = 16, 4, 64, 4
SWA_W, SWA_KVW, WINDOW = 1024, 256, 128
LN_EPS = 1e-5
NEG = -1e30
DEPTH = 1
ALPHA = (2.0 * DEPTH) ** 0.25
FOX_SCALE = FOX_DH ** -0.5
SWA_SCALE = SWA_DH ** -0.5
SLOPES = [2.0 ** (-8.0 * (h + 1.0) / SWA_HQ) for h in range(SWA_HQ)]

ADAM_LR, ADAM_B1, ADAM_B2, ADAM_EPS, ADAM_WD, ADAM_STEP = 0.001, 0.9, 0.999, 1e-08, 0.01, 10

N_FLOG = 8
FLOG_PAD = 512
OFF_FQ, OFF_FK, OFF_FV, OFF_FLOG = 0, 1024, 2048, 3072
OFF_SQ = OFF_FLOG + FLOG_PAD
OFF_SK = OFF_SQ + SWA_W
OFF_SV = OFF_SK + SWA_KVW
OFF_GF = OFF_SV + SWA_KVW
OFF_GS = OFF_GF + FOX_W
OFF_MF = OFF_GS + SWA_W
REAL_FLOG_END = OFF_FLOG + N_FLOG

ATT_BLK = 512
VMEM_LIMIT = 52 * 1024 * 1024


def _pcall(body, **kw):
    return pl.pallas_call(body, **kw)


def _cp(*sem):
    return pltpu.CompilerParams(dimension_semantics=sem, vmem_limit_bytes=VMEM_LIMIT)


def _sigmoid(x):
    return 1.0 / (1.0 + jnp.exp(-x))


def _all_gather(x, name, space):
    m_per, n = x.shape

    def body(x_ref, out_ref, send_sems, recv_sems, local_sem):
        mx, my, mc = lax.axis_index("x"), lax.axis_index("y"), lax.axis_index("c")
        me, sibling = (mx, my, mc), (mx, my, 1 - mc)
        chips = [(1 - mx, my), (mx, 1 - my), (1 - mx, 1 - my)]

        def rows(px, py, pc):
            return out_ref.at[4 * px + 2 * py + pc]

        def copy(k, block, to, src=None):
            return pltpu.make_async_remote_copy(
                src_ref=rows(*block) if src is None else src,
                dst_ref=rows(*block),
                send_sem=send_sems.at[k],
                recv_sem=recv_sems.at[k],
                device_id=to,
                device_id_type=MESH,
            )

        mine = pltpu.make_async_copy(x_ref, rows(*me), local_sem)
        mine.start()
        first = [copy(0, me, sibling, src=x_ref)]
        first += [copy(1 + j, me, (*chip, mc), src=x_ref) for j, chip in enumerate(chips)]
        for cp in first:
            cp.start()
        passed = [copy(4 + j, (*chip, mc), sibling) for j, chip in enumerate(chips)]
        for j, chip in enumerate(chips):
            copy(1 + j, (*chip, mc), me).wait_recv()
            passed[j].start()
        copy(0, sibling, me).wait_recv()
        for j, chip in enumerate(chips):
            copy(4 + j, (*chip, 1 - mc), me).wait_recv()
        for cp in first + passed:
            cp.wait_send()
        mine.wait()

    return _pcall(
        body,
        name=name,
        out_shape=jax.ShapeDtypeStruct((N_DEV, m_per, n), x.dtype),
        in_specs=[pl.BlockSpec(memory_space=space)],
        out_specs=pl.BlockSpec(memory_space=space),
        scratch_shapes=[pltpu.SemaphoreType.DMA((7,)), pltpu.SemaphoreType.DMA((7,)), pltpu.SemaphoreType.DMA],
    )(x)


def _peer(d, mx, my, mc):
    return (1 - mx if (d >> 2) & 1 else mx, 1 - my if (d >> 1) & 1 else my, 1 - mc if d & 1 else mc)


def _rider_copies(kind, ins, outs, send_sems, recv_sems, local_sems):
    mx, my, mc = lax.axis_index("x"), lax.axis_index("y"), lax.axis_index("c")
    me = 4 * mx + 2 * my + mc
    remote, local = [], []
    for a in range(len(ins)):
        if kind == "gather":
            m_per = ins[a].shape[0]
            mine = outs[a].at[pl.ds(me * m_per, m_per), :]
            local.append(pltpu.make_async_copy(ins[a], mine, local_sems.at[a]))
        else:
            local.append(pltpu.make_async_copy(ins[a].at[me], outs[a].at[0], local_sems.at[a]))
        for d in range(1, N_DEV):
            px, py, pc = _peer(d, mx, my, mc)
            if kind == "gather":
                src, dst = ins[a], mine
            else:
                src, dst = ins[a].at[4 * px + 2 * py + pc], outs[a].at[d]
            remote.append(
                pltpu.make_async_remote_copy(
                    src_ref=src,
                    dst_ref=dst,
                    send_sem=send_sems.at[a * 7 + d - 1],
                    recv_sem=recv_sems.at[a * 7 + d - 1],
                    device_id=(px, py, pc),
                    device_id_type=MESH,
                )
            )
    return remote, local


def _rider_start(*args):
    remote, local = _rider_copies(*args)
    for cp in local + remote:
        cp.start()


def _rider_wait(*args):
    remote, local = _rider_copies(*args)
    for cp in remote:
        cp.wait_recv()
    for cp in remote:
        cp.wait_send()
    for cp in local:
        cp.wait()


def _rider_scratch(n):
    return [pltpu.SemaphoreType.DMA((7 * n,)), pltpu.SemaphoreType.DMA((7 * n,)), pltpu.SemaphoreType.DMA((n,))]


def _gather_rows(v, name):
    n = v.shape[1]
    return _all_gather(jnp.broadcast_to(v, (8, n)), name, pltpu.VMEM)[:, 0, :]


def _adamw(w, g, m, v):
    m = ADAM_B1 * m + (1.0 - ADAM_B1) * g
    v = ADAM_B2 * v + (1.0 - ADAM_B2) * (g * g)
    m_hat = m / (1.0 - ADAM_B1**ADAM_STEP)
    v_hat = v / (1.0 - ADAM_B2**ADAM_STEP)
    delta = -ADAM_LR * (m_hat / (jnp.sqrt(v_hat) + ADAM_EPS) + ADAM_WD * w)
    return delta, m, v


def _sum_adam(recv, w, m, v, name):
    _, r_tot, c = recv.shape
    c_pad = -(-c // 128) * 128
    tr = r_tot
    while 8 * tr * c_pad * 4 > 6 * 1024 * 1024 and tr % 32 == 0:
        tr //= 2

    def body(r_ref, w_ref, m_ref, v_ref, g_ref, d_ref, nm_ref, nv_ref):
        g = r_ref[0].astype(F32)
        for k in range(1, N_DEV):
            g = g + r_ref[k].astype(F32)
        d, nm, nv = _adamw(w_ref[...], g, m_ref[...], v_ref[...])
        g_ref[...] = g
        d_ref[...] = d
        nm_ref[...] = nm
        nv_ref[...] = nv

    blk = pl.BlockSpec((tr, c), lambda i: (i, 0))
    return _pcall(
        body,
        name=name,
        grid=(r_tot // tr,),
        out_shape=[jax.ShapeDtypeStruct((r_tot, c), F32)] * 4,
        in_specs=[pl.BlockSpec((N_DEV, tr, c), lambda i: (0, i, 0)), blk, blk, blk],
        out_specs=[blk] * 4,
        compiler_params=_cp("parallel"),
    )(recv, w, m, v)


def _sum_adam_t(recv, w, m, v, name):
    _, c, r_tot = recv.shape
    tr = min(256, r_tot)

    def body(r_ref, w_ref, m_ref, v_ref, g_ref, d_ref, nm_ref, nv_ref):
        g = r_ref[0].astype(F32)
        for k in range(1, N_DEV):
            g = g + r_ref[k].astype(F32)
        d, nm, nv = _adamw(w_ref[...], g, m_ref[...], v_ref[...])
        g_ref[...] = g
        d_ref[...] = d
        nm_ref[...] = nm
        nv_ref[...] = nv

    blk = pl.BlockSpec((c, tr), lambda i: (0, i))
    return _pcall(
        body,
        name=name,
        grid=(r_tot // tr,),
        out_shape=[jax.ShapeDtypeStruct((c, r_tot), F32)] * 4,
        in_specs=[pl.BlockSpec((N_DEV, c, tr), lambda i: (0, 0, i)), blk, blk, blk],
        out_specs=[blk] * 4,
        compiler_params=_cp("parallel"),
    )(recv, w, m, v)


def _wada_adam(c_t, dada_cols, w, m, v):
    d_model, c = w.shape
    tr = min(256, d_model)

    def body(ct_ref, da_ref, w_ref, m_ref, v_ref, g_ref, d_ref, nm_ref, nv_ref):
        g = jnp.dot(ct_ref[...].astype(BF16), da_ref[...].astype(BF16), preferred_element_type=F32)
        d, nm, nv = _adamw(w_ref[...], g, m_ref[...], v_ref[...])
        g_ref[...] = g
        d_ref[...] = d
        nm_ref[...] = nm
        nv_ref[...] = nv

    blk = pl.BlockSpec((tr, c), lambda i: (i, 0))
    return _pcall(
        body,
        name="wada_adam",
        grid=(d_model // tr,),
        out_shape=[jax.ShapeDtypeStruct((d_model, c), F32)] * 4,
        in_specs=[pl.BlockSpec((tr, N_DEV), lambda i: (i, 0)), pl.BlockSpec((N_DEV, c), lambda i: (0, 0)), blk, blk, blk],
        out_specs=[blk] * 4,
        compiler_params=_cp("parallel"),
    )(c_t, dada_cols, w, m, v)


def _small_adam(gathered, w, m, v):
    p = w.shape[1]

    def body(a_ref, w_ref, m_ref, v_ref, g_ref, d_ref, nm_ref, nv_ref):
        g = a_ref[0:1, :]
        for k in range(1, N_DEV):
            g = g + a_ref[k : k + 1, :]
        d, nm, nv = _adamw(w_ref[...], g, m_ref[...], v_ref[...])
        g_ref[...] = g
        d_ref[...] = d
        nm_ref[...] = nm
        nv_ref[...] = nv

    return _pcall(
        body,
        name="small_adam",
        out_shape=[jax.ShapeDtypeStruct((1, p), F32)] * 4,
    )(gathered, w, m, v)


def _ada_fwd(c_all, w_ada, b_cols):
    c = w_ada.shape[1]

    def body(c_ref, w_ref, b_ref, o_ref):
        o_ref[...] = jnp.dot(c_ref[...].astype(BF16), w_ref[...].astype(BF16), preferred_element_type=F32) + b_ref[...]

    return _pcall(
        body,
        name="ada_fwd",
        out_shape=jax.ShapeDtypeStruct((N_DEV, c), F32),
        compiler_params=_cp(),
    )(c_all, w_ada, b_cols)


def _ln_mod(x, ada):
    s_len, d = x.shape
    tm = min(512, s_len)

    def body(x_ref, sh_ref, sc_ref, h_ref):
        xv = x_ref[...]
        mu = jnp.mean(xv, axis=-1, keepdims=True)
        xc = xv - mu
        var = jnp.mean(xc * xc, axis=-1, keepdims=True)
        xhat = xc * lax.rsqrt(var + LN_EPS)
        h_ref[...] = (xhat * (1.0 + sc_ref[...]) + sh_ref[...]).astype(BF16)

    return _pcall(
        body,
        name="ln_mod",
        grid=(s_len // tm,),
        out_shape=jax.ShapeDtypeStruct((s_len, d), BF16),
        in_specs=[
            pl.BlockSpec((tm, d), lambda i: (i, 0)),
            pl.BlockSpec((1, d), lambda i: (0, 0)),
            pl.BlockSpec((1, d), lambda i: (0, 1)),
        ],
        out_specs=pl.BlockSpec((tm, d), lambda i: (i, 0)),
        compiler_params=_cp("parallel"),
    )(x, ada, ada)


def _mm_cols(a, b, col_off, n_cols, out_dtype, name, ride=()):
    m, k = a.shape
    tm, tn = min(1024, m), 512
    off = col_off // tn
    ni, nj = m // tm, n_cols // tn
    n = len(ride)

    def body(a_ref, b_ref, *rest):
        ins, o_ref, outs, sems = rest[:n], rest[n], rest[n + 1 : 2 * n + 1], rest[2 * n + 1 :]
        i, j = pl.program_id(0), pl.program_id(1)
        if n:

            @pl.when((i == 0) & (j == 0))
            def _():
                _rider_start("gather", ins, outs, *sems)

        o_ref[...] = lax.dot_general(a_ref[...], b_ref[...], _NT, preferred_element_type=F32).astype(out_dtype)
        if n:

            @pl.when((i == ni - 1) & (j == nj - 1))
            def _():
                _rider_wait("gather", ins, outs, *sems)

    hbm = pl.BlockSpec(memory_space=pltpu.HBM)
    out = _pcall(
        body,
        name=name,
        grid=(ni, nj),
        out_shape=[jax.ShapeDtypeStruct((m, n_cols), out_dtype)]
        + [jax.ShapeDtypeStruct((N_DEV * r.shape[0], r.shape[1]), r.dtype) for r in ride],
        in_specs=[pl.BlockSpec((tm, k), lambda i, j: (i, 0)), pl.BlockSpec((tn, k), lambda i, j: (off + j, 0))] + [hbm] * n,
        out_specs=[pl.BlockSpec((tm, tn), lambda i, j: (i, j))] + [hbm] * n,
        scratch_shapes=_rider_scratch(n) if n else [],
        compiler_params=_cp("arbitrary", "arbitrary") if n else _cp("parallel", "parallel"),
    )(a, b, *ride)
    return out if n else out[0]


def _mm_tn(a, b, name):
    s_len, m = a.shape
    n = b.shape[1]
    tm, tn, ts = min(1024, m), min(1024, n), min(1024, s_len)
    ns = s_len // ts

    def body(a_ref, b_ref, o_ref, acc_s):
        kk = pl.program_id(2)
        part = lax.dot_general(a_ref[...], b_ref[...], _TN, preferred_element_type=F32)

        @pl.when(kk == 0)
        def _():
            acc_s[...] = part

        @pl.when(kk > 0)
        def _():
            acc_s[...] += part

        @pl.when(kk == ns - 1)
        def _():
            o_ref[...] = acc_s[...].astype(BF16)

    return _pcall(
        body,
        name=name,
        grid=(m // tm, n // tn, ns),
        out_shape=jax.ShapeDtypeStruct((m, n), BF16),
        in_specs=[pl.BlockSpec((ts, tm), lambda i, j, kk: (kk, i)), pl.BlockSpec((ts, tn), lambda i, j, kk: (kk, j))],
        out_specs=pl.BlockSpec((tm, tn), lambda i, j, kk: (i, j)),
        scratch_shapes=[pltpu.VMEM((tm, tn), F32)],
        compiler_params=_cp("parallel", "parallel", "arbitrary"),
    )(a, b)


def _split3(a):
    hi = a.astype(BF16)
    r1 = a - hi.astype(F32)
    mid = r1.astype(BF16)
    lo = (r1 - mid.astype(F32)).astype(BF16)
    return hi, mid, lo


def _dot_ones(a, tri):
    return sum(jnp.dot(t, tri, preferred_element_type=F32) for t in _split3(a))


def _log_sigmoid(x):
    return jnp.minimum(x, 0.0) - jnp.log1p(jnp.exp(-jnp.abs(x)))


def _fox_cum(flog_t, bf_col):
    s_len = flog_t.shape[1]

    def body(fl_ref, bf_ref, cum_ref):
        r = lax.broadcasted_iota(jnp.int32, (128, 128), 0)
        c = lax.broadcasted_iota(jnp.int32, (128, 128), 1)
        upper = (r <= c).astype(BF16)

        def step(t, carry):
            sl = pl.ds(pl.multiple_of(t * 128, 128), 128)
            lf = _log_sigmoid(fl_ref[:, sl] + bf_ref[...])
            cs = _dot_ones(lf, upper) + carry
            cum_ref[:, sl] = cs
            return cs[:, 127:128]

        lax.fori_loop(0, s_len // 128, step, jnp.zeros((FOX_H, 1), F32))

    return _pcall(body, name="fox_cum", out_shape=jax.ShapeDtypeStruct((FOX_H, s_len), F32))(flog_t, bf_col)


def _fox_gate_bwd(drow, dcol, flog_t, bf_col):
    s_len = flog_t.shape[1]
    n = s_len // 128

    def body(dr_ref, dc_ref, fl_ref, bf_ref, dfl_ref, dbf_ref):
        r = lax.broadcasted_iota(jnp.int32, (128, 128), 0)
        c = lax.broadcasted_iota(jnp.int32, (128, 128), 1)
        lower = (r >= c).astype(BF16)

        def step(t, carry):
            run, tot = carry
            sl = pl.ds(pl.multiple_of((n - 1 - t) * 128, 128), 128)
            rc = _dot_ones(dr_ref[:, sl] - dc_ref[:, sl], lower) + run
            dfl = rc * _sigmoid(-(fl_ref[:, sl] + bf_ref[...]))
            dfl_ref[:, sl] = dfl
            return rc[:, 0:1], tot + jnp.sum(dfl, axis=1, keepdims=True)

        zero = jnp.zeros((FOX_H, 1), F32)
        _, tot = lax.fori_loop(0, n, step, (zero, zero))
        dbf_ref[...] = jnp.broadcast_to(tot, (FOX_H, 128))

    return _pcall(
        body,
        name="fox_gate_bwd",
        out_shape=[jax.ShapeDtypeStruct((FOX_H, s_len), F32), jax.ShapeDtypeStruct((FOX_H, 128), F32)],
    )(drow, dcol, flog_t, bf_col)


def _diag_mask(blk, transposed=False):
    r = lax.broadcasted_iota(jnp.int32, (blk, blk), 0)
    c = lax.broadcasted_iota(jnp.int32, (blk, blk), 1)
    return c >= r if transposed else r >= c


_NT = (((1,), (1,)), ((), ()))
_TN = (((0,), (0,)), ((), ()))


def _fox_fwd(qkv, cum_row):
    s_len = qkv.shape[0]
    blk = min(ATT_BLK, s_len)
    nb = s_len // blk
    log2e = 1.4426950408889634

    def body(q_ref, k_ref, v_ref, c_ref, o_ref, lse_ref, mx_s, acc_s):
        i = pl.program_id(1)

        def logits(j, masked):
            cols = pl.ds(pl.multiple_of(j * blk, blk), blk)
            u = lax.dot_general(q_ref[...], k_ref[cols, :], _NT, preferred_element_type=F32) - c_ref[:, cols] * (1.0 / FOX_SCALE)
            if masked:
                u = jnp.where(_diag_mask(blk), u, NEG)
            return u, cols

        def lane_max(j, masked):
            u, _ = logits(j, masked)
            part = u[:, 0:128]
            for t in range(1, blk // 128):
                part = jnp.maximum(part, u[:, t * 128 : (t + 1) * 128])
            mx_s[...] = jnp.maximum(mx_s[...], part)

        mx_s[...] = jnp.full(mx_s.shape, NEG, F32)
        lax.fori_loop(0, i, lambda j, c: (lane_max(j, False), c)[1], 0)
        lane_max(i, True)
        m = jnp.max(mx_s[...], axis=1, keepdims=True)

        ones_col = (lax.broadcasted_iota(jnp.int32, (blk, 128), 1) == 0).astype(BF16)

        def weigh(j, masked):
            u, cols = logits(j, masked)
            p = jnp.exp2((u - m) * (FOX_SCALE * log2e))
            v1 = jnp.concatenate([v_ref[cols, :], ones_col], axis=1)
            acc_s[...] += jnp.dot(p.astype(BF16), v1, preferred_element_type=F32)

        acc_s[...] = jnp.zeros(acc_s.shape, F32)
        lax.fori_loop(0, i, lambda j, c: (weigh(j, False), c)[1], 0)
        weigh(i, True)
        l = acc_s[:, FOX_DH : FOX_DH + 1]
        o_ref[...] = acc_s[:, :FOX_DH] / l
        lse_ref[...] = m * FOX_SCALE + jnp.log(l)

    return _pcall(
        body,
        name="fox_fwd",
        grid=(FOX_H, nb),
        out_shape=[jax.ShapeDtypeStruct((s_len, FOX_W), F32), jax.ShapeDtypeStruct((FOX_H, s_len, 1), F32)],
        in_specs=[
            pl.BlockSpec((blk, FOX_DH), lambda h, i: (i, h)),
            pl.BlockSpec((s_len, FOX_DH), lambda h, i: (0, FOX_H + h)),
            pl.BlockSpec((s_len, FOX_DH), lambda h, i: (0, 2 * FOX_H + h)),
            pl.BlockSpec((None, 1, s_len), lambda h, i: (h, 0, 0)),
        ],
        out_specs=[
            pl.BlockSpec((blk, FOX_DH), lambda h, i: (i, h)),
            pl.BlockSpec((None, blk, 1), lambda h, i: (h, i, 0)),
        ],
        scratch_shapes=[pltpu.VMEM((blk, 128), F32), pltpu.VMEM((blk, 2 * FOX_DH), F32)],
        compiler_params=_cp("parallel", "arbitrary"),
    )(qkv, qkv, qkv, cum_row)


def _fox_bwd(qkv, cum_col, lse_row, delta_row, do):
    s_len = qkv.shape[0]
    blk = min(ATT_BLK, s_len)
    nb = s_len // blk

    def body(q_ref, k_ref, v_ref, c_ref, lse_ref, dl_ref, do_ref, dq_ref, dk_ref, dv_ref, dc_ref, dr_ref, dk_s, dv_s, dc_s, cb_s):
        j = pl.program_id(1)

        @pl.when(j == 0)
        def _():
            dq_ref[...] = jnp.zeros(dq_ref.shape, F32)
            dr_ref[...] = jnp.zeros(dr_ref.shape, F32)

        dk_s[...] = jnp.zeros(dk_s.shape, F32)
        dv_s[...] = jnp.zeros(dv_s.shape, F32)
        dc_s[...] = jnp.zeros(dc_s.shape, F32)
        cb_s[...] = jnp.broadcast_to(c_ref[...], cb_s.shape)

        def tile(i, diag):
            rows = pl.ds(pl.multiple_of(i * blk, blk), blk)
            q, dob = q_ref[rows, :], do_ref[rows, :]
            k, v = k_ref[...], v_ref[...]
            s_t = lax.dot_general(k, q, _NT, preferred_element_type=F32) * FOX_SCALE - cb_s[...]
            p_t = jnp.exp(s_t - lse_ref[:, rows])
            if diag:
                p_t = jnp.where(_diag_mask(blk, transposed=True), p_t, 0.0)
            dp_t = lax.dot_general(v, dob, _NT, preferred_element_type=F32)
            ds_t = p_t * (dp_t - dl_ref[:, rows])
            dsb = ds_t.astype(BF16)
            dv_s[...] += jnp.dot(p_t.astype(BF16), dob, preferred_element_type=F32)
            dk_s[...] += jnp.dot(dsb, q, preferred_element_type=F32)
            dq_c = lax.dot_general(dsb, k, _TN, preferred_element_type=F32)
            part = ds_t[:, 0:128]
            for t in range(1, blk // 128):
                part = part + ds_t[:, t * 128 : (t + 1) * 128]
            dc_s[...] += part
            dr_ref[:, rows] += jnp.sum(ds_t, axis=0, keepdims=True)
            if diag:
                dq_ref[rows, :] = (dq_ref[rows, :] + dq_c) * FOX_SCALE
            else:
                dq_ref[rows, :] += dq_c

        tile(j, True)

        def below(i, carry):
            tile(i, False)
            return carry

        lax.fori_loop(j + 1, nb, below, 0)
        dk_ref[...] = (dk_s[...] * FOX_SCALE).astype(BF16)
        dv_ref[...] = dv_s[...].astype(BF16)
        dc_ref[...] = jnp.sum(dc_s[...], axis=1, keepdims=True)

    head = lambda h, j: (0, h)
    row = pl.BlockSpec((None, 1, s_len), lambda h, j: (h, 0, 0))
    return _pcall(
        body,
        name="fox_bwd",
        grid=(FOX_H, nb),
        out_shape=[
            jax.ShapeDtypeStruct((s_len, FOX_W), F32),
            jax.ShapeDtypeStruct((s_len, FOX_W), BF16),
            jax.ShapeDtypeStruct((s_len, FOX_W), BF16),
            jax.ShapeDtypeStruct((FOX_H, s_len, 1), F32),
            jax.ShapeDtypeStruct((FOX_H, 1, s_len), F32),
        ],
        in_specs=[
            pl.BlockSpec((s_len, FOX_DH), head),
            pl.BlockSpec((blk, FOX_DH), lambda h, j: (j, FOX_H + h)),
            pl.BlockSpec((blk, FOX_DH), lambda h, j: (j, 2 * FOX_H + h)),
            pl.BlockSpec((None, blk, 1), lambda h, j: (h, j, 0)),
            row,
            row,
            pl.BlockSpec((s_len, FOX_DH), head),
        ],
        out_specs=[
            pl.BlockSpec((s_len, FOX_DH), head),
            pl.BlockSpec((blk, FOX_DH), lambda h, j: (j, h)),
            pl.BlockSpec((blk, FOX_DH), lambda h, j: (j, h)),
            pl.BlockSpec((None, blk, 1), lambda h, j: (h, j, 0)),
            row,
        ],
        scratch_shapes=[
            pltpu.VMEM((blk, FOX_DH), F32),
            pltpu.VMEM((blk, FOX_DH), F32),
            pltpu.VMEM((blk, 128), F32),
            pltpu.VMEM((blk, blk), F32),
        ],
        compiler_params=_cp("parallel", "arbitrary"),
    )(qkv, qkv, qkv, cum_col, lse_row, delta_row, do)


def _swa_group(i, q_ref, kk, sinks_ref, g):
    rows = SWA_G * WINDOW
    r = lax.broadcasted_iota(jnp.int32, (rows, 2 * WINDOW), 0)
    c = lax.broadcasted_iota(jnp.int32, (rows, 2 * WINDOW), 1)
    dist = (r & (WINDOW - 1)) - c + WINDOW
    valid = (dist >= 0) & (dist < WINDOW) & ((c >= WINDOW) | (i > 0))
    head = lax.broadcasted_iota(jnp.int32, (rows, 1), 0) // WINDOW
    slope = jnp.zeros((rows, 1), F32)
    sink = jnp.zeros((rows, 1), F32)
    for t in range(SWA_G):
        h = g * SWA_G + t
        slope = jnp.where(head == t, SLOPES[h], slope)
        sink = jnp.where(head == t, sinks_ref[h], sink)
    q = jnp.concatenate([q_ref[:, (g * SWA_G + t) * SWA_DH : (g * SWA_G + t + 1) * SWA_DH] for t in range(SWA_G)], axis=0)
    k = kk[:, g * SWA_DH : (g + 1) * SWA_DH]
    s = lax.dot_general(q, k, _NT, preferred_element_type=F32) * SWA_SCALE - slope * dist.astype(F32)
    s = jnp.where(valid, s, NEG)
    m = jnp.maximum(jnp.max(s, axis=1, keepdims=True), sink)
    e = jnp.exp(s - m)
    e_sink = jnp.exp(sink - m)
    inv = 1.0 / (jnp.sum(e, axis=1, keepdims=True) + e_sink)
    return q, k, e * inv, e_sink * inv


def _swa_specs(col_q, col_k, col_v, rev, nb):
    def blk(t):
        return nb - 1 - t if rev else t

    return [
        pl.BlockSpec((WINDOW, SWA_W), lambda t: (blk(t), col_q)),
        pl.BlockSpec((WINDOW, SWA_KVW), lambda t: (jnp.maximum(blk(t) - 1, 0), col_k)),
        pl.BlockSpec((WINDOW, SWA_KVW), lambda t: (blk(t), col_k)),
        pl.BlockSpec((WINDOW, SWA_KVW), lambda t: (jnp.maximum(blk(t) - 1, 0), col_v)),
        pl.BlockSpec((WINDOW, SWA_KVW), lambda t: (blk(t), col_v)),
    ]


def _swa_fwd(qkv, sinks):
    s_len = qkv.shape[0]
    nb = s_len // WINDOW

    def body(q_ref, kp_ref, kc_ref, vp_ref, vc_ref, sinks_ref, o_ref):
        i = pl.program_id(0)
        kk = jnp.concatenate([kp_ref[...], kc_ref[...]], axis=0)
        vv = jnp.concatenate([vp_ref[...], vc_ref[...]], axis=0)
        for g in range(SWA_HKV):
            _, _, p, _ = _swa_group(i, q_ref, kk, sinks_ref, g)
            o = jnp.dot(p.astype(BF16), vv[:, g * SWA_DH : (g + 1) * SWA_DH], preferred_element_type=F32)
            for t in range(SWA_G):
                h = g * SWA_G + t
                o_ref[:, h * SWA_DH : (h + 1) * SWA_DH] = o[t * WINDOW : (t + 1) * WINDOW, :]

    return _pcall(
        body,
        name="swa_fwd",
        grid=(nb,),
        out_shape=jax.ShapeDtypeStruct((s_len, SWA_W), F32),
        in_specs=_swa_specs(0, 4, 5, False, nb) + [pl.BlockSpec(memory_space=pltpu.SMEM)],
        out_specs=pl.BlockSpec((WINDOW, SWA_W), lambda t: (t, 0)),
        compiler_params=_cp("parallel"),
    )(qkv, qkv, qkv, qkv, qkv, sinks)


def _swa_bwd(qkv, sinks, do):
    s_len = qkv.shape[0]
    nb = s_len // WINDOW

    def body(q_ref, kp_ref, kc_ref, vp_ref, vc_ref, sinks_ref, do_ref, dq_ref, dk_ref, dv_ref, dsink_ref, ck_s, cv_s, dkk_s, dvv_s):
        t = pl.program_id(0)
        i = nb - 1 - t

        @pl.when(t == 0)
        def _():
            ck_s[...] = jnp.zeros(ck_s.shape, F32)
            cv_s[...] = jnp.zeros(cv_s.shape, F32)
            dsink_ref[...] = jnp.zeros(dsink_ref.shape, F32)

        kk = jnp.concatenate([kp_ref[...], kc_ref[...]], axis=0)
        vv = jnp.concatenate([vp_ref[...], vc_ref[...]], axis=0)
        lane = lax.broadcasted_iota(jnp.int32, (1, 128), 1)
        dsink = jnp.zeros((1, 128), F32)
        for g in range(SWA_HKV):
            cols = slice(g * SWA_DH, (g + 1) * SWA_DH)
            q, k, p, p_sink = _swa_group(i, q_ref, kk, sinks_ref, g)
            dob = jnp.concatenate([do_ref[:, (g * SWA_G + t) * SWA_DH : (g * SWA_G + t + 1) * SWA_DH] for t in range(SWA_G)], axis=0)
            dp = lax.dot_general(dob, vv[:, cols], _NT, preferred_element_type=F32)
            delta = jnp.sum(p * dp, axis=1, keepdims=True)
            dsb = (p * (dp - delta)).astype(BF16)
            dq = (jnp.dot(dsb, k, preferred_element_type=F32) * SWA_SCALE).astype(BF16)
            ps_d = p_sink * delta
            for t in range(SWA_G):
                h = g * SWA_G + t
                dq_ref[:, h * SWA_DH : (h + 1) * SWA_DH] = dq[t * WINDOW : (t + 1) * WINDOW, :]
                dsink = dsink + jnp.where(lane == h, -jnp.sum(ps_d[t * WINDOW : (t + 1) * WINDOW, :], axis=0, keepdims=True), 0.0)
            dkk_s[:, cols] = lax.dot_general(dsb, q, _TN, preferred_element_type=F32) * SWA_SCALE
            dvv_s[:, cols] = lax.dot_general(p.astype(BF16), dob, _TN, preferred_element_type=F32)
        dk_ref[...] = (dkk_s[WINDOW:, :] + ck_s[...]).astype(BF16)
        dv_ref[...] = (dvv_s[WINDOW:, :] + cv_s[...]).astype(BF16)
        ck_s[...] = dkk_s[:WINDOW, :]
        cv_s[...] = dvv_s[:WINDOW, :]
        dsink_ref[...] += dsink

    row = lambda t: (nb - 1 - t, 0)
    return _pcall(
        body,
        name="swa_bwd",
        grid=(nb,),
        out_shape=[
            jax.ShapeDtypeStruct((s_len, SWA_W), BF16),
            jax.ShapeDtypeStruct((s_len, SWA_KVW), BF16),
            jax.ShapeDtypeStruct((s_len, SWA_KVW), BF16),
            jax.ShapeDtypeStruct((1, 128), F32),
        ],
        in_specs=_swa_specs(0, 4, 5, True, nb)
        + [pl.BlockSpec(memory_space=pltpu.SMEM), pl.BlockSpec((WINDOW, SWA_W), row)],
        out_specs=[
            pl.BlockSpec((WINDOW, SWA_W), row),
            pl.BlockSpec((WINDOW, SWA_KVW), row),
            pl.BlockSpec((WINDOW, SWA_KVW), row),
            pl.BlockSpec((1, 128), lambda t: (0, 0)),
        ],
        scratch_shapes=[
            pltpu.VMEM((WINDOW, SWA_KVW), F32),
            pltpu.VMEM((WINDOW, SWA_KVW), F32),
            pltpu.VMEM((2 * WINDOW, SWA_KVW), F32),
            pltpu.VMEM((2 * WINDOW, SWA_KVW), F32),
        ],
        compiler_params=_cp("arbitrary"),
    )(qkv, qkv, qkv, qkv, qkv, sinks, do)


def _branch_fwd(o, gates, g_blk, w_b, name):
    s_len, wd = o.shape
    d = w_b.shape[1]
    tm = min(512, s_len)

    def body(o_ref, g_ref, w_ref, y_ref, a_ref):
        g = g_ref[...]
        a = (o_ref[...] * (g * _sigmoid(g))).astype(BF16)
        a_ref[...] = a
        y_ref[...] = jnp.dot(a, w_ref[...], preferred_element_type=F32)

    return _pcall(
        body,
        name=name,
        grid=(s_len // tm,),
        out_shape=[jax.ShapeDtypeStruct((s_len, d), F32), jax.ShapeDtypeStruct((s_len, wd), BF16)],
        in_specs=[
            pl.BlockSpec((tm, wd), lambda i: (i, 0)),
            pl.BlockSpec((tm, wd), lambda i: (i, g_blk)),
            pl.BlockSpec((wd, d), lambda i: (0, 0)),
        ],
        out_specs=[pl.BlockSpec((tm, d), lambda i: (i, 0)), pl.BlockSpec((tm, wd), lambda i: (i, 0))],
        compiler_params=_cp("parallel"),
    )(o, gates, w_b)


def _out_stage(gates, mf_blk, y_fox, y_swa, w_out, x, ada, ln_g, ln_b, target):
    s_len, d = x.shape
    tm = min(128, s_len)
    n_steps = s_len // tm

    def body(mf_ref, ms_ref, yf_ref, ys_ref, w_ref, x_ref, gate_ref, lg_ref, lb_ref, t_ref, mg_ref, dza_ref, dsub_ref, red_ref):
        i = pl.program_id(0)
        merged = _sigmoid(mf_ref[...]) * yf_ref[...] + _sigmoid(ms_ref[...]) * ys_ref[...]
        mb = merged.astype(BF16)
        mg_ref[...] = mb
        sub = jnp.dot(mb, w_ref[...], preferred_element_type=F32)
        gate = gate_ref[...]
        z = ALPHA * x_ref[...] + gate * sub
        mu = jnp.mean(z, axis=-1, keepdims=True)
        zc = z - mu
        var = jnp.mean(zc * zc, axis=-1, keepdims=True)
        rstd = lax.rsqrt(var + LN_EPS)
        zhat = zc * rstd
        err = zhat * lg_ref[...] + lb_ref[...] - t_ref[...]
        dout = err * (1.0 / d)
        dzhat = dout * lg_ref[...]
        dz = rstd * (dzhat - jnp.mean(dzhat, axis=-1, keepdims=True) - zhat * jnp.mean(dzhat * zhat, axis=-1, keepdims=True))
        dza_ref[...] = ALPHA * dz
        dsub_ref[...] = (gate * dz).astype(BF16)
        part = jnp.concatenate(
            [
                jnp.sum(dz * sub, axis=0, keepdims=True),
                jnp.sum(dout * zhat, axis=0, keepdims=True),
                jnp.sum(dout, axis=0, keepdims=True),
                jnp.sum(err * err, axis=0, keepdims=True),
                jnp.zeros((4, d), F32),
            ],
            axis=0,
        )

        @pl.when(i == 0)
        def _():
            red_ref[...] = part

        @pl.when(i > 0)
        def _():
            red_ref[...] += part

        @pl.when(i == n_steps - 1)
        def _():
            red_ref[4:5, :] = jnp.broadcast_to(jnp.sum(red_ref[3:4, :], axis=1, keepdims=True), (1, d))

    row = pl.BlockSpec((tm, d), lambda i: (i, 0))
    vec = pl.BlockSpec((1, d), lambda i: (0, 0))
    return _pcall(
        body,
        name="out_stage",
        grid=(n_steps,),
        out_shape=[
            jax.ShapeDtypeStruct((s_len, d), BF16),
            jax.ShapeDtypeStruct((s_len, d), F32),
            jax.ShapeDtypeStruct((s_len, d), BF16),
            jax.ShapeDtypeStruct((8, d), F32),
        ],
        in_specs=[
            pl.BlockSpec((tm, d), lambda i: (i, mf_blk)),
            pl.BlockSpec((tm, d), lambda i: (i, mf_blk + 1)),
            row,
            row,
            pl.BlockSpec((d, d), lambda i: (0, 0)),
            row,
            pl.BlockSpec((1, d), lambda i: (0, 2)),
            vec,
            vec,
            row,
        ],
        out_specs=[row, row, row, pl.BlockSpec((8, d), lambda i: (0, 0))],
        compiler_params=_cp("arbitrary"),
    )(gates, gates, y_fox, y_swa, w_out, x, ada, ln_g, ln_b, target)


def _merge_bwd(dsub, w_out, gates, mf_blk, y_fox, y_swa):
    s_len, d = dsub.shape
    tm = min(128, s_len)

    def body(ds_ref, w_ref, mf_ref, ms_ref, yf_ref, ys_ref, dmf_ref, dms_ref, dyf_ref, dys_ref):
        dm = lax.dot_general(ds_ref[...], w_ref[...], _NT, preferred_element_type=F32)
        sf, ss = _sigmoid(mf_ref[...]), _sigmoid(ms_ref[...])
        dmf_ref[...] = (dm * yf_ref[...] * (sf * (1.0 - sf))).astype(BF16)
        dms_ref[...] = (dm * ys_ref[...] * (ss * (1.0 - ss))).astype(BF16)
        dyf_ref[...] = (dm * sf).astype(BF16)
        dys_ref[...] = (dm * ss).astype(BF16)

    row = pl.BlockSpec((tm, d), lambda i: (i, 0))
    return _pcall(
        body,
        name="merge_bwd",
        grid=(s_len // tm,),
        out_shape=[jax.ShapeDtypeStruct((s_len, d), BF16)] * 4,
        in_specs=[
            row,
            pl.BlockSpec((d, d), lambda i: (0, 0)),
            pl.BlockSpec((tm, d), lambda i: (i, mf_blk)),
            pl.BlockSpec((tm, d), lambda i: (i, mf_blk + 1)),
            row,
            row,
        ],
        out_specs=[row] * 4,
        compiler_params=_cp("parallel"),
    )(dsub, w_out, gates, gates, y_fox, y_swa)


def _branch_bwd(dy, w_b, o, gates, g_blk, name, n_heads):
    s_len, d = dy.shape
    wd = w_b.shape[0]
    tm = min(512, s_len)

    def body(dy_ref, w_ref, o_ref, g_ref, do_ref, dg_ref, *rest):
        da = lax.dot_general(dy_ref[...], w_ref[...], _NT, preferred_element_type=F32)
        g = g_ref[...]
        sg = _sigmoid(g)
        do = da * (g * sg)
        do_ref[...] = do.astype(BF16)
        o = o_ref[...]
        dg_ref[...] = (da * o * (sg * (1.0 + g * (1.0 - sg)))).astype(BF16)
        if n_heads:
            prod = do * o
            lane = lax.broadcasted_iota(jnp.int32, (1, 128), 1)
            delta = jnp.zeros((tm, 128), F32)
            for h in range(n_heads):
                dh = jnp.sum(prod[:, h * 128 : (h + 1) * 128], axis=1, keepdims=True)
                delta = delta + jnp.where(lane == h, dh, 0.0)
            rest[0][...] = delta

    out_shape = [jax.ShapeDtypeStruct((s_len, wd), BF16), jax.ShapeDtypeStruct((s_len, wd), BF16)]
    out_specs = [pl.BlockSpec((tm, wd), lambda i: (i, 0))] * 2
    if n_heads:
        out_shape.append(jax.ShapeDtypeStruct((s_len, 128), F32))
        out_specs.append(pl.BlockSpec((tm, 128), lambda i: (i, 0)))
    return _pcall(
        body,
        name=name,
        grid=(s_len // tm,),
        out_shape=out_shape,
        in_specs=[
            pl.BlockSpec((tm, d), lambda i: (i, 0)),
            pl.BlockSpec((wd, d), lambda i: (0, 0)),
            pl.BlockSpec((tm, wd), lambda i: (i, 0)),
            pl.BlockSpec((tm, wd), lambda i: (i, g_blk)),
        ],
        out_specs=out_specs,
        compiler_params=_cp("parallel"),
    )(dy, w_b, o, gates)


def _in_bwd(dproj, w_in_t, x, ada, dza, ride):
    s_len, d = x.shape
    k_tot = dproj.shape[1]
    tm, tk = min(512, s_len), 1024
    ni, nk = s_len // tm, k_tot // tk
    n = len(ride)

    def body(dp_ref, w_ref, x_ref, sc_ref, dza_ref, *rest):
        ins, (gx_ref, red_ref), outs = rest[:n], rest[n : n + 2], rest[n + 2 : 2 * n + 2]
        sems, acc_s = rest[2 * n + 2 : 2 * n + 5], rest[2 * n + 5]
        i, kk = pl.program_id(0), pl.program_id(1)

        @pl.when((i == 0) & (kk == 0))
        def _():
            _rider_start("exchange", ins, outs, *sems)

        @pl.when((i == ni - 1) & (kk == nk - 1))
        def _():
            _rider_wait("exchange", ins, outs, *sems)

        part = jnp.dot(dp_ref[...], w_ref[...], preferred_element_type=F32)

        @pl.when(kk == 0)
        def _():
            acc_s[...] = part

        @pl.when(kk > 0)
        def _():
            acc_s[...] += part

        @pl.when(kk == nk - 1)
        def _():
            dh = acc_s[...]
            xv = x_ref[...]
            mu = jnp.mean(xv, axis=-1, keepdims=True)
            xc = xv - mu
            var = jnp.mean(xc * xc, axis=-1, keepdims=True)
            rstd = lax.rsqrt(var + LN_EPS)
            xhat = xc * rstd
            dxhat = dh * (1.0 + sc_ref[...])
            dx = rstd * (dxhat - jnp.mean(dxhat, axis=-1, keepdims=True) - xhat * jnp.mean(dxhat * xhat, axis=-1, keepdims=True))
            gx_ref[...] = dza_ref[...] + dx
            part_r = jnp.concatenate(
                [jnp.sum(dh, axis=0, keepdims=True), jnp.sum(dh * xhat, axis=0, keepdims=True), jnp.zeros((6, d), F32)], axis=0
            )

            @pl.when(i == 0)
            def _():
                red_ref[...] = part_r

            @pl.when(i > 0)
            def _():
                red_ref[...] += part_r

    row = pl.BlockSpec((tm, d), lambda i, kk: (i, 0))
    hbm = pl.BlockSpec(memory_space=pltpu.HBM)
    return _pcall(
        body,
        name="in_bwd",
        grid=(ni, nk),
        out_shape=[jax.ShapeDtypeStruct((s_len, d), F32), jax.ShapeDtypeStruct((8, d), F32)]
        + [jax.ShapeDtypeStruct(r.shape, r.dtype) for r in ride],
        in_specs=[
            pl.BlockSpec((tm, tk), lambda i, kk: (i, kk)),
            pl.BlockSpec((tk, d), lambda i, kk: (kk, 0)),
            row,
            pl.BlockSpec((1, d), lambda i, kk: (0, 1)),
            row,
        ]
        + [hbm] * n,
        out_specs=[row, pl.BlockSpec((8, d), lambda i, kk: (0, 0))] + [hbm] * n,
        scratch_shapes=_rider_scratch(n) + [pltpu.VMEM((tm, d), F32)],
        compiler_params=_cp("arbitrary", "arbitrary"),
    )(dproj, w_in_t, x, ada, dza, *ride)


def _pad_lanes(v, n):
    return jnp.pad(v, ((0, 0), (0, n - v.shape[1])))


def kernel(x, c, w_ada, b_ada, w_in, b_f, attn_sinks, w_br_fox, w_br_swa, w_out, ln_g, ln_b, loss_target, m_w_ada, m_b_ada, m_w_in, m_b_f, m_attn_sinks, m_w_br_fox, m_w_br_swa, m_w_out, m_ln_g, m_ln_b, v_w_ada, v_b_ada, v_w_in, v_b_f, v_attn_sinks, v_w_br_fox, v_w_br_swa, v_w_out, v_ln_g, v_ln_b):
    x2, tgt = x[0], loss_target[0]
    s_len, d = x2.shape
    me = 4 * lax.axis_index("x") + 2 * lax.axis_index("y") + lax.axis_index("c")
    off_ms = OFF_MF + d
    in_pad = off_ms + d
    c_ada = w_ada.shape[2]
    c_in = w_in.shape[2]
    c_br = w_br_fox.shape[2]

    w_in_full = _all_gather(w_in[0].T.astype(BF16), "ag_w_in", pltpu.HBM).reshape(N_DEV * c_in, d)
    w_in_pad = jnp.concatenate(
        [w_in_full[:REAL_FLOG_END], jnp.zeros((FLOG_PAD - N_FLOG, d), BF16), w_in_full[REAL_FLOG_END:]], axis=0
    )

    c_all = _gather_rows(c, "ag_c")
    b_cols = lax.dynamic_slice(b_ada, (0, me * c_ada), (1, c_ada))
    ada_cols = _ada_fwd(c_all, w_ada[0], b_cols)
    ada_g = _all_gather(ada_cols, "ag_ada", pltpu.VMEM)
    ada = lax.dynamic_index_in_dim(ada_g, me, axis=1, keepdims=False).reshape(1, N_DEV * c_ada)

    h = _ln_mod(x2, ada)
    qkv_fox = _mm_cols(h, w_in_pad, OFF_FQ, 3 * FOX_W, BF16, "proj_fox")
    flog = _mm_cols(h, w_in_pad, OFF_FLOG, FLOG_PAD, F32, "proj_flog")
    qkv_swa = _mm_cols(h, w_in_pad, OFF_SQ, SWA_W + 2 * SWA_KVW, BF16, "proj_swa")
    gates, w_bf, w_bs, w_o = _mm_cols(
        h, w_in_pad, OFF_GF, in_pad - OFF_GF, F32, "proj_gates",
        ride=(w_br_fox[0].astype(BF16), w_br_swa[0].astype(BF16), w_out[0].astype(BF16)),
    )
    w_bf = w_bf.reshape(N_DEV, FOX_W, c_br).transpose(1, 0, 2).reshape(FOX_W, d)
    w_bs = w_bs.reshape(N_DEV, SWA_W, c_br).transpose(1, 0, 2).reshape(SWA_W, d)
    w_o = w_o.reshape(d, d)
    mf_blk = (OFF_MF - OFF_GF) // d

    flog_t = flog[:, :N_FLOG].T
    bf_col = b_f.reshape(FOX_H, 1)
    cum = _fox_cum(flog_t, bf_col)
    cum_row = cum.reshape(FOX_H, 1, s_len)
    o_fox, lse = _fox_fwd(qkv_fox, cum_row)
    sinks = attn_sinks.reshape(SWA_HQ)
    o_swa = _swa_fwd(qkv_swa, sinks)

    y_fox, a_fox = _branch_fwd(o_fox, gates, 0, w_bf, "branch_fox")
    y_swa, a_swa = _branch_fwd(o_swa, gates, 1, w_bs, "branch_swa")
    merged, dza, dsub, red = _out_stage(gates, mf_blk, y_fox, y_swa, w_o, x2, ada, ln_g, ln_b, tgt)
    loss = lax.psum(0.5 * red[4, 0] / d, ("x", "y", "c"))

    dmf, dms, dy_fox, dy_swa = _merge_bwd(dsub, w_o, gates, mf_blk, y_fox, y_swa)
    do_fox, dg_fox, delta = _branch_bwd(dy_fox, w_bf, o_fox, gates, 0, "branch_fox_bwd", FOX_H)
    do_swa, dg_swa = _branch_bwd(dy_swa, w_bs, o_swa, gates, 1, "branch_swa_bwd", 0)
    delta_row = delta[:, :FOX_H].T.reshape(FOX_H, 1, s_len)
    dq_f, dk_f, dv_f, dcol, drow = _fox_bwd(
        qkv_fox, cum.reshape(FOX_H, s_len, 1), lse.reshape(FOX_H, 1, s_len), delta_row, do_fox
    )
    dflog_t, dbf = _fox_gate_bwd(drow.reshape(FOX_H, s_len), dcol.reshape(FOX_H, s_len), flog_t, bf_col)
    dq_s, dk_s, dv_s, dsink = _swa_bwd(qkv_swa, sinks, do_swa)
    dflog = _pad_lanes(dflog_t.T, FLOG_PAD).astype(BF16)
    dproj = jnp.concatenate([dq_f.astype(BF16), dk_f, dv_f, dflog, dq_s, dk_s, dv_s, dg_fox, dg_swa, dmf, dms], axis=1)
    g_w_in = _mm_tn(dproj, h, "grad_w_in")
    g_w_in = jnp.concatenate([g_w_in[:REAL_FLOG_END], g_w_in[OFF_SQ:]], axis=0)
    g_w_bf = _mm_tn(a_fox, dy_fox, "grad_w_br_fox")
    g_w_bs = _mm_tn(a_swa, dy_swa, "grad_w_br_swa")
    g_w_o = _mm_tn(merged, dsub, "grad_w_out")

    grad_x, red2, r_in, r_bf, r_bs, r_o = _in_bwd(
        dproj, w_in_pad, x2, ada, dza,
        ride=(
            g_w_in.reshape(N_DEV, c_in, d),
            g_w_bf.reshape(FOX_W, N_DEV, c_br).transpose(1, 0, 2),
            g_w_bs.reshape(SWA_W, N_DEV, c_br).transpose(1, 0, 2),
            g_w_o.reshape(N_DEV, d // N_DEV, d),
        ),
    )
    out_w_in = _sum_adam_t(r_in, w_in[0].T, m_w_in[0].T, v_w_in[0].T, "adam_w_in")
    out_w_in = [o.T for o in out_w_in]
    out_w_bf = _sum_adam(r_bf, w_br_fox[0], m_w_br_fox[0], v_w_br_fox[0], "adam_w_br_fox")
    out_w_bs = _sum_adam(r_bs, w_br_swa[0], m_w_br_swa[0], v_w_br_swa[0], "adam_w_br_swa")
    out_w_o = _sum_adam(r_o, w_out[0], m_w_out[0], v_w_out[0], "adam_w_out")

    packed = jnp.concatenate([red2[0:1], red2[1:2], red[0:1], _pad_lanes(dbf[:, 0].reshape(1, FOX_H), 128), dsink, red[1:2], red[2:3]], axis=1)
    gathered = _gather_rows(packed, "ag_small")
    pack = lambda a, b, cc, dd, e: jnp.concatenate([a, _pad_lanes(b, 128), _pad_lanes(cc, 128), dd, e], axis=1)
    small = _small_adam(
        gathered,
        pack(b_ada, b_f, attn_sinks, ln_g, ln_b),
        pack(m_b_ada, m_b_f, m_attn_sinks, m_ln_g, m_ln_b),
        pack(v_b_ada, v_b_f, v_attn_sinks, v_ln_g, v_ln_b),
    )
    dada_cols = lax.dynamic_slice(gathered, (0, me * c_ada), (N_DEV, c_ada))
    out_w_ada = _wada_adam(c_all.T, dada_cols, w_ada[0], m_w_ada[0], v_w_ada[0])

    o1, o2, o3 = 3 * d, 3 * d + 128, 3 * d + 256

    def unpack(p):
        return p[:, :o1], p[:, o1 : o1 + FOX_H], p[:, o2 : o2 + SWA_HQ], p[:, o3 : o3 + d], p[:, o3 + d : o3 + 2 * d]

    kinds = []
    for k in range(4):
        b_ada_k, b_f_k, sinks_k, ln_g_k, ln_b_k = unpack(small[k])
        kinds.append(
            [out_w_ada[k][None], b_ada_k, out_w_in[k][None], b_f_k, sinks_k, out_w_bf[k][None], out_w_bs[k][None], out_w_o[k][None], ln_g_k, ln_b_k]
        )
    return (loss, grad_x[None], *kinds[0], *kinds[1], *kinds[2], *kinds[3])
```

```python
import numpy as np
import jax
import jax.numpy as jnp
from jax import lax
from jax.experimental import pallas as pl
from jax.experimental.pallas import tpu as pltpu

F32 = jnp.float32
BF16 = jnp.bfloat16
N_DEV = 8
MESH = pl.DeviceIdType.MESH

FOX_H, FOX_DH, FOX_W = 8, 128, 1024
SWA_HQ, SWA_HKV, SWA_DH, SWA_G = 16, 4, 64, 4
SWA_W, SWA_KVW, WINDOW = 1024, 256, 128
LN_EPS = 1e-5
NEG = -1e30
DEPTH = 1
ALPHA = (2.0 * DEPTH) ** 0.25
FOX_SCALE = FOX_DH ** -0.5
SWA_SCALE = SWA_DH ** -0.5
SLOPES = [2.0 ** (-8.0 * (h + 1.0) / SWA_HQ) for h in range(SWA_HQ)]

ADAM_LR, ADAM_B1, ADAM_B2, ADAM_EPS, ADAM_WD, ADAM_STEP = 0.001, 0.9, 0.999, 1e-08, 0.01, 10

N_FLOG = 8
FLOG_PAD = 512
OFF_FQ, OFF_FK, OFF_FV, OFF_FLOG = 0, 1024, 2048, 3072
OFF_SQ = OFF_FLOG + FLOG_PAD
OFF_SK = OFF_SQ + SWA_W
OFF_SV = OFF_SK + SWA_KVW
OFF_GF = OFF_SV + SWA_KVW
OFF_GS = OFF_GF + FOX_W
OFF_MF = OFF_GS + SWA_W
REAL_FLOG_END = OFF_FLOG + N_FLOG

ATT_BLK = 512
VMEM_LIMIT = 52 * 1024 * 1024


def _pcall(body, **kw):
    return pl.pallas_call(body, **kw)


def _cp(*sem):
    return pltpu.CompilerParams(dimension_semantics=sem, vmem_limit_bytes=VMEM_LIMIT)


def _sigmoid(x):
    return 1.0 / (1.0 + jnp.exp(-x))


def _all_gather(x, name, space):
    m_per, n = x.shape

    def body(x_ref, out_ref, send_sems, recv_sems, local_sem):
        mx, my, mc = lax.axis_index("x"), lax.axis_index("y"), lax.axis_index("c")
        me, sibling = (mx, my, mc), (mx, my, 1 - mc)
        chips = [(1 - mx, my), (mx, 1 - my), (1 - mx, 1 - my)]

        def rows(px, py, pc):
            return out_ref.at[4 * px + 2 * py + pc]

        def copy(k, block, to, src=None):
            return pltpu.make_async_remote_copy(
                src_ref=rows(*block) if src is None else src,
                dst_ref=rows(*block),
                send_sem=send_sems.at[k],
                recv_sem=recv_sems.at[k],
                device_id=to,
                device_id_type=MESH,
            )

        mine = pltpu.make_async_copy(x_ref, rows(*me), local_sem)
        mine.start()
        first = [copy(0, me, sibling, src=x_ref)]
        first += [copy(1 + j, me, (*chip, mc), src=x_ref) for j, chip in enumerate(chips)]
        for cp in first:
            cp.start()
        passed = [copy(4 + j, (*chip, mc), sibling) for j, chip in enumerate(chips)]
        for j, chip in enumerate(chips):
            copy(1 + j, (*chip, mc), me).wait_recv()
            passed[j].start()
        copy(0, sibling, me).wait_recv()
        for j, chip in enumerate(chips):
            copy(4 + j, (*chip, 1 - mc), me).wait_recv()
        for cp in first + passed:
            cp.wait_send()
        mine.wait()

    return _pcall(
        body,
        name=name,
        out_shape=jax.ShapeDtypeStruct((N_DEV, m_per, n), x.dtype),
        in_specs=[pl.BlockSpec(memory_space=space)],
        out_specs=pl.BlockSpec(memory_space=space),
        scratch_shapes=[pltpu.SemaphoreType.DMA((7,)), pltpu.SemaphoreType.DMA((7,)), pltpu.SemaphoreType.DMA],
    )(x)


def _peer(d, mx, my, mc):
    return (1 - mx if (d >> 2) & 1 else mx, 1 - my if (d >> 1) & 1 else my, 1 - mc if d & 1 else mc)


def _rider_copies(kind, ins, outs, send_sems, recv_sems, local_sems):
    mx, my, mc = lax.axis_index("x"), lax.axis_index("y"), lax.axis_index("c")
    me = 4 * mx + 2 * my + mc
    remote, local = [], []
    for a in range(len(ins)):
        if kind == "gather":
            m_per = ins[a].shape[0]
            mine = outs[a].at[pl.ds(me * m_per, m_per), :]
            local.append(pltpu.make_async_copy(ins[a], mine, local_sems.at[a]))
        else:
            local.append(pltpu.make_async_copy(ins[a].at[me], outs[a].at[0], local_sems.at[a]))
        for d in range(1, N_DEV):
            px, py, pc = _peer(d, mx, my, mc)
            if kind == "gather":
                src, dst = ins[a], mine
            else:
                src, dst = ins[a].at[4 * px + 2 * py + pc], outs[a].at[d]
            remote.append(
                pltpu.make_async_remote_copy(
                    src_ref=src,
                    dst_ref=dst,
                    send_sem=send_sems.at[a * 7 + d - 1],
                    recv_sem=recv_sems.at[a * 7 + d - 1],
                    device_id=(px, py, pc),
                    device_id_type=MESH,
                )
            )
    return remote, local


def _rider_start(*args):
    remote, local = _rider_copies(*args)
    for cp in local + remote:
        cp.start()


def _rider_wait(*args):
    remote, local = _rider_copies(*args)
    for cp in remote:
        cp.wait_recv()
    for cp in remote:
        cp.wait_send()
    for cp in local:
        cp.wait()


def _rider_scratch(n):
    return [pltpu.SemaphoreType.DMA((7 * n,)), pltpu.SemaphoreType.DMA((7 * n,)), pltpu.SemaphoreType.DMA((n,))]


def _gather_rows(v, name):
    n = v.shape[1]
    return _all_gather(jnp.broadcast_to(v, (8, n)), name, pltpu.VMEM)[:, 0, :]


def _adamw(w, g, m, v):
    m = ADAM_B1 * m + (1.0 - ADAM_B1) * g
    v = ADAM_B2 * v + (1.0 - ADAM_B2) * (g * g)
    m_hat = m / (1.0 - ADAM_B1**ADAM_STEP)
    v_hat = v / (1.0 - ADAM_B2**ADAM_STEP)
    delta = -ADAM_LR * (m_hat / (jnp.sqrt(v_hat) + ADAM_EPS) + ADAM_WD * w)
    return delta, m, v


def _sum_adam(recv, w, m, v, name):
    _, r_tot, c = recv.shape
    c_pad = -(-c // 128) * 128
    tr = r_tot
    while 8 * tr * c_pad * 4 > 6 * 1024 * 1024 and tr % 32 == 0:
        tr //= 2

    def body(r_ref, w_ref, m_ref, v_ref, g_ref, d_ref, nm_ref, nv_ref):
        g = r_ref[0].astype(F32)
        for k in range(1, N_DEV):
            g = g + r_ref[k].astype(F32)
        d, nm, nv = _adamw(w_ref[...], g, m_ref[...], v_ref[...])
        g_ref[...] = g
        d_ref[...] = d
        nm_ref[...] = nm
        nv_ref[...] = nv

    blk = pl.BlockSpec((tr, c), lambda i: (i, 0))
    return _pcall(
        body,
        name=name,
        grid=(r_tot // tr,),
        out_shape=[jax.ShapeDtypeStruct((r_tot, c), F32)] * 4,
        in_specs=[pl.BlockSpec((N_DEV, tr, c), lambda i: (0, i, 0)), blk, blk, blk],
        out_specs=[blk] * 4,
        compiler_params=_cp("parallel"),
    )(recv, w, m, v)


def _sum_adam_t(recv, w, m, v, name):
    _, c, r_tot = recv.shape
    tr = min(256, r_tot)

    def body(r_ref, w_ref, m_ref, v_ref, g_ref, d_ref, nm_ref, nv_ref):
        g = r_ref[0].astype(F32)
        for k in range(1, N_DEV):
            g = g + r_ref[k].astype(F32)
        d, nm, nv = _adamw(w_ref[...], g, m_ref[...], v_ref[...])
        g_ref[...] = g
        d_ref[...] = d
        nm_ref[...] = nm
        nv_ref[...] = nv

    blk = pl.BlockSpec((c, tr), lambda i: (0, i))
    return _pcall(
        body,
        name=name,
        grid=(r_tot // tr,),
        out_shape=[jax.ShapeDtypeStruct((c, r_tot), F32)] * 4,
        in_specs=[pl.BlockSpec((N_DEV, c, tr), lambda i: (0, 0, i)), blk, blk, blk],
        out_specs=[blk] * 4,
        compiler_params=_cp("parallel"),
    )(recv, w, m, v)


def _wada_adam(c_t, dada_cols, w, m, v):
    d_model, c = w.shape
    tr = min(256, d_model)

    def body(ct_ref, da_ref, w_ref, m_ref, v_ref, g_ref, d_ref, nm_ref, nv_ref):
        g = jnp.dot(ct_ref[...].astype(BF16), da_ref[...].astype(BF16), preferred_element_type=F32)
        d, nm, nv = _adamw(w_ref[...], g, m_ref[...], v_ref[...])
        g_ref[...] = g
        d_ref[...] = d
        nm_ref[...] = nm
        nv_ref[...] = nv

    blk = pl.BlockSpec((tr, c), lambda i: (i, 0))
    return _pcall(
        body,
        name="wada_adam",
        grid=(d_model // tr,),
        out_shape=[jax.ShapeDtypeStruct((d_model, c), F32)] * 4,
        in_specs=[pl.BlockSpec((tr, N_DEV), lambda i: (i, 0)), pl.BlockSpec((N_DEV, c), lambda i: (0, 0)), blk, blk, blk],
        out_specs=[blk] * 4,
        compiler_params=_cp("parallel"),
    )(c_t, dada_cols, w, m, v)


def _small_adam(gathered, w, m, v):
    p = w.shape[1]

    def body(a_ref, w_ref, m_ref, v_ref, g_ref, d_ref, nm_ref, nv_ref):
        g = a_ref[0:1, :]
        for k in range(1, N_DEV):
            g = g + a_ref[k : k + 1, :]
        d, nm, nv = _adamw(w_ref[...], g, m_ref[...], v_ref[...])
        g_ref[...] = g
        d_ref[...] = d
        nm_ref[...] = nm
        nv_ref[...] = nv

    return _pcall(
        body,
        name="small_adam",
        out_shape=[jax.ShapeDtypeStruct((1, p), F32)] * 4,
    )(gathered, w, m, v)


def _ada_fwd(c_all, w_ada, b_cols):
    c = w_ada.shape[1]

    def body(c_ref, w_ref, b_ref, o_ref):
        o_ref[...] = jnp.dot(c_ref[...].astype(BF16), w_ref[...].astype(BF16), preferred_element_type=F32) + b_ref[...]

    return _pcall(
        body,
        name="ada_fwd",
        out_shape=jax.ShapeDtypeStruct((N_DEV, c), F32),
        compiler_params=_cp(),
    )(c_all, w_ada, b_cols)


def _ln_mod(x, ada):
    s_len, d = x.shape
    tm = min(512, s_len)

    def body(x_ref, sh_ref, sc_ref, h_ref):
        xv = x_ref[...]
        mu = jnp.mean(xv, axis=-1, keepdims=True)
        xc = xv - mu
        var = jnp.mean(xc * xc, axis=-1, keepdims=True)
        xhat = xc * lax.rsqrt(var + LN_EPS)
        h_ref[...] = (xhat * (1.0 + sc_ref[...]) + sh_ref[...]).astype(BF16)

    return _pcall(
        body,
        name="ln_mod",
        grid=(s_len // tm,),
        out_shape=jax.ShapeDtypeStruct((s_len, d), BF16),
        in_specs=[
            pl.BlockSpec((tm, d), lambda i: (i, 0)),
            pl.BlockSpec((1, d), lambda i: (0, 0)),
            pl.BlockSpec((1, d), lambda i: (0, 1)),
        ],
        out_specs=pl.BlockSpec((tm, d), lambda i: (i, 0)),
        compiler_params=_cp("parallel"),
    )(x, ada, ada)


def _mm_cols(a, b, col_off, n_cols, out_dtype, name, ride=()):
    m, k = a.shape
    tm, tn = min(1024, m), 512
    off = col_off // tn
    ni, nj = m // tm, n_cols // tn
    n = len(ride)

    def body(a_ref, b_ref, *rest):
        ins, o_ref, outs, sems = rest[:n], rest[n], rest[n + 1 : 2 * n + 1], rest[2 * n + 1 :]
        i, j = pl.program_id(0), pl.program_id(1)
        if n:

            @pl.when((i == 0) & (j == 0))
            def _():
                _rider_start("gather", ins, outs, *sems)

        o_ref[...] = lax.dot_general(a_ref[...], b_ref[...], _NT, preferred_element_type=F32).astype(out_dtype)
        if n:

            @pl.when((i == ni - 1) & (j == nj - 1))
            def _():
                _rider_wait("gather", ins, outs, *sems)

    hbm = pl.BlockSpec(memory_space=pltpu.HBM)
    out = _pcall(
        body,
        name=name,
        grid=(ni, nj),
        out_shape=[jax.ShapeDtypeStruct((m, n_cols), out_dtype)]
        + [jax.ShapeDtypeStruct((N_DEV * r.shape[0], r.shape[1]), r.dtype) for r in ride],
        in_specs=[pl.BlockSpec((tm, k), lambda i, j: (i, 0)), pl.BlockSpec((tn, k), lambda i, j: (off + j, 0))] + [hbm] * n,
        out_specs=[pl.BlockSpec((tm, tn), lambda i, j: (i, j))] + [hbm] * n,
        scratch_shapes=_rider_scratch(n) if n else [],
        compiler_params=_cp("arbitrary", "arbitrary") if n else _cp("parallel", "parallel"),
    )(a, b, *ride)
    return out if n else out[0]


def _mm_tn(a, b, name):
    s_len, m = a.shape
    n = b.shape[1]
    tm, tn, ts = min(1024, m), min(1024, n), min(1024, s_len)
    ns = s_len // ts

    def body(a_ref, b_ref, o_ref, acc_s):
        kk = pl.program_id(2)
        part = lax.dot_general(a_ref[...], b_ref[...], _TN, preferred_element_type=F32)

        @pl.when(kk == 0)
        def _():
            acc_s[...] = part

        @pl.when(kk > 0)
        def _():
            acc_s[...] += part

        @pl.when(kk == ns - 1)
        def _():
            o_ref[...] = acc_s[...].astype(BF16)

    return _pcall(
        body,
        name=name,
        grid=(m // tm, n // tn, ns),
        out_shape=jax.ShapeDtypeStruct((m, n), BF16),
        in_specs=[pl.BlockSpec((ts, tm), lambda i, j, kk: (kk, i)), pl.BlockSpec((ts, tn), lambda i, j, kk: (kk, j))],
        out_specs=pl.BlockSpec((tm, tn), lambda i, j, kk: (i, j)),
        scratch_shapes=[pltpu.VMEM((tm, tn), F32)],
        compiler_params=_cp("parallel", "parallel", "arbitrary"),
    )(a, b)


def _split3(a):
    hi = a.astype(BF16)
    r1 = a - hi.astype(F32)
    mid = r1.astype(BF16)
    lo = (r1 - mid.astype(F32)).astype(BF16)
    return hi, mid, lo


def _dot_ones(a, tri):
    return sum(jnp.dot(t, tri, preferred_element_type=F32) for t in _split3(a))


def _log_sigmoid(x):
    return jnp.minimum(x, 0.0) - jnp.log1p(jnp.exp(-jnp.abs(x)))


def _fox_cum(flog_t, bf_col):
    s_len = flog_t.shape[1]

    def body(fl_ref, bf_ref, cum_ref):
        r = lax.broadcasted_iota(jnp.int32, (128, 128), 0)
        c = lax.broadcasted_iota(jnp.int32, (128, 128), 1)
        upper = (r <= c).astype(BF16)

        def step(t, carry):
            sl = pl.ds(pl.multiple_of(t * 128, 128), 128)
            lf = _log_sigmoid(fl_ref[:, sl] + bf_ref[...])
            cs = _dot_ones(lf, upper) + carry
            cum_ref[:, sl] = cs
            return cs[:, 127:128]

        lax.fori_loop(0, s_len // 128, step, jnp.zeros((FOX_H, 1), F32))

    return _pcall(body, name="fox_cum", out_shape=jax.ShapeDtypeStruct((FOX_H, s_len), F32))(flog_t, bf_col)


def _fox_gate_bwd(drow, dcol, flog_t, bf_col):
    s_len = flog_t.shape[1]
    n = s_len // 128

    def body(dr_ref, dc_ref, fl_ref, bf_ref, dfl_ref, dbf_ref):
        r = lax.broadcasted_iota(jnp.int32, (128, 128), 0)
        c = lax.broadcasted_iota(jnp.int32, (128, 128), 1)
        lower = (r >= c).astype(BF16)

        def step(t, carry):
            run, tot = carry
            sl = pl.ds(pl.multiple_of((n - 1 - t) * 128, 128), 128)
            rc = _dot_ones(dr_ref[:, sl] - dc_ref[:, sl], lower) + run
            dfl = rc * _sigmoid(-(fl_ref[:, sl] + bf_ref[...]))
            dfl_ref[:, sl] = dfl
            return rc[:, 0:1], tot + jnp.sum(dfl, axis=1, keepdims=True)

        zero = jnp.zeros((FOX_H, 1), F32)
        _, tot = lax.fori_loop(0, n, step, (zero, zero))
        dbf_ref[...] = jnp.broadcast_to(tot, (FOX_H, 128))

    return _pcall(
        body,
        name="fox_gate_bwd",
        out_shape=[jax.ShapeDtypeStruct((FOX_H, s_len), F32), jax.ShapeDtypeStruct((FOX_H, 128), F32)],
    )(drow, dcol, flog_t, bf_col)


def _diag_mask(blk, transposed=False):
    r = lax.broadcasted_iota(jnp.int32, (blk, blk), 0)
    c = lax.broadcasted_iota(jnp.int32, (blk, blk), 1)
    return c >= r if transposed else r >= c


_NT = (((1,), (1,)), ((), ()))
_TN = (((0,), (0,)), ((), ()))


def _fox_fwd(qkv, cum_row):
    s_len = qkv.shape[0]
    blk = min(ATT_BLK, s_len)
    nb = s_len // blk
    log2e = 1.4426950408889634

    def body(q_ref, k_ref, v_ref, c_ref, o_ref, lse_ref, mx_s, acc_s):
        i = pl.program_id(1)

        def logits(j, n, masked):
            cols = pl.ds(pl.multiple_of(j * blk, blk), n * blk)
            u = lax.dot_general(q_ref[...], k_ref[cols, :], _NT, preferred_element_type=F32) - c_ref[:, cols] * (1.0 / FOX_SCALE)
            if masked:
                u = jnp.where(_diag_mask(blk), u, NEG)
            return u, cols

        def walk(tile):
            lax.fori_loop(0, i // 2, lambda t, c: (tile(2 * t, 2, False), c)[1], 0)

            @pl.when(i % 2 == 1)
            def _():
                tile(i - 1, 1, False)

            tile(i, 1, True)

        def lane_max(j, n, masked):
            u, _ = logits(j, n, masked)
            part = u[:, 0:128]
            for t in range(1, n * blk // 128):
                part = jnp.maximum(part, u[:, t * 128 : (t + 1) * 128])
            mx_s[...] = jnp.maximum(mx_s[...], part)

        mx_s[...] = jnp.full(mx_s.shape, NEG, F32)
        walk(lane_max)
        m = jnp.max(mx_s[...], axis=1, keepdims=True)

        def weigh(j, n, masked):
            u, cols = logits(j, n, masked)
            p = jnp.exp2((u - m) * (FOX_SCALE * log2e))
            ones_col = (lax.broadcasted_iota(jnp.int32, (n * blk, 128), 1) == 0).astype(BF16)
            v1 = jnp.concatenate([v_ref[cols, :], ones_col], axis=1)
            acc_s[...] += jnp.dot(p.astype(BF16), v1, preferred_element_type=F32)

        acc_s[...] = jnp.zeros(acc_s.shape, F32)
        walk(weigh)
        l = acc_s[:, FOX_DH : FOX_DH + 1]
        o_ref[...] = acc_s[:, :FOX_DH] / l
        lse_ref[...] = m * FOX_SCALE + jnp.log(l)

    return _pcall(
        body,
        name="fox_fwd",
        grid=(FOX_H, nb),
        out_shape=[jax.ShapeDtypeStruct((s_len, FOX_W), F32), jax.ShapeDtypeStruct((FOX_H, s_len, 1), F32)],
        in_specs=[
            pl.BlockSpec((blk, FOX_DH), lambda h, i: (i, h)),
            pl.BlockSpec((s_len, FOX_DH), lambda h, i: (0, FOX_H + h)),
            pl.BlockSpec((s_len, FOX_DH), lambda h, i: (0, 2 * FOX_H + h)),
            pl.BlockSpec((None, 1, s_len), lambda h, i: (h, 0, 0)),
        ],
        out_specs=[
            pl.BlockSpec((blk, FOX_DH), lambda h, i: (i, h)),
            pl.BlockSpec((None, blk, 1), lambda h, i: (h, i, 0)),
        ],
        scratch_shapes=[pltpu.VMEM((blk, 128), F32), pltpu.VMEM((blk, 2 * FOX_DH), F32)],
        compiler_params=_cp("parallel", "arbitrary"),
    )(qkv, qkv, qkv, cum_row)


def _fox_bwd(qkv, cum_col, lse_row, delta_row, do):
    s_len = qkv.shape[0]
    blk = min(ATT_BLK, s_len)
    nb = s_len // blk

    def body(q_ref, k_ref, v_ref, c_ref, lse_ref, dl_ref, do_ref, dq_ref, dk_ref, dv_ref, dc_ref, dr_ref, dk_s, dv_s, dc_s, cb_s):
        j = pl.program_id(1)

        @pl.when(j == 0)
        def _():
            dq_ref[...] = jnp.zeros(dq_ref.shape, F32)
            dr_ref[...] = jnp.zeros(dr_ref.shape, F32)

        dk_s[...] = jnp.zeros(dk_s.shape, F32)
        dv_s[...] = jnp.zeros(dv_s.shape, F32)
        dc_s[...] = jnp.zeros(dc_s.shape, F32)
        cb_s[...] = jnp.broadcast_to(c_ref[...], cb_s.shape)

        def tile(i, n, diag):
            rows = pl.ds(pl.multiple_of(i * blk, blk), n * blk)
            q, dob = q_ref[rows, :], do_ref[rows, :]
            k, v = k_ref[...], v_ref[...]
            s_t = lax.dot_general(k, q, _NT, preferred_element_type=F32) * FOX_SCALE - cb_s[:, : n * blk]
            p_t = jnp.exp(s_t - lse_ref[:, rows])
            if diag:
                p_t = jnp.where(_diag_mask(blk, transposed=True), p_t, 0.0)
            dp_t = lax.dot_general(v, dob, _NT, preferred_element_type=F32)
            ds_t = p_t * (dp_t - dl_ref[:, rows])
            dsb = ds_t.astype(BF16)
            dv_s[...] += jnp.dot(p_t.astype(BF16), dob, preferred_element_type=F32)
            dk_s[...] += jnp.dot(dsb, q, preferred_element_type=F32)
            dq_c = lax.dot_general(dsb, k, _TN, preferred_element_type=F32)
            part = ds_t[:, 0:128]
            for t in range(1, n * blk // 128):
                part = part + ds_t[:, t * 128 : (t + 1) * 128]
            dc_s[...] += part
            dr_ref[:, rows] += jnp.sum(ds_t, axis=0, keepdims=True)
            if diag:
                dq_ref[rows, :] = (dq_ref[rows, :] + dq_c) * FOX_SCALE
            else:
                dq_ref[rows, :] += dq_c

        tile(j, 1, True)
        odd = (nb - 1 - j) % 2

        @pl.when(odd == 1)
        def _():
            tile(j + 1, 1, False)

        lax.fori_loop(0, (nb - 1 - j) // 2, lambda t, c: (tile(j + 1 + odd + 2 * t, 2, False), c)[1], 0)
        dk_ref[...] = (dk_s[...] * FOX_SCALE).astype(BF16)
        dv_ref[...] = dv_s[...].astype(BF16)
        dc_ref[...] = jnp.sum(dc_s[...], axis=1, keepdims=True)

    head = lambda h, j: (0, h)
    row = pl.BlockSpec((None, 1, s_len), lambda h, j: (h, 0, 0))
    return _pcall(
        body,
        name="fox_bwd",
        grid=(FOX_H, nb),
        out_shape=[
            jax.ShapeDtypeStruct((s_len, FOX_W), F32),
            jax.ShapeDtypeStruct((s_len, FOX_W), BF16),
            jax.ShapeDtypeStruct((s_len, FOX_W), BF16),
            jax.ShapeDtypeStruct((FOX_H, s_len, 1), F32),
            jax.ShapeDtypeStruct((FOX_H, 1, s_len), F32),
        ],
        in_specs=[
            pl.BlockSpec((s_len, FOX_DH), head),
            pl.BlockSpec((blk, FOX_DH), lambda h, j: (j, FOX_H + h)),
            pl.BlockSpec((blk, FOX_DH), lambda h, j: (j, 2 * FOX_H + h)),
            pl.BlockSpec((None, blk, 1), lambda h, j: (h, j, 0)),
            row,
            row,
            pl.BlockSpec((s_len, FOX_DH), head),
        ],
        out_specs=[
            pl.BlockSpec((s_len, FOX_DH), head),
            pl.BlockSpec((blk, FOX_DH), lambda h, j: (j, h)),
            pl.BlockSpec((blk, FOX_DH), lambda h, j: (j, h)),
            pl.BlockSpec((None, blk, 1), lambda h, j: (h, j, 0)),
            row,
        ],
        scratch_shapes=[
            pltpu.VMEM((blk, FOX_DH), F32),
            pltpu.VMEM((blk, FOX_DH), F32),
            pltpu.VMEM((blk, 128), F32),
            pltpu.VMEM((blk, 2 * blk), F32),
        ],
        compiler_params=_cp("parallel", "arbitrary"),
    )(qkv, qkv, qkv, cum_col, lse_row, delta_row, do)


def _swa_group(i, q_ref, kk, sinks_ref, g):
    rows = SWA_G * WINDOW
    r = lax.broadcasted_iota(jnp.int32, (rows, 2 * WINDOW), 0)
    c = lax.broadcasted_iota(jnp.int32, (rows, 2 * WINDOW), 1)
    dist = (r & (WINDOW - 1)) - c + WINDOW
    valid = (dist >= 0) & (dist < WINDOW) & ((c >= WINDOW) | (i > 0))
    head = lax.broadcasted_iota(jnp.int32, (rows, 1), 0) // WINDOW
    slope = jnp.zeros((rows, 1), F32)
    sink = jnp.zeros((rows, 1), F32)
    for t in range(SWA_G):
        h = g * SWA_G + t
        slope = jnp.where(head == t, SLOPES[h], slope)
        sink = jnp.where(head == t, sinks_ref[h], sink)
    q = jnp.concatenate([q_ref[:, (g * SWA_G + t) * SWA_DH : (g * SWA_G + t + 1) * SWA_DH] for t in range(SWA_G)], axis=0)
    k = kk[:, g * SWA_DH : (g + 1) * SWA_DH]
    s = lax.dot_general(q, k, _NT, preferred_element_type=F32) * SWA_SCALE - slope * dist.astype(F32)
    s = jnp.where(valid, s, NEG)
    m = jnp.maximum(jnp.max(s, axis=1, keepdims=True), sink)
    e = jnp.exp(s - m)
    e_sink = jnp.exp(sink - m)
    inv = 1.0 / (jnp.sum(e, axis=1, keepdims=True) + e_sink)
    return q, k, e * inv, e_sink * inv


def _swa_specs(col_q, col_k, col_v, rev, nb):
    def blk(t):
        return nb - 1 - t if rev else t

    return [
        pl.BlockSpec((WINDOW, SWA_W), lambda t: (blk(t), col_q)),
        pl.BlockSpec((WINDOW, SWA_KVW), lambda t: (jnp.maximum(blk(t) - 1, 0), col_k)),
        pl.BlockSpec((WINDOW, SWA_KVW), lambda t: (blk(t), col_k)),
        pl.BlockSpec((WINDOW, SWA_KVW), lambda t: (jnp.maximum(blk(t) - 1, 0), col_v)),
        pl.BlockSpec((WINDOW, SWA_KVW), lambda t: (blk(t), col_v)),
    ]


def _swa_fwd(qkv, sinks):
    s_len = qkv.shape[0]
    nb = s_len // WINDOW

    def body(q_ref, kp_ref, kc_ref, vp_ref, vc_ref, sinks_ref, o_ref):
        i = pl.program_id(0)
        kk = jnp.concatenate([kp_ref[...], kc_ref[...]], axis=0)
        vv = jnp.concatenate([vp_ref[...], vc_ref[...]], axis=0)
        for g in range(SWA_HKV):
            _, _, p, _ = _swa_group(i, q_ref, kk, sinks_ref, g)
            o = jnp.dot(p.astype(BF16), vv[:, g * SWA_DH : (g + 1) * SWA_DH], preferred_element_type=F32)
            for t in range(SWA_G):
                h = g * SWA_G + t
                o_ref[:, h * SWA_DH : (h + 1) * SWA_DH] = o[t * WINDOW : (t + 1) * WINDOW, :]

    return _pcall(
        body,
        name="swa_fwd",
        grid=(nb,),
        out_shape=jax.ShapeDtypeStruct((s_len, SWA_W), F32),
        in_specs=_swa_specs(0, 4, 5, False, nb) + [pl.BlockSpec(memory_space=pltpu.SMEM)],
        out_specs=pl.BlockSpec((WINDOW, SWA_W), lambda t: (t, 0)),
        compiler_params=_cp("parallel"),
    )(qkv, qkv, qkv, qkv, qkv, sinks)


def _swa_bwd(qkv, sinks, do):
    s_len = qkv.shape[0]
    nb = s_len // WINDOW

    def body(q_ref, kp_ref, kc_ref, vp_ref, vc_ref, sinks_ref, do_ref, dq_ref, dk_ref, dv_ref, dsink_ref, ck_s, cv_s, dkk_s, dvv_s):
        t = pl.program_id(0)
        i = nb - 1 - t

        @pl.when(t == 0)
        def _():
            ck_s[...] = jnp.zeros(ck_s.shape, F32)
            cv_s[...] = jnp.zeros(cv_s.shape, F32)
            dsink_ref[...] = jnp.zeros(dsink_ref.shape, F32)

        kk = jnp.concatenate([kp_ref[...], kc_ref[...]], axis=0)
        vv = jnp.concatenate([vp_ref[...], vc_ref[...]], axis=0)
        lane = lax.broadcasted_iota(jnp.int32, (1, 128), 1)
        dsink = jnp.zeros((1, 128), F32)
        for g in range(SWA_HKV):
            cols = slice(g * SWA_DH, (g + 1) * SWA_DH)
            q, k, p, p_sink = _swa_group(i, q_ref, kk, sinks_ref, g)
            dob = jnp.concatenate([do_ref[:, (g * SWA_G + t) * SWA_DH : (g * SWA_G + t + 1) * SWA_DH] for t in range(SWA_G)], axis=0)
            dp = lax.dot_general(dob, vv[:, cols], _NT, preferred_element_type=F32)
            delta = jnp.sum(p * dp, axis=1, keepdims=True)
            dsb = (p * (dp - delta)).astype(BF16)
            dq = (jnp.dot(dsb, k, preferred_element_type=F32) * SWA_SCALE).astype(BF16)
            ps_d = p_sink * delta
            for t in range(SWA_G):
                h = g * SWA_G + t
                dq_ref[:, h * SWA_DH : (h + 1) * SWA_DH] = dq[t * WINDOW : (t + 1) * WINDOW, :]
                dsink = dsink + jnp.where(lane == h, -jnp.sum(ps_d[t * WINDOW : (t + 1) * WINDOW, :], axis=0, keepdims=True), 0.0)
            dkk_s[:, cols] = lax.dot_general(dsb, q, _TN, preferred_element_type=F32) * SWA_SCALE
            dvv_s[:, cols] = lax.dot_general(p.astype(BF16), dob, _TN, preferred_element_type=F32)
        dk_ref[...] = (dkk_s[WINDOW:, :] + ck_s[...]).astype(BF16)
        dv_ref[...] = (dvv_s[WINDOW:, :] + cv_s[...]).astype(BF16)
        ck_s[...] = dkk_s[:WINDOW, :]
        cv_s[...] = dvv_s[:WINDOW, :]
        dsink_ref[...] += dsink

    row = lambda t: (nb - 1 - t, 0)
    return _pcall(
        body,
        name="swa_bwd",
        grid=(nb,),
        out_shape=[
            jax.ShapeDtypeStruct((s_len, SWA_W), BF16),
            jax.ShapeDtypeStruct((s_len, SWA_KVW), BF16),
            jax.ShapeDtypeStruct((s_len, SWA_KVW), BF16),
            jax.ShapeDtypeStruct((1, 128), F32),
        ],
        in_specs=_swa_specs(0, 4, 5, True, nb)
        + [pl.BlockSpec(memory_space=pltpu.SMEM), pl.BlockSpec((WINDOW, SWA_W), row)],
        out_specs=[
            pl.BlockSpec((WINDOW, SWA_W), row),
            pl.BlockSpec((WINDOW, SWA_KVW), row),
            pl.BlockSpec((WINDOW, SWA_KVW), row),
            pl.BlockSpec((1, 128), lambda t: (0, 0)),
        ],
        scratch_shapes=[
            pltpu.VMEM((WINDOW, SWA_KVW), F32),
            pltpu.VMEM((WINDOW, SWA_KVW), F32),
            pltpu.VMEM((2 * WINDOW, SWA_KVW), F32),
            pltpu.VMEM((2 * WINDOW, SWA_KVW), F32),
        ],
        compiler_params=_cp("arbitrary"),
    )(qkv, qkv, qkv, qkv, qkv, sinks, do)


def _branch_fwd(o, gates, g_blk, w_b, name):
    s_len, wd = o.shape
    d = w_b.shape[1]
    tm = min(512, s_len)

    def body(o_ref, g_ref, w_ref, y_ref, a_ref):
        g = g_ref[...]
        a = (o_ref[...] * (g * _sigmoid(g))).astype(BF16)
        a_ref[...] = a
        y_ref[...] = jnp.dot(a, w_ref[...], preferred_element_type=F32)

    return _pcall(
        body,
        name=name,
        grid=(s_len // tm,),
        out_shape=[jax.ShapeDtypeStruct((s_len, d), F32), jax.ShapeDtypeStruct((s_len, wd), BF16)],
        in_specs=[
            pl.BlockSpec((tm, wd), lambda i: (i, 0)),
            pl.BlockSpec((tm, wd), lambda i: (i, g_blk)),
            pl.BlockSpec((wd, d), lambda i: (0, 0)),
        ],
        out_specs=[pl.BlockSpec((tm, d), lambda i: (i, 0)), pl.BlockSpec((tm, wd), lambda i: (i, 0))],
        compiler_params=_cp("parallel"),
    )(o, gates, w_b)


def _out_stage(gates, mf_blk, y_fox, y_swa, w_out, x, ada, ln_g, ln_b, target):
    s_len, d = x.shape
    tm = min(256, s_len)
    n_steps = s_len // tm

    def body(mf_ref, ms_ref, yf_ref, ys_ref, w_ref, x_ref, gate_ref, lg_ref, lb_ref, t_ref, mg_ref, dza_ref, dsub_ref, red_ref):
        i = pl.program_id(0)
        merged = _sigmoid(mf_ref[...]) * yf_ref[...] + _sigmoid(ms_ref[...]) * ys_ref[...]
        mb = merged.astype(BF16)
        mg_ref[...] = mb
        sub = jnp.dot(mb, w_ref[...], preferred_element_type=F32)
        gate = gate_ref[...]
        z = ALPHA * x_ref[...] + gate * sub
        mu = jnp.mean(z, axis=-1, keepdims=True)
        zc = z - mu
        var = jnp.mean(zc * zc, axis=-1, keepdims=True)
        rstd = lax.rsqrt(var + LN_EPS)
        zhat = zc * rstd
        err = zhat * lg_ref[...] + lb_ref[...] - t_ref[...]
        dout = err * (1.0 / d)
        dzhat = dout * lg_ref[...]
        dz = rstd * (dzhat - jnp.mean(dzhat, axis=-1, keepdims=True) - zhat * jnp.mean(dzhat * zhat, axis=-1, keepdims=True))
        dza_ref[...] = ALPHA * dz
        dsub_ref[...] = (gate * dz).astype(BF16)
        part = jnp.concatenate(
            [
                jnp.sum(dz * sub, axis=0, keepdims=True),
                jnp.sum(dout * zhat, axis=0, keepdims=True),
                jnp.sum(dout, axis=0, keepdims=True),
                jnp.sum(err * err, axis=0, keepdims=True),
                jnp.zeros((4, d), F32),
            ],
            axis=0,
        )

        @pl.when(i == 0)
        def _():
            red_ref[...] = part

        @pl.when(i > 0)
        def _():
            red_ref[...] += part

        @pl.when(i == n_steps - 1)
        def _():
            red_ref[4:5, :] = jnp.broadcast_to(jnp.sum(red_ref[3:4, :], axis=1, keepdims=True), (1, d))

    row = pl.BlockSpec((tm, d), lambda i: (i, 0))
    vec = pl.BlockSpec((1, d), lambda i: (0, 0))
    return _pcall(
        body,
        name="out_stage",
        grid=(n_steps,),
        out_shape=[
            jax.ShapeDtypeStruct((s_len, d), BF16),
            jax.ShapeDtypeStruct((s_len, d), F32),
            jax.ShapeDtypeStruct((s_len, d), BF16),
            jax.ShapeDtypeStruct((8, d), F32),
        ],
        in_specs=[
            pl.BlockSpec((tm, d), lambda i: (i, mf_blk)),
            pl.BlockSpec((tm, d), lambda i: (i, mf_blk + 1)),
            row,
            row,
            pl.BlockSpec((d, d), lambda i: (0, 0), pipeline_mode=pl.Buffered(1)),
            row,
            pl.BlockSpec((1, d), lambda i: (0, 2)),
            vec,
            vec,
            row,
        ],
        out_specs=[row, row, row, pl.BlockSpec((8, d), lambda i: (0, 0))],
        compiler_params=_cp("arbitrary"),
    )(gates, gates, y_fox, y_swa, w_out, x, ada, ln_g, ln_b, target)


def _merge_bwd(dsub, w_out, gates, mf_blk, y_fox, y_swa):
    s_len, d = dsub.shape
    tm = min(256, s_len)

    def body(ds_ref, w_ref, mf_ref, ms_ref, yf_ref, ys_ref, dmf_ref, dms_ref, dyf_ref, dys_ref):
        dm = lax.dot_general(ds_ref[...], w_ref[...], _NT, preferred_element_type=F32)
        sf, ss = _sigmoid(mf_ref[...]), _sigmoid(ms_ref[...])
        dmf_ref[...] = (dm * yf_ref[...] * (sf * (1.0 - sf))).astype(BF16)
        dms_ref[...] = (dm * ys_ref[...] * (ss * (1.0 - ss))).astype(BF16)
        dyf_ref[...] = (dm * sf).astype(BF16)
        dys_ref[...] = (dm * ss).astype(BF16)

    row = pl.BlockSpec((tm, d), lambda i: (i, 0))
    return _pcall(
        body,
        name="merge_bwd",
        grid=(s_len // tm,),
        out_shape=[jax.ShapeDtypeStruct((s_len, d), BF16)] * 4,
        in_specs=[
            row,
            pl.BlockSpec((d, d), lambda i: (0, 0), pipeline_mode=pl.Buffered(1)),
            pl.BlockSpec((tm, d), lambda i: (i, mf_blk)),
            pl.BlockSpec((tm, d), lambda i: (i, mf_blk + 1)),
            row,
            row,
        ],
        out_specs=[row] * 4,
        compiler_params=_cp("parallel"),
    )(dsub, w_out, gates, gates, y_fox, y_swa)


def _branch_bwd(dy, w_b, o, gates, g_blk, name, n_heads):
    s_len, d = dy.shape
    wd = w_b.shape[0]
    tm = min(512, s_len)

    def body(dy_ref, w_ref, o_ref, g_ref, do_ref, dg_ref, *rest):
        da = lax.dot_general(dy_ref[...], w_ref[...], _NT, preferred_element_type=F32)
        g = g_ref[...]
        sg = _sigmoid(g)
        do = da * (g * sg)
        do_ref[...] = do.astype(BF16)
        o = o_ref[...]
        dg_ref[...] = (da * o * (sg * (1.0 + g * (1.0 - sg)))).astype(BF16)
        if n_heads:
            prod = do * o
            lane = lax.broadcasted_iota(jnp.int32, (1, 128), 1)
            delta = jnp.zeros((tm, 128), F32)
            for h in range(n_heads):
                dh = jnp.sum(prod[:, h * 128 : (h + 1) * 128], axis=1, keepdims=True)
                delta = delta + jnp.where(lane == h, dh, 0.0)
            rest[0][...] = delta

    out_shape = [jax.ShapeDtypeStruct((s_len, wd), BF16), jax.ShapeDtypeStruct((s_len, wd), BF16)]
    out_specs = [pl.BlockSpec((tm, wd), lambda i: (i, 0))] * 2
    if n_heads:
        out_shape.append(jax.ShapeDtypeStruct((s_len, 128), F32))
        out_specs.append(pl.BlockSpec((tm, 128), lambda i: (i, 0)))
    return _pcall(
        body,
        name=name,
        grid=(s_len // tm,),
        out_shape=out_shape,
        in_specs=[
            pl.BlockSpec((tm, d), lambda i: (i, 0)),
            pl.BlockSpec((wd, d), lambda i: (0, 0)),
            pl.BlockSpec((tm, wd), lambda i: (i, 0)),
            pl.BlockSpec((tm, wd), lambda i: (i, g_blk)),
        ],
        out_specs=out_specs,
        compiler_params=_cp("parallel"),
    )(dy, w_b, o, gates)


def _in_bwd(dproj, w_in_t, x, ada, dza, ride):
    s_len, d = x.shape
    k_tot = dproj.shape[1]
    tm, tk = min(512, s_len), 1024
    ni, nk = s_len // tm, k_tot // tk
    n = len(ride)

    def body(dp_ref, w_ref, x_ref, sc_ref, dza_ref, *rest):
        ins, (gx_ref, red_ref), outs = rest[:n], rest[n : n + 2], rest[n + 2 : 2 * n + 2]
        sems, acc_s = rest[2 * n + 2 : 2 * n + 5], rest[2 * n + 5]
        i, kk = pl.program_id(0), pl.program_id(1)

        @pl.when((i == 0) & (kk == 0))
        def _():
            _rider_start("exchange", ins, outs, *sems)

        @pl.when((i == ni - 1) & (kk == nk - 1))
        def _():
            _rider_wait("exchange", ins, outs, *sems)

        part = jnp.dot(dp_ref[...], w_ref[...], preferred_element_type=F32)

        @pl.when(kk == 0)
        def _():
            acc_s[...] = part

        @pl.when(kk > 0)
        def _():
            acc_s[...] += part

        @pl.when(kk == nk - 1)
        def _():
            dh = acc_s[...]
            xv = x_ref[...]
            mu = jnp.mean(xv, axis=-1, keepdims=True)
            xc = xv - mu
            var = jnp.mean(xc * xc, axis=-1, keepdims=True)
            rstd = lax.rsqrt(var + LN_EPS)
            xhat = xc * rstd
            dxhat = dh * (1.0 + sc_ref[...])
            dx = rstd * (dxhat - jnp.mean(dxhat, axis=-1, keepdims=True) - xhat * jnp.mean(dxhat * xhat, axis=-1, keepdims=True))
            gx_ref[...] = dza_ref[...] + dx
            part_r = jnp.concatenate(
                [jnp.sum(dh, axis=0, keepdims=True), jnp.sum(dh * xhat, axis=0, keepdims=True), jnp.zeros((6, d), F32)], axis=0
            )

            @pl.when(i == 0)
            def _():
                red_ref[...] = part_r

            @pl.when(i > 0)
            def _():
                red_ref[...] += part_r

    row = pl.BlockSpec((tm, d), lambda i, kk: (i, 0))
    hbm = pl.BlockSpec(memory_space=pltpu.HBM)
    return _pcall(
        body,
        name="in_bwd",
        grid=(ni, nk),
        out_shape=[jax.ShapeDtypeStruct((s_len, d), F32), jax.ShapeDtypeStruct((8, d), F32)]
        + [jax.ShapeDtypeStruct(r.shape, r.dtype) for r in ride],
        in_specs=[
            pl.BlockSpec((tm, tk), lambda i, kk: (i, kk)),
            pl.BlockSpec((tk, d), lambda i, kk: (kk, 0)),
            row,
            pl.BlockSpec((1, d), lambda i, kk: (0, 1)),
            row,
        ]
        + [hbm] * n,
        out_specs=[row, pl.BlockSpec((8, d), lambda i, kk: (0, 0))] + [hbm] * n,
        scratch_shapes=_rider_scratch(n) + [pltpu.VMEM((tm, d), F32)],
        compiler_params=_cp("arbitrary", "arbitrary"),
    )(dproj, w_in_t, x, ada, dza, *ride)


def _pad_lanes(v, n):
    return jnp.pad(v, ((0, 0), (0, n - v.shape[1])))


def kernel(x, c, w_ada, b_ada, w_in, b_f, attn_sinks, w_br_fox, w_br_swa, w_out, ln_g, ln_b, loss_target, m_w_ada, m_b_ada, m_w_in, m_b_f, m_attn_sinks, m_w_br_fox, m_w_br_swa, m_w_out, m_ln_g, m_ln_b, v_w_ada, v_b_ada, v_w_in, v_b_f, v_attn_sinks, v_w_br_fox, v_w_br_swa, v_w_out, v_ln_g, v_ln_b):
    x2, tgt = x[0], loss_target[0]
    s_len, d = x2.shape
    me = 4 * lax.axis_index("x") + 2 * lax.axis_index("y") + lax.axis_index("c")
    off_ms = OFF_MF + d
    in_pad = off_ms + d
    c_ada = w_ada.shape[2]
    c_in = w_in.shape[2]
    c_br = w_br_fox.shape[2]

    w_in_full = _all_gather(w_in[0].T.astype(BF16), "ag_w_in", pltpu.HBM).reshape(N_DEV * c_in, d)
    w_in_pad = jnp.concatenate(
        [w_in_full[:REAL_FLOG_END], jnp.zeros((FLOG_PAD - N_FLOG, d), BF16), w_in_full[REAL_FLOG_END:]], axis=0
    )

    c_all = _gather_rows(c, "ag_c")
    b_cols = lax.dynamic_slice(b_ada, (0, me * c_ada), (1, c_ada))
    ada_cols = _ada_fwd(c_all, w_ada[0], b_cols)
    ada_g = _all_gather(ada_cols, "ag_ada", pltpu.VMEM)
    ada = lax.dynamic_index_in_dim(ada_g, me, axis=1, keepdims=False).reshape(1, N_DEV * c_ada)

    h = _ln_mod(x2, ada)
    qkv_fox = _mm_cols(h, w_in_pad, OFF_FQ, 3 * FOX_W, BF16, "proj_fox")
    flog = _mm_cols(h, w_in_pad, OFF_FLOG, FLOG_PAD, F32, "proj_flog")
    qkv_swa = _mm_cols(h, w_in_pad, OFF_SQ, SWA_W + 2 * SWA_KVW, BF16, "proj_swa")
    gates, w_bf, w_bs, w_o = _mm_cols(
        h, w_in_pad, OFF_GF, in_pad - OFF_GF, F32, "proj_gates",
        ride=(w_br_fox[0].astype(BF16), w_br_swa[0].astype(BF16), w_out[0].astype(BF16)),
    )
    w_bf = w_bf.reshape(N_DEV, FOX_W, c_br).transpose(1, 0, 2).reshape(FOX_W, d)
    w_bs = w_bs.reshape(N_DEV, SWA_W, c_br).transpose(1, 0, 2).reshape(SWA_W, d)
    w_o = w_o.reshape(d, d)
    mf_blk = (OFF_MF - OFF_GF) // d

    flog_t = flog[:, :N_FLOG].T
    bf_col = b_f.reshape(FOX_H, 1)
    cum = _fox_cum(flog_t, bf_col)
    cum_row = cum.reshape(FOX_H, 1, s_len)
    o_fox, lse = _fox_fwd(qkv_fox, cum_row)
    sinks = attn_sinks.reshape(SWA_HQ)
    o_swa = _swa_fwd(qkv_swa, sinks)

    y_fox, a_fox = _branch_fwd(o_fox, gates, 0, w_bf, "branch_fox")
    y_swa, a_swa = _branch_fwd(o_swa, gates, 1, w_bs, "branch_swa")
    merged, dza, dsub, red = _out_stage(gates, mf_blk, y_fox, y_swa, w_o, x2, ada, ln_g, ln_b, tgt)
    loss = lax.psum(0.5 * red[4, 0] / d, ("x", "y", "c"))

    dmf, dms, dy_fox, dy_swa = _merge_bwd(dsub, w_o, gates, mf_blk, y_fox, y_swa)
    do_fox, dg_fox, delta = _branch_bwd(dy_fox, w_bf, o_fox, gates, 0, "branch_fox_bwd", FOX_H)
    do_swa, dg_swa = _branch_bwd(dy_swa, w_bs, o_swa, gates, 1, "branch_swa_bwd", 0)
    delta_row = delta[:, :FOX_H].T.reshape(FOX_H, 1, s_len)
    dq_f, dk_f, dv_f, dcol, drow = _fox_bwd(
        qkv_fox, cum.reshape(FOX_H, s_len, 1), lse.reshape(FOX_H, 1, s_len), delta_row, do_fox
    )
    dflog_t, dbf = _fox_gate_bwd(drow.reshape(FOX_H, s_len), dcol.reshape(FOX_H, s_len), flog_t, bf_col)
    dq_s, dk_s, dv_s, dsink = _swa_bwd(qkv_swa, sinks, do_swa)
    dflog = _pad_lanes(dflog_t.T, FLOG_PAD).astype(BF16)
    dproj = jnp.concatenate([dq_f.astype(BF16), dk_f, dv_f, dflog, dq_s, dk_s, dv_s, dg_fox, dg_swa, dmf, dms], axis=1)
    g_w_in = _mm_tn(dproj, h, "grad_w_in")
    g_w_in = jnp.concatenate([g_w_in[:REAL_FLOG_END], g_w_in[OFF_SQ:]], axis=0)
    g_w_bf = _mm_tn(a_fox, dy_fox, "grad_w_br_fox")
    g_w_bs = _mm_tn(a_swa, dy_swa, "grad_w_br_swa")
    g_w_o = _mm_tn(merged, dsub, "grad_w_out")

    grad_x, red2, r_in, r_bf, r_bs, r_o = _in_bwd(
        dproj, w_in_pad, x2, ada, dza,
        ride=(
            g_w_in.reshape(N_DEV, c_in, d),
            g_w_bf.reshape(FOX_W, N_DEV, c_br).transpose(1, 0, 2),
            g_w_bs.reshape(SWA_W, N_DEV, c_br).transpose(1, 0, 2),
            g_w_o.reshape(N_DEV, d // N_DEV, d),
        ),
    )
    out_w_in = _sum_adam_t(r_in, w_in[0].T, m_w_in[0].T, v_w_in[0].T, "adam_w_in")
    out_w_in = [o.T for o in out_w_in]
    out_w_bf = _sum_adam(r_bf, w_br_fox[0], m_w_br_fox[0], v_w_br_fox[0], "adam_w_br_fox")
    out_w_bs = _sum_adam(r_bs, w_br_swa[0], m_w_br_swa[0], v_w_br_swa[0], "adam_w_br_swa")
    out_w_o = _sum_adam(r_o, w_out[0], m_w_out[0], v_w_out[0], "adam_w_out")

    packed = jnp.concatenate([red2[0:1], red2[1:2], red[0:1], _pad_lanes(dbf[:, 0].reshape(1, FOX_H), 128), dsink, red[1:2], red[2:3]], axis=1)
    gathered = _gather_rows(packed, "ag_small")
    pack = lambda a, b, cc, dd, e: jnp.concatenate([a, _pad_lanes(b, 128), _pad_lanes(cc, 128), dd, e], axis=1)
    small = _small_adam(
        gathered,
        pack(b_ada, b_f, attn_sinks, ln_g, ln_b),
        pack(m_b_ada, m_b_f, m_attn_sinks, m_ln_g, m_ln_b),
        pack(v_b_ada, v_b_f, v_attn_sinks, v_ln_g, v_ln_b),
    )
    dada_cols = lax.dynamic_slice(gathered, (0, me * c_ada), (N_DEV, c_ada))
    out_w_ada = _wada_adam(c_all.T, dada_cols, w_ada[0], m_w_ada[0], v_w_ada[0])

    o1, o2, o3 = 3 * d, 3 * d + 128, 3 * d + 256

    def unpack(p):
        return p[:, :o1], p[:, o1 : o1 + FOX_H], p[:, o2 : o2 + SWA_HQ], p[:, o3 : o3 + d], p[:, o3 + d : o3 + 2 * d]

    kinds = []
    for k in range(4):
        b_ada_k, b_f_k, sinks_k, ln_g_k, ln_b_k = unpack(small[k])
        kinds.append(
            [out_w_ada[k][None], b_ada_k, out_w_in[k][None], b_f_k, sinks_k, out_w_bf[k][None], out_w_bs[k][None], out_w_o[k][None], ln_g_k, ln_b_k]
        )
    return (loss, grad_x[None], *kinds[0], *kinds[1], *kinds[2], *kinds[3])
```

```python
import numpy as np
import jax
import jax.numpy as jnp
from jax import lax
from jax.experimental import pallas as pl
from jax.experimental.pallas import tpu as pltpu

F32 = jnp.float32
BF16 = jnp.bfloat16
N_DEV = 8
MESH = pl.DeviceIdType.MESH

FOX_H, FOX_DH, FOX_W = 8, 128, 1024
SWA_HQ, SWA_HKV, SWA_DH, SWA_G = 16, 4, 64, 4
SWA_W, SWA_KVW, WINDOW = 1024, 256, 128
LN_EPS = 1e-5
NEG = -1e30
DEPTH = 1
ALPHA = (2.0 * DEPTH) ** 0.25
FOX_SCALE = FOX_DH ** -0.5
SWA_SCALE = SWA_DH ** -0.5
SLOPES = [2.0 ** (-8.0 * (h + 1.0) / SWA_HQ) for h in range(SWA_HQ)]

ADAM_LR, ADAM_B1, ADAM_B2, ADAM_EPS, ADAM_WD, ADAM_STEP = 0.001, 0.9, 0.999, 1e-08, 0.01, 10

N_FLOG = 8
FLOG_PAD = 512
OFF_FQ, OFF_FK, OFF_FV, OFF_FLOG = 0, 1024, 2048, 3072
OFF_SQ = OFF_FLOG + FLOG_PAD
OFF_SK = OFF_SQ + SWA_W
OFF_SV = OFF_SK + SWA_KVW
OFF_GF = OFF_SV + SWA_KVW
OFF_GS = OFF_GF + FOX_W
OFF_MF = OFF_GS + SWA_W
REAL_FLOG_END = OFF_FLOG + N_FLOG

ATT_BLK = 512
VMEM_LIMIT = 52 * 1024 * 1024


def _pcall(body, **kw):
    return pl.pallas_call(body, **kw)


def _cp(*sem):
    return pltpu.CompilerParams(dimension_semantics=sem, vmem_limit_bytes=VMEM_LIMIT)


def _sigmoid(x):
    return 1.0 / (1.0 + jnp.exp(-x))


def _all_gather(x, name, space):
    m_per, n = x.shape

    def body(x_ref, out_ref, send_sems, recv_sems, local_sem):
        mx, my, mc = lax.axis_index("x"), lax.axis_index("y"), lax.axis_index("c")
        me, sibling = (mx, my, mc), (mx, my, 1 - mc)
        chips = [(1 - mx, my), (mx, 1 - my), (1 - mx, 1 - my)]

        def rows(px, py, pc):
            return out_ref.at[4 * px + 2 * py + pc]

        def copy(k, block, to, src=None):
            return pltpu.make_async_remote_copy(
                src_ref=rows(*block) if src is None else src,
                dst_ref=rows(*block),
                send_sem=send_sems.at[k],
                recv_sem=recv_sems.at[k],
                device_id=to,
                device_id_type=MESH,
            )

        mine = pltpu.make_async_copy(x_ref, rows(*me), local_sem)
        mine.start()
        first = [copy(0, me, sibling, src=x_ref)]
        first += [copy(1 + j, me, (*chip, mc), src=x_ref) for j, chip in enumerate(chips)]
        for cp in first:
            cp.start()
        passed = [copy(4 + j, (*chip, mc), sibling) for j, chip in enumerate(chips)]
        for j, chip in enumerate(chips):
            copy(1 + j, (*chip, mc), me).wait_recv()
            passed[j].start()
        copy(0, sibling, me).wait_recv()
        for j, chip in enumerate(chips):
            copy(4 + j, (*chip, 1 - mc), me).wait_recv()
        for cp in first + passed:
            cp.wait_send()
        mine.wait()

    return _pcall(
        body,
        name=name,
        out_shape=jax.ShapeDtypeStruct((N_DEV, m_per, n), x.dtype),
        in_specs=[pl.BlockSpec(memory_space=space)],
        out_specs=pl.BlockSpec(memory_space=space),
        scratch_shapes=[pltpu.SemaphoreType.DMA((7,)), pltpu.SemaphoreType.DMA((7,)), pltpu.SemaphoreType.DMA],
    )(x)


def _peer(d, mx, my, mc):
    return (1 - mx if (d >> 2) & 1 else mx, 1 - my if (d >> 1) & 1 else my, 1 - mc if d & 1 else mc)


def _rider_copies(kind, ins, outs, send_sems, recv_sems, local_sems):
    mx, my, mc = lax.axis_index("x"), lax.axis_index("y"), lax.axis_index("c")
    me = 4 * mx + 2 * my + mc
    remote, local = [], []
    for a in range(len(ins)):
        if kind == "gather":
            m_per = ins[a].shape[0]
            mine = outs[a].at[pl.ds(me * m_per, m_per), :]
            local.append(pltpu.make_async_copy(ins[a], mine, local_sems.at[a]))
        else:
            local.append(pltpu.make_async_copy(ins[a].at[me], outs[a].at[0], local_sems.at[a]))
        for d in range(1, N_DEV):
            px, py, pc = _peer(d, mx, my, mc)
            if kind == "gather":
                src, dst = ins[a], mine
            else:
                src, dst = ins[a].at[4 * px + 2 * py + pc], outs[a].at[d]
            remote.append(
                pltpu.make_async_remote_copy(
                    src_ref=src,
                    dst_ref=dst,
                    send_sem=send_sems.at[a * 7 + d - 1],
                    recv_sem=recv_sems.at[a * 7 + d - 1],
                    device_id=(px, py, pc),
                    device_id_type=MESH,
                )
            )
    return remote, local


def _rider_start(*args):
    remote, local = _rider_copies(*args)
    for cp in local + remote:
        cp.start()


def _rider_wait(*args):
    remote, local = _rider_copies(*args)
    for cp in remote:
        cp.wait_recv()
    for cp in remote:
        cp.wait_send()
    for cp in local:
        cp.wait()


def _rider_scratch(n):
    return [pltpu.SemaphoreType.DMA((7 * n,)), pltpu.SemaphoreType.DMA((7 * n,)), pltpu.SemaphoreType.DMA((n,))]


def _gather_rows(v, name):
    n = v.shape[1]
    return _all_gather(jnp.broadcast_to(v, (8, n)), name, pltpu.VMEM)[:, 0, :]


def _adamw(w, g, m, v):
    m = ADAM_B1 * m + (1.0 - ADAM_B1) * g
    v = ADAM_B2 * v + (1.0 - ADAM_B2) * (g * g)
    m_hat = m / (1.0 - ADAM_B1**ADAM_STEP)
    v_hat = v / (1.0 - ADAM_B2**ADAM_STEP)
    delta = -ADAM_LR * (m_hat / (jnp.sqrt(v_hat) + ADAM_EPS) + ADAM_WD * w)
    return delta, m, v


def _sum_adam(recv, w, m, v, name):
    _, r_tot, c = recv.shape
    c_pad = -(-c // 128) * 128
    tr = r_tot
    while 8 * tr * c_pad * 4 > 6 * 1024 * 1024 and tr % 32 == 0:
        tr //= 2

    def body(r_ref, w_ref, m_ref, v_ref, g_ref, d_ref, nm_ref, nv_ref):
        g = r_ref[0].astype(F32)
        for k in range(1, N_DEV):
            g = g + r_ref[k].astype(F32)
        d, nm, nv = _adamw(w_ref[...], g, m_ref[...], v_ref[...])
        g_ref[...] = g
        d_ref[...] = d
        nm_ref[...] = nm
        nv_ref[...] = nv

    blk = pl.BlockSpec((tr, c), lambda i: (i, 0))
    return _pcall(
        body,
        name=name,
        grid=(r_tot // tr,),
        out_shape=[jax.ShapeDtypeStruct((r_tot, c), F32)] * 4,
        in_specs=[pl.BlockSpec((N_DEV, tr, c), lambda i: (0, i, 0)), blk, blk, blk],
        out_specs=[blk] * 4,
        compiler_params=_cp("parallel"),
    )(recv, w, m, v)


def _sum_adam_t(recv, w, m, v, name):
    _, c, r_tot = recv.shape
    tr = min(256, r_tot)

    def body(r_ref, w_ref, m_ref, v_ref, g_ref, d_ref, nm_ref, nv_ref):
        g = r_ref[0].astype(F32)
        for k in range(1, N_DEV):
            g = g + r_ref[k].astype(F32)
        d, nm, nv = _adamw(w_ref[...], g, m_ref[...], v_ref[...])
        g_ref[...] = g
        d_ref[...] = d
        nm_ref[...] = nm
        nv_ref[...] = nv

    blk = pl.BlockSpec((c, tr), lambda i: (0, i))
    return _pcall(
        body,
        name=name,
        grid=(r_tot // tr,),
        out_shape=[jax.ShapeDtypeStruct((c, r_tot), F32)] * 4,
        in_specs=[pl.BlockSpec((N_DEV, c, tr), lambda i: (0, 0, i)), blk, blk, blk],
        out_specs=[blk] * 4,
        compiler_params=_cp("parallel"),
    )(recv, w, m, v)


def _wada_adam(c_t, dada_cols, w, m, v):
    d_model, c = w.shape
    tr = min(256, d_model)

    def body(ct_ref, da_ref, w_ref, m_ref, v_ref, g_ref, d_ref, nm_ref, nv_ref):
        g = jnp.dot(ct_ref[...].astype(BF16), da_ref[...].astype(BF16), preferred_element_type=F32)
        d, nm, nv = _adamw(w_ref[...], g, m_ref[...], v_ref[...])
        g_ref[...] = g
        d_ref[...] = d
        nm_ref[...] = nm
        nv_ref[...] = nv

    blk = pl.BlockSpec((tr, c), lambda i: (i, 0))
    return _pcall(
        body,
        name="wada_adam",
        grid=(d_model // tr,),
        out_shape=[jax.ShapeDtypeStruct((d_model, c), F32)] * 4,
        in_specs=[pl.BlockSpec((tr, N_DEV), lambda i: (i, 0)), pl.BlockSpec((N_DEV, c), lambda i: (0, 0)), blk, blk, blk],
        out_specs=[blk] * 4,
        compiler_params=_cp("parallel"),
    )(c_t, dada_cols, w, m, v)


def _small_adam(gathered, w, m, v):
    p = w.shape[1]

    def body(a_ref, w_ref, m_ref, v_ref, g_ref, d_ref, nm_ref, nv_ref):
        g = a_ref[0:1, :]
        for k in range(1, N_DEV):
            g = g + a_ref[k : k + 1, :]
        d, nm, nv = _adamw(w_ref[...], g, m_ref[...], v_ref[...])
        g_ref[...] = g
        d_ref[...] = d
        nm_ref[...] = nm
        nv_ref[...] = nv

    return _pcall(
        body,
        name="small_adam",
        out_shape=[jax.ShapeDtypeStruct((1, p), F32)] * 4,
    )(gathered, w, m, v)


def _ada_fwd(c_all, w_ada, b_cols):
    c = w_ada.shape[1]

    def body(c_ref, w_ref, b_ref, o_ref):
        o_ref[...] = jnp.dot(c_ref[...].astype(BF16), w_ref[...].astype(BF16), preferred_element_type=F32) + b_ref[...]

    return _pcall(
        body,
        name="ada_fwd",
        out_shape=jax.ShapeDtypeStruct((N_DEV, c), F32),
        compiler_params=_cp(),
    )(c_all, w_ada, b_cols)


def _ln_mod(x, ada):
    s_len, d = x.shape
    tm = min(512, s_len)

    def body(x_ref, sh_ref, sc_ref, h_ref):
        xv = x_ref[...]
        mu = jnp.mean(xv, axis=-1, keepdims=True)
        xc = xv - mu
        var = jnp.mean(xc * xc, axis=-1, keepdims=True)
        xhat = xc * lax.rsqrt(var + LN_EPS)
        h_ref[...] = (xhat * (1.0 + sc_ref[...]) + sh_ref[...]).astype(BF16)

    return _pcall(
        body,
        name="ln_mod",
        grid=(s_len // tm,),
        out_shape=jax.ShapeDtypeStruct((s_len, d), BF16),
        in_specs=[
            pl.BlockSpec((tm, d), lambda i: (i, 0)),
            pl.BlockSpec((1, d), lambda i: (0, 0)),
            pl.BlockSpec((1, d), lambda i: (0, 1)),
        ],
        out_specs=pl.BlockSpec((tm, d), lambda i: (i, 0)),
        compiler_params=_cp("parallel"),
    )(x, ada, ada)


def _mm_cols(a, b, col_off, n_cols, out_dtype, name, ride=()):
    m, k = a.shape
    tm, tn = min(1024, m), 512
    off = col_off // tn
    ni, nj = m // tm, n_cols // tn
    n = len(ride)

    def body(a_ref, b_ref, *rest):
        ins, o_ref, outs, sems = rest[:n], rest[n], rest[n + 1 : 2 * n + 1], rest[2 * n + 1 :]
        i, j = pl.program_id(0), pl.program_id(1)
        if n:

            @pl.when((i == 0) & (j == 0))
            def _():
                _rider_start("gather", ins, outs, *sems)

        o_ref[...] = lax.dot_general(a_ref[...], b_ref[...], _NT, preferred_element_type=F32).astype(out_dtype)
        if n:

            @pl.when((i == ni - 1) & (j == nj - 1))
            def _():
                _rider_wait("gather", ins, outs, *sems)

    hbm = pl.BlockSpec(memory_space=pltpu.HBM)
    out = _pcall(
        body,
        name=name,
        grid=(ni, nj),
        out_shape=[jax.ShapeDtypeStruct((m, n_cols), out_dtype)]
        + [jax.ShapeDtypeStruct((N_DEV * r.shape[0], r.shape[1]), r.dtype) for r in ride],
        in_specs=[pl.BlockSpec((tm, k), lambda i, j: (i, 0)), pl.BlockSpec((tn, k), lambda i, j: (off + j, 0))] + [hbm] * n,
        out_specs=[pl.BlockSpec((tm, tn), lambda i, j: (i, j))] + [hbm] * n,
        scratch_shapes=_rider_scratch(n) if n else [],
        compiler_params=_cp("arbitrary", "arbitrary") if n else _cp("parallel", "parallel"),
    )(a, b, *ride)
    return out if n else out[0]


def _mm_tn(a, b, name, ride=()):
    s_len, m = a.shape
    n = b.shape[1]
    tm, tn, ts = min(1024, m), min(1024, n), min(1024, s_len)
    ni, nj, ns = m // tm, n // tn, s_len // ts
    nr = len(ride)

    def body(a_ref, b_ref, *rest):
        ins, o_ref, outs = rest[:nr], rest[nr], rest[nr + 1 : 2 * nr + 1]
        sems, acc_s = rest[2 * nr + 1 : -1], rest[-1]
        i, j, kk = pl.program_id(0), pl.program_id(1), pl.program_id(2)
        if nr:

            @pl.when((i == 0) & (j == 0) & (kk == 0))
            def _():
                _rider_start("exchange", ins, outs, *sems)

            @pl.when((i == ni - 1) & (j == nj - 1) & (kk == ns - 1))
            def _():
                _rider_wait("exchange", ins, outs, *sems)

        part = lax.dot_general(a_ref[...], b_ref[...], _TN, preferred_element_type=F32)

        @pl.when(kk == 0)
        def _():
            acc_s[...] = part

        @pl.when(kk > 0)
        def _():
            acc_s[...] += part

        @pl.when(kk == ns - 1)
        def _():
            o_ref[...] = acc_s[...].astype(BF16)

    hbm = pl.BlockSpec(memory_space=pltpu.HBM)
    out = _pcall(
        body,
        name=name,
        grid=(ni, nj, ns),
        out_shape=[jax.ShapeDtypeStruct((m, n), BF16)] + [jax.ShapeDtypeStruct(r.shape, r.dtype) for r in ride],
        in_specs=[pl.BlockSpec((ts, tm), lambda i, j, kk: (kk, i)), pl.BlockSpec((ts, tn), lambda i, j, kk: (kk, j))] + [hbm] * nr,
        out_specs=[pl.BlockSpec((tm, tn), lambda i, j, kk: (i, j))] + [hbm] * nr,
        scratch_shapes=(_rider_scratch(nr) if nr else []) + [pltpu.VMEM((tm, tn), F32)],
        compiler_params=_cp("arbitrary", "arbitrary", "arbitrary") if nr else _cp("parallel", "parallel", "arbitrary"),
    )(a, b, *ride)
    return out if nr else out[0]


def _split3(a):
    hi = a.astype(BF16)
    r1 = a - hi.astype(F32)
    mid = r1.astype(BF16)
    lo = (r1 - mid.astype(F32)).astype(BF16)
    return hi, mid, lo


def _dot_ones(a, tri):
    return sum(jnp.dot(t, tri, preferred_element_type=F32) for t in _split3(a))


def _log_sigmoid(x):
    return jnp.minimum(x, 0.0) - jnp.log1p(jnp.exp(-jnp.abs(x)))


def _fox_cum(flog_t, bf_col):
    s_len = flog_t.shape[1]

    def body(fl_ref, bf_ref, cum_ref):
        r = lax.broadcasted_iota(jnp.int32, (128, 128), 0)
        c = lax.broadcasted_iota(jnp.int32, (128, 128), 1)
        upper = (r <= c).astype(BF16)

        def step(t, carry):
            sl = pl.ds(pl.multiple_of(t * 128, 128), 128)
            lf = _log_sigmoid(fl_ref[:, sl] + bf_ref[...])
            cs = _dot_ones(lf, upper) + carry
            cum_ref[:, sl] = cs
            return cs[:, 127:128]

        lax.fori_loop(0, s_len // 128, step, jnp.zeros((FOX_H, 1), F32))

    return _pcall(body, name="fox_cum", out_shape=jax.ShapeDtypeStruct((FOX_H, s_len), F32))(flog_t, bf_col)


def _fox_gate_bwd(drow, dcol, flog_t, bf_col):
    s_len = flog_t.shape[1]
    n = s_len // 128

    def body(dr_ref, dc_ref, fl_ref, bf_ref, dfl_ref, dbf_ref):
        r = lax.broadcasted_iota(jnp.int32, (128, 128), 0)
        c = lax.broadcasted_iota(jnp.int32, (128, 128), 1)
        lower = (r >= c).astype(BF16)

        def step(t, carry):
            run, tot = carry
            sl = pl.ds(pl.multiple_of((n - 1 - t) * 128, 128), 128)
            rc = _dot_ones(dr_ref[:, sl] - dc_ref[:, sl], lower) + run
            dfl = rc * _sigmoid(-(fl_ref[:, sl] + bf_ref[...]))
            dfl_ref[:, sl] = dfl
            return rc[:, 0:1], tot + jnp.sum(dfl, axis=1, keepdims=True)

        zero = jnp.zeros((FOX_H, 1), F32)
        _, tot = lax.fori_loop(0, n, step, (zero, zero))
        dbf_ref[...] = jnp.broadcast_to(tot, (FOX_H, 128))

    return _pcall(
        body,
        name="fox_gate_bwd",
        out_shape=[jax.ShapeDtypeStruct((FOX_H, s_len), F32), jax.ShapeDtypeStruct((FOX_H, 128), F32)],
    )(drow, dcol, flog_t, bf_col)


def _diag_mask(blk, transposed=False):
    r = lax.broadcasted_iota(jnp.int32, (blk, blk), 0)
    c = lax.broadcasted_iota(jnp.int32, (blk, blk), 1)
    return c >= r if transposed else r >= c


_NT = (((1,), (1,)), ((), ()))
_TN = (((0,), (0,)), ((), ()))


def _fox_fwd(qkv, cum_row):
    s_len = qkv.shape[0]
    blk = min(ATT_BLK, s_len)
    nb = s_len // blk
    log2e = 1.4426950408889634

    def body(q_ref, k_ref, v_ref, c_ref, o_ref, lse_ref, mx_s, acc_s):
        i = pl.program_id(1)

        def logits(j, n, masked):
            cols = pl.ds(pl.multiple_of(j * blk, blk), n * blk)
            u = lax.dot_general(q_ref[...], k_ref[cols, :], _NT, preferred_element_type=F32) - c_ref[:, cols] * (1.0 / FOX_SCALE)
            if masked:
                u = jnp.where(_diag_mask(blk), u, NEG)
            return u, cols

        def walk(tile):
            lax.fori_loop(0, i // 2, lambda t, c: (tile(2 * t, 2, False), c)[1], 0)

            @pl.when(i % 2 == 1)
            def _():
                tile(i - 1, 1, False)

            tile(i, 1, True)

        def lane_max(j, n, masked):
            u, _ = logits(j, n, masked)
            part = u[:, 0:128]
            for t in range(1, n * blk // 128):
                part = jnp.maximum(part, u[:, t * 128 : (t + 1) * 128])
            mx_s[...] = jnp.maximum(mx_s[...], part)

        mx_s[...] = jnp.full(mx_s.shape, NEG, F32)
        walk(lane_max)
        m = jnp.max(mx_s[...], axis=1, keepdims=True)

        def weigh(j, n, masked):
            u, cols = logits(j, n, masked)
            p = jnp.exp2((u - m) * (FOX_SCALE * log2e))
            ones_col = (lax.broadcasted_iota(jnp.int32, (n * blk, 128), 1) == 0).astype(BF16)
            v1 = jnp.concatenate([v_ref[cols, :], ones_col], axis=1)
            acc_s[...] += jnp.dot(p.astype(BF16), v1, preferred_element_type=F32)

        acc_s[...] = jnp.zeros(acc_s.shape, F32)
        walk(weigh)
        l = acc_s[:, FOX_DH : FOX_DH + 1]
        o_ref[...] = acc_s[:, :FOX_DH] / l
        lse_ref[...] = m * FOX_SCALE + jnp.log(l)

    return _pcall(
        body,
        name="fox_fwd",
        grid=(FOX_H, nb),
        out_shape=[jax.ShapeDtypeStruct((s_len, FOX_W), F32), jax.ShapeDtypeStruct((FOX_H, s_len, 1), F32)],
        in_specs=[
            pl.BlockSpec((blk, FOX_DH), lambda h, i: (i, h)),
            pl.BlockSpec((s_len, FOX_DH), lambda h, i: (0, FOX_H + h)),
            pl.BlockSpec((s_len, FOX_DH), lambda h, i: (0, 2 * FOX_H + h)),
            pl.BlockSpec((None, 1, s_len), lambda h, i: (h, 0, 0)),
        ],
        out_specs=[
            pl.BlockSpec((blk, FOX_DH), lambda h, i: (i, h)),
            pl.BlockSpec((None, blk, 1), lambda h, i: (h, i, 0)),
        ],
        scratch_shapes=[pltpu.VMEM((blk, 128), F32), pltpu.VMEM((blk, 2 * FOX_DH), F32)],
        compiler_params=_cp("parallel", "arbitrary"),
    )(qkv, qkv, qkv, cum_row)


def _fox_bwd(qkv, cum_col, lse_row, delta_row, do):
    s_len = qkv.shape[0]
    blk = min(ATT_BLK, s_len)
    nb = s_len // blk

    def body(q_ref, k_ref, v_ref, c_ref, lse_ref, dl_ref, do_ref, dq_ref, dk_ref, dv_ref, dc_ref, dr_ref, dk_s, dv_s, dc_s, cb_s):
        j = pl.program_id(1)

        @pl.when(j == 0)
        def _():
            dq_ref[...] = jnp.zeros(dq_ref.shape, F32)
            dr_ref[...] = jnp.zeros(dr_ref.shape, F32)

        dk_s[...] = jnp.zeros(dk_s.shape, F32)
        dv_s[...] = jnp.zeros(dv_s.shape, F32)
        dc_s[...] = jnp.zeros(dc_s.shape, F32)
        cb_s[...] = jnp.broadcast_to(c_ref[...], cb_s.shape)

        def tile(i, n, diag):
            rows = pl.ds(pl.multiple_of(i * blk, blk), n * blk)
            q, dob = q_ref[rows, :], do_ref[rows, :]
            k, v = k_ref[...], v_ref[...]
            s_t = lax.dot_general(k, q, _NT, preferred_element_type=F32) * FOX_SCALE - cb_s[:, : n * blk]
            p_t = jnp.exp(s_t - lse_ref[:, rows])
            if diag:
                p_t = jnp.where(_diag_mask(blk, transposed=True), p_t, 0.0)
            dp_t = lax.dot_general(v, dob, _NT, preferred_element_type=F32)
            ds_t = p_t * (dp_t - dl_ref[:, rows])
            dsb = ds_t.astype(BF16)
            dv_s[...] += jnp.dot(p_t.astype(BF16), dob, preferred_element_type=F32)
            dk_s[...] += jnp.dot(dsb, q, preferred_element_type=F32)
            dq_c = lax.dot_general(dsb, k, _TN, preferred_element_type=F32)
            part = ds_t[:, 0:128]
            for t in range(1, n * blk // 128):
                part = part + ds_t[:, t * 128 : (t + 1) * 128]
            dc_s[...] += part
            dr_ref[:, rows] += jnp.sum(ds_t, axis=0, keepdims=True)
            if diag:
                dq_ref[rows, :] = (dq_ref[rows, :] + dq_c) * FOX_SCALE
            else:
                dq_ref[rows, :] += dq_c

        tile(j, 1, True)
        odd = (nb - 1 - j) % 2

        @pl.when(odd == 1)
        def _():
            tile(j + 1, 1, False)

        lax.fori_loop(0, (nb - 1 - j) // 2, lambda t, c: (tile(j + 1 + odd + 2 * t, 2, False), c)[1], 0)
        dk_ref[...] = (dk_s[...] * FOX_SCALE).astype(BF16)
        dv_ref[...] = dv_s[...].astype(BF16)
        dc_ref[...] = jnp.sum(dc_s[...], axis=1, keepdims=True)

    head = lambda h, j: (0, h)
    row = pl.BlockSpec((None, 1, s_len), lambda h, j: (h, 0, 0))
    return _pcall(
        body,
        name="fox_bwd",
        grid=(FOX_H, nb),
        out_shape=[
            jax.ShapeDtypeStruct((s_len, FOX_W), F32),
            jax.ShapeDtypeStruct((s_len, FOX_W), BF16),
            jax.ShapeDtypeStruct((s_len, FOX_W), BF16),
            jax.ShapeDtypeStruct((FOX_H, s_len, 1), F32),
            jax.ShapeDtypeStruct((FOX_H, 1, s_len), F32),
        ],
        in_specs=[
            pl.BlockSpec((s_len, FOX_DH), head),
            pl.BlockSpec((blk, FOX_DH), lambda h, j: (j, FOX_H + h)),
            pl.BlockSpec((blk, FOX_DH), lambda h, j: (j, 2 * FOX_H + h)),
            pl.BlockSpec((None, blk, 1), lambda h, j: (h, j, 0)),
            row,
            row,
            pl.BlockSpec((s_len, FOX_DH), head),
        ],
        out_specs=[
            pl.BlockSpec((s_len, FOX_DH), head),
            pl.BlockSpec((blk, FOX_DH), lambda h, j: (j, h)),
            pl.BlockSpec((blk, FOX_DH), lambda h, j: (j, h)),
            pl.BlockSpec((None, blk, 1), lambda h, j: (h, j, 0)),
            row,
        ],
        scratch_shapes=[
            pltpu.VMEM((blk, FOX_DH), F32),
            pltpu.VMEM((blk, FOX_DH), F32),
            pltpu.VMEM((blk, 128), F32),
            pltpu.VMEM((blk, 2 * blk), F32),
        ],
        compiler_params=_cp("parallel", "arbitrary"),
    )(qkv, qkv, qkv, cum_col, lse_row, delta_row, do)


def _swa_group(i, q_ref, kk, sinks_ref, g):
    rows = SWA_G * WINDOW
    r = lax.broadcasted_iota(jnp.int32, (rows, 2 * WINDOW), 0)
    c = lax.broadcasted_iota(jnp.int32, (rows, 2 * WINDOW), 1)
    dist = (r & (WINDOW - 1)) - c + WINDOW
    valid = (dist >= 0) & (dist < WINDOW) & ((c >= WINDOW) | (i > 0))
    head = lax.broadcasted_iota(jnp.int32, (rows, 1), 0) // WINDOW
    slope = jnp.zeros((rows, 1), F32)
    sink = jnp.zeros((rows, 1), F32)
    for t in range(SWA_G):
        h = g * SWA_G + t
        slope = jnp.where(head == t, SLOPES[h], slope)
        sink = jnp.where(head == t, sinks_ref[h], sink)
    q = jnp.concatenate([q_ref[:, (g * SWA_G + t) * SWA_DH : (g * SWA_G + t + 1) * SWA_DH] for t in range(SWA_G)], axis=0)
    k = kk[:, g * SWA_DH : (g + 1) * SWA_DH]
    s = lax.dot_general(q, k, _NT, preferred_element_type=F32) * SWA_SCALE - slope * dist.astype(F32)
    s = jnp.where(valid, s, NEG)
    m = jnp.maximum(jnp.max(s, axis=1, keepdims=True), sink)
    e = jnp.exp(s - m)
    e_sink = jnp.exp(sink - m)
    inv = 1.0 / (jnp.sum(e, axis=1, keepdims=True) + e_sink)
    return q, k, e * inv, e_sink * inv


def _swa_specs(col_q, col_k, col_v, rev, nb):
    def blk(t):
        return nb - 1 - t if rev else t

    return [
        pl.BlockSpec((WINDOW, SWA_W), lambda t: (blk(t), col_q)),
        pl.BlockSpec((WINDOW, SWA_KVW), lambda t: (jnp.maximum(blk(t) - 1, 0), col_k)),
        pl.BlockSpec((WINDOW, SWA_KVW), lambda t: (blk(t), col_k)),
        pl.BlockSpec((WINDOW, SWA_KVW), lambda t: (jnp.maximum(blk(t) - 1, 0), col_v)),
        pl.BlockSpec((WINDOW, SWA_KVW), lambda t: (blk(t), col_v)),
    ]


def _swa_fwd(qkv, sinks):
    s_len = qkv.shape[0]
    nb = s_len // WINDOW

    def body(q_ref, kp_ref, kc_ref, vp_ref, vc_ref, sinks_ref, o_ref):
        i = pl.program_id(0)
        kk = jnp.concatenate([kp_ref[...], kc_ref[...]], axis=0)
        vv = jnp.concatenate([vp_ref[...], vc_ref[...]], axis=0)
        for g in range(SWA_HKV):
            _, _, p, _ = _swa_group(i, q_ref, kk, sinks_ref, g)
            o = jnp.dot(p.astype(BF16), vv[:, g * SWA_DH : (g + 1) * SWA_DH], preferred_element_type=F32)
            for t in range(SWA_G):
                h = g * SWA_G + t
                o_ref[:, h * SWA_DH : (h + 1) * SWA_DH] = o[t * WINDOW : (t + 1) * WINDOW, :]

    return _pcall(
        body,
        name="swa_fwd",
        grid=(nb,),
        out_shape=jax.ShapeDtypeStruct((s_len, SWA_W), F32),
        in_specs=_swa_specs(0, 4, 5, False, nb) + [pl.BlockSpec(memory_space=pltpu.SMEM)],
        out_specs=pl.BlockSpec((WINDOW, SWA_W), lambda t: (t, 0)),
        compiler_params=_cp("parallel"),
    )(qkv, qkv, qkv, qkv, qkv, sinks)


def _swa_bwd(qkv, sinks, do):
    s_len = qkv.shape[0]
    nb = s_len // WINDOW

    def body(q_ref, kp_ref, kc_ref, vp_ref, vc_ref, sinks_ref, do_ref, dq_ref, dk_ref, dv_ref, dsink_ref, ck_s, cv_s, dkk_s, dvv_s):
        t = pl.program_id(0)
        i = nb - 1 - t

        @pl.when(t == 0)
        def _():
            ck_s[...] = jnp.zeros(ck_s.shape, F32)
            cv_s[...] = jnp.zeros(cv_s.shape, F32)
            dsink_ref[...] = jnp.zeros(dsink_ref.shape, F32)

        kk = jnp.concatenate([kp_ref[...], kc_ref[...]], axis=0)
        vv = jnp.concatenate([vp_ref[...], vc_ref[...]], axis=0)
        lane = lax.broadcasted_iota(jnp.int32, (1, 128), 1)
        dsink = jnp.zeros((1, 128), F32)
        for g in range(SWA_HKV):
            cols = slice(g * SWA_DH, (g + 1) * SWA_DH)
            q, k, p, p_sink = _swa_group(i, q_ref, kk, sinks_ref, g)
            dob = jnp.concatenate([do_ref[:, (g * SWA_G + t) * SWA_DH : (g * SWA_G + t + 1) * SWA_DH] for t in range(SWA_G)], axis=0)
            dp = lax.dot_general(dob, vv[:, cols], _NT, preferred_element_type=F32)
            delta = jnp.sum(p * dp, axis=1, keepdims=True)
            dsb = (p * (dp - delta)).astype(BF16)
            dq = (jnp.dot(dsb, k, preferred_element_type=F32) * SWA_SCALE).astype(BF16)
            ps_d = p_sink * delta
            for t in range(SWA_G):
                h = g * SWA_G + t
                dq_ref[:, h * SWA_DH : (h + 1) * SWA_DH] = dq[t * WINDOW : (t + 1) * WINDOW, :]
                dsink = dsink + jnp.where(lane == h, -jnp.sum(ps_d[t * WINDOW : (t + 1) * WINDOW, :], axis=0, keepdims=True), 0.0)
            dkk_s[:, cols] = lax.dot_general(dsb, q, _TN, preferred_element_type=F32) * SWA_SCALE
            dvv_s[:, cols] = lax.dot_general(p.astype(BF16), dob, _TN, preferred_element_type=F32)
        dk_ref[...] = (dkk_s[WINDOW:, :] + ck_s[...]).astype(BF16)
        dv_ref[...] = (dvv_s[WINDOW:, :] + cv_s[...]).astype(BF16)
        ck_s[...] = dkk_s[:WINDOW, :]
        cv_s[...] = dvv_s[:WINDOW, :]
        dsink_ref[...] += dsink

    row = lambda t: (nb - 1 - t, 0)
    return _pcall(
        body,
        name="swa_bwd",
        grid=(nb,),
        out_shape=[
            jax.ShapeDtypeStruct((s_len, SWA_W), BF16),
            jax.ShapeDtypeStruct((s_len, SWA_KVW), BF16),
            jax.ShapeDtypeStruct((s_len, SWA_KVW), BF16),
            jax.ShapeDtypeStruct((1, 128), F32),
        ],
        in_specs=_swa_specs(0, 4, 5, True, nb)
        + [pl.BlockSpec(memory_space=pltpu.SMEM), pl.BlockSpec((WINDOW, SWA_W), row)],
        out_specs=[
            pl.BlockSpec((WINDOW, SWA_W), row),
            pl.BlockSpec((WINDOW, SWA_KVW), row),
            pl.BlockSpec((WINDOW, SWA_KVW), row),
            pl.BlockSpec((1, 128), lambda t: (0, 0)),
        ],
        scratch_shapes=[
            pltpu.VMEM((WINDOW, SWA_KVW), F32),
            pltpu.VMEM((WINDOW, SWA_KVW), F32),
            pltpu.VMEM((2 * WINDOW, SWA_KVW), F32),
            pltpu.VMEM((2 * WINDOW, SWA_KVW), F32),
        ],
        compiler_params=_cp("arbitrary"),
    )(qkv, qkv, qkv, qkv, qkv, sinks, do)


def _branch_fwd(o, gates, g_blk, w_b, name):
    s_len, wd = o.shape
    d = w_b.shape[1]
    tm = min(512, s_len)

    def body(o_ref, g_ref, w_ref, y_ref, a_ref):
        g = g_ref[...].astype(F32)
        a = (o_ref[...] * (g * _sigmoid(g))).astype(BF16)
        a_ref[...] = a
        y_ref[...] = jnp.dot(a, w_ref[...], preferred_element_type=F32).astype(BF16)

    return _pcall(
        body,
        name=name,
        grid=(s_len // tm,),
        out_shape=[jax.ShapeDtypeStruct((s_len, d), BF16), jax.ShapeDtypeStruct((s_len, wd), BF16)],
        in_specs=[
            pl.BlockSpec((tm, wd), lambda i: (i, 0)),
            pl.BlockSpec((tm, wd), lambda i: (i, g_blk)),
            pl.BlockSpec((wd, d), lambda i: (0, 0)),
        ],
        out_specs=[pl.BlockSpec((tm, d), lambda i: (i, 0)), pl.BlockSpec((tm, wd), lambda i: (i, 0))],
        compiler_params=_cp("parallel"),
    )(o, gates, w_b)


def _out_stage(gates, mf_blk, y_fox, y_swa, w_out, x, ada, ln_g, ln_b, target):
    s_len, d = x.shape
    tm = min(256, s_len)
    n_steps = s_len // tm

    def body(mf_ref, ms_ref, yf_ref, ys_ref, w_ref, x_ref, gate_ref, lg_ref, lb_ref, t_ref, mg_ref, dza_ref, dsub_ref, red_ref):
        i = pl.program_id(0)
        merged = _sigmoid(mf_ref[...].astype(F32)) * yf_ref[...].astype(F32) + _sigmoid(ms_ref[...].astype(F32)) * ys_ref[...].astype(F32)
        mb = merged.astype(BF16)
        mg_ref[...] = mb
        sub = jnp.dot(mb, w_ref[...], preferred_element_type=F32)
        gate = gate_ref[...]
        z = ALPHA * x_ref[...] + gate * sub
        mu = jnp.mean(z, axis=-1, keepdims=True)
        zc = z - mu
        var = jnp.mean(zc * zc, axis=-1, keepdims=True)
        rstd = lax.rsqrt(var + LN_EPS)
        zhat = zc * rstd
        err = zhat * lg_ref[...] + lb_ref[...] - t_ref[...]
        dout = err * (1.0 / d)
        dzhat = dout * lg_ref[...]
        dz = rstd * (dzhat - jnp.mean(dzhat, axis=-1, keepdims=True) - zhat * jnp.mean(dzhat * zhat, axis=-1, keepdims=True))
        dza_ref[...] = ALPHA * dz
        dsub_ref[...] = (gate * dz).astype(BF16)
        part = jnp.concatenate(
            [
                jnp.sum(dz * sub, axis=0, keepdims=True),
                jnp.sum(dout * zhat, axis=0, keepdims=True),
                jnp.sum(dout, axis=0, keepdims=True),
                jnp.sum(err * err, axis=0, keepdims=True),
                jnp.zeros((4, d), F32),
            ],
            axis=0,
        )

        @pl.when(i == 0)
        def _():
            red_ref[...] = part

        @pl.when(i > 0)
        def _():
            red_ref[...] += part

        @pl.when(i == n_steps - 1)
        def _():
            red_ref[4:5, :] = jnp.broadcast_to(jnp.sum(red_ref[3:4, :], axis=1, keepdims=True), (1, d))

    row = pl.BlockSpec((tm, d), lambda i: (i, 0))
    vec = pl.BlockSpec((1, d), lambda i: (0, 0))
    return _pcall(
        body,
        name="out_stage",
        grid=(n_steps,),
        out_shape=[
            jax.ShapeDtypeStruct((s_len, d), BF16),
            jax.ShapeDtypeStruct((s_len, d), F32),
            jax.ShapeDtypeStruct((s_len, d), BF16),
            jax.ShapeDtypeStruct((8, d), F32),
        ],
        in_specs=[
            pl.BlockSpec((tm, d), lambda i: (i, mf_blk)),
            pl.BlockSpec((tm, d), lambda i: (i, mf_blk + 1)),
            row,
            row,
            pl.BlockSpec((d, d), lambda i: (0, 0), pipeline_mode=pl.Buffered(1)),
            row,
            pl.BlockSpec((1, d), lambda i: (0, 2)),
            vec,
            vec,
            row,
        ],
        out_specs=[row, row, row, pl.BlockSpec((8, d), lambda i: (0, 0))],
        compiler_params=_cp("arbitrary"),
    )(gates, gates, y_fox, y_swa, w_out, x, ada, ln_g, ln_b, target)


def _merge_bwd(dsub, w_out, gates, mf_blk, y_fox, y_swa):
    s_len, d = dsub.shape
    tm = min(256, s_len)

    def body(ds_ref, w_ref, mf_ref, ms_ref, yf_ref, ys_ref, dmf_ref, dms_ref, dyf_ref, dys_ref):
        dm = lax.dot_general(ds_ref[...], w_ref[...], _NT, preferred_element_type=F32)
        sf, ss = _sigmoid(mf_ref[...].astype(F32)), _sigmoid(ms_ref[...].astype(F32))
        dmf_ref[...] = (dm * yf_ref[...].astype(F32) * (sf * (1.0 - sf))).astype(BF16)
        dms_ref[...] = (dm * ys_ref[...].astype(F32) * (ss * (1.0 - ss))).astype(BF16)
        dyf_ref[...] = (dm * sf).astype(BF16)
        dys_ref[...] = (dm * ss).astype(BF16)

    row = pl.BlockSpec((tm, d), lambda i: (i, 0))
    return _pcall(
        body,
        name="merge_bwd",
        grid=(s_len // tm,),
        out_shape=[jax.ShapeDtypeStruct((s_len, d), BF16)] * 4,
        in_specs=[
            row,
            pl.BlockSpec((d, d), lambda i: (0, 0), pipeline_mode=pl.Buffered(1)),
            pl.BlockSpec((tm, d), lambda i: (i, mf_blk)),
            pl.BlockSpec((tm, d), lambda i: (i, mf_blk + 1)),
            row,
            row,
        ],
        out_specs=[row] * 4,
        compiler_params=_cp("parallel"),
    )(dsub, w_out, gates, gates, y_fox, y_swa)


def _branch_bwd(dy, w_b, o, gates, g_blk, name, n_heads):
    s_len, d = dy.shape
    wd = w_b.shape[0]
    tm = min(512, s_len)

    def body(dy_ref, w_ref, o_ref, g_ref, do_ref, dg_ref, *rest):
        da = lax.dot_general(dy_ref[...], w_ref[...], _NT, preferred_element_type=F32)
        g = g_ref[...].astype(F32)
        sg = _sigmoid(g)
        do = da * (g * sg)
        do_ref[...] = do.astype(BF16)
        o = o_ref[...]
        dg_ref[...] = (da * o * (sg * (1.0 + g * (1.0 - sg)))).astype(BF16)
        if n_heads:
            prod = do * o
            lane = lax.broadcasted_iota(jnp.int32, (1, 128), 1)
            delta = jnp.zeros((tm, 128), F32)
            for h in range(n_heads):
                dh = jnp.sum(prod[:, h * 128 : (h + 1) * 128], axis=1, keepdims=True)
                delta = delta + jnp.where(lane == h, dh, 0.0)
            rest[0][...] = delta

    out_shape = [jax.ShapeDtypeStruct((s_len, wd), BF16), jax.ShapeDtypeStruct((s_len, wd), BF16)]
    out_specs = [pl.BlockSpec((tm, wd), lambda i: (i, 0))] * 2
    if n_heads:
        out_shape.append(jax.ShapeDtypeStruct((s_len, 128), F32))
        out_specs.append(pl.BlockSpec((tm, 128), lambda i: (i, 0)))
    return _pcall(
        body,
        name=name,
        grid=(s_len // tm,),
        out_shape=out_shape,
        in_specs=[
            pl.BlockSpec((tm, d), lambda i: (i, 0)),
            pl.BlockSpec((wd, d), lambda i: (0, 0)),
            pl.BlockSpec((tm, wd), lambda i: (i, 0)),
            pl.BlockSpec((tm, wd), lambda i: (i, g_blk)),
        ],
        out_specs=out_specs,
        compiler_params=_cp("parallel"),
    )(dy, w_b, o, gates)


def _in_bwd(dproj, w_in_t, x, ada, dza, ride):
    s_len, d = x.shape
    k_tot = dproj.shape[1]
    tm, tk = min(512, s_len), 1024
    ni, nk = s_len // tm, k_tot // tk
    n = len(ride)

    def body(dp_ref, w_ref, x_ref, sc_ref, dza_ref, *rest):
        ins, (gx_ref, red_ref), outs = rest[:n], rest[n : n + 2], rest[n + 2 : 2 * n + 2]
        sems, acc_s = rest[2 * n + 2 : 2 * n + 5], rest[2 * n + 5]
        i, kk = pl.program_id(0), pl.program_id(1)

        @pl.when((i == 0) & (kk == 0))
        def _():
            _rider_start("exchange", ins, outs, *sems)

        @pl.when((i == ni - 1) & (kk == nk - 1))
        def _():
            _rider_wait("exchange", ins, outs, *sems)

        part = jnp.dot(dp_ref[...], w_ref[...], preferred_element_type=F32)

        @pl.when(kk == 0)
        def _():
            acc_s[...] = part

        @pl.when(kk > 0)
        def _():
            acc_s[...] += part

        @pl.when(kk == nk - 1)
        def _():
            dh = acc_s[...]
            xv = x_ref[...]
            mu = jnp.mean(xv, axis=-1, keepdims=True)
            xc = xv - mu
            var = jnp.mean(xc * xc, axis=-1, keepdims=True)
            rstd = lax.rsqrt(var + LN_EPS)
            xhat = xc * rstd
            dxhat = dh * (1.0 + sc_ref[...])
            dx = rstd * (dxhat - jnp.mean(dxhat, axis=-1, keepdims=True) - xhat * jnp.mean(dxhat * xhat, axis=-1, keepdims=True))
            gx_ref[...] = dza_ref[...] + dx
            part_r = jnp.concatenate(
                [jnp.sum(dh, axis=0, keepdims=True), jnp.sum(dh * xhat, axis=0, keepdims=True), jnp.zeros((6, d), F32)], axis=0
            )

            @pl.when(i == 0)
            def _():
                red_ref[...] = part_r

            @pl.when(i > 0)
            def _():
                red_ref[...] += part_r

    row = pl.BlockSpec((tm, d), lambda i, kk: (i, 0))
    hbm = pl.BlockSpec(memory_space=pltpu.HBM)
    return _pcall(
        body,
        name="in_bwd",
        grid=(ni, nk),
        out_shape=[jax.ShapeDtypeStruct((s_len, d), F32), jax.ShapeDtypeStruct((8, d), F32)]
        + [jax.ShapeDtypeStruct(r.shape, r.dtype) for r in ride],
        in_specs=[
            pl.BlockSpec((tm, tk), lambda i, kk: (i, kk)),
            pl.BlockSpec((tk, d), lambda i, kk: (kk, 0)),
            row,
            pl.BlockSpec((1, d), lambda i, kk: (0, 1)),
            row,
        ]
        + [hbm] * n,
        out_specs=[row, pl.BlockSpec((8, d), lambda i, kk: (0, 0))] + [hbm] * n,
        scratch_shapes=_rider_scratch(n) + [pltpu.VMEM((tm, d), F32)],
        compiler_params=_cp("arbitrary", "arbitrary"),
    )(dproj, w_in_t, x, ada, dza, *ride)


def _pad_lanes(v, n):
    return jnp.pad(v, ((0, 0), (0, n - v.shape[1])))


def kernel(x, c, w_ada, b_ada, w_in, b_f, attn_sinks, w_br_fox, w_br_swa, w_out, ln_g, ln_b, loss_target, m_w_ada, m_b_ada, m_w_in, m_b_f, m_attn_sinks, m_w_br_fox, m_w_br_swa, m_w_out, m_ln_g, m_ln_b, v_w_ada, v_b_ada, v_w_in, v_b_f, v_attn_sinks, v_w_br_fox, v_w_br_swa, v_w_out, v_ln_g, v_ln_b):
    x2, tgt = x[0], loss_target[0]
    s_len, d = x2.shape
    me = 4 * lax.axis_index("x") + 2 * lax.axis_index("y") + lax.axis_index("c")
    off_ms = OFF_MF + d
    in_pad = off_ms + d
    c_ada = w_ada.shape[2]
    c_in = w_in.shape[2]
    c_br = w_br_fox.shape[2]

    w_in_full = _all_gather(w_in[0].T.astype(BF16), "ag_w_in", pltpu.HBM).reshape(N_DEV * c_in, d)
    w_in_pad = jnp.concatenate(
        [w_in_full[:REAL_FLOG_END], jnp.zeros((FLOG_PAD - N_FLOG, d), BF16), w_in_full[REAL_FLOG_END:]], axis=0
    )

    c_all = _gather_rows(c, "ag_c")
    b_cols = lax.dynamic_slice(b_ada, (0, me * c_ada), (1, c_ada))
    ada_cols = _ada_fwd(c_all, w_ada[0], b_cols)
    ada_g = _all_gather(ada_cols, "ag_ada", pltpu.VMEM)
    ada = lax.dynamic_index_in_dim(ada_g, me, axis=1, keepdims=False).reshape(1, N_DEV * c_ada)

    h = _ln_mod(x2, ada)
    qkv_fox = _mm_cols(h, w_in_pad, OFF_FQ, 3 * FOX_W, BF16, "proj_fox")
    flog = _mm_cols(h, w_in_pad, OFF_FLOG, FLOG_PAD, F32, "proj_flog")
    qkv_swa = _mm_cols(h, w_in_pad, OFF_SQ, SWA_W + 2 * SWA_KVW, BF16, "proj_swa")
    gates, w_bf, w_bs, w_o = _mm_cols(
        h, w_in_pad, OFF_GF, in_pad - OFF_GF, BF16, "proj_gates",
        ride=(w_br_fox[0].astype(BF16), w_br_swa[0].astype(BF16), w_out[0].astype(BF16)),
    )
    w_bf = w_bf.reshape(N_DEV, FOX_W, c_br).transpose(1, 0, 2).reshape(FOX_W, d)
    w_bs = w_bs.reshape(N_DEV, SWA_W, c_br).transpose(1, 0, 2).reshape(SWA_W, d)
    w_o = w_o.reshape(d, d)
    mf_blk = (OFF_MF - OFF_GF) // d

    flog_t = flog[:, :N_FLOG].T
    bf_col = b_f.reshape(FOX_H, 1)
    cum = _fox_cum(flog_t, bf_col)
    cum_row = cum.reshape(FOX_H, 1, s_len)
    o_fox, lse = _fox_fwd(qkv_fox, cum_row)
    sinks = attn_sinks.reshape(SWA_HQ)
    o_swa = _swa_fwd(qkv_swa, sinks)

    y_fox, a_fox = _branch_fwd(o_fox, gates, 0, w_bf, "branch_fox")
    y_swa, a_swa = _branch_fwd(o_swa, gates, 1, w_bs, "branch_swa")
    merged, dza, dsub, red = _out_stage(gates, mf_blk, y_fox, y_swa, w_o, x2, ada, ln_g, ln_b, tgt)
    loss = lax.psum(0.5 * red[4, 0] / d, ("x", "y", "c"))

    dmf, dms, dy_fox, dy_swa = _merge_bwd(dsub, w_o, gates, mf_blk, y_fox, y_swa)
    do_fox, dg_fox, delta = _branch_bwd(dy_fox, w_bf, o_fox, gates, 0, "branch_fox_bwd", FOX_H)
    do_swa, dg_swa = _branch_bwd(dy_swa, w_bs, o_swa, gates, 1, "branch_swa_bwd", 0)
    delta_row = delta[:, :FOX_H].T.reshape(FOX_H, 1, s_len)
    dq_f, dk_f, dv_f, dcol, drow = _fox_bwd(
        qkv_fox, cum.reshape(FOX_H, s_len, 1), lse.reshape(FOX_H, 1, s_len), delta_row, do_fox
    )
    dflog_t, dbf = _fox_gate_bwd(drow.reshape(FOX_H, s_len), dcol.reshape(FOX_H, s_len), flog_t, bf_col)
    dq_s, dk_s, dv_s, dsink = _swa_bwd(qkv_swa, sinks, do_swa)
    dflog = _pad_lanes(dflog_t.T, FLOG_PAD).astype(BF16)
    dproj = jnp.concatenate([dq_f.astype(BF16), dk_f, dv_f, dflog, dq_s, dk_s, dv_s, dg_fox, dg_swa, dmf, dms], axis=1)
    g_w_bf = _mm_tn(a_fox, dy_fox, "grad_w_br_fox")
    g_w_bs = _mm_tn(a_swa, dy_swa, "grad_w_br_swa")
    g_w_o = _mm_tn(merged, dsub, "grad_w_out")
    g_w_in, r_bf, r_bs, r_o = _mm_tn(
        dproj, h, "grad_w_in",
        ride=(
            g_w_bf.reshape(FOX_W, N_DEV, c_br).transpose(1, 0, 2),
            g_w_bs.reshape(SWA_W, N_DEV, c_br).transpose(1, 0, 2),
            g_w_o.reshape(N_DEV, d // N_DEV, d),
        ),
    )
    g_w_in = jnp.concatenate([g_w_in[:REAL_FLOG_END], g_w_in[OFF_SQ:]], axis=0)

    grad_x, red2, r_in = _in_bwd(dproj, w_in_pad, x2, ada, dza, ride=(g_w_in.reshape(N_DEV, c_in, d),))
    out_w_in = _sum_adam_t(r_in, w_in[0].T, m_w_in[0].T, v_w_in[0].T, "adam_w_in")
    out_w_in = [o.T for o in out_w_in]
    out_w_bf = _sum_adam(r_bf, w_br_fox[0], m_w_br_fox[0], v_w_br_fox[0], "adam_w_br_fox")
    out_w_bs = _sum_adam(r_bs, w_br_swa[0], m_w_br_swa[0], v_w_br_swa[0], "adam_w_br_swa")
    out_w_o = _sum_adam(r_o, w_out[0], m_w_out[0], v_w_out[0], "adam_w_out")

    packed = jnp.concatenate([red2[0:1], red2[1:2], red[0:1], _pad_lanes(dbf[:, 0].reshape(1, FOX_H), 128), dsink, red[1:2], red[2:3]], axis=1)
    gathered = _gather_rows(packed, "ag_small")
    pack = lambda a, b, cc, dd, e: jnp.concatenate([a, _pad_lanes(b, 128), _pad_lanes(cc, 128), dd, e], axis=1)
    small = _small_adam(
        gathered,
        pack(b_ada, b_f, attn_sinks, ln_g, ln_b),
        pack(m_b_ada, m_b_f, m_attn_sinks, m_ln_g, m_ln_b),
        pack(v_b_ada, v_b_f, v_attn_sinks, v_ln_g, v_ln_b),
    )
    dada_cols = lax.dynamic_slice(gathered, (0, me * c_ada), (N_DEV, c_ada))
    out_w_ada = _wada_adam(c_all.T, dada_cols, w_ada[0], m_w_ada[0], v_w_ada[0])

    o1, o2, o3 = 3 * d, 3 * d + 128, 3 * d + 256

    def unpack(p):
        return p[:, :o1], p[:, o1 : o1 + FOX_H], p[:, o2 : o2 + SWA_HQ], p[:, o3 : o3 + d], p[:, o3 + d : o3 + 2 * d]

    kinds = []
    for k in range(4):
        b_ada_k, b_f_k, sinks_k, ln_g_k, ln_b_k = unpack(small[k])
        kinds.append(
            [out_w_ada[k][None], b_ada_k, out_w_in[k][None], b_f_k, sinks_k, out_w_bf[k][None], out_w_bs[k][None], out_w_o[k][None], ln_g_k, ln_b_k]
        )
    return (loss, grad_x[None], *kinds[0], *kinds[1], *kinds[2], *kinds[3])
```

```python
import numpy as np
import jax
import jax.numpy as jnp
from jax import lax
from jax.experimental import pallas as pl
from jax.experimental.pallas import tpu as pltpu

F32 = jnp.float32
BF16 = jnp.bfloat16
N_DEV = 8
MESH = pl.DeviceIdType.MESH

FOX_H, FOX_DH, FOX_W = 8, 128, 1024
SWA_HQ, SWA_HKV, SWA_DH, SWA_G = 16, 4, 64, 4
SWA_W, SWA_KVW, WINDOW = 1024, 256, 128
LN_EPS = 1e-5
NEG = -1e30
DEPTH = 1
ALPHA = (2.0 * DEPTH) ** 0.25
FOX_SCALE = FOX_DH ** -0.5
SWA_SCALE = SWA_DH ** -0.5
SLOPES = [2.0 ** (-8.0 * (h + 1.0) / SWA_HQ) for h in range(SWA_HQ)]

ADAM_LR, ADAM_B1, ADAM_B2, ADAM_EPS, ADAM_WD, ADAM_STEP = 0.001, 0.9, 0.999, 1e-08, 0.01, 10

N_FLOG = 8
FLOG_PAD = 512
OFF_FQ, OFF_FK, OFF_FV, OFF_FLOG = 0, 1024, 2048, 3072
OFF_SQ = OFF_FLOG + FLOG_PAD
OFF_SK = OFF_SQ + SWA_W
OFF_SV = OFF_SK + SWA_KVW
OFF_GF = OFF_SV + SWA_KVW
OFF_GS = OFF_GF + FOX_W
OFF_MF = OFF_GS + SWA_W
REAL_FLOG_END = OFF_FLOG + N_FLOG

ATT_BLK = 512
VMEM_LIMIT = 52 * 1024 * 1024


def _pcall(body, **kw):
    return pl.pallas_call(body, **kw)


def _cp(*sem):
    return pltpu.CompilerParams(dimension_semantics=sem, vmem_limit_bytes=VMEM_LIMIT)


def _sigmoid(x):
    return 1.0 / (1.0 + jnp.exp(-x))


def _all_gather(x, name, space):
    m_per, n = x.shape

    def body(x_ref, out_ref, send_sems, recv_sems, local_sem):
        mx, my, mc = lax.axis_index("x"), lax.axis_index("y"), lax.axis_index("c")
        me, sibling = (mx, my, mc), (mx, my, 1 - mc)
        chips = [(1 - mx, my), (mx, 1 - my), (1 - mx, 1 - my)]

        def rows(px, py, pc):
            return out_ref.at[4 * px + 2 * py + pc]

        def copy(k, block, to, src=None):
            return pltpu.make_async_remote_copy(
                src_ref=rows(*block) if src is None else src,
                dst_ref=rows(*block),
                send_sem=send_sems.at[k],
                recv_sem=recv_sems.at[k],
                device_id=to,
                device_id_type=MESH,
            )

        mine = pltpu.make_async_copy(x_ref, rows(*me), local_sem)
        mine.start()
        first = [copy(0, me, sibling, src=x_ref)]
        first += [copy(1 + j, me, (*chip, mc), src=x_ref) for j, chip in enumerate(chips)]
        for cp in first:
            cp.start()
        passed = [copy(4 + j, (*chip, mc), sibling) for j, chip in enumerate(chips)]
        for j, chip in enumerate(chips):
            copy(1 + j, (*chip, mc), me).wait_recv()
            passed[j].start()
        copy(0, sibling, me).wait_recv()
        for j, chip in enumerate(chips):
            copy(4 + j, (*chip, 1 - mc), me).wait_recv()
        for cp in first + passed:
            cp.wait_send()
        mine.wait()

    return _pcall(
        body,
        name=name,
        out_shape=jax.ShapeDtypeStruct((N_DEV, m_per, n), x.dtype),
        in_specs=[pl.BlockSpec(memory_space=space)],
        out_specs=pl.BlockSpec(memory_space=space),
        scratch_shapes=[pltpu.SemaphoreType.DMA((7,)), pltpu.SemaphoreType.DMA((7,)), pltpu.SemaphoreType.DMA],
    )(x)


def _peer(d, mx, my, mc):
    return (1 - mx if (d >> 2) & 1 else mx, 1 - my if (d >> 1) & 1 else my, 1 - mc if d & 1 else mc)


def _rider_copies(kind, ins, outs, send_sems, recv_sems, local_sems):
    mx, my, mc = lax.axis_index("x"), lax.axis_index("y"), lax.axis_index("c")
    me = 4 * mx + 2 * my + mc
    remote, local = [], []
    for a in range(len(ins)):
        if kind == "gather":
            m_per = ins[a].shape[0]
            mine = outs[a].at[pl.ds(me * m_per, m_per), :]
            local.append(pltpu.make_async_copy(ins[a], mine, local_sems.at[a]))
        else:
            local.append(pltpu.make_async_copy(ins[a].at[me], outs[a].at[0], local_sems.at[a]))
        for d in range(1, N_DEV):
            px, py, pc = _peer(d, mx, my, mc)
            if kind == "gather":
                src, dst = ins[a], mine
            else:
                src, dst = ins[a].at[4 * px + 2 * py + pc], outs[a].at[d]
            remote.append(
                pltpu.make_async_remote_copy(
                    src_ref=src,
                    dst_ref=dst,
                    send_sem=send_sems.at[a * 7 + d - 1],
                    recv_sem=recv_sems.at[a * 7 + d - 1],
                    device_id=(px, py, pc),
                    device_id_type=MESH,
                )
            )
    return remote, local


def _rider_start(*args):
    remote, local = _rider_copies(*args)
    for cp in local + remote:
        cp.start()


def _rider_wait(*args):
    remote, local = _rider_copies(*args)
    for cp in remote:
        cp.wait_recv()
    for cp in remote:
        cp.wait_send()
    for cp in local:
        cp.wait()


def _rider_scratch(n):
    return [pltpu.SemaphoreType.DMA((7 * n,)), pltpu.SemaphoreType.DMA((7 * n,)), pltpu.SemaphoreType.DMA((n,))]


def _gather_rows(v, name):
    n = v.shape[1]
    return _all_gather(jnp.broadcast_to(v, (8, n)), name, pltpu.VMEM)[:, 0, :]


def _adamw(w, g, m, v):
    m = ADAM_B1 * m + (1.0 - ADAM_B1) * g
    v = ADAM_B2 * v + (1.0 - ADAM_B2) * (g * g)
    m_hat = m / (1.0 - ADAM_B1**ADAM_STEP)
    v_hat = v / (1.0 - ADAM_B2**ADAM_STEP)
    delta = -ADAM_LR * (m_hat / (jnp.sqrt(v_hat) + ADAM_EPS) + ADAM_WD * w)
    return delta, m, v


def _sum_adam(recv, w, m, v, name):
    _, r_tot, c = recv.shape
    c_pad = -(-c // 128) * 128
    tr = r_tot
    while 8 * tr * c_pad * 4 > 6 * 1024 * 1024 and tr % 32 == 0:
        tr //= 2

    def body(r_ref, w_ref, m_ref, v_ref, g_ref, d_ref, nm_ref, nv_ref):
        g = r_ref[0].astype(F32)
        for k in range(1, N_DEV):
            g = g + r_ref[k].astype(F32)
        d, nm, nv = _adamw(w_ref[...], g, m_ref[...], v_ref[...])
        g_ref[...] = g
        d_ref[...] = d
        nm_ref[...] = nm
        nv_ref[...] = nv

    blk = pl.BlockSpec((tr, c), lambda i: (i, 0))
    return _pcall(
        body,
        name=name,
        grid=(r_tot // tr,),
        out_shape=[jax.ShapeDtypeStruct((r_tot, c), F32)] * 4,
        in_specs=[pl.BlockSpec((N_DEV, tr, c), lambda i: (0, i, 0)), blk, blk, blk],
        out_specs=[blk] * 4,
        compiler_params=_cp("parallel"),
    )(recv, w, m, v)


def _sum_adam_t(recv, w, m, v, name):
    _, c, r_tot = recv.shape
    tr = min(256, r_tot)

    def body(r_ref, w_ref, m_ref, v_ref, g_ref, d_ref, nm_ref, nv_ref):
        g = r_ref[0].astype(F32)
        for k in range(1, N_DEV):
            g = g + r_ref[k].astype(F32)
        d, nm, nv = _adamw(w_ref[...], g, m_ref[...], v_ref[...])
        g_ref[...] = g
        d_ref[...] = d
        nm_ref[...] = nm
        nv_ref[...] = nv

    blk = pl.BlockSpec((c, tr), lambda i: (0, i))
    return _pcall(
        body,
        name=name,
        grid=(r_tot // tr,),
        out_shape=[jax.ShapeDtypeStruct((c, r_tot), F32)] * 4,
        in_specs=[pl.BlockSpec((N_DEV, c, tr), lambda i: (0, 0, i)), blk, blk, blk],
        out_specs=[blk] * 4,
        compiler_params=_cp("parallel"),
    )(recv, w, m, v)


def _wada_adam(c_t, dada_cols, w, m, v):
    d_model, c = w.shape
    tr = min(256, d_model)

    def body(ct_ref, da_ref, w_ref, m_ref, v_ref, g_ref, d_ref, nm_ref, nv_ref):
        g = jnp.dot(ct_ref[...].astype(BF16), da_ref[...].astype(BF16), preferred_element_type=F32)
        d, nm, nv = _adamw(w_ref[...], g, m_ref[...], v_ref[...])
        g_ref[...] = g
        d_ref[...] = d
        nm_ref[...] = nm
        nv_ref[...] = nv

    blk = pl.BlockSpec((tr, c), lambda i: (i, 0))
    return _pcall(
        body,
        name="wada_adam",
        grid=(d_model // tr,),
        out_shape=[jax.ShapeDtypeStruct((d_model, c), F32)] * 4,
        in_specs=[pl.BlockSpec((tr, N_DEV), lambda i: (i, 0)), pl.BlockSpec((N_DEV, c), lambda i: (0, 0)), blk, blk, blk],
        out_specs=[blk] * 4,
        compiler_params=_cp("parallel"),
    )(c_t, dada_cols, w, m, v)


def _small_adam(gathered, w, m, v):
    p = w.shape[1]

    def body(a_ref, w_ref, m_ref, v_ref, g_ref, d_ref, nm_ref, nv_ref):
        g = a_ref[0:1, :]
        for k in range(1, N_DEV):
            g = g + a_ref[k : k + 1, :]
        d, nm, nv = _adamw(w_ref[...], g, m_ref[...], v_ref[...])
        g_ref[...] = g
        d_ref[...] = d
        nm_ref[...] = nm
        nv_ref[...] = nv

    return _pcall(
        body,
        name="small_adam",
        out_shape=[jax.ShapeDtypeStruct((1, p), F32)] * 4,
    )(gathered, w, m, v)


def _ada_fwd(c_all, w_ada, b_cols):
    c = w_ada.shape[1]

    def body(c_ref, w_ref, b_ref, o_ref):
        o_ref[...] = jnp.dot(c_ref[...].astype(BF16), w_ref[...].astype(BF16), preferred_element_type=F32) + b_ref[...]

    return _pcall(
        body,
        name="ada_fwd",
        out_shape=jax.ShapeDtypeStruct((N_DEV, c), F32),
        compiler_params=_cp(),
    )(c_all, w_ada, b_cols)


def _ln_mod(x, ada):
    s_len, d = x.shape
    tm = min(512, s_len)

    def body(x_ref, sh_ref, sc_ref, h_ref):
        xv = x_ref[...]
        mu = jnp.mean(xv, axis=-1, keepdims=True)
        xc = xv - mu
        var = jnp.mean(xc * xc, axis=-1, keepdims=True)
        xhat = xc * lax.rsqrt(var + LN_EPS)
        h_ref[...] = (xhat * (1.0 + sc_ref[...]) + sh_ref[...]).astype(BF16)

    return _pcall(
        body,
        name="ln_mod",
        grid=(s_len // tm,),
        out_shape=jax.ShapeDtypeStruct((s_len, d), BF16),
        in_specs=[
            pl.BlockSpec((tm, d), lambda i: (i, 0)),
            pl.BlockSpec((1, d), lambda i: (0, 0)),
            pl.BlockSpec((1, d), lambda i: (0, 1)),
        ],
        out_specs=pl.BlockSpec((tm, d), lambda i: (i, 0)),
        compiler_params=_cp("parallel"),
    )(x, ada, ada)


def _mm_cols(a, b, col_off, n_cols, out_dtype, name, ride=()):
    m, k = a.shape
    tm, tn = min(1024, m), 512
    off = col_off // tn
    ni, nj = m // tm, n_cols // tn
    n = len(ride)

    def body(a_ref, b_ref, *rest):
        ins, o_ref, outs, sems = rest[:n], rest[n], rest[n + 1 : 2 * n + 1], rest[2 * n + 1 :]
        i, j = pl.program_id(0), pl.program_id(1)
        if n:

            @pl.when((i == 0) & (j == 0))
            def _():
                _rider_start("gather", ins, outs, *sems)

        o_ref[...] = lax.dot_general(a_ref[...], b_ref[...], _NT, preferred_element_type=F32).astype(out_dtype)
        if n:

            @pl.when((i == ni - 1) & (j == nj - 1))
            def _():
                _rider_wait("gather", ins, outs, *sems)

    hbm = pl.BlockSpec(memory_space=pltpu.HBM)
    out = _pcall(
        body,
        name=name,
        grid=(ni, nj),
        out_shape=[jax.ShapeDtypeStruct((m, n_cols), out_dtype)]
        + [jax.ShapeDtypeStruct((N_DEV * r.shape[0], r.shape[1]), r.dtype) for r in ride],
        in_specs=[pl.BlockSpec((tm, k), lambda i, j: (i, 0)), pl.BlockSpec((tn, k), lambda i, j: (off + j, 0))] + [hbm] * n,
        out_specs=[pl.BlockSpec((tm, tn), lambda i, j: (i, j))] + [hbm] * n,
        scratch_shapes=_rider_scratch(n) if n else [],
        compiler_params=_cp("arbitrary", "arbitrary") if n else _cp("parallel", "parallel"),
    )(a, b, *ride)
    return out if n else out[0]


def _mm_tn(a, b, name, ride=()):
    s_len, m = a.shape
    n = b.shape[1]
    tm, tn, ts = min(1024, m), min(1024, n), min(1024, s_len)
    ni, nj, ns = m // tm, n // tn, s_len // ts
    nr = len(ride)

    def body(a_ref, b_ref, *rest):
        ins, o_ref, outs = rest[:nr], rest[nr], rest[nr + 1 : 2 * nr + 1]
        sems, acc_s = rest[2 * nr + 1 : -1], rest[-1]
        i, j, kk = pl.program_id(0), pl.program_id(1), pl.program_id(2)
        if nr:

            @pl.when((i == 0) & (j == 0) & (kk == 0))
            def _():
                _rider_start("exchange", ins, outs, *sems)

            @pl.when((i == ni - 1) & (j == nj - 1) & (kk == ns - 1))
            def _():
                _rider_wait("exchange", ins, outs, *sems)

        part = lax.dot_general(a_ref[...], b_ref[...], _TN, preferred_element_type=F32)

        @pl.when(kk == 0)
        def _():
            acc_s[...] = part

        @pl.when(kk > 0)
        def _():
            acc_s[...] += part

        @pl.when(kk == ns - 1)
        def _():
            o_ref[...] = acc_s[...].astype(BF16)

    hbm = pl.BlockSpec(memory_space=pltpu.HBM)
    out = _pcall(
        body,
        name=name,
        grid=(ni, nj, ns),
        out_shape=[jax.ShapeDtypeStruct((m, n), BF16)] + [jax.ShapeDtypeStruct(r.shape, r.dtype) for r in ride],
        in_specs=[pl.BlockSpec((ts, tm), lambda i, j, kk: (kk, i)), pl.BlockSpec((ts, tn), lambda i, j, kk: (kk, j))] + [hbm] * nr,
        out_specs=[pl.BlockSpec((tm, tn), lambda i, j, kk: (i, j))] + [hbm] * nr,
        scratch_shapes=(_rider_scratch(nr) if nr else []) + [pltpu.VMEM((tm, tn), F32)],
        compiler_params=_cp("arbitrary", "arbitrary", "arbitrary") if nr else _cp("parallel", "parallel", "arbitrary"),
    )(a, b, *ride)
    return out if nr else out[0]


def _split3(a):
    hi = a.astype(BF16)
    r1 = a - hi.astype(F32)
    mid = r1.astype(BF16)
    lo = (r1 - mid.astype(F32)).astype(BF16)
    return hi, mid, lo


def _dot_ones(a, tri):
    return sum(jnp.dot(t, tri, preferred_element_type=F32) for t in _split3(a))


def _log_sigmoid(x):
    return jnp.minimum(x, 0.0) - jnp.log1p(jnp.exp(-jnp.abs(x)))


def _fox_cum(flog_t, bf_col):
    s_len = flog_t.shape[1]

    def body(fl_ref, bf_ref, cum_ref):
        r = lax.broadcasted_iota(jnp.int32, (128, 128), 0)
        c = lax.broadcasted_iota(jnp.int32, (128, 128), 1)
        upper = (r <= c).astype(BF16)

        def step(t, carry):
            sl = pl.ds(pl.multiple_of(t * 128, 128), 128)
            lf = _log_sigmoid(fl_ref[:, sl] + bf_ref[...])
            cs = _dot_ones(lf, upper) + carry
            cum_ref[:, sl] = cs
            return cs[:, 127:128]

        lax.fori_loop(0, s_len // 128, step, jnp.zeros((FOX_H, 1), F32))

    return _pcall(body, name="fox_cum", out_shape=jax.ShapeDtypeStruct((FOX_H, s_len), F32))(flog_t, bf_col)


def _fox_gate_bwd(drow, dcol, flog_t, bf_col):
    s_len = flog_t.shape[1]
    n = s_len // 128

    def body(dr_ref, dc_ref, fl_ref, bf_ref, dfl_ref, dbf_ref):
        r = lax.broadcasted_iota(jnp.int32, (128, 128), 0)
        c = lax.broadcasted_iota(jnp.int32, (128, 128), 1)
        lower = (r >= c).astype(BF16)

        def step(t, carry):
            run, tot = carry
            sl = pl.ds(pl.multiple_of((n - 1 - t) * 128, 128), 128)
            rc = _dot_ones(dr_ref[:, sl] - dc_ref[:, sl], lower) + run
            dfl = rc * _sigmoid(-(fl_ref[:, sl] + bf_ref[...]))
            dfl_ref[:, sl] = dfl
            return rc[:, 0:1], tot + jnp.sum(dfl, axis=1, keepdims=True)

        zero = jnp.zeros((FOX_H, 1), F32)
        _, tot = lax.fori_loop(0, n, step, (zero, zero))
        dbf_ref[...] = jnp.broadcast_to(tot, (FOX_H, 128))

    return _pcall(
        body,
        name="fox_gate_bwd",
        out_shape=[jax.ShapeDtypeStruct((FOX_H, s_len), F32), jax.ShapeDtypeStruct((FOX_H, 128), F32)],
    )(drow, dcol, flog_t, bf_col)


def _diag_mask(blk, transposed=False):
    r = lax.broadcasted_iota(jnp.int32, (blk, blk), 0)
    c = lax.broadcasted_iota(jnp.int32, (blk, blk), 1)
    return c >= r if transposed else r >= c


_NT = (((1,), (1,)), ((), ()))
_TN = (((0,), (0,)), ((), ()))


def _fox_fwd(qkv, cum_row):
    s_len = qkv.shape[0]
    blk = min(ATT_BLK, s_len)
    nb = s_len // blk
    log2e = 1.4426950408889634

    def body(q_ref, k_ref, v_ref, c_ref, o_ref, lse_ref, mx_s, acc_s, u_s):
        i = pl.program_id(1)

        def key_cols(j, n):
            return pl.ds(pl.multiple_of(j * blk, blk), n * blk)

        def walk(tile):
            lax.fori_loop(0, i // 2, lambda t, c: (tile(2 * t, 2, False), c)[1], 0)

            @pl.when(i % 2 == 1)
            def _():
                tile(i - 1, 1, False)

            tile(i, 1, True)

        def lane_max(j, n, masked):
            cols = key_cols(j, n)
            u = lax.dot_general(q_ref[...], k_ref[cols, :], _NT, preferred_element_type=F32) - c_ref[:, cols] * (1.0 / FOX_SCALE)
            if masked:
                u = jnp.where(_diag_mask(blk), u, NEG)
            u_s[:, cols] = u
            part = u[:, 0:128]
            for t in range(1, n * blk // 128):
                part = jnp.maximum(part, u[:, t * 128 : (t + 1) * 128])
            mx_s[...] = jnp.maximum(mx_s[...], part)

        mx_s[...] = jnp.full(mx_s.shape, NEG, F32)
        walk(lane_max)
        m = jnp.max(mx_s[...], axis=1, keepdims=True)

        def weigh(j, n, masked):
            cols = key_cols(j, n)
            p = jnp.exp2((u_s[:, cols] - m) * (FOX_SCALE * log2e))
            ones_col = (lax.broadcasted_iota(jnp.int32, (n * blk, 128), 1) == 0).astype(BF16)
            v1 = jnp.concatenate([v_ref[cols, :], ones_col], axis=1)
            acc_s[...] += jnp.dot(p.astype(BF16), v1, preferred_element_type=F32)

        acc_s[...] = jnp.zeros(acc_s.shape, F32)
        walk(weigh)
        l = acc_s[:, FOX_DH : FOX_DH + 1]
        o_ref[...] = acc_s[:, :FOX_DH] / l
        lse_ref[...] = m * FOX_SCALE + jnp.log(l)

    return _pcall(
        body,
        name="fox_fwd",
        grid=(FOX_H, nb),
        out_shape=[jax.ShapeDtypeStruct((s_len, FOX_W), F32), jax.ShapeDtypeStruct((FOX_H, s_len, 1), F32)],
        in_specs=[
            pl.BlockSpec((blk, FOX_DH), lambda h, i: (i, h)),
            pl.BlockSpec((s_len, FOX_DH), lambda h, i: (0, FOX_H + h)),
            pl.BlockSpec((s_len, FOX_DH), lambda h, i: (0, 2 * FOX_H + h)),
            pl.BlockSpec((None, 1, s_len), lambda h, i: (h, 0, 0)),
        ],
        out_specs=[
            pl.BlockSpec((blk, FOX_DH), lambda h, i: (i, h)),
            pl.BlockSpec((None, blk, 1), lambda h, i: (h, i, 0)),
        ],
        scratch_shapes=[pltpu.VMEM((blk, 128), F32), pltpu.VMEM((blk, 2 * FOX_DH), F32), pltpu.VMEM((blk, s_len), F32)],
        compiler_params=_cp("parallel", "arbitrary"),
    )(qkv, qkv, qkv, cum_row)


def _fox_bwd(qkv, cum_col, lse_row, delta_row, do):
    s_len = qkv.shape[0]
    blk = min(ATT_BLK, s_len)
    nb = s_len // blk

    def body(q_ref, k_ref, v_ref, c_ref, lse_ref, dl_ref, do_ref, dq_ref, dk_ref, dv_ref, dc_ref, dr_ref, dk_s, dv_s, dc_s, cb_s):
        j = pl.program_id(1)

        @pl.when(j == 0)
        def _():
            dq_ref[...] = jnp.zeros(dq_ref.shape, F32)
            dr_ref[...] = jnp.zeros(dr_ref.shape, F32)

        dk_s[...] = jnp.zeros(dk_s.shape, F32)
        dv_s[...] = jnp.zeros(dv_s.shape, F32)
        dc_s[...] = jnp.zeros(dc_s.shape, F32)
        cb_s[...] = jnp.broadcast_to(c_ref[...], cb_s.shape)

        def tile(i, n, diag):
            rows = pl.ds(pl.multiple_of(i * blk, blk), n * blk)
            q, dob = q_ref[rows, :], do_ref[rows, :]
            k, v = k_ref[...], v_ref[...]
            s_t = lax.dot_general(k, q, _NT, preferred_element_type=F32) * FOX_SCALE - cb_s[:, : n * blk]
            p_t = jnp.exp(s_t - lse_ref[:, rows])
            if diag:
                p_t = jnp.where(_diag_mask(blk, transposed=True), p_t, 0.0)
            dp_t = lax.dot_general(v, dob, _NT, preferred_element_type=F32)
            ds_t = p_t * (dp_t - dl_ref[:, rows])
            dsb = ds_t.astype(BF16)
            dv_s[...] += jnp.dot(p_t.astype(BF16), dob, preferred_element_type=F32)
            dk_s[...] += jnp.dot(dsb, q, preferred_element_type=F32)
            dq_c = lax.dot_general(dsb, k, _TN, preferred_element_type=F32)
            part = ds_t[:, 0:128]
            for t in range(1, n * blk // 128):
                part = part + ds_t[:, t * 128 : (t + 1) * 128]
            dc_s[...] += part
            dr_ref[:, rows] += jnp.sum(ds_t, axis=0, keepdims=True)
            if diag:
                dq_ref[rows, :] = (dq_ref[rows, :] + dq_c) * FOX_SCALE
            else:
                dq_ref[rows, :] += dq_c

        tile(j, 1, True)
        odd = (nb - 1 - j) % 2

        @pl.when(odd == 1)
        def _():
            tile(j + 1, 1, False)

        lax.fori_loop(0, (nb - 1 - j) // 2, lambda t, c: (tile(j + 1 + odd + 2 * t, 2, False), c)[1], 0)
        dk_ref[...] = (dk_s[...] * FOX_SCALE).astype(BF16)
        dv_ref[...] = dv_s[...].astype(BF16)
        dc_ref[...] = jnp.sum(dc_s[...], axis=1, keepdims=True)

    head = lambda h, j: (0, h)
    row = pl.BlockSpec((None, 1, s_len), lambda h, j: (h, 0, 0))
    return _pcall(
        body,
        name="fox_bwd",
        grid=(FOX_H, nb),
        out_shape=[
            jax.ShapeDtypeStruct((s_len, FOX_W), F32),
            jax.ShapeDtypeStruct((s_len, FOX_W), BF16),
            jax.ShapeDtypeStruct((s_len, FOX_W), BF16),
            jax.ShapeDtypeStruct((FOX_H, s_len, 1), F32),
            jax.ShapeDtypeStruct((FOX_H, 1, s_len), F32),
        ],
        in_specs=[
            pl.BlockSpec((s_len, FOX_DH), head),
            pl.BlockSpec((blk, FOX_DH), lambda h, j: (j, FOX_H + h)),
            pl.BlockSpec((blk, FOX_DH), lambda h, j: (j, 2 * FOX_H + h)),
            pl.BlockSpec((None, blk, 1), lambda h, j: (h, j, 0)),
            row,
            row,
            pl.BlockSpec((s_len, FOX_DH), head),
        ],
        out_specs=[
            pl.BlockSpec((s_len, FOX_DH), head),
            pl.BlockSpec((blk, FOX_DH), lambda h, j: (j, h)),
            pl.BlockSpec((blk, FOX_DH), lambda h, j: (j, h)),
            pl.BlockSpec((None, blk, 1), lambda h, j: (h, j, 0)),
            row,
        ],
        scratch_shapes=[
            pltpu.VMEM((blk, FOX_DH), F32),
            pltpu.VMEM((blk, FOX_DH), F32),
            pltpu.VMEM((blk, 128), F32),
            pltpu.VMEM((blk, 2 * blk), F32),
        ],
        compiler_params=_cp("parallel", "arbitrary"),
    )(qkv, qkv, qkv, cum_col, lse_row, delta_row, do)


def _swa_bias():
    rows = SWA_G * WINDOW
    r = np.arange(rows)[:, None]
    c = np.arange(2 * WINDOW)[None, :]
    dist = (r % WINDOW) - c + WINDOW
    valid = (dist >= 0) & (dist < WINDOW)
    out = np.empty((SWA_HKV, rows, 2 * WINDOW), np.float32)
    for g in range(SWA_HKV):
        slope = np.array([SLOPES[g * SWA_G + t] for t in range(SWA_G)], np.float32)[r // WINDOW]
        out[g] = np.where(valid, -(slope * dist.astype(np.float32)), np.float32(NEG))
    return jnp.asarray(out)


def _swa_group(i, q_ref, kk, sinks_ref, bias_ref, g):
    rows = SWA_G * WINDOW
    c = lax.broadcasted_iota(jnp.int32, (1, 2 * WINDOW), 1)
    no_prev = jnp.where((c < WINDOW) & (i == 0), NEG, 0.0)
    head = lax.broadcasted_iota(jnp.int32, (rows, 1), 0) // WINDOW
    sink = jnp.zeros((rows, 1), F32)
    for t in range(SWA_G):
        sink = jnp.where(head == t, sinks_ref[g * SWA_G + t], sink)
    q = jnp.concatenate([q_ref[:, (g * SWA_G + t) * SWA_DH : (g * SWA_G + t + 1) * SWA_DH] for t in range(SWA_G)], axis=0)
    k = kk[:, g * SWA_DH : (g + 1) * SWA_DH]
    s = lax.dot_general(q, k, _NT, preferred_element_type=F32) * SWA_SCALE + bias_ref[g] + no_prev
    m = jnp.maximum(jnp.max(s, axis=1, keepdims=True), sink)
    e = jnp.exp(s - m)
    e_sink = jnp.exp(sink - m)
    inv = 1.0 / (jnp.sum(e, axis=1, keepdims=True) + e_sink)
    return q, k, e * inv, e_sink * inv


def _swa_specs(col_q, col_k, col_v, rev, nb):
    def blk(t):
        return nb - 1 - t if rev else t

    return [
        pl.BlockSpec((WINDOW, SWA_W), lambda t: (blk(t), col_q)),
        pl.BlockSpec((WINDOW, SWA_KVW), lambda t: (jnp.maximum(blk(t) - 1, 0), col_k)),
        pl.BlockSpec((WINDOW, SWA_KVW), lambda t: (blk(t), col_k)),
        pl.BlockSpec((WINDOW, SWA_KVW), lambda t: (jnp.maximum(blk(t) - 1, 0), col_v)),
        pl.BlockSpec((WINDOW, SWA_KVW), lambda t: (blk(t), col_v)),
    ]


def _swa_fwd(qkv, sinks):
    s_len = qkv.shape[0]
    nb = s_len // WINDOW
    bias_spec = pl.BlockSpec((SWA_HKV, SWA_G * WINDOW, 2 * WINDOW), lambda t: (0, 0, 0))

    def body(q_ref, kp_ref, kc_ref, vp_ref, vc_ref, sinks_ref, bias_ref, o_ref):
        i = pl.program_id(0)
        kk = jnp.concatenate([kp_ref[...], kc_ref[...]], axis=0)
        vv = jnp.concatenate([vp_ref[...], vc_ref[...]], axis=0)
        for g in range(SWA_HKV):
            _, _, p, _ = _swa_group(i, q_ref, kk, sinks_ref, bias_ref, g)
            o = jnp.dot(p.astype(BF16), vv[:, g * SWA_DH : (g + 1) * SWA_DH], preferred_element_type=F32)
            for t in range(SWA_G):
                h = g * SWA_G + t
                o_ref[:, h * SWA_DH : (h + 1) * SWA_DH] = o[t * WINDOW : (t + 1) * WINDOW, :]

    return _pcall(
        body,
        name="swa_fwd",
        grid=(nb,),
        out_shape=jax.ShapeDtypeStruct((s_len, SWA_W), F32),
        in_specs=_swa_specs(0, 4, 5, False, nb) + [pl.BlockSpec(memory_space=pltpu.SMEM), bias_spec],
        out_specs=pl.BlockSpec((WINDOW, SWA_W), lambda t: (t, 0)),
        compiler_params=_cp("parallel"),
    )(qkv, qkv, qkv, qkv, qkv, sinks, _swa_bias())


def _swa_bwd(qkv, sinks, do):
    s_len = qkv.shape[0]
    nb = s_len // WINDOW
    bias_spec = pl.BlockSpec((SWA_HKV, SWA_G * WINDOW, 2 * WINDOW), lambda t: (0, 0, 0))

    def body(q_ref, kp_ref, kc_ref, vp_ref, vc_ref, sinks_ref, bias_ref, do_ref, dq_ref, dk_ref, dv_ref, dsink_ref, ck_s, cv_s, dkk_s, dvv_s):
        t = pl.program_id(0)
        i = nb - 1 - t

        @pl.when(t == 0)
        def _():
            ck_s[...] = jnp.zeros(ck_s.shape, F32)
            cv_s[...] = jnp.zeros(cv_s.shape, F32)
            dsink_ref[...] = jnp.zeros(dsink_ref.shape, F32)

        kk = jnp.concatenate([kp_ref[...], kc_ref[...]], axis=0)
        vv = jnp.concatenate([vp_ref[...], vc_ref[...]], axis=0)
        lane = lax.broadcasted_iota(jnp.int32, (1, 128), 1)
        dsink = jnp.zeros((1, 128), F32)
        for g in range(SWA_HKV):
            cols = slice(g * SWA_DH, (g + 1) * SWA_DH)
            q, k, p, p_sink = _swa_group(i, q_ref, kk, sinks_ref, bias_ref, g)
            dob = jnp.concatenate([do_ref[:, (g * SWA_G + t) * SWA_DH : (g * SWA_G + t + 1) * SWA_DH] for t in range(SWA_G)], axis=0)
            dp = lax.dot_general(dob, vv[:, cols], _NT, preferred_element_type=F32)
            delta = jnp.sum(p * dp, axis=1, keepdims=True)
            dsb = (p * (dp - delta)).astype(BF16)
            dq = (jnp.dot(dsb, k, preferred_element_type=F32) * SWA_SCALE).astype(BF16)
            ps_d = p_sink * delta
            for t in range(SWA_G):
                h = g * SWA_G + t
                dq_ref[:, h * SWA_DH : (h + 1) * SWA_DH] = dq[t * WINDOW : (t + 1) * WINDOW, :]
                dsink = dsink + jnp.where(lane == h, -jnp.sum(ps_d[t * WINDOW : (t + 1) * WINDOW, :], axis=0, keepdims=True), 0.0)
            dkk_s[:, cols] = lax.dot_general(dsb, q, _TN, preferred_element_type=F32) * SWA_SCALE
            dvv_s[:, cols] = lax.dot_general(p.astype(BF16), dob, _TN, preferred_element_type=F32)
        dk_ref[...] = (dkk_s[WINDOW:, :] + ck_s[...]).astype(BF16)
        dv_ref[...] = (dvv_s[WINDOW:, :] + cv_s[...]).astype(BF16)
        ck_s[...] = dkk_s[:WINDOW, :]
        cv_s[...] = dvv_s[:WINDOW, :]
        dsink_ref[...] += dsink

    row = lambda t: (nb - 1 - t, 0)
    return _pcall(
        body,
        name="swa_bwd",
        grid=(nb,),
        out_shape=[
            jax.ShapeDtypeStruct((s_len, SWA_W), BF16),
            jax.ShapeDtypeStruct((s_len, SWA_KVW), BF16),
            jax.ShapeDtypeStruct((s_len, SWA_KVW), BF16),
            jax.ShapeDtypeStruct((1, 128), F32),
        ],
        in_specs=_swa_specs(0, 4, 5, True, nb)
        + [pl.BlockSpec(memory_space=pltpu.SMEM), bias_spec, pl.BlockSpec((WINDOW, SWA_W), row)],
        out_specs=[
            pl.BlockSpec((WINDOW, SWA_W), row),
            pl.BlockSpec((WINDOW, SWA_KVW), row),
            pl.BlockSpec((WINDOW, SWA_KVW), row),
            pl.BlockSpec((1, 128), lambda t: (0, 0)),
        ],
        scratch_shapes=[
            pltpu.VMEM((WINDOW, SWA_KVW), F32),
            pltpu.VMEM((WINDOW, SWA_KVW), F32),
            pltpu.VMEM((2 * WINDOW, SWA_KVW), F32),
            pltpu.VMEM((2 * WINDOW, SWA_KVW), F32),
        ],
        compiler_params=_cp("arbitrary"),
    )(qkv, qkv, qkv, qkv, qkv, sinks, _swa_bias(), do)


def _branch_fwd(o, gates, g_blk, w_b, name):
    s_len, wd = o.shape
    d = w_b.shape[1]
    tm = min(512, s_len)

    def body(o_ref, g_ref, w_ref, y_ref, a_ref):
        g = g_ref[...].astype(F32)
        a = (o_ref[...] * (g * _sigmoid(g))).astype(BF16)
        a_ref[...] = a
        y_ref[...] = jnp.dot(a, w_ref[...], preferred_element_type=F32).astype(BF16)

    return _pcall(
        body,
        name=name,
        grid=(s_len // tm,),
        out_shape=[jax.ShapeDtypeStruct((s_len, d), BF16), jax.ShapeDtypeStruct((s_len, wd), BF16)],
        in_specs=[
            pl.BlockSpec((tm, wd), lambda i: (i, 0)),
            pl.BlockSpec((tm, wd), lambda i: (i, g_blk)),
            pl.BlockSpec((wd, d), lambda i: (0, 0)),
        ],
        out_specs=[pl.BlockSpec((tm, d), lambda i: (i, 0)), pl.BlockSpec((tm, wd), lambda i: (i, 0))],
        compiler_params=_cp("parallel"),
    )(o, gates, w_b)


def _out_stage(gates, mf_blk, y_fox, y_swa, w_out, x, ada, ln_g, ln_b, target):
    s_len, d = x.shape
    tm = min(256, s_len)
    n_steps = s_len // tm

    def body(mf_ref, ms_ref, yf_ref, ys_ref, w_ref, x_ref, gate_ref, lg_ref, lb_ref, t_ref, mg_ref, dza_ref, dsub_ref, red_ref):
        i = pl.program_id(0)
        merged = _sigmoid(mf_ref[...].astype(F32)) * yf_ref[...].astype(F32) + _sigmoid(ms_ref[...].astype(F32)) * ys_ref[...].astype(F32)
        mb = merged.astype(BF16)
        mg_ref[...] = mb
        sub = jnp.dot(mb, w_ref[...], preferred_element_type=F32)
        gate = gate_ref[...]
        z = ALPHA * x_ref[...] + gate * sub
        mu = jnp.mean(z, axis=-1, keepdims=True)
        zc = z - mu
        var = jnp.mean(zc * zc, axis=-1, keepdims=True)
        rstd = lax.rsqrt(var + LN_EPS)
        zhat = zc * rstd
        err = zhat * lg_ref[...] + lb_ref[...] - t_ref[...]
        dout = err * (1.0 / d)
        dzhat = dout * lg_ref[...]
        dz = rstd * (dzhat - jnp.mean(dzhat, axis=-1, keepdims=True) - zhat * jnp.mean(dzhat * zhat, axis=-1, keepdims=True))
        dza_ref[...] = ALPHA * dz
        dsub_ref[...] = (gate * dz).astype(BF16)
        part = jnp.concatenate(
            [
                jnp.sum(dz * sub, axis=0, keepdims=True),
                jnp.sum(dout * zhat, axis=0, keepdims=True),
                jnp.sum(dout, axis=0, keepdims=True),
                jnp.sum(err * err, axis=0, keepdims=True),
                jnp.zeros((4, d), F32),
            ],
            axis=0,
        )

        @pl.when(i == 0)
        def _():
            red_ref[...] = part

        @pl.when(i > 0)
        def _():
            red_ref[...] += part

        @pl.when(i == n_steps - 1)
        def _():
            red_ref[4:5, :] = jnp.broadcast_to(jnp.sum(red_ref[3:4, :], axis=1, keepdims=True), (1, d))

    row = pl.BlockSpec((tm, d), lambda i: (i, 0))
    vec = pl.BlockSpec((1, d), lambda i: (0, 0))
    return _pcall(
        body,
        name="out_stage",
        grid=(n_steps,),
        out_shape=[
            jax.ShapeDtypeStruct((s_len, d), BF16),
            jax.ShapeDtypeStruct((s_len, d), F32),
            jax.ShapeDtypeStruct((s_len, d), BF16),
            jax.ShapeDtypeStruct((8, d), F32),
        ],
        in_specs=[
            pl.BlockSpec((tm, d), lambda i: (i, mf_blk)),
            pl.BlockSpec((tm, d), lambda i: (i, mf_blk + 1)),
            row,
            row,
            pl.BlockSpec((d, d), lambda i: (0, 0), pipeline_mode=pl.Buffered(1)),
            row,
            pl.BlockSpec((1, d), lambda i: (0, 2)),
            vec,
            vec,
            row,
        ],
        out_specs=[row, row, row, pl.BlockSpec((8, d), lambda i: (0, 0))],
        compiler_params=_cp("arbitrary"),
    )(gates, gates, y_fox, y_swa, w_out, x, ada, ln_g, ln_b, target)


def _merge_bwd(dsub, w_out, gates, mf_blk, y_fox, y_swa):
    s_len, d = dsub.shape
    tm = min(256, s_len)

    def body(ds_ref, w_ref, mf_ref, ms_ref, yf_ref, ys_ref, dmf_ref, dms_ref, dyf_ref, dys_ref):
        dm = lax.dot_general(ds_ref[...], w_ref[...], _NT, preferred_element_type=F32)
        sf, ss = _sigmoid(mf_ref[...].astype(F32)), _sigmoid(ms_ref[...].astype(F32))
        dmf_ref[...] = (dm * yf_ref[...].astype(F32) * (sf * (1.0 - sf))).astype(BF16)
        dms_ref[...] = (dm * ys_ref[...].astype(F32) * (ss * (1.0 - ss))).astype(BF16)
        dyf_ref[...] = (dm * sf).astype(BF16)
        dys_ref[...] = (dm * ss).astype(BF16)

    row = pl.BlockSpec((tm, d), lambda i: (i, 0))
    return _pcall(
        body,
        name="merge_bwd",
        grid=(s_len // tm,),
        out_shape=[jax.ShapeDtypeStruct((s_len, d), BF16)] * 4,
        in_specs=[
            row,
            pl.BlockSpec((d, d), lambda i: (0, 0), pipeline_mode=pl.Buffered(1)),
            pl.BlockSpec((tm, d), lambda i: (i, mf_blk)),
            pl.BlockSpec((tm, d), lambda i: (i, mf_blk + 1)),
            row,
            row,
        ],
        out_specs=[row] * 4,
        compiler_params=_cp("parallel"),
    )(dsub, w_out, gates, gates, y_fox, y_swa)


def _branch_bwd(dy, w_b, o, gates, g_blk, name, n_heads):
    s_len, d = dy.shape
    wd = w_b.shape[0]
    tm = min(512, s_len)

    def body(dy_ref, w_ref, o_ref, g_ref, do_ref, dg_ref, *rest):
        da = lax.dot_general(dy_ref[...], w_ref[...], _NT, preferred_element_type=F32)
        g = g_ref[...].astype(F32)
        sg = _sigmoid(g)
        do = da * (g * sg)
        do_ref[...] = do.astype(BF16)
        o = o_ref[...]
        dg_ref[...] = (da * o * (sg * (1.0 + g * (1.0 - sg)))).astype(BF16)
        if n_heads:
            prod = do * o
            lane = lax.broadcasted_iota(jnp.int32, (1, 128), 1)
            delta = jnp.zeros((tm, 128), F32)
            for h in range(n_heads):
                dh = jnp.sum(prod[:, h * 128 : (h + 1) * 128], axis=1, keepdims=True)
                delta = delta + jnp.where(lane == h, dh, 0.0)
            rest[0][...] = delta

    out_shape = [jax.ShapeDtypeStruct((s_len, wd), BF16), jax.ShapeDtypeStruct((s_len, wd), BF16)]
    out_specs = [pl.BlockSpec((tm, wd), lambda i: (i, 0))] * 2
    if n_heads:
        out_shape.append(jax.ShapeDtypeStruct((s_len, 128), F32))
        out_specs.append(pl.BlockSpec((tm, 128), lambda i: (i, 0)))
    return _pcall(
        body,
        name=name,
        grid=(s_len // tm,),
        out_shape=out_shape,
        in_specs=[
            pl.BlockSpec((tm, d), lambda i: (i, 0)),
            pl.BlockSpec((wd, d), lambda i: (0, 0)),
            pl.BlockSpec((tm, wd), lambda i: (i, 0)),
            pl.BlockSpec((tm, wd), lambda i: (i, g_blk)),
        ],
        out_specs=out_specs,
        compiler_params=_cp("parallel"),
    )(dy, w_b, o, gates)


def _in_bwd(dproj, w_in_t, x, ada, dza, ride):
    s_len, d = x.shape
    k_tot = dproj.shape[1]
    tm, tk = min(512, s_len), 1024
    ni, nk = s_len // tm, k_tot // tk
    n = len(ride)

    def body(dp_ref, w_ref, x_ref, sc_ref, dza_ref, *rest):
        ins, (gx_ref, red_ref), outs = rest[:n], rest[n : n + 2], rest[n + 2 : 2 * n + 2]
        sems, acc_s = rest[2 * n + 2 : 2 * n + 5], rest[2 * n + 5]
        i, kk = pl.program_id(0), pl.program_id(1)

        @pl.when((i == 0) & (kk == 0))
        def _():
            _rider_start("exchange", ins, outs, *sems)

        @pl.when((i == ni - 1) & (kk == nk - 1))
        def _():
            _rider_wait("exchange", ins, outs, *sems)

        part = jnp.dot(dp_ref[...], w_ref[...], preferred_element_type=F32)

        @pl.when(kk == 0)
        def _():
            acc_s[...] = part

        @pl.when(kk > 0)
        def _():
            acc_s[...] += part

        @pl.when(kk == nk - 1)
        def _():
            dh = acc_s[...]
            xv = x_ref[...]
            mu = jnp.mean(xv, axis=-1, keepdims=True)
            xc = xv - mu
            var = jnp.mean(xc * xc, axis=-1, keepdims=True)
            rstd = lax.rsqrt(var + LN_EPS)
            xhat = xc * rstd
            dxhat = dh * (1.0 + sc_ref[...])
            dx = rstd * (dxhat - jnp.mean(dxhat, axis=-1, keepdims=True) - xhat * jnp.mean(dxhat * xhat, axis=-1, keepdims=True))
            gx_ref[...] = dza_ref[...] + dx
            part_r = jnp.concatenate(
                [jnp.sum(dh, axis=0, keepdims=True), jnp.sum(dh * xhat, axis=0, keepdims=True), jnp.zeros((6, d), F32)], axis=0
            )

            @pl.when(i == 0)
            def _():
                red_ref[...] = part_r

            @pl.when(i > 0)
            def _():
                red_ref[...] += part_r

    row = pl.BlockSpec((tm, d), lambda i, kk: (i, 0))
    hbm = pl.BlockSpec(memory_space=pltpu.HBM)
    return _pcall(
        body,
        name="in_bwd",
        grid=(ni, nk),
        out_shape=[jax.ShapeDtypeStruct((s_len, d), F32), jax.ShapeDtypeStruct((8, d), F32)]
        + [jax.ShapeDtypeStruct(r.shape, r.dtype) for r in ride],
        in_specs=[
            pl.BlockSpec((tm, tk), lambda i, kk: (i, kk)),
            pl.BlockSpec((tk, d), lambda i, kk: (kk, 0)),
            row,
            pl.BlockSpec((1, d), lambda i, kk: (0, 1)),
            row,
        ]
        + [hbm] * n,
        out_specs=[row, pl.BlockSpec((8, d), lambda i, kk: (0, 0))] + [hbm] * n,
        scratch_shapes=_rider_scratch(n) + [pltpu.VMEM((tm, d), F32)],
        compiler_params=_cp("arbitrary", "arbitrary"),
    )(dproj, w_in_t, x, ada, dza, *ride)


def _pad_lanes(v, n):
    return jnp.pad(v, ((0, 0), (0, n - v.shape[1])))


def kernel(x, c, w_ada, b_ada, w_in, b_f, attn_sinks, w_br_fox, w_br_swa, w_out, ln_g, ln_b, loss_target, m_w_ada, m_b_ada, m_w_in, m_b_f, m_attn_sinks, m_w_br_fox, m_w_br_swa, m_w_out, m_ln_g, m_ln_b, v_w_ada, v_b_ada, v_w_in, v_b_f, v_attn_sinks, v_w_br_fox, v_w_br_swa, v_w_out, v_ln_g, v_ln_b):
    x2, tgt = x[0], loss_target[0]
    s_len, d = x2.shape
    me = 4 * lax.axis_index("x") + 2 * lax.axis_index("y") + lax.axis_index("c")
    off_ms = OFF_MF + d
    in_pad = off_ms + d
    c_ada = w_ada.shape[2]
    c_in = w_in.shape[2]
    c_br = w_br_fox.shape[2]

    w_in_full = _all_gather(w_in[0].T.astype(BF16), "ag_w_in", pltpu.HBM).reshape(N_DEV * c_in, d)
    w_in_pad = jnp.concatenate(
        [w_in_full[:REAL_FLOG_END], jnp.zeros((FLOG_PAD - N_FLOG, d), BF16), w_in_full[REAL_FLOG_END:]], axis=0
    )

    c_all = _gather_rows(c, "ag_c")
    b_cols = lax.dynamic_slice(b_ada, (0, me * c_ada), (1, c_ada))
    ada_cols = _ada_fwd(c_all, w_ada[0], b_cols)
    ada_g = _all_gather(ada_cols, "ag_ada", pltpu.VMEM)
    ada = lax.dynamic_index_in_dim(ada_g, me, axis=1, keepdims=False).reshape(1, N_DEV * c_ada)

    h = _ln_mod(x2, ada)
    qkv_fox = _mm_cols(h, w_in_pad, OFF_FQ, 3 * FOX_W, BF16, "proj_fox")
    flog = _mm_cols(h, w_in_pad, OFF_FLOG, FLOG_PAD, F32, "proj_flog")
    qkv_swa = _mm_cols(h, w_in_pad, OFF_SQ, SWA_W + 2 * SWA_KVW, BF16, "proj_swa")
    gates, w_bf, w_bs, w_o = _mm_cols(
        h, w_in_pad, OFF_GF, in_pad - OFF_GF, BF16, "proj_gates",
        ride=(w_br_fox[0].astype(BF16), w_br_swa[0].astype(BF16), w_out[0].astype(BF16)),
    )
    w_bf = w_bf.reshape(N_DEV, FOX_W, c_br).transpose(1, 0, 2).reshape(FOX_W, d)
    w_bs = w_bs.reshape(N_DEV, SWA_W, c_br).transpose(1, 0, 2).reshape(SWA_W, d)
    w_o = w_o.reshape(d, d)
    mf_blk = (OFF_MF - OFF_GF) // d

    flog_t = flog[:, :N_FLOG].T
    bf_col = b_f.reshape(FOX_H, 1)
    cum = _fox_cum(flog_t, bf_col)
    cum_row = cum.reshape(FOX_H, 1, s_len)
    o_fox, lse = _fox_fwd(qkv_fox, cum_row)
    sinks = attn_sinks.reshape(SWA_HQ)
    o_swa = _swa_fwd(qkv_swa, sinks)

    y_fox, a_fox = _branch_fwd(o_fox, gates, 0, w_bf, "branch_fox")
    y_swa, a_swa = _branch_fwd(o_swa, gates, 1, w_bs, "branch_swa")
    merged, dza, dsub, red = _out_stage(gates, mf_blk, y_fox, y_swa, w_o, x2, ada, ln_g, ln_b, tgt)
    loss = lax.psum(0.5 * red[4, 0] / d, ("x", "y", "c"))

    dmf, dms, dy_fox, dy_swa = _merge_bwd(dsub, w_o, gates, mf_blk, y_fox, y_swa)
    do_fox, dg_fox, delta = _branch_bwd(dy_fox, w_bf, o_fox, gates, 0, "branch_fox_bwd", FOX_H)
    do_swa, dg_swa = _branch_bwd(dy_swa, w_bs, o_swa, gates, 1, "branch_swa_bwd", 0)
    delta_row = delta[:, :FOX_H].T.reshape(FOX_H, 1, s_len)
    dq_f, dk_f, dv_f, dcol, drow = _fox_bwd(
        qkv_fox, cum.reshape(FOX_H, s_len, 1), lse.reshape(FOX_H, 1, s_len), delta_row, do_fox
    )
    dflog_t, dbf = _fox_gate_bwd(drow.reshape(FOX_H, s_len), dcol.reshape(FOX_H, s_len), flog_t, bf_col)
    dq_s, dk_s, dv_s, dsink = _swa_bwd(qkv_swa, sinks, do_swa)
    dflog = _pad_lanes(dflog_t.T, FLOG_PAD).astype(BF16)
    dproj = jnp.concatenate([dq_f.astype(BF16), dk_f, dv_f, dflog, dq_s, dk_s, dv_s, dg_fox, dg_swa, dmf, dms], axis=1)
    g_w_bf = _mm_tn(a_fox, dy_fox, "grad_w_br_fox")
    g_w_bs = _mm_tn(a_swa, dy_swa, "grad_w_br_swa")
    g_w_o = _mm_tn(merged, dsub, "grad_w_out")
    g_w_in, r_bf, r_bs, r_o = _mm_tn(
        dproj, h, "grad_w_in",
        ride=(
            g_w_bf.reshape(FOX_W, N_DEV, c_br).transpose(1, 0, 2),
            g_w_bs.reshape(SWA_W, N_DEV, c_br).transpose(1, 0, 2),
            g_w_o.reshape(N_DEV, d // N_DEV, d),
        ),
    )
    g_w_in = jnp.concatenate([g_w_in[:REAL_FLOG_END], g_w_in[OFF_SQ:]], axis=0)

    grad_x, red2, r_in = _in_bwd(dproj, w_in_pad, x2, ada, dza, ride=(g_w_in.reshape(N_DEV, c_in, d),))
    out_w_in = _sum_adam_t(r_in, w_in[0].T, m_w_in[0].T, v_w_in[0].T, "adam_w_in")
    out_w_in = [o.T for o in out_w_in]
    out_w_bf = _sum_adam(r_bf, w_br_fox[0], m_w_br_fox[0], v_w_br_fox[0], "adam_w_br_fox")
    out_w_bs = _sum_adam(r_bs, w_br_swa[0], m_w_br_swa[0], v_w_br_swa[0], "adam_w_br_swa")
    out_w_o = _sum_adam(r_o, w_out[0], m_w_out[0], v_w_out[0], "adam_w_out")

    packed = jnp.concatenate([red2[0:1], red2[1:2], red[0:1], _pad_lanes(dbf[:, 0].reshape(1, FOX_H), 128), dsink, red[1:2], red[2:3]], axis=1)
    gathered = _gather_rows(packed, "ag_small")
    pack = lambda a, b, cc, dd, e: jnp.concatenate([a, _pad_lanes(b, 128), _pad_lanes(cc, 128), dd, e], axis=1)
    small = _small_adam(
        gathered,
        pack(b_ada, b_f, attn_sinks, ln_g, ln_b),
        pack(m_b_ada, m_b_f, m_attn_sinks, m_ln_g, m_ln_b),
        pack(v_b_ada, v_b_f, v_attn_sinks, v_ln_g, v_ln_b),
    )
    dada_cols = lax.dynamic_slice(gathered, (0, me * c_ada), (N_DEV, c_ada))
    out_w_ada = _wada_adam(c_all.T, dada_cols, w_ada[0], m_w_ada[0], v_w_ada[0])

    o1, o2, o3 = 3 * d, 3 * d + 128, 3 * d + 256

    def unpack(p):
        return p[:, :o1], p[:, o1 : o1 + FOX_H], p[:, o2 : o2 + SWA_HQ], p[:, o3 : o3 + d], p[:, o3 + d : o3 + 2 * d]

    kinds = []
    for k in range(4):
        b_ada_k, b_f_k, sinks_k, ln_g_k, ln_b_k = unpack(small[k])
        kinds.append(
            [out_w_ada[k][None], b_ada_k, out_w_in[k][None], b_f_k, sinks_k, out_w_bf[k][None], out_w_bs[k][None], out_w_o[k][None], ln_g_k, ln_b_k]
        )
    return (loss, grad_x[None], *kinds[0], *kinds[1], *kinds[2], *kinds[3])
```

```python
import numpy as np
import jax
import jax.numpy as jnp
from jax import lax
from jax.experimental import pallas as pl
from jax.experimental.pallas import tpu as pltpu

F32 = jnp.float32
BF16 = jnp.bfloat16
N_DEV = 8
MESH = pl.DeviceIdType.MESH

FOX_H, FOX_DH, FOX_W = 8, 128, 1024
SWA_HQ, SWA_HKV, SWA_DH, SWA_G = 16, 4, 64, 4
SWA_W, SWA_KVW, WINDOW = 1024, 256, 128
LN_EPS = 1e-5
NEG = -1e30
DEPTH = 1
ALPHA = (2.0 * DEPTH) ** 0.25
FOX_SCALE = FOX_DH ** -0.5
SWA_SCALE = SWA_DH ** -0.5
SLOPES = [2.0 ** (-8.0 * (h + 1.0) / SWA_HQ) for h in range(SWA_HQ)]

ADAM_LR, ADAM_B1, ADAM_B2, ADAM_EPS, ADAM_WD, ADAM_STEP = 0.001, 0.9, 0.999, 1e-08, 0.01, 10

N_FLOG = 8
FLOG_PAD = 512
OFF_FQ, OFF_FK, OFF_FV, OFF_FLOG = 0, 1024, 2048, 3072
OFF_SQ = OFF_FLOG + FLOG_PAD
OFF_SK = OFF_SQ + SWA_W
OFF_SV = OFF_SK + SWA_KVW
OFF_GF = OFF_SV + SWA_KVW
OFF_GS = OFF_GF + FOX_W
OFF_MF = OFF_GS + SWA_W
REAL_FLOG_END = OFF_FLOG + N_FLOG

ATT_BLK = 512
VMEM_LIMIT = 52 * 1024 * 1024


def _pcall(body, **kw):
    return pl.pallas_call(body, **kw)


def _cp(*sem):
    return pltpu.CompilerParams(dimension_semantics=sem, vmem_limit_bytes=VMEM_LIMIT)


def _sigmoid(x):
    return 1.0 / (1.0 + jnp.exp(-x))


def _all_gather(x, name, space):
    m_per, n = x.shape

    def body(x_ref, out_ref, send_sems, recv_sems, local_sem):
        mx, my, mc = lax.axis_index("x"), lax.axis_index("y"), lax.axis_index("c")
        me, sibling = (mx, my, mc), (mx, my, 1 - mc)
        chips = [(1 - mx, my), (mx, 1 - my), (1 - mx, 1 - my)]

        def rows(px, py, pc):
            return out_ref.at[4 * px + 2 * py + pc]

        def copy(k, block, to, src=None):
            return pltpu.make_async_remote_copy(
                src_ref=rows(*block) if src is None else src,
                dst_ref=rows(*block),
                send_sem=send_sems.at[k],
                recv_sem=recv_sems.at[k],
                device_id=to,
                device_id_type=MESH,
            )

        mine = pltpu.make_async_copy(x_ref, rows(*me), local_sem)
        mine.start()
        first = [copy(0, me, sibling, src=x_ref)]
        first += [copy(1 + j, me, (*chip, mc), src=x_ref) for j, chip in enumerate(chips)]
        for cp in first:
            cp.start()
        passed = [copy(4 + j, (*chip, mc), sibling) for j, chip in enumerate(chips)]
        for j, chip in enumerate(chips):
            copy(1 + j, (*chip, mc), me).wait_recv()
            passed[j].start()
        copy(0, sibling, me).wait_recv()
        for j, chip in enumerate(chips):
            copy(4 + j, (*chip, 1 - mc), me).wait_recv()
        for cp in first + passed:
            cp.wait_send()
        mine.wait()

    return _pcall(
        body,
        name=name,
        out_shape=jax.ShapeDtypeStruct((N_DEV, m_per, n), x.dtype),
        in_specs=[pl.BlockSpec(memory_space=space)],
        out_specs=pl.BlockSpec(memory_space=space),
        scratch_shapes=[pltpu.SemaphoreType.DMA((7,)), pltpu.SemaphoreType.DMA((7,)), pltpu.SemaphoreType.DMA],
    )(x)


def _peer(d, mx, my, mc):
    return (1 - mx if (d >> 2) & 1 else mx, 1 - my if (d >> 1) & 1 else my, 1 - mc if d & 1 else mc)


def _rider_copies(kind, ins, outs, send_sems, recv_sems, local_sems):
    mx, my, mc = lax.axis_index("x"), lax.axis_index("y"), lax.axis_index("c")
    me = 4 * mx + 2 * my + mc
    remote, local = [], []
    for a in range(len(ins)):
        if kind == "gather":
            m_per = ins[a].shape[0]
            mine = outs[a].at[pl.ds(me * m_per, m_per), :]
            local.append(pltpu.make_async_copy(ins[a], mine, local_sems.at[a]))
        else:
            local.append(pltpu.make_async_copy(ins[a].at[me], outs[a].at[0], local_sems.at[a]))
        for d in range(1, N_DEV):
            px, py, pc = _peer(d, mx, my, mc)
            if kind == "gather":
                src, dst = ins[a], mine
            else:
                src, dst = ins[a].at[4 * px + 2 * py + pc], outs[a].at[d]
            remote.append(
                pltpu.make_async_remote_copy(
                    src_ref=src,
                    dst_ref=dst,
                    send_sem=send_sems.at[a * 7 + d - 1],
                    recv_sem=recv_sems.at[a * 7 + d - 1],
                    device_id=(px, py, pc),
                    device_id_type=MESH,
                )
            )
    return remote, local


def _rider_start(*args):
    remote, local = _rider_copies(*args)
    for cp in local + remote:
        cp.start()


def _rider_wait(*args):
    remote, local = _rider_copies(*args)
    for cp in remote:
        cp.wait_recv()
    for cp in remote:
        cp.wait_send()
    for cp in local:
        cp.wait()


def _rider_scratch(n):
    return [pltpu.SemaphoreType.DMA((7 * n,)), pltpu.SemaphoreType.DMA((7 * n,)), pltpu.SemaphoreType.DMA((n,))]


def _gather_rows(v, name):
    n = v.shape[1]
    return _all_gather(jnp.broadcast_to(v, (8, n)), name, pltpu.VMEM)[:, 0, :]


def _adamw(w, g, m, v):
    m = ADAM_B1 * m + (1.0 - ADAM_B1) * g
    v = ADAM_B2 * v + (1.0 - ADAM_B2) * (g * g)
    m_hat = m / (1.0 - ADAM_B1**ADAM_STEP)
    v_hat = v / (1.0 - ADAM_B2**ADAM_STEP)
    delta = -ADAM_LR * (m_hat / (jnp.sqrt(v_hat) + ADAM_EPS) + ADAM_WD * w)
    return delta, m, v


def _sum_adam(recv, w, m, v, name):
    _, r_tot, c = recv.shape
    c_pad = -(-c // 128) * 128
    tr = r_tot
    while 8 * tr * c_pad * 4 > 6 * 1024 * 1024 and tr % 32 == 0:
        tr //= 2

    def body(r_ref, w_ref, m_ref, v_ref, g_ref, d_ref, nm_ref, nv_ref):
        g = r_ref[0].astype(F32)
        for k in range(1, N_DEV):
            g = g + r_ref[k].astype(F32)
        d, nm, nv = _adamw(w_ref[...], g, m_ref[...], v_ref[...])
        g_ref[...] = g
        d_ref[...] = d
        nm_ref[...] = nm
        nv_ref[...] = nv

    blk = pl.BlockSpec((tr, c), lambda i: (i, 0))
    return _pcall(
        body,
        name=name,
        grid=(r_tot // tr,),
        out_shape=[jax.ShapeDtypeStruct((r_tot, c), F32)] * 4,
        in_specs=[pl.BlockSpec((N_DEV, tr, c), lambda i: (0, i, 0)), blk, blk, blk],
        out_specs=[blk] * 4,
        compiler_params=_cp("parallel"),
    )(recv, w, m, v)


def _sum_adam_t(recv, w, m, v, name):
    _, c, r_tot = recv.shape
    tr = min(256, r_tot)

    def body(r_ref, w_ref, m_ref, v_ref, g_ref, d_ref, nm_ref, nv_ref):
        g = r_ref[0].astype(F32)
        for k in range(1, N_DEV):
            g = g + r_ref[k].astype(F32)
        d, nm, nv = _adamw(w_ref[...], g, m_ref[...], v_ref[...])
        g_ref[...] = g
        d_ref[...] = d
        nm_ref[...] = nm
        nv_ref[...] = nv

    blk = pl.BlockSpec((c, tr), lambda i: (0, i))
    return _pcall(
        body,
        name=name,
        grid=(r_tot // tr,),
        out_shape=[jax.ShapeDtypeStruct((c, r_tot), F32)] * 4,
        in_specs=[pl.BlockSpec((N_DEV, c, tr), lambda i: (0, 0, i)), blk, blk, blk],
        out_specs=[blk] * 4,
        compiler_params=_cp("parallel"),
    )(recv, w, m, v)


def _wada_adam(c_t, dada_cols, w, m, v):
    d_model, c = w.shape
    tr = min(256, d_model)

    def body(ct_ref, da_ref, w_ref, m_ref, v_ref, g_ref, d_ref, nm_ref, nv_ref):
        g = jnp.dot(ct_ref[...].astype(BF16), da_ref[...].astype(BF16), preferred_element_type=F32)
        d, nm, nv = _adamw(w_ref[...], g, m_ref[...], v_ref[...])
        g_ref[...] = g
        d_ref[...] = d
        nm_ref[...] = nm
        nv_ref[...] = nv

    blk = pl.BlockSpec((tr, c), lambda i: (i, 0))
    return _pcall(
        body,
        name="wada_adam",
        grid=(d_model // tr,),
        out_shape=[jax.ShapeDtypeStruct((d_model, c), F32)] * 4,
        in_specs=[pl.BlockSpec((tr, N_DEV), lambda i: (i, 0)), pl.BlockSpec((N_DEV, c), lambda i: (0, 0)), blk, blk, blk],
        out_specs=[blk] * 4,
        compiler_params=_cp("parallel"),
    )(c_t, dada_cols, w, m, v)


def _small_adam(gathered, w, m, v):
    p = w.shape[1]

    def body(a_ref, w_ref, m_ref, v_ref, g_ref, d_ref, nm_ref, nv_ref):
        g = a_ref[0:1, :]
        for k in range(1, N_DEV):
            g = g + a_ref[k : k + 1, :]
        d, nm, nv = _adamw(w_ref[...], g, m_ref[...], v_ref[...])
        g_ref[...] = g
        d_ref[...] = d
        nm_ref[...] = nm
        nv_ref[...] = nv

    return _pcall(
        body,
        name="small_adam",
        out_shape=[jax.ShapeDtypeStruct((1, p), F32)] * 4,
    )(gathered, w, m, v)


def _ada_fwd(c_all, w_ada, b_cols):
    c = w_ada.shape[1]

    def body(c_ref, w_ref, b_ref, o_ref):
        o_ref[...] = jnp.dot(c_ref[...].astype(BF16), w_ref[...].astype(BF16), preferred_element_type=F32) + b_ref[...]

    return _pcall(
        body,
        name="ada_fwd",
        out_shape=jax.ShapeDtypeStruct((N_DEV, c), F32),
        compiler_params=_cp(),
    )(c_all, w_ada, b_cols)


def _ln_mod(x, ada):
    s_len, d = x.shape
    tm = min(512, s_len)

    def body(x_ref, sh_ref, sc_ref, h_ref):
        xv = x_ref[...]
        mu = jnp.mean(xv, axis=-1, keepdims=True)
        xc = xv - mu
        var = jnp.mean(xc * xc, axis=-1, keepdims=True)
        xhat = xc * lax.rsqrt(var + LN_EPS)
        h_ref[...] = (xhat * (1.0 + sc_ref[...]) + sh_ref[...]).astype(BF16)

    return _pcall(
        body,
        name="ln_mod",
        grid=(s_len // tm,),
        out_shape=jax.ShapeDtypeStruct((s_len, d), BF16),
        in_specs=[
            pl.BlockSpec((tm, d), lambda i: (i, 0)),
            pl.BlockSpec((1, d), lambda i: (0, 0)),
            pl.BlockSpec((1, d), lambda i: (0, 1)),
        ],
        out_specs=pl.BlockSpec((tm, d), lambda i: (i, 0)),
        compiler_params=_cp("parallel"),
    )(x, ada, ada)


def _mm_cols(a, b, col_off, n_cols, out_dtype, name, ride=()):
    m, k = a.shape
    tm, tn = min(1024, m), 512
    off = col_off // tn
    ni, nj = m // tm, n_cols // tn
    n = len(ride)

    def body(a_ref, b_ref, *rest):
        ins, o_ref, outs, sems = rest[:n], rest[n], rest[n + 1 : 2 * n + 1], rest[2 * n + 1 :]
        i, j = pl.program_id(0), pl.program_id(1)
        if n:

            @pl.when((i == 0) & (j == 0))
            def _():
                _rider_start("gather", ins, outs, *sems)

        o_ref[...] = lax.dot_general(a_ref[...], b_ref[...], _NT, preferred_element_type=F32).astype(out_dtype)
        if n:

            @pl.when((i == ni - 1) & (j == nj - 1))
            def _():
                _rider_wait("gather", ins, outs, *sems)

    hbm = pl.BlockSpec(memory_space=pltpu.HBM)
    out = _pcall(
        body,
        name=name,
        grid=(ni, nj),
        out_shape=[jax.ShapeDtypeStruct((m, n_cols), out_dtype)]
        + [jax.ShapeDtypeStruct((N_DEV * r.shape[0], r.shape[1]), r.dtype) for r in ride],
        in_specs=[pl.BlockSpec((tm, k), lambda i, j: (i, 0)), pl.BlockSpec((tn, k), lambda i, j: (off + j, 0))] + [hbm] * n,
        out_specs=[pl.BlockSpec((tm, tn), lambda i, j: (i, j))] + [hbm] * n,
        scratch_shapes=_rider_scratch(n) if n else [],
        compiler_params=_cp("arbitrary", "arbitrary") if n else _cp("parallel", "parallel"),
    )(a, b, *ride)
    return out if n else out[0]


def _mm_tn(a, b, name, ride=()):
    s_len, m = a.shape
    n = b.shape[1]
    tm, tn, ts = min(1024, m), min(1024, n), min(2048, s_len)
    ni, nj, ns = m // tm, n // tn, s_len // ts
    nr = len(ride)

    def body(a_ref, b_ref, *rest):
        ins, o_ref, outs = rest[:nr], rest[nr], rest[nr + 1 : 2 * nr + 1]
        sems, acc_s = rest[2 * nr + 1 : -1], rest[-1]
        i, j, kk = pl.program_id(0), pl.program_id(1), pl.program_id(2)
        if nr:

            @pl.when((i == 0) & (j == 0) & (kk == 0))
            def _():
                _rider_start("exchange", ins, outs, *sems)

            @pl.when((i == ni - 1) & (j == nj - 1) & (kk == ns - 1))
            def _():
                _rider_wait("exchange", ins, outs, *sems)

        part = lax.dot_general(a_ref[...], b_ref[...], _TN, preferred_element_type=F32)

        @pl.when(kk == 0)
        def _():
            acc_s[...] = part

        @pl.when(kk > 0)
        def _():
            acc_s[...] += part

        @pl.when(kk == ns - 1)
        def _():
            o_ref[...] = acc_s[...].astype(BF16)

    hbm = pl.BlockSpec(memory_space=pltpu.HBM)
    out = _pcall(
        body,
        name=name,
        grid=(ni, nj, ns),
        out_shape=[jax.ShapeDtypeStruct((m, n), BF16)] + [jax.ShapeDtypeStruct(r.shape, r.dtype) for r in ride],
        in_specs=[pl.BlockSpec((ts, tm), lambda i, j, kk: (kk, i)), pl.BlockSpec((ts, tn), lambda i, j, kk: (kk, j))] + [hbm] * nr,
        out_specs=[pl.BlockSpec((tm, tn), lambda i, j, kk: (i, j))] + [hbm] * nr,
        scratch_shapes=(_rider_scratch(nr) if nr else []) + [pltpu.VMEM((tm, tn), F32)],
        compiler_params=_cp("arbitrary", "arbitrary", "arbitrary") if nr else _cp("parallel", "parallel", "arbitrary"),
    )(a, b, *ride)
    return out if nr else out[0]


def _split3(a):
    hi = a.astype(BF16)
    r1 = a - hi.astype(F32)
    mid = r1.astype(BF16)
    lo = (r1 - mid.astype(F32)).astype(BF16)
    return hi, mid, lo


def _dot_ones(a, tri):
    return sum(jnp.dot(t, tri, preferred_element_type=F32) for t in _split3(a))


def _log_sigmoid(x):
    return jnp.minimum(x, 0.0) - jnp.log1p(jnp.exp(-jnp.abs(x)))


def _fox_cum(flog_t, bf_col):
    s_len = flog_t.shape[1]

    def body(fl_ref, bf_ref, cum_ref):
        r = lax.broadcasted_iota(jnp.int32, (128, 128), 0)
        c = lax.broadcasted_iota(jnp.int32, (128, 128), 1)
        upper = (r <= c).astype(BF16)

        def step(t, carry):
            sl = pl.ds(pl.multiple_of(t * 128, 128), 128)
            lf = _log_sigmoid(fl_ref[:, sl] + bf_ref[...])
            cs = _dot_ones(lf, upper) + carry
            cum_ref[:, sl] = cs
            return cs[:, 127:128]

        lax.fori_loop(0, s_len // 128, step, jnp.zeros((FOX_H, 1), F32))

    return _pcall(body, name="fox_cum", out_shape=jax.ShapeDtypeStruct((FOX_H, s_len), F32))(flog_t, bf_col)


def _fox_gate_bwd(drow, dcol, flog_t, bf_col):
    s_len = flog_t.shape[1]
    n = s_len // 128

    def body(dr_ref, dc_ref, fl_ref, bf_ref, dfl_ref, dbf_ref):
        r = lax.broadcasted_iota(jnp.int32, (128, 128), 0)
        c = lax.broadcasted_iota(jnp.int32, (128, 128), 1)
        lower = (r >= c).astype(BF16)

        def step(t, carry):
            run, tot = carry
            sl = pl.ds(pl.multiple_of((n - 1 - t) * 128, 128), 128)
            rc = _dot_ones(dr_ref[:, sl] - dc_ref[:, sl], lower) + run
            dfl = rc * _sigmoid(-(fl_ref[:, sl] + bf_ref[...]))
            dfl_ref[:, sl] = dfl
            return rc[:, 0:1], tot + jnp.sum(dfl, axis=1, keepdims=True)

        zero = jnp.zeros((FOX_H, 1), F32)
        _, tot = lax.fori_loop(0, n, step, (zero, zero))
        dbf_ref[...] = jnp.broadcast_to(tot, (FOX_H, 128))

    return _pcall(
        body,
        name="fox_gate_bwd",
        out_shape=[jax.ShapeDtypeStruct((FOX_H, s_len), F32), jax.ShapeDtypeStruct((FOX_H, 128), F32)],
    )(drow, dcol, flog_t, bf_col)


def _diag_mask(blk, transposed=False):
    r = lax.broadcasted_iota(jnp.int32, (blk, blk), 0)
    c = lax.broadcasted_iota(jnp.int32, (blk, blk), 1)
    return c >= r if transposed else r >= c


_NT = (((1,), (1,)), ((), ()))
_TN = (((0,), (0,)), ((), ()))


def _fox_fwd(qkv, cum_row):
    s_len = qkv.shape[0]
    blk = min(ATT_BLK, s_len)
    nb = s_len // blk
    log2e = 1.4426950408889634

    def body(q_ref, k_ref, v_ref, c_ref, o_ref, lse_ref, mx_s, acc_s, u_s):
        i = pl.program_id(1)

        def key_cols(j, n):
            return pl.ds(pl.multiple_of(j * blk, blk), n * blk)

        def walk(tile):
            lax.fori_loop(0, i // 2, lambda t, c: (tile(2 * t, 2, False), c)[1], 0)

            @pl.when(i % 2 == 1)
            def _():
                tile(i - 1, 1, False)

            tile(i, 1, True)

        def lane_max(j, n, masked):
            cols = key_cols(j, n)
            u = lax.dot_general(q_ref[...], k_ref[cols, :], _NT, preferred_element_type=F32) - c_ref[:, cols] * (1.0 / FOX_SCALE)
            if masked:
                u = jnp.where(_diag_mask(blk), u, NEG)
            u_s[:, cols] = u
            part = u[:, 0:128]
            for t in range(1, n * blk // 128):
                part = jnp.maximum(part, u[:, t * 128 : (t + 1) * 128])
            mx_s[...] = jnp.maximum(mx_s[...], part)

        mx_s[...] = jnp.full(mx_s.shape, NEG, F32)
        walk(lane_max)
        m = jnp.max(mx_s[...], axis=1, keepdims=True)

        def weigh(j, n, masked):
            cols = key_cols(j, n)
            p = jnp.exp2((u_s[:, cols] - m) * (FOX_SCALE * log2e))
            ones_col = (lax.broadcasted_iota(jnp.int32, (n * blk, 128), 1) == 0).astype(BF16)
            v1 = jnp.concatenate([v_ref[cols, :], ones_col], axis=1)
            acc_s[...] += jnp.dot(p.astype(BF16), v1, preferred_element_type=F32)

        acc_s[...] = jnp.zeros(acc_s.shape, F32)
        walk(weigh)
        l = acc_s[:, FOX_DH : FOX_DH + 1]
        o_ref[...] = acc_s[:, :FOX_DH] / l
        lse_ref[...] = m * FOX_SCALE + jnp.log(l)

    return _pcall(
        body,
        name="fox_fwd",
        grid=(FOX_H, nb),
        out_shape=[jax.ShapeDtypeStruct((s_len, FOX_W), F32), jax.ShapeDtypeStruct((FOX_H, s_len, 1), F32)],
        in_specs=[
            pl.BlockSpec((blk, FOX_DH), lambda h, i: (i, h)),
            pl.BlockSpec((s_len, FOX_DH), lambda h, i: (0, FOX_H + h)),
            pl.BlockSpec((s_len, FOX_DH), lambda h, i: (0, 2 * FOX_H + h)),
            pl.BlockSpec((None, 1, s_len), lambda h, i: (h, 0, 0)),
        ],
        out_specs=[
            pl.BlockSpec((blk, FOX_DH), lambda h, i: (i, h)),
            pl.BlockSpec((None, blk, 1), lambda h, i: (h, i, 0)),
        ],
        scratch_shapes=[pltpu.VMEM((blk, 128), F32), pltpu.VMEM((blk, 2 * FOX_DH), F32), pltpu.VMEM((blk, s_len), F32)],
        compiler_params=_cp("parallel", "arbitrary"),
    )(qkv, qkv, qkv, cum_row)


def _fox_bwd(qkv, cum_col, lse_row, delta_row, do):
    s_len = qkv.shape[0]
    blk = min(ATT_BLK, s_len)
    nb = s_len // blk

    def body(q_ref, k_ref, v_ref, c_ref, lse_ref, dl_ref, do_ref, dq_ref, dk_ref, dv_ref, dc_ref, dr_ref, dk_s, dv_s, dc_s, cb_s):
        j = pl.program_id(1)

        @pl.when(j == 0)
        def _():
            dq_ref[...] = jnp.zeros(dq_ref.shape, F32)
            dr_ref[...] = jnp.zeros(dr_ref.shape, F32)

        dk_s[...] = jnp.zeros(dk_s.shape, F32)
        dv_s[...] = jnp.zeros(dv_s.shape, F32)
        dc_s[...] = jnp.zeros(dc_s.shape, F32)
        cb_s[...] = jnp.broadcast_to(c_ref[...], cb_s.shape)

        def tile(i, n, diag):
            rows = pl.ds(pl.multiple_of(i * blk, blk), n * blk)
            q, dob = q_ref[rows, :], do_ref[rows, :]
            k, v = k_ref[...], v_ref[...]
            s_t = lax.dot_general(k, q, _NT, preferred_element_type=F32) * FOX_SCALE - cb_s[:, : n * blk]
            p_t = jnp.exp(s_t - lse_ref[:, rows])
            if diag:
                p_t = jnp.where(_diag_mask(blk, transposed=True), p_t, 0.0)
            dp_t = lax.dot_general(v, dob, _NT, preferred_element_type=F32)
            ds_t = p_t * (dp_t - dl_ref[:, rows])
            dsb = ds_t.astype(BF16)
            dv_s[...] += jnp.dot(p_t.astype(BF16), dob, preferred_element_type=F32)
            dk_s[...] += jnp.dot(dsb, q, preferred_element_type=F32)
            dq_c = lax.dot_general(dsb, k, _TN, preferred_element_type=F32)
            part = ds_t[:, 0:128]
            for t in range(1, n * blk // 128):
                part = part + ds_t[:, t * 128 : (t + 1) * 128]
            dc_s[...] += part
            dr_ref[:, rows] += jnp.sum(ds_t, axis=0, keepdims=True)
            if diag:
                dq_ref[rows, :] = (dq_ref[rows, :] + dq_c) * FOX_SCALE
            else:
                dq_ref[rows, :] += dq_c

        tile(j, 1, True)
        odd = (nb - 1 - j) % 2

        @pl.when(odd == 1)
        def _():
            tile(j + 1, 1, False)

        lax.fori_loop(0, (nb - 1 - j) // 2, lambda t, c: (tile(j + 1 + odd + 2 * t, 2, False), c)[1], 0)
        dk_ref[...] = (dk_s[...] * FOX_SCALE).astype(BF16)
        dv_ref[...] = dv_s[...].astype(BF16)
        dc_ref[...] = jnp.sum(dc_s[...], axis=1, keepdims=True)

    head = lambda h, j: (0, h)
    row = pl.BlockSpec((None, 1, s_len), lambda h, j: (h, 0, 0))
    return _pcall(
        body,
        name="fox_bwd",
        grid=(FOX_H, nb),
        out_shape=[
            jax.ShapeDtypeStruct((s_len, FOX_W), F32),
            jax.ShapeDtypeStruct((s_len, FOX_W), BF16),
            jax.ShapeDtypeStruct((s_len, FOX_W), BF16),
            jax.ShapeDtypeStruct((FOX_H, s_len, 1), F32),
            jax.ShapeDtypeStruct((FOX_H, 1, s_len), F32),
        ],
        in_specs=[
            pl.BlockSpec((s_len, FOX_DH), head),
            pl.BlockSpec((blk, FOX_DH), lambda h, j: (j, FOX_H + h)),
            pl.BlockSpec((blk, FOX_DH), lambda h, j: (j, 2 * FOX_H + h)),
            pl.BlockSpec((None, blk, 1), lambda h, j: (h, j, 0)),
            row,
            row,
            pl.BlockSpec((s_len, FOX_DH), head),
        ],
        out_specs=[
            pl.BlockSpec((s_len, FOX_DH), head),
            pl.BlockSpec((blk, FOX_DH), lambda h, j: (j, h)),
            pl.BlockSpec((blk, FOX_DH), lambda h, j: (j, h)),
            pl.BlockSpec((None, blk, 1), lambda h, j: (h, j, 0)),
            row,
        ],
        scratch_shapes=[
            pltpu.VMEM((blk, FOX_DH), F32),
            pltpu.VMEM((blk, FOX_DH), F32),
            pltpu.VMEM((blk, 128), F32),
            pltpu.VMEM((blk, 2 * blk), F32),
        ],
        compiler_params=_cp("parallel", "arbitrary"),
    )(qkv, qkv, qkv, cum_col, lse_row, delta_row, do)


def _swa_bias():
    rows = SWA_G * WINDOW
    r = np.arange(rows)[:, None]
    c = np.arange(2 * WINDOW)[None, :]
    dist = (r % WINDOW) - c + WINDOW
    valid = (dist >= 0) & (dist < WINDOW)
    out = np.empty((SWA_HKV, rows, 2 * WINDOW), np.float32)
    for g in range(SWA_HKV):
        slope = np.array([SLOPES[g * SWA_G + t] for t in range(SWA_G)], np.float32)[r // WINDOW]
        out[g] = np.where(valid, -(slope * dist.astype(np.float32)), np.float32(NEG))
    return jnp.asarray(out)


def _swa_group(i, q_ref, kk, sinks_ref, bias_ref, g):
    rows = SWA_G * WINDOW
    c = lax.broadcasted_iota(jnp.int32, (1, 2 * WINDOW), 1)
    no_prev = jnp.where((c < WINDOW) & (i == 0), NEG, 0.0)
    head = lax.broadcasted_iota(jnp.int32, (rows, 1), 0) // WINDOW
    sink = jnp.zeros((rows, 1), F32)
    for t in range(SWA_G):
        sink = jnp.where(head == t, sinks_ref[g * SWA_G + t], sink)
    q = jnp.concatenate([q_ref[:, (g * SWA_G + t) * SWA_DH : (g * SWA_G + t + 1) * SWA_DH] for t in range(SWA_G)], axis=0)
    k = kk[:, g * SWA_DH : (g + 1) * SWA_DH]
    s = lax.dot_general(q, k, _NT, preferred_element_type=F32) * SWA_SCALE + bias_ref[g] + no_prev
    m = jnp.maximum(jnp.max(s, axis=1, keepdims=True), sink)
    e = jnp.exp(s - m)
    e_sink = jnp.exp(sink - m)
    inv = 1.0 / (jnp.sum(e, axis=1, keepdims=True) + e_sink)
    return q, k, e * inv, e_sink * inv


def _swa_specs(col_q, col_k, col_v, rev, nb):
    def blk(t):
        return nb - 1 - t if rev else t

    return [
        pl.BlockSpec((WINDOW, SWA_W), lambda t: (blk(t), col_q)),
        pl.BlockSpec((WINDOW, SWA_KVW), lambda t: (jnp.maximum(blk(t) - 1, 0), col_k)),
        pl.BlockSpec((WINDOW, SWA_KVW), lambda t: (blk(t), col_k)),
        pl.BlockSpec((WINDOW, SWA_KVW), lambda t: (jnp.maximum(blk(t) - 1, 0), col_v)),
        pl.BlockSpec((WINDOW, SWA_KVW), lambda t: (blk(t), col_v)),
    ]


def _swa_fwd(qkv, sinks):
    s_len = qkv.shape[0]
    nb = s_len // WINDOW
    bias_spec = pl.BlockSpec((SWA_HKV, SWA_G * WINDOW, 2 * WINDOW), lambda t: (0, 0, 0))

    def body(q_ref, kp_ref, kc_ref, vp_ref, vc_ref, sinks_ref, bias_ref, o_ref):
        i = pl.program_id(0)
        kk = jnp.concatenate([kp_ref[...], kc_ref[...]], axis=0)
        vv = jnp.concatenate([vp_ref[...], vc_ref[...]], axis=0)
        for g in range(SWA_HKV):
            _, _, p, _ = _swa_group(i, q_ref, kk, sinks_ref, bias_ref, g)
            o = jnp.dot(p.astype(BF16), vv[:, g * SWA_DH : (g + 1) * SWA_DH], preferred_element_type=F32)
            for t in range(SWA_G):
                h = g * SWA_G + t
                o_ref[:, h * SWA_DH : (h + 1) * SWA_DH] = o[t * WINDOW : (t + 1) * WINDOW, :]

    return _pcall(
        body,
        name="swa_fwd",
        grid=(nb,),
        out_shape=jax.ShapeDtypeStruct((s_len, SWA_W), F32),
        in_specs=_swa_specs(0, 4, 5, False, nb) + [pl.BlockSpec(memory_space=pltpu.SMEM), bias_spec],
        out_specs=pl.BlockSpec((WINDOW, SWA_W), lambda t: (t, 0)),
        compiler_params=_cp("parallel"),
    )(qkv, qkv, qkv, qkv, qkv, sinks, _swa_bias())


def _swa_bwd(qkv, sinks, do):
    s_len = qkv.shape[0]
    nb = s_len // WINDOW
    bias_spec = pl.BlockSpec((SWA_HKV, SWA_G * WINDOW, 2 * WINDOW), lambda t: (0, 0, 0))

    def body(q_ref, kp_ref, kc_ref, vp_ref, vc_ref, sinks_ref, bias_ref, do_ref, dq_ref, dk_ref, dv_ref, dsink_ref, ck_s, cv_s, dkk_s, dvv_s):
        t = pl.program_id(0)
        i = nb - 1 - t

        @pl.when(t == 0)
        def _():
            ck_s[...] = jnp.zeros(ck_s.shape, F32)
            cv_s[...] = jnp.zeros(cv_s.shape, F32)
            dsink_ref[...] = jnp.zeros(dsink_ref.shape, F32)

        kk = jnp.concatenate([kp_ref[...], kc_ref[...]], axis=0)
        vv = jnp.concatenate([vp_ref[...], vc_ref[...]], axis=0)
        lane = lax.broadcasted_iota(jnp.int32, (1, 128), 1)
        dsink = jnp.zeros((1, 128), F32)
        for g in range(SWA_HKV):
            cols = slice(g * SWA_DH, (g + 1) * SWA_DH)
            q, k, p, p_sink = _swa_group(i, q_ref, kk, sinks_ref, bias_ref, g)
            dob = jnp.concatenate([do_ref[:, (g * SWA_G + t) * SWA_DH : (g * SWA_G + t + 1) * SWA_DH] for t in range(SWA_G)], axis=0)
            dp = lax.dot_general(dob, vv[:, cols], _NT, preferred_element_type=F32)
            delta = jnp.sum(p * dp, axis=1, keepdims=True)
            dsb = (p * (dp - delta)).astype(BF16)
            dq = (jnp.dot(dsb, k, preferred_element_type=F32) * SWA_SCALE).astype(BF16)
            ps_d = p_sink * delta
            for t in range(SWA_G):
                h = g * SWA_G + t
                dq_ref[:, h * SWA_DH : (h + 1) * SWA_DH] = dq[t * WINDOW : (t + 1) * WINDOW, :]
                dsink = dsink + jnp.where(lane == h, -jnp.sum(ps_d[t * WINDOW : (t + 1) * WINDOW, :], axis=0, keepdims=True), 0.0)
            dkk_s[:, cols] = lax.dot_general(dsb, q, _TN, preferred_element_type=F32) * SWA_SCALE
            dvv_s[:, cols] = lax.dot_general(p.astype(BF16), dob, _TN, preferred_element_type=F32)
        dk_ref[...] = (dkk_s[WINDOW:, :] + ck_s[...]).astype(BF16)
        dv_ref[...] = (dvv_s[WINDOW:, :] + cv_s[...]).astype(BF16)
        ck_s[...] = dkk_s[:WINDOW, :]
        cv_s[...] = dvv_s[:WINDOW, :]
        dsink_ref[...] += dsink

    row = lambda t: (nb - 1 - t, 0)
    return _pcall(
        body,
        name="swa_bwd",
        grid=(nb,),
        out_shape=[
            jax.ShapeDtypeStruct((s_len, SWA_W), BF16),
            jax.ShapeDtypeStruct((s_len, SWA_KVW), BF16),
            jax.ShapeDtypeStruct((s_len, SWA_KVW), BF16),
            jax.ShapeDtypeStruct((1, 128), F32),
        ],
        in_specs=_swa_specs(0, 4, 5, True, nb)
        + [pl.BlockSpec(memory_space=pltpu.SMEM), bias_spec, pl.BlockSpec((WINDOW, SWA_W), row)],
        out_specs=[
            pl.BlockSpec((WINDOW, SWA_W), row),
            pl.BlockSpec((WINDOW, SWA_KVW), row),
            pl.BlockSpec((WINDOW, SWA_KVW), row),
            pl.BlockSpec((1, 128), lambda t: (0, 0)),
        ],
        scratch_shapes=[
            pltpu.VMEM((WINDOW, SWA_KVW), F32),
            pltpu.VMEM((WINDOW, SWA_KVW), F32),
            pltpu.VMEM((2 * WINDOW, SWA_KVW), F32),
            pltpu.VMEM((2 * WINDOW, SWA_KVW), F32),
        ],
        compiler_params=_cp("arbitrary"),
    )(qkv, qkv, qkv, qkv, qkv, sinks, _swa_bias(), do)


def _branch_fwd(o, gates, g_blk, w_b, name):
    s_len, wd = o.shape
    d = w_b.shape[1]
    tm = min(512, s_len)

    def body(o_ref, g_ref, w_ref, y_ref, a_ref):
        g = g_ref[...].astype(F32)
        a = (o_ref[...] * (g * _sigmoid(g))).astype(BF16)
        a_ref[...] = a
        y_ref[...] = jnp.dot(a, w_ref[...], preferred_element_type=F32).astype(BF16)

    return _pcall(
        body,
        name=name,
        grid=(s_len // tm,),
        out_shape=[jax.ShapeDtypeStruct((s_len, d), BF16), jax.ShapeDtypeStruct((s_len, wd), BF16)],
        in_specs=[
            pl.BlockSpec((tm, wd), lambda i: (i, 0)),
            pl.BlockSpec((tm, wd), lambda i: (i, g_blk)),
            pl.BlockSpec((wd, d), lambda i: (0, 0)),
        ],
        out_specs=[pl.BlockSpec((tm, d), lambda i: (i, 0)), pl.BlockSpec((tm, wd), lambda i: (i, 0))],
        compiler_params=_cp("parallel"),
    )(o, gates, w_b)


def _out_stage(gates, mf_blk, y_fox, y_swa, w_out, x, ada, ln_g, ln_b, target):
    s_len, d = x.shape
    tm = min(256, s_len)
    n_steps = s_len // tm

    def body(mf_ref, ms_ref, yf_ref, ys_ref, w_ref, x_ref, gate_ref, lg_ref, lb_ref, t_ref, mg_ref, dza_ref, dsub_ref, red_ref):
        i = pl.program_id(0)
        merged = _sigmoid(mf_ref[...].astype(F32)) * yf_ref[...].astype(F32) + _sigmoid(ms_ref[...].astype(F32)) * ys_ref[...].astype(F32)
        mb = merged.astype(BF16)
        mg_ref[...] = mb
        sub = jnp.dot(mb, w_ref[...], preferred_element_type=F32)
        gate = gate_ref[...]
        z = ALPHA * x_ref[...] + gate * sub
        mu = jnp.mean(z, axis=-1, keepdims=True)
        zc = z - mu
        var = jnp.mean(zc * zc, axis=-1, keepdims=True)
        rstd = lax.rsqrt(var + LN_EPS)
        zhat = zc * rstd
        err = zhat * lg_ref[...] + lb_ref[...] - t_ref[...]
        dout = err * (1.0 / d)
        dzhat = dout * lg_ref[...]
        dz = rstd * (dzhat - jnp.mean(dzhat, axis=-1, keepdims=True) - zhat * jnp.mean(dzhat * zhat, axis=-1, keepdims=True))
        dza_ref[...] = ALPHA * dz
        dsub_ref[...] = (gate * dz).astype(BF16)
        part = jnp.concatenate(
            [
                jnp.sum(dz * sub, axis=0, keepdims=True),
                jnp.sum(dout * zhat, axis=0, keepdims=True),
                jnp.sum(dout, axis=0, keepdims=True),
                jnp.sum(err * err, axis=0, keepdims=True),
                jnp.zeros((4, d), F32),
            ],
            axis=0,
        )

        @pl.when(i == 0)
        def _():
            red_ref[...] = part

        @pl.when(i > 0)
        def _():
            red_ref[...] += part

        @pl.when(i == n_steps - 1)
        def _():
            red_ref[4:5, :] = jnp.broadcast_to(jnp.sum(red_ref[3:4, :], axis=1, keepdims=True), (1, d))

    row = pl.BlockSpec((tm, d), lambda i: (i, 0))
    vec = pl.BlockSpec((1, d), lambda i: (0, 0))
    return _pcall(
        body,
        name="out_stage",
        grid=(n_steps,),
        out_shape=[
            jax.ShapeDtypeStruct((s_len, d), BF16),
            jax.ShapeDtypeStruct((s_len, d), F32),
            jax.ShapeDtypeStruct((s_len, d), BF16),
            jax.ShapeDtypeStruct((8, d), F32),
        ],
        in_specs=[
            pl.BlockSpec((tm, d), lambda i: (i, mf_blk)),
            pl.BlockSpec((tm, d), lambda i: (i, mf_blk + 1)),
            row,
            row,
            pl.BlockSpec((d, d), lambda i: (0, 0), pipeline_mode=pl.Buffered(1)),
            row,
            pl.BlockSpec((1, d), lambda i: (0, 2)),
            vec,
            vec,
            row,
        ],
        out_specs=[row, row, row, pl.BlockSpec((8, d), lambda i: (0, 0))],
        compiler_params=_cp("arbitrary"),
    )(gates, gates, y_fox, y_swa, w_out, x, ada, ln_g, ln_b, target)


def _merge_bwd(dsub, w_out, gates, mf_blk, y_fox, y_swa):
    s_len, d = dsub.shape
    tm = min(256, s_len)

    def body(ds_ref, w_ref, mf_ref, ms_ref, yf_ref, ys_ref, dmf_ref, dms_ref, dyf_ref, dys_ref):
        dm = lax.dot_general(ds_ref[...], w_ref[...], _NT, preferred_element_type=F32)
        sf, ss = _sigmoid(mf_ref[...].astype(F32)), _sigmoid(ms_ref[...].astype(F32))
        dmf_ref[...] = (dm * yf_ref[...].astype(F32) * (sf * (1.0 - sf))).astype(BF16)
        dms_ref[...] = (dm * ys_ref[...].astype(F32) * (ss * (1.0 - ss))).astype(BF16)
        dyf_ref[...] = (dm * sf).astype(BF16)
        dys_ref[...] = (dm * ss).astype(BF16)

    row = pl.BlockSpec((tm, d), lambda i: (i, 0))
    return _pcall(
        body,
        name="merge_bwd",
        grid=(s_len // tm,),
        out_shape=[jax.ShapeDtypeStruct((s_len, d), BF16)] * 4,
        in_specs=[
            row,
            pl.BlockSpec((d, d), lambda i: (0, 0), pipeline_mode=pl.Buffered(1)),
            pl.BlockSpec((tm, d), lambda i: (i, mf_blk)),
            pl.BlockSpec((tm, d), lambda i: (i, mf_blk + 1)),
            row,
            row,
        ],
        out_specs=[row] * 4,
        compiler_params=_cp("parallel"),
    )(dsub, w_out, gates, gates, y_fox, y_swa)


def _branch_bwd(dy, w_b, o, gates, g_blk, name, n_heads):
    s_len, d = dy.shape
    wd = w_b.shape[0]
    tm = min(512, s_len)

    def body(dy_ref, w_ref, o_ref, g_ref, do_ref, dg_ref, *rest):
        da = lax.dot_general(dy_ref[...], w_ref[...], _NT, preferred_element_type=F32)
        g = g_ref[...].astype(F32)
        sg = _sigmoid(g)
        do = da * (g * sg)
        do_ref[...] = do.astype(BF16)
        o = o_ref[...]
        dg_ref[...] = (da * o * (sg * (1.0 + g * (1.0 - sg)))).astype(BF16)
        if n_heads:
            prod = do * o
            lane = lax.broadcasted_iota(jnp.int32, (1, 128), 1)
            delta = jnp.zeros((tm, 128), F32)
            for h in range(n_heads):
                dh = jnp.sum(prod[:, h * 128 : (h + 1) * 128], axis=1, keepdims=True)
                delta = delta + jnp.where(lane == h, dh, 0.0)
            rest[0][...] = delta

    out_shape = [jax.ShapeDtypeStruct((s_len, wd), BF16), jax.ShapeDtypeStruct((s_len, wd), BF16)]
    out_specs = [pl.BlockSpec((tm, wd), lambda i: (i, 0))] * 2
    if n_heads:
        out_shape.append(jax.ShapeDtypeStruct((s_len, 128), F32))
        out_specs.append(pl.BlockSpec((tm, 128), lambda i: (i, 0)))
    return _pcall(
        body,
        name=name,
        grid=(s_len // tm,),
        out_shape=out_shape,
        in_specs=[
            pl.BlockSpec((tm, d), lambda i: (i, 0)),
            pl.BlockSpec((wd, d), lambda i: (0, 0)),
            pl.BlockSpec((tm, wd), lambda i: (i, 0)),
            pl.BlockSpec((tm, wd), lambda i: (i, g_blk)),
        ],
        out_specs=out_specs,
        compiler_params=_cp("parallel"),
    )(dy, w_b, o, gates)


def _in_bwd(dproj, w_in_t, x, ada, dza, ride):
    s_len, d = x.shape
    k_tot = dproj.shape[1]
    tm, tk = min(512, s_len), 1024
    ni, nk = s_len // tm, k_tot // tk
    n = len(ride)

    def body(dp_ref, w_ref, x_ref, sc_ref, dza_ref, *rest):
        ins, (gx_ref, red_ref), outs = rest[:n], rest[n : n + 2], rest[n + 2 : 2 * n + 2]
        sems, acc_s = rest[2 * n + 2 : 2 * n + 5], rest[2 * n + 5]
        i, kk = pl.program_id(0), pl.program_id(1)

        @pl.when((i == 0) & (kk == 0))
        def _():
            _rider_start("exchange", ins, outs, *sems)

        @pl.when((i == ni - 1) & (kk == nk - 1))
        def _():
            _rider_wait("exchange", ins, outs, *sems)

        part = jnp.dot(dp_ref[...], w_ref[...], preferred_element_type=F32)

        @pl.when(kk == 0)
        def _():
            acc_s[...] = part

        @pl.when(kk > 0)
        def _():
            acc_s[...] += part

        @pl.when(kk == nk - 1)
        def _():
            dh = acc_s[...]
            xv = x_ref[...]
            mu = jnp.mean(xv, axis=-1, keepdims=True)
            xc = xv - mu
            var = jnp.mean(xc * xc, axis=-1, keepdims=True)
            rstd = lax.rsqrt(var + LN_EPS)
            xhat = xc * rstd
            dxhat = dh * (1.0 + sc_ref[...])
            dx = rstd * (dxhat - jnp.mean(dxhat, axis=-1, keepdims=True) - xhat * jnp.mean(dxhat * xhat, axis=-1, keepdims=True))
            gx_ref[...] = dza_ref[...] + dx
            part_r = jnp.concatenate(
                [jnp.sum(dh, axis=0, keepdims=True), jnp.sum(dh * xhat, axis=0, keepdims=True), jnp.zeros((6, d), F32)], axis=0
            )

            @pl.when(i == 0)
            def _():
                red_ref[...] = part_r

            @pl.when(i > 0)
            def _():
                red_ref[...] += part_r

    row = pl.BlockSpec((tm, d), lambda i, kk: (i, 0))
    hbm = pl.BlockSpec(memory_space=pltpu.HBM)
    return _pcall(
        body,
        name="in_bwd",
        grid=(ni, nk),
        out_shape=[jax.ShapeDtypeStruct((s_len, d), F32), jax.ShapeDtypeStruct((8, d), F32)]
        + [jax.ShapeDtypeStruct(r.shape, r.dtype) for r in ride],
        in_specs=[
            pl.BlockSpec((tm, tk), lambda i, kk: (i, kk)),
            pl.BlockSpec((tk, d), lambda i, kk: (kk, 0)),
            row,
            pl.BlockSpec((1, d), lambda i, kk: (0, 1)),
            row,
        ]
        + [hbm] * n,
        out_specs=[row, pl.BlockSpec((8, d), lambda i, kk: (0, 0))] + [hbm] * n,
        scratch_shapes=_rider_scratch(n) + [pltpu.VMEM((tm, d), F32)],
        compiler_params=_cp("arbitrary", "arbitrary"),
    )(dproj, w_in_t, x, ada, dza, *ride)


def _pad_lanes(v, n):
    return jnp.pad(v, ((0, 0), (0, n - v.shape[1])))


def kernel(x, c, w_ada, b_ada, w_in, b_f, attn_sinks, w_br_fox, w_br_swa, w_out, ln_g, ln_b, loss_target, m_w_ada, m_b_ada, m_w_in, m_b_f, m_attn_sinks, m_w_br_fox, m_w_br_swa, m_w_out, m_ln_g, m_ln_b, v_w_ada, v_b_ada, v_w_in, v_b_f, v_attn_sinks, v_w_br_fox, v_w_br_swa, v_w_out, v_ln_g, v_ln_b):
    x2, tgt = x[0], loss_target[0]
    s_len, d = x2.shape
    me = 4 * lax.axis_index("x") + 2 * lax.axis_index("y") + lax.axis_index("c")
    off_ms = OFF_MF + d
    in_pad = off_ms + d
    c_ada = w_ada.shape[2]
    c_in = w_in.shape[2]
    c_br = w_br_fox.shape[2]

    w_in_g = _all_gather(w_in[0].T.astype(BF16), "ag_w_in", pltpu.HBM)
    k_cut, r_cut = divmod(REAL_FLOG_END, c_in)
    w_in_pad = jnp.concatenate(
        [w_in_g[k] for k in range(k_cut)]
        + [w_in_g[k_cut, :r_cut], jnp.zeros((FLOG_PAD - N_FLOG, d), BF16), w_in_g[k_cut, r_cut:]]
        + [w_in_g[k] for k in range(k_cut + 1, N_DEV)],
        axis=0,
    )

    c_all = _gather_rows(c, "ag_c")
    b_cols = lax.dynamic_slice(b_ada, (0, me * c_ada), (1, c_ada))
    ada_cols = _ada_fwd(c_all, w_ada[0], b_cols)
    ada_g = _all_gather(ada_cols, "ag_ada", pltpu.VMEM)
    ada = lax.dynamic_index_in_dim(ada_g, me, axis=1, keepdims=False).reshape(1, N_DEV * c_ada)

    h = _ln_mod(x2, ada)
    qkv_fox = _mm_cols(h, w_in_pad, OFF_FQ, 3 * FOX_W, BF16, "proj_fox")
    flog = _mm_cols(h, w_in_pad, OFF_FLOG, FLOG_PAD, F32, "proj_flog")
    qkv_swa = _mm_cols(h, w_in_pad, OFF_SQ, SWA_W + 2 * SWA_KVW, BF16, "proj_swa")
    gates, w_bf, w_bs, w_o = _mm_cols(
        h, w_in_pad, OFF_GF, in_pad - OFF_GF, BF16, "proj_gates",
        ride=(w_br_fox[0].astype(BF16), w_br_swa[0].astype(BF16), w_out[0].astype(BF16)),
    )
    w_bf = w_bf.reshape(N_DEV, FOX_W, c_br).transpose(1, 0, 2).reshape(FOX_W, d)
    w_bs = w_bs.reshape(N_DEV, SWA_W, c_br).transpose(1, 0, 2).reshape(SWA_W, d)
    w_o = w_o.reshape(d, d)
    mf_blk = (OFF_MF - OFF_GF) // d

    flog_t = flog[:, :N_FLOG].T
    bf_col = b_f.reshape(FOX_H, 1)
    cum = _fox_cum(flog_t, bf_col)
    cum_row = cum.reshape(FOX_H, 1, s_len)
    o_fox, lse = _fox_fwd(qkv_fox, cum_row)
    sinks = attn_sinks.reshape(SWA_HQ)
    o_swa = _swa_fwd(qkv_swa, sinks)

    y_fox, a_fox = _branch_fwd(o_fox, gates, 0, w_bf, "branch_fox")
    y_swa, a_swa = _branch_fwd(o_swa, gates, 1, w_bs, "branch_swa")
    merged, dza, dsub, red = _out_stage(gates, mf_blk, y_fox, y_swa, w_o, x2, ada, ln_g, ln_b, tgt)
    loss = lax.psum(0.5 * red[4, 0] / d, ("x", "y", "c"))

    dmf, dms, dy_fox, dy_swa = _merge_bwd(dsub, w_o, gates, mf_blk, y_fox, y_swa)
    do_fox, dg_fox, delta = _branch_bwd(dy_fox, w_bf, o_fox, gates, 0, "branch_fox_bwd", FOX_H)
    do_swa, dg_swa = _branch_bwd(dy_swa, w_bs, o_swa, gates, 1, "branch_swa_bwd", 0)
    delta_row = delta[:, :FOX_H].T.reshape(FOX_H, 1, s_len)
    dq_f, dk_f, dv_f, dcol, drow = _fox_bwd(
        qkv_fox, cum.reshape(FOX_H, s_len, 1), lse.reshape(FOX_H, 1, s_len), delta_row, do_fox
    )
    dflog_t, dbf = _fox_gate_bwd(drow.reshape(FOX_H, s_len), dcol.reshape(FOX_H, s_len), flog_t, bf_col)
    dq_s, dk_s, dv_s, dsink = _swa_bwd(qkv_swa, sinks, do_swa)
    dflog = _pad_lanes(dflog_t.T, FLOG_PAD).astype(BF16)
    dproj = jnp.concatenate([dq_f.astype(BF16), dk_f, dv_f, dflog, dq_s, dk_s, dv_s, dg_fox, dg_swa, dmf, dms], axis=1)
    g_w_bf = _mm_tn(a_fox, dy_fox, "grad_w_br_fox")
    g_w_bs = _mm_tn(a_swa, dy_swa, "grad_w_br_swa")
    g_w_o = _mm_tn(merged, dsub, "grad_w_out")
    g_w_in, r_bf, r_bs, r_o = _mm_tn(
        dproj, h, "grad_w_in",
        ride=(
            g_w_bf.reshape(FOX_W, N_DEV, c_br).transpose(1, 0, 2),
            g_w_bs.reshape(SWA_W, N_DEV, c_br).transpose(1, 0, 2),
            g_w_o.reshape(N_DEV, d // N_DEV, d),
        ),
    )
    pad = FLOG_PAD - N_FLOG
    g_blocks = jnp.stack(
        [g_w_in[k * c_in : (k + 1) * c_in] for k in range(k_cut)]
        + [jnp.concatenate([g_w_in[k_cut * c_in : REAL_FLOG_END], g_w_in[OFF_SQ : (k_cut + 1) * c_in + pad]], axis=0)]
        + [g_w_in[k * c_in + pad : (k + 1) * c_in + pad] for k in range(k_cut + 1, N_DEV)]
    )

    grad_x, red2, r_in = _in_bwd(dproj, w_in_pad, x2, ada, dza, ride=(g_blocks,))
    out_w_in = _sum_adam_t(r_in, w_in[0].T, m_w_in[0].T, v_w_in[0].T, "adam_w_in")
    out_w_in = [o.T for o in out_w_in]
    out_w_bf = _sum_adam(r_bf, w_br_fox[0], m_w_br_fox[0], v_w_br_fox[0], "adam_w_br_fox")
    out_w_bs = _sum_adam(r_bs, w_br_swa[0], m_w_br_swa[0], v_w_br_swa[0], "adam_w_br_swa")
    out_w_o = _sum_adam(r_o, w_out[0], m_w_out[0], v_w_out[0], "adam_w_out")

    packed = jnp.concatenate([red2[0:1], red2[1:2], red[0:1], _pad_lanes(dbf[:, 0].reshape(1, FOX_H), 128), dsink, red[1:2], red[2:3]], axis=1)
    gathered = _gather_rows(packed, "ag_small")
    pack = lambda a, b, cc, dd, e: jnp.concatenate([a, _pad_lanes(b, 128), _pad_lanes(cc, 128), dd, e], axis=1)
    small = _small_adam(
        gathered,
        pack(b_ada, b_f, attn_sinks, ln_g, ln_b),
        pack(m_b_ada, m_b_f, m_attn_sinks, m_ln_g, m_ln_b),
        pack(v_b_ada, v_b_f, v_attn_sinks, v_ln_g, v_ln_b),
    )
    dada_cols = lax.dynamic_slice(gathered, (0, me * c_ada), (N_DEV, c_ada))
    out_w_ada = _wada_adam(c_all.T, dada_cols, w_ada[0], m_w_ada[0], v_w_ada[0])

    o1, o2, o3 = 3 * d, 3 * d + 128, 3 * d + 256

    def unpack(p):
        return p[:, :o1], p[:, o1 : o1 + FOX_H], p[:, o2 : o2 + SWA_HQ], p[:, o3 : o3 + d], p[:, o3 + d : o3 + 2 * d]

    kinds = []
    for k in range(4):
        b_ada_k, b_f_k, sinks_k, ln_g_k, ln_b_k = unpack(small[k])
        kinds.append(
            [out_w_ada[k][None], b_ada_k, out_w_in[k][None], b_f_k, sinks_k, out_w_bf[k][None], out_w_bs[k][None], out_w_o[k][None], ln_g_k, ln_b_k]
        )
    return (loss, grad_x[None], *kinds[0], *kinds[1], *kinds[2], *kinds[3])
```

```python
import numpy as np
import jax
import jax.numpy as jnp
from jax import lax
from jax.experimental import pallas as pl
from jax.experimental.pallas import tpu as pltpu

F32 = jnp.float32
BF16 = jnp.bfloat16
N_DEV = 8
MESH = pl.DeviceIdType.MESH

FOX_H, FOX_DH, FOX_W = 8, 128, 1024
SWA_HQ, SWA_HKV, SWA_DH, SWA_G = 16, 4, 64, 4
SWA_W, SWA_KVW, WINDOW = 1024, 256, 128
LN_EPS = 1e-5
NEG = -1e30
DEPTH = 1
ALPHA = (2.0 * DEPTH) ** 0.25
FOX_SCALE = FOX_DH ** -0.5
SWA_SCALE = SWA_DH ** -0.5
SLOPES = [2.0 ** (-8.0 * (h + 1.0) / SWA_HQ) for h in range(SWA_HQ)]

ADAM_LR, ADAM_B1, ADAM_B2, ADAM_EPS, ADAM_WD, ADAM_STEP = 0.001, 0.9, 0.999, 1e-08, 0.01, 10

N_FLOG = 8
FLOG_PAD = 512
OFF_FQ, OFF_FK, OFF_FV, OFF_FLOG = 0, 1024, 2048, 3072
OFF_SQ = OFF_FLOG + FLOG_PAD
OFF_SK = OFF_SQ + SWA_W
OFF_SV = OFF_SK + SWA_KVW
OFF_GF = OFF_SV + SWA_KVW
OFF_GS = OFF_GF + FOX_W
OFF_MF = OFF_GS + SWA_W
REAL_FLOG_END = OFF_FLOG + N_FLOG

ATT_BLK = 512
VMEM_LIMIT = 52 * 1024 * 1024


def _pcall(body, **kw):
    return pl.pallas_call(body, **kw)


def _cp(*sem):
    return pltpu.CompilerParams(dimension_semantics=sem, vmem_limit_bytes=VMEM_LIMIT)


def _sigmoid(x):
    return 0.5 * jnp.tanh(0.5 * x) + 0.5


def _all_gather(x, name, space):
    m_per, n = x.shape

    def body(x_ref, out_ref, send_sems, recv_sems, local_sem):
        mx, my, mc = lax.axis_index("x"), lax.axis_index("y"), lax.axis_index("c")
        me, sibling = (mx, my, mc), (mx, my, 1 - mc)
        chips = [(1 - mx, my), (mx, 1 - my), (1 - mx, 1 - my)]

        def rows(px, py, pc):
            return out_ref.at[4 * px + 2 * py + pc]

        def copy(k, block, to, src=None):
            return pltpu.make_async_remote_copy(
                src_ref=rows(*block) if src is None else src,
                dst_ref=rows(*block),
                send_sem=send_sems.at[k],
                recv_sem=recv_sems.at[k],
                device_id=to,
                device_id_type=MESH,
            )

        mine = pltpu.make_async_copy(x_ref, rows(*me), local_sem)
        mine.start()
        first = [copy(0, me, sibling, src=x_ref)]
        first += [copy(1 + j, me, (*chip, mc), src=x_ref) for j, chip in enumerate(chips)]
        for cp in first:
            cp.start()
        passed = [copy(4 + j, (*chip, mc), sibling) for j, chip in enumerate(chips)]
        for j, chip in enumerate(chips):
            copy(1 + j, (*chip, mc), me).wait_recv()
            passed[j].start()
        copy(0, sibling, me).wait_recv()
        for j, chip in enumerate(chips):
            copy(4 + j, (*chip, 1 - mc), me).wait_recv()
        for cp in first + passed:
            cp.wait_send()
        mine.wait()

    return _pcall(
        body,
        name=name,
        out_shape=jax.ShapeDtypeStruct((N_DEV, m_per, n), x.dtype),
        in_specs=[pl.BlockSpec(memory_space=space)],
        out_specs=pl.BlockSpec(memory_space=space),
        scratch_shapes=[pltpu.SemaphoreType.DMA((7,)), pltpu.SemaphoreType.DMA((7,)), pltpu.SemaphoreType.DMA],
    )(x)


def _peer(d, mx, my, mc):
    return (1 - mx if (d >> 2) & 1 else mx, 1 - my if (d >> 1) & 1 else my, 1 - mc if d & 1 else mc)


def _rider_copies(kind, ins, outs, send_sems, recv_sems, local_sems):
    mx, my, mc = lax.axis_index("x"), lax.axis_index("y"), lax.axis_index("c")
    me = 4 * mx + 2 * my + mc
    remote, local = [], []
    for a in range(len(ins)):
        if kind == "gather":
            m_per = ins[a].shape[0]
            mine = outs[a].at[pl.ds(me * m_per, m_per), :]
            local.append(pltpu.make_async_copy(ins[a], mine, local_sems.at[a]))
        else:
            local.append(pltpu.make_async_copy(ins[a].at[me], outs[a].at[0], local_sems.at[a]))
        for d in range(1, N_DEV):
            px, py, pc = _peer(d, mx, my, mc)
            if kind == "gather":
                src, dst = ins[a], mine
            else:
                src, dst = ins[a].at[4 * px + 2 * py + pc], outs[a].at[d]
            remote.append(
                pltpu.make_async_remote_copy(
                    src_ref=src,
                    dst_ref=dst,
                    send_sem=send_sems.at[a * 7 + d - 1],
                    recv_sem=recv_sems.at[a * 7 + d - 1],
                    device_id=(px, py, pc),
                    device_id_type=MESH,
                )
            )
    return remote, local


def _rider_start(*args):
    remote, local = _rider_copies(*args)
    for cp in local + remote:
        cp.start()


def _rider_wait(*args):
    remote, local = _rider_copies(*args)
    for cp in remote:
        cp.wait_recv()
    for cp in remote:
        cp.wait_send()
    for cp in local:
        cp.wait()


def _rider_scratch(n):
    return [pltpu.SemaphoreType.DMA((7 * n,)), pltpu.SemaphoreType.DMA((7 * n,)), pltpu.SemaphoreType.DMA((n,))]


def _gather_rows(v, name):
    n = v.shape[1]
    return _all_gather(jnp.broadcast_to(v, (8, n)), name, pltpu.VMEM)[:, 0, :]


def _adamw(w, g, m, v):
    m = ADAM_B1 * m + (1.0 - ADAM_B1) * g
    v = ADAM_B2 * v + (1.0 - ADAM_B2) * (g * g)
    m_hat = m / (1.0 - ADAM_B1**ADAM_STEP)
    v_hat = v / (1.0 - ADAM_B2**ADAM_STEP)
    delta = -ADAM_LR * (m_hat / (jnp.sqrt(v_hat) + ADAM_EPS) + ADAM_WD * w)
    return delta, m, v


def _sum_adam(recv, w, m, v, name):
    _, r_tot, c = recv.shape
    c_pad = -(-c // 128) * 128
    tr = r_tot
    while 8 * tr * c_pad * 4 > 6 * 1024 * 1024 and tr % 32 == 0:
        tr //= 2

    def body(r_ref, w_ref, m_ref, v_ref, g_ref, d_ref, nm_ref, nv_ref):
        g = r_ref[0].astype(F32)
        for k in range(1, N_DEV):
            g = g + r_ref[k].astype(F32)
        d, nm, nv = _adamw(w_ref[...], g, m_ref[...], v_ref[...])
        g_ref[...] = g
        d_ref[...] = d
        nm_ref[...] = nm
        nv_ref[...] = nv

    blk = pl.BlockSpec((tr, c), lambda i: (i, 0))
    return _pcall(
        body,
        name=name,
        grid=(r_tot // tr,),
        out_shape=[jax.ShapeDtypeStruct((r_tot, c), F32)] * 4,
        in_specs=[pl.BlockSpec((N_DEV, tr, c), lambda i: (0, i, 0)), blk, blk, blk],
        out_specs=[blk] * 4,
        compiler_params=_cp("parallel"),
    )(recv, w, m, v)


def _sum_adam_t(recv, w, m, v, name):
    _, c, r_tot = recv.shape
    tr = min(256, r_tot)

    def body(r_ref, w_ref, m_ref, v_ref, g_ref, d_ref, nm_ref, nv_ref):
        g = r_ref[0].astype(F32)
        for k in range(1, N_DEV):
            g = g + r_ref[k].astype(F32)
        d, nm, nv = _adamw(w_ref[...], g, m_ref[...], v_ref[...])
        g_ref[...] = g
        d_ref[...] = d
        nm_ref[...] = nm
        nv_ref[...] = nv

    blk = pl.BlockSpec((c, tr), lambda i: (0, i))
    return _pcall(
        body,
        name=name,
        grid=(r_tot // tr,),
        out_shape=[jax.ShapeDtypeStruct((c, r_tot), F32)] * 4,
        in_specs=[pl.BlockSpec((N_DEV, c, tr), lambda i: (0, 0, i)), blk, blk, blk],
        out_specs=[blk] * 4,
        compiler_params=_cp("parallel"),
    )(recv, w, m, v)


def _wada_adam(c_t, dada_cols, w, m, v):
    d_model, c = w.shape
    tr = min(256, d_model)

    def body(ct_ref, da_ref, w_ref, m_ref, v_ref, g_ref, d_ref, nm_ref, nv_ref):
        g = jnp.dot(ct_ref[...].astype(BF16), da_ref[...].astype(BF16), preferred_element_type=F32)
        d, nm, nv = _adamw(w_ref[...], g, m_ref[...], v_ref[...])
        g_ref[...] = g
        d_ref[...] = d
        nm_ref[...] = nm
        nv_ref[...] = nv

    blk = pl.BlockSpec((tr, c), lambda i: (i, 0))
    return _pcall(
        body,
        name="wada_adam",
        grid=(d_model // tr,),
        out_shape=[jax.ShapeDtypeStruct((d_model, c), F32)] * 4,
        in_specs=[pl.BlockSpec((tr, N_DEV), lambda i: (i, 0)), pl.BlockSpec((N_DEV, c), lambda i: (0, 0)), blk, blk, blk],
        out_specs=[blk] * 4,
        compiler_params=_cp("parallel"),
    )(c_t, dada_cols, w, m, v)


def _small_adam(gathered, w, m, v):
    p = w.shape[1]

    def body(a_ref, w_ref, m_ref, v_ref, g_ref, d_ref, nm_ref, nv_ref):
        g = a_ref[0:1, :]
        for k in range(1, N_DEV):
            g = g + a_ref[k : k + 1, :]
        d, nm, nv = _adamw(w_ref[...], g, m_ref[...], v_ref[...])
        g_ref[...] = g
        d_ref[...] = d
        nm_ref[...] = nm
        nv_ref[...] = nv

    return _pcall(
        body,
        name="small_adam",
        out_shape=[jax.ShapeDtypeStruct((1, p), F32)] * 4,
    )(gathered, w, m, v)


def _ada_fwd(c_all, w_ada, b_cols):
    c = w_ada.shape[1]

    def body(c_ref, w_ref, b_ref, o_ref):
        o_ref[...] = jnp.dot(c_ref[...].astype(BF16), w_ref[...].astype(BF16), preferred_element_type=F32) + b_ref[...]

    return _pcall(
        body,
        name="ada_fwd",
        out_shape=jax.ShapeDtypeStruct((N_DEV, c), F32),
        compiler_params=_cp(),
    )(c_all, w_ada, b_cols)


def _ln_mod(x, ada):
    s_len, d = x.shape
    tm = min(512, s_len)

    def body(x_ref, sh_ref, sc_ref, h_ref):
        xv = x_ref[...]
        mu = jnp.mean(xv, axis=-1, keepdims=True)
        xc = xv - mu
        var = jnp.mean(xc * xc, axis=-1, keepdims=True)
        xhat = xc * lax.rsqrt(var + LN_EPS)
        h_ref[...] = (xhat * (1.0 + sc_ref[...]) + sh_ref[...]).astype(BF16)

    return _pcall(
        body,
        name="ln_mod",
        grid=(s_len // tm,),
        out_shape=jax.ShapeDtypeStruct((s_len, d), BF16),
        in_specs=[
            pl.BlockSpec((tm, d), lambda i: (i, 0)),
            pl.BlockSpec((1, d), lambda i: (0, 0)),
            pl.BlockSpec((1, d), lambda i: (0, 1)),
        ],
        out_specs=pl.BlockSpec((tm, d), lambda i: (i, 0)),
        compiler_params=_cp("parallel"),
    )(x, ada, ada)


def _mm_cols(a, b, col_off, n_cols, out_dtype, name, ride=()):
    m, k = a.shape
    tm, tn = min(1024, m), 512
    off = col_off // tn
    ni, nj = m // tm, n_cols // tn
    n = len(ride)

    def body(a_ref, b_ref, *rest):
        ins, o_ref, outs, sems = rest[:n], rest[n], rest[n + 1 : 2 * n + 1], rest[2 * n + 1 :]
        i, j = pl.program_id(0), pl.program_id(1)
        if n:

            @pl.when((i == 0) & (j == 0))
            def _():
                _rider_start("gather", ins, outs, *sems)

        o_ref[...] = lax.dot_general(a_ref[...], b_ref[...], _NT, preferred_element_type=F32).astype(out_dtype)
        if n:

            @pl.when((i == ni - 1) & (j == nj - 1))
            def _():
                _rider_wait("gather", ins, outs, *sems)

    hbm = pl.BlockSpec(memory_space=pltpu.HBM)
    out = _pcall(
        body,
        name=name,
        grid=(ni, nj),
        out_shape=[jax.ShapeDtypeStruct((m, n_cols), out_dtype)]
        + [jax.ShapeDtypeStruct((N_DEV * r.shape[0], r.shape[1]), r.dtype) for r in ride],
        in_specs=[pl.BlockSpec((tm, k), lambda i, j: (i, 0)), pl.BlockSpec((tn, k), lambda i, j: (off + j, 0))] + [hbm] * n,
        out_specs=[pl.BlockSpec((tm, tn), lambda i, j: (i, j))] + [hbm] * n,
        scratch_shapes=_rider_scratch(n) if n else [],
        compiler_params=_cp("arbitrary", "arbitrary") if n else _cp("parallel", "parallel"),
    )(a, b, *ride)
    return out if n else out[0]


def _mm_tn(a, b, name, ride=()):
    s_len, m = a.shape
    n = b.shape[1]
    tm, tn, ts = min(1024, m), min(1024, n), min(2048, s_len)
    ni, nj, ns = m // tm, n // tn, s_len // ts
    nr = len(ride)

    def body(a_ref, b_ref, *rest):
        ins, o_ref, outs = rest[:nr], rest[nr], rest[nr + 1 : 2 * nr + 1]
        sems, acc_s = rest[2 * nr + 1 : -1], rest[-1]
        i, j, kk = pl.program_id(0), pl.program_id(1), pl.program_id(2)
        if nr:

            @pl.when((i == 0) & (j == 0) & (kk == 0))
            def _():
                _rider_start("exchange", ins, outs, *sems)

            @pl.when((i == ni - 1) & (j == nj - 1) & (kk == ns - 1))
            def _():
                _rider_wait("exchange", ins, outs, *sems)

        part = lax.dot_general(a_ref[...], b_ref[...], _TN, preferred_element_type=F32)

        @pl.when(kk == 0)
        def _():
            acc_s[...] = part

        @pl.when(kk > 0)
        def _():
            acc_s[...] += part

        @pl.when(kk == ns - 1)
        def _():
            o_ref[...] = acc_s[...].astype(BF16)

    hbm = pl.BlockSpec(memory_space=pltpu.HBM)
    out = _pcall(
        body,
        name=name,
        grid=(ni, nj, ns),
        out_shape=[jax.ShapeDtypeStruct((m, n), BF16)] + [jax.ShapeDtypeStruct(r.shape, r.dtype) for r in ride],
        in_specs=[pl.BlockSpec((ts, tm), lambda i, j, kk: (kk, i)), pl.BlockSpec((ts, tn), lambda i, j, kk: (kk, j))] + [hbm] * nr,
        out_specs=[pl.BlockSpec((tm, tn), lambda i, j, kk: (i, j))] + [hbm] * nr,
        scratch_shapes=(_rider_scratch(nr) if nr else []) + [pltpu.VMEM((tm, tn), F32)],
        compiler_params=_cp("arbitrary", "arbitrary", "arbitrary") if nr else _cp("parallel", "parallel", "arbitrary"),
    )(a, b, *ride)
    return out if nr else out[0]


def _split3(a):
    hi = a.astype(BF16)
    r1 = a - hi.astype(F32)
    mid = r1.astype(BF16)
    lo = (r1 - mid.astype(F32)).astype(BF16)
    return hi, mid, lo


def _dot_ones(a, tri):
    return sum(jnp.dot(t, tri, preferred_element_type=F32) for t in _split3(a))


def _log_sigmoid(x):
    return jnp.minimum(x, 0.0) - jnp.log1p(jnp.exp(-jnp.abs(x)))


def _fox_cum(flog_t, bf_col):
    s_len = flog_t.shape[1]

    def body(fl_ref, bf_ref, cum_ref):
        r = lax.broadcasted_iota(jnp.int32, (128, 128), 0)
        c = lax.broadcasted_iota(jnp.int32, (128, 128), 1)
        upper = (r <= c).astype(BF16)

        def step(t, carry):
            sl = pl.ds(pl.multiple_of(t * 128, 128), 128)
            lf = _log_sigmoid(fl_ref[:, sl] + bf_ref[...])
            cs = _dot_ones(lf, upper) + carry
            cum_ref[:, sl] = cs
            return cs[:, 127:128]

        lax.fori_loop(0, s_len // 128, step, jnp.zeros((FOX_H, 1), F32))

    return _pcall(body, name="fox_cum", out_shape=jax.ShapeDtypeStruct((FOX_H, s_len), F32))(flog_t, bf_col)


def _fox_gate_bwd(drow, dcol, flog_t, bf_col):
    s_len = flog_t.shape[1]
    n = s_len // 128

    def body(dr_ref, dc_ref, fl_ref, bf_ref, dfl_ref, dbf_ref):
        r = lax.broadcasted_iota(jnp.int32, (128, 128), 0)
        c = lax.broadcasted_iota(jnp.int32, (128, 128), 1)
        lower = (r >= c).astype(BF16)

        def step(t, carry):
            run, tot = carry
            sl = pl.ds(pl.multiple_of((n - 1 - t) * 128, 128), 128)
            rc = _dot_ones(dr_ref[:, sl] - dc_ref[:, sl], lower) + run
            dfl = rc * _sigmoid(-(fl_ref[:, sl] + bf_ref[...]))
            dfl_ref[:, sl] = dfl
            return rc[:, 0:1], tot + jnp.sum(dfl, axis=1, keepdims=True)

        zero = jnp.zeros((FOX_H, 1), F32)
        _, tot = lax.fori_loop(0, n, step, (zero, zero))
        dbf_ref[...] = jnp.broadcast_to(tot, (FOX_H, 128))

    return _pcall(
        body,
        name="fox_gate_bwd",
        out_shape=[jax.ShapeDtypeStruct((FOX_H, s_len), F32), jax.ShapeDtypeStruct((FOX_H, 128), F32)],
    )(drow, dcol, flog_t, bf_col)


def _diag_mask(blk, transposed=False):
    r = lax.broadcasted_iota(jnp.int32, (blk, blk), 0)
    c = lax.broadcasted_iota(jnp.int32, (blk, blk), 1)
    return c >= r if transposed else r >= c


_NT = (((1,), (1,)), ((), ()))
_TN = (((0,), (0,)), ((), ()))


def _fox_fwd(qkv, cum_row):
    s_len = qkv.shape[0]
    blk = min(ATT_BLK, s_len)
    nb = s_len // blk
    log2e = 1.4426950408889634

    def body(q_ref, k_ref, v_ref, c_ref, o_ref, lse_ref, mx_s, acc_s, u_s):
        i = pl.program_id(1)

        def key_cols(j, n):
            return pl.ds(pl.multiple_of(j * blk, blk), n * blk)

        def walk(tile):
            lax.fori_loop(0, i // 2, lambda t, c: (tile(2 * t, 2, False), c)[1], 0)

            @pl.when(i % 2 == 1)
            def _():
                tile(i - 1, 1, False)

            tile(i, 1, True)

        def lane_max(j, n, masked):
            cols = key_cols(j, n)
            u = lax.dot_general(q_ref[...], k_ref[cols, :], _NT, preferred_element_type=F32) - c_ref[:, cols] * (1.0 / FOX_SCALE)
            if masked:
                u = jnp.where(_diag_mask(blk), u, NEG)
            u_s[:, cols] = u
            part = u[:, 0:128]
            for t in range(1, n * blk // 128):
                part = jnp.maximum(part, u[:, t * 128 : (t + 1) * 128])
            mx_s[...] = jnp.maximum(mx_s[...], part)

        mx_s[...] = jnp.full(mx_s.shape, NEG, F32)
        walk(lane_max)
        m = jnp.max(mx_s[...], axis=1, keepdims=True)

        def weigh(j, n, masked):
            cols = key_cols(j, n)
            p = jnp.exp2((u_s[:, cols] - m) * (FOX_SCALE * log2e))
            ones_col = (lax.broadcasted_iota(jnp.int32, (n * blk, 128), 1) == 0).astype(BF16)
            v1 = jnp.concatenate([v_ref[cols, :], ones_col], axis=1)
            acc_s[...] += jnp.dot(p.astype(BF16), v1, preferred_element_type=F32)

        acc_s[...] = jnp.zeros(acc_s.shape, F32)
        walk(weigh)
        l = acc_s[:, FOX_DH : FOX_DH + 1]
        o_ref[...] = acc_s[:, :FOX_DH] / l
        lse_ref[...] = m * FOX_SCALE + jnp.log(l)

    return _pcall(
        body,
        name="fox_fwd",
        grid=(FOX_H, nb),
        out_shape=[jax.ShapeDtypeStruct((s_len, FOX_W), F32), jax.ShapeDtypeStruct((FOX_H, s_len, 1), F32)],
        in_specs=[
            pl.BlockSpec((blk, FOX_DH), lambda h, i: (i, h)),
            pl.BlockSpec((s_len, FOX_DH), lambda h, i: (0, FOX_H + h)),
            pl.BlockSpec((s_len, FOX_DH), lambda h, i: (0, 2 * FOX_H + h)),
            pl.BlockSpec((None, 1, s_len), lambda h, i: (h, 0, 0)),
        ],
        out_specs=[
            pl.BlockSpec((blk, FOX_DH), lambda h, i: (i, h)),
            pl.BlockSpec((None, blk, 1), lambda h, i: (h, i, 0)),
        ],
        scratch_shapes=[pltpu.VMEM((blk, 128), F32), pltpu.VMEM((blk, 2 * FOX_DH), F32), pltpu.VMEM((blk, s_len), F32)],
        compiler_params=_cp("parallel", "arbitrary"),
    )(qkv, qkv, qkv, cum_row)


def _fox_bwd(qkv, cum_col, lse_row, delta_row, do):
    s_len = qkv.shape[0]
    blk = min(ATT_BLK, s_len)
    nb = s_len // blk

    def body(q_ref, k_ref, v_ref, c_ref, lse_ref, dl_ref, do_ref, dq_ref, dk_ref, dv_ref, dc_ref, dr_ref, dk_s, dv_s, dc_s, cb_s):
        j = pl.program_id(1)

        @pl.when(j == 0)
        def _():
            dq_ref[...] = jnp.zeros(dq_ref.shape, F32)
            dr_ref[...] = jnp.zeros(dr_ref.shape, F32)

        dk_s[...] = jnp.zeros(dk_s.shape, F32)
        dv_s[...] = jnp.zeros(dv_s.shape, F32)
        dc_s[...] = jnp.zeros(dc_s.shape, F32)
        cb_s[...] = jnp.broadcast_to(c_ref[...], cb_s.shape)

        def tile(i, n, diag):
            rows = pl.ds(pl.multiple_of(i * blk, blk), n * blk)
            q, dob = q_ref[rows, :], do_ref[rows, :]
            k, v = k_ref[...], v_ref[...]
            s_t = lax.dot_general(k, q, _NT, preferred_element_type=F32) * FOX_SCALE - cb_s[:, : n * blk]
            p_t = jnp.exp(s_t - lse_ref[:, rows])
            if diag:
                p_t = jnp.where(_diag_mask(blk, transposed=True), p_t, 0.0)
            dp_t = lax.dot_general(v, dob, _NT, preferred_element_type=F32)
            ds_t = p_t * (dp_t - dl_ref[:, rows])
            dsb = ds_t.astype(BF16)
            dv_s[...] += jnp.dot(p_t.astype(BF16), dob, preferred_element_type=F32)
            dk_s[...] += jnp.dot(dsb, q, preferred_element_type=F32)
            dq_c = lax.dot_general(dsb, k, _TN, preferred_element_type=F32)
            part = ds_t[:, 0:128]
            for t in range(1, n * blk // 128):
                part = part + ds_t[:, t * 128 : (t + 1) * 128]
            dc_s[...] += part
            dr_ref[:, rows] += jnp.sum(ds_t, axis=0, keepdims=True)
            if diag:
                dq_ref[rows, :] = (dq_ref[rows, :] + dq_c) * FOX_SCALE
            else:
                dq_ref[rows, :] += dq_c

        tile(j, 1, True)
        odd = (nb - 1 - j) % 2

        @pl.when(odd == 1)
        def _():
            tile(j + 1, 1, False)

        lax.fori_loop(0, (nb - 1 - j) // 2, lambda t, c: (tile(j + 1 + odd + 2 * t, 2, False), c)[1], 0)
        dk_ref[...] = (dk_s[...] * FOX_SCALE).astype(BF16)
        dv_ref[...] = dv_s[...].astype(BF16)
        dc_ref[...] = jnp.sum(dc_s[...], axis=1, keepdims=True)

    head = lambda h, j: (0, h)
    row = pl.BlockSpec((None, 1, s_len), lambda h, j: (h, 0, 0))
    return _pcall(
        body,
        name="fox_bwd",
        grid=(FOX_H, nb),
        out_shape=[
            jax.ShapeDtypeStruct((s_len, FOX_W), F32),
            jax.ShapeDtypeStruct((s_len, FOX_W), BF16),
            jax.ShapeDtypeStruct((s_len, FOX_W), BF16),
            jax.ShapeDtypeStruct((FOX_H, s_len, 1), F32),
            jax.ShapeDtypeStruct((FOX_H, 1, s_len), F32),
        ],
        in_specs=[
            pl.BlockSpec((s_len, FOX_DH), head),
            pl.BlockSpec((blk, FOX_DH), lambda h, j: (j, FOX_H + h)),
            pl.BlockSpec((blk, FOX_DH), lambda h, j: (j, 2 * FOX_H + h)),
            pl.BlockSpec((None, blk, 1), lambda h, j: (h, j, 0)),
            row,
            row,
            pl.BlockSpec((s_len, FOX_DH), head),
        ],
        out_specs=[
            pl.BlockSpec((s_len, FOX_DH), head),
            pl.BlockSpec((blk, FOX_DH), lambda h, j: (j, h)),
            pl.BlockSpec((blk, FOX_DH), lambda h, j: (j, h)),
            pl.BlockSpec((None, blk, 1), lambda h, j: (h, j, 0)),
            row,
        ],
        scratch_shapes=[
            pltpu.VMEM((blk, FOX_DH), F32),
            pltpu.VMEM((blk, FOX_DH), F32),
            pltpu.VMEM((blk, 128), F32),
            pltpu.VMEM((blk, 2 * blk), F32),
        ],
        compiler_params=_cp("parallel", "arbitrary"),
    )(qkv, qkv, qkv, cum_col, lse_row, delta_row, do)


def _swa_bias():
    rows = SWA_G * WINDOW
    r = np.arange(rows)[:, None]
    c = np.arange(2 * WINDOW)[None, :]
    dist = (r % WINDOW) - c + WINDOW
    valid = (dist >= 0) & (dist < WINDOW)
    out = np.empty((SWA_HKV, rows, 2 * WINDOW), np.float32)
    for g in range(SWA_HKV):
        slope = np.array([SLOPES[g * SWA_G + t] for t in range(SWA_G)], np.float32)[r // WINDOW]
        out[g] = np.where(valid, -(slope * dist.astype(np.float32)), np.float32(NEG))
    return jnp.asarray(out)


def _swa_group(i, q_ref, kk, sinks_ref, bias_ref, g):
    rows = SWA_G * WINDOW
    c = lax.broadcasted_iota(jnp.int32, (1, 2 * WINDOW), 1)
    no_prev = jnp.where((c < WINDOW) & (i == 0), NEG, 0.0)
    head = lax.broadcasted_iota(jnp.int32, (rows, 1), 0) // WINDOW
    sink = jnp.zeros((rows, 1), F32)
    for t in range(SWA_G):
        sink = jnp.where(head == t, sinks_ref[g * SWA_G + t], sink)
    q = jnp.concatenate([q_ref[:, (g * SWA_G + t) * SWA_DH : (g * SWA_G + t + 1) * SWA_DH] for t in range(SWA_G)], axis=0)
    k = kk[:, g * SWA_DH : (g + 1) * SWA_DH]
    s = lax.dot_general(q, k, _NT, preferred_element_type=F32) * SWA_SCALE + bias_ref[g] + no_prev
    m = jnp.maximum(jnp.max(s, axis=1, keepdims=True), sink)
    e = jnp.exp(s - m)
    e_sink = jnp.exp(sink - m)
    inv = 1.0 / (jnp.sum(e, axis=1, keepdims=True) + e_sink)
    return q, k, e * inv, e_sink * inv


def _swa_specs(col_q, col_k, col_v, rev, nb):
    def blk(t):
        return nb - 1 - t if rev else t

    return [
        pl.BlockSpec((WINDOW, SWA_W), lambda t: (blk(t), col_q)),
        pl.BlockSpec((WINDOW, SWA_KVW), lambda t: (jnp.maximum(blk(t) - 1, 0), col_k)),
        pl.BlockSpec((WINDOW, SWA_KVW), lambda t: (blk(t), col_k)),
        pl.BlockSpec((WINDOW, SWA_KVW), lambda t: (jnp.maximum(blk(t) - 1, 0), col_v)),
        pl.BlockSpec((WINDOW, SWA_KVW), lambda t: (blk(t), col_v)),
    ]


def _swa_fwd(qkv, sinks):
    s_len = qkv.shape[0]
    nb = s_len // WINDOW
    bias_spec = pl.BlockSpec((SWA_HKV, SWA_G * WINDOW, 2 * WINDOW), lambda t: (0, 0, 0))

    def body(q_ref, kp_ref, kc_ref, vp_ref, vc_ref, sinks_ref, bias_ref, o_ref):
        i = pl.program_id(0)
        kk = jnp.concatenate([kp_ref[...], kc_ref[...]], axis=0)
        vv = jnp.concatenate([vp_ref[...], vc_ref[...]], axis=0)
        for g in range(SWA_HKV):
            _, _, p, _ = _swa_group(i, q_ref, kk, sinks_ref, bias_ref, g)
            o = jnp.dot(p.astype(BF16), vv[:, g * SWA_DH : (g + 1) * SWA_DH], preferred_element_type=F32)
            for t in range(SWA_G):
                h = g * SWA_G + t
                o_ref[:, h * SWA_DH : (h + 1) * SWA_DH] = o[t * WINDOW : (t + 1) * WINDOW, :]

    return _pcall(
        body,
        name="swa_fwd",
        grid=(nb,),
        out_shape=jax.ShapeDtypeStruct((s_len, SWA_W), F32),
        in_specs=_swa_specs(0, 4, 5, False, nb) + [pl.BlockSpec(memory_space=pltpu.SMEM), bias_spec],
        out_specs=pl.BlockSpec((WINDOW, SWA_W), lambda t: (t, 0)),
        compiler_params=_cp("parallel"),
    )(qkv, qkv, qkv, qkv, qkv, sinks, _swa_bias())


def _swa_bwd(qkv, sinks, do):
    s_len = qkv.shape[0]
    nb = s_len // WINDOW
    bias_spec = pl.BlockSpec((SWA_HKV, SWA_G * WINDOW, 2 * WINDOW), lambda t: (0, 0, 0))

    def body(q_ref, kp_ref, kc_ref, vp_ref, vc_ref, sinks_ref, bias_ref, do_ref, dq_ref, dk_ref, dv_ref, dsink_ref, ck_s, cv_s, dkk_s, dvv_s):
        t = pl.program_id(0)
        i = nb - 1 - t

        @pl.when(t == 0)
        def _():
            ck_s[...] = jnp.zeros(ck_s.shape, F32)
            cv_s[...] = jnp.zeros(cv_s.shape, F32)
            dsink_ref[...] = jnp.zeros(dsink_ref.shape, F32)

        kk = jnp.concatenate([kp_ref[...], kc_ref[...]], axis=0)
        vv = jnp.concatenate([vp_ref[...], vc_ref[...]], axis=0)
        lane = lax.broadcasted_iota(jnp.int32, (1, 128), 1)
        dsink = jnp.zeros((1, 128), F32)
        for g in range(SWA_HKV):
            cols = slice(g * SWA_DH, (g + 1) * SWA_DH)
            q, k, p, p_sink = _swa_group(i, q_ref, kk, sinks_ref, bias_ref, g)
            dob = jnp.concatenate([do_ref[:, (g * SWA_G + t) * SWA_DH : (g * SWA_G + t + 1) * SWA_DH] for t in range(SWA_G)], axis=0)
            dp = lax.dot_general(dob, vv[:, cols], _NT, preferred_element_type=F32)
            delta = jnp.sum(p * dp, axis=1, keepdims=True)
            dsb = (p * (dp - delta)).astype(BF16)
            dq = (jnp.dot(dsb, k, preferred_element_type=F32) * SWA_SCALE).astype(BF16)
            ps_d = p_sink * delta
            for t in range(SWA_G):
                h = g * SWA_G + t
                dq_ref[:, h * SWA_DH : (h + 1) * SWA_DH] = dq[t * WINDOW : (t + 1) * WINDOW, :]
                dsink = dsink + jnp.where(lane == h, -jnp.sum(ps_d[t * WINDOW : (t + 1) * WINDOW, :], axis=0, keepdims=True), 0.0)
            dkk_s[:, cols] = lax.dot_general(dsb, q, _TN, preferred_element_type=F32) * SWA_SCALE
            dvv_s[:, cols] = lax.dot_general(p.astype(BF16), dob, _TN, preferred_element_type=F32)
        dk_ref[...] = (dkk_s[WINDOW:, :] + ck_s[...]).astype(BF16)
        dv_ref[...] = (dvv_s[WINDOW:, :] + cv_s[...]).astype(BF16)
        ck_s[...] = dkk_s[:WINDOW, :]
        cv_s[...] = dvv_s[:WINDOW, :]
        dsink_ref[...] += dsink

    row = lambda t: (nb - 1 - t, 0)
    return _pcall(
        body,
        name="swa_bwd",
        grid=(nb,),
        out_shape=[
            jax.ShapeDtypeStruct((s_len, SWA_W), BF16),
            jax.ShapeDtypeStruct((s_len, SWA_KVW), BF16),
            jax.ShapeDtypeStruct((s_len, SWA_KVW), BF16),
            jax.ShapeDtypeStruct((1, 128), F32),
        ],
        in_specs=_swa_specs(0, 4, 5, True, nb)
        + [pl.BlockSpec(memory_space=pltpu.SMEM), bias_spec, pl.BlockSpec((WINDOW, SWA_W), row)],
        out_specs=[
            pl.BlockSpec((WINDOW, SWA_W), row),
            pl.BlockSpec((WINDOW, SWA_KVW), row),
            pl.BlockSpec((WINDOW, SWA_KVW), row),
            pl.BlockSpec((1, 128), lambda t: (0, 0)),
        ],
        scratch_shapes=[
            pltpu.VMEM((WINDOW, SWA_KVW), F32),
            pltpu.VMEM((WINDOW, SWA_KVW), F32),
            pltpu.VMEM((2 * WINDOW, SWA_KVW), F32),
            pltpu.VMEM((2 * WINDOW, SWA_KVW), F32),
        ],
        compiler_params=_cp("arbitrary"),
    )(qkv, qkv, qkv, qkv, qkv, sinks, _swa_bias(), do)


def _branch_fwd(o, gates, g_blk, w_b, name):
    s_len, wd = o.shape
    d = w_b.shape[1]
    tm = min(512, s_len)

    def body(o_ref, g_ref, w_ref, y_ref, a_ref):
        g = g_ref[...].astype(F32)
        a = (o_ref[...] * (g * _sigmoid(g))).astype(BF16)
        a_ref[...] = a
        y_ref[...] = jnp.dot(a, w_ref[...], preferred_element_type=F32).astype(BF16)

    return _pcall(
        body,
        name=name,
        grid=(s_len // tm,),
        out_shape=[jax.ShapeDtypeStruct((s_len, d), BF16), jax.ShapeDtypeStruct((s_len, wd), BF16)],
        in_specs=[
            pl.BlockSpec((tm, wd), lambda i: (i, 0)),
            pl.BlockSpec((tm, wd), lambda i: (i, g_blk)),
            pl.BlockSpec((wd, d), lambda i: (0, 0)),
        ],
        out_specs=[pl.BlockSpec((tm, d), lambda i: (i, 0)), pl.BlockSpec((tm, wd), lambda i: (i, 0))],
        compiler_params=_cp("parallel"),
    )(o, gates, w_b)


def _out_stage(gates, mf_blk, y_fox, y_swa, w_out, x, ada, ln_g, ln_b, target):
    s_len, d = x.shape
    tm = min(256, s_len)
    n_steps = s_len // tm

    def body(mf_ref, ms_ref, yf_ref, ys_ref, w_ref, x_ref, gate_ref, lg_ref, lb_ref, t_ref, mg_ref, dza_ref, dsub_ref, red_ref):
        i = pl.program_id(0)
        merged = _sigmoid(mf_ref[...].astype(F32)) * yf_ref[...].astype(F32) + _sigmoid(ms_ref[...].astype(F32)) * ys_ref[...].astype(F32)
        mb = merged.astype(BF16)
        mg_ref[...] = mb
        sub = jnp.dot(mb, w_ref[...], preferred_element_type=F32)
        gate = gate_ref[...]
        z = ALPHA * x_ref[...] + gate * sub
        mu = jnp.mean(z, axis=-1, keepdims=True)
        zc = z - mu
        var = jnp.mean(zc * zc, axis=-1, keepdims=True)
        rstd = lax.rsqrt(var + LN_EPS)
        zhat = zc * rstd
        err = zhat * lg_ref[...] + lb_ref[...] - t_ref[...]
        dout = err * (1.0 / d)
        dzhat = dout * lg_ref[...]
        dz = rstd * (dzhat - jnp.mean(dzhat, axis=-1, keepdims=True) - zhat * jnp.mean(dzhat * zhat, axis=-1, keepdims=True))
        dza_ref[...] = ALPHA * dz
        dsub_ref[...] = (gate * dz).astype(BF16)
        part = jnp.concatenate(
            [
                jnp.sum(dz * sub, axis=0, keepdims=True),
                jnp.sum(dout * zhat, axis=0, keepdims=True),
                jnp.sum(dout, axis=0, keepdims=True),
                jnp.sum(err * err, axis=0, keepdims=True),
                jnp.zeros((4, d), F32),
            ],
            axis=0,
        )

        @pl.when(i == 0)
        def _():
            red_ref[...] = part

        @pl.when(i > 0)
        def _():
            red_ref[...] += part

        @pl.when(i == n_steps - 1)
        def _():
            red_ref[4:5, :] = jnp.broadcast_to(jnp.sum(red_ref[3:4, :], axis=1, keepdims=True), (1, d))

    row = pl.BlockSpec((tm, d), lambda i: (i, 0))
    vec = pl.BlockSpec((1, d), lambda i: (0, 0))
    return _pcall(
        body,
        name="out_stage",
        grid=(n_steps,),
        out_shape=[
            jax.ShapeDtypeStruct((s_len, d), BF16),
            jax.ShapeDtypeStruct((s_len, d), F32),
            jax.ShapeDtypeStruct((s_len, d), BF16),
            jax.ShapeDtypeStruct((8, d), F32),
        ],
        in_specs=[
            pl.BlockSpec((tm, d), lambda i: (i, mf_blk)),
            pl.BlockSpec((tm, d), lambda i: (i, mf_blk + 1)),
            row,
            row,
            pl.BlockSpec((d, d), lambda i: (0, 0), pipeline_mode=pl.Buffered(1)),
            row,
            pl.BlockSpec((1, d), lambda i: (0, 2)),
            vec,
            vec,
            row,
        ],
        out_specs=[row, row, row, pl.BlockSpec((8, d), lambda i: (0, 0))],
        compiler_params=_cp("arbitrary"),
    )(gates, gates, y_fox, y_swa, w_out, x, ada, ln_g, ln_b, target)


def _merge_bwd(dsub, w_out, gates, mf_blk, y_fox, y_swa):
    s_len, d = dsub.shape
    tm = min(256, s_len)

    def body(ds_ref, w_ref, mf_ref, ms_ref, yf_ref, ys_ref, dmf_ref, dms_ref, dyf_ref, dys_ref):
        dm = lax.dot_general(ds_ref[...], w_ref[...], _NT, preferred_element_type=F32)
        sf, ss = _sigmoid(mf_ref[...].astype(F32)), _sigmoid(ms_ref[...].astype(F32))
        dmf_ref[...] = (dm * yf_ref[...].astype(F32) * (sf * (1.0 - sf))).astype(BF16)
        dms_ref[...] = (dm * ys_ref[...].astype(F32) * (ss * (1.0 - ss))).astype(BF16)
        dyf_ref[...] = (dm * sf).astype(BF16)
        dys_ref[...] = (dm * ss).astype(BF16)

    row = pl.BlockSpec((tm, d), lambda i: (i, 0))
    return _pcall(
        body,
        name="merge_bwd",
        grid=(s_len // tm,),
        out_shape=[jax.ShapeDtypeStruct((s_len, d), BF16)] * 4,
        in_specs=[
            row,
            pl.BlockSpec((d, d), lambda i: (0, 0), pipeline_mode=pl.Buffered(1)),
            pl.BlockSpec((tm, d), lambda i: (i, mf_blk)),
            pl.BlockSpec((tm, d), lambda i: (i, mf_blk + 1)),
            row,
            row,
        ],
        out_specs=[row] * 4,
        compiler_params=_cp("parallel"),
    )(dsub, w_out, gates, gates, y_fox, y_swa)


def _branch_bwd(dy, w_b, o, gates, g_blk, name, n_heads):
    s_len, d = dy.shape
    wd = w_b.shape[0]
    tm = min(512, s_len)

    def body(dy_ref, w_ref, o_ref, g_ref, do_ref, dg_ref, *rest):
        da = lax.dot_general(dy_ref[...], w_ref[...], _NT, preferred_element_type=F32)
        g = g_ref[...].astype(F32)
        sg = _sigmoid(g)
        do = da * (g * sg)
        do_ref[...] = do.astype(BF16)
        o = o_ref[...]
        dg_ref[...] = (da * o * (sg * (1.0 + g * (1.0 - sg)))).astype(BF16)
        if n_heads:
            prod = do.astype(BF16).astype(F32) * o
            lane = lax.broadcasted_iota(jnp.int32, (1, 128), 1)
            delta = jnp.zeros((tm, 128), F32)
            for h in range(n_heads):
                dh = jnp.sum(prod[:, h * 128 : (h + 1) * 128], axis=1, keepdims=True)
                delta = delta + jnp.where(lane == h, dh, 0.0)
            rest[0][...] = delta

    out_shape = [jax.ShapeDtypeStruct((s_len, wd), BF16), jax.ShapeDtypeStruct((s_len, wd), BF16)]
    out_specs = [pl.BlockSpec((tm, wd), lambda i: (i, 0))] * 2
    if n_heads:
        out_shape.append(jax.ShapeDtypeStruct((s_len, 128), F32))
        out_specs.append(pl.BlockSpec((tm, 128), lambda i: (i, 0)))
    return _pcall(
        body,
        name=name,
        grid=(s_len // tm,),
        out_shape=out_shape,
        in_specs=[
            pl.BlockSpec((tm, d), lambda i: (i, 0)),
            pl.BlockSpec((wd, d), lambda i: (0, 0)),
            pl.BlockSpec((tm, wd), lambda i: (i, 0)),
            pl.BlockSpec((tm, wd), lambda i: (i, g_blk)),
        ],
        out_specs=out_specs,
        compiler_params=_cp("parallel"),
    )(dy, w_b, o, gates)


def _in_bwd(dproj, w_in_t, x, ada, dza, ride):
    s_len, d = x.shape
    k_tot = dproj.shape[1]
    tm, tk = min(512, s_len), 1024
    ni, nk = s_len // tm, k_tot // tk
    n = len(ride)

    def body(dp_ref, w_ref, x_ref, sc_ref, dza_ref, *rest):
        ins, (gx_ref, red_ref), outs = rest[:n], rest[n : n + 2], rest[n + 2 : 2 * n + 2]
        sems, acc_s = rest[2 * n + 2 : 2 * n + 5], rest[2 * n + 5]
        i, kk = pl.program_id(0), pl.program_id(1)

        @pl.when((i == 0) & (kk == 0))
        def _():
            _rider_start("exchange", ins, outs, *sems)

        @pl.when((i == ni - 1) & (kk == nk - 1))
        def _():
            _rider_wait("exchange", ins, outs, *sems)

        part = jnp.dot(dp_ref[...], w_ref[...], preferred_element_type=F32)

        @pl.when(kk == 0)
        def _():
            acc_s[...] = part

        @pl.when(kk > 0)
        def _():
            acc_s[...] += part

        @pl.when(kk == nk - 1)
        def _():
            dh = acc_s[...]
            xv = x_ref[...]
            mu = jnp.mean(xv, axis=-1, keepdims=True)
            xc = xv - mu
            var = jnp.mean(xc * xc, axis=-1, keepdims=True)
            rstd = lax.rsqrt(var + LN_EPS)
            xhat = xc * rstd
            dxhat = dh * (1.0 + sc_ref[...])
            dx = rstd * (dxhat - jnp.mean(dxhat, axis=-1, keepdims=True) - xhat * jnp.mean(dxhat * xhat, axis=-1, keepdims=True))
            gx_ref[...] = dza_ref[...] + dx
            part_r = jnp.concatenate(
                [jnp.sum(dh, axis=0, keepdims=True), jnp.sum(dh * xhat, axis=0, keepdims=True), jnp.zeros((6, d), F32)], axis=0
            )

            @pl.when(i == 0)
            def _():
                red_ref[...] = part_r

            @pl.when(i > 0)
            def _():
                red_ref[...] += part_r

    row = pl.BlockSpec((tm, d), lambda i, kk: (i, 0))
    hbm = pl.BlockSpec(memory_space=pltpu.HBM)
    return _pcall(
        body,
        name="in_bwd",
        grid=(ni, nk),
        out_shape=[jax.ShapeDtypeStruct((s_len, d), F32), jax.ShapeDtypeStruct((8, d), F32)]
        + [jax.ShapeDtypeStruct(r.shape, r.dtype) for r in ride],
        in_specs=[
            pl.BlockSpec((tm, tk), lambda i, kk: (i, kk)),
            pl.BlockSpec((tk, d), lambda i, kk: (kk, 0)),
            row,
            pl.BlockSpec((1, d), lambda i, kk: (0, 1)),
            row,
        ]
        + [hbm] * n,
        out_specs=[row, pl.BlockSpec((8, d), lambda i, kk: (0, 0))] + [hbm] * n,
        scratch_shapes=_rider_scratch(n) + [pltpu.VMEM((tm, d), F32)],
        compiler_params=_cp("arbitrary", "arbitrary"),
    )(dproj, w_in_t, x, ada, dza, *ride)


def _pad_lanes(v, n):
    return jnp.pad(v, ((0, 0), (0, n - v.shape[1])))


def kernel(x, c, w_ada, b_ada, w_in, b_f, attn_sinks, w_br_fox, w_br_swa, w_out, ln_g, ln_b, loss_target, m_w_ada, m_b_ada, m_w_in, m_b_f, m_attn_sinks, m_w_br_fox, m_w_br_swa, m_w_out, m_ln_g, m_ln_b, v_w_ada, v_b_ada, v_w_in, v_b_f, v_attn_sinks, v_w_br_fox, v_w_br_swa, v_w_out, v_ln_g, v_ln_b):
    x2, tgt = x[0], loss_target[0]
    s_len, d = x2.shape
    me = 4 * lax.axis_index("x") + 2 * lax.axis_index("y") + lax.axis_index("c")
    off_ms = OFF_MF + d
    in_pad = off_ms + d
    c_ada = w_ada.shape[2]
    c_in = w_in.shape[2]
    c_br = w_br_fox.shape[2]

    w_in_full = _all_gather(w_in[0].T.astype(BF16), "ag_w_in", pltpu.HBM).reshape(N_DEV * c_in, d)
    w_in_pad = jnp.concatenate(
        [w_in_full[:REAL_FLOG_END], jnp.zeros((FLOG_PAD - N_FLOG, d), BF16), w_in_full[REAL_FLOG_END:]], axis=0
    )
    k_cut = REAL_FLOG_END // c_in

    c_all = _gather_rows(c, "ag_c")
    b_cols = lax.dynamic_slice(b_ada, (0, me * c_ada), (1, c_ada))
    ada_cols = _ada_fwd(c_all, w_ada[0], b_cols)
    ada_g = _all_gather(ada_cols, "ag_ada", pltpu.VMEM)
    ada = lax.dynamic_index_in_dim(ada_g, me, axis=1, keepdims=False).reshape(1, N_DEV * c_ada)

    h = _ln_mod(x2, ada)
    qkv_fox = _mm_cols(h, w_in_pad, OFF_FQ, 3 * FOX_W, BF16, "proj_fox")
    flog = _mm_cols(h, w_in_pad, OFF_FLOG, FLOG_PAD, F32, "proj_flog")
    qkv_swa = _mm_cols(h, w_in_pad, OFF_SQ, SWA_W + 2 * SWA_KVW, BF16, "proj_swa")
    gates, w_bf, w_bs, w_o = _mm_cols(
        h, w_in_pad, OFF_GF, in_pad - OFF_GF, BF16, "proj_gates",
        ride=(w_br_fox[0].astype(BF16), w_br_swa[0].astype(BF16), w_out[0].astype(BF16)),
    )
    w_bf = w_bf.reshape(N_DEV, FOX_W, c_br).transpose(1, 0, 2).reshape(FOX_W, d)
    w_bs = w_bs.reshape(N_DEV, SWA_W, c_br).transpose(1, 0, 2).reshape(SWA_W, d)
    w_o = w_o.reshape(d, d)
    mf_blk = (OFF_MF - OFF_GF) // d

    flog_t = flog[:, :N_FLOG].T
    bf_col = b_f.reshape(FOX_H, 1)
    cum = _fox_cum(flog_t, bf_col)
    cum_row = cum.reshape(FOX_H, 1, s_len)
    o_fox, lse = _fox_fwd(qkv_fox, cum_row)
    sinks = attn_sinks.reshape(SWA_HQ)
    o_swa = _swa_fwd(qkv_swa, sinks)

    y_fox, a_fox = _branch_fwd(o_fox, gates, 0, w_bf, "branch_fox")
    y_swa, a_swa = _branch_fwd(o_swa, gates, 1, w_bs, "branch_swa")
    merged, dza, dsub, red = _out_stage(gates, mf_blk, y_fox, y_swa, w_o, x2, ada, ln_g, ln_b, tgt)
    loss = lax.psum(0.5 * red[4, 0] / d, ("x", "y", "c"))

    dmf, dms, dy_fox, dy_swa = _merge_bwd(dsub, w_o, gates, mf_blk, y_fox, y_swa)
    do_fox, dg_fox, delta = _branch_bwd(dy_fox, w_bf, o_fox, gates, 0, "branch_fox_bwd", FOX_H)
    do_swa, dg_swa = _branch_bwd(dy_swa, w_bs, o_swa, gates, 1, "branch_swa_bwd", 0)
    delta_row = delta[:, :FOX_H].T.reshape(FOX_H, 1, s_len)
    dq_f, dk_f, dv_f, dcol, drow = _fox_bwd(
        qkv_fox, cum.reshape(FOX_H, s_len, 1), lse.reshape(FOX_H, 1, s_len), delta_row, do_fox
    )
    dflog_t, dbf = _fox_gate_bwd(drow.reshape(FOX_H, s_len), dcol.reshape(FOX_H, s_len), flog_t, bf_col)
    dq_s, dk_s, dv_s, dsink = _swa_bwd(qkv_swa, sinks, do_swa)
    dflog = _pad_lanes(dflog_t.T, FLOG_PAD).astype(BF16)
    dproj = jnp.concatenate([dq_f.astype(BF16), dk_f, dv_f, dflog, dq_s, dk_s, dv_s, dg_fox, dg_swa, dmf, dms], axis=1)
    g_w_bf = _mm_tn(a_fox, dy_fox, "grad_w_br_fox")
    g_w_bs = _mm_tn(a_swa, dy_swa, "grad_w_br_swa")
    g_w_o = _mm_tn(merged, dsub, "grad_w_out")
    g_w_in, r_bf, r_bs, r_o = _mm_tn(
        dproj, h, "grad_w_in",
        ride=(
            g_w_bf.reshape(FOX_W, N_DEV, c_br).transpose(1, 0, 2),
            g_w_bs.reshape(SWA_W, N_DEV, c_br).transpose(1, 0, 2),
            g_w_o.reshape(N_DEV, d // N_DEV, d),
        ),
    )
    pad = FLOG_PAD - N_FLOG
    g_blocks = jnp.stack(
        [g_w_in[k * c_in : (k + 1) * c_in] for k in range(k_cut)]
        + [jnp.concatenate([g_w_in[k_cut * c_in : REAL_FLOG_END], g_w_in[OFF_SQ : (k_cut + 1) * c_in + pad]], axis=0)]
        + [g_w_in[k * c_in + pad : (k + 1) * c_in + pad] for k in range(k_cut + 1, N_DEV)]
    )

    grad_x, red2, r_in = _in_bwd(dproj, w_in_pad, x2, ada, dza, ride=(g_blocks,))
    out_w_in = _sum_adam_t(r_in, w_in[0].T, m_w_in[0].T, v_w_in[0].T, "adam_w_in")
    out_w_in = [o.T for o in out_w_in]
    out_w_bf = _sum_adam(r_bf, w_br_fox[0], m_w_br_fox[0], v_w_br_fox[0], "adam_w_br_fox")
    out_w_bs = _sum_adam(r_bs, w_br_swa[0], m_w_br_swa[0], v_w_br_swa[0], "adam_w_br_swa")
    out_w_o = _sum_adam(r_o, w_out[0], m_w_out[0], v_w_out[0], "adam_w_out")

    packed = jnp.concatenate([red2[0:1], red2[1:2], red[0:1], _pad_lanes(dbf[:, 0].reshape(1, FOX_H), 128), dsink, red[1:2], red[2:3]], axis=1)
    gathered = _gather_rows(packed, "ag_small")
    pack = lambda a, b, cc, dd, e: jnp.concatenate([a, _pad_lanes(b, 128), _pad_lanes(cc, 128), dd, e], axis=1)
    small = _small_adam(
        gathered,
        pack(b_ada, b_f, attn_sinks, ln_g, ln_b),
        pack(m_b_ada, m_b_f, m_attn_sinks, m_ln_g, m_ln_b),
        pack(v_b_ada, v_b_f, v_attn_sinks, v_ln_g, v_ln_b),
    )
    dada_cols = lax.dynamic_slice(gathered, (0, me * c_ada), (N_DEV, c_ada))
    out_w_ada = _wada_adam(c_all.T, dada_cols, w_ada[0], m_w_ada[0], v_w_ada[0])

    o1, o2, o3 = 3 * d, 3 * d + 128, 3 * d + 256

    def unpack(p):
        return p[:, :o1], p[:, o1 : o1 + FOX_H], p[:, o2 : o2 + SWA_HQ], p[:, o3 : o3 + d], p[:, o3 + d : o3 + 2 * d]

    kinds = []
    for k in range(4):
        b_ada_k, b_f_k, sinks_k, ln_g_k, ln_b_k = unpack(small[k])
        kinds.append(
            [out_w_ada[k][None], b_ada_k, out_w_in[k][None], b_f_k, sinks_k, out_w_bf[k][None], out_w_bs[k][None], out_w_o[k][None], ln_g_k, ln_b_k]
        )
    return (loss, grad_x[None], *kinds[0], *kinds[1], *kinds[2], *kinds[3])
```

```python
import numpy as np
import jax
import jax.numpy as jnp
from jax import lax
from jax.experimental import pallas as pl
from jax.experimental.pallas import tpu as pltpu

F32 = jnp.float32
BF16 = jnp.bfloat16
N_DEV = 8
MESH = pl.DeviceIdType.MESH

FOX_H, FOX_DH, FOX_W = 8, 128, 1024
SWA_HQ, SWA_HKV, SWA_DH, SWA_G = 16, 4, 64, 4
SWA_W, SWA_KVW, WINDOW = 1024, 256, 128
LN_EPS = 1e-5
NEG = -1e30
DEPTH = 1
ALPHA = (2.0 * DEPTH) ** 0.25
FOX_SCALE = FOX_DH ** -0.5
SWA_SCALE = SWA_DH ** -0.5
SLOPES = [2.0 ** (-8.0 * (h + 1.0) / SWA_HQ) for h in range(SWA_HQ)]

ADAM_LR, ADAM_B1, ADAM_B2, ADAM_EPS, ADAM_WD, ADAM_STEP = 0.001, 0.9, 0.999, 1e-08, 0.01, 10

N_FLOG = 8
FLOG_PAD = 512
OFF_FQ, OFF_FK, OFF_FV, OFF_FLOG = 0, 1024, 2048, 3072
OFF_SQ = OFF_FLOG + FLOG_PAD
OFF_SK = OFF_SQ + SWA_W
OFF_SV = OFF_SK + SWA_KVW
OFF_GF = OFF_SV + SWA_KVW
OFF_GS = OFF_GF + FOX_W
OFF_MF = OFF_GS + SWA_W
REAL_FLOG_END = OFF_FLOG + N_FLOG

ATT_BLK = 512
VMEM_LIMIT = 52 * 1024 * 1024


def _pcall(body, **kw):
    return pl.pallas_call(body, **kw)


def _cp(*sem):
    return pltpu.CompilerParams(dimension_semantics=sem, vmem_limit_bytes=VMEM_LIMIT)


def _sigmoid(x):
    return 0.5 * jnp.tanh(0.5 * x) + 0.5


def _all_gather(x, name, space):
    m_per, n = x.shape

    def body(x_ref, out_ref, send_sems, recv_sems, local_sem):
        mx, my, mc = lax.axis_index("x"), lax.axis_index("y"), lax.axis_index("c")
        me, sibling = (mx, my, mc), (mx, my, 1 - mc)
        chips = [(1 - mx, my), (mx, 1 - my), (1 - mx, 1 - my)]

        def rows(px, py, pc):
            return out_ref.at[4 * px + 2 * py + pc]

        def copy(k, block, to, src=None):
            return pltpu.make_async_remote_copy(
                src_ref=rows(*block) if src is None else src,
                dst_ref=rows(*block),
                send_sem=send_sems.at[k],
                recv_sem=recv_sems.at[k],
                device_id=to,
                device_id_type=MESH,
            )

        mine = pltpu.make_async_copy(x_ref, rows(*me), local_sem)
        mine.start()
        first = [copy(0, me, sibling, src=x_ref)]
        first += [copy(1 + j, me, (*chip, mc), src=x_ref) for j, chip in enumerate(chips)]
        for cp in first:
            cp.start()
        passed = [copy(4 + j, (*chip, mc), sibling) for j, chip in enumerate(chips)]
        for j, chip in enumerate(chips):
            copy(1 + j, (*chip, mc), me).wait_recv()
            passed[j].start()
        copy(0, sibling, me).wait_recv()
        for j, chip in enumerate(chips):
            copy(4 + j, (*chip, 1 - mc), me).wait_recv()
        for cp in first + passed:
            cp.wait_send()
        mine.wait()

    return _pcall(
        body,
        name=name,
        out_shape=jax.ShapeDtypeStruct((N_DEV, m_per, n), x.dtype),
        in_specs=[pl.BlockSpec(memory_space=space)],
        out_specs=pl.BlockSpec(memory_space=space),
        scratch_shapes=[pltpu.SemaphoreType.DMA((7,)), pltpu.SemaphoreType.DMA((7,)), pltpu.SemaphoreType.DMA],
    )(x)


def _peer(d, mx, my, mc):
    return (1 - mx if (d >> 2) & 1 else mx, 1 - my if (d >> 1) & 1 else my, 1 - mc if d & 1 else mc)


def _rider_copies(kind, ins, outs, send_sems, recv_sems, local_sems):
    mx, my, mc = lax.axis_index("x"), lax.axis_index("y"), lax.axis_index("c")
    me = 4 * mx + 2 * my + mc
    remote, local = [], []
    for a in range(len(ins)):
        if kind == "gather":
            m_per = ins[a].shape[0]
            mine = outs[a].at[pl.ds(me * m_per, m_per), :]
            local.append(pltpu.make_async_copy(ins[a], mine, local_sems.at[a]))
        else:
            local.append(pltpu.make_async_copy(ins[a].at[me], outs[a].at[0], local_sems.at[a]))
        for d in range(1, N_DEV):
            px, py, pc = _peer(d, mx, my, mc)
            if kind == "gather":
                src, dst = ins[a], mine
            else:
                src, dst = ins[a].at[4 * px + 2 * py + pc], outs[a].at[d]
            remote.append(
                pltpu.make_async_remote_copy(
                    src_ref=src,
                    dst_ref=dst,
                    send_sem=send_sems.at[a * 7 + d - 1],
                    recv_sem=recv_sems.at[a * 7 + d - 1],
                    device_id=(px, py, pc),
                    device_id_type=MESH,
                )
            )
    return remote, local


def _rider_start(*args):
    remote, local = _rider_copies(*args)
    for cp in local + remote:
        cp.start()


def _rider_wait(*args):
    remote, local = _rider_copies(*args)
    for cp in remote:
        cp.wait_recv()
    for cp in remote:
        cp.wait_send()
    for cp in local:
        cp.wait()


def _rider_scratch(n):
    return [pltpu.SemaphoreType.DMA((7 * n,)), pltpu.SemaphoreType.DMA((7 * n,)), pltpu.SemaphoreType.DMA((n,))]


def _gather_rows(v, name):
    n = v.shape[1]
    return _all_gather(jnp.broadcast_to(v, (8, n)), name, pltpu.VMEM)[:, 0, :]


def _adamw(w, g, m, v):
    m = ADAM_B1 * m + (1.0 - ADAM_B1) * g
    v = ADAM_B2 * v + (1.0 - ADAM_B2) * (g * g)
    m_hat = m / (1.0 - ADAM_B1**ADAM_STEP)
    v_hat = v / (1.0 - ADAM_B2**ADAM_STEP)
    delta = -ADAM_LR * (m_hat / (jnp.sqrt(v_hat) + ADAM_EPS) + ADAM_WD * w)
    return delta, m, v


def _sum_adam(recv, w, m, v, name):
    _, r_tot, c = recv.shape
    c_pad = -(-c // 128) * 128
    tr = r_tot
    while 8 * tr * c_pad * 4 > 6 * 1024 * 1024 and tr % 32 == 0:
        tr //= 2

    def body(r_ref, w_ref, m_ref, v_ref, g_ref, d_ref, nm_ref, nv_ref):
        g = r_ref[0].astype(F32)
        for k in range(1, N_DEV):
            g = g + r_ref[k].astype(F32)
        d, nm, nv = _adamw(w_ref[...], g, m_ref[...], v_ref[...])
        g_ref[...] = g
        d_ref[...] = d
        nm_ref[...] = nm
        nv_ref[...] = nv

    blk = pl.BlockSpec((tr, c), lambda i: (i, 0))
    return _pcall(
        body,
        name=name,
        grid=(r_tot // tr,),
        out_shape=[jax.ShapeDtypeStruct((r_tot, c), F32)] * 4,
        in_specs=[pl.BlockSpec((N_DEV, tr, c), lambda i: (0, i, 0)), blk, blk, blk],
        out_specs=[blk] * 4,
        compiler_params=_cp("parallel"),
    )(recv, w, m, v)


def _sum_adam_t(recv, w, m, v, name):
    _, c, r_tot = recv.shape
    tr = min(256, r_tot)

    def body(r_ref, w_ref, m_ref, v_ref, g_ref, d_ref, nm_ref, nv_ref):
        g = r_ref[0].astype(F32)
        for k in range(1, N_DEV):
            g = g + r_ref[k].astype(F32)
        d, nm, nv = _adamw(w_ref[...], g, m_ref[...], v_ref[...])
        g_ref[...] = g
        d_ref[...] = d
        nm_ref[...] = nm
        nv_ref[...] = nv

    blk = pl.BlockSpec((c, tr), lambda i: (0, i))
    return _pcall(
        body,
        name=name,
        grid=(r_tot // tr,),
        out_shape=[jax.ShapeDtypeStruct((c, r_tot), F32)] * 4,
        in_specs=[pl.BlockSpec((N_DEV, c, tr), lambda i: (0, 0, i)), blk, blk, blk],
        out_specs=[blk] * 4,
        compiler_params=_cp("parallel"),
    )(recv, w, m, v)


def _wada_adam(c_t, dada_cols, w, m, v):
    d_model, c = w.shape
    tr = min(256, d_model)

    def body(ct_ref, da_ref, w_ref, m_ref, v_ref, g_ref, d_ref, nm_ref, nv_ref):
        g = jnp.dot(ct_ref[...].astype(BF16), da_ref[...].astype(BF16), preferred_element_type=F32)
        d, nm, nv = _adamw(w_ref[...], g, m_ref[...], v_ref[...])
        g_ref[...] = g
        d_ref[...] = d
        nm_ref[...] = nm
        nv_ref[...] = nv

    blk = pl.BlockSpec((tr, c), lambda i: (i, 0))
    return _pcall(
        body,
        name="wada_adam",
        grid=(d_model // tr,),
        out_shape=[jax.ShapeDtypeStruct((d_model, c), F32)] * 4,
        in_specs=[pl.BlockSpec((tr, N_DEV), lambda i: (i, 0)), pl.BlockSpec((N_DEV, c), lambda i: (0, 0)), blk, blk, blk],
        out_specs=[blk] * 4,
        compiler_params=_cp("parallel"),
    )(c_t, dada_cols, w, m, v)


def _small_adam(gathered, w, m, v):
    p = w.shape[1]

    def body(a_ref, w_ref, m_ref, v_ref, g_ref, d_ref, nm_ref, nv_ref):
        g = a_ref[0:1, :]
        for k in range(1, N_DEV):
            g = g + a_ref[k : k + 1, :]
        d, nm, nv = _adamw(w_ref[...], g, m_ref[...], v_ref[...])
        g_ref[...] = g
        d_ref[...] = d
        nm_ref[...] = nm
        nv_ref[...] = nv

    return _pcall(
        body,
        name="small_adam",
        out_shape=[jax.ShapeDtypeStruct((1, p), F32)] * 4,
    )(gathered, w, m, v)


def _ada_fwd(c_all, w_ada, b_cols):
    c = w_ada.shape[1]

    def body(c_ref, w_ref, b_ref, o_ref):
        o_ref[...] = jnp.dot(c_ref[...].astype(BF16), w_ref[...].astype(BF16), preferred_element_type=F32) + b_ref[...]

    return _pcall(
        body,
        name="ada_fwd",
        out_shape=jax.ShapeDtypeStruct((N_DEV, c), F32),
        compiler_params=_cp(),
    )(c_all, w_ada, b_cols)


def _ln_mod(x, ada):
    s_len, d = x.shape
    tm = min(512, s_len)

    def body(x_ref, sh_ref, sc_ref, h_ref):
        xv = x_ref[...]
        mu = jnp.mean(xv, axis=-1, keepdims=True)
        xc = xv - mu
        var = jnp.mean(xc * xc, axis=-1, keepdims=True)
        xhat = xc * lax.rsqrt(var + LN_EPS)
        h_ref[...] = (xhat * (1.0 + sc_ref[...]) + sh_ref[...]).astype(BF16)

    return _pcall(
        body,
        name="ln_mod",
        grid=(s_len // tm,),
        out_shape=jax.ShapeDtypeStruct((s_len, d), BF16),
        in_specs=[
            pl.BlockSpec((tm, d), lambda i: (i, 0)),
            pl.BlockSpec((1, d), lambda i: (0, 0)),
            pl.BlockSpec((1, d), lambda i: (0, 1)),
        ],
        out_specs=pl.BlockSpec((tm, d), lambda i: (i, 0)),
        compiler_params=_cp("parallel"),
    )(x, ada, ada)


def _mm_cols(a, b, col_off, n_cols, out_dtype, name, ride=()):
    m, k = a.shape
    tm, tn = min(1024, m), 512
    off = col_off // tn
    ni, nj = m // tm, n_cols // tn
    n = len(ride)

    def body(a_ref, b_ref, *rest):
        ins, o_ref, outs, sems = rest[:n], rest[n], rest[n + 1 : 2 * n + 1], rest[2 * n + 1 :]
        i, j = pl.program_id(0), pl.program_id(1)
        if n:

            @pl.when((i == 0) & (j == 0))
            def _():
                _rider_start("gather", ins, outs, *sems)

        o_ref[...] = lax.dot_general(a_ref[...], b_ref[...], _NT, preferred_element_type=F32).astype(out_dtype)
        if n:

            @pl.when((i == ni - 1) & (j == nj - 1))
            def _():
                _rider_wait("gather", ins, outs, *sems)

    hbm = pl.BlockSpec(memory_space=pltpu.HBM)
    out = _pcall(
        body,
        name=name,
        grid=(ni, nj),
        out_shape=[jax.ShapeDtypeStruct((m, n_cols), out_dtype)]
        + [jax.ShapeDtypeStruct((N_DEV * r.shape[0], r.shape[1]), r.dtype) for r in ride],
        in_specs=[pl.BlockSpec((tm, k), lambda i, j: (i, 0)), pl.BlockSpec((tn, k), lambda i, j: (off + j, 0))] + [hbm] * n,
        out_specs=[pl.BlockSpec((tm, tn), lambda i, j: (i, j))] + [hbm] * n,
        scratch_shapes=_rider_scratch(n) if n else [],
        compiler_params=_cp("arbitrary", "arbitrary") if n else _cp("parallel", "parallel"),
    )(a, b, *ride)
    return out if n else out[0]


def _mm_tn(a, b, name, ride=()):
    s_len, m = a.shape
    n = b.shape[1]
    tm, tn, ts = min(1024, m), min(1024, n), min(2048, s_len)
    ni, nj, ns = m // tm, n // tn, s_len // ts
    nr = len(ride)

    def body(a_ref, b_ref, *rest):
        ins, o_ref, outs = rest[:nr], rest[nr], rest[nr + 1 : 2 * nr + 1]
        sems, acc_s = rest[2 * nr + 1 : -1], rest[-1]
        i, j, kk = pl.program_id(0), pl.program_id(1), pl.program_id(2)
        if nr:

            @pl.when((i == 0) & (j == 0) & (kk == 0))
            def _():
                _rider_start("exchange", ins, outs, *sems)

            @pl.when((i == ni - 1) & (j == nj - 1) & (kk == ns - 1))
            def _():
                _rider_wait("exchange", ins, outs, *sems)

        part = lax.dot_general(a_ref[...], b_ref[...], _TN, preferred_element_type=F32)

        @pl.when(kk == 0)
        def _():
            acc_s[...] = part

        @pl.when(kk > 0)
        def _():
            acc_s[...] += part

        @pl.when(kk == ns - 1)
        def _():
            o_ref[...] = acc_s[...].astype(BF16)

    hbm = pl.BlockSpec(memory_space=pltpu.HBM)
    out = _pcall(
        body,
        name=name,
        grid=(ni, nj, ns),
        out_shape=[jax.ShapeDtypeStruct((m, n), BF16)] + [jax.ShapeDtypeStruct(r.shape, r.dtype) for r in ride],
        in_specs=[pl.BlockSpec((ts, tm), lambda i, j, kk: (kk, i)), pl.BlockSpec((ts, tn), lambda i, j, kk: (kk, j))] + [hbm] * nr,
        out_specs=[pl.BlockSpec((tm, tn), lambda i, j, kk: (i, j))] + [hbm] * nr,
        scratch_shapes=(_rider_scratch(nr) if nr else []) + [pltpu.VMEM((tm, tn), F32)],
        compiler_params=_cp("arbitrary", "arbitrary", "arbitrary") if nr else _cp("parallel", "parallel", "arbitrary"),
    )(a, b, *ride)
    return out if nr else out[0]


def _split3(a):
    hi = a.astype(BF16)
    r1 = a - hi.astype(F32)
    mid = r1.astype(BF16)
    lo = (r1 - mid.astype(F32)).astype(BF16)
    return hi, mid, lo


def _dot_ones(a, tri):
    return sum(jnp.dot(t, tri, preferred_element_type=F32) for t in _split3(a))


def _log_sigmoid(x):
    return jnp.minimum(x, 0.0) - jnp.log1p(jnp.exp(-jnp.abs(x)))


def _fox_cum(flog_t, bf_col):
    s_len = flog_t.shape[1]

    def body(fl_ref, bf_ref, cum_ref):
        r = lax.broadcasted_iota(jnp.int32, (128, 128), 0)
        c = lax.broadcasted_iota(jnp.int32, (128, 128), 1)
        upper = (r <= c).astype(BF16)

        def step(t, carry):
            sl = pl.ds(pl.multiple_of(t * 128, 128), 128)
            lf = _log_sigmoid(fl_ref[:, sl] + bf_ref[...])
            cs = _dot_ones(lf, upper) + carry
            cum_ref[:, sl] = cs
            return cs[:, 127:128]

        lax.fori_loop(0, s_len // 128, step, jnp.zeros((FOX_H, 1), F32))

    return _pcall(body, name="fox_cum", out_shape=jax.ShapeDtypeStruct((FOX_H, s_len), F32))(flog_t, bf_col)


def _fox_gate_bwd(drow, dcol, flog_t, bf_col):
    s_len = flog_t.shape[1]
    n = s_len // 128

    def body(dr_ref, dc_ref, fl_ref, bf_ref, dfl_ref, dbf_ref):
        r = lax.broadcasted_iota(jnp.int32, (128, 128), 0)
        c = lax.broadcasted_iota(jnp.int32, (128, 128), 1)
        lower = (r >= c).astype(BF16)

        def step(t, carry):
            run, tot = carry
            sl = pl.ds(pl.multiple_of((n - 1 - t) * 128, 128), 128)
            rc = _dot_ones(dr_ref[:, sl] - dc_ref[:, sl], lower) + run
            dfl = rc * _sigmoid(-(fl_ref[:, sl] + bf_ref[...]))
            dfl_ref[:, sl] = dfl
            return rc[:, 0:1], tot + jnp.sum(dfl, axis=1, keepdims=True)

        zero = jnp.zeros((FOX_H, 1), F32)
        _, tot = lax.fori_loop(0, n, step, (zero, zero))
        dbf_ref[...] = jnp.broadcast_to(tot, (FOX_H, 128))

    return _pcall(
        body,
        name="fox_gate_bwd",
        out_shape=[jax.ShapeDtypeStruct((FOX_H, s_len), F32), jax.ShapeDtypeStruct((FOX_H, 128), F32)],
    )(drow, dcol, flog_t, bf_col)


def _diag_mask(blk, transposed=False):
    r = lax.broadcasted_iota(jnp.int32, (blk, blk), 0)
    c = lax.broadcasted_iota(jnp.int32, (blk, blk), 1)
    return c >= r if transposed else r >= c


_NT = (((1,), (1,)), ((), ()))
_TN = (((0,), (0,)), ((), ()))


def _fox_fwd(qkv, cum_row):
    s_len = qkv.shape[0]
    blk = min(ATT_BLK, s_len)
    nb = s_len // blk
    log2e = 1.4426950408889634

    def body(q_ref, k_ref, v_ref, c_ref, o_ref, lse_ref, mx_s, acc_s, u_s):
        i = pl.program_id(1)

        def key_cols(j, n):
            return pl.ds(pl.multiple_of(j * blk, blk), n * blk)

        def walk(tile):
            lax.fori_loop(0, i // 2, lambda t, c: (tile(2 * t, 2, False), c)[1], 0)

            @pl.when(i % 2 == 1)
            def _():
                tile(i - 1, 1, False)

            tile(i, 1, True)

        def lane_max(j, n, masked):
            cols = key_cols(j, n)
            u = lax.dot_general(q_ref[...], k_ref[cols, :], _NT, preferred_element_type=F32) * (FOX_SCALE * log2e) - c_ref[:, cols] * log2e
            if masked:
                u = jnp.where(_diag_mask(blk), u, NEG)
            u_s[:, cols] = u
            part = u[:, 0:128]
            for t in range(1, n * blk // 128):
                part = jnp.maximum(part, u[:, t * 128 : (t + 1) * 128])
            mx_s[...] = jnp.maximum(mx_s[...], part)

        mx_s[...] = jnp.full(mx_s.shape, NEG, F32)
        walk(lane_max)
        m = jnp.max(mx_s[...], axis=1, keepdims=True)

        def weigh(j, n, masked):
            cols = key_cols(j, n)
            p = jnp.exp2(u_s[:, cols] - m)
            ones_col = (lax.broadcasted_iota(jnp.int32, (n * blk, 128), 1) == 0).astype(BF16)
            v1 = jnp.concatenate([v_ref[cols, :], ones_col], axis=1)
            acc_s[...] += jnp.dot(p.astype(BF16), v1, preferred_element_type=F32)

        acc_s[...] = jnp.zeros(acc_s.shape, F32)
        walk(weigh)
        l = acc_s[:, FOX_DH : FOX_DH + 1]
        o_ref[...] = acc_s[:, :FOX_DH] / l
        lse_ref[...] = m * (1.0 / log2e) + jnp.log(l)

    return _pcall(
        body,
        name="fox_fwd",
        grid=(FOX_H, nb),
        out_shape=[jax.ShapeDtypeStruct((s_len, FOX_W), F32), jax.ShapeDtypeStruct((FOX_H, s_len, 1), F32)],
        in_specs=[
            pl.BlockSpec((blk, FOX_DH), lambda h, i: (i, h)),
            pl.BlockSpec((s_len, FOX_DH), lambda h, i: (0, FOX_H + h)),
            pl.BlockSpec((s_len, FOX_DH), lambda h, i: (0, 2 * FOX_H + h)),
            pl.BlockSpec((None, 1, s_len), lambda h, i: (h, 0, 0)),
        ],
        out_specs=[
            pl.BlockSpec((blk, FOX_DH), lambda h, i: (i, h)),
            pl.BlockSpec((None, blk, 1), lambda h, i: (h, i, 0)),
        ],
        scratch_shapes=[pltpu.VMEM((blk, 128), F32), pltpu.VMEM((blk, 2 * FOX_DH), F32), pltpu.VMEM((blk, s_len), F32)],
        compiler_params=_cp("parallel", "arbitrary"),
    )(qkv, qkv, qkv, cum_row)


def _fox_bwd(qkv, cum_col, lse_row, delta_row, do):
    s_len = qkv.shape[0]
    blk = min(ATT_BLK, s_len)
    nb = s_len // blk

    def body(q_ref, k_ref, v_ref, c_ref, lse_ref, dl_ref, do_ref, dq_ref, dk_ref, dv_ref, dc_ref, dr_ref, dk_s, dv_s, dc_s, cb_s):
        j = pl.program_id(1)

        @pl.when(j == 0)
        def _():
            dq_ref[...] = jnp.zeros(dq_ref.shape, F32)
            dr_ref[...] = jnp.zeros(dr_ref.shape, F32)

        dk_s[...] = jnp.zeros(dk_s.shape, F32)
        dv_s[...] = jnp.zeros(dv_s.shape, F32)
        dc_s[...] = jnp.zeros(dc_s.shape, F32)
        cb_s[...] = jnp.broadcast_to(c_ref[...], cb_s.shape)

        def tile(i, n, diag):
            rows = pl.ds(pl.multiple_of(i * blk, blk), n * blk)
            q, dob = q_ref[rows, :], do_ref[rows, :]
            k, v = k_ref[...], v_ref[...]
            s_t = lax.dot_general(k, q, _NT, preferred_element_type=F32) * FOX_SCALE - cb_s[:, : n * blk]
            p_t = jnp.exp(s_t - lse_ref[:, rows])
            if diag:
                p_t = jnp.where(_diag_mask(blk, transposed=True), p_t, 0.0)
            dp_t = lax.dot_general(v, dob, _NT, preferred_element_type=F32)
            ds_t = p_t * (dp_t - dl_ref[:, rows])
            dsb = ds_t.astype(BF16)
            dv_s[...] += jnp.dot(p_t.astype(BF16), dob, preferred_element_type=F32)
            dk_s[...] += jnp.dot(dsb, q, preferred_element_type=F32)
            dq_c = lax.dot_general(dsb, k, _TN, preferred_element_type=F32)
            part = ds_t[:, 0:128]
            for t in range(1, n * blk // 128):
                part = part + ds_t[:, t * 128 : (t + 1) * 128]
            dc_s[...] += part
            dr_ref[:, rows] += jnp.sum(ds_t, axis=0, keepdims=True)
            if diag:
                dq_ref[rows, :] = (dq_ref[rows, :] + dq_c) * FOX_SCALE
            else:
                dq_ref[rows, :] += dq_c

        tile(j, 1, True)
        odd = (nb - 1 - j) % 2

        @pl.when(odd == 1)
        def _():
            tile(j + 1, 1, False)

        lax.fori_loop(0, (nb - 1 - j) // 2, lambda t, c: (tile(j + 1 + odd + 2 * t, 2, False), c)[1], 0)
        dk_ref[...] = (dk_s[...] * FOX_SCALE).astype(BF16)
        dv_ref[...] = dv_s[...].astype(BF16)
        dc_ref[...] = jnp.sum(dc_s[...], axis=1, keepdims=True)

    head = lambda h, j: (0, h)
    row = pl.BlockSpec((None, 1, s_len), lambda h, j: (h, 0, 0))
    return _pcall(
        body,
        name="fox_bwd",
        grid=(FOX_H, nb),
        out_shape=[
            jax.ShapeDtypeStruct((s_len, FOX_W), F32),
            jax.ShapeDtypeStruct((s_len, FOX_W), BF16),
            jax.ShapeDtypeStruct((s_len, FOX_W), BF16),
            jax.ShapeDtypeStruct((FOX_H, s_len, 1), F32),
            jax.ShapeDtypeStruct((FOX_H, 1, s_len), F32),
        ],
        in_specs=[
            pl.BlockSpec((s_len, FOX_DH), head),
            pl.BlockSpec((blk, FOX_DH), lambda h, j: (j, FOX_H + h)),
            pl.BlockSpec((blk, FOX_DH), lambda h, j: (j, 2 * FOX_H + h)),
            pl.BlockSpec((None, blk, 1), lambda h, j: (h, j, 0)),
            row,
            row,
            pl.BlockSpec((s_len, FOX_DH), head),
        ],
        out_specs=[
            pl.BlockSpec((s_len, FOX_DH), head),
            pl.BlockSpec((blk, FOX_DH), lambda h, j: (j, h)),
            pl.BlockSpec((blk, FOX_DH), lambda h, j: (j, h)),
            pl.BlockSpec((None, blk, 1), lambda h, j: (h, j, 0)),
            row,
        ],
        scratch_shapes=[
            pltpu.VMEM((blk, FOX_DH), F32),
            pltpu.VMEM((blk, FOX_DH), F32),
            pltpu.VMEM((blk, 128), F32),
            pltpu.VMEM((blk, 2 * blk), F32),
        ],
        compiler_params=_cp("parallel", "arbitrary"),
    )(qkv, qkv, qkv, cum_col, lse_row, delta_row, do)


def _swa_bias():
    cols = SWA_G * WINDOW
    k = np.arange(2 * WINDOW)[:, None]
    q = np.arange(cols)[None, :]
    dist = (q % WINDOW) - k + WINDOW
    valid = (dist >= 0) & (dist < WINDOW)
    out = np.empty((2, SWA_HKV, 2 * WINDOW, cols), np.float32)
    for g in range(SWA_HKV):
        slope = np.array([SLOPES[g * SWA_G + t] for t in range(SWA_G)], np.float32)[q // WINDOW]
        bias = -(slope * dist.astype(np.float32))
        out[0, g] = np.where(valid & (k >= WINDOW), bias, np.float32(NEG))
        out[1, g] = np.where(valid, bias, np.float32(NEG))
    return jnp.asarray(out)


def _swa_group(i, q_ref, kk, sinks_ref, bias_ref, g):
    cols = SWA_G * WINDOW
    head = lax.broadcasted_iota(jnp.int32, (1, cols), 1) // WINDOW
    sink = jnp.zeros((1, cols), F32)
    for t in range(SWA_G):
        sink = jnp.where(head == t, sinks_ref[g * SWA_G + t], sink)
    q = jnp.concatenate([q_ref[:, (g * SWA_G + t) * SWA_DH : (g * SWA_G + t + 1) * SWA_DH] for t in range(SWA_G)], axis=0)
    k = kk[:, g * SWA_DH : (g + 1) * SWA_DH]
    s = lax.dot_general(k, q, _NT, preferred_element_type=F32) * SWA_SCALE + bias_ref[jnp.minimum(i, 1), g]
    m = jnp.maximum(jnp.max(s, axis=0, keepdims=True), sink)
    e = jnp.exp(s - m)
    e_sink = jnp.exp(sink - m)
    inv = 1.0 / (jnp.sum(e, axis=0, keepdims=True) + e_sink)
    return q, k, e * inv, e_sink * inv


def _swa_specs(col_q, col_k, col_v, rev, nb):
    def blk(t):
        return nb - 1 - t if rev else t

    return [
        pl.BlockSpec((WINDOW, SWA_W), lambda t: (blk(t), col_q)),
        pl.BlockSpec((WINDOW, SWA_KVW), lambda t: (jnp.maximum(blk(t) - 1, 0), col_k)),
        pl.BlockSpec((WINDOW, SWA_KVW), lambda t: (blk(t), col_k)),
        pl.BlockSpec((WINDOW, SWA_KVW), lambda t: (jnp.maximum(blk(t) - 1, 0), col_v)),
        pl.BlockSpec((WINDOW, SWA_KVW), lambda t: (blk(t), col_v)),
    ]


def _swa_fwd(qkv, sinks):
    s_len = qkv.shape[0]
    nb = s_len // WINDOW
    bias_spec = pl.BlockSpec((2, SWA_HKV, 2 * WINDOW, SWA_G * WINDOW), lambda t: (0, 0, 0, 0))

    def body(q_ref, kp_ref, kc_ref, vp_ref, vc_ref, sinks_ref, bias_ref, o_ref):
        i = pl.program_id(0)
        kk = jnp.concatenate([kp_ref[...], kc_ref[...]], axis=0)
        vv = jnp.concatenate([vp_ref[...], vc_ref[...]], axis=0)
        for g in range(SWA_HKV):
            _, _, p, _ = _swa_group(i, q_ref, kk, sinks_ref, bias_ref, g)
            o = lax.dot_general(p.astype(BF16), vv[:, g * SWA_DH : (g + 1) * SWA_DH], _TN, preferred_element_type=F32)
            for t in range(SWA_G):
                h = g * SWA_G + t
                o_ref[:, h * SWA_DH : (h + 1) * SWA_DH] = o[t * WINDOW : (t + 1) * WINDOW, :]

    return _pcall(
        body,
        name="swa_fwd",
        grid=(nb,),
        out_shape=jax.ShapeDtypeStruct((s_len, SWA_W), F32),
        in_specs=_swa_specs(0, 4, 5, False, nb) + [pl.BlockSpec(memory_space=pltpu.SMEM), bias_spec],
        out_specs=pl.BlockSpec((WINDOW, SWA_W), lambda t: (t, 0)),
        compiler_params=_cp("parallel"),
    )(qkv, qkv, qkv, qkv, qkv, sinks, _swa_bias())


def _swa_bwd(qkv, sinks, do):
    s_len = qkv.shape[0]
    nb = s_len // WINDOW
    bias_spec = pl.BlockSpec((2, SWA_HKV, 2 * WINDOW, SWA_G * WINDOW), lambda t: (0, 0, 0, 0))

    def body(q_ref, kp_ref, kc_ref, vp_ref, vc_ref, sinks_ref, bias_ref, do_ref, dq_ref, dk_ref, dv_ref, dsink_ref, ck_s, cv_s, dkk_s, dvv_s):
        t = pl.program_id(0)
        i = nb - 1 - t

        @pl.when(t == 0)
        def _():
            ck_s[...] = jnp.zeros(ck_s.shape, F32)
            cv_s[...] = jnp.zeros(cv_s.shape, F32)
            dsink_ref[...] = jnp.zeros(dsink_ref.shape, F32)

        kk = jnp.concatenate([kp_ref[...], kc_ref[...]], axis=0)
        vv = jnp.concatenate([vp_ref[...], vc_ref[...]], axis=0)
        lane = lax.broadcasted_iota(jnp.int32, (1, 128), 1)
        dsink = jnp.zeros((1, 128), F32)
        for g in range(SWA_HKV):
            cols = slice(g * SWA_DH, (g + 1) * SWA_DH)
            q, k, p, p_sink = _swa_group(i, q_ref, kk, sinks_ref, bias_ref, g)
            dob = jnp.concatenate([do_ref[:, (g * SWA_G + t) * SWA_DH : (g * SWA_G + t + 1) * SWA_DH] for t in range(SWA_G)], axis=0)
            dp = lax.dot_general(vv[:, cols], dob, _NT, preferred_element_type=F32)
            delta = jnp.sum(p * dp, axis=0, keepdims=True)
            dsb = (p * (dp - delta)).astype(BF16)
            dq = (lax.dot_general(dsb, k, _TN, preferred_element_type=F32) * SWA_SCALE).astype(BF16)
            ps_d = p_sink * delta
            for t in range(SWA_G):
                h = g * SWA_G + t
                dq_ref[:, h * SWA_DH : (h + 1) * SWA_DH] = dq[t * WINDOW : (t + 1) * WINDOW, :]
                dsink = dsink + jnp.where(lane == h, -jnp.sum(ps_d[:, t * WINDOW : (t + 1) * WINDOW], axis=1, keepdims=True), 0.0)
            dkk_s[:, cols] = jnp.dot(dsb, q, preferred_element_type=F32) * SWA_SCALE
            dvv_s[:, cols] = jnp.dot(p.astype(BF16), dob, preferred_element_type=F32)
        dk_ref[...] = (dkk_s[WINDOW:, :] + ck_s[...]).astype(BF16)
        dv_ref[...] = (dvv_s[WINDOW:, :] + cv_s[...]).astype(BF16)
        ck_s[...] = dkk_s[:WINDOW, :]
        cv_s[...] = dvv_s[:WINDOW, :]
        dsink_ref[...] += dsink

    row = lambda t: (nb - 1 - t, 0)
    return _pcall(
        body,
        name="swa_bwd",
        grid=(nb,),
        out_shape=[
            jax.ShapeDtypeStruct((s_len, SWA_W), BF16),
            jax.ShapeDtypeStruct((s_len, SWA_KVW), BF16),
            jax.ShapeDtypeStruct((s_len, SWA_KVW), BF16),
            jax.ShapeDtypeStruct((1, 128), F32),
        ],
        in_specs=_swa_specs(0, 4, 5, True, nb)
        + [pl.BlockSpec(memory_space=pltpu.SMEM), bias_spec, pl.BlockSpec((WINDOW, SWA_W), row)],
        out_specs=[
            pl.BlockSpec((WINDOW, SWA_W), row),
            pl.BlockSpec((WINDOW, SWA_KVW), row),
            pl.BlockSpec((WINDOW, SWA_KVW), row),
            pl.BlockSpec((1, 128), lambda t: (0, 0)),
        ],
        scratch_shapes=[
            pltpu.VMEM((WINDOW, SWA_KVW), F32),
            pltpu.VMEM((WINDOW, SWA_KVW), F32),
            pltpu.VMEM((2 * WINDOW, SWA_KVW), F32),
            pltpu.VMEM((2 * WINDOW, SWA_KVW), F32),
        ],
        compiler_params=_cp("arbitrary"),
    )(qkv, qkv, qkv, qkv, qkv, sinks, _swa_bias(), do)


def _branch_fwd(o, gates, g_blk, w_b, name):
    s_len, wd = o.shape
    d = w_b.shape[1]
    tm = min(512, s_len)

    def body(o_ref, g_ref, w_ref, y_ref, a_ref):
        g = g_ref[...].astype(F32)
        a = (o_ref[...] * (g * _sigmoid(g))).astype(BF16)
        a_ref[...] = a
        y_ref[...] = jnp.dot(a, w_ref[...], preferred_element_type=F32).astype(BF16)

    return _pcall(
        body,
        name=name,
        grid=(s_len // tm,),
        out_shape=[jax.ShapeDtypeStruct((s_len, d), BF16), jax.ShapeDtypeStruct((s_len, wd), BF16)],
        in_specs=[
            pl.BlockSpec((tm, wd), lambda i: (i, 0)),
            pl.BlockSpec((tm, wd), lambda i: (i, g_blk)),
            pl.BlockSpec((wd, d), lambda i: (0, 0)),
        ],
        out_specs=[pl.BlockSpec((tm, d), lambda i: (i, 0)), pl.BlockSpec((tm, wd), lambda i: (i, 0))],
        compiler_params=_cp("parallel"),
    )(o, gates, w_b)


def _out_stage(gates, mf_blk, y_fox, y_swa, w_out, x, ada, ln_g, ln_b, target):
    s_len, d = x.shape
    tm = min(256, s_len)
    n_steps = s_len // tm

    def body(mf_ref, ms_ref, yf_ref, ys_ref, w_ref, x_ref, gate_ref, lg_ref, lb_ref, t_ref, mg_ref, dza_ref, dsub_ref, red_ref):
        i = pl.program_id(0)
        merged = _sigmoid(mf_ref[...].astype(F32)) * yf_ref[...].astype(F32) + _sigmoid(ms_ref[...].astype(F32)) * ys_ref[...].astype(F32)
        mb = merged.astype(BF16)
        mg_ref[...] = mb
        sub = jnp.dot(mb, w_ref[...], preferred_element_type=F32)
        gate = gate_ref[...]
        z = ALPHA * x_ref[...] + gate * sub
        mu = jnp.mean(z, axis=-1, keepdims=True)
        zc = z - mu
        var = jnp.mean(zc * zc, axis=-1, keepdims=True)
        rstd = lax.rsqrt(var + LN_EPS)
        zhat = zc * rstd
        err = zhat * lg_ref[...] + lb_ref[...] - t_ref[...]
        dout = err * (1.0 / d)
        dzhat = dout * lg_ref[...]
        dz = rstd * (dzhat - jnp.mean(dzhat, axis=-1, keepdims=True) - zhat * jnp.mean(dzhat * zhat, axis=-1, keepdims=True))
        dza_ref[...] = ALPHA * dz
        dsub_ref[...] = (gate * dz).astype(BF16)
        part = jnp.concatenate(
            [
                jnp.sum(dz * sub, axis=0, keepdims=True),
                jnp.sum(dout * zhat, axis=0, keepdims=True),
                jnp.sum(dout, axis=0, keepdims=True),
                jnp.sum(err * err, axis=0, keepdims=True),
                jnp.zeros((4, d), F32),
            ],
            axis=0,
        )

        @pl.when(i == 0)
        def _():
            red_ref[...] = part

        @pl.when(i > 0)
        def _():
            red_ref[...] += part

        @pl.when(i == n_steps - 1)
        def _():
            red_ref[4:5, :] = jnp.broadcast_to(jnp.sum(red_ref[3:4, :], axis=1, keepdims=True), (1, d))

    row = pl.BlockSpec((tm, d), lambda i: (i, 0))
    vec = pl.BlockSpec((1, d), lambda i: (0, 0))
    return _pcall(
        body,
        name="out_stage",
        grid=(n_steps,),
        out_shape=[
            jax.ShapeDtypeStruct((s_len, d), BF16),
            jax.ShapeDtypeStruct((s_len, d), F32),
            jax.ShapeDtypeStruct((s_len, d), BF16),
            jax.ShapeDtypeStruct((8, d), F32),
        ],
        in_specs=[
            pl.BlockSpec((tm, d), lambda i: (i, mf_blk)),
            pl.BlockSpec((tm, d), lambda i: (i, mf_blk + 1)),
            row,
            row,
            pl.BlockSpec((d, d), lambda i: (0, 0), pipeline_mode=pl.Buffered(1)),
            row,
            pl.BlockSpec((1, d), lambda i: (0, 2)),
            vec,
            vec,
            row,
        ],
        out_specs=[row, row, row, pl.BlockSpec((8, d), lambda i: (0, 0))],
        compiler_params=_cp("arbitrary"),
    )(gates, gates, y_fox, y_swa, w_out, x, ada, ln_g, ln_b, target)


def _merge_bwd(dsub, w_out, gates, mf_blk, y_fox, y_swa):
    s_len, d = dsub.shape
    tm = min(256, s_len)

    def body(ds_ref, w_ref, mf_ref, ms_ref, yf_ref, ys_ref, dmf_ref, dms_ref, dyf_ref, dys_ref):
        dm = lax.dot_general(ds_ref[...], w_ref[...], _NT, preferred_element_type=F32)
        sf, ss = _sigmoid(mf_ref[...].astype(F32)), _sigmoid(ms_ref[...].astype(F32))
        dmf_ref[...] = (dm * yf_ref[...].astype(F32) * (sf * (1.0 - sf))).astype(BF16)
        dms_ref[...] = (dm * ys_ref[...].astype(F32) * (ss * (1.0 - ss))).astype(BF16)
        dyf_ref[...] = (dm * sf).astype(BF16)
        dys_ref[...] = (dm * ss).astype(BF16)

    row = pl.BlockSpec((tm, d), lambda i: (i, 0))
    return _pcall(
        body,
        name="merge_bwd",
        grid=(s_len // tm,),
        out_shape=[jax.ShapeDtypeStruct((s_len, d), BF16)] * 4,
        in_specs=[
            row,
            pl.BlockSpec((d, d), lambda i: (0, 0), pipeline_mode=pl.Buffered(1)),
            pl.BlockSpec((tm, d), lambda i: (i, mf_blk)),
            pl.BlockSpec((tm, d), lambda i: (i, mf_blk + 1)),
            row,
            row,
        ],
        out_specs=[row] * 4,
        compiler_params=_cp("parallel"),
    )(dsub, w_out, gates, gates, y_fox, y_swa)


def _branch_bwd(dy, w_b, o, gates, g_blk, name, n_heads):
    s_len, d = dy.shape
    wd = w_b.shape[0]
    tm = min(512, s_len)

    def body(dy_ref, w_ref, o_ref, g_ref, do_ref, dg_ref, *rest):
        da = lax.dot_general(dy_ref[...], w_ref[...], _NT, preferred_element_type=F32)
        g = g_ref[...].astype(F32)
        sg = _sigmoid(g)
        do = da * (g * sg)
        do_ref[...] = do.astype(BF16)
        o = o_ref[...]
        dg_ref[...] = (da * o * (sg * (1.0 + g * (1.0 - sg)))).astype(BF16)
        if n_heads:
            prod = do.astype(BF16).astype(F32) * o
            lane = lax.broadcasted_iota(jnp.int32, (1, 128), 1)
            delta = jnp.zeros((tm, 128), F32)
            for h in range(n_heads):
                dh = jnp.sum(prod[:, h * 128 : (h + 1) * 128], axis=1, keepdims=True)
                delta = delta + jnp.where(lane == h, dh, 0.0)
            rest[0][...] = delta

    out_shape = [jax.ShapeDtypeStruct((s_len, wd), BF16), jax.ShapeDtypeStruct((s_len, wd), BF16)]
    out_specs = [pl.BlockSpec((tm, wd), lambda i: (i, 0))] * 2
    if n_heads:
        out_shape.append(jax.ShapeDtypeStruct((s_len, 128), F32))
        out_specs.append(pl.BlockSpec((tm, 128), lambda i: (i, 0)))
    return _pcall(
        body,
        name=name,
        grid=(s_len // tm,),
        out_shape=out_shape,
        in_specs=[
            pl.BlockSpec((tm, d), lambda i: (i, 0)),
            pl.BlockSpec((wd, d), lambda i: (0, 0)),
            pl.BlockSpec((tm, wd), lambda i: (i, 0)),
            pl.BlockSpec((tm, wd), lambda i: (i, g_blk)),
        ],
        out_specs=out_specs,
        compiler_params=_cp("parallel"),
    )(dy, w_b, o, gates)


def _in_bwd(dproj, w_in_t, x, ada, dza, ride):
    s_len, d = x.shape
    k_tot = dproj.shape[1]
    tm, tk = min(512, s_len), 1024
    ni, nk = s_len // tm, k_tot // tk
    n = len(ride)

    def body(dp_ref, w_ref, x_ref, sc_ref, dza_ref, *rest):
        ins, (gx_ref, red_ref), outs = rest[:n], rest[n : n + 2], rest[n + 2 : 2 * n + 2]
        sems, acc_s = rest[2 * n + 2 : 2 * n + 5], rest[2 * n + 5]
        i, kk = pl.program_id(0), pl.program_id(1)

        @pl.when((i == 0) & (kk == 0))
        def _():
            _rider_start("exchange", ins, outs, *sems)

        @pl.when((i == ni - 1) & (kk == nk - 1))
        def _():
            _rider_wait("exchange", ins, outs, *sems)

        part = jnp.dot(dp_ref[...], w_ref[...], preferred_element_type=F32)

        @pl.when(kk == 0)
        def _():
            acc_s[...] = part

        @pl.when(kk > 0)
        def _():
            acc_s[...] += part

        @pl.when(kk == nk - 1)
        def _():
            dh = acc_s[...]
            xv = x_ref[...]
            mu = jnp.mean(xv, axis=-1, keepdims=True)
            xc = xv - mu
            var = jnp.mean(xc * xc, axis=-1, keepdims=True)
            rstd = lax.rsqrt(var + LN_EPS)
            xhat = xc * rstd
            dxhat = dh * (1.0 + sc_ref[...])
            dx = rstd * (dxhat - jnp.mean(dxhat, axis=-1, keepdims=True) - xhat * jnp.mean(dxhat * xhat, axis=-1, keepdims=True))
            gx_ref[...] = dza_ref[...] + dx
            part_r = jnp.concatenate(
                [jnp.sum(dh, axis=0, keepdims=True), jnp.sum(dh * xhat, axis=0, keepdims=True), jnp.zeros((6, d), F32)], axis=0
            )

            @pl.when(i == 0)
            def _():
                red_ref[...] = part_r

            @pl.when(i > 0)
            def _():
                red_ref[...] += part_r

    row = pl.BlockSpec((tm, d), lambda i, kk: (i, 0))
    hbm = pl.BlockSpec(memory_space=pltpu.HBM)
    return _pcall(
        body,
        name="in_bwd",
        grid=(ni, nk),
        out_shape=[jax.ShapeDtypeStruct((s_len, d), F32), jax.ShapeDtypeStruct((8, d), F32)]
        + [jax.ShapeDtypeStruct(r.shape, r.dtype) for r in ride],
        in_specs=[
            pl.BlockSpec((tm, tk), lambda i, kk: (i, kk)),
            pl.BlockSpec((tk, d), lambda i, kk: (kk, 0)),
            row,
            pl.BlockSpec((1, d), lambda i, kk: (0, 1)),
            row,
        ]
        + [hbm] * n,
        out_specs=[row, pl.BlockSpec((8, d), lambda i, kk: (0, 0))] + [hbm] * n,
        scratch_shapes=_rider_scratch(n) + [pltpu.VMEM((tm, d), F32)],
        compiler_params=_cp("arbitrary", "arbitrary"),
    )(dproj, w_in_t, x, ada, dza, *ride)


def _pad_lanes(v, n):
    return jnp.pad(v, ((0, 0), (0, n - v.shape[1])))


def kernel(x, c, w_ada, b_ada, w_in, b_f, attn_sinks, w_br_fox, w_br_swa, w_out, ln_g, ln_b, loss_target, m_w_ada, m_b_ada, m_w_in, m_b_f, m_attn_sinks, m_w_br_fox, m_w_br_swa, m_w_out, m_ln_g, m_ln_b, v_w_ada, v_b_ada, v_w_in, v_b_f, v_attn_sinks, v_w_br_fox, v_w_br_swa, v_w_out, v_ln_g, v_ln_b):
    x2, tgt = x[0], loss_target[0]
    s_len, d = x2.shape
    me = 4 * lax.axis_index("x") + 2 * lax.axis_index("y") + lax.axis_index("c")
    off_ms = OFF_MF + d
    in_pad = off_ms + d
    c_ada = w_ada.shape[2]
    c_in = w_in.shape[2]
    c_br = w_br_fox.shape[2]

    w_in_full = _all_gather(w_in[0].T.astype(BF16), "ag_w_in", pltpu.HBM).reshape(N_DEV * c_in, d)
    w_in_pad = jnp.concatenate(
        [w_in_full[:REAL_FLOG_END], jnp.zeros((FLOG_PAD - N_FLOG, d), BF16), w_in_full[REAL_FLOG_END:]], axis=0
    )
    k_cut = REAL_FLOG_END // c_in

    c_all = _gather_rows(c, "ag_c")
    b_cols = lax.dynamic_slice(b_ada, (0, me * c_ada), (1, c_ada))
    ada_cols = _ada_fwd(c_all, w_ada[0], b_cols)
    ada_g = _all_gather(ada_cols, "ag_ada", pltpu.VMEM)
    ada = lax.dynamic_index_in_dim(ada_g, me, axis=1, keepdims=False).reshape(1, N_DEV * c_ada)

    h = _ln_mod(x2, ada)
    qkv_fox = _mm_cols(h, w_in_pad, OFF_FQ, 3 * FOX_W, BF16, "proj_fox")
    flog = _mm_cols(h, w_in_pad, OFF_FLOG, FLOG_PAD, F32, "proj_flog")
    qkv_swa = _mm_cols(h, w_in_pad, OFF_SQ, SWA_W + 2 * SWA_KVW, BF16, "proj_swa")
    gates, w_bf, w_bs, w_o = _mm_cols(
        h, w_in_pad, OFF_GF, in_pad - OFF_GF, BF16, "proj_gates",
        ride=(w_br_fox[0].astype(BF16), w_br_swa[0].astype(BF16), w_out[0].astype(BF16)),
    )
    w_bf = w_bf.reshape(N_DEV, FOX_W, c_br).transpose(1, 0, 2).reshape(FOX_W, d)
    w_bs = w_bs.reshape(N_DEV, SWA_W, c_br).transpose(1, 0, 2).reshape(SWA_W, d)
    w_o = w_o.reshape(d, d)
    mf_blk = (OFF_MF - OFF_GF) // d

    flog_t = flog[:, :N_FLOG].T
    bf_col = b_f.reshape(FOX_H, 1)
    cum = _fox_cum(flog_t, bf_col)
    cum_row = cum.reshape(FOX_H, 1, s_len)
    o_fox, lse = _fox_fwd(qkv_fox, cum_row)
    sinks = attn_sinks.reshape(SWA_HQ)
    o_swa = _swa_fwd(qkv_swa, sinks)

    y_fox, a_fox = _branch_fwd(o_fox, gates, 0, w_bf, "branch_fox")
    y_swa, a_swa = _branch_fwd(o_swa, gates, 1, w_bs, "branch_swa")
    merged, dza, dsub, red = _out_stage(gates, mf_blk, y_fox, y_swa, w_o, x2, ada, ln_g, ln_b, tgt)
    loss = lax.psum(0.5 * red[4, 0] / d, ("x", "y", "c"))

    dmf, dms, dy_fox, dy_swa = _merge_bwd(dsub, w_o, gates, mf_blk, y_fox, y_swa)
    do_fox, dg_fox, delta = _branch_bwd(dy_fox, w_bf, o_fox, gates, 0, "branch_fox_bwd", FOX_H)
    do_swa, dg_swa = _branch_bwd(dy_swa, w_bs, o_swa, gates, 1, "branch_swa_bwd", 0)
    delta_row = delta[:, :FOX_H].T.reshape(FOX_H, 1, s_len)
    dq_f, dk_f, dv_f, dcol, drow = _fox_bwd(
        qkv_fox, cum.reshape(FOX_H, s_len, 1), lse.reshape(FOX_H, 1, s_len), delta_row, do_fox
    )
    dflog_t, dbf = _fox_gate_bwd(drow.reshape(FOX_H, s_len), dcol.reshape(FOX_H, s_len), flog_t, bf_col)
    dq_s, dk_s, dv_s, dsink = _swa_bwd(qkv_swa, sinks, do_swa)
    dflog = _pad_lanes(dflog_t.T, FLOG_PAD).astype(BF16)
    dproj = jnp.concatenate([dq_f.astype(BF16), dk_f, dv_f, dflog, dq_s, dk_s, dv_s, dg_fox, dg_swa, dmf, dms], axis=1)
    g_w_bf = _mm_tn(a_fox, dy_fox, "grad_w_br_fox")
    g_w_bs = _mm_tn(a_swa, dy_swa, "grad_w_br_swa")
    g_w_o = _mm_tn(merged, dsub, "grad_w_out")
    g_w_in, r_bf, r_bs, r_o = _mm_tn(
        dproj, h, "grad_w_in",
        ride=(
            g_w_bf.reshape(FOX_W, N_DEV, c_br).transpose(1, 0, 2),
            g_w_bs.reshape(SWA_W, N_DEV, c_br).transpose(1, 0, 2),
            g_w_o.reshape(N_DEV, d // N_DEV, d),
        ),
    )
    pad = FLOG_PAD - N_FLOG
    g_blocks = jnp.stack(
        [g_w_in[k * c_in : (k + 1) * c_in] for k in range(k_cut)]
        + [jnp.concatenate([g_w_in[k_cut * c_in : REAL_FLOG_END], g_w_in[OFF_SQ : (k_cut + 1) * c_in + pad]], axis=0)]
        + [g_w_in[k * c_in + pad : (k + 1) * c_in + pad] for k in range(k_cut + 1, N_DEV)]
    )

    grad_x, red2, r_in = _in_bwd(dproj, w_in_pad, x2, ada, dza, ride=(g_blocks,))
    out_w_in = _sum_adam_t(r_in, w_in[0].T, m_w_in[0].T, v_w_in[0].T, "adam_w_in")
    out_w_in = [o.T for o in out_w_in]
    out_w_bf = _sum_adam(r_bf, w_br_fox[0], m_w_br_fox[0], v_w_br_fox[0], "adam_w_br_fox")
    out_w_bs = _sum_adam(r_bs, w_br_swa[0], m_w_br_swa[0], v_w_br_swa[0], "adam_w_br_swa")
    out_w_o = _sum_adam(r_o, w_out[0], m_w_out[0], v_w_out[0], "adam_w_out")

    packed = jnp.concatenate([red2[0:1], red2[1:2], red[0:1], _pad_lanes(dbf[:, 0].reshape(1, FOX_H), 128), dsink, red[1:2], red[2:3]], axis=1)
    gathered = _gather_rows(packed, "ag_small")
    pack = lambda a, b, cc, dd, e: jnp.concatenate([a, _pad_lanes(b, 128), _pad_lanes(cc, 128), dd, e], axis=1)
    small = _small_adam(
        gathered,
        pack(b_ada, b_f, attn_sinks, ln_g, ln_b),
        pack(m_b_ada, m_b_f, m_attn_sinks, m_ln_g, m_ln_b),
        pack(v_b_ada, v_b_f, v_attn_sinks, v_ln_g, v_ln_b),
    )
    dada_cols = lax.dynamic_slice(gathered, (0, me * c_ada), (N_DEV, c_ada))
    out_w_ada = _wada_adam(c_all.T, dada_cols, w_ada[0], m_w_ada[0], v_w_ada[0])

    o1, o2, o3 = 3 * d, 3 * d + 128, 3 * d + 256

    def unpack(p):
        return p[:, :o1], p[:, o1 : o1 + FOX_H], p[:, o2 : o2 + SWA_HQ], p[:, o3 : o3 + d], p[:, o3 + d : o3 + 2 * d]

    kinds = []
    for k in range(4):
        b_ada_k, b_f_k, sinks_k, ln_g_k, ln_b_k = unpack(small[k])
        kinds.append(
            [out_w_ada[k][None], b_ada_k, out_w_in[k][None], b_f_k, sinks_k, out_w_bf[k][None], out_w_bs[k][None], out_w_o[k][None], ln_g_k, ln_b_k]
        )
    return (loss, grad_x[None], *kinds[0], *kinds[1], *kinds[2], *kinds[3])
```

```python
import numpy as np
import jax
import jax.numpy as jnp
from jax import lax
from jax.experimental import pallas as pl
from jax.experimental.pallas import tpu as pltpu

F32 = jnp.float32
BF16 = jnp.bfloat16
N_DEV = 8
MESH = pl.DeviceIdType.MESH

FOX_H, FOX_DH, FOX_W = 8, 128, 1024
SWA_HQ, SWA_HKV, SWA_DH, SWA_G = 16, 4, 64, 4
SWA_W, SWA_KVW, WINDOW = 1024, 256, 128
LN_EPS = 1e-5
NEG = -1e30
DEPTH = 1
ALPHA = (2.0 * DEPTH) ** 0.25
FOX_SCALE = FOX_DH ** -0.5
SWA_SCALE = SWA_DH ** -0.5
SLOPES = [2.0 ** (-8.0 * (h + 1.0) / SWA_HQ) for h in range(SWA_HQ)]

ADAM_LR, ADAM_B1, ADAM_B2, ADAM_EPS, ADAM_WD, ADAM_STEP = 0.001, 0.9, 0.999, 1e-08, 0.01, 10

N_FLOG = 8
FLOG_PAD = 512
OFF_FQ, OFF_FK, OFF_FV, OFF_FLOG = 0, 1024, 2048, 3072
OFF_SQ = OFF_FLOG + FLOG_PAD
OFF_SK = OFF_SQ + SWA_W
OFF_SV = OFF_SK + SWA_KVW
OFF_GF = OFF_SV + SWA_KVW
OFF_GS = OFF_GF + FOX_W
OFF_MF = OFF_GS + SWA_W
REAL_FLOG_END = OFF_FLOG + N_FLOG

ATT_BLK = 512
VMEM_LIMIT = 58 * 1024 * 1024


def _pcall(body, **kw):
    return pl.pallas_call(body, **kw)


def _cp(*sem):
    return pltpu.CompilerParams(dimension_semantics=sem, vmem_limit_bytes=VMEM_LIMIT)


def _sigmoid(x):
    return 0.5 * jnp.tanh(0.5 * x) + 0.5


def _all_gather(x, name, space):
    m_per, n = x.shape

    def body(x_ref, out_ref, send_sems, recv_sems, local_sem):
        mx, my, mc = lax.axis_index("x"), lax.axis_index("y"), lax.axis_index("c")
        me, sibling = (mx, my, mc), (mx, my, 1 - mc)
        chips = [(1 - mx, my), (mx, 1 - my), (1 - mx, 1 - my)]

        def rows(px, py, pc):
            return out_ref.at[4 * px + 2 * py + pc]

        def copy(k, block, to, src=None):
            return pltpu.make_async_remote_copy(
                src_ref=rows(*block) if src is None else src,
                dst_ref=rows(*block),
                send_sem=send_sems.at[k],
                recv_sem=recv_sems.at[k],
                device_id=to,
                device_id_type=MESH,
            )

        mine = pltpu.make_async_copy(x_ref, rows(*me), local_sem)
        mine.start()
        first = [copy(0, me, sibling, src=x_ref)]
        first += [copy(1 + j, me, (*chip, mc), src=x_ref) for j, chip in enumerate(chips)]
        for cp in first:
            cp.start()
        passed = [copy(4 + j, (*chip, mc), sibling) for j, chip in enumerate(chips)]
        for j, chip in enumerate(chips):
            copy(1 + j, (*chip, mc), me).wait_recv()
            passed[j].start()
        copy(0, sibling, me).wait_recv()
        for j, chip in enumerate(chips):
            copy(4 + j, (*chip, 1 - mc), me).wait_recv()
        for cp in first + passed:
            cp.wait_send()
        mine.wait()

    return _pcall(
        body,
        name=name,
        out_shape=jax.ShapeDtypeStruct((N_DEV, m_per, n), x.dtype),
        in_specs=[pl.BlockSpec(memory_space=space)],
        out_specs=pl.BlockSpec(memory_space=space),
        scratch_shapes=[pltpu.SemaphoreType.DMA((7,)), pltpu.SemaphoreType.DMA((7,)), pltpu.SemaphoreType.DMA],
    )(x)


def _peer(d, mx, my, mc):
    return (1 - mx if (d >> 2) & 1 else mx, 1 - my if (d >> 1) & 1 else my, 1 - mc if d & 1 else mc)


def _rider_copies(kind, ins, outs, send_sems, recv_sems, local_sems):
    mx, my, mc = lax.axis_index("x"), lax.axis_index("y"), lax.axis_index("c")
    me = 4 * mx + 2 * my + mc
    remote, local = [], []
    for a in range(len(ins)):
        if kind == "gather":
            m_per = ins[a].shape[0]
            mine = outs[a].at[pl.ds(me * m_per, m_per), :]
            local.append(pltpu.make_async_copy(ins[a], mine, local_sems.at[a]))
        else:
            local.append(pltpu.make_async_copy(ins[a].at[me], outs[a].at[0], local_sems.at[a]))
        for d in range(1, N_DEV):
            px, py, pc = _peer(d, mx, my, mc)
            if kind == "gather":
                src, dst = ins[a], mine
            else:
                src, dst = ins[a].at[4 * px + 2 * py + pc], outs[a].at[d]
            remote.append(
                pltpu.make_async_remote_copy(
                    src_ref=src,
                    dst_ref=dst,
                    send_sem=send_sems.at[a * 7 + d - 1],
                    recv_sem=recv_sems.at[a * 7 + d - 1],
                    device_id=(px, py, pc),
                    device_id_type=MESH,
                )
            )
    return remote, local


def _rider_start(*args):
    remote, local = _rider_copies(*args)
    for cp in local + remote:
        cp.start()


def _rider_wait(*args):
    remote, local = _rider_copies(*args)
    for cp in remote:
        cp.wait_recv()
    for cp in remote:
        cp.wait_send()
    for cp in local:
        cp.wait()


def _rider_scratch(n):
    return [pltpu.SemaphoreType.DMA((7 * n,)), pltpu.SemaphoreType.DMA((7 * n,)), pltpu.SemaphoreType.DMA((n,))]


def _gather_rows(v, name):
    n = v.shape[1]
    return _all_gather(jnp.broadcast_to(v, (8, n)), name, pltpu.VMEM)[:, 0, :]


def _adamw(w, g, m, v):
    m = ADAM_B1 * m + (1.0 - ADAM_B1) * g
    v = ADAM_B2 * v + (1.0 - ADAM_B2) * (g * g)
    m_hat = m / (1.0 - ADAM_B1**ADAM_STEP)
    v_hat = v / (1.0 - ADAM_B2**ADAM_STEP)
    delta = -ADAM_LR * (m_hat / (jnp.sqrt(v_hat) + ADAM_EPS) + ADAM_WD * w)
    return delta, m, v


def _sum_adam(recv, w, m, v, name):
    _, r_tot, c = recv.shape
    c_pad = -(-c // 128) * 128
    tr = r_tot
    while 8 * tr * c_pad * 4 > 6 * 1024 * 1024 and tr % 32 == 0:
        tr //= 2

    def body(r_ref, w_ref, m_ref, v_ref, g_ref, d_ref, nm_ref, nv_ref):
        g = r_ref[0].astype(F32)
        for k in range(1, N_DEV):
            g = g + r_ref[k].astype(F32)
        d, nm, nv = _adamw(w_ref[...], g, m_ref[...], v_ref[...])
        g_ref[...] = g
        d_ref[...] = d
        nm_ref[...] = nm
        nv_ref[...] = nv

    blk = pl.BlockSpec((tr, c), lambda i: (i, 0))
    return _pcall(
        body,
        name=name,
        grid=(r_tot // tr,),
        out_shape=[jax.ShapeDtypeStruct((r_tot, c), F32)] * 4,
        in_specs=[pl.BlockSpec((N_DEV, tr, c), lambda i: (0, i, 0)), blk, blk, blk],
        out_specs=[blk] * 4,
        compiler_params=_cp("parallel"),
    )(recv, w, m, v)


def _sum_adam_t(recv, w, m, v, name):
    _, c, r_tot = recv.shape
    tr = min(256, r_tot)

    def body(r_ref, w_ref, m_ref, v_ref, g_ref, d_ref, nm_ref, nv_ref):
        g = r_ref[0].astype(F32)
        for k in range(1, N_DEV):
            g = g + r_ref[k].astype(F32)
        d, nm, nv = _adamw(w_ref[...], g, m_ref[...], v_ref[...])
        g_ref[...] = g
        d_ref[...] = d
        nm_ref[...] = nm
        nv_ref[...] = nv

    blk = pl.BlockSpec((c, tr), lambda i: (0, i))
    return _pcall(
        body,
        name=name,
        grid=(r_tot // tr,),
        out_shape=[jax.ShapeDtypeStruct((c, r_tot), F32)] * 4,
        in_specs=[pl.BlockSpec((N_DEV, c, tr), lambda i: (0, 0, i)), blk, blk, blk],
        out_specs=[blk] * 4,
        compiler_params=_cp("parallel"),
    )(recv, w, m, v)


def _wada_adam(c_t, dada_cols, w, m, v):
    d_model, c = w.shape
    tr = min(256, d_model)

    def body(ct_ref, da_ref, w_ref, m_ref, v_ref, g_ref, d_ref, nm_ref, nv_ref):
        g = jnp.dot(ct_ref[...].astype(BF16), da_ref[...].astype(BF16), preferred_element_type=F32)
        d, nm, nv = _adamw(w_ref[...], g, m_ref[...], v_ref[...])
        g_ref[...] = g
        d_ref[...] = d
        nm_ref[...] = nm
        nv_ref[...] = nv

    blk = pl.BlockSpec((tr, c), lambda i: (i, 0))
    return _pcall(
        body,
        name="wada_adam",
        grid=(d_model // tr,),
        out_shape=[jax.ShapeDtypeStruct((d_model, c), F32)] * 4,
        in_specs=[pl.BlockSpec((tr, N_DEV), lambda i: (i, 0)), pl.BlockSpec((N_DEV, c), lambda i: (0, 0)), blk, blk, blk],
        out_specs=[blk] * 4,
        compiler_params=_cp("parallel"),
    )(c_t, dada_cols, w, m, v)


def _small_adam(gathered, w, m, v):
    p = w.shape[1]

    def body(a_ref, w_ref, m_ref, v_ref, g_ref, d_ref, nm_ref, nv_ref):
        g = a_ref[0:1, :]
        for k in range(1, N_DEV):
            g = g + a_ref[k : k + 1, :]
        d, nm, nv = _adamw(w_ref[...], g, m_ref[...], v_ref[...])
        g_ref[...] = g
        d_ref[...] = d
        nm_ref[...] = nm
        nv_ref[...] = nv

    return _pcall(
        body,
        name="small_adam",
        out_shape=[jax.ShapeDtypeStruct((1, p), F32)] * 4,
    )(gathered, w, m, v)


def _ada_fwd(c_all, w_ada, b_cols):
    c = w_ada.shape[1]

    def body(c_ref, w_ref, b_ref, o_ref):
        o_ref[...] = jnp.dot(c_ref[...].astype(BF16), w_ref[...].astype(BF16), preferred_element_type=F32) + b_ref[...]

    return _pcall(
        body,
        name="ada_fwd",
        out_shape=jax.ShapeDtypeStruct((N_DEV, c), F32),
        compiler_params=_cp(),
    )(c_all, w_ada, b_cols)


def _ln_mod(x, ada):
    s_len, d = x.shape
    tm = min(512, s_len)

    def body(x_ref, sh_ref, sc_ref, h_ref):
        xv = x_ref[...]
        mu = jnp.mean(xv, axis=-1, keepdims=True)
        xc = xv - mu
        var = jnp.mean(xc * xc, axis=-1, keepdims=True)
        xhat = xc * lax.rsqrt(var + LN_EPS)
        h_ref[...] = (xhat * (1.0 + sc_ref[...]) + sh_ref[...]).astype(BF16)

    return _pcall(
        body,
        name="ln_mod",
        grid=(s_len // tm,),
        out_shape=jax.ShapeDtypeStruct((s_len, d), BF16),
        in_specs=[
            pl.BlockSpec((tm, d), lambda i: (i, 0)),
            pl.BlockSpec((1, d), lambda i: (0, 0)),
            pl.BlockSpec((1, d), lambda i: (0, 1)),
        ],
        out_specs=pl.BlockSpec((tm, d), lambda i: (i, 0)),
        compiler_params=_cp("parallel"),
    )(x, ada, ada)


def _mm_cols(a, b, col_off, n_cols, out_dtype, name, ride=()):
    m, k = a.shape
    tm, tn = min(1024, m), 512
    off = col_off // tn
    ni, nj = m // tm, n_cols // tn
    n = len(ride)

    def body(a_ref, b_ref, *rest):
        ins, o_ref, outs, sems = rest[:n], rest[n], rest[n + 1 : 2 * n + 1], rest[2 * n + 1 :]
        i, j = pl.program_id(0), pl.program_id(1)
        if n:

            @pl.when((i == 0) & (j == 0))
            def _():
                _rider_start("gather", ins, outs, *sems)

        o_ref[...] = lax.dot_general(a_ref[...], b_ref[...], _NT, preferred_element_type=F32).astype(out_dtype)
        if n:

            @pl.when((i == ni - 1) & (j == nj - 1))
            def _():
                _rider_wait("gather", ins, outs, *sems)

    hbm = pl.BlockSpec(memory_space=pltpu.HBM)
    out = _pcall(
        body,
        name=name,
        grid=(ni, nj),
        out_shape=[jax.ShapeDtypeStruct((m, n_cols), out_dtype)]
        + [jax.ShapeDtypeStruct((N_DEV * r.shape[0], r.shape[1]), r.dtype) for r in ride],
        in_specs=[pl.BlockSpec((tm, k), lambda i, j: (i, 0)), pl.BlockSpec((tn, k), lambda i, j: (off + j, 0))] + [hbm] * n,
        out_specs=[pl.BlockSpec((tm, tn), lambda i, j: (i, j))] + [hbm] * n,
        scratch_shapes=_rider_scratch(n) if n else [],
        compiler_params=_cp("arbitrary", "arbitrary") if n else _cp("parallel", "parallel"),
    )(a, b, *ride)
    return out if n else out[0]


def _mm_tn(a, b, name, ride=()):
    s_len, m = a.shape
    n = b.shape[1]
    tm, tn, ts = min(1024, m), min(1024, n), min(2048, s_len)
    ni, nj, ns = m // tm, n // tn, s_len // ts
    nr = len(ride)

    def body(a_ref, b_ref, *rest):
        ins, o_ref, outs = rest[:nr], rest[nr], rest[nr + 1 : 2 * nr + 1]
        sems, acc_s = rest[2 * nr + 1 : -1], rest[-1]
        i, j, kk = pl.program_id(0), pl.program_id(1), pl.program_id(2)
        if nr:

            @pl.when((i == 0) & (j == 0) & (kk == 0))
            def _():
                _rider_start("exchange", ins, outs, *sems)

            @pl.when((i == ni - 1) & (j == nj - 1) & (kk == ns - 1))
            def _():
                _rider_wait("exchange", ins, outs, *sems)

        part = lax.dot_general(a_ref[...], b_ref[...], _TN, preferred_element_type=F32)

        @pl.when(kk == 0)
        def _():
            acc_s[...] = part

        @pl.when(kk > 0)
        def _():
            acc_s[...] += part

        @pl.when(kk == ns - 1)
        def _():
            o_ref[...] = acc_s[...].astype(BF16)

    hbm = pl.BlockSpec(memory_space=pltpu.HBM)
    out = _pcall(
        body,
        name=name,
        grid=(ni, nj, ns),
        out_shape=[jax.ShapeDtypeStruct((m, n), BF16)] + [jax.ShapeDtypeStruct(r.shape, r.dtype) for r in ride],
        in_specs=[pl.BlockSpec((ts, tm), lambda i, j, kk: (kk, i)), pl.BlockSpec((ts, tn), lambda i, j, kk: (kk, j))] + [hbm] * nr,
        out_specs=[pl.BlockSpec((tm, tn), lambda i, j, kk: (i, j))] + [hbm] * nr,
        scratch_shapes=(_rider_scratch(nr) if nr else []) + [pltpu.VMEM((tm, tn), F32)],
        compiler_params=_cp("arbitrary", "arbitrary", "arbitrary") if nr else _cp("parallel", "parallel", "arbitrary"),
    )(a, b, *ride)
    return out if nr else out[0]


def _split3(a):
    hi = a.astype(BF16)
    r1 = a - hi.astype(F32)
    mid = r1.astype(BF16)
    lo = (r1 - mid.astype(F32)).astype(BF16)
    return hi, mid, lo


def _dot_ones(a, tri):
    return sum(jnp.dot(t, tri, preferred_element_type=F32) for t in _split3(a))


def _log_sigmoid(x):
    return jnp.minimum(x, 0.0) - jnp.log1p(jnp.exp(-jnp.abs(x)))


def _fox_cum(flog_t, bf_col):
    s_len = flog_t.shape[1]

    def body(fl_ref, bf_ref, cum_ref):
        r = lax.broadcasted_iota(jnp.int32, (128, 128), 0)
        c = lax.broadcasted_iota(jnp.int32, (128, 128), 1)
        upper = (r <= c).astype(BF16)

        def step(t, carry):
            sl = pl.ds(pl.multiple_of(t * 128, 128), 128)
            lf = _log_sigmoid(fl_ref[:, sl] + bf_ref[...])
            cs = _dot_ones(lf, upper) + carry
            cum_ref[:, sl] = cs
            return cs[:, 127:128]

        lax.fori_loop(0, s_len // 128, step, jnp.zeros((FOX_H, 1), F32))

    return _pcall(body, name="fox_cum", out_shape=jax.ShapeDtypeStruct((FOX_H, s_len), F32))(flog_t, bf_col)


def _fox_gate_bwd(drow, dcol, flog_t, bf_col):
    s_len = flog_t.shape[1]
    n = s_len // 128

    def body(dr_ref, dc_ref, fl_ref, bf_ref, dfl_ref, dbf_ref):
        r = lax.broadcasted_iota(jnp.int32, (128, 128), 0)
        c = lax.broadcasted_iota(jnp.int32, (128, 128), 1)
        lower = (r >= c).astype(BF16)

        def step(t, carry):
            run, tot = carry
            sl = pl.ds(pl.multiple_of((n - 1 - t) * 128, 128), 128)
            rc = _dot_ones(dr_ref[:, sl] - dc_ref[:, sl], lower) + run
            dfl = rc * _sigmoid(-(fl_ref[:, sl] + bf_ref[...]))
            dfl_ref[:, sl] = dfl
            return rc[:, 0:1], tot + jnp.sum(dfl, axis=1, keepdims=True)

        zero = jnp.zeros((FOX_H, 1), F32)
        _, tot = lax.fori_loop(0, n, step, (zero, zero))
        dbf_ref[...] = jnp.broadcast_to(tot, (FOX_H, 128))

    return _pcall(
        body,
        name="fox_gate_bwd",
        out_shape=[jax.ShapeDtypeStruct((FOX_H, s_len), F32), jax.ShapeDtypeStruct((FOX_H, 128), F32)],
    )(drow, dcol, flog_t, bf_col)


def _diag_mask(blk, transposed=False):
    r = lax.broadcasted_iota(jnp.int32, (blk, blk), 0)
    c = lax.broadcasted_iota(jnp.int32, (blk, blk), 1)
    return c >= r if transposed else r >= c


_NT = (((1,), (1,)), ((), ()))
_TN = (((0,), (0,)), ((), ()))


def _fox_fwd(qkv, cum_row):
    s_len = qkv.shape[0]
    blk = min(ATT_BLK, s_len)
    nb = s_len // blk
    log2e = 1.4426950408889634

    def body(q_ref, k_ref, v_ref, c_ref, o_ref, lse_ref, mx_s, acc_s, u_s):
        i = pl.program_id(1)

        def key_cols(j, n):
            return pl.ds(pl.multiple_of(j * blk, blk), n * blk)

        def walk(tile):
            lax.fori_loop(0, i // 2, lambda t, c: (tile(2 * t, 2, False), c)[1], 0)

            @pl.when(i % 2 == 1)
            def _():
                tile(i - 1, 1, False)

            tile(i, 1, True)

        def lane_max(j, n, masked):
            cols = key_cols(j, n)
            u = lax.dot_general(q_ref[...], k_ref[cols, :], _NT, preferred_element_type=F32) * (FOX_SCALE * log2e) - c_ref[:, cols] * log2e
            if masked:
                u = jnp.where(_diag_mask(blk), u, NEG)
            u_s[:, cols] = u
            part = u[:, 0:128]
            for t in range(1, n * blk // 128):
                part = jnp.maximum(part, u[:, t * 128 : (t + 1) * 128])
            mx_s[...] = jnp.maximum(mx_s[...], part)

        mx_s[...] = jnp.full(mx_s.shape, NEG, F32)
        walk(lane_max)
        m = jnp.max(mx_s[...], axis=1, keepdims=True)

        def weigh(j, n, masked):
            cols = key_cols(j, n)
            p = jnp.exp2(u_s[:, cols] - m)
            ones_col = (lax.broadcasted_iota(jnp.int32, (n * blk, 128), 1) == 0).astype(BF16)
            v1 = jnp.concatenate([v_ref[cols, :], ones_col], axis=1)
            acc_s[...] += jnp.dot(p.astype(BF16), v1, preferred_element_type=F32)

        acc_s[...] = jnp.zeros(acc_s.shape, F32)
        walk(weigh)
        l = acc_s[:, FOX_DH : FOX_DH + 1]
        o_ref[...] = acc_s[:, :FOX_DH] / l
        lse_ref[...] = m * (1.0 / log2e) + jnp.log(l)

    return _pcall(
        body,
        name="fox_fwd",
        grid=(FOX_H, nb),
        out_shape=[jax.ShapeDtypeStruct((s_len, FOX_W), F32), jax.ShapeDtypeStruct((FOX_H, s_len, 1), F32)],
        in_specs=[
            pl.BlockSpec((blk, FOX_DH), lambda h, i: (i, h)),
            pl.BlockSpec((s_len, FOX_DH), lambda h, i: (0, FOX_H + h)),
            pl.BlockSpec((s_len, FOX_DH), lambda h, i: (0, 2 * FOX_H + h)),
            pl.BlockSpec((None, 1, s_len), lambda h, i: (h, 0, 0)),
        ],
        out_specs=[
            pl.BlockSpec((blk, FOX_DH), lambda h, i: (i, h)),
            pl.BlockSpec((None, blk, 1), lambda h, i: (h, i, 0)),
        ],
        scratch_shapes=[pltpu.VMEM((blk, 128), F32), pltpu.VMEM((blk, 2 * FOX_DH), F32), pltpu.VMEM((blk, s_len), F32)],
        compiler_params=_cp("parallel", "arbitrary"),
    )(qkv, qkv, qkv, cum_row)


def _fox_bwd(qkv, cum_col, lse_row, delta_row, do):
    s_len = qkv.shape[0]
    blk = min(ATT_BLK, s_len)
    nb = s_len // blk

    def body(q_ref, k_ref, v_ref, c_ref, lse_ref, dl_ref, do_ref, dq_ref, dk_ref, dv_ref, dc_ref, dr_ref, dk_s, dv_s, dc_s, cb_s):
        j = pl.program_id(1)

        @pl.when(j == 0)
        def _():
            dq_ref[...] = jnp.zeros(dq_ref.shape, F32)
            dr_ref[...] = jnp.zeros(dr_ref.shape, F32)

        dk_s[...] = jnp.zeros(dk_s.shape, F32)
        dv_s[...] = jnp.zeros(dv_s.shape, F32)
        dc_s[...] = jnp.zeros(dc_s.shape, F32)
        cb_s[...] = jnp.broadcast_to(c_ref[...], cb_s.shape)

        def tile(i, n, diag):
            rows = pl.ds(pl.multiple_of(i * blk, blk), n * blk)
            q, dob = q_ref[rows, :], do_ref[rows, :]
            k, v = k_ref[...], v_ref[...]
            s_t = lax.dot_general(k, q, _NT, preferred_element_type=F32) * FOX_SCALE - cb_s[:, : n * blk]
            p_t = jnp.exp(s_t - lse_ref[:, rows])
            if diag:
                p_t = jnp.where(_diag_mask(blk, transposed=True), p_t, 0.0)
            dp_t = lax.dot_general(v, dob, _NT, preferred_element_type=F32)
            ds_t = p_t * (dp_t - dl_ref[:, rows])
            dsb = ds_t.astype(BF16)
            dv_s[...] += jnp.dot(p_t.astype(BF16), dob, preferred_element_type=F32)
            dk_s[...] += jnp.dot(dsb, q, preferred_element_type=F32)
            dq_c = lax.dot_general(dsb, k, _TN, preferred_element_type=F32)
            part = ds_t[:, 0:128]
            for t in range(1, n * blk // 128):
                part = part + ds_t[:, t * 128 : (t + 1) * 128]
            dc_s[...] += part
            dr_ref[:, rows] += jnp.sum(ds_t, axis=0, keepdims=True)
            if diag:
                dq_ref[rows, :] = (dq_ref[rows, :] + dq_c) * FOX_SCALE
            else:
                dq_ref[rows, :] += dq_c

        tile(j, 1, True)
        odd = (nb - 1 - j) % 2

        @pl.when(odd == 1)
        def _():
            tile(j + 1, 1, False)

        lax.fori_loop(0, (nb - 1 - j) // 2, lambda t, c: (tile(j + 1 + odd + 2 * t, 2, False), c)[1], 0)
        dk_ref[...] = (dk_s[...] * FOX_SCALE).astype(BF16)
        dv_ref[...] = dv_s[...].astype(BF16)
        dc_ref[...] = jnp.sum(dc_s[...], axis=1, keepdims=True)

    head = lambda h, j: (0, h)
    row = pl.BlockSpec((None, 1, s_len), lambda h, j: (h, 0, 0))
    return _pcall(
        body,
        name="fox_bwd",
        grid=(FOX_H, nb),
        out_shape=[
            jax.ShapeDtypeStruct((s_len, FOX_W), F32),
            jax.ShapeDtypeStruct((s_len, FOX_W), BF16),
            jax.ShapeDtypeStruct((s_len, FOX_W), BF16),
            jax.ShapeDtypeStruct((FOX_H, s_len, 1), F32),
            jax.ShapeDtypeStruct((FOX_H, 1, s_len), F32),
        ],
        in_specs=[
            pl.BlockSpec((s_len, FOX_DH), head),
            pl.BlockSpec((blk, FOX_DH), lambda h, j: (j, FOX_H + h)),
            pl.BlockSpec((blk, FOX_DH), lambda h, j: (j, 2 * FOX_H + h)),
            pl.BlockSpec((None, blk, 1), lambda h, j: (h, j, 0)),
            row,
            row,
            pl.BlockSpec((s_len, FOX_DH), head),
        ],
        out_specs=[
            pl.BlockSpec((s_len, FOX_DH), head),
            pl.BlockSpec((blk, FOX_DH), lambda h, j: (j, h)),
            pl.BlockSpec((blk, FOX_DH), lambda h, j: (j, h)),
            pl.BlockSpec((None, blk, 1), lambda h, j: (h, j, 0)),
            row,
        ],
        scratch_shapes=[
            pltpu.VMEM((blk, FOX_DH), F32),
            pltpu.VMEM((blk, FOX_DH), F32),
            pltpu.VMEM((blk, 128), F32),
            pltpu.VMEM((blk, 2 * blk), F32),
        ],
        compiler_params=_cp("parallel", "arbitrary"),
    )(qkv, qkv, qkv, cum_col, lse_row, delta_row, do)


def _swa_bias():
    cols = SWA_G * WINDOW
    k = np.arange(2 * WINDOW)[:, None]
    q = np.arange(cols)[None, :]
    dist = (q % WINDOW) - k + WINDOW
    valid = (dist >= 0) & (dist < WINDOW)
    out = np.empty((2, SWA_HKV, 2 * WINDOW, cols), np.float32)
    for g in range(SWA_HKV):
        slope = np.array([SLOPES[g * SWA_G + t] for t in range(SWA_G)], np.float32)[q // WINDOW]
        bias = -(slope * dist.astype(np.float32))
        out[0, g] = np.where(valid & (k >= WINDOW), bias, np.float32(NEG))
        out[1, g] = np.where(valid, bias, np.float32(NEG))
    return jnp.asarray(out)


def _swa_group(i, q_ref, kk, sinks_ref, bias_ref, g):
    cols = SWA_G * WINDOW
    head = lax.broadcasted_iota(jnp.int32, (1, cols), 1) // WINDOW
    sink = jnp.zeros((1, cols), F32)
    for t in range(SWA_G):
        sink = jnp.where(head == t, sinks_ref[g * SWA_G + t], sink)
    q = jnp.concatenate([q_ref[:, (g * SWA_G + t) * SWA_DH : (g * SWA_G + t + 1) * SWA_DH] for t in range(SWA_G)], axis=0)
    k = kk[:, g * SWA_DH : (g + 1) * SWA_DH]
    s = lax.dot_general(k, q, _NT, preferred_element_type=F32) * SWA_SCALE + bias_ref[jnp.minimum(i, 1), g]
    m = jnp.maximum(jnp.max(s, axis=0, keepdims=True), sink)
    e = jnp.exp(s - m)
    e_sink = jnp.exp(sink - m)
    inv = 1.0 / (jnp.sum(e, axis=0, keepdims=True) + e_sink)
    return q, k, e * inv, e_sink * inv


def _swa_specs(col_q, col_k, col_v, rev, nb):
    def blk(t):
        return nb - 1 - t if rev else t

    return [
        pl.BlockSpec((WINDOW, SWA_W), lambda t: (blk(t), col_q)),
        pl.BlockSpec((WINDOW, SWA_KVW), lambda t: (jnp.maximum(blk(t) - 1, 0), col_k)),
        pl.BlockSpec((WINDOW, SWA_KVW), lambda t: (blk(t), col_k)),
        pl.BlockSpec((WINDOW, SWA_KVW), lambda t: (jnp.maximum(blk(t) - 1, 0), col_v)),
        pl.BlockSpec((WINDOW, SWA_KVW), lambda t: (blk(t), col_v)),
    ]


def _swa_fwd(qkv, sinks):
    s_len = qkv.shape[0]
    nb = s_len // WINDOW
    bias_spec = pl.BlockSpec((2, SWA_HKV, 2 * WINDOW, SWA_G * WINDOW), lambda t: (0, 0, 0, 0))

    def body(q_ref, kp_ref, kc_ref, vp_ref, vc_ref, sinks_ref, bias_ref, o_ref):
        i = pl.program_id(0)
        kk = jnp.concatenate([kp_ref[...], kc_ref[...]], axis=0)
        vv = jnp.concatenate([vp_ref[...], vc_ref[...]], axis=0)
        for g in range(SWA_HKV):
            _, _, p, _ = _swa_group(i, q_ref, kk, sinks_ref, bias_ref, g)
            o = lax.dot_general(p.astype(BF16), vv[:, g * SWA_DH : (g + 1) * SWA_DH], _TN, preferred_element_type=F32)
            for t in range(SWA_G):
                h = g * SWA_G + t
                o_ref[:, h * SWA_DH : (h + 1) * SWA_DH] = o[t * WINDOW : (t + 1) * WINDOW, :]

    return _pcall(
        body,
        name="swa_fwd",
        grid=(nb,),
        out_shape=jax.ShapeDtypeStruct((s_len, SWA_W), F32),
        in_specs=_swa_specs(0, 4, 5, False, nb) + [pl.BlockSpec(memory_space=pltpu.SMEM), bias_spec],
        out_specs=pl.BlockSpec((WINDOW, SWA_W), lambda t: (t, 0)),
        compiler_params=_cp("parallel"),
    )(qkv, qkv, qkv, qkv, qkv, sinks, _swa_bias())


def _swa_bwd(qkv, sinks, do):
    s_len = qkv.shape[0]
    nb = s_len // WINDOW
    bias_spec = pl.BlockSpec((2, SWA_HKV, 2 * WINDOW, SWA_G * WINDOW), lambda t: (0, 0, 0, 0))

    def body(q_ref, kp_ref, kc_ref, vp_ref, vc_ref, sinks_ref, bias_ref, do_ref, dq_ref, dk_ref, dv_ref, dsink_ref, ck_s, cv_s, dkk_s, dvv_s):
        t = pl.program_id(0)
        i = nb - 1 - t

        @pl.when(t == 0)
        def _():
            ck_s[...] = jnp.zeros(ck_s.shape, F32)
            cv_s[...] = jnp.zeros(cv_s.shape, F32)
            dsink_ref[...] = jnp.zeros(dsink_ref.shape, F32)

        kk = jnp.concatenate([kp_ref[...], kc_ref[...]], axis=0)
        vv = jnp.concatenate([vp_ref[...], vc_ref[...]], axis=0)
        lane = lax.broadcasted_iota(jnp.int32, (1, 128), 1)
        dsink = jnp.zeros((1, 128), F32)
        for g in range(SWA_HKV):
            cols = slice(g * SWA_DH, (g + 1) * SWA_DH)
            q, k, p, p_sink = _swa_group(i, q_ref, kk, sinks_ref, bias_ref, g)
            dob = jnp.concatenate([do_ref[:, (g * SWA_G + t) * SWA_DH : (g * SWA_G + t + 1) * SWA_DH] for t in range(SWA_G)], axis=0)
            dp = lax.dot_general(vv[:, cols], dob, _NT, preferred_element_type=F32)
            delta = jnp.sum(p * dp, axis=0, keepdims=True)
            dsb = (p * (dp - delta)).astype(BF16)
            dq = (lax.dot_general(dsb, k, _TN, preferred_element_type=F32) * SWA_SCALE).astype(BF16)
            ps_d = p_sink * delta
            for t in range(SWA_G):
                h = g * SWA_G + t
                dq_ref[:, h * SWA_DH : (h + 1) * SWA_DH] = dq[t * WINDOW : (t + 1) * WINDOW, :]
                dsink = dsink + jnp.where(lane == h, -jnp.sum(ps_d[:, t * WINDOW : (t + 1) * WINDOW], axis=1, keepdims=True), 0.0)
            dkk_s[:, cols] = jnp.dot(dsb, q, preferred_element_type=F32) * SWA_SCALE
            dvv_s[:, cols] = jnp.dot(p.astype(BF16), dob, preferred_element_type=F32)
        dk_ref[...] = (dkk_s[WINDOW:, :] + ck_s[...]).astype(BF16)
        dv_ref[...] = (dvv_s[WINDOW:, :] + cv_s[...]).astype(BF16)
        ck_s[...] = dkk_s[:WINDOW, :]
        cv_s[...] = dvv_s[:WINDOW, :]
        dsink_ref[...] += dsink

    row = lambda t: (nb - 1 - t, 0)
    return _pcall(
        body,
        name="swa_bwd",
        grid=(nb,),
        out_shape=[
            jax.ShapeDtypeStruct((s_len, SWA_W), BF16),
            jax.ShapeDtypeStruct((s_len, SWA_KVW), BF16),
            jax.ShapeDtypeStruct((s_len, SWA_KVW), BF16),
            jax.ShapeDtypeStruct((1, 128), F32),
        ],
        in_specs=_swa_specs(0, 4, 5, True, nb)
        + [pl.BlockSpec(memory_space=pltpu.SMEM), bias_spec, pl.BlockSpec((WINDOW, SWA_W), row)],
        out_specs=[
            pl.BlockSpec((WINDOW, SWA_W), row),
            pl.BlockSpec((WINDOW, SWA_KVW), row),
            pl.BlockSpec((WINDOW, SWA_KVW), row),
            pl.BlockSpec((1, 128), lambda t: (0, 0)),
        ],
        scratch_shapes=[
            pltpu.VMEM((WINDOW, SWA_KVW), F32),
            pltpu.VMEM((WINDOW, SWA_KVW), F32),
            pltpu.VMEM((2 * WINDOW, SWA_KVW), F32),
            pltpu.VMEM((2 * WINDOW, SWA_KVW), F32),
        ],
        compiler_params=_cp("arbitrary"),
    )(qkv, qkv, qkv, qkv, qkv, sinks, _swa_bias(), do)


def _branch_fwd(o, gates, g_blk, w_b, name):
    s_len, wd = o.shape
    d = w_b.shape[1]
    tm = min(512, s_len)

    def body(o_ref, g_ref, w_ref, y_ref, a_ref):
        g = g_ref[...].astype(F32)
        a = (o_ref[...] * (g * _sigmoid(g))).astype(BF16)
        a_ref[...] = a
        y_ref[...] = jnp.dot(a, w_ref[...], preferred_element_type=F32).astype(BF16)

    return _pcall(
        body,
        name=name,
        grid=(s_len // tm,),
        out_shape=[jax.ShapeDtypeStruct((s_len, d), BF16), jax.ShapeDtypeStruct((s_len, wd), BF16)],
        in_specs=[
            pl.BlockSpec((tm, wd), lambda i: (i, 0)),
            pl.BlockSpec((tm, wd), lambda i: (i, g_blk)),
            pl.BlockSpec((wd, d), lambda i: (0, 0)),
        ],
        out_specs=[pl.BlockSpec((tm, d), lambda i: (i, 0)), pl.BlockSpec((tm, wd), lambda i: (i, 0))],
        compiler_params=_cp("parallel"),
    )(o, gates, w_b)


def _out_stage(gates, mf_blk, y_fox, y_swa, w_out, x, ada, ln_g, ln_b, target):
    s_len, d = x.shape
    tm = min(256, s_len)
    n_steps = s_len // tm

    def body(mf_ref, ms_ref, yf_ref, ys_ref, w_ref, x_ref, gate_ref, lg_ref, lb_ref, t_ref, mg_ref, dza_ref, dsub_ref, red_ref):
        i = pl.program_id(0)
        merged = _sigmoid(mf_ref[...].astype(F32)) * yf_ref[...].astype(F32) + _sigmoid(ms_ref[...].astype(F32)) * ys_ref[...].astype(F32)
        mb = merged.astype(BF16)
        mg_ref[...] = mb
        sub = jnp.dot(mb, w_ref[...], preferred_element_type=F32)
        gate = gate_ref[...]
        z = ALPHA * x_ref[...] + gate * sub
        mu = jnp.mean(z, axis=-1, keepdims=True)
        zc = z - mu
        var = jnp.mean(zc * zc, axis=-1, keepdims=True)
        rstd = lax.rsqrt(var + LN_EPS)
        zhat = zc * rstd
        err = zhat * lg_ref[...] + lb_ref[...] - t_ref[...]
        dout = err * (1.0 / d)
        dzhat = dout * lg_ref[...]
        dz = rstd * (dzhat - jnp.mean(dzhat, axis=-1, keepdims=True) - zhat * jnp.mean(dzhat * zhat, axis=-1, keepdims=True))
        dza_ref[...] = ALPHA * dz
        dsub_ref[...] = (gate * dz).astype(BF16)
        part = jnp.concatenate(
            [
                jnp.sum(dz * sub, axis=0, keepdims=True),
                jnp.sum(dout * zhat, axis=0, keepdims=True),
                jnp.sum(dout, axis=0, keepdims=True),
                jnp.sum(err * err, axis=0, keepdims=True),
                jnp.zeros((4, d), F32),
            ],
            axis=0,
        )

        @pl.when(i == 0)
        def _():
            red_ref[...] = part

        @pl.when(i > 0)
        def _():
            red_ref[...] += part

        @pl.when(i == n_steps - 1)
        def _():
            red_ref[4:5, :] = jnp.broadcast_to(jnp.sum(red_ref[3:4, :], axis=1, keepdims=True), (1, d))

    row = pl.BlockSpec((tm, d), lambda i: (i, 0))
    vec = pl.BlockSpec((1, d), lambda i: (0, 0))
    return _pcall(
        body,
        name="out_stage",
        grid=(n_steps,),
        out_shape=[
            jax.ShapeDtypeStruct((s_len, d), BF16),
            jax.ShapeDtypeStruct((s_len, d), F32),
            jax.ShapeDtypeStruct((s_len, d), BF16),
            jax.ShapeDtypeStruct((8, d), F32),
        ],
        in_specs=[
            pl.BlockSpec((tm, d), lambda i: (i, mf_blk)),
            pl.BlockSpec((tm, d), lambda i: (i, mf_blk + 1)),
            row,
            row,
            pl.BlockSpec((d, d), lambda i: (0, 0), pipeline_mode=pl.Buffered(1)),
            row,
            pl.BlockSpec((1, d), lambda i: (0, 2)),
            vec,
            vec,
            row,
        ],
        out_specs=[row, row, row, pl.BlockSpec((8, d), lambda i: (0, 0))],
        compiler_params=_cp("arbitrary"),
    )(gates, gates, y_fox, y_swa, w_out, x, ada, ln_g, ln_b, target)


def _merge_bwd(dsub, w_out, gates, mf_blk, y_fox, y_swa):
    s_len, d = dsub.shape
    tm = min(256, s_len)

    def body(ds_ref, w_ref, mf_ref, ms_ref, yf_ref, ys_ref, dmf_ref, dms_ref, dyf_ref, dys_ref):
        dm = lax.dot_general(ds_ref[...], w_ref[...], _NT, preferred_element_type=F32)
        sf, ss = _sigmoid(mf_ref[...].astype(F32)), _sigmoid(ms_ref[...].astype(F32))
        dmf_ref[...] = (dm * yf_ref[...].astype(F32) * (sf * (1.0 - sf))).astype(BF16)
        dms_ref[...] = (dm * ys_ref[...].astype(F32) * (ss * (1.0 - ss))).astype(BF16)
        dyf_ref[...] = (dm * sf).astype(BF16)
        dys_ref[...] = (dm * ss).astype(BF16)

    row = pl.BlockSpec((tm, d), lambda i: (i, 0))
    return _pcall(
        body,
        name="merge_bwd",
        grid=(s_len // tm,),
        out_shape=[jax.ShapeDtypeStruct((s_len, d), BF16)] * 4,
        in_specs=[
            row,
            pl.BlockSpec((d, d), lambda i: (0, 0), pipeline_mode=pl.Buffered(1)),
            pl.BlockSpec((tm, d), lambda i: (i, mf_blk)),
            pl.BlockSpec((tm, d), lambda i: (i, mf_blk + 1)),
            row,
            row,
        ],
        out_specs=[row] * 4,
        compiler_params=_cp("parallel"),
    )(dsub, w_out, gates, gates, y_fox, y_swa)


def _branch_bwd(dy, w_b, o, gates, g_blk, name, n_heads):
    s_len, d = dy.shape
    wd = w_b.shape[0]
    tm = min(512, s_len)

    def body(dy_ref, w_ref, o_ref, g_ref, do_ref, dg_ref, *rest):
        da = lax.dot_general(dy_ref[...], w_ref[...], _NT, preferred_element_type=F32)
        g = g_ref[...].astype(F32)
        sg = _sigmoid(g)
        do = da * (g * sg)
        do_ref[...] = do.astype(BF16)
        o = o_ref[...]
        dg_ref[...] = (da * o * (sg * (1.0 + g * (1.0 - sg)))).astype(BF16)
        if n_heads:
            prod = do.astype(BF16).astype(F32) * o
            lane = lax.broadcasted_iota(jnp.int32, (1, 128), 1)
            delta = jnp.zeros((tm, 128), F32)
            for h in range(n_heads):
                dh = jnp.sum(prod[:, h * 128 : (h + 1) * 128], axis=1, keepdims=True)
                delta = delta + jnp.where(lane == h, dh, 0.0)
            rest[0][...] = delta

    out_shape = [jax.ShapeDtypeStruct((s_len, wd), BF16), jax.ShapeDtypeStruct((s_len, wd), BF16)]
    out_specs = [pl.BlockSpec((tm, wd), lambda i: (i, 0))] * 2
    if n_heads:
        out_shape.append(jax.ShapeDtypeStruct((s_len, 128), F32))
        out_specs.append(pl.BlockSpec((tm, 128), lambda i: (i, 0)))
    return _pcall(
        body,
        name=name,
        grid=(s_len // tm,),
        out_shape=out_shape,
        in_specs=[
            pl.BlockSpec((tm, d), lambda i: (i, 0)),
            pl.BlockSpec((wd, d), lambda i: (0, 0)),
            pl.BlockSpec((tm, wd), lambda i: (i, 0)),
            pl.BlockSpec((tm, wd), lambda i: (i, g_blk)),
        ],
        out_specs=out_specs,
        compiler_params=_cp("parallel"),
    )(dy, w_b, o, gates)


def _in_bwd(dproj, w_in_t, x, ada, dza, ride):
    s_len, d = x.shape
    k_tot = dproj.shape[1]
    tm, tk, dn = min(512, s_len), k_tot // 4, d // 2
    ni, nk = s_len // tm, k_tot // tk
    n = len(ride)

    def body(dp_ref, w_ref, x_ref, sc_ref, dza_ref, *rest):
        ins, (gx_ref, red_ref), outs = rest[:n], rest[n : n + 2], rest[n + 2 : 2 * n + 2]
        sems, acc_s = rest[2 * n + 2 : 2 * n + 5], rest[2 * n + 5]
        i, nh, kk = pl.program_id(0), pl.program_id(1), pl.program_id(2)

        @pl.when((i == 0) & (nh == 0) & (kk == 0))
        def _():
            _rider_start("exchange", ins, outs, *sems)

        @pl.when((i == ni - 1) & (nh == 1) & (kk == nk - 1))
        def _():
            _rider_wait("exchange", ins, outs, *sems)

        part = jnp.dot(dp_ref[...], w_ref[...], preferred_element_type=F32)
        half = pl.ds(pl.multiple_of(nh * dn, dn), dn)

        @pl.when(kk == 0)
        def _():
            acc_s[:, half] = part

        @pl.when(kk > 0)
        def _():
            acc_s[:, half] += part

        @pl.when((nh == 1) & (kk == nk - 1))
        def _():
            dh = acc_s[...]
            xv = x_ref[...]
            mu = jnp.mean(xv, axis=-1, keepdims=True)
            xc = xv - mu
            var = jnp.mean(xc * xc, axis=-1, keepdims=True)
            rstd = lax.rsqrt(var + LN_EPS)
            xhat = xc * rstd
            dxhat = dh * (1.0 + sc_ref[...])
            dx = rstd * (dxhat - jnp.mean(dxhat, axis=-1, keepdims=True) - xhat * jnp.mean(dxhat * xhat, axis=-1, keepdims=True))
            gx_ref[...] = dza_ref[...] + dx
            part_r = jnp.concatenate(
                [jnp.sum(dh, axis=0, keepdims=True), jnp.sum(dh * xhat, axis=0, keepdims=True), jnp.zeros((6, d), F32)], axis=0
            )

            @pl.when(i == 0)
            def _():
                red_ref[...] = part_r

            @pl.when(i > 0)
            def _():
                red_ref[...] += part_r

    row = pl.BlockSpec((tm, d), lambda i, nh, kk: (i, 0))
    hbm = pl.BlockSpec(memory_space=pltpu.HBM)
    return _pcall(
        body,
        name="in_bwd",
        grid=(ni, 2, nk),
        out_shape=[jax.ShapeDtypeStruct((s_len, d), F32), jax.ShapeDtypeStruct((8, d), F32)]
        + [jax.ShapeDtypeStruct(r.shape, r.dtype) for r in ride],
        in_specs=[
            pl.BlockSpec((tm, tk), lambda i, nh, kk: (i, kk)),
            pl.BlockSpec((tk, dn), lambda i, nh, kk: (kk, nh)),
            row,
            pl.BlockSpec((1, d), lambda i, nh, kk: (0, 1)),
            row,
        ]
        + [hbm] * n,
        out_specs=[row, pl.BlockSpec((8, d), lambda i, nh, kk: (0, 0))] + [hbm] * n,
        scratch_shapes=_rider_scratch(n) + [pltpu.VMEM((tm, d), F32)],
        compiler_params=_cp("arbitrary", "arbitrary", "arbitrary"),
    )(dproj, w_in_t, x, ada, dza, *ride)


def _pad_lanes(v, n):
    return jnp.pad(v, ((0, 0), (0, n - v.shape[1])))


def kernel(x, c, w_ada, b_ada, w_in, b_f, attn_sinks, w_br_fox, w_br_swa, w_out, ln_g, ln_b, loss_target, m_w_ada, m_b_ada, m_w_in, m_b_f, m_attn_sinks, m_w_br_fox, m_w_br_swa, m_w_out, m_ln_g, m_ln_b, v_w_ada, v_b_ada, v_w_in, v_b_f, v_attn_sinks, v_w_br_fox, v_w_br_swa, v_w_out, v_ln_g, v_ln_b):
    x2, tgt = x[0], loss_target[0]
    s_len, d = x2.shape
    me = 4 * lax.axis_index("x") + 2 * lax.axis_index("y") + lax.axis_index("c")
    off_ms = OFF_MF + d
    in_pad = off_ms + d
    c_ada = w_ada.shape[2]
    c_in = w_in.shape[2]
    c_br = w_br_fox.shape[2]

    w_in_full = _all_gather(w_in[0].T.astype(BF16), "ag_w_in", pltpu.HBM).reshape(N_DEV * c_in, d)
    w_in_pad = jnp.concatenate(
        [w_in_full[:REAL_FLOG_END], jnp.zeros((FLOG_PAD - N_FLOG, d), BF16), w_in_full[REAL_FLOG_END:]], axis=0
    )
    k_cut = REAL_FLOG_END // c_in

    c_all = _gather_rows(c, "ag_c")
    b_cols = lax.dynamic_slice(b_ada, (0, me * c_ada), (1, c_ada))
    ada_cols = _ada_fwd(c_all, w_ada[0], b_cols)
    ada_g = _all_gather(ada_cols, "ag_ada", pltpu.VMEM)
    ada = lax.dynamic_index_in_dim(ada_g, me, axis=1, keepdims=False).reshape(1, N_DEV * c_ada)

    h = _ln_mod(x2, ada)
    qkv_fox = _mm_cols(h, w_in_pad, OFF_FQ, 3 * FOX_W, BF16, "proj_fox")
    flog = _mm_cols(h, w_in_pad, OFF_FLOG, FLOG_PAD, F32, "proj_flog")
    qkv_swa = _mm_cols(h, w_in_pad, OFF_SQ, SWA_W + 2 * SWA_KVW, BF16, "proj_swa")
    gates, w_bf, w_bs, w_o = _mm_cols(
        h, w_in_pad, OFF_GF, in_pad - OFF_GF, BF16, "proj_gates",
        ride=(w_br_fox[0].astype(BF16), w_br_swa[0].astype(BF16), w_out[0].astype(BF16)),
    )
    w_bf = w_bf.reshape(N_DEV, FOX_W, c_br).transpose(1, 0, 2).reshape(FOX_W, d)
    w_bs = w_bs.reshape(N_DEV, SWA_W, c_br).transpose(1, 0, 2).reshape(SWA_W, d)
    w_o = w_o.reshape(d, d)
    mf_blk = (OFF_MF - OFF_GF) // d

    flog_t = flog[:, :N_FLOG].T
    bf_col = b_f.reshape(FOX_H, 1)
    cum = _fox_cum(flog_t, bf_col)
    cum_row = cum.reshape(FOX_H, 1, s_len)
    o_fox, lse = _fox_fwd(qkv_fox, cum_row)
    sinks = attn_sinks.reshape(SWA_HQ)
    o_swa = _swa_fwd(qkv_swa, sinks)

    y_fox, a_fox = _branch_fwd(o_fox, gates, 0, w_bf, "branch_fox")
    y_swa, a_swa = _branch_fwd(o_swa, gates, 1, w_bs, "branch_swa")
    merged, dza, dsub, red = _out_stage(gates, mf_blk, y_fox, y_swa, w_o, x2, ada, ln_g, ln_b, tgt)
    loss = lax.psum(0.5 * red[4, 0] / d, ("x", "y", "c"))

    dmf, dms, dy_fox, dy_swa = _merge_bwd(dsub, w_o, gates, mf_blk, y_fox, y_swa)
    do_fox, dg_fox, delta = _branch_bwd(dy_fox, w_bf, o_fox, gates, 0, "branch_fox_bwd", FOX_H)
    do_swa, dg_swa = _branch_bwd(dy_swa, w_bs, o_swa, gates, 1, "branch_swa_bwd", 0)
    delta_row = delta[:, :FOX_H].T.reshape(FOX_H, 1, s_len)
    dq_f, dk_f, dv_f, dcol, drow = _fox_bwd(
        qkv_fox, cum.reshape(FOX_H, s_len, 1), lse.reshape(FOX_H, 1, s_len), delta_row, do_fox
    )
    dflog_t, dbf = _fox_gate_bwd(drow.reshape(FOX_H, s_len), dcol.reshape(FOX_H, s_len), flog_t, bf_col)
    dq_s, dk_s, dv_s, dsink = _swa_bwd(qkv_swa, sinks, do_swa)
    dflog = _pad_lanes(dflog_t.T, FLOG_PAD).astype(BF16)
    dproj = jnp.concatenate([dq_f.astype(BF16), dk_f, dv_f, dflog, dq_s, dk_s, dv_s, dg_fox, dg_swa, dmf, dms], axis=1)
    g_w_bf = _mm_tn(a_fox, dy_fox, "grad_w_br_fox")
    g_w_bs = _mm_tn(a_swa, dy_swa, "grad_w_br_swa")
    g_w_o = _mm_tn(merged, dsub, "grad_w_out")
    g_w_in, r_bf, r_bs, r_o = _mm_tn(
        dproj, h, "grad_w_in",
        ride=(
            g_w_bf.reshape(FOX_W, N_DEV, c_br).transpose(1, 0, 2),
            g_w_bs.reshape(SWA_W, N_DEV, c_br).transpose(1, 0, 2),
            g_w_o.reshape(N_DEV, d // N_DEV, d),
        ),
    )
    pad = FLOG_PAD - N_FLOG
    g_blocks = jnp.stack(
        [g_w_in[k * c_in : (k + 1) * c_in] for k in range(k_cut)]
        + [jnp.concatenate([g_w_in[k_cut * c_in : REAL_FLOG_END], g_w_in[OFF_SQ : (k_cut + 1) * c_in + pad]], axis=0)]
        + [g_w_in[k * c_in + pad : (k + 1) * c_in + pad] for k in range(k_cut + 1, N_DEV)]
    )

    grad_x, red2, r_in = _in_bwd(dproj, w_in_pad, x2, ada, dza, ride=(g_blocks,))
    out_w_in = _sum_adam_t(r_in, w_in[0].T, m_w_in[0].T, v_w_in[0].T, "adam_w_in")
    out_w_in = [o.T for o in out_w_in]
    out_w_bf = _sum_adam(r_bf, w_br_fox[0], m_w_br_fox[0], v_w_br_fox[0], "adam_w_br_fox")
    out_w_bs = _sum_adam(r_bs, w_br_swa[0], m_w_br_swa[0], v_w_br_swa[0], "adam_w_br_swa")
    out_w_o = _sum_adam(r_o, w_out[0], m_w_out[0], v_w_out[0], "adam_w_out")

    packed = jnp.concatenate([red2[0:1], red2[1:2], red[0:1], _pad_lanes(dbf[:, 0].reshape(1, FOX_H), 128), dsink, red[1:2], red[2:3]], axis=1)
    gathered = _gather_rows(packed, "ag_small")
    pack = lambda a, b, cc, dd, e: jnp.concatenate([a, _pad_lanes(b, 128), _pad_lanes(cc, 128), dd, e], axis=1)
    small = _small_adam(
        gathered,
        pack(b_ada, b_f, attn_sinks, ln_g, ln_b),
        pack(m_b_ada, m_b_f, m_attn_sinks, m_ln_g, m_ln_b),
        pack(v_b_ada, v_b_f, v_attn_sinks, v_ln_g, v_ln_b),
    )
    dada_cols = lax.dynamic_slice(gathered, (0, me * c_ada), (N_DEV, c_ada))
    out_w_ada = _wada_adam(c_all.T, dada_cols, w_ada[0], m_w_ada[0], v_w_ada[0])

    o1, o2, o3 = 3 * d, 3 * d + 128, 3 * d + 256

    def unpack(p):
        return p[:, :o1], p[:, o1 : o1 + FOX_H], p[:, o2 : o2 + SWA_HQ], p[:, o3 : o3 + d], p[:, o3 + d : o3 + 2 * d]

    kinds = []
    for k in range(4):
        b_ada_k, b_f_k, sinks_k, ln_g_k, ln_b_k = unpack(small[k])
        kinds.append(
            [out_w_ada[k][None], b_ada_k, out_w_in[k][None], b_f_k, sinks_k, out_w_bf[k][None], out_w_bs[k][None], out_w_o[k][None], ln_g_k, ln_b_k]
        )
    return (loss, grad_x[None], *kinds[0], *kinds[1], *kinds[2], *kinds[3])
```

```python
import numpy as np
import jax
import jax.numpy as jnp
from jax import lax
from jax.experimental import pallas as pl
from jax.experimental.pallas import tpu as pltpu

F32 = jnp.float32
BF16 = jnp.bfloat16
N_DEV = 8
MESH = pl.DeviceIdType.MESH

FOX_H, FOX_DH, FOX_W = 8, 128, 1024
SWA_HQ, SWA_HKV, SWA_DH, SWA_G = 16, 4, 64, 4
SWA_W, SWA_KVW, WINDOW = 1024, 256, 128
LN_EPS = 1e-5
NEG = -1e30
DEPTH = 1
ALPHA = (2.0 * DEPTH) ** 0.25
FOX_SCALE = FOX_DH ** -0.5
SWA_SCALE = SWA_DH ** -0.5
SLOPES = [2.0 ** (-8.0 * (h + 1.0) / SWA_HQ) for h in range(SWA_HQ)]

ADAM_LR, ADAM_B1, ADAM_B2, ADAM_EPS, ADAM_WD, ADAM_STEP = 0.001, 0.9, 0.999, 1e-08, 0.01, 10

N_FLOG = 8
FLOG_PAD = 512
OFF_FQ, OFF_FK, OFF_FV, OFF_FLOG = 0, 1024, 2048, 3072
OFF_SQ = OFF_FLOG + FLOG_PAD
OFF_SK = OFF_SQ + SWA_W
OFF_SV = OFF_SK + SWA_KVW
OFF_GF = OFF_SV + SWA_KVW
OFF_GS = OFF_GF + FOX_W
OFF_MF = OFF_GS + SWA_W
REAL_FLOG_END = OFF_FLOG + N_FLOG

ATT_BLK = 512
VMEM_LIMIT = 58 * 1024 * 1024


def _pcall(body, **kw):
    return pl.pallas_call(body, **kw)


def _cp(*sem):
    return pltpu.CompilerParams(dimension_semantics=sem, vmem_limit_bytes=VMEM_LIMIT)


def _sigmoid(x):
    return 0.5 * jnp.tanh(0.5 * x) + 0.5


def _all_gather(x, name, space):
    m_per, n = x.shape

    def body(x_ref, out_ref, send_sems, recv_sems, local_sem):
        mx, my, mc = lax.axis_index("x"), lax.axis_index("y"), lax.axis_index("c")
        me, sibling = (mx, my, mc), (mx, my, 1 - mc)
        xn, yn, dg = (1 - mx, my), (mx, 1 - my), (1 - mx, 1 - my)
        south = mc == 0
        src_chip = (jnp.where(south, 1 - mx, mx), jnp.where(south, my, 1 - my))
        dst_chip = (jnp.where(south, mx, 1 - mx), jnp.where(south, 1 - my, my))

        def rows(px, py, pc):
            return out_ref.at[4 * px + 2 * py + pc]

        def copy(k, block, to, src=None):
            return pltpu.make_async_remote_copy(
                src_ref=rows(*block) if src is None else src,
                dst_ref=rows(*block),
                send_sem=send_sems.at[k],
                recv_sem=recv_sems.at[k],
                device_id=to,
                device_id_type=MESH,
            )

        mine = pltpu.make_async_copy(x_ref, rows(*me), local_sem)
        mine.start()
        first = [copy(0, me, sibling, src=x_ref), copy(1, me, (*xn, mc), src=x_ref), copy(2, me, (*yn, mc), src=x_ref)]
        for cp in first:
            cp.start()
        copy(1, (*xn, mc), me).wait_recv()
        copy(2, (*yn, mc), me).wait_recv()
        later = [copy(3, (*src_chip, mc), (*dst_chip, mc)), copy(4, (*xn, mc), sibling), copy(5, (*yn, mc), sibling)]
        for cp in later:
            cp.start()
        copy(3, (*dg, mc), me).wait_recv()
        last = copy(6, (*dg, mc), sibling)
        last.start()
        copy(0, sibling, me).wait_recv()
        for k, chip in ((4, xn), (5, yn), (6, dg)):
            copy(k, (*chip, 1 - mc), me).wait_recv()
        for cp in first + later + [last]:
            cp.wait_send()
        mine.wait()

    return _pcall(
        body,
        name=name,
        out_shape=jax.ShapeDtypeStruct((N_DEV, m_per, n), x.dtype),
        in_specs=[pl.BlockSpec(memory_space=space)],
        out_specs=pl.BlockSpec(memory_space=space),
        scratch_shapes=[pltpu.SemaphoreType.DMA((7,)), pltpu.SemaphoreType.DMA((7,)), pltpu.SemaphoreType.DMA],
    )(x)


def _peer(d, mx, my, mc):
    return (1 - mx if (d >> 2) & 1 else mx, 1 - my if (d >> 1) & 1 else my, 1 - mc if d & 1 else mc)


def _rider_copies(kind, ins, outs, send_sems, recv_sems, local_sems):
    mx, my, mc = lax.axis_index("x"), lax.axis_index("y"), lax.axis_index("c")
    me = 4 * mx + 2 * my + mc
    remote, local = [], []
    for a in range(len(ins)):
        if kind == "gather":
            m_per = ins[a].shape[0]
            mine = outs[a].at[pl.ds(me * m_per, m_per), :]
            local.append(pltpu.make_async_copy(ins[a], mine, local_sems.at[a]))
        else:
            local.append(pltpu.make_async_copy(ins[a].at[me], outs[a].at[0], local_sems.at[a]))
        for d in range(1, N_DEV):
            px, py, pc = _peer(d, mx, my, mc)
            if kind == "gather":
                src, dst = ins[a], mine
            else:
                src, dst = ins[a].at[4 * px + 2 * py + pc], outs[a].at[d]
            remote.append(
                pltpu.make_async_remote_copy(
                    src_ref=src,
                    dst_ref=dst,
                    send_sem=send_sems.at[a * 7 + d - 1],
                    recv_sem=recv_sems.at[a * 7 + d - 1],
                    device_id=(px, py, pc),
                    device_id_type=MESH,
                )
            )
    return remote, local


def _rider_start(*args):
    remote, local = _rider_copies(*args)
    for cp in local + remote:
        cp.start()


def _rider_wait(*args):
    remote, local = _rider_copies(*args)
    for cp in remote:
        cp.wait_recv()
    for cp in remote:
        cp.wait_send()
    for cp in local:
        cp.wait()


def _rider_scratch(n):
    return [pltpu.SemaphoreType.DMA((7 * n,)), pltpu.SemaphoreType.DMA((7 * n,)), pltpu.SemaphoreType.DMA((n,))]


def _gather_rows(v, name):
    n = v.shape[1]
    return _all_gather(jnp.broadcast_to(v, (8, n)), name, pltpu.VMEM)[:, 0, :]


def _adamw(w, g, m, v):
    m = ADAM_B1 * m + (1.0 - ADAM_B1) * g
    v = ADAM_B2 * v + (1.0 - ADAM_B2) * (g * g)
    m_hat = m / (1.0 - ADAM_B1**ADAM_STEP)
    v_hat = v / (1.0 - ADAM_B2**ADAM_STEP)
    delta = -ADAM_LR * (m_hat / (jnp.sqrt(v_hat) + ADAM_EPS) + ADAM_WD * w)
    return delta, m, v


def _sum_adam(recv, w, m, v, name):
    _, r_tot, c = recv.shape
    c_pad = -(-c // 128) * 128
    tr = r_tot
    while 8 * tr * c_pad * 4 > 6 * 1024 * 1024 and tr % 32 == 0:
        tr //= 2

    def body(r_ref, w_ref, m_ref, v_ref, g_ref, d_ref, nm_ref, nv_ref):
        g = r_ref[0].astype(F32)
        for k in range(1, N_DEV):
            g = g + r_ref[k].astype(F32)
        d, nm, nv = _adamw(w_ref[...], g, m_ref[...], v_ref[...])
        g_ref[...] = g
        d_ref[...] = d
        nm_ref[...] = nm
        nv_ref[...] = nv

    blk = pl.BlockSpec((tr, c), lambda i: (i, 0))
    return _pcall(
        body,
        name=name,
        grid=(r_tot // tr,),
        out_shape=[jax.ShapeDtypeStruct((r_tot, c), F32)] * 4,
        in_specs=[pl.BlockSpec((N_DEV, tr, c), lambda i: (0, i, 0)), blk, blk, blk],
        out_specs=[blk] * 4,
        compiler_params=_cp("parallel"),
    )(recv, w, m, v)


def _sum_adam_t(recv, w, m, v, name):
    _, c, r_tot = recv.shape
    tr = min(256, r_tot)

    def body(r_ref, w_ref, m_ref, v_ref, g_ref, d_ref, nm_ref, nv_ref):
        g = r_ref[0].astype(F32)
        for k in range(1, N_DEV):
            g = g + r_ref[k].astype(F32)
        d, nm, nv = _adamw(w_ref[...], g, m_ref[...], v_ref[...])
        g_ref[...] = g
        d_ref[...] = d
        nm_ref[...] = nm
        nv_ref[...] = nv

    blk = pl.BlockSpec((c, tr), lambda i: (0, i))
    return _pcall(
        body,
        name=name,
        grid=(r_tot // tr,),
        out_shape=[jax.ShapeDtypeStruct((c, r_tot), F32)] * 4,
        in_specs=[pl.BlockSpec((N_DEV, c, tr), lambda i: (0, 0, i)), blk, blk, blk],
        out_specs=[blk] * 4,
        compiler_params=_cp("parallel"),
    )(recv, w, m, v)


def _wada_adam(c_t, dada_cols, w, m, v):
    d_model, c = w.shape
    tr = min(256, d_model)

    def body(ct_ref, da_ref, w_ref, m_ref, v_ref, g_ref, d_ref, nm_ref, nv_ref):
        g = jnp.dot(ct_ref[...].astype(BF16), da_ref[...].astype(BF16), preferred_element_type=F32)
        d, nm, nv = _adamw(w_ref[...], g, m_ref[...], v_ref[...])
        g_ref[...] = g
        d_ref[...] = d
        nm_ref[...] = nm
        nv_ref[...] = nv

    blk = pl.BlockSpec((tr, c), lambda i: (i, 0))
    return _pcall(
        body,
        name="wada_adam",
        grid=(d_model // tr,),
        out_shape=[jax.ShapeDtypeStruct((d_model, c), F32)] * 4,
        in_specs=[pl.BlockSpec((tr, N_DEV), lambda i: (i, 0)), pl.BlockSpec((N_DEV, c), lambda i: (0, 0)), blk, blk, blk],
        out_specs=[blk] * 4,
        compiler_params=_cp("parallel"),
    )(c_t, dada_cols, w, m, v)


def _small_adam(gathered, w, m, v):
    p = w.shape[1]

    def body(a_ref, w_ref, m_ref, v_ref, g_ref, d_ref, nm_ref, nv_ref):
        g = a_ref[0:1, :]
        for k in range(1, N_DEV):
            g = g + a_ref[k : k + 1, :]
        d, nm, nv = _adamw(w_ref[...], g, m_ref[...], v_ref[...])
        g_ref[...] = g
        d_ref[...] = d
        nm_ref[...] = nm
        nv_ref[...] = nv

    return _pcall(
        body,
        name="small_adam",
        out_shape=[jax.ShapeDtypeStruct((1, p), F32)] * 4,
    )(gathered, w, m, v)


def _ada_fwd(c_all, w_ada, b_cols):
    c = w_ada.shape[1]

    def body(c_ref, w_ref, b_ref, o_ref):
        o_ref[...] = jnp.dot(c_ref[...].astype(BF16), w_ref[...].astype(BF16), preferred_element_type=F32) + b_ref[...]

    return _pcall(
        body,
        name="ada_fwd",
        out_shape=jax.ShapeDtypeStruct((N_DEV, c), F32),
        compiler_params=_cp(),
    )(c_all, w_ada, b_cols)


def _ln_mod(x, ada):
    s_len, d = x.shape
    tm = min(512, s_len)

    def body(x_ref, sh_ref, sc_ref, h_ref):
        xv = x_ref[...]
        mu = jnp.mean(xv, axis=-1, keepdims=True)
        xc = xv - mu
        var = jnp.mean(xc * xc, axis=-1, keepdims=True)
        xhat = xc * lax.rsqrt(var + LN_EPS)
        h_ref[...] = (xhat * (1.0 + sc_ref[...]) + sh_ref[...]).astype(BF16)

    return _pcall(
        body,
        name="ln_mod",
        grid=(s_len // tm,),
        out_shape=jax.ShapeDtypeStruct((s_len, d), BF16),
        in_specs=[
            pl.BlockSpec((tm, d), lambda i: (i, 0)),
            pl.BlockSpec((1, d), lambda i: (0, 0)),
            pl.BlockSpec((1, d), lambda i: (0, 1)),
        ],
        out_specs=pl.BlockSpec((tm, d), lambda i: (i, 0)),
        compiler_params=_cp("parallel"),
    )(x, ada, ada)


def _mm_cols(a, b, col_off, n_cols, out_dtype, name, ride=()):
    m, k = a.shape
    tm, tn = min(1024, m), 512
    off = col_off // tn
    ni, nj = m // tm, n_cols // tn
    n = len(ride)

    def body(a_ref, b_ref, *rest):
        ins, o_ref, outs, sems = rest[:n], rest[n], rest[n + 1 : 2 * n + 1], rest[2 * n + 1 :]
        i, j = pl.program_id(0), pl.program_id(1)
        if n:

            @pl.when((i == 0) & (j == 0))
            def _():
                _rider_start("gather", ins, outs, *sems)

        o_ref[...] = lax.dot_general(a_ref[...], b_ref[...], _NT, preferred_element_type=F32).astype(out_dtype)
        if n:

            @pl.when((i == ni - 1) & (j == nj - 1))
            def _():
                _rider_wait("gather", ins, outs, *sems)

    hbm = pl.BlockSpec(memory_space=pltpu.HBM)
    out = _pcall(
        body,
        name=name,
        grid=(ni, nj),
        out_shape=[jax.ShapeDtypeStruct((m, n_cols), out_dtype)]
        + [jax.ShapeDtypeStruct((N_DEV * r.shape[0], r.shape[1]), r.dtype) for r in ride],
        in_specs=[pl.BlockSpec((tm, k), lambda i, j: (i, 0)), pl.BlockSpec((tn, k), lambda i, j: (off + j, 0))] + [hbm] * n,
        out_specs=[pl.BlockSpec((tm, tn), lambda i, j: (i, j))] + [hbm] * n,
        scratch_shapes=_rider_scratch(n) if n else [],
        compiler_params=_cp("arbitrary", "arbitrary") if n else _cp("parallel", "parallel"),
    )(a, b, *ride)
    return out if n else out[0]


def _mm_tn(a, b, name, ride=()):
    s_len, m = a.shape
    n = b.shape[1]
    tm, tn, ts = min(1024, m), min(1024, n), min(2048, s_len)
    ni, nj, ns = m // tm, n // tn, s_len // ts
    nr = len(ride)

    def body(a_ref, b_ref, *rest):
        ins, o_ref, outs = rest[:nr], rest[nr], rest[nr + 1 : 2 * nr + 1]
        sems, acc_s = rest[2 * nr + 1 : -1], rest[-1]
        i, j, kk = pl.program_id(0), pl.program_id(1), pl.program_id(2)
        if nr:

            @pl.when((i == 0) & (j == 0) & (kk == 0))
            def _():
                _rider_start("exchange", ins, outs, *sems)

            @pl.when((i == ni - 1) & (j == nj - 1) & (kk == ns - 1))
            def _():
                _rider_wait("exchange", ins, outs, *sems)

        part = lax.dot_general(a_ref[...], b_ref[...], _TN, preferred_element_type=F32)

        @pl.when(kk == 0)
        def _():
            acc_s[...] = part

        @pl.when(kk > 0)
        def _():
            acc_s[...] += part

        @pl.when(kk == ns - 1)
        def _():
            o_ref[...] = acc_s[...].astype(BF16)

    hbm = pl.BlockSpec(memory_space=pltpu.HBM)
    out = _pcall(
        body,
        name=name,
        grid=(ni, nj, ns),
        out_shape=[jax.ShapeDtypeStruct((m, n), BF16)] + [jax.ShapeDtypeStruct(r.shape, r.dtype) for r in ride],
        in_specs=[pl.BlockSpec((ts, tm), lambda i, j, kk: (kk, i)), pl.BlockSpec((ts, tn), lambda i, j, kk: (kk, j))] + [hbm] * nr,
        out_specs=[pl.BlockSpec((tm, tn), lambda i, j, kk: (i, j))] + [hbm] * nr,
        scratch_shapes=(_rider_scratch(nr) if nr else []) + [pltpu.VMEM((tm, tn), F32)],
        compiler_params=_cp("arbitrary", "arbitrary", "arbitrary") if nr else _cp("parallel", "parallel", "arbitrary"),
    )(a, b, *ride)
    return out if nr else out[0]


def _split3(a):
    hi = a.astype(BF16)
    r1 = a - hi.astype(F32)
    mid = r1.astype(BF16)
    lo = (r1 - mid.astype(F32)).astype(BF16)
    return hi, mid, lo


def _dot_ones(a, tri):
    return sum(jnp.dot(t, tri, preferred_element_type=F32) for t in _split3(a))


def _log_sigmoid(x):
    return jnp.minimum(x, 0.0) - jnp.log1p(jnp.exp(-jnp.abs(x)))


def _fox_cum(flog_t, bf_col):
    s_len = flog_t.shape[1]

    def body(fl_ref, bf_ref, cum_ref):
        r = lax.broadcasted_iota(jnp.int32, (128, 128), 0)
        c = lax.broadcasted_iota(jnp.int32, (128, 128), 1)
        upper = (r <= c).astype(BF16)

        def step(t, carry):
            sl = pl.ds(pl.multiple_of(t * 128, 128), 128)
            lf = _log_sigmoid(fl_ref[:, sl] + bf_ref[...])
            cs = _dot_ones(lf, upper) + carry
            cum_ref[:, sl] = cs
            return cs[:, 127:128]

        lax.fori_loop(0, s_len // 128, step, jnp.zeros((FOX_H, 1), F32))

    return _pcall(body, name="fox_cum", out_shape=jax.ShapeDtypeStruct((FOX_H, s_len), F32))(flog_t, bf_col)


def _fox_gate_bwd(drow, dcol, flog_t, bf_col):
    s_len = flog_t.shape[1]
    n = s_len // 128

    def body(dr_ref, dc_ref, fl_ref, bf_ref, dfl_ref, dbf_ref):
        r = lax.broadcasted_iota(jnp.int32, (128, 128), 0)
        c = lax.broadcasted_iota(jnp.int32, (128, 128), 1)
        lower = (r >= c).astype(BF16)

        def step(t, carry):
            run, tot = carry
            sl = pl.ds(pl.multiple_of((n - 1 - t) * 128, 128), 128)
            rc = _dot_ones(dr_ref[:, sl] - dc_ref[:, sl], lower) + run
            dfl = rc * _sigmoid(-(fl_ref[:, sl] + bf_ref[...]))
            dfl_ref[:, sl] = dfl
            return rc[:, 0:1], tot + jnp.sum(dfl, axis=1, keepdims=True)

        zero = jnp.zeros((FOX_H, 1), F32)
        _, tot = lax.fori_loop(0, n, step, (zero, zero))
        dbf_ref[...] = jnp.broadcast_to(tot, (FOX_H, 128))

    return _pcall(
        body,
        name="fox_gate_bwd",
        out_shape=[jax.ShapeDtypeStruct((FOX_H, s_len), F32), jax.ShapeDtypeStruct((FOX_H, 128), F32)],
    )(drow, dcol, flog_t, bf_col)


def _diag_mask(blk, transposed=False):
    r = lax.broadcasted_iota(jnp.int32, (blk, blk), 0)
    c = lax.broadcasted_iota(jnp.int32, (blk, blk), 1)
    return c >= r if transposed else r >= c


_NT = (((1,), (1,)), ((), ()))
_TN = (((0,), (0,)), ((), ()))


def _fox_fwd(qkv, cum_row):
    s_len = qkv.shape[0]
    blk = min(ATT_BLK, s_len)
    nb = s_len // blk
    log2e = 1.4426950408889634

    def body(q_ref, k_ref, v_ref, c_ref, o_ref, lse_ref, mx_s, acc_s, u_s):
        i = pl.program_id(1)

        def key_cols(j, n):
            return pl.ds(pl.multiple_of(j * blk, blk), n * blk)

        def walk(tile):
            lax.fori_loop(0, i // 2, lambda t, c: (tile(2 * t, 2, False), c)[1], 0)

            @pl.when(i % 2 == 1)
            def _():
                tile(i - 1, 1, False)

            tile(i, 1, True)

        def lane_max(j, n, masked):
            cols = key_cols(j, n)
            u = lax.dot_general(q_ref[...], k_ref[cols, :], _NT, preferred_element_type=F32) * (FOX_SCALE * log2e) - c_ref[:, cols] * log2e
            if masked:
                u = jnp.where(_diag_mask(blk), u, NEG)
            u_s[:, cols] = u
            part = u[:, 0:128]
            for t in range(1, n * blk // 128):
                part = jnp.maximum(part, u[:, t * 128 : (t + 1) * 128])
            mx_s[...] = jnp.maximum(mx_s[...], part)

        mx_s[...] = jnp.full(mx_s.shape, NEG, F32)
        walk(lane_max)
        m = jnp.max(mx_s[...], axis=1, keepdims=True)

        def weigh(j, n, masked):
            cols = key_cols(j, n)
            p = jnp.exp2(u_s[:, cols] - m)
            ones_col = (lax.broadcasted_iota(jnp.int32, (n * blk, 128), 1) == 0).astype(BF16)
            v1 = jnp.concatenate([v_ref[cols, :], ones_col], axis=1)
            acc_s[...] += jnp.dot(p.astype(BF16), v1, preferred_element_type=F32)

        acc_s[...] = jnp.zeros(acc_s.shape, F32)
        walk(weigh)
        l = acc_s[:, FOX_DH : FOX_DH + 1]
        o_ref[...] = acc_s[:, :FOX_DH] / l
        lse_ref[...] = m * (1.0 / log2e) + jnp.log(l)

    return _pcall(
        body,
        name="fox_fwd",
        grid=(FOX_H, nb),
        out_shape=[jax.ShapeDtypeStruct((s_len, FOX_W), F32), jax.ShapeDtypeStruct((FOX_H, s_len, 1), F32)],
        in_specs=[
            pl.BlockSpec((blk, FOX_DH), lambda h, i: (i, h)),
            pl.BlockSpec((s_len, FOX_DH), lambda h, i: (0, FOX_H + h)),
            pl.BlockSpec((s_len, FOX_DH), lambda h, i: (0, 2 * FOX_H + h)),
            pl.BlockSpec((None, 1, s_len), lambda h, i: (h, 0, 0)),
        ],
        out_specs=[
            pl.BlockSpec((blk, FOX_DH), lambda h, i: (i, h)),
            pl.BlockSpec((None, blk, 1), lambda h, i: (h, i, 0)),
        ],
        scratch_shapes=[pltpu.VMEM((blk, 128), F32), pltpu.VMEM((blk, 2 * FOX_DH), F32), pltpu.VMEM((blk, s_len), F32)],
        compiler_params=_cp("parallel", "arbitrary"),
    )(qkv, qkv, qkv, cum_row)


def _fox_bwd(qkv, cum_col, lse_row, delta_row, do):
    s_len = qkv.shape[0]
    blk = min(ATT_BLK, s_len)
    nb = s_len // blk

    def body(q_ref, k_ref, v_ref, c_ref, lse_ref, dl_ref, do_ref, dq_ref, dk_ref, dv_ref, dc_ref, dr_ref, dk_s, dv_s, dc_s, cb_s, dq_s):
        j = pl.program_id(1)

        @pl.when(j == 0)
        def _():
            dq_s[...] = jnp.zeros(dq_s.shape, F32)
            dr_ref[...] = jnp.zeros(dr_ref.shape, F32)

        dk_s[...] = jnp.zeros(dk_s.shape, F32)
        dv_s[...] = jnp.zeros(dv_s.shape, F32)
        dc_s[...] = jnp.zeros(dc_s.shape, F32)
        cb_s[...] = jnp.broadcast_to(c_ref[...], cb_s.shape)

        def tile(i, n, diag):
            rows = pl.ds(pl.multiple_of(i * blk, blk), n * blk)
            q, dob = q_ref[rows, :], do_ref[rows, :]
            k, v = k_ref[...], v_ref[...]
            s_t = lax.dot_general(k, q, _NT, preferred_element_type=F32) * FOX_SCALE - cb_s[:, : n * blk]
            p_t = jnp.exp(s_t - lse_ref[:, rows])
            if diag:
                p_t = jnp.where(_diag_mask(blk, transposed=True), p_t, 0.0)
            dp_t = lax.dot_general(v, dob, _NT, preferred_element_type=F32)
            ds_t = p_t * (dp_t - dl_ref[:, rows])
            dsb = ds_t.astype(BF16)
            dv_s[...] += jnp.dot(p_t.astype(BF16), dob, preferred_element_type=F32)
            dk_s[...] += jnp.dot(dsb, q, preferred_element_type=F32)
            dq_c = lax.dot_general(dsb, k, _TN, preferred_element_type=F32)
            part = ds_t[:, 0:128]
            for t in range(1, n * blk // 128):
                part = part + ds_t[:, t * 128 : (t + 1) * 128]
            dc_s[...] += part
            dr_ref[:, rows] += jnp.sum(ds_t, axis=0, keepdims=True)
            if diag:
                dq_s[rows, :] = (dq_s[rows, :] + dq_c) * FOX_SCALE
            else:
                dq_s[rows, :] += dq_c

        tile(j, 1, True)
        odd = (nb - 1 - j) % 2

        @pl.when(odd == 1)
        def _():
            tile(j + 1, 1, False)

        lax.fori_loop(0, (nb - 1 - j) // 2, lambda t, c: (tile(j + 1 + odd + 2 * t, 2, False), c)[1], 0)
        dk_ref[...] = (dk_s[...] * FOX_SCALE).astype(BF16)
        dv_ref[...] = dv_s[...].astype(BF16)
        dc_ref[...] = jnp.sum(dc_s[...], axis=1, keepdims=True)

        @pl.when(j == nb - 1)
        def _():
            dq_ref[...] = dq_s[...].astype(BF16)

    head = lambda h, j: (0, h)
    row = pl.BlockSpec((None, 1, s_len), lambda h, j: (h, 0, 0))
    return _pcall(
        body,
        name="fox_bwd",
        grid=(FOX_H, nb),
        out_shape=[
            jax.ShapeDtypeStruct((s_len, FOX_W), BF16),
            jax.ShapeDtypeStruct((s_len, FOX_W), BF16),
            jax.ShapeDtypeStruct((s_len, FOX_W), BF16),
            jax.ShapeDtypeStruct((FOX_H, s_len, 1), F32),
            jax.ShapeDtypeStruct((FOX_H, 1, s_len), F32),
        ],
        in_specs=[
            pl.BlockSpec((s_len, FOX_DH), head),
            pl.BlockSpec((blk, FOX_DH), lambda h, j: (j, FOX_H + h)),
            pl.BlockSpec((blk, FOX_DH), lambda h, j: (j, 2 * FOX_H + h)),
            pl.BlockSpec((None, blk, 1), lambda h, j: (h, j, 0)),
            row,
            row,
            pl.BlockSpec((s_len, FOX_DH), head),
        ],
        out_specs=[
            pl.BlockSpec((s_len, FOX_DH), head),
            pl.BlockSpec((blk, FOX_DH), lambda h, j: (j, h)),
            pl.BlockSpec((blk, FOX_DH), lambda h, j: (j, h)),
            pl.BlockSpec((None, blk, 1), lambda h, j: (h, j, 0)),
            row,
        ],
        scratch_shapes=[
            pltpu.VMEM((blk, FOX_DH), F32),
            pltpu.VMEM((blk, FOX_DH), F32),
            pltpu.VMEM((blk, 128), F32),
            pltpu.VMEM((blk, 2 * blk), F32),
            pltpu.VMEM((s_len, FOX_DH), F32),
        ],
        compiler_params=_cp("parallel", "arbitrary"),
    )(qkv, qkv, qkv, cum_col, lse_row, delta_row, do)


def _swa_bias():
    cols = SWA_G * WINDOW
    k = np.arange(2 * WINDOW)[:, None]
    q = np.arange(cols)[None, :]
    dist = (q % WINDOW) - k + WINDOW
    valid = (dist >= 0) & (dist < WINDOW)
    out = np.empty((2, SWA_HKV, 2 * WINDOW, cols), np.float32)
    for g in range(SWA_HKV):
        slope = np.array([SLOPES[g * SWA_G + t] for t in range(SWA_G)], np.float32)[q // WINDOW]
        bias = -(slope * dist.astype(np.float32))
        out[0, g] = np.where(valid & (k >= WINDOW), bias, np.float32(NEG))
        out[1, g] = np.where(valid, bias, np.float32(NEG))
    return jnp.asarray(out)


def _swa_group(i, q_ref, kk, sinks_ref, bias_ref, g):
    cols = SWA_G * WINDOW
    head = lax.broadcasted_iota(jnp.int32, (1, cols), 1) // WINDOW
    sink = jnp.zeros((1, cols), F32)
    for t in range(SWA_G):
        sink = jnp.where(head == t, sinks_ref[g * SWA_G + t], sink)
    q = jnp.concatenate([q_ref[:, (g * SWA_G + t) * SWA_DH : (g * SWA_G + t + 1) * SWA_DH] for t in range(SWA_G)], axis=0)
    k = kk[:, g * SWA_DH : (g + 1) * SWA_DH]
    s = lax.dot_general(k, q, _NT, preferred_element_type=F32) * SWA_SCALE + bias_ref[jnp.minimum(i, 1), g]
    m = jnp.maximum(jnp.max(s, axis=0, keepdims=True), sink)
    e = jnp.exp(s - m)
    e_sink = jnp.exp(sink - m)
    inv = 1.0 / (jnp.sum(e, axis=0, keepdims=True) + e_sink)
    return q, k, e * inv, e_sink * inv


def _swa_specs(col_q, col_k, col_v, rev, nb):
    def blk(t):
        return nb - 1 - t if rev else t

    return [
        pl.BlockSpec((WINDOW, SWA_W), lambda t: (blk(t), col_q)),
        pl.BlockSpec((WINDOW, SWA_KVW), lambda t: (jnp.maximum(blk(t) - 1, 0), col_k)),
        pl.BlockSpec((WINDOW, SWA_KVW), lambda t: (blk(t), col_k)),
        pl.BlockSpec((WINDOW, SWA_KVW), lambda t: (jnp.maximum(blk(t) - 1, 0), col_v)),
        pl.BlockSpec((WINDOW, SWA_KVW), lambda t: (blk(t), col_v)),
    ]


def _swa_fwd(qkv, sinks):
    s_len = qkv.shape[0]
    nb = s_len // WINDOW
    bias_spec = pl.BlockSpec((2, SWA_HKV, 2 * WINDOW, SWA_G * WINDOW), lambda t: (0, 0, 0, 0))

    def body(q_ref, kp_ref, kc_ref, vp_ref, vc_ref, sinks_ref, bias_ref, o_ref):
        i = pl.program_id(0)
        kk = jnp.concatenate([kp_ref[...], kc_ref[...]], axis=0)
        vv = jnp.concatenate([vp_ref[...], vc_ref[...]], axis=0)
        for g in range(SWA_HKV):
            _, _, p, _ = _swa_group(i, q_ref, kk, sinks_ref, bias_ref, g)
            o = lax.dot_general(p.astype(BF16), vv[:, g * SWA_DH : (g + 1) * SWA_DH], _TN, preferred_element_type=F32)
            for t in range(SWA_G):
                h = g * SWA_G + t
                o_ref[:, h * SWA_DH : (h + 1) * SWA_DH] = o[t * WINDOW : (t + 1) * WINDOW, :]

    return _pcall(
        body,
        name="swa_fwd",
        grid=(nb,),
        out_shape=jax.ShapeDtypeStruct((s_len, SWA_W), F32),
        in_specs=_swa_specs(0, 4, 5, False, nb) + [pl.BlockSpec(memory_space=pltpu.SMEM), bias_spec],
        out_specs=pl.BlockSpec((WINDOW, SWA_W), lambda t: (t, 0)),
        compiler_params=_cp("parallel"),
    )(qkv, qkv, qkv, qkv, qkv, sinks, _swa_bias())


def _swa_bwd(qkv, sinks, do):
    s_len = qkv.shape[0]
    nb = s_len // WINDOW
    bias_spec = pl.BlockSpec((2, SWA_HKV, 2 * WINDOW, SWA_G * WINDOW), lambda t: (0, 0, 0, 0))

    def body(q_ref, kp_ref, kc_ref, vp_ref, vc_ref, sinks_ref, bias_ref, do_ref, dq_ref, dk_ref, dv_ref, dsink_ref, ck_s, cv_s, dkk_s, dvv_s):
        t = pl.program_id(0)
        i = nb - 1 - t

        @pl.when(t == 0)
        def _():
            ck_s[...] = jnp.zeros(ck_s.shape, F32)
            cv_s[...] = jnp.zeros(cv_s.shape, F32)
            dsink_ref[...] = jnp.zeros(dsink_ref.shape, F32)

        kk = jnp.concatenate([kp_ref[...], kc_ref[...]], axis=0)
        vv = jnp.concatenate([vp_ref[...], vc_ref[...]], axis=0)
        lane = lax.broadcasted_iota(jnp.int32, (1, 128), 1)
        dsink = jnp.zeros((1, 128), F32)
        for g in range(SWA_HKV):
            cols = slice(g * SWA_DH, (g + 1) * SWA_DH)
            q, k, p, p_sink = _swa_group(i, q_ref, kk, sinks_ref, bias_ref, g)
            dob = jnp.concatenate([do_ref[:, (g * SWA_G + t) * SWA_DH : (g * SWA_G + t + 1) * SWA_DH] for t in range(SWA_G)], axis=0)
            dp = lax.dot_general(vv[:, cols], dob, _NT, preferred_element_type=F32)
            delta = jnp.sum(p * dp, axis=0, keepdims=True)
            dsb = (p * (dp - delta)).astype(BF16)
            dq = (lax.dot_general(dsb, k, _TN, preferred_element_type=F32) * SWA_SCALE).astype(BF16)
            ps_d = p_sink * delta
            for t in range(SWA_G):
                h = g * SWA_G + t
                dq_ref[:, h * SWA_DH : (h + 1) * SWA_DH] = dq[t * WINDOW : (t + 1) * WINDOW, :]
                dsink = dsink + jnp.where(lane == h, -jnp.sum(ps_d[:, t * WINDOW : (t + 1) * WINDOW], axis=1, keepdims=True), 0.0)
            dkk_s[:, cols] = jnp.dot(dsb, q, preferred_element_type=F32) * SWA_SCALE
            dvv_s[:, cols] = jnp.dot(p.astype(BF16), dob, preferred_element_type=F32)
        dk_ref[...] = (dkk_s[WINDOW:, :] + ck_s[...]).astype(BF16)
        dv_ref[...] = (dvv_s[WINDOW:, :] + cv_s[...]).astype(BF16)
        ck_s[...] = dkk_s[:WINDOW, :]
        cv_s[...] = dvv_s[:WINDOW, :]
        dsink_ref[...] += dsink

    row = lambda t: (nb - 1 - t, 0)
    return _pcall(
        body,
        name="swa_bwd",
        grid=(nb,),
        out_shape=[
            jax.ShapeDtypeStruct((s_len, SWA_W), BF16),
            jax.ShapeDtypeStruct((s_len, SWA_KVW), BF16),
            jax.ShapeDtypeStruct((s_len, SWA_KVW), BF16),
            jax.ShapeDtypeStruct((1, 128), F32),
        ],
        in_specs=_swa_specs(0, 4, 5, True, nb)
        + [pl.BlockSpec(memory_space=pltpu.SMEM), bias_spec, pl.BlockSpec((WINDOW, SWA_W), row)],
        out_specs=[
            pl.BlockSpec((WINDOW, SWA_W), row),
            pl.BlockSpec((WINDOW, SWA_KVW), row),
            pl.BlockSpec((WINDOW, SWA_KVW), row),
            pl.BlockSpec((1, 128), lambda t: (0, 0)),
        ],
        scratch_shapes=[
            pltpu.VMEM((WINDOW, SWA_KVW), F32),
            pltpu.VMEM((WINDOW, SWA_KVW), F32),
            pltpu.VMEM((2 * WINDOW, SWA_KVW), F32),
            pltpu.VMEM((2 * WINDOW, SWA_KVW), F32),
        ],
        compiler_params=_cp("arbitrary"),
    )(qkv, qkv, qkv, qkv, qkv, sinks, _swa_bias(), do)


def _branch_fwd(o, gates, g_blk, w_b, name):
    s_len, wd = o.shape
    d = w_b.shape[1]
    tm = min(512, s_len)

    def body(o_ref, g_ref, w_ref, y_ref, a_ref):
        g = g_ref[...].astype(F32)
        a = (o_ref[...] * (g * _sigmoid(g))).astype(BF16)
        a_ref[...] = a
        y_ref[...] = jnp.dot(a, w_ref[...], preferred_element_type=F32).astype(BF16)

    return _pcall(
        body,
        name=name,
        grid=(s_len // tm,),
        out_shape=[jax.ShapeDtypeStruct((s_len, d), BF16), jax.ShapeDtypeStruct((s_len, wd), BF16)],
        in_specs=[
            pl.BlockSpec((tm, wd), lambda i: (i, 0)),
            pl.BlockSpec((tm, wd), lambda i: (i, g_blk)),
            pl.BlockSpec((wd, d), lambda i: (0, 0)),
        ],
        out_specs=[pl.BlockSpec((tm, d), lambda i: (i, 0)), pl.BlockSpec((tm, wd), lambda i: (i, 0))],
        compiler_params=_cp("parallel"),
    )(o, gates, w_b)


def _out_stage(gates, mf_blk, y_fox, y_swa, w_out, x, ada, ln_g, ln_b, target):
    s_len, d = x.shape
    tm = min(256, s_len)
    n_steps = s_len // tm

    def body(mf_ref, ms_ref, yf_ref, ys_ref, w_ref, x_ref, gate_ref, lg_ref, lb_ref, t_ref, mg_ref, dza_ref, dsub_ref, red_ref):
        i = pl.program_id(0)
        merged = _sigmoid(mf_ref[...].astype(F32)) * yf_ref[...].astype(F32) + _sigmoid(ms_ref[...].astype(F32)) * ys_ref[...].astype(F32)
        mb = merged.astype(BF16)
        mg_ref[...] = mb
        sub = jnp.dot(mb, w_ref[...], preferred_element_type=F32)
        gate = gate_ref[...]
        z = ALPHA * x_ref[...] + gate * sub
        mu = jnp.mean(z, axis=-1, keepdims=True)
        zc = z - mu
        var = jnp.mean(zc * zc, axis=-1, keepdims=True)
        rstd = lax.rsqrt(var + LN_EPS)
        zhat = zc * rstd
        err = zhat * lg_ref[...] + lb_ref[...] - t_ref[...]
        dout = err * (1.0 / d)
        dzhat = dout * lg_ref[...]
        dz = rstd * (dzhat - jnp.mean(dzhat, axis=-1, keepdims=True) - zhat * jnp.mean(dzhat * zhat, axis=-1, keepdims=True))
        dza_ref[...] = ALPHA * dz
        dsub_ref[...] = (gate * dz).astype(BF16)
        part = jnp.concatenate(
            [
                jnp.sum(dz * sub, axis=0, keepdims=True),
                jnp.sum(dout * zhat, axis=0, keepdims=True),
                jnp.sum(dout, axis=0, keepdims=True),
                jnp.sum(err * err, axis=0, keepdims=True),
                jnp.zeros((4, d), F32),
            ],
            axis=0,
        )

        @pl.when(i == 0)
        def _():
            red_ref[...] = part

        @pl.when(i > 0)
        def _():
            red_ref[...] += part

        @pl.when(i == n_steps - 1)
        def _():
            red_ref[4:5, :] = jnp.broadcast_to(jnp.sum(red_ref[3:4, :], axis=1, keepdims=True), (1, d))

    row = pl.BlockSpec((tm, d), lambda i: (i, 0))
    vec = pl.BlockSpec((1, d), lambda i: (0, 0))
    return _pcall(
        body,
        name="out_stage",
        grid=(n_steps,),
        out_shape=[
            jax.ShapeDtypeStruct((s_len, d), BF16),
            jax.ShapeDtypeStruct((s_len, d), F32),
            jax.ShapeDtypeStruct((s_len, d), BF16),
            jax.ShapeDtypeStruct((8, d), F32),
        ],
        in_specs=[
            pl.BlockSpec((tm, d), lambda i: (i, mf_blk)),
            pl.BlockSpec((tm, d), lambda i: (i, mf_blk + 1)),
            row,
            row,
            pl.BlockSpec((d, d), lambda i: (0, 0), pipeline_mode=pl.Buffered(1)),
            row,
            pl.BlockSpec((1, d), lambda i: (0, 2)),
            vec,
            vec,
            row,
        ],
        out_specs=[row, row, row, pl.BlockSpec((8, d), lambda i: (0, 0))],
        compiler_params=_cp("arbitrary"),
    )(gates, gates, y_fox, y_swa, w_out, x, ada, ln_g, ln_b, target)


def _merge_bwd(dsub, w_out, gates, mf_blk, y_fox, y_swa):
    s_len, d = dsub.shape
    tm = min(256, s_len)

    def body(ds_ref, w_ref, mf_ref, ms_ref, yf_ref, ys_ref, dmf_ref, dms_ref, dyf_ref, dys_ref):
        dm = lax.dot_general(ds_ref[...], w_ref[...], _NT, preferred_element_type=F32)
        sf, ss = _sigmoid(mf_ref[...].astype(F32)), _sigmoid(ms_ref[...].astype(F32))
        dmf_ref[...] = (dm * yf_ref[...].astype(F32) * (sf * (1.0 - sf))).astype(BF16)
        dms_ref[...] = (dm * ys_ref[...].astype(F32) * (ss * (1.0 - ss))).astype(BF16)
        dyf_ref[...] = (dm * sf).astype(BF16)
        dys_ref[...] = (dm * ss).astype(BF16)

    row = pl.BlockSpec((tm, d), lambda i: (i, 0))
    return _pcall(
        body,
        name="merge_bwd",
        grid=(s_len // tm,),
        out_shape=[jax.ShapeDtypeStruct((s_len, d), BF16)] * 4,
        in_specs=[
            row,
            pl.BlockSpec((d, d), lambda i: (0, 0), pipeline_mode=pl.Buffered(1)),
            pl.BlockSpec((tm, d), lambda i: (i, mf_blk)),
            pl.BlockSpec((tm, d), lambda i: (i, mf_blk + 1)),
            row,
            row,
        ],
        out_specs=[row] * 4,
        compiler_params=_cp("parallel"),
    )(dsub, w_out, gates, gates, y_fox, y_swa)


def _branch_bwd(dy, w_b, o, gates, g_blk, name, n_heads):
    s_len, d = dy.shape
    wd = w_b.shape[0]
    tm = min(512, s_len)

    def body(dy_ref, w_ref, o_ref, g_ref, do_ref, dg_ref, *rest):
        da = lax.dot_general(dy_ref[...], w_ref[...], _NT, preferred_element_type=F32)
        g = g_ref[...].astype(F32)
        sg = _sigmoid(g)
        do = da * (g * sg)
        do_ref[...] = do.astype(BF16)
        o = o_ref[...]
        dg_ref[...] = (da * o * (sg * (1.0 + g * (1.0 - sg)))).astype(BF16)
        if n_heads:
            prod = do.astype(BF16).astype(F32) * o
            lane = lax.broadcasted_iota(jnp.int32, (1, 128), 1)
            delta = jnp.zeros((tm, 128), F32)
            for h in range(n_heads):
                dh = jnp.sum(prod[:, h * 128 : (h + 1) * 128], axis=1, keepdims=True)
                delta = delta + jnp.where(lane == h, dh, 0.0)
            rest[0][...] = delta

    out_shape = [jax.ShapeDtypeStruct((s_len, wd), BF16), jax.ShapeDtypeStruct((s_len, wd), BF16)]
    out_specs = [pl.BlockSpec((tm, wd), lambda i: (i, 0))] * 2
    if n_heads:
        out_shape.append(jax.ShapeDtypeStruct((s_len, 128), F32))
        out_specs.append(pl.BlockSpec((tm, 128), lambda i: (i, 0)))
    return _pcall(
        body,
        name=name,
        grid=(s_len // tm,),
        out_shape=out_shape,
        in_specs=[
            pl.BlockSpec((tm, d), lambda i: (i, 0)),
            pl.BlockSpec((wd, d), lambda i: (0, 0)),
            pl.BlockSpec((tm, wd), lambda i: (i, 0)),
            pl.BlockSpec((tm, wd), lambda i: (i, g_blk)),
        ],
        out_specs=out_specs,
        compiler_params=_cp("parallel"),
    )(dy, w_b, o, gates)


def _in_bwd(dproj, w_in_t, x, ada, dza, ride):
    s_len, d = x.shape
    k_tot = dproj.shape[1]
    tm, tk, dn = min(512, s_len), k_tot // 4, d // 2
    ni, nk = s_len // tm, k_tot // tk
    n = len(ride)

    def body(dp_ref, w_ref, x_ref, sc_ref, dza_ref, *rest):
        ins, (gx_ref, red_ref), outs = rest[:n], rest[n : n + 2], rest[n + 2 : 2 * n + 2]
        sems, acc_s = rest[2 * n + 2 : 2 * n + 5], rest[2 * n + 5]
        i, nh, kk = pl.program_id(0), pl.program_id(1), pl.program_id(2)

        @pl.when((i == 0) & (nh == 0) & (kk == 0))
        def _():
            _rider_start("exchange", ins, outs, *sems)

        @pl.when((i == ni - 1) & (nh == 1) & (kk == nk - 1))
        def _():
            _rider_wait("exchange", ins, outs, *sems)

        part = jnp.dot(dp_ref[...], w_ref[...], preferred_element_type=F32)
        half = pl.ds(pl.multiple_of(nh * dn, dn), dn)

        @pl.when(kk == 0)
        def _():
            acc_s[:, half] = part

        @pl.when(kk > 0)
        def _():
            acc_s[:, half] += part

        @pl.when((nh == 1) & (kk == nk - 1))
        def _():
            dh = acc_s[...]
            xv = x_ref[...]
            mu = jnp.mean(xv, axis=-1, keepdims=True)
            xc = xv - mu
            var = jnp.mean(xc * xc, axis=-1, keepdims=True)
            rstd = lax.rsqrt(var + LN_EPS)
            xhat = xc * rstd
            dxhat = dh * (1.0 + sc_ref[...])
            dx = rstd * (dxhat - jnp.mean(dxhat, axis=-1, keepdims=True) - xhat * jnp.mean(dxhat * xhat, axis=-1, keepdims=True))
            gx_ref[...] = dza_ref[...] + dx
            part_r = jnp.concatenate(
                [jnp.sum(dh, axis=0, keepdims=True), jnp.sum(dh * xhat, axis=0, keepdims=True), jnp.zeros((6, d), F32)], axis=0
            )

            @pl.when(i == 0)
            def _():
                red_ref[...] = part_r

            @pl.when(i > 0)
            def _():
                red_ref[...] += part_r

    row = pl.BlockSpec((tm, d), lambda i, nh, kk: (i, 0))
    hbm = pl.BlockSpec(memory_space=pltpu.HBM)
    return _pcall(
        body,
        name="in_bwd",
        grid=(ni, 2, nk),
        out_shape=[jax.ShapeDtypeStruct((s_len, d), F32), jax.ShapeDtypeStruct((8, d), F32)]
        + [jax.ShapeDtypeStruct(r.shape, r.dtype) for r in ride],
        in_specs=[
            pl.BlockSpec((tm, tk), lambda i, nh, kk: (i, kk)),
            pl.BlockSpec((tk, dn), lambda i, nh, kk: (kk, nh)),
            row,
            pl.BlockSpec((1, d), lambda i, nh, kk: (0, 1)),
            row,
        ]
        + [hbm] * n,
        out_specs=[row, pl.BlockSpec((8, d), lambda i, nh, kk: (0, 0))] + [hbm] * n,
        scratch_shapes=_rider_scratch(n) + [pltpu.VMEM((tm, d), F32)],
        compiler_params=_cp("arbitrary", "arbitrary", "arbitrary"),
    )(dproj, w_in_t, x, ada, dza, *ride)


def _pad_lanes(v, n):
    return jnp.pad(v, ((0, 0), (0, n - v.shape[1])))


def kernel(x, c, w_ada, b_ada, w_in, b_f, attn_sinks, w_br_fox, w_br_swa, w_out, ln_g, ln_b, loss_target, m_w_ada, m_b_ada, m_w_in, m_b_f, m_attn_sinks, m_w_br_fox, m_w_br_swa, m_w_out, m_ln_g, m_ln_b, v_w_ada, v_b_ada, v_w_in, v_b_f, v_attn_sinks, v_w_br_fox, v_w_br_swa, v_w_out, v_ln_g, v_ln_b):
    x2, tgt = x[0], loss_target[0]
    s_len, d = x2.shape
    me = 4 * lax.axis_index("x") + 2 * lax.axis_index("y") + lax.axis_index("c")
    off_ms = OFF_MF + d
    in_pad = off_ms + d
    c_ada = w_ada.shape[2]
    c_in = w_in.shape[2]
    c_br = w_br_fox.shape[2]

    w_in_full = _all_gather(w_in[0].T.astype(BF16), "ag_w_in", pltpu.HBM).reshape(N_DEV * c_in, d)
    w_in_pad = jnp.concatenate(
        [w_in_full[:REAL_FLOG_END], jnp.zeros((FLOG_PAD - N_FLOG, d), BF16), w_in_full[REAL_FLOG_END:]], axis=0
    )
    k_cut = REAL_FLOG_END // c_in

    c_all = _gather_rows(c, "ag_c")
    b_cols = lax.dynamic_slice(b_ada, (0, me * c_ada), (1, c_ada))
    ada_cols = _ada_fwd(c_all, w_ada[0], b_cols)
    ada_g = _all_gather(ada_cols, "ag_ada", pltpu.VMEM)
    ada = lax.dynamic_index_in_dim(ada_g, me, axis=1, keepdims=False).reshape(1, N_DEV * c_ada)

    h = _ln_mod(x2, ada)
    qkv_fox = _mm_cols(h, w_in_pad, OFF_FQ, 3 * FOX_W, BF16, "proj_fox")
    flog = _mm_cols(h, w_in_pad, OFF_FLOG, FLOG_PAD, F32, "proj_flog")
    qkv_swa = _mm_cols(h, w_in_pad, OFF_SQ, SWA_W + 2 * SWA_KVW, BF16, "proj_swa")
    gates, w_bf, w_bs, w_o = _mm_cols(
        h, w_in_pad, OFF_GF, in_pad - OFF_GF, BF16, "proj_gates",
        ride=(w_br_fox[0].astype(BF16), w_br_swa[0].astype(BF16), w_out[0].astype(BF16)),
    )
    w_bf = w_bf.reshape(N_DEV, FOX_W, c_br).transpose(1, 0, 2).reshape(FOX_W, d)
    w_bs = w_bs.reshape(N_DEV, SWA_W, c_br).transpose(1, 0, 2).reshape(SWA_W, d)
    w_o = w_o.reshape(d, d)
    mf_blk = (OFF_MF - OFF_GF) // d

    flog_t = flog[:, :N_FLOG].T
    bf_col = b_f.reshape(FOX_H, 1)
    cum = _fox_cum(flog_t, bf_col)
    cum_row = cum.reshape(FOX_H, 1, s_len)
    o_fox, lse = _fox_fwd(qkv_fox, cum_row)
    sinks = attn_sinks.reshape(SWA_HQ)
    o_swa = _swa_fwd(qkv_swa, sinks)

    y_fox, a_fox = _branch_fwd(o_fox, gates, 0, w_bf, "branch_fox")
    y_swa, a_swa = _branch_fwd(o_swa, gates, 1, w_bs, "branch_swa")
    merged, dza, dsub, red = _out_stage(gates, mf_blk, y_fox, y_swa, w_o, x2, ada, ln_g, ln_b, tgt)
    loss = lax.psum(0.5 * red[4, 0] / d, ("x", "y", "c"))

    dmf, dms, dy_fox, dy_swa = _merge_bwd(dsub, w_o, gates, mf_blk, y_fox, y_swa)
    do_fox, dg_fox, delta = _branch_bwd(dy_fox, w_bf, o_fox, gates, 0, "branch_fox_bwd", FOX_H)
    do_swa, dg_swa = _branch_bwd(dy_swa, w_bs, o_swa, gates, 1, "branch_swa_bwd", 0)
    delta_row = delta[:, :FOX_H].T.reshape(FOX_H, 1, s_len)
    dq_f, dk_f, dv_f, dcol, drow = _fox_bwd(
        qkv_fox, cum.reshape(FOX_H, s_len, 1), lse.reshape(FOX_H, 1, s_len), delta_row, do_fox
    )
    dflog_t, dbf = _fox_gate_bwd(drow.reshape(FOX_H, s_len), dcol.reshape(FOX_H, s_len), flog_t, bf_col)
    dq_s, dk_s, dv_s, dsink = _swa_bwd(qkv_swa, sinks, do_swa)
    dflog = _pad_lanes(dflog_t.T, FLOG_PAD).astype(BF16)
    dproj = jnp.concatenate([dq_f, dk_f, dv_f, dflog, dq_s, dk_s, dv_s, dg_fox, dg_swa, dmf, dms], axis=1)
    g_w_bf = _mm_tn(a_fox, dy_fox, "grad_w_br_fox")
    g_w_bs = _mm_tn(a_swa, dy_swa, "grad_w_br_swa")
    g_w_o = _mm_tn(merged, dsub, "grad_w_out")
    g_w_in, r_bf, r_bs, r_o = _mm_tn(
        dproj, h, "grad_w_in",
        ride=(
            g_w_bf.reshape(FOX_W, N_DEV, c_br).transpose(1, 0, 2),
            g_w_bs.reshape(SWA_W, N_DEV, c_br).transpose(1, 0, 2),
            g_w_o.reshape(N_DEV, d // N_DEV, d),
        ),
    )
    pad = FLOG_PAD - N_FLOG
    g_blocks = jnp.stack(
        [g_w_in[k * c_in : (k + 1) * c_in] for k in range(k_cut)]
        + [jnp.concatenate([g_w_in[k_cut * c_in : REAL_FLOG_END], g_w_in[OFF_SQ : (k_cut + 1) * c_in + pad]], axis=0)]
        + [g_w_in[k * c_in + pad : (k + 1) * c_in + pad] for k in range(k_cut + 1, N_DEV)]
    )

    grad_x, red2, r_in = _in_bwd(dproj, w_in_pad, x2, ada, dza, ride=(g_blocks,))
    out_w_in = _sum_adam_t(r_in, w_in[0].T, m_w_in[0].T, v_w_in[0].T, "adam_w_in")
    out_w_in = [o.T for o in out_w_in]
    out_w_bf = _sum_adam(r_bf, w_br_fox[0], m_w_br_fox[0], v_w_br_fox[0], "adam_w_br_fox")
    out_w_bs = _sum_adam(r_bs, w_br_swa[0], m_w_br_swa[0], v_w_br_swa[0], "adam_w_br_swa")
    out_w_o = _sum_adam(r_o, w_out[0], m_w_out[0], v_w_out[0], "adam_w_out")

    packed = jnp.concatenate([red2[0:1], red2[1:2], red[0:1], _pad_lanes(dbf[:, 0].reshape(1, FOX_H), 128), dsink, red[1:2], red[2:3]], axis=1)
    gathered = _gather_rows(packed, "ag_small")
    pack = lambda a, b, cc, dd, e: jnp.concatenate([a, _pad_lanes(b, 128), _pad_lanes(cc, 128), dd, e], axis=1)
    small = _small_adam(
        gathered,
        pack(b_ada, b_f, attn_sinks, ln_g, ln_b),
        pack(m_b_ada, m_b_f, m_attn_sinks, m_ln_g, m_ln_b),
        pack(v_b_ada, v_b_f, v_attn_sinks, v_ln_g, v_ln_b),
    )
    dada_cols = lax.dynamic_slice(gathered, (0, me * c_ada), (N_DEV, c_ada))
    out_w_ada = _wada_adam(c_all.T, dada_cols, w_ada[0], m_w_ada[0], v_w_ada[0])

    o1, o2, o3 = 3 * d, 3 * d + 128, 3 * d + 256

    def unpack(p):
        return p[:, :o1], p[:, o1 : o1 + FOX_H], p[:, o2 : o2 + SWA_HQ], p[:, o3 : o3 + d], p[:, o3 + d : o3 + 2 * d]

    kinds = []
    for k in range(4):
        b_ada_k, b_f_k, sinks_k, ln_g_k, ln_b_k = unpack(small[k])
        kinds.append(
            [out_w_ada[k][None], b_ada_k, out_w_in[k][None], b_f_k, sinks_k, out_w_bf[k][None], out_w_bs[k][None], out_w_o[k][None], ln_g_k, ln_b_k]
        )
    return (loss, grad_x[None], *kinds[0], *kinds[1], *kinds[2], *kinds[3])
```

```python
import numpy as np
import jax
import jax.numpy as jnp
from jax import lax
from jax.experimental import pallas as pl
from jax.experimental.pallas import tpu as pltpu

F32 = jnp.float32
BF16 = jnp.bfloat16
N_DEV = 8
MESH = pl.DeviceIdType.MESH

FOX_H, FOX_DH, FOX_W = 8, 128, 1024
SWA_HQ, SWA_HKV, SWA_DH, SWA_G = 16, 4, 64, 4
SWA_W, SWA_KVW, WINDOW = 1024, 256, 128
LN_EPS = 1e-5
NEG = -1e30
DEPTH = 1
ALPHA = (2.0 * DEPTH) ** 0.25
FOX_SCALE = FOX_DH ** -0.5
SWA_SCALE = SWA_DH ** -0.5
SLOPES = [2.0 ** (-8.0 * (h + 1.0) / SWA_HQ) for h in range(SWA_HQ)]

ADAM_LR, ADAM_B1, ADAM_B2, ADAM_EPS, ADAM_WD, ADAM_STEP = 0.001, 0.9, 0.999, 1e-08, 0.01, 10

N_FLOG = 8
FLOG_PAD = 512
OFF_FQ, OFF_FK, OFF_FV, OFF_FLOG = 0, 1024, 2048, 3072
OFF_SQ = OFF_FLOG + FLOG_PAD
OFF_SK = OFF_SQ + SWA_W
OFF_SV = OFF_SK + SWA_KVW
OFF_GF = OFF_SV + SWA_KVW
OFF_GS = OFF_GF + FOX_W
OFF_MF = OFF_GS + SWA_W
REAL_FLOG_END = OFF_FLOG + N_FLOG

ATT_BLK = 512
VMEM_LIMIT = 58 * 1024 * 1024


def _pcall(body, **kw):
    return pl.pallas_call(body, **kw)


def _cp(*sem):
    return pltpu.CompilerParams(dimension_semantics=sem, vmem_limit_bytes=VMEM_LIMIT)


def _sigmoid(x):
    return 0.5 * jnp.tanh(0.5 * x) + 0.5


def _all_gather(x, name, space):
    m_per, n = x.shape

    def body(x_ref, out_ref, send_sems, recv_sems, local_sem):
        mx, my, mc = lax.axis_index("x"), lax.axis_index("y"), lax.axis_index("c")
        me, sibling = (mx, my, mc), (mx, my, 1 - mc)
        xn, yn, dg = (1 - mx, my), (mx, 1 - my), (1 - mx, 1 - my)
        south = mc == 0
        src_chip = (jnp.where(south, 1 - mx, mx), jnp.where(south, my, 1 - my))
        dst_chip = (jnp.where(south, mx, 1 - mx), jnp.where(south, 1 - my, my))

        def rows(px, py, pc):
            return out_ref.at[4 * px + 2 * py + pc]

        def copy(k, block, to, src=None):
            return pltpu.make_async_remote_copy(
                src_ref=rows(*block) if src is None else src,
                dst_ref=rows(*block),
                send_sem=send_sems.at[k],
                recv_sem=recv_sems.at[k],
                device_id=to,
                device_id_type=MESH,
            )

        mine = pltpu.make_async_copy(x_ref, rows(*me), local_sem)
        mine.start()
        first = [copy(0, me, sibling, src=x_ref), copy(1, me, (*xn, mc), src=x_ref), copy(2, me, (*yn, mc), src=x_ref)]
        for cp in first:
            cp.start()
        copy(1, (*xn, mc), me).wait_recv()
        copy(2, (*yn, mc), me).wait_recv()
        later = [copy(3, (*src_chip, mc), (*dst_chip, mc)), copy(4, (*xn, mc), sibling), copy(5, (*yn, mc), sibling)]
        for cp in later:
            cp.start()
        copy(3, (*dg, mc), me).wait_recv()
        last = copy(6, (*dg, mc), sibling)
        last.start()
        copy(0, sibling, me).wait_recv()
        for k, chip in ((4, xn), (5, yn), (6, dg)):
            copy(k, (*chip, 1 - mc), me).wait_recv()
        for cp in first + later + [last]:
            cp.wait_send()
        mine.wait()

    return _pcall(
        body,
        name=name,
        out_shape=jax.ShapeDtypeStruct((N_DEV, m_per, n), x.dtype),
        in_specs=[pl.BlockSpec(memory_space=space)],
        out_specs=pl.BlockSpec(memory_space=space),
        scratch_shapes=[pltpu.SemaphoreType.DMA((7,)), pltpu.SemaphoreType.DMA((7,)), pltpu.SemaphoreType.DMA],
    )(x)


def _peer(d, mx, my, mc):
    return (1 - mx if (d >> 2) & 1 else mx, 1 - my if (d >> 1) & 1 else my, 1 - mc if d & 1 else mc)


def _rider_copies(kind, ins, outs, send_sems, recv_sems, local_sems):
    mx, my, mc = lax.axis_index("x"), lax.axis_index("y"), lax.axis_index("c")
    me = 4 * mx + 2 * my + mc
    remote, local = [], []
    for a in range(len(ins)):
        if kind == "gather":
            m_per = ins[a].shape[0]
            mine = outs[a].at[pl.ds(me * m_per, m_per), :]
            local.append(pltpu.make_async_copy(ins[a], mine, local_sems.at[a]))
        else:
            local.append(pltpu.make_async_copy(ins[a].at[me], outs[a].at[0], local_sems.at[a]))
        for d in range(1, N_DEV):
            px, py, pc = _peer(d, mx, my, mc)
            if kind == "gather":
                src, dst = ins[a], mine
            else:
                src, dst = ins[a].at[4 * px + 2 * py + pc], outs[a].at[d]
            remote.append(
                pltpu.make_async_remote_copy(
                    src_ref=src,
                    dst_ref=dst,
                    send_sem=send_sems.at[a * 7 + d - 1],
                    recv_sem=recv_sems.at[a * 7 + d - 1],
                    device_id=(px, py, pc),
                    device_id_type=MESH,
                )
            )
    return remote, local


def _rider_start(*args):
    remote, local = _rider_copies(*args)
    for cp in local + remote:
        cp.start()


def _rider_wait(*args):
    remote, local = _rider_copies(*args)
    for cp in remote:
        cp.wait_recv()
    for cp in remote:
        cp.wait_send()
    for cp in local:
        cp.wait()


def _rider_scratch(n):
    return [pltpu.SemaphoreType.DMA((7 * n,)), pltpu.SemaphoreType.DMA((7 * n,)), pltpu.SemaphoreType.DMA((n,))]


def _gather_rows(v, name):
    n = v.shape[1]
    return _all_gather(jnp.broadcast_to(v, (8, n)), name, pltpu.VMEM)[:, 0, :]


def _adamw(w, g, m, v):
    m = ADAM_B1 * m + (1.0 - ADAM_B1) * g
    v = ADAM_B2 * v + (1.0 - ADAM_B2) * (g * g)
    m_hat = m / (1.0 - ADAM_B1**ADAM_STEP)
    v_hat = v / (1.0 - ADAM_B2**ADAM_STEP)
    delta = -ADAM_LR * (m_hat / (jnp.sqrt(v_hat) + ADAM_EPS) + ADAM_WD * w)
    return delta, m, v


def _sum_adam(recv, w, m, v, name):
    _, r_tot, c = recv.shape
    c_pad = -(-c // 128) * 128
    tr = r_tot
    while 8 * tr * c_pad * 4 > 6 * 1024 * 1024 and tr % 32 == 0:
        tr //= 2

    def body(r_ref, w_ref, m_ref, v_ref, g_ref, d_ref, nm_ref, nv_ref):
        g = r_ref[0].astype(F32)
        for k in range(1, N_DEV):
            g = g + r_ref[k].astype(F32)
        d, nm, nv = _adamw(w_ref[...], g, m_ref[...], v_ref[...])
        g_ref[...] = g
        d_ref[...] = d
        nm_ref[...] = nm
        nv_ref[...] = nv

    blk = pl.BlockSpec((tr, c), lambda i: (i, 0))
    return _pcall(
        body,
        name=name,
        grid=(r_tot // tr,),
        out_shape=[jax.ShapeDtypeStruct((r_tot, c), F32)] * 4,
        in_specs=[pl.BlockSpec((N_DEV, tr, c), lambda i: (0, i, 0)), blk, blk, blk],
        out_specs=[blk] * 4,
        compiler_params=_cp("parallel"),
    )(recv, w, m, v)


def _sum_adam_t(recv, w, m, v, name):
    _, c, r_tot = recv.shape
    tr = min(256, r_tot)

    def body(r_ref, w_ref, m_ref, v_ref, g_ref, d_ref, nm_ref, nv_ref):
        g = r_ref[0].astype(F32)
        for k in range(1, N_DEV):
            g = g + r_ref[k].astype(F32)
        d, nm, nv = _adamw(w_ref[...], g, m_ref[...], v_ref[...])
        g_ref[...] = g
        d_ref[...] = d
        nm_ref[...] = nm
        nv_ref[...] = nv

    blk = pl.BlockSpec((c, tr), lambda i: (0, i))
    return _pcall(
        body,
        name=name,
        grid=(r_tot // tr,),
        out_shape=[jax.ShapeDtypeStruct((c, r_tot), F32)] * 4,
        in_specs=[pl.BlockSpec((N_DEV, c, tr), lambda i: (0, 0, i)), blk, blk, blk],
        out_specs=[blk] * 4,
        compiler_params=_cp("parallel"),
    )(recv, w, m, v)


def _wada_adam(c_t, dada_cols, w, m, v):
    d_model, c = w.shape
    tr = min(256, d_model)

    def body(ct_ref, da_ref, w_ref, m_ref, v_ref, g_ref, d_ref, nm_ref, nv_ref):
        g = jnp.dot(ct_ref[...].astype(BF16), da_ref[...].astype(BF16), preferred_element_type=F32)
        d, nm, nv = _adamw(w_ref[...], g, m_ref[...], v_ref[...])
        g_ref[...] = g
        d_ref[...] = d
        nm_ref[...] = nm
        nv_ref[...] = nv

    blk = pl.BlockSpec((tr, c), lambda i: (i, 0))
    return _pcall(
        body,
        name="wada_adam",
        grid=(d_model // tr,),
        out_shape=[jax.ShapeDtypeStruct((d_model, c), F32)] * 4,
        in_specs=[pl.BlockSpec((tr, N_DEV), lambda i: (i, 0)), pl.BlockSpec((N_DEV, c), lambda i: (0, 0)), blk, blk, blk],
        out_specs=[blk] * 4,
        compiler_params=_cp("parallel"),
    )(c_t, dada_cols, w, m, v)


def _small_adam(gathered, w, m, v):
    p = w.shape[1]

    def body(a_ref, w_ref, m_ref, v_ref, g_ref, d_ref, nm_ref, nv_ref):
        g = a_ref[0:1, :]
        for k in range(1, N_DEV):
            g = g + a_ref[k : k + 1, :]
        d, nm, nv = _adamw(w_ref[...], g, m_ref[...], v_ref[...])
        g_ref[...] = g
        d_ref[...] = d
        nm_ref[...] = nm
        nv_ref[...] = nv

    return _pcall(
        body,
        name="small_adam",
        out_shape=[jax.ShapeDtypeStruct((1, p), F32)] * 4,
    )(gathered, w, m, v)


def _ada_fwd(c_all, w_ada, b_cols):
    c = w_ada.shape[1]

    def body(c_ref, w_ref, b_ref, o_ref):
        o_ref[...] = jnp.dot(c_ref[...].astype(BF16), w_ref[...].astype(BF16), preferred_element_type=F32) + b_ref[...]

    return _pcall(
        body,
        name="ada_fwd",
        out_shape=jax.ShapeDtypeStruct((N_DEV, c), F32),
        compiler_params=_cp(),
    )(c_all, w_ada, b_cols)


def _ln_mod(x, ada):
    s_len, d = x.shape
    tm = min(512, s_len)

    def body(x_ref, sh_ref, sc_ref, h_ref):
        xv = x_ref[...]
        mu = jnp.mean(xv, axis=-1, keepdims=True)
        xc = xv - mu
        var = jnp.mean(xc * xc, axis=-1, keepdims=True)
        xhat = xc * lax.rsqrt(var + LN_EPS)
        h_ref[...] = (xhat * (1.0 + sc_ref[...]) + sh_ref[...]).astype(BF16)

    return _pcall(
        body,
        name="ln_mod",
        grid=(s_len // tm,),
        out_shape=jax.ShapeDtypeStruct((s_len, d), BF16),
        in_specs=[
            pl.BlockSpec((tm, d), lambda i: (i, 0)),
            pl.BlockSpec((1, d), lambda i: (0, 0)),
            pl.BlockSpec((1, d), lambda i: (0, 1)),
        ],
        out_specs=pl.BlockSpec((tm, d), lambda i: (i, 0)),
        compiler_params=_cp("parallel"),
    )(x, ada, ada)


def _mm_cols(a, b, col_off, n_cols, out_dtype, name, ride=()):
    m, k = a.shape
    tm, tn = min(1024, m), 512
    off = col_off // tn
    ni, nj = m // tm, n_cols // tn
    n = len(ride)

    def body(a_ref, b_ref, *rest):
        ins, o_ref, outs, sems = rest[:n], rest[n], rest[n + 1 : 2 * n + 1], rest[2 * n + 1 :]
        i, j = pl.program_id(0), pl.program_id(1)
        if n:

            @pl.when((i == 0) & (j == 0))
            def _():
                _rider_start("gather", ins, outs, *sems)

        o_ref[...] = lax.dot_general(a_ref[...], b_ref[...], _NT, preferred_element_type=F32).astype(out_dtype)
        if n:

            @pl.when((i == ni - 1) & (j == nj - 1))
            def _():
                _rider_wait("gather", ins, outs, *sems)

    hbm = pl.BlockSpec(memory_space=pltpu.HBM)
    out = _pcall(
        body,
        name=name,
        grid=(ni, nj),
        out_shape=[jax.ShapeDtypeStruct((m, n_cols), out_dtype)]
        + [jax.ShapeDtypeStruct((N_DEV * r.shape[0], r.shape[1]), r.dtype) for r in ride],
        in_specs=[pl.BlockSpec((tm, k), lambda i, j: (i, 0)), pl.BlockSpec((tn, k), lambda i, j: (off + j, 0))] + [hbm] * n,
        out_specs=[pl.BlockSpec((tm, tn), lambda i, j: (i, j))] + [hbm] * n,
        scratch_shapes=_rider_scratch(n) if n else [],
        compiler_params=_cp("arbitrary", "arbitrary") if n else _cp("parallel", "parallel"),
    )(a, b, *ride)
    return out if n else out[0]


def _mm_tn(a, b, name, ride=()):
    s_len, m = a.shape
    n = b.shape[1]
    tm, tn, ts = min(1024, m), min(1024, n), min(2048, s_len)
    ni, nj, ns = m // tm, n // tn, s_len // ts
    nr = len(ride)

    def body(a_ref, b_ref, *rest):
        ins, o_ref, outs = rest[:nr], rest[nr], rest[nr + 1 : 2 * nr + 1]
        sems, acc_s = rest[2 * nr + 1 : -1], rest[-1]
        i, j, kk = pl.program_id(0), pl.program_id(1), pl.program_id(2)
        if nr:

            @pl.when((i == 0) & (j == 0) & (kk == 0))
            def _():
                _rider_start("exchange", ins, outs, *sems)

            @pl.when((i == ni - 1) & (j == nj - 1) & (kk == ns - 1))
            def _():
                _rider_wait("exchange", ins, outs, *sems)

        part = lax.dot_general(a_ref[...], b_ref[...], _TN, preferred_element_type=F32)

        @pl.when(kk == 0)
        def _():
            acc_s[...] = part

        @pl.when(kk > 0)
        def _():
            acc_s[...] += part

        @pl.when(kk == ns - 1)
        def _():
            o_ref[...] = acc_s[...].astype(BF16)

    hbm = pl.BlockSpec(memory_space=pltpu.HBM)
    out = _pcall(
        body,
        name=name,
        grid=(ni, nj, ns),
        out_shape=[jax.ShapeDtypeStruct((m, n), BF16)] + [jax.ShapeDtypeStruct(r.shape, r.dtype) for r in ride],
        in_specs=[pl.BlockSpec((ts, tm), lambda i, j, kk: (kk, i)), pl.BlockSpec((ts, tn), lambda i, j, kk: (kk, j))] + [hbm] * nr,
        out_specs=[pl.BlockSpec((tm, tn), lambda i, j, kk: (i, j))] + [hbm] * nr,
        scratch_shapes=(_rider_scratch(nr) if nr else []) + [pltpu.VMEM((tm, tn), F32)],
        compiler_params=_cp("arbitrary", "arbitrary", "arbitrary") if nr else _cp("parallel", "parallel", "arbitrary"),
    )(a, b, *ride)
    return out if nr else out[0]


def _split3(a):
    hi = a.astype(BF16)
    r1 = a - hi.astype(F32)
    mid = r1.astype(BF16)
    lo = (r1 - mid.astype(F32)).astype(BF16)
    return hi, mid, lo


def _dot_ones(a, tri):
    return sum(jnp.dot(t, tri, preferred_element_type=F32) for t in _split3(a))


def _log_sigmoid(x):
    return jnp.minimum(x, 0.0) - jnp.log1p(jnp.exp(-jnp.abs(x)))


def _fox_cum(flog_t, bf_col):
    s_len = flog_t.shape[1]

    def body(fl_ref, bf_ref, cum_ref):
        r = lax.broadcasted_iota(jnp.int32, (128, 128), 0)
        c = lax.broadcasted_iota(jnp.int32, (128, 128), 1)
        upper = (r <= c).astype(BF16)

        def step(t, carry):
            sl = pl.ds(pl.multiple_of(t * 128, 128), 128)
            lf = _log_sigmoid(fl_ref[:, sl] + bf_ref[...])
            cs = _dot_ones(lf, upper) + carry
            cum_ref[:, sl] = cs
            return cs[:, 127:128]

        lax.fori_loop(0, s_len // 128, step, jnp.zeros((FOX_H, 1), F32))

    return _pcall(body, name="fox_cum", out_shape=jax.ShapeDtypeStruct((FOX_H, s_len), F32))(flog_t, bf_col)


def _fox_gate_bwd(drow, dcol, flog_t, bf_col):
    s_len = flog_t.shape[1]
    n = s_len // 128

    def body(dr_ref, dc_ref, fl_ref, bf_ref, dfl_ref, dbf_ref):
        r = lax.broadcasted_iota(jnp.int32, (128, 128), 0)
        c = lax.broadcasted_iota(jnp.int32, (128, 128), 1)
        lower = (r >= c).astype(BF16)

        def step(t, carry):
            run, tot = carry
            sl = pl.ds(pl.multiple_of((n - 1 - t) * 128, 128), 128)
            rc = _dot_ones(dr_ref[:, sl] - dc_ref[:, sl], lower) + run
            dfl = rc * _sigmoid(-(fl_ref[:, sl] + bf_ref[...]))
            dfl_ref[:, sl] = dfl
            return rc[:, 0:1], tot + jnp.sum(dfl, axis=1, keepdims=True)

        zero = jnp.zeros((FOX_H, 1), F32)
        _, tot = lax.fori_loop(0, n, step, (zero, zero))
        dbf_ref[...] = jnp.broadcast_to(tot, (FOX_H, 128))

    return _pcall(
        body,
        name="fox_gate_bwd",
        out_shape=[jax.ShapeDtypeStruct((FOX_H, s_len), F32), jax.ShapeDtypeStruct((FOX_H, 128), F32)],
    )(drow, dcol, flog_t, bf_col)


def _diag_mask(blk, transposed=False):
    r = lax.broadcasted_iota(jnp.int32, (blk, blk), 0)
    c = lax.broadcasted_iota(jnp.int32, (blk, blk), 1)
    return c >= r if transposed else r >= c


_NT = (((1,), (1,)), ((), ()))
_TN = (((0,), (0,)), ((), ()))


def _fox_fwd(qkv, cum_row):
    s_len = qkv.shape[0]
    blk = min(ATT_BLK, s_len)
    nb = s_len // blk
    log2e = 1.4426950408889634

    def body(q_ref, k_ref, v_ref, c_ref, o_ref, lse_ref, mx_s, acc_s, u_s):
        i = pl.program_id(1)

        def key_cols(j, n):
            return pl.ds(pl.multiple_of(j * blk, blk), n * blk)

        def walk(tile):
            lax.fori_loop(0, i // 2, lambda t, c: (tile(2 * t, 2, False), c)[1], 0)

            @pl.when(i % 2 == 1)
            def _():
                tile(i - 1, 1, False)

            tile(i, 1, True)

        def lane_max(j, n, masked):
            cols = key_cols(j, n)
            u = lax.dot_general(q_ref[...], k_ref[cols, :], _NT, preferred_element_type=F32) * (FOX_SCALE * log2e) - c_ref[:, cols] * log2e
            if masked:
                u = jnp.where(_diag_mask(blk), u, NEG)
            u_s[:, cols] = u
            part = u[:, 0:128]
            for t in range(1, n * blk // 128):
                part = jnp.maximum(part, u[:, t * 128 : (t + 1) * 128])
            mx_s[...] = jnp.maximum(mx_s[...], part)

        mx_s[...] = jnp.full(mx_s.shape, NEG, F32)
        walk(lane_max)
        m = jnp.max(mx_s[...], axis=1, keepdims=True)

        def weigh(j, n, masked):
            cols = key_cols(j, n)
            p = jnp.exp2(u_s[:, cols] - m)
            ones_col = (lax.broadcasted_iota(jnp.int32, (n * blk, 128), 1) == 0).astype(BF16)
            v1 = jnp.concatenate([v_ref[cols, :], ones_col], axis=1)
            acc_s[...] += jnp.dot(p.astype(BF16), v1, preferred_element_type=F32)

        acc_s[...] = jnp.zeros(acc_s.shape, F32)
        walk(weigh)
        l = acc_s[:, FOX_DH : FOX_DH + 1]
        o_ref[...] = acc_s[:, :FOX_DH] / l
        lse_ref[...] = m * (1.0 / log2e) + jnp.log(l)

    return _pcall(
        body,
        name="fox_fwd",
        grid=(FOX_H, nb),
        out_shape=[jax.ShapeDtypeStruct((s_len, FOX_W), F32), jax.ShapeDtypeStruct((FOX_H, s_len, 1), F32)],
        in_specs=[
            pl.BlockSpec((blk, FOX_DH), lambda h, i: (i, h)),
            pl.BlockSpec((s_len, FOX_DH), lambda h, i: (0, FOX_H + h)),
            pl.BlockSpec((s_len, FOX_DH), lambda h, i: (0, 2 * FOX_H + h)),
            pl.BlockSpec((None, 1, s_len), lambda h, i: (h, 0, 0)),
        ],
        out_specs=[
            pl.BlockSpec((blk, FOX_DH), lambda h, i: (i, h)),
            pl.BlockSpec((None, blk, 1), lambda h, i: (h, i, 0)),
        ],
        scratch_shapes=[pltpu.VMEM((blk, 128), F32), pltpu.VMEM((blk, 2 * FOX_DH), F32), pltpu.VMEM((blk, s_len), F32)],
        compiler_params=_cp("parallel", "arbitrary"),
    )(qkv, qkv, qkv, cum_row)


def _fox_bwd(qkv, cum_col, lse_row, delta_row, do):
    s_len = qkv.shape[0]
    blk = min(ATT_BLK, s_len)
    nb = s_len // blk

    def body(q_ref, k_ref, v_ref, c_ref, lse_ref, dl_ref, do_ref, dq_ref, dk_ref, dv_ref, dc_ref, dr_ref, dk_s, dv_s, dc_s, cb_s, dq_s):
        j = pl.program_id(1)

        @pl.when(j == 0)
        def _():
            dq_s[...] = jnp.zeros(dq_s.shape, F32)
            dr_ref[...] = jnp.zeros(dr_ref.shape, F32)

        dk_s[...] = jnp.zeros(dk_s.shape, F32)
        dv_s[...] = jnp.zeros(dv_s.shape, F32)
        dc_s[...] = jnp.zeros(dc_s.shape, F32)
        cb_s[...] = jnp.broadcast_to(c_ref[...], cb_s.shape)

        def tile(i, n, diag):
            rows = pl.ds(pl.multiple_of(i * blk, blk), n * blk)
            q, dob = q_ref[rows, :], do_ref[rows, :]
            k, v = k_ref[...], v_ref[...]
            s_t = lax.dot_general(k, q, _NT, preferred_element_type=F32) * FOX_SCALE - cb_s[:, : n * blk]
            p_t = jnp.exp(s_t - lse_ref[:, rows])
            if diag:
                p_t = jnp.where(_diag_mask(blk, transposed=True), p_t, 0.0)
            dp_t = lax.dot_general(v, dob, _NT, preferred_element_type=F32)
            ds_t = p_t * (dp_t - dl_ref[:, rows])
            dsb = ds_t.astype(BF16)
            dv_s[...] += jnp.dot(p_t.astype(BF16), dob, preferred_element_type=F32)
            dk_s[...] += jnp.dot(dsb, q, preferred_element_type=F32)
            dq_c = lax.dot_general(dsb, k, _TN, preferred_element_type=F32)
            part = ds_t[:, 0:128]
            for t in range(1, n * blk // 128):
                part = part + ds_t[:, t * 128 : (t + 1) * 128]
            dc_s[...] += part
            dr_ref[:, rows] += jnp.sum(ds_t, axis=0, keepdims=True)
            if diag:
                dq_s[rows, :] = (dq_s[rows, :] + dq_c) * FOX_SCALE
            else:
                dq_s[rows, :] += dq_c

        tile(j, 1, True)
        below = nb - 1 - j
        odd, odd_pair = below % 2, (below // 2) % 2

        @pl.when(odd == 1)
        def _():
            tile(j + 1, 1, False)

        @pl.when(odd_pair == 1)
        def _():
            tile(j + 1 + odd, 2, False)

        first = j + 1 + odd + 2 * odd_pair

        def two_pairs(t, carry):
            tile(first + 4 * t, 2, False)
            tile(first + 4 * t + 2, 2, False)
            return carry

        lax.fori_loop(0, below // 4, two_pairs, 0)
        dk_ref[...] = (dk_s[...] * FOX_SCALE).astype(BF16)
        dv_ref[...] = dv_s[...].astype(BF16)
        dc_ref[...] = jnp.sum(dc_s[...], axis=1, keepdims=True)

        @pl.when(j == nb - 1)
        def _():
            dq_ref[...] = dq_s[...].astype(BF16)

    head = lambda h, j: (0, h)
    row = pl.BlockSpec((None, 1, s_len), lambda h, j: (h, 0, 0))
    return _pcall(
        body,
        name="fox_bwd",
        grid=(FOX_H, nb),
        out_shape=[
            jax.ShapeDtypeStruct((s_len, FOX_W), BF16),
            jax.ShapeDtypeStruct((s_len, FOX_W), BF16),
            jax.ShapeDtypeStruct((s_len, FOX_W), BF16),
            jax.ShapeDtypeStruct((FOX_H, s_len, 1), F32),
            jax.ShapeDtypeStruct((FOX_H, 1, s_len), F32),
        ],
        in_specs=[
            pl.BlockSpec((s_len, FOX_DH), head),
            pl.BlockSpec((blk, FOX_DH), lambda h, j: (j, FOX_H + h)),
            pl.BlockSpec((blk, FOX_DH), lambda h, j: (j, 2 * FOX_H + h)),
            pl.BlockSpec((None, blk, 1), lambda h, j: (h, j, 0)),
            row,
            row,
            pl.BlockSpec((s_len, FOX_DH), head),
        ],
        out_specs=[
            pl.BlockSpec((s_len, FOX_DH), head),
            pl.BlockSpec((blk, FOX_DH), lambda h, j: (j, h)),
            pl.BlockSpec((blk, FOX_DH), lambda h, j: (j, h)),
            pl.BlockSpec((None, blk, 1), lambda h, j: (h, j, 0)),
            row,
        ],
        scratch_shapes=[
            pltpu.VMEM((blk, FOX_DH), F32),
            pltpu.VMEM((blk, FOX_DH), F32),
            pltpu.VMEM((blk, 128), F32),
            pltpu.VMEM((blk, 2 * blk), F32),
            pltpu.VMEM((s_len, FOX_DH), F32),
        ],
        compiler_params=_cp("parallel", "arbitrary"),
    )(qkv, qkv, qkv, cum_col, lse_row, delta_row, do)


def _swa_bias():
    cols = SWA_G * WINDOW
    k = np.arange(2 * WINDOW)[:, None]
    q = np.arange(cols)[None, :]
    dist = (q % WINDOW) - k + WINDOW
    valid = (dist >= 0) & (dist < WINDOW)
    out = np.empty((2, SWA_HKV, 2 * WINDOW, cols), np.float32)
    for g in range(SWA_HKV):
        slope = np.array([SLOPES[g * SWA_G + t] for t in range(SWA_G)], np.float32)[q // WINDOW]
        bias = -(slope * dist.astype(np.float32))
        out[0, g] = np.where(valid & (k >= WINDOW), bias, np.float32(NEG))
        out[1, g] = np.where(valid, bias, np.float32(NEG))
    return jnp.asarray(out)


def _swa_group(i, q_ref, kk, sinks_ref, bias_ref, g):
    cols = SWA_G * WINDOW
    head = lax.broadcasted_iota(jnp.int32, (1, cols), 1) // WINDOW
    sink = jnp.zeros((1, cols), F32)
    for t in range(SWA_G):
        sink = jnp.where(head == t, sinks_ref[g * SWA_G + t], sink)
    q = jnp.concatenate([q_ref[:, (g * SWA_G + t) * SWA_DH : (g * SWA_G + t + 1) * SWA_DH] for t in range(SWA_G)], axis=0)
    k = kk[:, g * SWA_DH : (g + 1) * SWA_DH]
    s = lax.dot_general(k, q, _NT, preferred_element_type=F32) * SWA_SCALE + bias_ref[jnp.minimum(i, 1), g]
    m = jnp.maximum(jnp.max(s, axis=0, keepdims=True), sink)
    e = jnp.exp(s - m)
    e_sink = jnp.exp(sink - m)
    inv = 1.0 / (jnp.sum(e, axis=0, keepdims=True) + e_sink)
    return q, k, e * inv, e_sink * inv


def _swa_specs(col_q, col_k, col_v, rev, nb):
    def blk(t):
        return nb - 1 - t if rev else t

    return [
        pl.BlockSpec((WINDOW, SWA_W), lambda t: (blk(t), col_q)),
        pl.BlockSpec((WINDOW, SWA_KVW), lambda t: (jnp.maximum(blk(t) - 1, 0), col_k)),
        pl.BlockSpec((WINDOW, SWA_KVW), lambda t: (blk(t), col_k)),
        pl.BlockSpec((WINDOW, SWA_KVW), lambda t: (jnp.maximum(blk(t) - 1, 0), col_v)),
        pl.BlockSpec((WINDOW, SWA_KVW), lambda t: (blk(t), col_v)),
    ]


def _swa_fwd(qkv, sinks):
    s_len = qkv.shape[0]
    nb = s_len // WINDOW
    bias_spec = pl.BlockSpec((2, SWA_HKV, 2 * WINDOW, SWA_G * WINDOW), lambda t: (0, 0, 0, 0))

    def body(q_ref, kp_ref, kc_ref, vp_ref, vc_ref, sinks_ref, bias_ref, o_ref):
        i = pl.program_id(0)
        kk = jnp.concatenate([kp_ref[...], kc_ref[...]], axis=0)
        vv = jnp.concatenate([vp_ref[...], vc_ref[...]], axis=0)
        for g in range(SWA_HKV):
            _, _, p, _ = _swa_group(i, q_ref, kk, sinks_ref, bias_ref, g)
            o = lax.dot_general(p.astype(BF16), vv[:, g * SWA_DH : (g + 1) * SWA_DH], _TN, preferred_element_type=F32)
            for t in range(SWA_G):
                h = g * SWA_G + t
                o_ref[:, h * SWA_DH : (h + 1) * SWA_DH] = o[t * WINDOW : (t + 1) * WINDOW, :]

    return _pcall(
        body,
        name="swa_fwd",
        grid=(nb,),
        out_shape=jax.ShapeDtypeStruct((s_len, SWA_W), F32),
        in_specs=_swa_specs(0, 4, 5, False, nb) + [pl.BlockSpec(memory_space=pltpu.SMEM), bias_spec],
        out_specs=pl.BlockSpec((WINDOW, SWA_W), lambda t: (t, 0)),
        compiler_params=_cp("parallel"),
    )(qkv, qkv, qkv, qkv, qkv, sinks, _swa_bias())


def _swa_bwd(qkv, sinks, do):
    s_len = qkv.shape[0]
    nb = s_len // WINDOW
    bias_spec = pl.BlockSpec((2, SWA_HKV, 2 * WINDOW, SWA_G * WINDOW), lambda t: (0, 0, 0, 0))

    def body(q_ref, kp_ref, kc_ref, vp_ref, vc_ref, sinks_ref, bias_ref, do_ref, dq_ref, dk_ref, dv_ref, dsink_ref, ck_s, cv_s, dkk_s, dvv_s):
        t = pl.program_id(0)
        i = nb - 1 - t

        @pl.when(t == 0)
        def _():
            ck_s[...] = jnp.zeros(ck_s.shape, F32)
            cv_s[...] = jnp.zeros(cv_s.shape, F32)
            dsink_ref[...] = jnp.zeros(dsink_ref.shape, F32)

        kk = jnp.concatenate([kp_ref[...], kc_ref[...]], axis=0)
        vv = jnp.concatenate([vp_ref[...], vc_ref[...]], axis=0)
        lane = lax.broadcasted_iota(jnp.int32, (1, 128), 1)
        dsink = jnp.zeros((1, 128), F32)
        for g in range(SWA_HKV):
            cols = slice(g * SWA_DH, (g + 1) * SWA_DH)
            q, k, p, p_sink = _swa_group(i, q_ref, kk, sinks_ref, bias_ref, g)
            dob = jnp.concatenate([do_ref[:, (g * SWA_G + t) * SWA_DH : (g * SWA_G + t + 1) * SWA_DH] for t in range(SWA_G)], axis=0)
            dp = lax.dot_general(vv[:, cols], dob, _NT, preferred_element_type=F32)
            delta = jnp.sum(p * dp, axis=0, keepdims=True)
            dsb = (p * (dp - delta)).astype(BF16)
            dq = (lax.dot_general(dsb, k, _TN, preferred_element_type=F32) * SWA_SCALE).astype(BF16)
            ps_d = p_sink * delta
            for t in range(SWA_G):
                h = g * SWA_G + t
                dq_ref[:, h * SWA_DH : (h + 1) * SWA_DH] = dq[t * WINDOW : (t + 1) * WINDOW, :]
                dsink = dsink + jnp.where(lane == h, -jnp.sum(ps_d[:, t * WINDOW : (t + 1) * WINDOW], axis=1, keepdims=True), 0.0)
            dkk_s[:, cols] = jnp.dot(dsb, q, preferred_element_type=F32) * SWA_SCALE
            dvv_s[:, cols] = jnp.dot(p.astype(BF16), dob, preferred_element_type=F32)
        dk_ref[...] = (dkk_s[WINDOW:, :] + ck_s[...]).astype(BF16)
        dv_ref[...] = (dvv_s[WINDOW:, :] + cv_s[...]).astype(BF16)
        ck_s[...] = dkk_s[:WINDOW, :]
        cv_s[...] = dvv_s[:WINDOW, :]
        dsink_ref[...] += dsink

    row = lambda t: (nb - 1 - t, 0)
    return _pcall(
        body,
        name="swa_bwd",
        grid=(nb,),
        out_shape=[
            jax.ShapeDtypeStruct((s_len, SWA_W), BF16),
            jax.ShapeDtypeStruct((s_len, SWA_KVW), BF16),
            jax.ShapeDtypeStruct((s_len, SWA_KVW), BF16),
            jax.ShapeDtypeStruct((1, 128), F32),
        ],
        in_specs=_swa_specs(0, 4, 5, True, nb)
        + [pl.BlockSpec(memory_space=pltpu.SMEM), bias_spec, pl.BlockSpec((WINDOW, SWA_W), row)],
        out_specs=[
            pl.BlockSpec((WINDOW, SWA_W), row),
            pl.BlockSpec((WINDOW, SWA_KVW), row),
            pl.BlockSpec((WINDOW, SWA_KVW), row),
            pl.BlockSpec((1, 128), lambda t: (0, 0)),
        ],
        scratch_shapes=[
            pltpu.VMEM((WINDOW, SWA_KVW), F32),
            pltpu.VMEM((WINDOW, SWA_KVW), F32),
            pltpu.VMEM((2 * WINDOW, SWA_KVW), F32),
            pltpu.VMEM((2 * WINDOW, SWA_KVW), F32),
        ],
        compiler_params=_cp("arbitrary"),
    )(qkv, qkv, qkv, qkv, qkv, sinks, _swa_bias(), do)


def _branch_fwd(o, gates, g_blk, w_b, name):
    s_len, wd = o.shape
    d = w_b.shape[1]
    tm = min(512, s_len)

    def body(o_ref, g_ref, w_ref, y_ref, a_ref):
        g = g_ref[...].astype(F32)
        a = (o_ref[...] * (g * _sigmoid(g))).astype(BF16)
        a_ref[...] = a
        y_ref[...] = jnp.dot(a, w_ref[...], preferred_element_type=F32).astype(BF16)

    return _pcall(
        body,
        name=name,
        grid=(s_len // tm,),
        out_shape=[jax.ShapeDtypeStruct((s_len, d), BF16), jax.ShapeDtypeStruct((s_len, wd), BF16)],
        in_specs=[
            pl.BlockSpec((tm, wd), lambda i: (i, 0)),
            pl.BlockSpec((tm, wd), lambda i: (i, g_blk)),
            pl.BlockSpec((wd, d), lambda i: (0, 0)),
        ],
        out_specs=[pl.BlockSpec((tm, d), lambda i: (i, 0)), pl.BlockSpec((tm, wd), lambda i: (i, 0))],
        compiler_params=_cp("parallel"),
    )(o, gates, w_b)


def _out_stage(gates, mf_blk, y_fox, y_swa, w_out, x, ada, ln_g, ln_b, target):
    s_len, d = x.shape
    tm = min(256, s_len)
    n_steps = s_len // tm

    def body(mf_ref, ms_ref, yf_ref, ys_ref, w_ref, x_ref, gate_ref, lg_ref, lb_ref, t_ref, mg_ref, dza_ref, dsub_ref, red_ref):
        i = pl.program_id(0)
        merged = _sigmoid(mf_ref[...].astype(F32)) * yf_ref[...].astype(F32) + _sigmoid(ms_ref[...].astype(F32)) * ys_ref[...].astype(F32)
        mb = merged.astype(BF16)
        mg_ref[...] = mb
        sub = jnp.dot(mb, w_ref[...], preferred_element_type=F32)
        gate = gate_ref[...]
        z = ALPHA * x_ref[...] + gate * sub
        mu = jnp.mean(z, axis=-1, keepdims=True)
        zc = z - mu
        var = jnp.mean(zc * zc, axis=-1, keepdims=True)
        rstd = lax.rsqrt(var + LN_EPS)
        zhat = zc * rstd
        err = zhat * lg_ref[...] + lb_ref[...] - t_ref[...]
        dout = err * (1.0 / d)
        dzhat = dout * lg_ref[...]
        dz = rstd * (dzhat - jnp.mean(dzhat, axis=-1, keepdims=True) - zhat * jnp.mean(dzhat * zhat, axis=-1, keepdims=True))
        dza_ref[...] = ALPHA * dz
        dsub_ref[...] = (gate * dz).astype(BF16)
        part = jnp.concatenate(
            [
                jnp.sum(dz * sub, axis=0, keepdims=True),
                jnp.sum(dout * zhat, axis=0, keepdims=True),
                jnp.sum(dout, axis=0, keepdims=True),
                jnp.sum(err * err, axis=0, keepdims=True),
                jnp.zeros((4, d), F32),
            ],
            axis=0,
        )

        @pl.when(i == 0)
        def _():
            red_ref[...] = part

        @pl.when(i > 0)
        def _():
            red_ref[...] += part

        @pl.when(i == n_steps - 1)
        def _():
            red_ref[4:5, :] = jnp.broadcast_to(jnp.sum(red_ref[3:4, :], axis=1, keepdims=True), (1, d))

    row = pl.BlockSpec((tm, d), lambda i: (i, 0))
    vec = pl.BlockSpec((1, d), lambda i: (0, 0))
    return _pcall(
        body,
        name="out_stage",
        grid=(n_steps,),
        out_shape=[
            jax.ShapeDtypeStruct((s_len, d), BF16),
            jax.ShapeDtypeStruct((s_len, d), F32),
            jax.ShapeDtypeStruct((s_len, d), BF16),
            jax.ShapeDtypeStruct((8, d), F32),
        ],
        in_specs=[
            pl.BlockSpec((tm, d), lambda i: (i, mf_blk)),
            pl.BlockSpec((tm, d), lambda i: (i, mf_blk + 1)),
            row,
            row,
            pl.BlockSpec((d, d), lambda i: (0, 0), pipeline_mode=pl.Buffered(1)),
            row,
            pl.BlockSpec((1, d), lambda i: (0, 2)),
            vec,
            vec,
            row,
        ],
        out_specs=[row, row, row, pl.BlockSpec((8, d), lambda i: (0, 0))],
        compiler_params=_cp("arbitrary"),
    )(gates, gates, y_fox, y_swa, w_out, x, ada, ln_g, ln_b, target)


def _merge_bwd(dsub, w_out, gates, mf_blk, y_fox, y_swa):
    s_len, d = dsub.shape
    tm = min(256, s_len)

    def body(ds_ref, w_ref, mf_ref, ms_ref, yf_ref, ys_ref, dmf_ref, dms_ref, dyf_ref, dys_ref):
        dm = lax.dot_general(ds_ref[...], w_ref[...], _NT, preferred_element_type=F32)
        sf, ss = _sigmoid(mf_ref[...].astype(F32)), _sigmoid(ms_ref[...].astype(F32))
        dmf_ref[...] = (dm * yf_ref[...].astype(F32) * (sf * (1.0 - sf))).astype(BF16)
        dms_ref[...] = (dm * ys_ref[...].astype(F32) * (ss * (1.0 - ss))).astype(BF16)
        dyf_ref[...] = (dm * sf).astype(BF16)
        dys_ref[...] = (dm * ss).astype(BF16)

    row = pl.BlockSpec((tm, d), lambda i: (i, 0))
    return _pcall(
        body,
        name="merge_bwd",
        grid=(s_len // tm,),
        out_shape=[jax.ShapeDtypeStruct((s_len, d), BF16)] * 4,
        in_specs=[
            row,
            pl.BlockSpec((d, d), lambda i: (0, 0), pipeline_mode=pl.Buffered(1)),
            pl.BlockSpec((tm, d), lambda i: (i, mf_blk)),
            pl.BlockSpec((tm, d), lambda i: (i, mf_blk + 1)),
            row,
            row,
        ],
        out_specs=[row] * 4,
        compiler_params=_cp("parallel"),
    )(dsub, w_out, gates, gates, y_fox, y_swa)


def _branch_bwd(dy, w_b, o, gates, g_blk, name, n_heads):
    s_len, d = dy.shape
    wd = w_b.shape[0]
    tm = min(512, s_len)

    def body(dy_ref, w_ref, o_ref, g_ref, do_ref, dg_ref, *rest):
        da = lax.dot_general(dy_ref[...], w_ref[...], _NT, preferred_element_type=F32)
        g = g_ref[...].astype(F32)
        sg = _sigmoid(g)
        do = da * (g * sg)
        do_ref[...] = do.astype(BF16)
        o = o_ref[...]
        dg_ref[...] = (da * o * (sg * (1.0 + g * (1.0 - sg)))).astype(BF16)
        if n_heads:
            prod = do.astype(BF16).astype(F32) * o
            lane = lax.broadcasted_iota(jnp.int32, (1, 128), 1)
            delta = jnp.zeros((tm, 128), F32)
            for h in range(n_heads):
                dh = jnp.sum(prod[:, h * 128 : (h + 1) * 128], axis=1, keepdims=True)
                delta = delta + jnp.where(lane == h, dh, 0.0)
            rest[0][...] = delta

    out_shape = [jax.ShapeDtypeStruct((s_len, wd), BF16), jax.ShapeDtypeStruct((s_len, wd), BF16)]
    out_specs = [pl.BlockSpec((tm, wd), lambda i: (i, 0))] * 2
    if n_heads:
        out_shape.append(jax.ShapeDtypeStruct((s_len, 128), F32))
        out_specs.append(pl.BlockSpec((tm, 128), lambda i: (i, 0)))
    return _pcall(
        body,
        name=name,
        grid=(s_len // tm,),
        out_shape=out_shape,
        in_specs=[
            pl.BlockSpec((tm, d), lambda i: (i, 0)),
            pl.BlockSpec((wd, d), lambda i: (0, 0)),
            pl.BlockSpec((tm, wd), lambda i: (i, 0)),
            pl.BlockSpec((tm, wd), lambda i: (i, g_blk)),
        ],
        out_specs=out_specs,
        compiler_params=_cp("parallel"),
    )(dy, w_b, o, gates)


def _in_bwd(dproj, w_in_t, x, ada, dza, ride):
    s_len, d = x.shape
    k_tot = dproj.shape[1]
    tm, tk, dn = min(512, s_len), k_tot // 4, d // 2
    ni, nk = s_len // tm, k_tot // tk
    n = len(ride)

    def body(dp_ref, w_ref, x_ref, sc_ref, dza_ref, *rest):
        ins, (gx_ref, red_ref), outs = rest[:n], rest[n : n + 2], rest[n + 2 : 2 * n + 2]
        sems, acc_s = rest[2 * n + 2 : 2 * n + 5], rest[2 * n + 5]
        i, nh, kk = pl.program_id(0), pl.program_id(1), pl.program_id(2)

        @pl.when((i == 0) & (nh == 0) & (kk == 0))
        def _():
            _rider_start("exchange", ins, outs, *sems)

        @pl.when((i == ni - 1) & (nh == 1) & (kk == nk - 1))
        def _():
            _rider_wait("exchange", ins, outs, *sems)

        part = jnp.dot(dp_ref[...], w_ref[...], preferred_element_type=F32)
        half = pl.ds(pl.multiple_of(nh * dn, dn), dn)

        @pl.when(kk == 0)
        def _():
            acc_s[:, half] = part

        @pl.when(kk > 0)
        def _():
            acc_s[:, half] += part

        @pl.when((nh == 1) & (kk == nk - 1))
        def _():
            dh = acc_s[...]
            xv = x_ref[...]
            mu = jnp.mean(xv, axis=-1, keepdims=True)
            xc = xv - mu
            var = jnp.mean(xc * xc, axis=-1, keepdims=True)
            rstd = lax.rsqrt(var + LN_EPS)
            xhat = xc * rstd
            dxhat = dh * (1.0 + sc_ref[...])
            dx = rstd * (dxhat - jnp.mean(dxhat, axis=-1, keepdims=True) - xhat * jnp.mean(dxhat * xhat, axis=-1, keepdims=True))
            gx_ref[...] = dza_ref[...] + dx
            part_r = jnp.concatenate(
                [jnp.sum(dh, axis=0, keepdims=True), jnp.sum(dh * xhat, axis=0, keepdims=True), jnp.zeros((6, d), F32)], axis=0
            )

            @pl.when(i == 0)
            def _():
                red_ref[...] = part_r

            @pl.when(i > 0)
            def _():
                red_ref[...] += part_r

    row = pl.BlockSpec((tm, d), lambda i, nh, kk: (i, 0))
    hbm = pl.BlockSpec(memory_space=pltpu.HBM)
    return _pcall(
        body,
        name="in_bwd",
        grid=(ni, 2, nk),
        out_shape=[jax.ShapeDtypeStruct((s_len, d), F32), jax.ShapeDtypeStruct((8, d), F32)]
        + [jax.ShapeDtypeStruct(r.shape, r.dtype) for r in ride],
        in_specs=[
            pl.BlockSpec((tm, tk), lambda i, nh, kk: (i, kk)),
            pl.BlockSpec((tk, dn), lambda i, nh, kk: (kk, nh)),
            row,
            pl.BlockSpec((1, d), lambda i, nh, kk: (0, 1)),
            row,
        ]
        + [hbm] * n,
        out_specs=[row, pl.BlockSpec((8, d), lambda i, nh, kk: (0, 0))] + [hbm] * n,
        scratch_shapes=_rider_scratch(n) + [pltpu.VMEM((tm, d), F32)],
        compiler_params=_cp("arbitrary", "arbitrary", "arbitrary"),
    )(dproj, w_in_t, x, ada, dza, *ride)


def _pad_lanes(v, n):
    return jnp.pad(v, ((0, 0), (0, n - v.shape[1])))


def kernel(x, c, w_ada, b_ada, w_in, b_f, attn_sinks, w_br_fox, w_br_swa, w_out, ln_g, ln_b, loss_target, m_w_ada, m_b_ada, m_w_in, m_b_f, m_attn_sinks, m_w_br_fox, m_w_br_swa, m_w_out, m_ln_g, m_ln_b, v_w_ada, v_b_ada, v_w_in, v_b_f, v_attn_sinks, v_w_br_fox, v_w_br_swa, v_w_out, v_ln_g, v_ln_b):
    x2, tgt = x[0], loss_target[0]
    s_len, d = x2.shape
    me = 4 * lax.axis_index("x") + 2 * lax.axis_index("y") + lax.axis_index("c")
    off_ms = OFF_MF + d
    in_pad = off_ms + d
    c_ada = w_ada.shape[2]
    c_in = w_in.shape[2]
    c_br = w_br_fox.shape[2]

    w_in_full = _all_gather(w_in[0].T.astype(BF16), "ag_w_in", pltpu.HBM).reshape(N_DEV * c_in, d)
    w_in_pad = jnp.concatenate(
        [w_in_full[:REAL_FLOG_END], jnp.zeros((FLOG_PAD - N_FLOG, d), BF16), w_in_full[REAL_FLOG_END:]], axis=0
    )
    k_cut = REAL_FLOG_END // c_in

    c_all = _gather_rows(c, "ag_c")
    b_cols = lax.dynamic_slice(b_ada, (0, me * c_ada), (1, c_ada))
    ada_cols = _ada_fwd(c_all, w_ada[0], b_cols)
    ada_g = _all_gather(ada_cols, "ag_ada", pltpu.VMEM)
    ada = lax.dynamic_index_in_dim(ada_g, me, axis=1, keepdims=False).reshape(1, N_DEV * c_ada)

    h = _ln_mod(x2, ada)
    qkv_fox = _mm_cols(h, w_in_pad, OFF_FQ, 3 * FOX_W, BF16, "proj_fox")
    flog = _mm_cols(h, w_in_pad, OFF_FLOG, FLOG_PAD, F32, "proj_flog")
    qkv_swa = _mm_cols(h, w_in_pad, OFF_SQ, SWA_W + 2 * SWA_KVW, BF16, "proj_swa")
    gates, w_bf, w_bs, w_o = _mm_cols(
        h, w_in_pad, OFF_GF, in_pad - OFF_GF, BF16, "proj_gates",
        ride=(w_br_fox[0].astype(BF16), w_br_swa[0].astype(BF16), w_out[0].astype(BF16)),
    )
    w_bf = w_bf.reshape(N_DEV, FOX_W, c_br).transpose(1, 0, 2).reshape(FOX_W, d)
    w_bs = w_bs.reshape(N_DEV, SWA_W, c_br).transpose(1, 0, 2).reshape(SWA_W, d)
    w_o = w_o.reshape(d, d)
    mf_blk = (OFF_MF - OFF_GF) // d

    flog_t = flog[:, :N_FLOG].T
    bf_col = b_f.reshape(FOX_H, 1)
    cum = _fox_cum(flog_t, bf_col)
    cum_row = cum.reshape(FOX_H, 1, s_len)
    o_fox, lse = _fox_fwd(qkv_fox, cum_row)
    sinks = attn_sinks.reshape(SWA_HQ)
    o_swa = _swa_fwd(qkv_swa, sinks)

    y_fox, a_fox = _branch_fwd(o_fox, gates, 0, w_bf, "branch_fox")
    y_swa, a_swa = _branch_fwd(o_swa, gates, 1, w_bs, "branch_swa")
    merged, dza, dsub, red = _out_stage(gates, mf_blk, y_fox, y_swa, w_o, x2, ada, ln_g, ln_b, tgt)
    loss = lax.psum(0.5 * red[4, 0] / d, ("x", "y", "c"))

    dmf, dms, dy_fox, dy_swa = _merge_bwd(dsub, w_o, gates, mf_blk, y_fox, y_swa)
    do_fox, dg_fox, delta = _branch_bwd(dy_fox, w_bf, o_fox, gates, 0, "branch_fox_bwd", FOX_H)
    do_swa, dg_swa = _branch_bwd(dy_swa, w_bs, o_swa, gates, 1, "branch_swa_bwd", 0)
    delta_row = delta[:, :FOX_H].T.reshape(FOX_H, 1, s_len)
    dq_f, dk_f, dv_f, dcol, drow = _fox_bwd(
        qkv_fox, cum.reshape(FOX_H, s_len, 1), lse.reshape(FOX_H, 1, s_len), delta_row, do_fox
    )
    dflog_t, dbf = _fox_gate_bwd(drow.reshape(FOX_H, s_len), dcol.reshape(FOX_H, s_len), flog_t, bf_col)
    dq_s, dk_s, dv_s, dsink = _swa_bwd(qkv_swa, sinks, do_swa)
    dflog = _pad_lanes(dflog_t.T, FLOG_PAD).astype(BF16)
    dproj = jnp.concatenate([dq_f, dk_f, dv_f, dflog, dq_s, dk_s, dv_s, dg_fox, dg_swa, dmf, dms], axis=1)
    g_w_bf = _mm_tn(a_fox, dy_fox, "grad_w_br_fox")
    g_w_bs = _mm_tn(a_swa, dy_swa, "grad_w_br_swa")
    g_w_o = _mm_tn(merged, dsub, "grad_w_out")
    g_w_in, r_bf, r_bs, r_o = _mm_tn(
        dproj, h, "grad_w_in",
        ride=(
            g_w_bf.reshape(FOX_W, N_DEV, c_br).transpose(1, 0, 2),
            g_w_bs.reshape(SWA_W, N_DEV, c_br).transpose(1, 0, 2),
            g_w_o.reshape(N_DEV, d // N_DEV, d),
        ),
    )
    pad = FLOG_PAD - N_FLOG
    g_blocks = jnp.stack(
        [g_w_in[k * c_in : (k + 1) * c_in] for k in range(k_cut)]
        + [jnp.concatenate([g_w_in[k_cut * c_in : REAL_FLOG_END], g_w_in[OFF_SQ : (k_cut + 1) * c_in + pad]], axis=0)]
        + [g_w_in[k * c_in + pad : (k + 1) * c_in + pad] for k in range(k_cut + 1, N_DEV)]
    )

    grad_x, red2, r_in = _in_bwd(dproj, w_in_pad, x2, ada, dza, ride=(g_blocks,))
    out_w_in = _sum_adam_t(r_in, w_in[0].T, m_w_in[0].T, v_w_in[0].T, "adam_w_in")
    out_w_in = [o.T for o in out_w_in]
    out_w_bf = _sum_adam(r_bf, w_br_fox[0], m_w_br_fox[0], v_w_br_fox[0], "adam_w_br_fox")
    out_w_bs = _sum_adam(r_bs, w_br_swa[0], m_w_br_swa[0], v_w_br_swa[0], "adam_w_br_swa")
    out_w_o = _sum_adam(r_o, w_out[0], m_w_out[0], v_w_out[0], "adam_w_out")

    packed = jnp.concatenate([red2[0:1], red2[1:2], red[0:1], _pad_lanes(dbf[:, 0].reshape(1, FOX_H), 128), dsink, red[1:2], red[2:3]], axis=1)
    gathered = _gather_rows(packed, "ag_small")
    pack = lambda a, b, cc, dd, e: jnp.concatenate([a, _pad_lanes(b, 128), _pad_lanes(cc, 128), dd, e], axis=1)
    small = _small_adam(
        gathered,
        pack(b_ada, b_f, attn_sinks, ln_g, ln_b),
        pack(m_b_ada, m_b_f, m_attn_sinks, m_ln_g, m_ln_b),
        pack(v_b_ada, v_b_f, v_attn_sinks, v_ln_g, v_ln_b),
    )
    dada_cols = lax.dynamic_slice(gathered, (0, me * c_ada), (N_DEV, c_ada))
    out_w_ada = _wada_adam(c_all.T, dada_cols, w_ada[0], m_w_ada[0], v_w_ada[0])

    o1, o2, o3 = 3 * d, 3 * d + 128, 3 * d + 256

    def unpack(p):
        return p[:, :o1], p[:, o1 : o1 + FOX_H], p[:, o2 : o2 + SWA_HQ], p[:, o3 : o3 + d], p[:, o3 + d : o3 + 2 * d]

    kinds = []
    for k in range(4):
        b_ada_k, b_f_k, sinks_k, ln_g_k, ln_b_k = unpack(small[k])
        kinds.append(
            [out_w_ada[k][None], b_ada_k, out_w_in[k][None], b_f_k, sinks_k, out_w_bf[k][None], out_w_bs[k][None], out_w_o[k][None], ln_g_k, ln_b_k]
        )
    return (loss, grad_x[None], *kinds[0], *kinds[1], *kinds[2], *kinds[3])
```

```python
import numpy as np
import jax
import jax.numpy as jnp
from jax import lax
from jax.experimental import pallas as pl
from jax.experimental.pallas import tpu as pltpu

F32 = jnp.float32
BF16 = jnp.bfloat16
N_DEV = 8
MESH = pl.DeviceIdType.MESH

FOX_H, FOX_DH, FOX_W = 8, 128, 1024
SWA_HQ, SWA_HKV, SWA_DH, SWA_G = 16, 4, 64, 4
SWA_W, SWA_KVW, WINDOW = 1024, 256, 128
LN_EPS = 1e-5
NEG = -1e30
DEPTH = 1
ALPHA = (2.0 * DEPTH) ** 0.25
FOX_SCALE = FOX_DH ** -0.5
SWA_SCALE = SWA_DH ** -0.5
SLOPES = [2.0 ** (-8.0 * (h + 1.0) / SWA_HQ) for h in range(SWA_HQ)]

ADAM_LR, ADAM_B1, ADAM_B2, ADAM_EPS, ADAM_WD, ADAM_STEP = 0.001, 0.9, 0.999, 1e-08, 0.01, 10

N_FLOG = 8
FLOG_PAD = 512
OFF_FQ, OFF_FK, OFF_FV, OFF_FLOG = 0, 1024, 2048, 3072
OFF_SQ = OFF_FLOG + FLOG_PAD
OFF_SK = OFF_SQ + SWA_W
OFF_SV = OFF_SK + SWA_KVW
OFF_GF = OFF_SV + SWA_KVW
OFF_GS = OFF_GF + FOX_W
OFF_MF = OFF_GS + SWA_W
REAL_FLOG_END = OFF_FLOG + N_FLOG

ATT_BLK = 512
VMEM_LIMIT = 58 * 1024 * 1024


def _pcall(body, **kw):
    return pl.pallas_call(body, **kw)


def _cp(*sem):
    return pltpu.CompilerParams(dimension_semantics=sem, vmem_limit_bytes=VMEM_LIMIT)


def _sigmoid(x):
    return 0.5 * jnp.tanh(0.5 * x) + 0.5


def _all_gather(x, name, space):
    m_per, n = x.shape

    def body(x_ref, out_ref, send_sems, recv_sems, local_sem):
        mx, my, mc = lax.axis_index("x"), lax.axis_index("y"), lax.axis_index("c")
        me, sibling = (mx, my, mc), (mx, my, 1 - mc)
        xn, yn, dg = (1 - mx, my), (mx, 1 - my), (1 - mx, 1 - my)
        south = mc == 0
        src_chip = (jnp.where(south, 1 - mx, mx), jnp.where(south, my, 1 - my))
        dst_chip = (jnp.where(south, mx, 1 - mx), jnp.where(south, 1 - my, my))

        def rows(px, py, pc):
            return out_ref.at[4 * px + 2 * py + pc]

        def copy(k, block, to, src=None):
            return pltpu.make_async_remote_copy(
                src_ref=rows(*block) if src is None else src,
                dst_ref=rows(*block),
                send_sem=send_sems.at[k],
                recv_sem=recv_sems.at[k],
                device_id=to,
                device_id_type=MESH,
            )

        mine = pltpu.make_async_copy(x_ref, rows(*me), local_sem)
        mine.start()
        first = [copy(0, me, sibling, src=x_ref), copy(1, me, (*xn, mc), src=x_ref), copy(2, me, (*yn, mc), src=x_ref)]
        for cp in first:
            cp.start()
        copy(1, (*xn, mc), me).wait_recv()
        copy(2, (*yn, mc), me).wait_recv()
        later = [copy(3, (*src_chip, mc), (*dst_chip, mc)), copy(4, (*xn, mc), sibling), copy(5, (*yn, mc), sibling)]
        for cp in later:
            cp.start()
        copy(3, (*dg, mc), me).wait_recv()
        last = copy(6, (*dg, mc), sibling)
        last.start()
        copy(0, sibling, me).wait_recv()
        for k, chip in ((4, xn), (5, yn), (6, dg)):
            copy(k, (*chip, 1 - mc), me).wait_recv()
        for cp in first + later + [last]:
            cp.wait_send()
        mine.wait()

    return _pcall(
        body,
        name=name,
        out_shape=jax.ShapeDtypeStruct((N_DEV, m_per, n), x.dtype),
        in_specs=[pl.BlockSpec(memory_space=space)],
        out_specs=pl.BlockSpec(memory_space=space),
        scratch_shapes=[pltpu.SemaphoreType.DMA((7,)), pltpu.SemaphoreType.DMA((7,)), pltpu.SemaphoreType.DMA],
    )(x)


def _peer(d, mx, my, mc):
    return (1 - mx if (d >> 2) & 1 else mx, 1 - my if (d >> 1) & 1 else my, 1 - mc if d & 1 else mc)


def _rider_copies(kind, ins, outs, send_sems, recv_sems, local_sems):
    mx, my, mc = lax.axis_index("x"), lax.axis_index("y"), lax.axis_index("c")
    me = 4 * mx + 2 * my + mc
    remote, local = [], []
    for a in range(len(ins)):
        if kind == "gather":
            m_per = ins[a].shape[0]
            mine = outs[a].at[pl.ds(me * m_per, m_per), :]
            local.append(pltpu.make_async_copy(ins[a], mine, local_sems.at[a]))
        else:
            local.append(pltpu.make_async_copy(ins[a].at[me], outs[a].at[0], local_sems.at[a]))
        for d in range(1, N_DEV):
            px, py, pc = _peer(d, mx, my, mc)
            if kind == "gather":
                src, dst = ins[a], mine
            else:
                src, dst = ins[a].at[4 * px + 2 * py + pc], outs[a].at[d]
            remote.append(
                pltpu.make_async_remote_copy(
                    src_ref=src,
                    dst_ref=dst,
                    send_sem=send_sems.at[a * 7 + d - 1],
                    recv_sem=recv_sems.at[a * 7 + d - 1],
                    device_id=(px, py, pc),
                    device_id_type=MESH,
                )
            )
    return remote, local


def _rider_start(*args):
    remote, local = _rider_copies(*args)
    for cp in local + remote:
        cp.start()


def _rider_wait(*args):
    remote, local = _rider_copies(*args)
    for cp in remote:
        cp.wait_recv()
    for cp in remote:
        cp.wait_send()
    for cp in local:
        cp.wait()


def _rider_scratch(n):
    return [pltpu.SemaphoreType.DMA((7 * n,)), pltpu.SemaphoreType.DMA((7 * n,)), pltpu.SemaphoreType.DMA((n,))]


def _gather_rows(v, name):
    n = v.shape[1]
    return _all_gather(jnp.broadcast_to(v, (8, n)), name, pltpu.VMEM)[:, 0, :]


def _adamw(w, g, m, v):
    m = ADAM_B1 * m + (1.0 - ADAM_B1) * g
    v = ADAM_B2 * v + (1.0 - ADAM_B2) * (g * g)
    m_hat = m / (1.0 - ADAM_B1**ADAM_STEP)
    v_hat = v / (1.0 - ADAM_B2**ADAM_STEP)
    delta = -ADAM_LR * (m_hat / (jnp.sqrt(v_hat) + ADAM_EPS) + ADAM_WD * w)
    return delta, m, v


def _sum_adam(recv, w, m, v, name):
    _, r_tot, c = recv.shape
    c_pad = -(-c // 128) * 128
    tr = r_tot
    while 8 * tr * c_pad * 4 > 6 * 1024 * 1024 and tr % 32 == 0:
        tr //= 2

    def body(r_ref, w_ref, m_ref, v_ref, g_ref, d_ref, nm_ref, nv_ref):
        g = r_ref[0].astype(F32)
        for k in range(1, N_DEV):
            g = g + r_ref[k].astype(F32)
        d, nm, nv = _adamw(w_ref[...], g, m_ref[...], v_ref[...])
        g_ref[...] = g
        d_ref[...] = d
        nm_ref[...] = nm
        nv_ref[...] = nv

    blk = pl.BlockSpec((tr, c), lambda i: (i, 0))
    return _pcall(
        body,
        name=name,
        grid=(r_tot // tr,),
        out_shape=[jax.ShapeDtypeStruct((r_tot, c), F32)] * 4,
        in_specs=[pl.BlockSpec((N_DEV, tr, c), lambda i: (0, i, 0)), blk, blk, blk],
        out_specs=[blk] * 4,
        compiler_params=_cp("parallel"),
    )(recv, w, m, v)


def _sum_adam_t(recv, w, m, v, name):
    _, c, r_tot = recv.shape
    tr = min(256, r_tot)

    def body(r_ref, w_ref, m_ref, v_ref, g_ref, d_ref, nm_ref, nv_ref):
        g = r_ref[0].astype(F32)
        for k in range(1, N_DEV):
            g = g + r_ref[k].astype(F32)
        d, nm, nv = _adamw(w_ref[...], g, m_ref[...], v_ref[...])
        g_ref[...] = g
        d_ref[...] = d
        nm_ref[...] = nm
        nv_ref[...] = nv

    blk = pl.BlockSpec((c, tr), lambda i: (0, i))
    return _pcall(
        body,
        name=name,
        grid=(r_tot // tr,),
        out_shape=[jax.ShapeDtypeStruct((c, r_tot), F32)] * 4,
        in_specs=[pl.BlockSpec((N_DEV, c, tr), lambda i: (0, 0, i)), blk, blk, blk],
        out_specs=[blk] * 4,
        compiler_params=_cp("parallel"),
    )(recv, w, m, v)


def _wada_adam(c_t, dada_cols, w, m, v):
    d_model, c = w.shape
    tr = min(256, d_model)

    def body(ct_ref, da_ref, w_ref, m_ref, v_ref, g_ref, d_ref, nm_ref, nv_ref):
        g = jnp.dot(ct_ref[...].astype(BF16), da_ref[...].astype(BF16), preferred_element_type=F32)
        d, nm, nv = _adamw(w_ref[...], g, m_ref[...], v_ref[...])
        g_ref[...] = g
        d_ref[...] = d
        nm_ref[...] = nm
        nv_ref[...] = nv

    blk = pl.BlockSpec((tr, c), lambda i: (i, 0))
    return _pcall(
        body,
        name="wada_adam",
        grid=(d_model // tr,),
        out_shape=[jax.ShapeDtypeStruct((d_model, c), F32)] * 4,
        in_specs=[pl.BlockSpec((tr, N_DEV), lambda i: (i, 0)), pl.BlockSpec((N_DEV, c), lambda i: (0, 0)), blk, blk, blk],
        out_specs=[blk] * 4,
        compiler_params=_cp("parallel"),
    )(c_t, dada_cols, w, m, v)


def _small_adam(gathered, w, m, v):
    p = w.shape[1]

    def body(a_ref, w_ref, m_ref, v_ref, g_ref, d_ref, nm_ref, nv_ref):
        g = a_ref[0:1, :]
        for k in range(1, N_DEV):
            g = g + a_ref[k : k + 1, :]
        d, nm, nv = _adamw(w_ref[...], g, m_ref[...], v_ref[...])
        g_ref[...] = g
        d_ref[...] = d
        nm_ref[...] = nm
        nv_ref[...] = nv

    return _pcall(
        body,
        name="small_adam",
        out_shape=[jax.ShapeDtypeStruct((1, p), F32)] * 4,
    )(gathered, w, m, v)


def _ada_fwd(c_all, w_ada, b_cols):
    c = w_ada.shape[1]

    def body(c_ref, w_ref, b_ref, o_ref):
        o_ref[...] = jnp.dot(c_ref[...].astype(BF16), w_ref[...].astype(BF16), preferred_element_type=F32) + b_ref[...]

    return _pcall(
        body,
        name="ada_fwd",
        out_shape=jax.ShapeDtypeStruct((N_DEV, c), F32),
        compiler_params=_cp(),
    )(c_all, w_ada, b_cols)


def _ln_mod(x, ada):
    s_len, d = x.shape
    tm = min(512, s_len)

    def body(x_ref, sh_ref, sc_ref, h_ref):
        xv = x_ref[...]
        mu = jnp.mean(xv, axis=-1, keepdims=True)
        xc = xv - mu
        var = jnp.mean(xc * xc, axis=-1, keepdims=True)
        xhat = xc * lax.rsqrt(var + LN_EPS)
        h_ref[...] = (xhat * (1.0 + sc_ref[...]) + sh_ref[...]).astype(BF16)

    return _pcall(
        body,
        name="ln_mod",
        grid=(s_len // tm,),
        out_shape=jax.ShapeDtypeStruct((s_len, d), BF16),
        in_specs=[
            pl.BlockSpec((tm, d), lambda i: (i, 0)),
            pl.BlockSpec((1, d), lambda i: (0, 0)),
            pl.BlockSpec((1, d), lambda i: (0, 1)),
        ],
        out_specs=pl.BlockSpec((tm, d), lambda i: (i, 0)),
        compiler_params=_cp("parallel"),
    )(x, ada, ada)


def _mm_cols(a, b, col_off, n_cols, out_dtype, name, ride=()):
    m, k = a.shape
    tm, tn = min(1024, m), min(512, n_cols)
    off = col_off // tn
    ni, nj = m // tm, n_cols // tn
    n = len(ride)

    def body(a_ref, b_ref, *rest):
        ins, o_ref, outs, sems = rest[:n], rest[n], rest[n + 1 : 2 * n + 1], rest[2 * n + 1 :]
        i, j = pl.program_id(0), pl.program_id(1)
        if n:

            @pl.when((i == 0) & (j == 0))
            def _():
                _rider_start("gather", ins, outs, *sems)

        o_ref[...] = lax.dot_general(a_ref[...], b_ref[...], _NT, preferred_element_type=F32).astype(out_dtype)
        if n:

            @pl.when((i == ni - 1) & (j == nj - 1))
            def _():
                _rider_wait("gather", ins, outs, *sems)

    hbm = pl.BlockSpec(memory_space=pltpu.HBM)
    out = _pcall(
        body,
        name=name,
        grid=(ni, nj),
        out_shape=[jax.ShapeDtypeStruct((m, n_cols), out_dtype)]
        + [jax.ShapeDtypeStruct((N_DEV * r.shape[0], r.shape[1]), r.dtype) for r in ride],
        in_specs=[pl.BlockSpec((tm, k), lambda i, j: (i, 0)), pl.BlockSpec((tn, k), lambda i, j: (off + j, 0))] + [hbm] * n,
        out_specs=[pl.BlockSpec((tm, tn), lambda i, j: (i, j))] + [hbm] * n,
        scratch_shapes=_rider_scratch(n) if n else [],
        compiler_params=_cp("arbitrary", "arbitrary") if n else _cp("parallel", "parallel"),
    )(a, b, *ride)
    return out if n else out[0]


def _mm_tn(a, b, name, ride=()):
    s_len, m = a.shape
    n = b.shape[1]
    tm, tn, ts = min(1024, m), min(1024, n), min(2048, s_len)
    ni, nj, ns = m // tm, n // tn, s_len // ts
    nr = len(ride)

    def body(a_ref, b_ref, *rest):
        ins, o_ref, outs = rest[:nr], rest[nr], rest[nr + 1 : 2 * nr + 1]
        sems, acc_s = rest[2 * nr + 1 : -1], rest[-1]
        i, j, kk = pl.program_id(0), pl.program_id(1), pl.program_id(2)
        if nr:

            @pl.when((i == 0) & (j == 0) & (kk == 0))
            def _():
                _rider_start("exchange", ins, outs, *sems)

            @pl.when((i == ni - 1) & (j == nj - 1) & (kk == ns - 1))
            def _():
                _rider_wait("exchange", ins, outs, *sems)

        part = lax.dot_general(a_ref[...], b_ref[...], _TN, preferred_element_type=F32)

        @pl.when(kk == 0)
        def _():
            acc_s[...] = part

        @pl.when(kk > 0)
        def _():
            acc_s[...] += part

        @pl.when(kk == ns - 1)
        def _():
            o_ref[...] = acc_s[...].astype(BF16)

    hbm = pl.BlockSpec(memory_space=pltpu.HBM)
    out = _pcall(
        body,
        name=name,
        grid=(ni, nj, ns),
        out_shape=[jax.ShapeDtypeStruct((m, n), BF16)] + [jax.ShapeDtypeStruct(r.shape, r.dtype) for r in ride],
        in_specs=[pl.BlockSpec((ts, tm), lambda i, j, kk: (kk, i)), pl.BlockSpec((ts, tn), lambda i, j, kk: (kk, j))] + [hbm] * nr,
        out_specs=[pl.BlockSpec((tm, tn), lambda i, j, kk: (i, j))] + [hbm] * nr,
        scratch_shapes=(_rider_scratch(nr) if nr else []) + [pltpu.VMEM((tm, tn), F32)],
        compiler_params=_cp("arbitrary", "arbitrary", "arbitrary") if nr else _cp("parallel", "parallel", "arbitrary"),
    )(a, b, *ride)
    return out if nr else out[0]


def _split3(a):
    hi = a.astype(BF16)
    r1 = a - hi.astype(F32)
    mid = r1.astype(BF16)
    lo = (r1 - mid.astype(F32)).astype(BF16)
    return hi, mid, lo


def _dot_ones(a, tri):
    return sum(jnp.dot(t, tri, preferred_element_type=F32) for t in _split3(a))


def _log_sigmoid(x):
    return jnp.minimum(x, 0.0) - jnp.log1p(jnp.exp(-jnp.abs(x)))


def _fox_cum(flog_t, bf_col):
    s_len = flog_t.shape[1]

    def body(fl_ref, bf_ref, cum_ref):
        r = lax.broadcasted_iota(jnp.int32, (128, 128), 0)
        c = lax.broadcasted_iota(jnp.int32, (128, 128), 1)
        upper = (r <= c).astype(BF16)

        def step(t, carry):
            sl = pl.ds(pl.multiple_of(t * 128, 128), 128)
            lf = _log_sigmoid(fl_ref[:, sl] + bf_ref[...])
            cs = _dot_ones(lf, upper) + carry
            cum_ref[:, sl] = cs
            return cs[:, 127:128]

        lax.fori_loop(0, s_len // 128, step, jnp.zeros((FOX_H, 1), F32))

    return _pcall(body, name="fox_cum", out_shape=jax.ShapeDtypeStruct((FOX_H, s_len), F32))(flog_t, bf_col)


def _fox_gate_bwd(drow, dcol, flog_t, bf_col):
    s_len = flog_t.shape[1]
    n = s_len // 128

    def body(dr_ref, dc_ref, fl_ref, bf_ref, dfl_ref, dbf_ref):
        r = lax.broadcasted_iota(jnp.int32, (128, 128), 0)
        c = lax.broadcasted_iota(jnp.int32, (128, 128), 1)
        lower = (r >= c).astype(BF16)

        def step(t, carry):
            run, tot = carry
            sl = pl.ds(pl.multiple_of((n - 1 - t) * 128, 128), 128)
            rc = _dot_ones(dr_ref[:, sl] - dc_ref[:, sl], lower) + run
            dfl = rc * _sigmoid(-(fl_ref[:, sl] + bf_ref[...]))
            dfl_ref[:, sl] = dfl
            return rc[:, 0:1], tot + jnp.sum(dfl, axis=1, keepdims=True)

        zero = jnp.zeros((FOX_H, 1), F32)
        _, tot = lax.fori_loop(0, n, step, (zero, zero))
        dbf_ref[...] = jnp.broadcast_to(tot, (FOX_H, 128))

    return _pcall(
        body,
        name="fox_gate_bwd",
        out_shape=[jax.ShapeDtypeStruct((FOX_H, s_len), F32), jax.ShapeDtypeStruct((FOX_H, 128), F32)],
    )(drow, dcol, flog_t, bf_col)


def _diag_mask(blk, transposed=False):
    r = lax.broadcasted_iota(jnp.int32, (blk, blk), 0)
    c = lax.broadcasted_iota(jnp.int32, (blk, blk), 1)
    return c >= r if transposed else r >= c


_NT = (((1,), (1,)), ((), ()))
_TN = (((0,), (0,)), ((), ()))


def _fox_fwd(qkv, cum_row):
    s_len = qkv.shape[0]
    blk = min(ATT_BLK, s_len)
    nb = s_len // blk
    log2e = 1.4426950408889634

    def body(q_ref, k_ref, v_ref, c_ref, o_ref, lse_ref, mx_s, acc_s, u_s):
        i = pl.program_id(1)

        def key_cols(j, n):
            return pl.ds(pl.multiple_of(j * blk, blk), n * blk)

        def walk(tile):
            def two_pairs(t, carry):
                tile(4 * t, 2, False)
                tile(4 * t + 2, 2, False)
                return carry

            lax.fori_loop(0, i // 4, two_pairs, 0)

            @pl.when((i // 2) % 2 == 1)
            def _():
                tile(4 * (i // 4), 2, False)

            @pl.when(i % 2 == 1)
            def _():
                tile(i - 1, 1, False)

            tile(i, 1, True)

        def lane_max(j, n, masked):
            cols = key_cols(j, n)
            u = lax.dot_general(q_ref[...], k_ref[cols, :], _NT, preferred_element_type=F32) * (FOX_SCALE * log2e) - c_ref[:, cols] * log2e
            if masked:
                u = jnp.where(_diag_mask(blk), u, NEG)
            u_s[:, cols] = u
            part = u[:, 0:128]
            for t in range(1, n * blk // 128):
                part = jnp.maximum(part, u[:, t * 128 : (t + 1) * 128])
            mx_s[...] = jnp.maximum(mx_s[...], part)

        mx_s[...] = jnp.full(mx_s.shape, NEG, F32)
        walk(lane_max)
        m = jnp.max(mx_s[...], axis=1, keepdims=True)

        def weigh(j, n, masked):
            cols = key_cols(j, n)
            p = jnp.exp2(u_s[:, cols] - m)
            ones_col = (lax.broadcasted_iota(jnp.int32, (n * blk, 128), 1) == 0).astype(BF16)
            v1 = jnp.concatenate([v_ref[cols, :], ones_col], axis=1)
            acc_s[...] += jnp.dot(p.astype(BF16), v1, preferred_element_type=F32)

        acc_s[...] = jnp.zeros(acc_s.shape, F32)
        walk(weigh)
        l = acc_s[:, FOX_DH : FOX_DH + 1]
        o_ref[...] = acc_s[:, :FOX_DH] / l
        lse_ref[...] = m * (1.0 / log2e) + jnp.log(l)

    return _pcall(
        body,
        name="fox_fwd",
        grid=(FOX_H, nb),
        out_shape=[jax.ShapeDtypeStruct((s_len, FOX_W), F32), jax.ShapeDtypeStruct((FOX_H, s_len, 1), F32)],
        in_specs=[
            pl.BlockSpec((blk, FOX_DH), lambda h, i: (i, h)),
            pl.BlockSpec((s_len, FOX_DH), lambda h, i: (0, FOX_H + h)),
            pl.BlockSpec((s_len, FOX_DH), lambda h, i: (0, 2 * FOX_H + h)),
            pl.BlockSpec((None, 1, s_len), lambda h, i: (h, 0, 0)),
        ],
        out_specs=[
            pl.BlockSpec((blk, FOX_DH), lambda h, i: (i, h)),
            pl.BlockSpec((None, blk, 1), lambda h, i: (h, i, 0)),
        ],
        scratch_shapes=[pltpu.VMEM((blk, 128), F32), pltpu.VMEM((blk, 2 * FOX_DH), F32), pltpu.VMEM((blk, s_len), F32)],
        compiler_params=_cp("parallel", "arbitrary"),
    )(qkv, qkv, qkv, cum_row)


def _fox_bwd(qkv, cum_col, lse_row, delta_row, do):
    s_len = qkv.shape[0]
    blk = min(ATT_BLK, s_len)
    nb = s_len // blk

    def body(q_ref, k_ref, v_ref, c_ref, lse_ref, dl_ref, do_ref, dq_ref, dk_ref, dv_ref, dc_ref, dr_ref, dk_s, dv_s, dc_s, cb_s, dq_s):
        j = pl.program_id(1)

        @pl.when(j == 0)
        def _():
            dq_s[...] = jnp.zeros(dq_s.shape, F32)
            dr_ref[...] = jnp.zeros(dr_ref.shape, F32)

        dk_s[...] = jnp.zeros(dk_s.shape, F32)
        dv_s[...] = jnp.zeros(dv_s.shape, F32)
        dc_s[...] = jnp.zeros(dc_s.shape, F32)
        cb_s[...] = jnp.broadcast_to(c_ref[...], cb_s.shape)

        def tile(i, n, diag):
            rows = pl.ds(pl.multiple_of(i * blk, blk), n * blk)
            q, dob = q_ref[rows, :], do_ref[rows, :]
            k, v = k_ref[...], v_ref[...]
            s_t = lax.dot_general(k, q, _NT, preferred_element_type=F32) * FOX_SCALE - cb_s[:, : n * blk]
            p_t = jnp.exp(s_t - lse_ref[:, rows])
            if diag:
                p_t = jnp.where(_diag_mask(blk, transposed=True), p_t, 0.0)
            dp_t = lax.dot_general(v, dob, _NT, preferred_element_type=F32)
            ds_t = p_t * (dp_t - dl_ref[:, rows])
            dsb = ds_t.astype(BF16)
            dv_s[...] += jnp.dot(p_t.astype(BF16), dob, preferred_element_type=F32)
            dk_s[...] += jnp.dot(dsb, q, preferred_element_type=F32)
            dq_c = lax.dot_general(dsb, k, _TN, preferred_element_type=F32)
            part = ds_t[:, 0:128]
            for t in range(1, n * blk // 128):
                part = part + ds_t[:, t * 128 : (t + 1) * 128]
            dc_s[...] += part
            dr_ref[:, rows] += jnp.sum(ds_t, axis=0, keepdims=True)
            if diag:
                dq_s[rows, :] = (dq_s[rows, :] + dq_c) * FOX_SCALE
            else:
                dq_s[rows, :] += dq_c

        tile(j, 1, True)
        below = nb - 1 - j
        odd, odd_pair = below % 2, (below // 2) % 2

        @pl.when(odd == 1)
        def _():
            tile(j + 1, 1, False)

        @pl.when(odd_pair == 1)
        def _():
            tile(j + 1 + odd, 2, False)

        first = j + 1 + odd + 2 * odd_pair

        def two_pairs(t, carry):
            tile(first + 4 * t, 2, False)
            tile(first + 4 * t + 2, 2, False)
            return carry

        lax.fori_loop(0, below // 4, two_pairs, 0)
        dk_ref[...] = (dk_s[...] * FOX_SCALE).astype(BF16)
        dv_ref[...] = dv_s[...].astype(BF16)
        dc_ref[...] = jnp.sum(dc_s[...], axis=1, keepdims=True)

        @pl.when(j == nb - 1)
        def _():
            dq_ref[...] = dq_s[...].astype(BF16)

    head = lambda h, j: (0, h)
    row = pl.BlockSpec((None, 1, s_len), lambda h, j: (h, 0, 0))
    return _pcall(
        body,
        name="fox_bwd",
        grid=(FOX_H, nb),
        out_shape=[
            jax.ShapeDtypeStruct((s_len, FOX_W), BF16),
            jax.ShapeDtypeStruct((s_len, FOX_W), BF16),
            jax.ShapeDtypeStruct((s_len, FOX_W), BF16),
            jax.ShapeDtypeStruct((FOX_H, s_len, 1), F32),
            jax.ShapeDtypeStruct((FOX_H, 1, s_len), F32),
        ],
        in_specs=[
            pl.BlockSpec((s_len, FOX_DH), head),
            pl.BlockSpec((blk, FOX_DH), lambda h, j: (j, FOX_H + h)),
            pl.BlockSpec((blk, FOX_DH), lambda h, j: (j, 2 * FOX_H + h)),
            pl.BlockSpec((None, blk, 1), lambda h, j: (h, j, 0)),
            row,
            row,
            pl.BlockSpec((s_len, FOX_DH), head),
        ],
        out_specs=[
            pl.BlockSpec((s_len, FOX_DH), head),
            pl.BlockSpec((blk, FOX_DH), lambda h, j: (j, h)),
            pl.BlockSpec((blk, FOX_DH), lambda h, j: (j, h)),
            pl.BlockSpec((None, blk, 1), lambda h, j: (h, j, 0)),
            row,
        ],
        scratch_shapes=[
            pltpu.VMEM((blk, FOX_DH), F32),
            pltpu.VMEM((blk, FOX_DH), F32),
            pltpu.VMEM((blk, 128), F32),
            pltpu.VMEM((blk, 2 * blk), F32),
            pltpu.VMEM((s_len, FOX_DH), F32),
        ],
        compiler_params=_cp("parallel", "arbitrary"),
    )(qkv, qkv, qkv, cum_col, lse_row, delta_row, do)


def _swa_bias():
    cols = SWA_G * WINDOW
    k = np.arange(2 * WINDOW)[:, None]
    q = np.arange(cols)[None, :]
    dist = (q % WINDOW) - k + WINDOW
    valid = (dist >= 0) & (dist < WINDOW)
    out = np.empty((2, SWA_HKV, 2 * WINDOW, cols), np.float32)
    for g in range(SWA_HKV):
        slope = np.array([SLOPES[g * SWA_G + t] for t in range(SWA_G)], np.float32)[q // WINDOW]
        bias = -(slope * dist.astype(np.float32))
        out[0, g] = np.where(valid & (k >= WINDOW), bias, np.float32(NEG))
        out[1, g] = np.where(valid, bias, np.float32(NEG))
    return jnp.asarray(out)


def _swa_group(i, q_ref, kk, sinks_ref, bias_ref, g):
    cols = SWA_G * WINDOW
    head = lax.broadcasted_iota(jnp.int32, (1, cols), 1) // WINDOW
    sink = jnp.zeros((1, cols), F32)
    for t in range(SWA_G):
        sink = jnp.where(head == t, sinks_ref[g * SWA_G + t], sink)
    q = jnp.concatenate([q_ref[:, (g * SWA_G + t) * SWA_DH : (g * SWA_G + t + 1) * SWA_DH] for t in range(SWA_G)], axis=0)
    k = kk[:, g * SWA_DH : (g + 1) * SWA_DH]
    s = lax.dot_general(k, q, _NT, preferred_element_type=F32) * SWA_SCALE + bias_ref[jnp.minimum(i, 1), g]
    m = jnp.maximum(jnp.max(s, axis=0, keepdims=True), sink)
    e = jnp.exp(s - m)
    e_sink = jnp.exp(sink - m)
    inv = 1.0 / (jnp.sum(e, axis=0, keepdims=True) + e_sink)
    return q, k, e * inv, e_sink * inv


def _swa_specs(col_q, col_k, col_v, rev, nb):
    def blk(t):
        return nb - 1 - t if rev else t

    return [
        pl.BlockSpec((WINDOW, SWA_W), lambda t: (blk(t), col_q)),
        pl.BlockSpec((WINDOW, SWA_KVW), lambda t: (jnp.maximum(blk(t) - 1, 0), col_k)),
        pl.BlockSpec((WINDOW, SWA_KVW), lambda t: (blk(t), col_k)),
        pl.BlockSpec((WINDOW, SWA_KVW), lambda t: (jnp.maximum(blk(t) - 1, 0), col_v)),
        pl.BlockSpec((WINDOW, SWA_KVW), lambda t: (blk(t), col_v)),
    ]


def _swa_fwd(qkv, sinks):
    s_len = qkv.shape[0]
    nb = s_len // WINDOW
    bias_spec = pl.BlockSpec((2, SWA_HKV, 2 * WINDOW, SWA_G * WINDOW), lambda t: (0, 0, 0, 0))

    def body(q_ref, kp_ref, kc_ref, vp_ref, vc_ref, sinks_ref, bias_ref, o_ref):
        i = pl.program_id(0)
        kk = jnp.concatenate([kp_ref[...], kc_ref[...]], axis=0)
        vv = jnp.concatenate([vp_ref[...], vc_ref[...]], axis=0)
        for g in range(SWA_HKV):
            _, _, p, _ = _swa_group(i, q_ref, kk, sinks_ref, bias_ref, g)
            o = lax.dot_general(p.astype(BF16), vv[:, g * SWA_DH : (g + 1) * SWA_DH], _TN, preferred_element_type=F32)
            for t in range(SWA_G):
                h = g * SWA_G + t
                o_ref[:, h * SWA_DH : (h + 1) * SWA_DH] = o[t * WINDOW : (t + 1) * WINDOW, :]

    return _pcall(
        body,
        name="swa_fwd",
        grid=(nb,),
        out_shape=jax.ShapeDtypeStruct((s_len, SWA_W), F32),
        in_specs=_swa_specs(0, 4, 5, False, nb) + [pl.BlockSpec(memory_space=pltpu.SMEM), bias_spec],
        out_specs=pl.BlockSpec((WINDOW, SWA_W), lambda t: (t, 0)),
        compiler_params=_cp("parallel"),
    )(qkv, qkv, qkv, qkv, qkv, sinks, _swa_bias())


def _swa_bwd(qkv, sinks, do):
    s_len = qkv.shape[0]
    nb = s_len // WINDOW
    bias_spec = pl.BlockSpec((2, SWA_HKV, 2 * WINDOW, SWA_G * WINDOW), lambda t: (0, 0, 0, 0))

    def body(q_ref, kp_ref, kc_ref, vp_ref, vc_ref, sinks_ref, bias_ref, do_ref, dq_ref, dk_ref, dv_ref, dsink_ref, ck_s, cv_s, dkk_s, dvv_s):
        t = pl.program_id(0)
        i = nb - 1 - t

        @pl.when(t == 0)
        def _():
            ck_s[...] = jnp.zeros(ck_s.shape, F32)
            cv_s[...] = jnp.zeros(cv_s.shape, F32)
            dsink_ref[...] = jnp.zeros(dsink_ref.shape, F32)

        kk = jnp.concatenate([kp_ref[...], kc_ref[...]], axis=0)
        vv = jnp.concatenate([vp_ref[...], vc_ref[...]], axis=0)
        lane = lax.broadcasted_iota(jnp.int32, (1, 128), 1)
        dsink = jnp.zeros((1, 128), F32)
        for g in range(SWA_HKV):
            cols = slice(g * SWA_DH, (g + 1) * SWA_DH)
            q, k, p, p_sink = _swa_group(i, q_ref, kk, sinks_ref, bias_ref, g)
            dob = jnp.concatenate([do_ref[:, (g * SWA_G + t) * SWA_DH : (g * SWA_G + t + 1) * SWA_DH] for t in range(SWA_G)], axis=0)
            dp = lax.dot_general(vv[:, cols], dob, _NT, preferred_element_type=F32)
            delta = jnp.sum(p * dp, axis=0, keepdims=True)
            dsb = (p * (dp - delta)).astype(BF16)
            dq = (lax.dot_general(dsb, k, _TN, preferred_element_type=F32) * SWA_SCALE).astype(BF16)
            ps_d = p_sink * delta
            for t in range(SWA_G):
                h = g * SWA_G + t
                dq_ref[:, h * SWA_DH : (h + 1) * SWA_DH] = dq[t * WINDOW : (t + 1) * WINDOW, :]
                dsink = dsink + jnp.where(lane == h, -jnp.sum(ps_d[:, t * WINDOW : (t + 1) * WINDOW], axis=1, keepdims=True), 0.0)
            dkk_s[:, cols] = jnp.dot(dsb, q, preferred_element_type=F32) * SWA_SCALE
            dvv_s[:, cols] = jnp.dot(p.astype(BF16), dob, preferred_element_type=F32)
        dk_ref[...] = (dkk_s[WINDOW:, :] + ck_s[...]).astype(BF16)
        dv_ref[...] = (dvv_s[WINDOW:, :] + cv_s[...]).astype(BF16)
        ck_s[...] = dkk_s[:WINDOW, :]
        cv_s[...] = dvv_s[:WINDOW, :]
        dsink_ref[...] += dsink

    row = lambda t: (nb - 1 - t, 0)
    return _pcall(
        body,
        name="swa_bwd",
        grid=(nb,),
        out_shape=[
            jax.ShapeDtypeStruct((s_len, SWA_W), BF16),
            jax.ShapeDtypeStruct((s_len, SWA_KVW), BF16),
            jax.ShapeDtypeStruct((s_len, SWA_KVW), BF16),
            jax.ShapeDtypeStruct((1, 128), F32),
        ],
        in_specs=_swa_specs(0, 4, 5, True, nb)
        + [pl.BlockSpec(memory_space=pltpu.SMEM), bias_spec, pl.BlockSpec((WINDOW, SWA_W), row)],
        out_specs=[
            pl.BlockSpec((WINDOW, SWA_W), row),
            pl.BlockSpec((WINDOW, SWA_KVW), row),
            pl.BlockSpec((WINDOW, SWA_KVW), row),
            pl.BlockSpec((1, 128), lambda t: (0, 0)),
        ],
        scratch_shapes=[
            pltpu.VMEM((WINDOW, SWA_KVW), F32),
            pltpu.VMEM((WINDOW, SWA_KVW), F32),
            pltpu.VMEM((2 * WINDOW, SWA_KVW), F32),
            pltpu.VMEM((2 * WINDOW, SWA_KVW), F32),
        ],
        compiler_params=_cp("arbitrary"),
    )(qkv, qkv, qkv, qkv, qkv, sinks, _swa_bias(), do)


def _branch_fwd(o, gates, g_blk, w_b, name):
    s_len, wd = o.shape
    d = w_b.shape[1]
    tm = min(512, s_len)

    def body(o_ref, g_ref, w_ref, y_ref, a_ref):
        g = g_ref[...].astype(F32)
        a = (o_ref[...] * (g * _sigmoid(g))).astype(BF16)
        a_ref[...] = a
        y_ref[...] = jnp.dot(a, w_ref[...], preferred_element_type=F32).astype(BF16)

    return _pcall(
        body,
        name=name,
        grid=(s_len // tm,),
        out_shape=[jax.ShapeDtypeStruct((s_len, d), BF16), jax.ShapeDtypeStruct((s_len, wd), BF16)],
        in_specs=[
            pl.BlockSpec((tm, wd), lambda i: (i, 0)),
            pl.BlockSpec((tm, wd), lambda i: (i, g_blk)),
            pl.BlockSpec((wd, d), lambda i: (0, 0)),
        ],
        out_specs=[pl.BlockSpec((tm, d), lambda i: (i, 0)), pl.BlockSpec((tm, wd), lambda i: (i, 0))],
        compiler_params=_cp("parallel"),
    )(o, gates, w_b)


def _out_stage(gates, mf_blk, y_fox, y_swa, w_out, x, ada, ln_g, ln_b, target):
    s_len, d = x.shape
    tm = min(256, s_len)
    n_steps = s_len // tm

    def body(mf_ref, ms_ref, yf_ref, ys_ref, w_ref, x_ref, gate_ref, lg_ref, lb_ref, t_ref, mg_ref, dza_ref, dsub_ref, red_ref):
        i = pl.program_id(0)
        merged = _sigmoid(mf_ref[...].astype(F32)) * yf_ref[...].astype(F32) + _sigmoid(ms_ref[...].astype(F32)) * ys_ref[...].astype(F32)
        mb = merged.astype(BF16)
        mg_ref[...] = mb
        sub = jnp.dot(mb, w_ref[...], preferred_element_type=F32)
        gate = gate_ref[...]
        z = ALPHA * x_ref[...] + gate * sub
        mu = jnp.mean(z, axis=-1, keepdims=True)
        zc = z - mu
        var = jnp.mean(zc * zc, axis=-1, keepdims=True)
        rstd = lax.rsqrt(var + LN_EPS)
        zhat = zc * rstd
        err = zhat * lg_ref[...] + lb_ref[...] - t_ref[...]
        dout = err * (1.0 / d)
        dzhat = dout * lg_ref[...]
        dz = rstd * (dzhat - jnp.mean(dzhat, axis=-1, keepdims=True) - zhat * jnp.mean(dzhat * zhat, axis=-1, keepdims=True))
        dza_ref[...] = ALPHA * dz
        dsub_ref[...] = (gate * dz).astype(BF16)
        part = jnp.concatenate(
            [
                jnp.sum(dz * sub, axis=0, keepdims=True),
                jnp.sum(dout * zhat, axis=0, keepdims=True),
                jnp.sum(dout, axis=0, keepdims=True),
                jnp.sum(err * err, axis=0, keepdims=True),
                jnp.zeros((4, d), F32),
            ],
            axis=0,
        )

        @pl.when(i == 0)
        def _():
            red_ref[...] = part

        @pl.when(i > 0)
        def _():
            red_ref[...] += part

        @pl.when(i == n_steps - 1)
        def _():
            red_ref[4:5, :] = jnp.broadcast_to(jnp.sum(red_ref[3:4, :], axis=1, keepdims=True), (1, d))

    row = pl.BlockSpec((tm, d), lambda i: (i, 0))
    vec = pl.BlockSpec((1, d), lambda i: (0, 0))
    return _pcall(
        body,
        name="out_stage",
        grid=(n_steps,),
        out_shape=[
            jax.ShapeDtypeStruct((s_len, d), BF16),
            jax.ShapeDtypeStruct((s_len, d), F32),
            jax.ShapeDtypeStruct((s_len, d), BF16),
            jax.ShapeDtypeStruct((8, d), F32),
        ],
        in_specs=[
            pl.BlockSpec((tm, d), lambda i: (i, mf_blk)),
            pl.BlockSpec((tm, d), lambda i: (i, mf_blk + 1)),
            row,
            row,
            pl.BlockSpec((d, d), lambda i: (0, 0), pipeline_mode=pl.Buffered(1)),
            row,
            pl.BlockSpec((1, d), lambda i: (0, 2)),
            vec,
            vec,
            row,
        ],
        out_specs=[row, row, row, pl.BlockSpec((8, d), lambda i: (0, 0))],
        compiler_params=_cp("arbitrary"),
    )(gates, gates, y_fox, y_swa, w_out, x, ada, ln_g, ln_b, target)


def _merge_bwd(dsub, w_out, gates, mf_blk, y_fox, y_swa):
    s_len, d = dsub.shape
    tm = min(256, s_len)

    def body(ds_ref, w_ref, mf_ref, ms_ref, yf_ref, ys_ref, dmf_ref, dms_ref, dyf_ref, dys_ref):
        dm = lax.dot_general(ds_ref[...], w_ref[...], _NT, preferred_element_type=F32)
        sf, ss = _sigmoid(mf_ref[...].astype(F32)), _sigmoid(ms_ref[...].astype(F32))
        dmf_ref[...] = (dm * yf_ref[...].astype(F32) * (sf * (1.0 - sf))).astype(BF16)
        dms_ref[...] = (dm * ys_ref[...].astype(F32) * (ss * (1.0 - ss))).astype(BF16)
        dyf_ref[...] = (dm * sf).astype(BF16)
        dys_ref[...] = (dm * ss).astype(BF16)

    row = pl.BlockSpec((tm, d), lambda i: (i, 0))
    return _pcall(
        body,
        name="merge_bwd",
        grid=(s_len // tm,),
        out_shape=[jax.ShapeDtypeStruct((s_len, d), BF16)] * 4,
        in_specs=[
            row,
            pl.BlockSpec((d, d), lambda i: (0, 0), pipeline_mode=pl.Buffered(1)),
            pl.BlockSpec((tm, d), lambda i: (i, mf_blk)),
            pl.BlockSpec((tm, d), lambda i: (i, mf_blk + 1)),
            row,
            row,
        ],
        out_specs=[row] * 4,
        compiler_params=_cp("parallel"),
    )(dsub, w_out, gates, gates, y_fox, y_swa)


def _branch_bwd(dy, w_b, o, gates, g_blk, name, n_heads):
    s_len, d = dy.shape
    wd = w_b.shape[0]
    tm = min(512, s_len)

    def body(dy_ref, w_ref, o_ref, g_ref, do_ref, dg_ref, *rest):
        da = lax.dot_general(dy_ref[...], w_ref[...], _NT, preferred_element_type=F32)
        g = g_ref[...].astype(F32)
        sg = _sigmoid(g)
        do = da * (g * sg)
        do_ref[...] = do.astype(BF16)
        o = o_ref[...]
        dg_ref[...] = (da * o * (sg * (1.0 + g * (1.0 - sg)))).astype(BF16)
        if n_heads:
            prod = do.astype(BF16).astype(F32) * o
            lane = lax.broadcasted_iota(jnp.int32, (1, 128), 1)
            delta = jnp.zeros((tm, 128), F32)
            for h in range(n_heads):
                dh = jnp.sum(prod[:, h * 128 : (h + 1) * 128], axis=1, keepdims=True)
                delta = delta + jnp.where(lane == h, dh, 0.0)
            rest[0][...] = delta

    out_shape = [jax.ShapeDtypeStruct((s_len, wd), BF16), jax.ShapeDtypeStruct((s_len, wd), BF16)]
    out_specs = [pl.BlockSpec((tm, wd), lambda i: (i, 0))] * 2
    if n_heads:
        out_shape.append(jax.ShapeDtypeStruct((s_len, 128), F32))
        out_specs.append(pl.BlockSpec((tm, 128), lambda i: (i, 0)))
    return _pcall(
        body,
        name=name,
        grid=(s_len // tm,),
        out_shape=out_shape,
        in_specs=[
            pl.BlockSpec((tm, d), lambda i: (i, 0)),
            pl.BlockSpec((wd, d), lambda i: (0, 0)),
            pl.BlockSpec((tm, wd), lambda i: (i, 0)),
            pl.BlockSpec((tm, wd), lambda i: (i, g_blk)),
        ],
        out_specs=out_specs,
        compiler_params=_cp("parallel"),
    )(dy, w_b, o, gates)


def _in_bwd(dproj, w_in_t, x, ada, dza, ride):
    s_len, d = x.shape
    k_tot = dproj.shape[1]
    tm, tk, dn = min(512, s_len), k_tot // 4, d // 2
    ni, nk = s_len // tm, k_tot // tk
    n = len(ride)

    def body(dp_ref, w_ref, x_ref, sc_ref, dza_ref, *rest):
        ins, (gx_ref, red_ref), outs = rest[:n], rest[n : n + 2], rest[n + 2 : 2 * n + 2]
        sems, acc_s = rest[2 * n + 2 : 2 * n + 5], rest[2 * n + 5]
        i, nh, kk = pl.program_id(0), pl.program_id(1), pl.program_id(2)

        @pl.when((i == 0) & (nh == 0) & (kk == 0))
        def _():
            _rider_start("exchange", ins, outs, *sems)

        @pl.when((i == ni - 1) & (nh == 1) & (kk == nk - 1))
        def _():
            _rider_wait("exchange", ins, outs, *sems)

        part = jnp.dot(dp_ref[...], w_ref[...], preferred_element_type=F32)
        half = pl.ds(pl.multiple_of(nh * dn, dn), dn)

        @pl.when(kk == 0)
        def _():
            acc_s[:, half] = part

        @pl.when(kk > 0)
        def _():
            acc_s[:, half] += part

        @pl.when((nh == 1) & (kk == nk - 1))
        def _():
            dh = acc_s[...]
            xv = x_ref[...]
            mu = jnp.mean(xv, axis=-1, keepdims=True)
            xc = xv - mu
            var = jnp.mean(xc * xc, axis=-1, keepdims=True)
            rstd = lax.rsqrt(var + LN_EPS)
            xhat = xc * rstd
            dxhat = dh * (1.0 + sc_ref[...])
            dx = rstd * (dxhat - jnp.mean(dxhat, axis=-1, keepdims=True) - xhat * jnp.mean(dxhat * xhat, axis=-1, keepdims=True))
            gx_ref[...] = dza_ref[...] + dx
            part_r = jnp.concatenate(
                [jnp.sum(dh, axis=0, keepdims=True), jnp.sum(dh * xhat, axis=0, keepdims=True), jnp.zeros((6, d), F32)], axis=0
            )

            @pl.when(i == 0)
            def _():
                red_ref[...] = part_r

            @pl.when(i > 0)
            def _():
                red_ref[...] += part_r

    row = pl.BlockSpec((tm, d), lambda i, nh, kk: (i, 0))
    hbm = pl.BlockSpec(memory_space=pltpu.HBM)
    return _pcall(
        body,
        name="in_bwd",
        grid=(ni, 2, nk),
        out_shape=[jax.ShapeDtypeStruct((s_len, d), F32), jax.ShapeDtypeStruct((8, d), F32)]
        + [jax.ShapeDtypeStruct(r.shape, r.dtype) for r in ride],
        in_specs=[
            pl.BlockSpec((tm, tk), lambda i, nh, kk: (i, kk)),
            pl.BlockSpec((tk, dn), lambda i, nh, kk: (kk, nh)),
            row,
            pl.BlockSpec((1, d), lambda i, nh, kk: (0, 1)),
            row,
        ]
        + [hbm] * n,
        out_specs=[row, pl.BlockSpec((8, d), lambda i, nh, kk: (0, 0))] + [hbm] * n,
        scratch_shapes=_rider_scratch(n) + [pltpu.VMEM((tm, d), F32)],
        compiler_params=_cp("arbitrary", "arbitrary", "arbitrary"),
    )(dproj, w_in_t, x, ada, dza, *ride)


def _pad_lanes(v, n):
    return jnp.pad(v, ((0, 0), (0, n - v.shape[1])))


def kernel(x, c, w_ada, b_ada, w_in, b_f, attn_sinks, w_br_fox, w_br_swa, w_out, ln_g, ln_b, loss_target, m_w_ada, m_b_ada, m_w_in, m_b_f, m_attn_sinks, m_w_br_fox, m_w_br_swa, m_w_out, m_ln_g, m_ln_b, v_w_ada, v_b_ada, v_w_in, v_b_f, v_attn_sinks, v_w_br_fox, v_w_br_swa, v_w_out, v_ln_g, v_ln_b):
    x2, tgt = x[0], loss_target[0]
    s_len, d = x2.shape
    me = 4 * lax.axis_index("x") + 2 * lax.axis_index("y") + lax.axis_index("c")
    off_ms = OFF_MF + d
    in_pad = off_ms + d
    c_ada = w_ada.shape[2]
    c_in = w_in.shape[2]
    c_br = w_br_fox.shape[2]

    w_in_full = _all_gather(w_in[0].T.astype(BF16), "ag_w_in", pltpu.HBM).reshape(N_DEV * c_in, d)
    w_in_pad = jnp.concatenate(
        [w_in_full[:REAL_FLOG_END], jnp.zeros((FLOG_PAD - N_FLOG, d), BF16), w_in_full[REAL_FLOG_END:]], axis=0
    )
    k_cut = REAL_FLOG_END // c_in

    c_all = _gather_rows(c, "ag_c")
    b_cols = lax.dynamic_slice(b_ada, (0, me * c_ada), (1, c_ada))
    ada_cols = _ada_fwd(c_all, w_ada[0], b_cols)
    ada_g = _all_gather(ada_cols, "ag_ada", pltpu.VMEM)
    ada = lax.dynamic_index_in_dim(ada_g, me, axis=1, keepdims=False).reshape(1, N_DEV * c_ada)

    h = _ln_mod(x2, ada)
    qkv_fox = _mm_cols(h, w_in_pad, OFF_FQ, 3 * FOX_W, BF16, "proj_fox")
    flog = _mm_cols(h, w_in_pad, OFF_FLOG, 128, F32, "proj_flog")
    qkv_swa = _mm_cols(h, w_in_pad, OFF_SQ, SWA_W + 2 * SWA_KVW, BF16, "proj_swa")
    gates, w_bf, w_bs, w_o = _mm_cols(
        h, w_in_pad, OFF_GF, in_pad - OFF_GF, BF16, "proj_gates",
        ride=(w_br_fox[0].astype(BF16), w_br_swa[0].astype(BF16), w_out[0].astype(BF16)),
    )
    w_bf = w_bf.reshape(N_DEV, FOX_W, c_br).transpose(1, 0, 2).reshape(FOX_W, d)
    w_bs = w_bs.reshape(N_DEV, SWA_W, c_br).transpose(1, 0, 2).reshape(SWA_W, d)
    w_o = w_o.reshape(d, d)
    mf_blk = (OFF_MF - OFF_GF) // d

    flog_t = flog[:, :N_FLOG].T
    bf_col = b_f.reshape(FOX_H, 1)
    cum = _fox_cum(flog_t, bf_col)
    cum_row = cum.reshape(FOX_H, 1, s_len)
    o_fox, lse = _fox_fwd(qkv_fox, cum_row)
    sinks = attn_sinks.reshape(SWA_HQ)
    o_swa = _swa_fwd(qkv_swa, sinks)

    y_fox, a_fox = _branch_fwd(o_fox, gates, 0, w_bf, "branch_fox")
    y_swa, a_swa = _branch_fwd(o_swa, gates, 1, w_bs, "branch_swa")
    merged, dza, dsub, red = _out_stage(gates, mf_blk, y_fox, y_swa, w_o, x2, ada, ln_g, ln_b, tgt)
    loss = lax.psum(0.5 * red[4, 0] / d, ("x", "y", "c"))

    dmf, dms, dy_fox, dy_swa = _merge_bwd(dsub, w_o, gates, mf_blk, y_fox, y_swa)
    do_fox, dg_fox, delta = _branch_bwd(dy_fox, w_bf, o_fox, gates, 0, "branch_fox_bwd", FOX_H)
    do_swa, dg_swa = _branch_bwd(dy_swa, w_bs, o_swa, gates, 1, "branch_swa_bwd", 0)
    delta_row = delta[:, :FOX_H].T.reshape(FOX_H, 1, s_len)
    dq_f, dk_f, dv_f, dcol, drow = _fox_bwd(
        qkv_fox, cum.reshape(FOX_H, s_len, 1), lse.reshape(FOX_H, 1, s_len), delta_row, do_fox
    )
    dflog_t, dbf = _fox_gate_bwd(drow.reshape(FOX_H, s_len), dcol.reshape(FOX_H, s_len), flog_t, bf_col)
    dq_s, dk_s, dv_s, dsink = _swa_bwd(qkv_swa, sinks, do_swa)
    dflog = _pad_lanes(dflog_t.T, FLOG_PAD).astype(BF16)
    dproj = jnp.concatenate([dq_f, dk_f, dv_f, dflog, dq_s, dk_s, dv_s, dg_fox, dg_swa, dmf, dms], axis=1)
    g_w_bf = _mm_tn(a_fox, dy_fox, "grad_w_br_fox")
    g_w_bs = _mm_tn(a_swa, dy_swa, "grad_w_br_swa")
    g_w_o = _mm_tn(merged, dsub, "grad_w_out")
    g_w_in, r_bf, r_bs, r_o = _mm_tn(
        dproj, h, "grad_w_in",
        ride=(
            g_w_bf.reshape(FOX_W, N_DEV, c_br).transpose(1, 0, 2),
            g_w_bs.reshape(SWA_W, N_DEV, c_br).transpose(1, 0, 2),
            g_w_o.reshape(N_DEV, d // N_DEV, d),
        ),
    )
    pad = FLOG_PAD - N_FLOG
    g_blocks = jnp.stack(
        [g_w_in[k * c_in : (k + 1) * c_in] for k in range(k_cut)]
        + [jnp.concatenate([g_w_in[k_cut * c_in : REAL_FLOG_END], g_w_in[OFF_SQ : (k_cut + 1) * c_in + pad]], axis=0)]
        + [g_w_in[k * c_in + pad : (k + 1) * c_in + pad] for k in range(k_cut + 1, N_DEV)]
    )

    grad_x, red2, r_in = _in_bwd(dproj, w_in_pad, x2, ada, dza, ride=(g_blocks,))
    out_w_in = _sum_adam_t(r_in, w_in[0].T, m_w_in[0].T, v_w_in[0].T, "adam_w_in")
    out_w_in = [o.T for o in out_w_in]
    out_w_bf = _sum_adam(r_bf, w_br_fox[0], m_w_br_fox[0], v_w_br_fox[0], "adam_w_br_fox")
    out_w_bs = _sum_adam(r_bs, w_br_swa[0], m_w_br_swa[0], v_w_br_swa[0], "adam_w_br_swa")
    out_w_o = _sum_adam(r_o, w_out[0], m_w_out[0], v_w_out[0], "adam_w_out")

    packed = jnp.concatenate([red2[0:1], red2[1:2], red[0:1], _pad_lanes(dbf[:, 0].reshape(1, FOX_H), 128), dsink, red[1:2], red[2:3]], axis=1)
    gathered = _gather_rows(packed, "ag_small")
    pack = lambda a, b, cc, dd, e: jnp.concatenate([a, _pad_lanes(b, 128), _pad_lanes(cc, 128), dd, e], axis=1)
    small = _small_adam(
        gathered,
        pack(b_ada, b_f, attn_sinks, ln_g, ln_b),
        pack(m_b_ada, m_b_f, m_attn_sinks, m_ln_g, m_ln_b),
        pack(v_b_ada, v_b_f, v_attn_sinks, v_ln_g, v_ln_b),
    )
    dada_cols = lax.dynamic_slice(gathered, (0, me * c_ada), (N_DEV, c_ada))
    out_w_ada = _wada_adam(c_all.T, dada_cols, w_ada[0], m_w_ada[0], v_w_ada[0])

    o1, o2, o3 = 3 * d, 3 * d + 128, 3 * d + 256

    def unpack(p):
        return p[:, :o1], p[:, o1 : o1 + FOX_H], p[:, o2 : o2 + SWA_HQ], p[:, o3 : o3 + d], p[:, o3 + d : o3 + 2 * d]

    kinds = []
    for k in range(4):
        b_ada_k, b_f_k, sinks_k, ln_g_k, ln_b_k = unpack(small[k])
        kinds.append(
            [out_w_ada[k][None], b_ada_k, out_w_in[k][None], b_f_k, sinks_k, out_w_bf[k][None], out_w_bs[k][None], out_w_o[k][None], ln_g_k, ln_b_k]
        )
    return (loss, grad_x[None], *kinds[0], *kinds[1], *kinds[2], *kinds[3])
```

```python
import numpy as np
import jax
import jax.numpy as jnp
from jax import lax
from jax.experimental import pallas as pl
from jax.experimental.pallas import tpu as pltpu

F32 = jnp.float32
BF16 = jnp.bfloat16
N_DEV = 8
MESH = pl.DeviceIdType.MESH

FOX_H, FOX_DH, FOX_W = 8, 128, 1024
SWA_HQ, SWA_HKV, SWA_DH, SWA_G = 16, 4, 64, 4
SWA_W, SWA_KVW, WINDOW = 1024, 256, 128
LN_EPS = 1e-5
NEG = -1e30
DEPTH = 1
ALPHA = (2.0 * DEPTH) ** 0.25
FOX_SCALE = FOX_DH ** -0.5
SWA_SCALE = SWA_DH ** -0.5
SLOPES = [2.0 ** (-8.0 * (h + 1.0) / SWA_HQ) for h in range(SWA_HQ)]

ADAM_LR, ADAM_B1, ADAM_B2, ADAM_EPS, ADAM_WD, ADAM_STEP = 0.001, 0.9, 0.999, 1e-08, 0.01, 10

N_FLOG = 8
FLOG_PAD = 512
OFF_FQ, OFF_FK, OFF_FV, OFF_FLOG = 0, 1024, 2048, 3072
OFF_SQ = OFF_FLOG + FLOG_PAD
OFF_SK = OFF_SQ + SWA_W
OFF_SV = OFF_SK + SWA_KVW
OFF_GF = OFF_SV + SWA_KVW
OFF_GS = OFF_GF + FOX_W
OFF_MF = OFF_GS + SWA_W
REAL_FLOG_END = OFF_FLOG + N_FLOG

ATT_BLK = 512
VMEM_LIMIT = 58 * 1024 * 1024


def _pcall(body, **kw):
    return pl.pallas_call(body, **kw)


def _cp(*sem):
    return pltpu.CompilerParams(dimension_semantics=sem, vmem_limit_bytes=VMEM_LIMIT)


def _sigmoid(x):
    return 0.5 * jnp.tanh(0.5 * x) + 0.5


def _all_gather(x, name, space):
    m_per, n = x.shape

    def body(x_ref, out_ref, send_sems, recv_sems, local_sem):
        mx, my, mc = lax.axis_index("x"), lax.axis_index("y"), lax.axis_index("c")
        me, sibling = (mx, my, mc), (mx, my, 1 - mc)
        xn, yn, dg = (1 - mx, my), (mx, 1 - my), (1 - mx, 1 - my)
        south = mc == 0
        src_chip = (jnp.where(south, 1 - mx, mx), jnp.where(south, my, 1 - my))
        dst_chip = (jnp.where(south, mx, 1 - mx), jnp.where(south, 1 - my, my))

        def rows(px, py, pc):
            return out_ref.at[4 * px + 2 * py + pc]

        def copy(k, block, to, src=None):
            return pltpu.make_async_remote_copy(
                src_ref=rows(*block) if src is None else src,
                dst_ref=rows(*block),
                send_sem=send_sems.at[k],
                recv_sem=recv_sems.at[k],
                device_id=to,
                device_id_type=MESH,
            )

        mine = pltpu.make_async_copy(x_ref, rows(*me), local_sem)
        mine.start()
        first = [copy(0, me, sibling, src=x_ref), copy(1, me, (*xn, mc), src=x_ref), copy(2, me, (*yn, mc), src=x_ref)]
        for cp in first:
            cp.start()
        copy(1, (*xn, mc), me).wait_recv()
        copy(2, (*yn, mc), me).wait_recv()
        later = [copy(3, (*src_chip, mc), (*dst_chip, mc)), copy(4, (*xn, mc), sibling), copy(5, (*yn, mc), sibling)]
        for cp in later:
            cp.start()
        copy(3, (*dg, mc), me).wait_recv()
        last = copy(6, (*dg, mc), sibling)
        last.start()
        copy(0, sibling, me).wait_recv()
        for k, chip in ((4, xn), (5, yn), (6, dg)):
            copy(k, (*chip, 1 - mc), me).wait_recv()
        for cp in first + later + [last]:
            cp.wait_send()
        mine.wait()

    return _pcall(
        body,
        name=name,
        out_shape=jax.ShapeDtypeStruct((N_DEV, m_per, n), x.dtype),
        in_specs=[pl.BlockSpec(memory_space=space)],
        out_specs=pl.BlockSpec(memory_space=space),
        scratch_shapes=[pltpu.SemaphoreType.DMA((7,)), pltpu.SemaphoreType.DMA((7,)), pltpu.SemaphoreType.DMA],
    )(x)


def _peer(d, mx, my, mc):
    return (1 - mx if (d >> 2) & 1 else mx, 1 - my if (d >> 1) & 1 else my, 1 - mc if d & 1 else mc)


def _rider_copies(kind, ins, outs, send_sems, recv_sems, local_sems):
    mx, my, mc = lax.axis_index("x"), lax.axis_index("y"), lax.axis_index("c")
    me = 4 * mx + 2 * my + mc
    remote, local = [], []
    for a in range(len(ins)):
        if kind == "gather":
            m_per = ins[a].shape[0]
            mine = outs[a].at[pl.ds(me * m_per, m_per), :]
            local.append(pltpu.make_async_copy(ins[a], mine, local_sems.at[a]))
        else:
            local.append(pltpu.make_async_copy(ins[a].at[me], outs[a].at[0], local_sems.at[a]))
        for d in range(1, N_DEV):
            px, py, pc = _peer(d, mx, my, mc)
            if kind == "gather":
                src, dst = ins[a], mine
            else:
                src, dst = ins[a].at[4 * px + 2 * py + pc], outs[a].at[d]
            remote.append(
                pltpu.make_async_remote_copy(
                    src_ref=src,
                    dst_ref=dst,
                    send_sem=send_sems.at[a * 7 + d - 1],
                    recv_sem=recv_sems.at[a * 7 + d - 1],
                    device_id=(px, py, pc),
                    device_id_type=MESH,
                )
            )
    return remote, local


def _rider_start(*args):
    remote, local = _rider_copies(*args)
    for cp in local + remote:
        cp.start()


def _rider_wait(*args):
    remote, local = _rider_copies(*args)
    for cp in remote:
        cp.wait_recv()
    for cp in remote:
        cp.wait_send()
    for cp in local:
        cp.wait()


def _rider_scratch(n):
    return [pltpu.SemaphoreType.DMA((7 * n,)), pltpu.SemaphoreType.DMA((7 * n,)), pltpu.SemaphoreType.DMA((n,))]


def _gather_rows(v, name):
    n = v.shape[1]
    return _all_gather(jnp.broadcast_to(v, (8, n)), name, pltpu.VMEM)[:, 0, :]


def _adamw(w, g, m, v):
    m = ADAM_B1 * m + (1.0 - ADAM_B1) * g
    v = ADAM_B2 * v + (1.0 - ADAM_B2) * (g * g)
    m_hat = m / (1.0 - ADAM_B1**ADAM_STEP)
    v_hat = v / (1.0 - ADAM_B2**ADAM_STEP)
    delta = -ADAM_LR * (m_hat / (jnp.sqrt(v_hat) + ADAM_EPS) + ADAM_WD * w)
    return delta, m, v


def _sum_adam(recv, w, m, v, name):
    _, r_tot, c = recv.shape
    c_pad = -(-c // 128) * 128
    tr = r_tot
    while 8 * tr * c_pad * 4 > 6 * 1024 * 1024 and tr % 32 == 0:
        tr //= 2

    def body(r_ref, w_ref, m_ref, v_ref, g_ref, d_ref, nm_ref, nv_ref):
        g = r_ref[0].astype(F32)
        for k in range(1, N_DEV):
            g = g + r_ref[k].astype(F32)
        d, nm, nv = _adamw(w_ref[...], g, m_ref[...], v_ref[...])
        g_ref[...] = g
        d_ref[...] = d
        nm_ref[...] = nm
        nv_ref[...] = nv

    blk = pl.BlockSpec((tr, c), lambda i: (i, 0))
    return _pcall(
        body,
        name=name,
        grid=(r_tot // tr,),
        out_shape=[jax.ShapeDtypeStruct((r_tot, c), F32)] * 4,
        in_specs=[pl.BlockSpec((N_DEV, tr, c), lambda i: (0, i, 0)), blk, blk, blk],
        out_specs=[blk] * 4,
        compiler_params=_cp("parallel"),
    )(recv, w, m, v)


def _sum_adam_t(recv, w, m, v, name):
    _, c, r_tot = recv.shape
    tr = min(256, r_tot)

    def body(r_ref, w_ref, m_ref, v_ref, g_ref, d_ref, nm_ref, nv_ref):
        g = r_ref[0].astype(F32)
        for k in range(1, N_DEV):
            g = g + r_ref[k].astype(F32)
        d, nm, nv = _adamw(w_ref[...], g, m_ref[...], v_ref[...])
        g_ref[...] = g
        d_ref[...] = d
        nm_ref[...] = nm
        nv_ref[...] = nv

    blk = pl.BlockSpec((c, tr), lambda i: (0, i))
    return _pcall(
        body,
        name=name,
        grid=(r_tot // tr,),
        out_shape=[jax.ShapeDtypeStruct((c, r_tot), F32)] * 4,
        in_specs=[pl.BlockSpec((N_DEV, c, tr), lambda i: (0, 0, i)), blk, blk, blk],
        out_specs=[blk] * 4,
        compiler_params=_cp("parallel"),
    )(recv, w, m, v)


def _wada_adam(c_t, dada_cols, w, m, v):
    d_model, c = w.shape
    tr = min(256, d_model)

    def body(ct_ref, da_ref, w_ref, m_ref, v_ref, g_ref, d_ref, nm_ref, nv_ref):
        g = jnp.dot(ct_ref[...].astype(BF16), da_ref[...].astype(BF16), preferred_element_type=F32)
        d, nm, nv = _adamw(w_ref[...], g, m_ref[...], v_ref[...])
        g_ref[...] = g
        d_ref[...] = d
        nm_ref[...] = nm
        nv_ref[...] = nv

    blk = pl.BlockSpec((tr, c), lambda i: (i, 0))
    return _pcall(
        body,
        name="wada_adam",
        grid=(d_model // tr,),
        out_shape=[jax.ShapeDtypeStruct((d_model, c), F32)] * 4,
        in_specs=[pl.BlockSpec((tr, N_DEV), lambda i: (i, 0)), pl.BlockSpec((N_DEV, c), lambda i: (0, 0)), blk, blk, blk],
        out_specs=[blk] * 4,
        compiler_params=_cp("parallel"),
    )(c_t, dada_cols, w, m, v)


def _small_adam(gathered, w, m, v):
    p = w.shape[1]

    def body(a_ref, w_ref, m_ref, v_ref, g_ref, d_ref, nm_ref, nv_ref):
        g = a_ref[0:1, :]
        for k in range(1, N_DEV):
            g = g + a_ref[k : k + 1, :]
        d, nm, nv = _adamw(w_ref[...], g, m_ref[...], v_ref[...])
        g_ref[...] = g
        d_ref[...] = d
        nm_ref[...] = nm
        nv_ref[...] = nv

    return _pcall(
        body,
        name="small_adam",
        out_shape=[jax.ShapeDtypeStruct((1, p), F32)] * 4,
    )(gathered, w, m, v)


def _ada_fwd(c_all, w_ada, b_cols):
    c = w_ada.shape[1]

    def body(c_ref, w_ref, b_ref, o_ref):
        o_ref[...] = jnp.dot(c_ref[...].astype(BF16), w_ref[...].astype(BF16), preferred_element_type=F32) + b_ref[...]

    return _pcall(
        body,
        name="ada_fwd",
        out_shape=jax.ShapeDtypeStruct((N_DEV, c), F32),
        compiler_params=_cp(),
    )(c_all, w_ada, b_cols)


def _ln_mod(x, ada):
    s_len, d = x.shape
    tm = min(512, s_len)

    def body(x_ref, sh_ref, sc_ref, h_ref):
        xv = x_ref[...]
        mu = jnp.mean(xv, axis=-1, keepdims=True)
        xc = xv - mu
        var = jnp.mean(xc * xc, axis=-1, keepdims=True)
        xhat = xc * lax.rsqrt(var + LN_EPS)
        h_ref[...] = (xhat * (1.0 + sc_ref[...]) + sh_ref[...]).astype(BF16)

    return _pcall(
        body,
        name="ln_mod",
        grid=(s_len // tm,),
        out_shape=jax.ShapeDtypeStruct((s_len, d), BF16),
        in_specs=[
            pl.BlockSpec((tm, d), lambda i: (i, 0)),
            pl.BlockSpec((1, d), lambda i: (0, 0)),
            pl.BlockSpec((1, d), lambda i: (0, 1)),
        ],
        out_specs=pl.BlockSpec((tm, d), lambda i: (i, 0)),
        compiler_params=_cp("parallel"),
    )(x, ada, ada)


def _mm_cols(a, b, col_off, n_cols, out_dtype, name, ride=()):
    m, k = a.shape
    tm, tn = min(1024, m), min(512, n_cols)
    off = col_off // tn
    ni, nj = m // tm, n_cols // tn
    n = len(ride)

    def body(a_ref, b_ref, *rest):
        ins, o_ref, outs, sems = rest[:n], rest[n], rest[n + 1 : 2 * n + 1], rest[2 * n + 1 :]
        i, j = pl.program_id(0), pl.program_id(1)
        if n:

            @pl.when((i == 0) & (j == 0))
            def _():
                _rider_start("gather", ins, outs, *sems)

        o_ref[...] = lax.dot_general(a_ref[...], b_ref[...], _NT, preferred_element_type=F32).astype(out_dtype)
        if n:

            @pl.when((i == ni - 1) & (j == nj - 1))
            def _():
                _rider_wait("gather", ins, outs, *sems)

    hbm = pl.BlockSpec(memory_space=pltpu.HBM)
    out = _pcall(
        body,
        name=name,
        grid=(ni, nj),
        out_shape=[jax.ShapeDtypeStruct((m, n_cols), out_dtype)]
        + [jax.ShapeDtypeStruct((N_DEV * r.shape[0], r.shape[1]), r.dtype) for r in ride],
        in_specs=[pl.BlockSpec((tm, k), lambda i, j: (i, 0)), pl.BlockSpec((tn, k), lambda i, j: (off + j, 0))] + [hbm] * n,
        out_specs=[pl.BlockSpec((tm, tn), lambda i, j: (i, j))] + [hbm] * n,
        scratch_shapes=_rider_scratch(n) if n else [],
        compiler_params=_cp("arbitrary", "arbitrary") if n else _cp("parallel", "parallel"),
    )(a, b, *ride)
    return out if n else out[0]


def _mm_tn(a, b, name, ride=()):
    s_len, m = a.shape
    n = b.shape[1]
    tm, tn, ts = min(1024, m), min(1024, n), min(2048, s_len)
    ni, nj, ns = m // tm, n // tn, s_len // ts
    nr = len(ride)

    def body(a_ref, b_ref, *rest):
        ins, o_ref, outs = rest[:nr], rest[nr], rest[nr + 1 : 2 * nr + 1]
        sems, acc_s = rest[2 * nr + 1 : -1], rest[-1]
        i, j, kk = pl.program_id(0), pl.program_id(1), pl.program_id(2)
        if nr:

            @pl.when((i == 0) & (j == 0) & (kk == 0))
            def _():
                _rider_start("exchange", ins, outs, *sems)

            @pl.when((i == ni - 1) & (j == nj - 1) & (kk == ns - 1))
            def _():
                _rider_wait("exchange", ins, outs, *sems)

        part = lax.dot_general(a_ref[...], b_ref[...], _TN, preferred_element_type=F32)

        @pl.when(kk == 0)
        def _():
            acc_s[...] = part

        @pl.when(kk > 0)
        def _():
            acc_s[...] += part

        @pl.when(kk == ns - 1)
        def _():
            o_ref[...] = acc_s[...].astype(BF16)

    hbm = pl.BlockSpec(memory_space=pltpu.HBM)
    out = _pcall(
        body,
        name=name,
        grid=(ni, nj, ns),
        out_shape=[jax.ShapeDtypeStruct((m, n), BF16)] + [jax.ShapeDtypeStruct(r.shape, r.dtype) for r in ride],
        in_specs=[pl.BlockSpec((ts, tm), lambda i, j, kk: (kk, i)), pl.BlockSpec((ts, tn), lambda i, j, kk: (kk, j))] + [hbm] * nr,
        out_specs=[pl.BlockSpec((tm, tn), lambda i, j, kk: (i, j))] + [hbm] * nr,
        scratch_shapes=(_rider_scratch(nr) if nr else []) + [pltpu.VMEM((tm, tn), F32)],
        compiler_params=_cp("arbitrary", "arbitrary", "arbitrary") if nr else _cp("parallel", "parallel", "arbitrary"),
    )(a, b, *ride)
    return out if nr else out[0]


def _split3(a):
    hi = a.astype(BF16)
    r1 = a - hi.astype(F32)
    mid = r1.astype(BF16)
    lo = (r1 - mid.astype(F32)).astype(BF16)
    return hi, mid, lo


def _dot_ones(a, tri):
    return sum(jnp.dot(t, tri, preferred_element_type=F32) for t in _split3(a))


def _log_sigmoid(x):
    return jnp.minimum(x, 0.0) - jnp.log1p(jnp.exp(-jnp.abs(x)))


def _fox_cum(flog_t, bf_col):
    s_len = flog_t.shape[1]

    def body(fl_ref, bf_ref, cum_ref):
        r = lax.broadcasted_iota(jnp.int32, (128, 128), 0)
        c = lax.broadcasted_iota(jnp.int32, (128, 128), 1)
        upper = (r <= c).astype(BF16)

        def step(t, carry):
            sl = pl.ds(pl.multiple_of(t * 128, 128), 128)
            lf = _log_sigmoid(fl_ref[:, sl] + bf_ref[...])
            cs = _dot_ones(lf, upper) + carry
            cum_ref[:, sl] = cs
            return cs[:, 127:128]

        lax.fori_loop(0, s_len // 128, step, jnp.zeros((FOX_H, 1), F32))

    return _pcall(body, name="fox_cum", out_shape=jax.ShapeDtypeStruct((FOX_H, s_len), F32))(flog_t, bf_col)


def _fox_gate_bwd(drow, dcol, flog_t, bf_col):
    s_len = flog_t.shape[1]
    n = s_len // 128

    def body(dr_ref, dc_ref, fl_ref, bf_ref, dfl_ref, dbf_ref):
        r = lax.broadcasted_iota(jnp.int32, (128, 128), 0)
        c = lax.broadcasted_iota(jnp.int32, (128, 128), 1)
        lower = (r >= c).astype(BF16)

        def step(t, carry):
            run, tot = carry
            sl = pl.ds(pl.multiple_of((n - 1 - t) * 128, 128), 128)
            rc = _dot_ones(dr_ref[:, sl] - dc_ref[:, sl], lower) + run
            dfl = rc * _sigmoid(-(fl_ref[:, sl] + bf_ref[...]))
            dfl_ref[:, sl] = dfl
            return rc[:, 0:1], tot + jnp.sum(dfl, axis=1, keepdims=True)

        zero = jnp.zeros((FOX_H, 1), F32)
        _, tot = lax.fori_loop(0, n, step, (zero, zero))
        dbf_ref[...] = jnp.broadcast_to(tot, (FOX_H, 128))

    return _pcall(
        body,
        name="fox_gate_bwd",
        out_shape=[jax.ShapeDtypeStruct((FOX_H, s_len), F32), jax.ShapeDtypeStruct((FOX_H, 128), F32)],
    )(drow, dcol, flog_t, bf_col)


def _diag_mask(blk, transposed=False):
    r = lax.broadcasted_iota(jnp.int32, (blk, blk), 0)
    c = lax.broadcasted_iota(jnp.int32, (blk, blk), 1)
    return c >= r if transposed else r >= c


_NT = (((1,), (1,)), ((), ()))
_TN = (((0,), (0,)), ((), ()))


def _fox_fwd(qkv, cum_row):
    s_len = qkv.shape[0]
    blk = min(ATT_BLK, s_len)
    nb = s_len // blk
    log2e = 1.4426950408889634

    def body(q_ref, k_ref, v_ref, c_ref, o_ref, lse_ref, mx_s, acc_s, u_s):
        i = pl.program_id(1)

        def key_cols(j, n):
            return pl.ds(pl.multiple_of(j * blk, blk), n * blk)

        def walk(tile):
            def four_pairs(t, carry):
                for u in range(4):
                    tile(8 * t + 2 * u, 2, False)
                return carry

            lax.fori_loop(0, i // 8, four_pairs, 0)

            @pl.when((i // 4) % 2 == 1)
            def _():
                tile(8 * (i // 8), 2, False)
                tile(8 * (i // 8) + 2, 2, False)

            @pl.when((i // 2) % 2 == 1)
            def _():
                tile(4 * (i // 4), 2, False)

            @pl.when(i % 2 == 1)
            def _():
                tile(i - 1, 1, False)

            tile(i, 1, True)

        def lane_max(j, n, masked):
            cols = key_cols(j, n)
            u = lax.dot_general(q_ref[...], k_ref[cols, :], _NT, preferred_element_type=F32) * (FOX_SCALE * log2e) - c_ref[:, cols] * log2e
            if masked:
                u = jnp.where(_diag_mask(blk), u, NEG)
            u_s[:, cols] = u
            part = u[:, 0:128]
            for t in range(1, n * blk // 128):
                part = jnp.maximum(part, u[:, t * 128 : (t + 1) * 128])
            mx_s[...] = jnp.maximum(mx_s[...], part)

        mx_s[...] = jnp.full(mx_s.shape, NEG, F32)
        walk(lane_max)
        m = jnp.max(mx_s[...], axis=1, keepdims=True)

        def weigh(j, n, masked):
            cols = key_cols(j, n)
            p = jnp.exp2(u_s[:, cols] - m)
            ones_col = (lax.broadcasted_iota(jnp.int32, (n * blk, 128), 1) == 0).astype(BF16)
            v1 = jnp.concatenate([v_ref[cols, :], ones_col], axis=1)
            acc_s[...] += jnp.dot(p.astype(BF16), v1, preferred_element_type=F32)

        acc_s[...] = jnp.zeros(acc_s.shape, F32)
        walk(weigh)
        l = acc_s[:, FOX_DH : FOX_DH + 1]
        o_ref[...] = acc_s[:, :FOX_DH] / l
        lse_ref[...] = m * (1.0 / log2e) + jnp.log(l)

    return _pcall(
        body,
        name="fox_fwd",
        grid=(FOX_H, nb),
        out_shape=[jax.ShapeDtypeStruct((s_len, FOX_W), F32), jax.ShapeDtypeStruct((FOX_H, s_len, 1), F32)],
        in_specs=[
            pl.BlockSpec((blk, FOX_DH), lambda h, i: (i, h)),
            pl.BlockSpec((s_len, FOX_DH), lambda h, i: (0, FOX_H + h)),
            pl.BlockSpec((s_len, FOX_DH), lambda h, i: (0, 2 * FOX_H + h)),
            pl.BlockSpec((None, 1, s_len), lambda h, i: (h, 0, 0)),
        ],
        out_specs=[
            pl.BlockSpec((blk, FOX_DH), lambda h, i: (i, h)),
            pl.BlockSpec((None, blk, 1), lambda h, i: (h, i, 0)),
        ],
        scratch_shapes=[pltpu.VMEM((blk, 128), F32), pltpu.VMEM((blk, 2 * FOX_DH), F32), pltpu.VMEM((blk, s_len), F32)],
        compiler_params=_cp("parallel", "arbitrary"),
    )(qkv, qkv, qkv, cum_row)


def _fox_bwd(qkv, cum_col, lse_row, delta_row, do):
    s_len = qkv.shape[0]
    blk = min(ATT_BLK, s_len)
    nb = s_len // blk

    def body(q_ref, k_ref, v_ref, c_ref, lse_ref, dl_ref, do_ref, dq_ref, dk_ref, dv_ref, dc_ref, dr_ref, dk_s, dv_s, dc_s, cb_s, dq_s):
        j = pl.program_id(1)

        @pl.when(j == 0)
        def _():
            dq_s[...] = jnp.zeros(dq_s.shape, F32)
            dr_ref[...] = jnp.zeros(dr_ref.shape, F32)

        dk_s[...] = jnp.zeros(dk_s.shape, F32)
        dv_s[...] = jnp.zeros(dv_s.shape, F32)
        dc_s[...] = jnp.zeros(dc_s.shape, F32)
        cb_s[...] = jnp.broadcast_to(c_ref[...], cb_s.shape)

        def tile(i, n, diag):
            rows = pl.ds(pl.multiple_of(i * blk, blk), n * blk)
            q, dob = q_ref[rows, :], do_ref[rows, :]
            k, v = k_ref[...], v_ref[...]
            s_t = lax.dot_general(k, q, _NT, preferred_element_type=F32) * FOX_SCALE - cb_s[:, : n * blk]
            p_t = jnp.exp(s_t - lse_ref[:, rows])
            if diag:
                p_t = jnp.where(_diag_mask(blk, transposed=True), p_t, 0.0)
            dp_t = lax.dot_general(v, dob, _NT, preferred_element_type=F32)
            ds_t = p_t * (dp_t - dl_ref[:, rows])
            dsb = ds_t.astype(BF16)
            dv_s[...] += jnp.dot(p_t.astype(BF16), dob, preferred_element_type=F32)
            dk_s[...] += jnp.dot(dsb, q, preferred_element_type=F32)
            dq_c = lax.dot_general(dsb, k, _TN, preferred_element_type=F32)
            part = ds_t[:, 0:128]
            for t in range(1, n * blk // 128):
                part = part + ds_t[:, t * 128 : (t + 1) * 128]
            dc_s[...] += part
            dr_ref[:, rows] += jnp.sum(ds_t, axis=0, keepdims=True)
            if diag:
                dq_s[rows, :] = (dq_s[rows, :] + dq_c) * FOX_SCALE
            else:
                dq_s[rows, :] += dq_c

        tile(j, 1, True)
        below = nb - 1 - j
        b0, b1, b2 = below % 2, (below // 2) % 2, (below // 4) % 2

        @pl.when(b0 == 1)
        def _():
            tile(j + 1, 1, False)

        @pl.when(b1 == 1)
        def _():
            tile(j + 1 + b0, 2, False)

        @pl.when(b2 == 1)
        def _():
            tile(j + 1 + b0 + 2 * b1, 2, False)
            tile(j + 3 + b0 + 2 * b1, 2, False)

        first = j + 1 + b0 + 2 * b1 + 4 * b2

        def four_pairs(t, carry):
            for u in range(4):
                tile(first + 8 * t + 2 * u, 2, False)
            return carry

        lax.fori_loop(0, below // 8, four_pairs, 0)
        dk_ref[...] = (dk_s[...] * FOX_SCALE).astype(BF16)
        dv_ref[...] = dv_s[...].astype(BF16)
        dc_ref[...] = jnp.sum(dc_s[...], axis=1, keepdims=True)

        @pl.when(j == nb - 1)
        def _():
            dq_ref[...] = dq_s[...].astype(BF16)

    head = lambda h, j: (0, h)
    row = pl.BlockSpec((None, 1, s_len), lambda h, j: (h, 0, 0))
    return _pcall(
        body,
        name="fox_bwd",
        grid=(FOX_H, nb),
        out_shape=[
            jax.ShapeDtypeStruct((s_len, FOX_W), BF16),
            jax.ShapeDtypeStruct((s_len, FOX_W), BF16),
            jax.ShapeDtypeStruct((s_len, FOX_W), BF16),
            jax.ShapeDtypeStruct((FOX_H, s_len, 1), F32),
            jax.ShapeDtypeStruct((FOX_H, 1, s_len), F32),
        ],
        in_specs=[
            pl.BlockSpec((s_len, FOX_DH), head),
            pl.BlockSpec((blk, FOX_DH), lambda h, j: (j, FOX_H + h)),
            pl.BlockSpec((blk, FOX_DH), lambda h, j: (j, 2 * FOX_H + h)),
            pl.BlockSpec((None, blk, 1), lambda h, j: (h, j, 0)),
            row,
            row,
            pl.BlockSpec((s_len, FOX_DH), head),
        ],
        out_specs=[
            pl.BlockSpec((s_len, FOX_DH), head),
            pl.BlockSpec((blk, FOX_DH), lambda h, j: (j, h)),
            pl.BlockSpec((blk, FOX_DH), lambda h, j: (j, h)),
            pl.BlockSpec((None, blk, 1), lambda h, j: (h, j, 0)),
            row,
        ],
        scratch_shapes=[
            pltpu.VMEM((blk, FOX_DH), F32),
            pltpu.VMEM((blk, FOX_DH), F32),
            pltpu.VMEM((blk, 128), F32),
            pltpu.VMEM((blk, 2 * blk), F32),
            pltpu.VMEM((s_len, FOX_DH), F32),
        ],
        compiler_params=_cp("parallel", "arbitrary"),
    )(qkv, qkv, qkv, cum_col, lse_row, delta_row, do)


def _swa_bias():
    cols = SWA_G * WINDOW
    k = np.arange(2 * WINDOW)[:, None]
    q = np.arange(cols)[None, :]
    dist = (q % WINDOW) - k + WINDOW
    valid = (dist >= 0) & (dist < WINDOW)
    out = np.empty((2, SWA_HKV, 2 * WINDOW, cols), np.float32)
    for g in range(SWA_HKV):
        slope = np.array([SLOPES[g * SWA_G + t] for t in range(SWA_G)], np.float32)[q // WINDOW]
        bias = -(slope * dist.astype(np.float32))
        out[0, g] = np.where(valid & (k >= WINDOW), bias, np.float32(NEG))
        out[1, g] = np.where(valid, bias, np.float32(NEG))
    return jnp.asarray(out)


def _swa_group(i, q_ref, kk, sinks_ref, bias_ref, g):
    cols = SWA_G * WINDOW
    head = lax.broadcasted_iota(jnp.int32, (1, cols), 1) // WINDOW
    sink = jnp.zeros((1, cols), F32)
    for t in range(SWA_G):
        sink = jnp.where(head == t, sinks_ref[g * SWA_G + t], sink)
    q = jnp.concatenate([q_ref[:, (g * SWA_G + t) * SWA_DH : (g * SWA_G + t + 1) * SWA_DH] for t in range(SWA_G)], axis=0)
    k = kk[:, g * SWA_DH : (g + 1) * SWA_DH]
    s = lax.dot_general(k, q, _NT, preferred_element_type=F32) * SWA_SCALE + bias_ref[jnp.minimum(i, 1), g]
    m = jnp.maximum(jnp.max(s, axis=0, keepdims=True), sink)
    e = jnp.exp(s - m)
    e_sink = jnp.exp(sink - m)
    inv = 1.0 / (jnp.sum(e, axis=0, keepdims=True) + e_sink)
    return q, k, e * inv, e_sink * inv


def _swa_specs(col_q, col_k, col_v, rev, nb):
    def blk(t):
        return nb - 1 - t if rev else t

    return [
        pl.BlockSpec((WINDOW, SWA_W), lambda t: (blk(t), col_q)),
        pl.BlockSpec((WINDOW, SWA_KVW), lambda t: (jnp.maximum(blk(t) - 1, 0), col_k)),
        pl.BlockSpec((WINDOW, SWA_KVW), lambda t: (blk(t), col_k)),
        pl.BlockSpec((WINDOW, SWA_KVW), lambda t: (jnp.maximum(blk(t) - 1, 0), col_v)),
        pl.BlockSpec((WINDOW, SWA_KVW), lambda t: (blk(t), col_v)),
    ]


def _swa_fwd(qkv, sinks):
    s_len = qkv.shape[0]
    nb = s_len // WINDOW
    bias_spec = pl.BlockSpec((2, SWA_HKV, 2 * WINDOW, SWA_G * WINDOW), lambda t: (0, 0, 0, 0))

    def body(q_ref, kp_ref, kc_ref, vp_ref, vc_ref, sinks_ref, bias_ref, o_ref):
        i = pl.program_id(0)
        kk = jnp.concatenate([kp_ref[...], kc_ref[...]], axis=0)
        vv = jnp.concatenate([vp_ref[...], vc_ref[...]], axis=0)
        for g in range(SWA_HKV):
            _, _, p, _ = _swa_group(i, q_ref, kk, sinks_ref, bias_ref, g)
            o = lax.dot_general(p.astype(BF16), vv[:, g * SWA_DH : (g + 1) * SWA_DH], _TN, preferred_element_type=F32)
            for t in range(SWA_G):
                h = g * SWA_G + t
                o_ref[:, h * SWA_DH : (h + 1) * SWA_DH] = o[t * WINDOW : (t + 1) * WINDOW, :]

    return _pcall(
        body,
        name="swa_fwd",
        grid=(nb,),
        out_shape=jax.ShapeDtypeStruct((s_len, SWA_W), F32),
        in_specs=_swa_specs(0, 4, 5, False, nb) + [pl.BlockSpec(memory_space=pltpu.SMEM), bias_spec],
        out_specs=pl.BlockSpec((WINDOW, SWA_W), lambda t: (t, 0)),
        compiler_params=_cp("parallel"),
    )(qkv, qkv, qkv, qkv, qkv, sinks, _swa_bias())


def _swa_bwd(qkv, sinks, do):
    s_len = qkv.shape[0]
    nb = s_len // WINDOW
    bias_spec = pl.BlockSpec((2, SWA_HKV, 2 * WINDOW, SWA_G * WINDOW), lambda t: (0, 0, 0, 0))

    def body(q_ref, kp_ref, kc_ref, vp_ref, vc_ref, sinks_ref, bias_ref, do_ref, dq_ref, dk_ref, dv_ref, dsink_ref, ck_s, cv_s, dkk_s, dvv_s):
        t = pl.program_id(0)
        i = nb - 1 - t

        @pl.when(t == 0)
        def _():
            ck_s[...] = jnp.zeros(ck_s.shape, F32)
            cv_s[...] = jnp.zeros(cv_s.shape, F32)
            dsink_ref[...] = jnp.zeros(dsink_ref.shape, F32)

        kk = jnp.concatenate([kp_ref[...], kc_ref[...]], axis=0)
        vv = jnp.concatenate([vp_ref[...], vc_ref[...]], axis=0)
        lane = lax.broadcasted_iota(jnp.int32, (1, 128), 1)
        dsink = jnp.zeros((1, 128), F32)
        for g in range(SWA_HKV):
            cols = slice(g * SWA_DH, (g + 1) * SWA_DH)
            q, k, p, p_sink = _swa_group(i, q_ref, kk, sinks_ref, bias_ref, g)
            dob = jnp.concatenate([do_ref[:, (g * SWA_G + t) * SWA_DH : (g * SWA_G + t + 1) * SWA_DH] for t in range(SWA_G)], axis=0)
            dp = lax.dot_general(vv[:, cols], dob, _NT, preferred_element_type=F32)
            delta = jnp.sum(p * dp, axis=0, keepdims=True)
            dsb = (p * (dp - delta)).astype(BF16)
            dq = (lax.dot_general(dsb, k, _TN, preferred_element_type=F32) * SWA_SCALE).astype(BF16)
            ps_d = p_sink * delta
            for t in range(SWA_G):
                h = g * SWA_G + t
                dq_ref[:, h * SWA_DH : (h + 1) * SWA_DH] = dq[t * WINDOW : (t + 1) * WINDOW, :]
                dsink = dsink + jnp.where(lane == h, -jnp.sum(ps_d[:, t * WINDOW : (t + 1) * WINDOW], axis=1, keepdims=True), 0.0)
            dkk_s[:, cols] = jnp.dot(dsb, q, preferred_element_type=F32) * SWA_SCALE
            dvv_s[:, cols] = jnp.dot(p.astype(BF16), dob, preferred_element_type=F32)
        dk_ref[...] = (dkk_s[WINDOW:, :] + ck_s[...]).astype(BF16)
        dv_ref[...] = (dvv_s[WINDOW:, :] + cv_s[...]).astype(BF16)
        ck_s[...] = dkk_s[:WINDOW, :]
        cv_s[...] = dvv_s[:WINDOW, :]
        dsink_ref[...] += dsink

    row = lambda t: (nb - 1 - t, 0)
    return _pcall(
        body,
        name="swa_bwd",
        grid=(nb,),
        out_shape=[
            jax.ShapeDtypeStruct((s_len, SWA_W), BF16),
            jax.ShapeDtypeStruct((s_len, SWA_KVW), BF16),
            jax.ShapeDtypeStruct((s_len, SWA_KVW), BF16),
            jax.ShapeDtypeStruct((1, 128), F32),
        ],
        in_specs=_swa_specs(0, 4, 5, True, nb)
        + [pl.BlockSpec(memory_space=pltpu.SMEM), bias_spec, pl.BlockSpec((WINDOW, SWA_W), row)],
        out_specs=[
            pl.BlockSpec((WINDOW, SWA_W), row),
            pl.BlockSpec((WINDOW, SWA_KVW), row),
            pl.BlockSpec((WINDOW, SWA_KVW), row),
            pl.BlockSpec((1, 128), lambda t: (0, 0)),
        ],
        scratch_shapes=[
            pltpu.VMEM((WINDOW, SWA_KVW), F32),
            pltpu.VMEM((WINDOW, SWA_KVW), F32),
            pltpu.VMEM((2 * WINDOW, SWA_KVW), F32),
            pltpu.VMEM((2 * WINDOW, SWA_KVW), F32),
        ],
        compiler_params=_cp("arbitrary"),
    )(qkv, qkv, qkv, qkv, qkv, sinks, _swa_bias(), do)


def _branch_fwd(o, gates, g_blk, w_b, name):
    s_len, wd = o.shape
    d = w_b.shape[1]
    tm = min(512, s_len)

    def body(o_ref, g_ref, w_ref, y_ref, a_ref):
        g = g_ref[...].astype(F32)
        a = (o_ref[...] * (g * _sigmoid(g))).astype(BF16)
        a_ref[...] = a
        y_ref[...] = jnp.dot(a, w_ref[...], preferred_element_type=F32).astype(BF16)

    return _pcall(
        body,
        name=name,
        grid=(s_len // tm,),
        out_shape=[jax.ShapeDtypeStruct((s_len, d), BF16), jax.ShapeDtypeStruct((s_len, wd), BF16)],
        in_specs=[
            pl.BlockSpec((tm, wd), lambda i: (i, 0)),
            pl.BlockSpec((tm, wd), lambda i: (i, g_blk)),
            pl.BlockSpec((wd, d), lambda i: (0, 0)),
        ],
        out_specs=[pl.BlockSpec((tm, d), lambda i: (i, 0)), pl.BlockSpec((tm, wd), lambda i: (i, 0))],
        compiler_params=_cp("parallel"),
    )(o, gates, w_b)


def _out_stage(gates, mf_blk, y_fox, y_swa, w_out, x, ada, ln_g, ln_b, target):
    s_len, d = x.shape
    tm = min(256, s_len)
    n_steps = s_len // tm

    def body(mf_ref, ms_ref, yf_ref, ys_ref, w_ref, x_ref, gate_ref, lg_ref, lb_ref, t_ref, mg_ref, dza_ref, dsub_ref, red_ref):
        i = pl.program_id(0)
        merged = _sigmoid(mf_ref[...].astype(F32)) * yf_ref[...].astype(F32) + _sigmoid(ms_ref[...].astype(F32)) * ys_ref[...].astype(F32)
        mb = merged.astype(BF16)
        mg_ref[...] = mb
        sub = jnp.dot(mb, w_ref[...], preferred_element_type=F32)
        gate = gate_ref[...]
        z = ALPHA * x_ref[...] + gate * sub
        mu = jnp.mean(z, axis=-1, keepdims=True)
        zc = z - mu
        var = jnp.mean(zc * zc, axis=-1, keepdims=True)
        rstd = lax.rsqrt(var + LN_EPS)
        zhat = zc * rstd
        err = zhat * lg_ref[...] + lb_ref[...] - t_ref[...]
        dout = err * (1.0 / d)
        dzhat = dout * lg_ref[...]
        dz = rstd * (dzhat - jnp.mean(dzhat, axis=-1, keepdims=True) - zhat * jnp.mean(dzhat * zhat, axis=-1, keepdims=True))
        dza_ref[...] = ALPHA * dz
        dsub_ref[...] = (gate * dz).astype(BF16)
        part = jnp.concatenate(
            [
                jnp.sum(dz * sub, axis=0, keepdims=True),
                jnp.sum(dout * zhat, axis=0, keepdims=True),
                jnp.sum(dout, axis=0, keepdims=True),
                jnp.sum(err * err, axis=0, keepdims=True),
                jnp.zeros((4, d), F32),
            ],
            axis=0,
        )

        @pl.when(i == 0)
        def _():
            red_ref[...] = part

        @pl.when(i > 0)
        def _():
            red_ref[...] += part

        @pl.when(i == n_steps - 1)
        def _():
            red_ref[4:5, :] = jnp.broadcast_to(jnp.sum(red_ref[3:4, :], axis=1, keepdims=True), (1, d))

    row = pl.BlockSpec((tm, d), lambda i: (i, 0))
    vec = pl.BlockSpec((1, d), lambda i: (0, 0))
    return _pcall(
        body,
        name="out_stage",
        grid=(n_steps,),
        out_shape=[
            jax.ShapeDtypeStruct((s_len, d), BF16),
            jax.ShapeDtypeStruct((s_len, d), F32),
            jax.ShapeDtypeStruct((s_len, d), BF16),
            jax.ShapeDtypeStruct((8, d), F32),
        ],
        in_specs=[
            pl.BlockSpec((tm, d), lambda i: (i, mf_blk)),
            pl.BlockSpec((tm, d), lambda i: (i, mf_blk + 1)),
            row,
            row,
            pl.BlockSpec((d, d), lambda i: (0, 0), pipeline_mode=pl.Buffered(1)),
            row,
            pl.BlockSpec((1, d), lambda i: (0, 2)),
            vec,
            vec,
            row,
        ],
        out_specs=[row, row, row, pl.BlockSpec((8, d), lambda i: (0, 0))],
        compiler_params=_cp("arbitrary"),
    )(gates, gates, y_fox, y_swa, w_out, x, ada, ln_g, ln_b, target)


def _merge_bwd(dsub, w_out, gates, mf_blk, y_fox, y_swa):
    s_len, d = dsub.shape
    tm = min(256, s_len)

    def body(ds_ref, w_ref, mf_ref, ms_ref, yf_ref, ys_ref, dmf_ref, dms_ref, dyf_ref, dys_ref):
        dm = lax.dot_general(ds_ref[...], w_ref[...], _NT, preferred_element_type=F32)
        sf, ss = _sigmoid(mf_ref[...].astype(F32)), _sigmoid(ms_ref[...].astype(F32))
        dmf_ref[...] = (dm * yf_ref[...].astype(F32) * (sf * (1.0 - sf))).astype(BF16)
        dms_ref[...] = (dm * ys_ref[...].astype(F32) * (ss * (1.0 - ss))).astype(BF16)
        dyf_ref[...] = (dm * sf).astype(BF16)
        dys_ref[...] = (dm * ss).astype(BF16)

    row = pl.BlockSpec((tm, d), lambda i: (i, 0))
    return _pcall(
        body,
        name="merge_bwd",
        grid=(s_len // tm,),
        out_shape=[jax.ShapeDtypeStruct((s_len, d), BF16)] * 4,
        in_specs=[
            row,
            pl.BlockSpec((d, d), lambda i: (0, 0), pipeline_mode=pl.Buffered(1)),
            pl.BlockSpec((tm, d), lambda i: (i, mf_blk)),
            pl.BlockSpec((tm, d), lambda i: (i, mf_blk + 1)),
            row,
            row,
        ],
        out_specs=[row] * 4,
        compiler_params=_cp("parallel"),
    )(dsub, w_out, gates, gates, y_fox, y_swa)


def _branch_bwd(dy, w_b, o, gates, g_blk, name, n_heads):
    s_len, d = dy.shape
    wd = w_b.shape[0]
    tm = min(512, s_len)

    def body(dy_ref, w_ref, o_ref, g_ref, do_ref, dg_ref, *rest):
        da = lax.dot_general(dy_ref[...], w_ref[...], _NT, preferred_element_type=F32)
        g = g_ref[...].astype(F32)
        sg = _sigmoid(g)
        do = da * (g * sg)
        do_ref[...] = do.astype(BF16)
        o = o_ref[...]
        dg_ref[...] = (da * o * (sg * (1.0 + g * (1.0 - sg)))).astype(BF16)
        if n_heads:
            prod = do.astype(BF16).astype(F32) * o
            lane = lax.broadcasted_iota(jnp.int32, (1, 128), 1)
            delta = jnp.zeros((tm, 128), F32)
            for h in range(n_heads):
                dh = jnp.sum(prod[:, h * 128 : (h + 1) * 128], axis=1, keepdims=True)
                delta = delta + jnp.where(lane == h, dh, 0.0)
            rest[0][...] = delta

    out_shape = [jax.ShapeDtypeStruct((s_len, wd), BF16), jax.ShapeDtypeStruct((s_len, wd), BF16)]
    out_specs = [pl.BlockSpec((tm, wd), lambda i: (i, 0))] * 2
    if n_heads:
        out_shape.append(jax.ShapeDtypeStruct((s_len, 128), F32))
        out_specs.append(pl.BlockSpec((tm, 128), lambda i: (i, 0)))
    return _pcall(
        body,
        name=name,
        grid=(s_len // tm,),
        out_shape=out_shape,
        in_specs=[
            pl.BlockSpec((tm, d), lambda i: (i, 0)),
            pl.BlockSpec((wd, d), lambda i: (0, 0)),
            pl.BlockSpec((tm, wd), lambda i: (i, 0)),
            pl.BlockSpec((tm, wd), lambda i: (i, g_blk)),
        ],
        out_specs=out_specs,
        compiler_params=_cp("parallel"),
    )(dy, w_b, o, gates)


def _in_bwd(dproj, w_in_t, x, ada, dza, ride):
    s_len, d = x.shape
    k_tot = dproj.shape[1]
    tm, tk, dn = min(512, s_len), k_tot // 4, d // 2
    ni, nk = s_len // tm, k_tot // tk
    n = len(ride)

    def body(dp_ref, w_ref, x_ref, sc_ref, dza_ref, *rest):
        ins, (gx_ref, red_ref), outs = rest[:n], rest[n : n + 2], rest[n + 2 : 2 * n + 2]
        sems, acc_s = rest[2 * n + 2 : 2 * n + 5], rest[2 * n + 5]
        i, nh, kk = pl.program_id(0), pl.program_id(1), pl.program_id(2)

        @pl.when((i == 0) & (nh == 0) & (kk == 0))
        def _():
            _rider_start("exchange", ins, outs, *sems)

        @pl.when((i == ni - 1) & (nh == 1) & (kk == nk - 1))
        def _():
            _rider_wait("exchange", ins, outs, *sems)

        part = jnp.dot(dp_ref[...], w_ref[...], preferred_element_type=F32)
        half = pl.ds(pl.multiple_of(nh * dn, dn), dn)

        @pl.when(kk == 0)
        def _():
            acc_s[:, half] = part

        @pl.when(kk > 0)
        def _():
            acc_s[:, half] += part

        @pl.when((nh == 1) & (kk == nk - 1))
        def _():
            dh = acc_s[...]
            xv = x_ref[...]
            mu = jnp.mean(xv, axis=-1, keepdims=True)
            xc = xv - mu
            var = jnp.mean(xc * xc, axis=-1, keepdims=True)
            rstd = lax.rsqrt(var + LN_EPS)
            xhat = xc * rstd
            dxhat = dh * (1.0 + sc_ref[...])
            dx = rstd * (dxhat - jnp.mean(dxhat, axis=-1, keepdims=True) - xhat * jnp.mean(dxhat * xhat, axis=-1, keepdims=True))
            gx_ref[...] = dza_ref[...] + dx
            part_r = jnp.concatenate(
                [jnp.sum(dh, axis=0, keepdims=True), jnp.sum(dh * xhat, axis=0, keepdims=True), jnp.zeros((6, d), F32)], axis=0
            )

            @pl.when(i == 0)
            def _():
                red_ref[...] = part_r

            @pl.when(i > 0)
            def _():
                red_ref[...] += part_r

    row = pl.BlockSpec((tm, d), lambda i, nh, kk: (i, 0))
    hbm = pl.BlockSpec(memory_space=pltpu.HBM)
    return _pcall(
        body,
        name="in_bwd",
        grid=(ni, 2, nk),
        out_shape=[jax.ShapeDtypeStruct((s_len, d), F32), jax.ShapeDtypeStruct((8, d), F32)]
        + [jax.ShapeDtypeStruct(r.shape, r.dtype) for r in ride],
        in_specs=[
            pl.BlockSpec((tm, tk), lambda i, nh, kk: (i, kk)),
            pl.BlockSpec((tk, dn), lambda i, nh, kk: (kk, nh)),
            row,
            pl.BlockSpec((1, d), lambda i, nh, kk: (0, 1)),
            row,
        ]
        + [hbm] * n,
        out_specs=[row, pl.BlockSpec((8, d), lambda i, nh, kk: (0, 0))] + [hbm] * n,
        scratch_shapes=_rider_scratch(n) + [pltpu.VMEM((tm, d), F32)],
        compiler_params=_cp("arbitrary", "arbitrary", "arbitrary"),
    )(dproj, w_in_t, x, ada, dza, *ride)


def _pad_lanes(v, n):
    return jnp.pad(v, ((0, 0), (0, n - v.shape[1])))


def kernel(x, c, w_ada, b_ada, w_in, b_f, attn_sinks, w_br_fox, w_br_swa, w_out, ln_g, ln_b, loss_target, m_w_ada, m_b_ada, m_w_in, m_b_f, m_attn_sinks, m_w_br_fox, m_w_br_swa, m_w_out, m_ln_g, m_ln_b, v_w_ada, v_b_ada, v_w_in, v_b_f, v_attn_sinks, v_w_br_fox, v_w_br_swa, v_w_out, v_ln_g, v_ln_b):
    x2, tgt = x[0], loss_target[0]
    s_len, d = x2.shape
    me = 4 * lax.axis_index("x") + 2 * lax.axis_index("y") + lax.axis_index("c")
    off_ms = OFF_MF + d
    in_pad = off_ms + d
    c_ada = w_ada.shape[2]
    c_in = w_in.shape[2]
    c_br = w_br_fox.shape[2]

    w_in_full = _all_gather(w_in[0].T.astype(BF16), "ag_w_in", pltpu.HBM).reshape(N_DEV * c_in, d)
    w_in_pad = jnp.concatenate(
        [w_in_full[:REAL_FLOG_END], jnp.zeros((FLOG_PAD - N_FLOG, d), BF16), w_in_full[REAL_FLOG_END:]], axis=0
    )
    k_cut = REAL_FLOG_END // c_in

    c_all = _gather_rows(c, "ag_c")
    b_cols = lax.dynamic_slice(b_ada, (0, me * c_ada), (1, c_ada))
    ada_cols = _ada_fwd(c_all, w_ada[0], b_cols)
    ada_g = _all_gather(ada_cols, "ag_ada", pltpu.VMEM)
    ada = lax.dynamic_index_in_dim(ada_g, me, axis=1, keepdims=False).reshape(1, N_DEV * c_ada)

    h = _ln_mod(x2, ada)
    qkv_fox = _mm_cols(h, w_in_pad, OFF_FQ, 3 * FOX_W, BF16, "proj_fox")
    flog = _mm_cols(h, w_in_pad, OFF_FLOG, 128, F32, "proj_flog")
    qkv_swa = _mm_cols(h, w_in_pad, OFF_SQ, SWA_W + 2 * SWA_KVW, BF16, "proj_swa")
    gates, w_bf, w_bs, w_o = _mm_cols(
        h, w_in_pad, OFF_GF, in_pad - OFF_GF, BF16, "proj_gates",
        ride=(w_br_fox[0].astype(BF16), w_br_swa[0].astype(BF16), w_out[0].astype(BF16)),
    )
    w_bf = w_bf.reshape(N_DEV, FOX_W, c_br).transpose(1, 0, 2).reshape(FOX_W, d)
    w_bs = w_bs.reshape(N_DEV, SWA_W, c_br).transpose(1, 0, 2).reshape(SWA_W, d)
    w_o = w_o.reshape(d, d)
    mf_blk = (OFF_MF - OFF_GF) // d

    flog_t = flog[:, :N_FLOG].T
    bf_col = b_f.reshape(FOX_H, 1)
    cum = _fox_cum(flog_t, bf_col)
    cum_row = cum.reshape(FOX_H, 1, s_len)
    o_fox, lse = _fox_fwd(qkv_fox, cum_row)
    sinks = attn_sinks.reshape(SWA_HQ)
    o_swa = _swa_fwd(qkv_swa, sinks)

    y_fox, a_fox = _branch_fwd(o_fox, gates, 0, w_bf, "branch_fox")
    y_swa, a_swa = _branch_fwd(o_swa, gates, 1, w_bs, "branch_swa")
    merged, dza, dsub, red = _out_stage(gates, mf_blk, y_fox, y_swa, w_o, x2, ada, ln_g, ln_b, tgt)
    loss = lax.psum(0.5 * red[4, 0] / d, ("x", "y", "c"))

    dmf, dms, dy_fox, dy_swa = _merge_bwd(dsub, w_o, gates, mf_blk, y_fox, y_swa)
    do_fox, dg_fox, delta = _branch_bwd(dy_fox, w_bf, o_fox, gates, 0, "branch_fox_bwd", FOX_H)
    do_swa, dg_swa = _branch_bwd(dy_swa, w_bs, o_swa, gates, 1, "branch_swa_bwd", 0)
    delta_row = delta[:, :FOX_H].T.reshape(FOX_H, 1, s_len)
    dq_f, dk_f, dv_f, dcol, drow = _fox_bwd(
        qkv_fox, cum.reshape(FOX_H, s_len, 1), lse.reshape(FOX_H, 1, s_len), delta_row, do_fox
    )
    dflog_t, dbf = _fox_gate_bwd(drow.reshape(FOX_H, s_len), dcol.reshape(FOX_H, s_len), flog_t, bf_col)
    dq_s, dk_s, dv_s, dsink = _swa_bwd(qkv_swa, sinks, do_swa)
    dflog = _pad_lanes(dflog_t.T, FLOG_PAD).astype(BF16)
    dproj = jnp.concatenate([dq_f, dk_f, dv_f, dflog, dq_s, dk_s, dv_s, dg_fox, dg_swa, dmf, dms], axis=1)
    g_w_bf = _mm_tn(a_fox, dy_fox, "grad_w_br_fox")
    g_w_bs = _mm_tn(a_swa, dy_swa, "grad_w_br_swa")
    g_w_o = _mm_tn(merged, dsub, "grad_w_out")
    g_w_in, r_bf, r_bs, r_o = _mm_tn(
        dproj, h, "grad_w_in",
        ride=(
            g_w_bf.reshape(FOX_W, N_DEV, c_br).transpose(1, 0, 2),
            g_w_bs.reshape(SWA_W, N_DEV, c_br).transpose(1, 0, 2),
            g_w_o.reshape(N_DEV, d // N_DEV, d),
        ),
    )
    pad = FLOG_PAD - N_FLOG
    g_blocks = jnp.stack(
        [g_w_in[k * c_in : (k + 1) * c_in] for k in range(k_cut)]
        + [jnp.concatenate([g_w_in[k_cut * c_in : REAL_FLOG_END], g_w_in[OFF_SQ : (k_cut + 1) * c_in + pad]], axis=0)]
        + [g_w_in[k * c_in + pad : (k + 1) * c_in + pad] for k in range(k_cut + 1, N_DEV)]
    )

    grad_x, red2, r_in = _in_bwd(dproj, w_in_pad, x2, ada, dza, ride=(g_blocks,))
    out_w_in = _sum_adam_t(r_in, w_in[0].T, m_w_in[0].T, v_w_in[0].T, "adam_w_in")
    out_w_in = [o.T for o in out_w_in]
    out_w_bf = _sum_adam(r_bf, w_br_fox[0], m_w_br_fox[0], v_w_br_fox[0], "adam_w_br_fox")
    out_w_bs = _sum_adam(r_bs, w_br_swa[0], m_w_br_swa[0], v_w_br_swa[0], "adam_w_br_swa")
    out_w_o = _sum_adam(r_o, w_out[0], m_w_out[0], v_w_out[0], "adam_w_out")

    packed = jnp.concatenate([red2[0:1], red2[1:2], red[0:1], _pad_lanes(dbf[:, 0].reshape(1, FOX_H), 128), dsink, red[1:2], red[2:3]], axis=1)
    gathered = _gather_rows(packed, "ag_small")
    pack = lambda a, b, cc, dd, e: jnp.concatenate([a, _pad_lanes(b, 128), _pad_lanes(cc, 128), dd, e], axis=1)
    small = _small_adam(
        gathered,
        pack(b_ada, b_f, attn_sinks, ln_g, ln_b),
        pack(m_b_ada, m_b_f, m_attn_sinks, m_ln_g, m_ln_b),
        pack(v_b_ada, v_b_f, v_attn_sinks, v_ln_g, v_ln_b),
    )
    dada_cols = lax.dynamic_slice(gathered, (0, me * c_ada), (N_DEV, c_ada))
    out_w_ada = _wada_adam(c_all.T, dada_cols, w_ada[0], m_w_ada[0], v_w_ada[0])

    o1, o2, o3 = 3 * d, 3 * d + 128, 3 * d + 256

    def unpack(p):
        return p[:, :o1], p[:, o1 : o1 + FOX_H], p[:, o2 : o2 + SWA_HQ], p[:, o3 : o3 + d], p[:, o3 + d : o3 + 2 * d]

    kinds = []
    for k in range(4):
        b_ada_k, b_f_k, sinks_k, ln_g_k, ln_b_k = unpack(small[k])
        kinds.append(
            [out_w_ada[k][None], b_ada_k, out_w_in[k][None], b_f_k, sinks_k, out_w_bf[k][None], out_w_bs[k][None], out_w_o[k][None], ln_g_k, ln_b_k]
        )
    return (loss, grad_x[None], *kinds[0], *kinds[1], *kinds[2], *kinds[3])
```

```python
import numpy as np
import jax
import jax.numpy as jnp
from jax import lax
from jax.experimental import pallas as pl
from jax.experimental.pallas import tpu as pltpu

F32 = jnp.float32
BF16 = jnp.bfloat16
N_DEV = 8
MESH = pl.DeviceIdType.MESH

FOX_H, FOX_DH, FOX_W = 8, 128, 1024
SWA_HQ, SWA_HKV, SWA_DH, SWA_G = 16, 4, 64, 4
SWA_W, SWA_KVW, WINDOW = 1024, 256, 128
LN_EPS = 1e-5
NEG = -1e30
DEPTH = 1
ALPHA = (2.0 * DEPTH) ** 0.25
FOX_SCALE = FOX_DH ** -0.5
SWA_SCALE = SWA_DH ** -0.5
SLOPES = [2.0 ** (-8.0 * (h + 1.0) / SWA_HQ) for h in range(SWA_HQ)]

ADAM_LR, ADAM_B1, ADAM_B2, ADAM_EPS, ADAM_WD, ADAM_STEP = 0.001, 0.9, 0.999, 1e-08, 0.01, 10

N_FLOG = 8
FLOG_PAD = 512
OFF_FQ, OFF_FK, OFF_FV, OFF_FLOG = 0, 1024, 2048, 3072
OFF_SQ = OFF_FLOG + FLOG_PAD
OFF_SK = OFF_SQ + SWA_W
OFF_SV = OFF_SK + SWA_KVW
OFF_GF = OFF_SV + SWA_KVW
OFF_GS = OFF_GF + FOX_W
OFF_MF = OFF_GS + SWA_W
REAL_FLOG_END = OFF_FLOG + N_FLOG

ATT_BLK = 512
VMEM_LIMIT = 58 * 1024 * 1024


def _pcall(body, **kw):
    return pl.pallas_call(body, **kw)


def _cp(*sem):
    return pltpu.CompilerParams(dimension_semantics=sem, vmem_limit_bytes=VMEM_LIMIT)


def _sigmoid(x):
    return 0.5 * jnp.tanh(0.5 * x) + 0.5


def _all_gather(x, name, space):
    m_per, n = x.shape

    def body(x_ref, out_ref, send_sems, recv_sems, local_sem):
        mx, my, mc = lax.axis_index("x"), lax.axis_index("y"), lax.axis_index("c")
        me, sibling = (mx, my, mc), (mx, my, 1 - mc)
        xn, yn, dg = (1 - mx, my), (mx, 1 - my), (1 - mx, 1 - my)
        south = mc == 0
        src_chip = (jnp.where(south, 1 - mx, mx), jnp.where(south, my, 1 - my))
        dst_chip = (jnp.where(south, mx, 1 - mx), jnp.where(south, 1 - my, my))

        def rows(px, py, pc):
            return out_ref.at[4 * px + 2 * py + pc]

        def copy(k, block, to, src=None):
            return pltpu.make_async_remote_copy(
                src_ref=rows(*block) if src is None else src,
                dst_ref=rows(*block),
                send_sem=send_sems.at[k],
                recv_sem=recv_sems.at[k],
                device_id=to,
                device_id_type=MESH,
            )

        mine = pltpu.make_async_copy(x_ref, rows(*me), local_sem)
        mine.start()
        first = [copy(0, me, sibling, src=x_ref), copy(1, me, (*xn, mc), src=x_ref), copy(2, me, (*yn, mc), src=x_ref)]
        for cp in first:
            cp.start()
        copy(1, (*xn, mc), me).wait_recv()
        copy(2, (*yn, mc), me).wait_recv()
        later = [copy(3, (*src_chip, mc), (*dst_chip, mc)), copy(4, (*xn, mc), sibling), copy(5, (*yn, mc), sibling)]
        for cp in later:
            cp.start()
        copy(3, (*dg, mc), me).wait_recv()
        last = copy(6, (*dg, mc), sibling)
        last.start()
        copy(0, sibling, me).wait_recv()
        for k, chip in ((4, xn), (5, yn), (6, dg)):
            copy(k, (*chip, 1 - mc), me).wait_recv()
        for cp in first + later + [last]:
            cp.wait_send()
        mine.wait()

    return _pcall(
        body,
        name=name,
        out_shape=jax.ShapeDtypeStruct((N_DEV, m_per, n), x.dtype),
        in_specs=[pl.BlockSpec(memory_space=space)],
        out_specs=pl.BlockSpec(memory_space=space),
        scratch_shapes=[pltpu.SemaphoreType.DMA((7,)), pltpu.SemaphoreType.DMA((7,)), pltpu.SemaphoreType.DMA],
    )(x)


def _peer(d, mx, my, mc):
    return (1 - mx if (d >> 2) & 1 else mx, 1 - my if (d >> 1) & 1 else my, 1 - mc if d & 1 else mc)


def _rider_copies(kind, ins, outs, send_sems, recv_sems, local_sems):
    mx, my, mc = lax.axis_index("x"), lax.axis_index("y"), lax.axis_index("c")
    me = 4 * mx + 2 * my + mc
    remote, local = [], []
    for a in range(len(ins)):
        if kind == "gather":
            m_per = ins[a].shape[0]
            mine = outs[a].at[pl.ds(me * m_per, m_per), :]
            local.append(pltpu.make_async_copy(ins[a], mine, local_sems.at[a]))
        else:
            local.append(pltpu.make_async_copy(ins[a].at[me], outs[a].at[0], local_sems.at[a]))
        for d in range(1, N_DEV):
            px, py, pc = _peer(d, mx, my, mc)
            if kind == "gather":
                src, dst = ins[a], mine
            else:
                src, dst = ins[a].at[4 * px + 2 * py + pc], outs[a].at[d]
            remote.append(
                pltpu.make_async_remote_copy(
                    src_ref=src,
                    dst_ref=dst,
                    send_sem=send_sems.at[a * 7 + d - 1],
                    recv_sem=recv_sems.at[a * 7 + d - 1],
                    device_id=(px, py, pc),
                    device_id_type=MESH,
                )
            )
    return remote, local


def _rider_start(*args):
    remote, local = _rider_copies(*args)
    for cp in local + remote:
        cp.start()


def _rider_wait(*args):
    remote, local = _rider_copies(*args)
    for cp in remote:
        cp.wait_recv()
    for cp in remote:
        cp.wait_send()
    for cp in local:
        cp.wait()


def _rider_scratch(n):
    return [pltpu.SemaphoreType.DMA((7 * n,)), pltpu.SemaphoreType.DMA((7 * n,)), pltpu.SemaphoreType.DMA((n,))]


def _gather_rows(v, name):
    n = v.shape[1]
    return _all_gather(jnp.broadcast_to(v, (8, n)), name, pltpu.VMEM)[:, 0, :]


def _adamw(w, g, m, v):
    m = ADAM_B1 * m + (1.0 - ADAM_B1) * g
    v = ADAM_B2 * v + (1.0 - ADAM_B2) * (g * g)
    m_hat = m / (1.0 - ADAM_B1**ADAM_STEP)
    v_hat = v / (1.0 - ADAM_B2**ADAM_STEP)
    delta = -ADAM_LR * (m_hat / (jnp.sqrt(v_hat) + ADAM_EPS) + ADAM_WD * w)
    return delta, m, v


def _sum_adam(recv, w, m, v, name):
    _, r_tot, c = recv.shape
    c_pad = -(-c // 128) * 128
    tr = r_tot
    while 8 * tr * c_pad * 4 > 6 * 1024 * 1024 and tr % 32 == 0:
        tr //= 2

    def body(r_ref, w_ref, m_ref, v_ref, g_ref, d_ref, nm_ref, nv_ref):
        g = r_ref[0].astype(F32)
        for k in range(1, N_DEV):
            g = g + r_ref[k].astype(F32)
        d, nm, nv = _adamw(w_ref[...], g, m_ref[...], v_ref[...])
        g_ref[...] = g
        d_ref[...] = d
        nm_ref[...] = nm
        nv_ref[...] = nv

    blk = pl.BlockSpec((tr, c), lambda i: (i, 0))
    return _pcall(
        body,
        name=name,
        grid=(r_tot // tr,),
        out_shape=[jax.ShapeDtypeStruct((r_tot, c), F32)] * 4,
        in_specs=[pl.BlockSpec((N_DEV, tr, c), lambda i: (0, i, 0)), blk, blk, blk],
        out_specs=[blk] * 4,
        compiler_params=_cp("parallel"),
    )(recv, w, m, v)


def _sum_adam_t(recv, w, m, v, name):
    _, c, r_tot = recv.shape
    tr = min(256, r_tot)

    def body(r_ref, w_ref, m_ref, v_ref, g_ref, d_ref, nm_ref, nv_ref):
        g = r_ref[0].astype(F32)
        for k in range(1, N_DEV):
            g = g + r_ref[k].astype(F32)
        d, nm, nv = _adamw(w_ref[...], g, m_ref[...], v_ref[...])
        g_ref[...] = g
        d_ref[...] = d
        nm_ref[...] = nm
        nv_ref[...] = nv

    blk = pl.BlockSpec((c, tr), lambda i: (0, i))
    return _pcall(
        body,
        name=name,
        grid=(r_tot // tr,),
        out_shape=[jax.ShapeDtypeStruct((c, r_tot), F32)] * 4,
        in_specs=[pl.BlockSpec((N_DEV, c, tr), lambda i: (0, 0, i)), blk, blk, blk],
        out_specs=[blk] * 4,
        compiler_params=_cp("parallel"),
    )(recv, w, m, v)


def _wada_adam(c_t, dada_cols, w, m, v):
    d_model, c = w.shape
    tr = min(256, d_model)

    def body(ct_ref, da_ref, w_ref, m_ref, v_ref, g_ref, d_ref, nm_ref, nv_ref):
        g = jnp.dot(ct_ref[...].astype(BF16), da_ref[...].astype(BF16), preferred_element_type=F32)
        d, nm, nv = _adamw(w_ref[...], g, m_ref[...], v_ref[...])
        g_ref[...] = g
        d_ref[...] = d
        nm_ref[...] = nm
        nv_ref[...] = nv

    blk = pl.BlockSpec((tr, c), lambda i: (i, 0))
    return _pcall(
        body,
        name="wada_adam",
        grid=(d_model // tr,),
        out_shape=[jax.ShapeDtypeStruct((d_model, c), F32)] * 4,
        in_specs=[pl.BlockSpec((tr, N_DEV), lambda i: (i, 0)), pl.BlockSpec((N_DEV, c), lambda i: (0, 0)), blk, blk, blk],
        out_specs=[blk] * 4,
        compiler_params=_cp("parallel"),
    )(c_t, dada_cols, w, m, v)


def _small_adam(gathered, w, m, v):
    p = w.shape[1]

    def body(a_ref, w_ref, m_ref, v_ref, g_ref, d_ref, nm_ref, nv_ref):
        g = a_ref[0:1, :]
        for k in range(1, N_DEV):
            g = g + a_ref[k : k + 1, :]
        d, nm, nv = _adamw(w_ref[...], g, m_ref[...], v_ref[...])
        g_ref[...] = g
        d_ref[...] = d
        nm_ref[...] = nm
        nv_ref[...] = nv

    return _pcall(
        body,
        name="small_adam",
        out_shape=[jax.ShapeDtypeStruct((1, p), F32)] * 4,
    )(gathered, w, m, v)


def _ada_fwd(c_all, w_ada, b_cols):
    c = w_ada.shape[1]

    def body(c_ref, w_ref, b_ref, o_ref):
        o_ref[...] = jnp.dot(c_ref[...].astype(BF16), w_ref[...].astype(BF16), preferred_element_type=F32) + b_ref[...]

    return _pcall(
        body,
        name="ada_fwd",
        out_shape=jax.ShapeDtypeStruct((N_DEV, c), F32),
        compiler_params=_cp(),
    )(c_all, w_ada, b_cols)


def _ln_mod(x, ada):
    s_len, d = x.shape
    tm = min(512, s_len)

    def body(x_ref, sh_ref, sc_ref, h_ref):
        xv = x_ref[...]
        mu = jnp.mean(xv, axis=-1, keepdims=True)
        xc = xv - mu
        var = jnp.mean(xc * xc, axis=-1, keepdims=True)
        xhat = xc * lax.rsqrt(var + LN_EPS)
        h_ref[...] = (xhat * (1.0 + sc_ref[...]) + sh_ref[...]).astype(BF16)

    return _pcall(
        body,
        name="ln_mod",
        grid=(s_len // tm,),
        out_shape=jax.ShapeDtypeStruct((s_len, d), BF16),
        in_specs=[
            pl.BlockSpec((tm, d), lambda i: (i, 0)),
            pl.BlockSpec((1, d), lambda i: (0, 0)),
            pl.BlockSpec((1, d), lambda i: (0, 1)),
        ],
        out_specs=pl.BlockSpec((tm, d), lambda i: (i, 0)),
        compiler_params=_cp("parallel"),
    )(x, ada, ada)


def _mm_cols(a, b, col_off, n_cols, out_dtype, name, ride=()):
    m, k = a.shape
    tm = min(1024, m)
    tn = next(t for t in (1024, 512, 128) if n_cols % t == 0 and col_off % t == 0)
    off = col_off // tn
    ni, nj = m // tm, n_cols // tn
    n = len(ride)

    def body(a_ref, b_ref, *rest):
        ins, o_ref, outs, sems = rest[:n], rest[n], rest[n + 1 : 2 * n + 1], rest[2 * n + 1 :]
        i, j = pl.program_id(0), pl.program_id(1)
        if n:

            @pl.when((i == 0) & (j == 0))
            def _():
                _rider_start("gather", ins, outs, *sems)

        o_ref[...] = lax.dot_general(a_ref[...], b_ref[...], _NT, preferred_element_type=F32).astype(out_dtype)
        if n:

            @pl.when((i == ni - 1) & (j == nj - 1))
            def _():
                _rider_wait("gather", ins, outs, *sems)

    hbm = pl.BlockSpec(memory_space=pltpu.HBM)
    out = _pcall(
        body,
        name=name,
        grid=(ni, nj),
        out_shape=[jax.ShapeDtypeStruct((m, n_cols), out_dtype)]
        + [jax.ShapeDtypeStruct((N_DEV * r.shape[0], r.shape[1]), r.dtype) for r in ride],
        in_specs=[pl.BlockSpec((tm, k), lambda i, j: (i, 0)), pl.BlockSpec((tn, k), lambda i, j: (off + j, 0))] + [hbm] * n,
        out_specs=[pl.BlockSpec((tm, tn), lambda i, j: (i, j))] + [hbm] * n,
        scratch_shapes=_rider_scratch(n) if n else [],
        compiler_params=_cp("arbitrary", "arbitrary") if n else _cp("parallel", "parallel"),
    )(a, b, *ride)
    return out if n else out[0]


def _mm_tn(a, b, name, ride=()):
    s_len, m = a.shape
    n = b.shape[1]
    tm, tn, ts = min(1024, m), min(2048, n), min(2048, s_len)
    ni, nj, ns = m // tm, n // tn, s_len // ts
    nr = len(ride)

    def body(a_ref, b_ref, *rest):
        ins, o_ref, outs = rest[:nr], rest[nr], rest[nr + 1 : 2 * nr + 1]
        sems, acc_s = rest[2 * nr + 1 : -1], rest[-1]
        i, j, kk = pl.program_id(0), pl.program_id(1), pl.program_id(2)
        if nr:

            @pl.when((i == 0) & (j == 0) & (kk == 0))
            def _():
                _rider_start("exchange", ins, outs, *sems)

            @pl.when((i == ni - 1) & (j == nj - 1) & (kk == ns - 1))
            def _():
                _rider_wait("exchange", ins, outs, *sems)

        part = lax.dot_general(a_ref[...], b_ref[...], _TN, preferred_element_type=F32)

        @pl.when(kk == 0)
        def _():
            acc_s[...] = part

        @pl.when(kk > 0)
        def _():
            acc_s[...] += part

        @pl.when(kk == ns - 1)
        def _():
            o_ref[...] = acc_s[...].astype(BF16)

    hbm = pl.BlockSpec(memory_space=pltpu.HBM)
    out = _pcall(
        body,
        name=name,
        grid=(ni, nj, ns),
        out_shape=[jax.ShapeDtypeStruct((m, n), BF16)] + [jax.ShapeDtypeStruct(r.shape, r.dtype) for r in ride],
        in_specs=[pl.BlockSpec((ts, tm), lambda i, j, kk: (kk, i)), pl.BlockSpec((ts, tn), lambda i, j, kk: (kk, j))] + [hbm] * nr,
        out_specs=[pl.BlockSpec((tm, tn), lambda i, j, kk: (i, j))] + [hbm] * nr,
        scratch_shapes=(_rider_scratch(nr) if nr else []) + [pltpu.VMEM((tm, tn), F32)],
        compiler_params=_cp("arbitrary", "arbitrary", "arbitrary") if nr else _cp("parallel", "parallel", "arbitrary"),
    )(a, b, *ride)
    return out if nr else out[0]


def _split3(a):
    hi = a.astype(BF16)
    r1 = a - hi.astype(F32)
    mid = r1.astype(BF16)
    lo = (r1 - mid.astype(F32)).astype(BF16)
    return hi, mid, lo


def _dot_ones(a, tri):
    return sum(jnp.dot(t, tri, preferred_element_type=F32) for t in _split3(a))


def _log_sigmoid(x):
    return jnp.minimum(x, 0.0) - jnp.log1p(jnp.exp(-jnp.abs(x)))


def _fox_cum(flog_t, bf_col):
    s_len = flog_t.shape[1]

    def body(fl_ref, bf_ref, cum_ref):
        r = lax.broadcasted_iota(jnp.int32, (128, 128), 0)
        c = lax.broadcasted_iota(jnp.int32, (128, 128), 1)
        upper = (r <= c).astype(BF16)

        def step(t, carry):
            sl = pl.ds(pl.multiple_of(t * 128, 128), 128)
            lf = _log_sigmoid(fl_ref[:, sl] + bf_ref[...])
            cs = _dot_ones(lf, upper) + carry
            cum_ref[:, sl] = cs
            return cs[:, 127:128]

        lax.fori_loop(0, s_len // 128, step, jnp.zeros((FOX_H, 1), F32))

    return _pcall(body, name="fox_cum", out_shape=jax.ShapeDtypeStruct((FOX_H, s_len), F32))(flog_t, bf_col)


def _fox_gate_bwd(drow, dcol, flog_t, bf_col):
    s_len = flog_t.shape[1]
    n = s_len // 128

    def body(dr_ref, dc_ref, fl_ref, bf_ref, dfl_ref, dbf_ref):
        r = lax.broadcasted_iota(jnp.int32, (128, 128), 0)
        c = lax.broadcasted_iota(jnp.int32, (128, 128), 1)
        lower = (r >= c).astype(BF16)

        def step(t, carry):
            run, tot = carry
            sl = pl.ds(pl.multiple_of((n - 1 - t) * 128, 128), 128)
            rc = _dot_ones(dr_ref[:, sl] - dc_ref[:, sl], lower) + run
            dfl = rc * _sigmoid(-(fl_ref[:, sl] + bf_ref[...]))
            dfl_ref[:, sl] = dfl
            return rc[:, 0:1], tot + jnp.sum(dfl, axis=1, keepdims=True)

        zero = jnp.zeros((FOX_H, 1), F32)
        _, tot = lax.fori_loop(0, n, step, (zero, zero))
        dbf_ref[...] = jnp.broadcast_to(tot, (FOX_H, 128))

    return _pcall(
        body,
        name="fox_gate_bwd",
        out_shape=[jax.ShapeDtypeStruct((FOX_H, s_len), F32), jax.ShapeDtypeStruct((FOX_H, 128), F32)],
    )(drow, dcol, flog_t, bf_col)


def _diag_mask(blk, transposed=False):
    r = lax.broadcasted_iota(jnp.int32, (blk, blk), 0)
    c = lax.broadcasted_iota(jnp.int32, (blk, blk), 1)
    return c >= r if transposed else r >= c


_NT = (((1,), (1,)), ((), ()))
_TN = (((0,), (0,)), ((), ()))


def _fox_fwd(qkv, cum_row):
    s_len = qkv.shape[0]
    blk = min(ATT_BLK, s_len)
    nb = s_len // blk
    log2e = 1.4426950408889634

    def body(q_ref, k_ref, v_ref, c_ref, o_ref, lse_ref, mx_s, acc_s, u_s):
        i = pl.program_id(1)

        def key_cols(j, n):
            return pl.ds(pl.multiple_of(j * blk, blk), n * blk)

        def walk(tile):
            def four_pairs(t, carry):
                for u in range(4):
                    tile(8 * t + 2 * u, 2, False)
                return carry

            lax.fori_loop(0, i // 8, four_pairs, 0)

            @pl.when((i // 4) % 2 == 1)
            def _():
                tile(8 * (i // 8), 2, False)
                tile(8 * (i // 8) + 2, 2, False)

            @pl.when((i // 2) % 2 == 1)
            def _():
                tile(4 * (i // 4), 2, False)

            @pl.when(i % 2 == 1)
            def _():
                tile(i - 1, 1, False)

            tile(i, 1, True)

        def lane_max(j, n, masked):
            cols = key_cols(j, n)
            u = lax.dot_general(q_ref[...], k_ref[cols, :], _NT, preferred_element_type=F32) * (FOX_SCALE * log2e) - c_ref[:, cols] * log2e
            if masked:
                u = jnp.where(_diag_mask(blk), u, NEG)
            u_s[:, cols] = u
            part = u[:, 0:128]
            for t in range(1, n * blk // 128):
                part = jnp.maximum(part, u[:, t * 128 : (t + 1) * 128])
            mx_s[...] = jnp.maximum(mx_s[...], part)

        mx_s[...] = jnp.full(mx_s.shape, NEG, F32)
        walk(lane_max)
        m = jnp.max(mx_s[...], axis=1, keepdims=True)

        def weigh(j, n, masked):
            cols = key_cols(j, n)
            p = jnp.exp2(u_s[:, cols] - m)
            ones_col = (lax.broadcasted_iota(jnp.int32, (n * blk, 128), 1) == 0).astype(BF16)
            v1 = jnp.concatenate([v_ref[cols, :], ones_col], axis=1)
            acc_s[...] += jnp.dot(p.astype(BF16), v1, preferred_element_type=F32)

        acc_s[...] = jnp.zeros(acc_s.shape, F32)
        walk(weigh)
        l = acc_s[:, FOX_DH : FOX_DH + 1]
        o_ref[...] = acc_s[:, :FOX_DH] / l
        lse_ref[...] = m * (1.0 / log2e) + jnp.log(l)

    return _pcall(
        body,
        name="fox_fwd",
        grid=(FOX_H, nb),
        out_shape=[jax.ShapeDtypeStruct((s_len, FOX_W), F32), jax.ShapeDtypeStruct((FOX_H, s_len, 1), F32)],
        in_specs=[
            pl.BlockSpec((blk, FOX_DH), lambda h, i: (i, h)),
            pl.BlockSpec((s_len, FOX_DH), lambda h, i: (0, FOX_H + h)),
            pl.BlockSpec((s_len, FOX_DH), lambda h, i: (0, 2 * FOX_H + h)),
            pl.BlockSpec((None, 1, s_len), lambda h, i: (h, 0, 0)),
        ],
        out_specs=[
            pl.BlockSpec((blk, FOX_DH), lambda h, i: (i, h)),
            pl.BlockSpec((None, blk, 1), lambda h, i: (h, i, 0)),
        ],
        scratch_shapes=[pltpu.VMEM((blk, 128), F32), pltpu.VMEM((blk, 2 * FOX_DH), F32), pltpu.VMEM((blk, s_len), F32)],
        compiler_params=_cp("parallel", "arbitrary"),
    )(qkv, qkv, qkv, cum_row)


def _fox_bwd(qkv, cum_col, lse_row, delta_row, do):
    s_len = qkv.shape[0]
    blk = min(ATT_BLK, s_len)
    nb = s_len // blk

    def body(q_ref, k_ref, v_ref, c_ref, lse_ref, dl_ref, do_ref, dq_ref, dk_ref, dv_ref, dc_ref, dr_ref, dk_s, dv_s, dc_s, cb_s, dq_s):
        j = pl.program_id(1)

        @pl.when(j == 0)
        def _():
            dq_s[...] = jnp.zeros(dq_s.shape, F32)
            dr_ref[...] = jnp.zeros(dr_ref.shape, F32)

        dk_s[...] = jnp.zeros(dk_s.shape, F32)
        dv_s[...] = jnp.zeros(dv_s.shape, F32)
        dc_s[...] = jnp.zeros(dc_s.shape, F32)
        cb_s[...] = jnp.broadcast_to(c_ref[...], cb_s.shape)

        def tile(i, n, diag):
            rows = pl.ds(pl.multiple_of(i * blk, blk), n * blk)
            q, dob = q_ref[rows, :], do_ref[rows, :]
            k, v = k_ref[...], v_ref[...]
            s_t = lax.dot_general(k, q, _NT, preferred_element_type=F32) * FOX_SCALE - cb_s[:, : n * blk]
            p_t = jnp.exp(s_t - lse_ref[:, rows])
            if diag:
                p_t = jnp.where(_diag_mask(blk, transposed=True), p_t, 0.0)
            dp_t = lax.dot_general(v, dob, _NT, preferred_element_type=F32)
            ds_t = p_t * (dp_t - dl_ref[:, rows])
            dsb = ds_t.astype(BF16)
            dv_s[...] += jnp.dot(p_t.astype(BF16), dob, preferred_element_type=F32)
            dk_s[...] += jnp.dot(dsb, q, preferred_element_type=F32)
            dq_c = lax.dot_general(dsb, k, _TN, preferred_element_type=F32)
            part = ds_t[:, 0:128]
            for t in range(1, n * blk // 128):
                part = part + ds_t[:, t * 128 : (t + 1) * 128]
            dc_s[...] += part
            dr_ref[:, rows] += jnp.sum(ds_t, axis=0, keepdims=True)
            if diag:
                dq_s[rows, :] = (dq_s[rows, :] + dq_c) * FOX_SCALE
            else:
                dq_s[rows, :] += dq_c

        tile(j, 1, True)
        below = nb - 1 - j
        b0, b1, b2 = below % 2, (below // 2) % 2, (below // 4) % 2

        @pl.when(b0 == 1)
        def _():
            tile(j + 1, 1, False)

        @pl.when(b1 == 1)
        def _():
            tile(j + 1 + b0, 2, False)

        @pl.when(b2 == 1)
        def _():
            tile(j + 1 + b0 + 2 * b1, 2, False)
            tile(j + 3 + b0 + 2 * b1, 2, False)

        first = j + 1 + b0 + 2 * b1 + 4 * b2

        def four_pairs(t, carry):
            for u in range(4):
                tile(first + 8 * t + 2 * u, 2, False)
            return carry

        lax.fori_loop(0, below // 8, four_pairs, 0)
        dk_ref[...] = (dk_s[...] * FOX_SCALE).astype(BF16)
        dv_ref[...] = dv_s[...].astype(BF16)
        dc_ref[...] = jnp.sum(dc_s[...], axis=1, keepdims=True)

        @pl.when(j == nb - 1)
        def _():
            dq_ref[...] = dq_s[...].astype(BF16)

    head = lambda h, j: (0, h)
    row = pl.BlockSpec((None, 1, s_len), lambda h, j: (h, 0, 0))
    return _pcall(
        body,
        name="fox_bwd",
        grid=(FOX_H, nb),
        out_shape=[
            jax.ShapeDtypeStruct((s_len, FOX_W), BF16),
            jax.ShapeDtypeStruct((s_len, FOX_W), BF16),
            jax.ShapeDtypeStruct((s_len, FOX_W), BF16),
            jax.ShapeDtypeStruct((FOX_H, s_len, 1), F32),
            jax.ShapeDtypeStruct((FOX_H, 1, s_len), F32),
        ],
        in_specs=[
            pl.BlockSpec((s_len, FOX_DH), head),
            pl.BlockSpec((blk, FOX_DH), lambda h, j: (j, FOX_H + h)),
            pl.BlockSpec((blk, FOX_DH), lambda h, j: (j, 2 * FOX_H + h)),
            pl.BlockSpec((None, blk, 1), lambda h, j: (h, j, 0)),
            row,
            row,
            pl.BlockSpec((s_len, FOX_DH), head),
        ],
        out_specs=[
            pl.BlockSpec((s_len, FOX_DH), head),
            pl.BlockSpec((blk, FOX_DH), lambda h, j: (j, h)),
            pl.BlockSpec((blk, FOX_DH), lambda h, j: (j, h)),
            pl.BlockSpec((None, blk, 1), lambda h, j: (h, j, 0)),
            row,
        ],
        scratch_shapes=[
            pltpu.VMEM((blk, FOX_DH), F32),
            pltpu.VMEM((blk, FOX_DH), F32),
            pltpu.VMEM((blk, 128), F32),
            pltpu.VMEM((blk, 2 * blk), F32),
            pltpu.VMEM((s_len, FOX_DH), F32),
        ],
        compiler_params=_cp("parallel", "arbitrary"),
    )(qkv, qkv, qkv, cum_col, lse_row, delta_row, do)


def _swa_bias():
    cols = SWA_G * WINDOW
    k = np.arange(2 * WINDOW)[:, None]
    q = np.arange(cols)[None, :]
    dist = (q % WINDOW) - k + WINDOW
    valid = (dist >= 0) & (dist < WINDOW)
    out = np.empty((2, SWA_HKV, 2 * WINDOW, cols), np.float32)
    for g in range(SWA_HKV):
        slope = np.array([SLOPES[g * SWA_G + t] for t in range(SWA_G)], np.float32)[q // WINDOW]
        bias = -(slope * dist.astype(np.float32))
        out[0, g] = np.where(valid & (k >= WINDOW), bias, np.float32(NEG))
        out[1, g] = np.where(valid, bias, np.float32(NEG))
    return jnp.asarray(out)


def _swa_group(i, q_ref, kk, sinks_ref, bias_ref, g):
    cols = SWA_G * WINDOW
    head = lax.broadcasted_iota(jnp.int32, (1, cols), 1) // WINDOW
    sink = jnp.zeros((1, cols), F32)
    for t in range(SWA_G):
        sink = jnp.where(head == t, sinks_ref[g * SWA_G + t], sink)
    q = jnp.concatenate([q_ref[:, (g * SWA_G + t) * SWA_DH : (g * SWA_G + t + 1) * SWA_DH] for t in range(SWA_G)], axis=0)
    k = kk[:, g * SWA_DH : (g + 1) * SWA_DH]
    s = lax.dot_general(k, q, _NT, preferred_element_type=F32) * SWA_SCALE + bias_ref[jnp.minimum(i, 1), g]
    m = jnp.maximum(jnp.max(s, axis=0, keepdims=True), sink)
    e = jnp.exp(s - m)
    e_sink = jnp.exp(sink - m)
    inv = 1.0 / (jnp.sum(e, axis=0, keepdims=True) + e_sink)
    return q, k, e * inv, e_sink * inv


def _swa_specs(col_q, col_k, col_v, rev, nb):
    def blk(t):
        return nb - 1 - t if rev else t

    return [
        pl.BlockSpec((WINDOW, SWA_W), lambda t: (blk(t), col_q)),
        pl.BlockSpec((WINDOW, SWA_KVW), lambda t: (jnp.maximum(blk(t) - 1, 0), col_k)),
        pl.BlockSpec((WINDOW, SWA_KVW), lambda t: (blk(t), col_k)),
        pl.BlockSpec((WINDOW, SWA_KVW), lambda t: (jnp.maximum(blk(t) - 1, 0), col_v)),
        pl.BlockSpec((WINDOW, SWA_KVW), lambda t: (blk(t), col_v)),
    ]


def _swa_fwd(qkv, sinks):
    s_len = qkv.shape[0]
    nb = s_len // WINDOW
    bias_spec = pl.BlockSpec((2, SWA_HKV, 2 * WINDOW, SWA_G * WINDOW), lambda t: (0, 0, 0, 0))

    def body(q_ref, kp_ref, kc_ref, vp_ref, vc_ref, sinks_ref, bias_ref, o_ref):
        i = pl.program_id(0)
        kk = jnp.concatenate([kp_ref[...], kc_ref[...]], axis=0)
        vv = jnp.concatenate([vp_ref[...], vc_ref[...]], axis=0)
        for g in range(SWA_HKV):
            _, _, p, _ = _swa_group(i, q_ref, kk, sinks_ref, bias_ref, g)
            o = lax.dot_general(p.astype(BF16), vv[:, g * SWA_DH : (g + 1) * SWA_DH], _TN, preferred_element_type=F32)
            for t in range(SWA_G):
                h = g * SWA_G + t
                o_ref[:, h * SWA_DH : (h + 1) * SWA_DH] = o[t * WINDOW : (t + 1) * WINDOW, :]

    return _pcall(
        body,
        name="swa_fwd",
        grid=(nb,),
        out_shape=jax.ShapeDtypeStruct((s_len, SWA_W), F32),
        in_specs=_swa_specs(0, 4, 5, False, nb) + [pl.BlockSpec(memory_space=pltpu.SMEM), bias_spec],
        out_specs=pl.BlockSpec((WINDOW, SWA_W), lambda t: (t, 0)),
        compiler_params=_cp("parallel"),
    )(qkv, qkv, qkv, qkv, qkv, sinks, _swa_bias())


def _swa_bwd(qkv, sinks, do):
    s_len = qkv.shape[0]
    nb = s_len // WINDOW
    bias_spec = pl.BlockSpec((2, SWA_HKV, 2 * WINDOW, SWA_G * WINDOW), lambda t: (0, 0, 0, 0))

    def body(q_ref, kp_ref, kc_ref, vp_ref, vc_ref, sinks_ref, bias_ref, do_ref, dq_ref, dk_ref, dv_ref, dsink_ref, ck_s, cv_s, dkk_s, dvv_s):
        t = pl.program_id(0)
        i = nb - 1 - t

        @pl.when(t == 0)
        def _():
            ck_s[...] = jnp.zeros(ck_s.shape, F32)
            cv_s[...] = jnp.zeros(cv_s.shape, F32)
            dsink_ref[...] = jnp.zeros(dsink_ref.shape, F32)

        kk = jnp.concatenate([kp_ref[...], kc_ref[...]], axis=0)
        vv = jnp.concatenate([vp_ref[...], vc_ref[...]], axis=0)
        lane = lax.broadcasted_iota(jnp.int32, (1, 128), 1)
        dsink = jnp.zeros((1, 128), F32)
        for g in range(SWA_HKV):
            cols = slice(g * SWA_DH, (g + 1) * SWA_DH)
            q, k, p, p_sink = _swa_group(i, q_ref, kk, sinks_ref, bias_ref, g)
            dob = jnp.concatenate([do_ref[:, (g * SWA_G + t) * SWA_DH : (g * SWA_G + t + 1) * SWA_DH] for t in range(SWA_G)], axis=0)
            dp = lax.dot_general(vv[:, cols], dob, _NT, preferred_element_type=F32)
            delta = jnp.sum(p * dp, axis=0, keepdims=True)
            dsb = (p * (dp - delta)).astype(BF16)
            dq = (lax.dot_general(dsb, k, _TN, preferred_element_type=F32) * SWA_SCALE).astype(BF16)
            ps_d = p_sink * delta
            for t in range(SWA_G):
                h = g * SWA_G + t
                dq_ref[:, h * SWA_DH : (h + 1) * SWA_DH] = dq[t * WINDOW : (t + 1) * WINDOW, :]
                dsink = dsink + jnp.where(lane == h, -jnp.sum(ps_d[:, t * WINDOW : (t + 1) * WINDOW], axis=1, keepdims=True), 0.0)
            dkk_s[:, cols] = jnp.dot(dsb, q, preferred_element_type=F32) * SWA_SCALE
            dvv_s[:, cols] = jnp.dot(p.astype(BF16), dob, preferred_element_type=F32)
        dk_ref[...] = (dkk_s[WINDOW:, :] + ck_s[...]).astype(BF16)
        dv_ref[...] = (dvv_s[WINDOW:, :] + cv_s[...]).astype(BF16)
        ck_s[...] = dkk_s[:WINDOW, :]
        cv_s[...] = dvv_s[:WINDOW, :]
        dsink_ref[...] += dsink

    row = lambda t: (nb - 1 - t, 0)
    return _pcall(
        body,
        name="swa_bwd",
        grid=(nb,),
        out_shape=[
            jax.ShapeDtypeStruct((s_len, SWA_W), BF16),
            jax.ShapeDtypeStruct((s_len, SWA_KVW), BF16),
            jax.ShapeDtypeStruct((s_len, SWA_KVW), BF16),
            jax.ShapeDtypeStruct((1, 128), F32),
        ],
        in_specs=_swa_specs(0, 4, 5, True, nb)
        + [pl.BlockSpec(memory_space=pltpu.SMEM), bias_spec, pl.BlockSpec((WINDOW, SWA_W), row)],
        out_specs=[
            pl.BlockSpec((WINDOW, SWA_W), row),
            pl.BlockSpec((WINDOW, SWA_KVW), row),
            pl.BlockSpec((WINDOW, SWA_KVW), row),
            pl.BlockSpec((1, 128), lambda t: (0, 0)),
        ],
        scratch_shapes=[
            pltpu.VMEM((WINDOW, SWA_KVW), F32),
            pltpu.VMEM((WINDOW, SWA_KVW), F32),
            pltpu.VMEM((2 * WINDOW, SWA_KVW), F32),
            pltpu.VMEM((2 * WINDOW, SWA_KVW), F32),
        ],
        compiler_params=_cp("arbitrary"),
    )(qkv, qkv, qkv, qkv, qkv, sinks, _swa_bias(), do)


def _branch_fwd(o, gates, g_blk, w_b, name):
    s_len, wd = o.shape
    d = w_b.shape[1]
    tm = min(512, s_len)

    def body(o_ref, g_ref, w_ref, y_ref, a_ref):
        g = g_ref[...].astype(F32)
        a = (o_ref[...] * (g * _sigmoid(g))).astype(BF16)
        a_ref[...] = a
        y_ref[...] = jnp.dot(a, w_ref[...], preferred_element_type=F32).astype(BF16)

    return _pcall(
        body,
        name=name,
        grid=(s_len // tm,),
        out_shape=[jax.ShapeDtypeStruct((s_len, d), BF16), jax.ShapeDtypeStruct((s_len, wd), BF16)],
        in_specs=[
            pl.BlockSpec((tm, wd), lambda i: (i, 0)),
            pl.BlockSpec((tm, wd), lambda i: (i, g_blk)),
            pl.BlockSpec((wd, d), lambda i: (0, 0)),
        ],
        out_specs=[pl.BlockSpec((tm, d), lambda i: (i, 0)), pl.BlockSpec((tm, wd), lambda i: (i, 0))],
        compiler_params=_cp("parallel"),
    )(o, gates, w_b)


def _out_stage(gates, mf_blk, y_fox, y_swa, w_out, x, ada, ln_g, ln_b, target):
    s_len, d = x.shape
    tm = min(256, s_len)
    n_steps = s_len // tm

    def body(mf_ref, ms_ref, yf_ref, ys_ref, w_ref, x_ref, gate_ref, lg_ref, lb_ref, t_ref, mg_ref, dza_ref, dsub_ref, red_ref):
        i = pl.program_id(0)
        merged = _sigmoid(mf_ref[...].astype(F32)) * yf_ref[...].astype(F32) + _sigmoid(ms_ref[...].astype(F32)) * ys_ref[...].astype(F32)
        mb = merged.astype(BF16)
        mg_ref[...] = mb
        sub = jnp.dot(mb, w_ref[...], preferred_element_type=F32)
        gate = gate_ref[...]
        z = ALPHA * x_ref[...] + gate * sub
        mu = jnp.mean(z, axis=-1, keepdims=True)
        zc = z - mu
        var = jnp.mean(zc * zc, axis=-1, keepdims=True)
        rstd = lax.rsqrt(var + LN_EPS)
        zhat = zc * rstd
        err = zhat * lg_ref[...] + lb_ref[...] - t_ref[...]
        dout = err * (1.0 / d)
        dzhat = dout * lg_ref[...]
        dz = rstd * (dzhat - jnp.mean(dzhat, axis=-1, keepdims=True) - zhat * jnp.mean(dzhat * zhat, axis=-1, keepdims=True))
        dza_ref[...] = ALPHA * dz
        dsub_ref[...] = (gate * dz).astype(BF16)
        part = jnp.concatenate(
            [
                jnp.sum(dz * sub, axis=0, keepdims=True),
                jnp.sum(dout * zhat, axis=0, keepdims=True),
                jnp.sum(dout, axis=0, keepdims=True),
                jnp.sum(err * err, axis=0, keepdims=True),
                jnp.zeros((4, d), F32),
            ],
            axis=0,
        )

        @pl.when(i == 0)
        def _():
            red_ref[...] = part

        @pl.when(i > 0)
        def _():
            red_ref[...] += part

        @pl.when(i == n_steps - 1)
        def _():
            red_ref[4:5, :] = jnp.broadcast_to(jnp.sum(red_ref[3:4, :], axis=1, keepdims=True), (1, d))

    row = pl.BlockSpec((tm, d), lambda i: (i, 0))
    vec = pl.BlockSpec((1, d), lambda i: (0, 0))
    return _pcall(
        body,
        name="out_stage",
        grid=(n_steps,),
        out_shape=[
            jax.ShapeDtypeStruct((s_len, d), BF16),
            jax.ShapeDtypeStruct((s_len, d), F32),
            jax.ShapeDtypeStruct((s_len, d), BF16),
            jax.ShapeDtypeStruct((8, d), F32),
        ],
        in_specs=[
            pl.BlockSpec((tm, d), lambda i: (i, mf_blk)),
            pl.BlockSpec((tm, d), lambda i: (i, mf_blk + 1)),
            row,
            row,
            pl.BlockSpec((d, d), lambda i: (0, 0), pipeline_mode=pl.Buffered(1)),
            row,
            pl.BlockSpec((1, d), lambda i: (0, 2)),
            vec,
            vec,
            row,
        ],
        out_specs=[row, row, row, pl.BlockSpec((8, d), lambda i: (0, 0))],
        compiler_params=_cp("arbitrary"),
    )(gates, gates, y_fox, y_swa, w_out, x, ada, ln_g, ln_b, target)


def _merge_bwd(dsub, w_out, gates, mf_blk, y_fox, y_swa):
    s_len, d = dsub.shape
    tm = min(256, s_len)

    def body(ds_ref, w_ref, mf_ref, ms_ref, yf_ref, ys_ref, dmf_ref, dms_ref, dyf_ref, dys_ref):
        dm = lax.dot_general(ds_ref[...], w_ref[...], _NT, preferred_element_type=F32)
        sf, ss = _sigmoid(mf_ref[...].astype(F32)), _sigmoid(ms_ref[...].astype(F32))
        dmf_ref[...] = (dm * yf_ref[...].astype(F32) * (sf * (1.0 - sf))).astype(BF16)
        dms_ref[...] = (dm * ys_ref[...].astype(F32) * (ss * (1.0 - ss))).astype(BF16)
        dyf_ref[...] = (dm * sf).astype(BF16)
        dys_ref[...] = (dm * ss).astype(BF16)

    row = pl.BlockSpec((tm, d), lambda i: (i, 0))
    return _pcall(
        body,
        name="merge_bwd",
        grid=(s_len // tm,),
        out_shape=[jax.ShapeDtypeStruct((s_len, d), BF16)] * 4,
        in_specs=[
            row,
            pl.BlockSpec((d, d), lambda i: (0, 0), pipeline_mode=pl.Buffered(1)),
            pl.BlockSpec((tm, d), lambda i: (i, mf_blk)),
            pl.BlockSpec((tm, d), lambda i: (i, mf_blk + 1)),
            row,
            row,
        ],
        out_specs=[row] * 4,
        compiler_params=_cp("parallel"),
    )(dsub, w_out, gates, gates, y_fox, y_swa)


def _branch_bwd(dy, w_b, o, gates, g_blk, name, n_heads):
    s_len, d = dy.shape
    wd = w_b.shape[0]
    tm = min(512, s_len)

    def body(dy_ref, w_ref, o_ref, g_ref, do_ref, dg_ref, *rest):
        da = lax.dot_general(dy_ref[...], w_ref[...], _NT, preferred_element_type=F32)
        g = g_ref[...].astype(F32)
        sg = _sigmoid(g)
        do = da * (g * sg)
        do_ref[...] = do.astype(BF16)
        o = o_ref[...]
        dg_ref[...] = (da * o * (sg * (1.0 + g * (1.0 - sg)))).astype(BF16)
        if n_heads:
            prod = do.astype(BF16).astype(F32) * o
            lane = lax.broadcasted_iota(jnp.int32, (1, 128), 1)
            delta = jnp.zeros((tm, 128), F32)
            for h in range(n_heads):
                dh = jnp.sum(prod[:, h * 128 : (h + 1) * 128], axis=1, keepdims=True)
                delta = delta + jnp.where(lane == h, dh, 0.0)
            rest[0][...] = delta

    out_shape = [jax.ShapeDtypeStruct((s_len, wd), BF16), jax.ShapeDtypeStruct((s_len, wd), BF16)]
    out_specs = [pl.BlockSpec((tm, wd), lambda i: (i, 0))] * 2
    if n_heads:
        out_shape.append(jax.ShapeDtypeStruct((s_len, 128), F32))
        out_specs.append(pl.BlockSpec((tm, 128), lambda i: (i, 0)))
    return _pcall(
        body,
        name=name,
        grid=(s_len // tm,),
        out_shape=out_shape,
        in_specs=[
            pl.BlockSpec((tm, d), lambda i: (i, 0)),
            pl.BlockSpec((wd, d), lambda i: (0, 0)),
            pl.BlockSpec((tm, wd), lambda i: (i, 0)),
            pl.BlockSpec((tm, wd), lambda i: (i, g_blk)),
        ],
        out_specs=out_specs,
        compiler_params=_cp("parallel"),
    )(dy, w_b, o, gates)


def _in_bwd(dproj, w_in_t, x, ada, dza, ride):
    s_len, d = x.shape
    k_tot = dproj.shape[1]
    tm, tk, dn = min(512, s_len), k_tot // 4, d // 2
    ni, nk = s_len // tm, k_tot // tk
    n = len(ride)

    def body(dp_ref, w_ref, x_ref, sc_ref, dza_ref, *rest):
        ins, (gx_ref, red_ref), outs = rest[:n], rest[n : n + 2], rest[n + 2 : 2 * n + 2]
        sems, acc_s = rest[2 * n + 2 : 2 * n + 5], rest[2 * n + 5]
        i, nh, kk = pl.program_id(0), pl.program_id(1), pl.program_id(2)

        @pl.when((i == 0) & (nh == 0) & (kk == 0))
        def _():
            _rider_start("exchange", ins, outs, *sems)

        @pl.when((i == ni - 1) & (nh == 1) & (kk == nk - 1))
        def _():
            _rider_wait("exchange", ins, outs, *sems)

        part = jnp.dot(dp_ref[...], w_ref[...], preferred_element_type=F32)
        half = pl.ds(pl.multiple_of(nh * dn, dn), dn)

        @pl.when(kk == 0)
        def _():
            acc_s[:, half] = part

        @pl.when(kk > 0)
        def _():
            acc_s[:, half] += part

        @pl.when((nh == 1) & (kk == nk - 1))
        def _():
            dh = acc_s[...]
            xv = x_ref[...]
            mu = jnp.mean(xv, axis=-1, keepdims=True)
            xc = xv - mu
            var = jnp.mean(xc * xc, axis=-1, keepdims=True)
            rstd = lax.rsqrt(var + LN_EPS)
            xhat = xc * rstd
            dxhat = dh * (1.0 + sc_ref[...])
            dx = rstd * (dxhat - jnp.mean(dxhat, axis=-1, keepdims=True) - xhat * jnp.mean(dxhat * xhat, axis=-1, keepdims=True))
            gx_ref[...] = dza_ref[...] + dx
            part_r = jnp.concatenate(
                [jnp.sum(dh, axis=0, keepdims=True), jnp.sum(dh * xhat, axis=0, keepdims=True), jnp.zeros((6, d), F32)], axis=0
            )

            @pl.when(i == 0)
            def _():
                red_ref[...] = part_r

            @pl.when(i > 0)
            def _():
                red_ref[...] += part_r

    row = pl.BlockSpec((tm, d), lambda i, nh, kk: (i, 0))
    hbm = pl.BlockSpec(memory_space=pltpu.HBM)
    return _pcall(
        body,
        name="in_bwd",
        grid=(ni, 2, nk),
        out_shape=[jax.ShapeDtypeStruct((s_len, d), F32), jax.ShapeDtypeStruct((8, d), F32)]
        + [jax.ShapeDtypeStruct(r.shape, r.dtype) for r in ride],
        in_specs=[
            pl.BlockSpec((tm, tk), lambda i, nh, kk: (i, kk)),
            pl.BlockSpec((tk, dn), lambda i, nh, kk: (kk, nh)),
            row,
            pl.BlockSpec((1, d), lambda i, nh, kk: (0, 1)),
            row,
        ]
        + [hbm] * n,
        out_specs=[row, pl.BlockSpec((8, d), lambda i, nh, kk: (0, 0))] + [hbm] * n,
        scratch_shapes=_rider_scratch(n) + [pltpu.VMEM((tm, d), F32)],
        compiler_params=_cp("arbitrary", "arbitrary", "arbitrary"),
    )(dproj, w_in_t, x, ada, dza, *ride)


def _pad_lanes(v, n):
    return jnp.pad(v, ((0, 0), (0, n - v.shape[1])))


def kernel(x, c, w_ada, b_ada, w_in, b_f, attn_sinks, w_br_fox, w_br_swa, w_out, ln_g, ln_b, loss_target, m_w_ada, m_b_ada, m_w_in, m_b_f, m_attn_sinks, m_w_br_fox, m_w_br_swa, m_w_out, m_ln_g, m_ln_b, v_w_ada, v_b_ada, v_w_in, v_b_f, v_attn_sinks, v_w_br_fox, v_w_br_swa, v_w_out, v_ln_g, v_ln_b):
    x2, tgt = x[0], loss_target[0]
    s_len, d = x2.shape
    me = 4 * lax.axis_index("x") + 2 * lax.axis_index("y") + lax.axis_index("c")
    off_ms = OFF_MF + d
    in_pad = off_ms + d
    c_ada = w_ada.shape[2]
    c_in = w_in.shape[2]
    c_br = w_br_fox.shape[2]

    w_in_full = _all_gather(w_in[0].T.astype(BF16), "ag_w_in", pltpu.HBM).reshape(N_DEV * c_in, d)
    w_in_pad = jnp.concatenate(
        [w_in_full[:REAL_FLOG_END], jnp.zeros((FLOG_PAD - N_FLOG, d), BF16), w_in_full[REAL_FLOG_END:]], axis=0
    )
    k_cut = REAL_FLOG_END // c_in

    c_all = _gather_rows(c, "ag_c")
    b_cols = lax.dynamic_slice(b_ada, (0, me * c_ada), (1, c_ada))
    ada_cols = _ada_fwd(c_all, w_ada[0], b_cols)
    ada_g = _all_gather(ada_cols, "ag_ada", pltpu.VMEM)
    ada = lax.dynamic_index_in_dim(ada_g, me, axis=1, keepdims=False).reshape(1, N_DEV * c_ada)

    h = _ln_mod(x2, ada)
    qkv_fox = _mm_cols(h, w_in_pad, OFF_FQ, 3 * FOX_W, BF16, "proj_fox")
    flog = _mm_cols(h, w_in_pad, OFF_FLOG, 128, F32, "proj_flog")
    qkv_swa = _mm_cols(h, w_in_pad, OFF_SQ, SWA_W + 2 * SWA_KVW, BF16, "proj_swa")
    gates, w_bf, w_bs, w_o = _mm_cols(
        h, w_in_pad, OFF_GF, in_pad - OFF_GF, BF16, "proj_gates",
        ride=(w_br_fox[0].astype(BF16), w_br_swa[0].astype(BF16), w_out[0].astype(BF16)),
    )
    w_bf = w_bf.reshape(N_DEV, FOX_W, c_br).transpose(1, 0, 2).reshape(FOX_W, d)
    w_bs = w_bs.reshape(N_DEV, SWA_W, c_br).transpose(1, 0, 2).reshape(SWA_W, d)
    w_o = w_o.reshape(d, d)
    mf_blk = (OFF_MF - OFF_GF) // d

    flog_t = flog[:, :N_FLOG].T
    bf_col = b_f.reshape(FOX_H, 1)
    cum = _fox_cum(flog_t, bf_col)
    cum_row = cum.reshape(FOX_H, 1, s_len)
    o_fox, lse = _fox_fwd(qkv_fox, cum_row)
    sinks = attn_sinks.reshape(SWA_HQ)
    o_swa = _swa_fwd(qkv_swa, sinks)

    y_fox, a_fox = _branch_fwd(o_fox, gates, 0, w_bf, "branch_fox")
    y_swa, a_swa = _branch_fwd(o_swa, gates, 1, w_bs, "branch_swa")
    merged, dza, dsub, red = _out_stage(gates, mf_blk, y_fox, y_swa, w_o, x2, ada, ln_g, ln_b, tgt)
    loss = lax.psum(0.5 * red[4, 0] / d, ("x", "y", "c"))

    dmf, dms, dy_fox, dy_swa = _merge_bwd(dsub, w_o, gates, mf_blk, y_fox, y_swa)
    do_fox, dg_fox, delta = _branch_bwd(dy_fox, w_bf, o_fox, gates, 0, "branch_fox_bwd", FOX_H)
    do_swa, dg_swa = _branch_bwd(dy_swa, w_bs, o_swa, gates, 1, "branch_swa_bwd", 0)
    delta_row = delta[:, :FOX_H].T.reshape(FOX_H, 1, s_len)
    dq_f, dk_f, dv_f, dcol, drow = _fox_bwd(
        qkv_fox, cum.reshape(FOX_H, s_len, 1), lse.reshape(FOX_H, 1, s_len), delta_row, do_fox
    )
    dflog_t, dbf = _fox_gate_bwd(drow.reshape(FOX_H, s_len), dcol.reshape(FOX_H, s_len), flog_t, bf_col)
    dq_s, dk_s, dv_s, dsink = _swa_bwd(qkv_swa, sinks, do_swa)
    dflog = _pad_lanes(dflog_t.T, FLOG_PAD).astype(BF16)
    dproj = jnp.concatenate([dq_f, dk_f, dv_f, dflog, dq_s, dk_s, dv_s, dg_fox, dg_swa, dmf, dms], axis=1)
    g_w_bf = _mm_tn(a_fox, dy_fox, "grad_w_br_fox")
    g_w_bs = _mm_tn(a_swa, dy_swa, "grad_w_br_swa")
    g_w_o = _mm_tn(merged, dsub, "grad_w_out")
    g_w_in, r_bf, r_bs, r_o = _mm_tn(
        dproj, h, "grad_w_in",
        ride=(
            g_w_bf.reshape(FOX_W, N_DEV, c_br).transpose(1, 0, 2),
            g_w_bs.reshape(SWA_W, N_DEV, c_br).transpose(1, 0, 2),
            g_w_o.reshape(N_DEV, d // N_DEV, d),
        ),
    )
    pad = FLOG_PAD - N_FLOG
    g_blocks = jnp.stack(
        [g_w_in[k * c_in : (k + 1) * c_in] for k in range(k_cut)]
        + [jnp.concatenate([g_w_in[k_cut * c_in : REAL_FLOG_END], g_w_in[OFF_SQ : (k_cut + 1) * c_in + pad]], axis=0)]
        + [g_w_in[k * c_in + pad : (k + 1) * c_in + pad] for k in range(k_cut + 1, N_DEV)]
    )

    grad_x, red2, r_in = _in_bwd(dproj, w_in_pad, x2, ada, dza, ride=(g_blocks,))
    out_w_in = _sum_adam_t(r_in, w_in[0].T, m_w_in[0].T, v_w_in[0].T, "adam_w_in")
    out_w_in = [o.T for o in out_w_in]
    out_w_bf = _sum_adam(r_bf, w_br_fox[0], m_w_br_fox[0], v_w_br_fox[0], "adam_w_br_fox")
    out_w_bs = _sum_adam(r_bs, w_br_swa[0], m_w_br_swa[0], v_w_br_swa[0], "adam_w_br_swa")
    out_w_o = _sum_adam(r_o, w_out[0], m_w_out[0], v_w_out[0], "adam_w_out")

    packed = jnp.concatenate([red2[0:1], red2[1:2], red[0:1], _pad_lanes(dbf[:, 0].reshape(1, FOX_H), 128), dsink, red[1:2], red[2:3]], axis=1)
    gathered = _gather_rows(packed, "ag_small")
    pack = lambda a, b, cc, dd, e: jnp.concatenate([a, _pad_lanes(b, 128), _pad_lanes(cc, 128), dd, e], axis=1)
    small = _small_adam(
        gathered,
        pack(b_ada, b_f, attn_sinks, ln_g, ln_b),
        pack(m_b_ada, m_b_f, m_attn_sinks, m_ln_g, m_ln_b),
        pack(v_b_ada, v_b_f, v_attn_sinks, v_ln_g, v_ln_b),
    )
    dada_cols = lax.dynamic_slice(gathered, (0, me * c_ada), (N_DEV, c_ada))
    out_w_ada = _wada_adam(c_all.T, dada_cols, w_ada[0], m_w_ada[0], v_w_ada[0])

    o1, o2, o3 = 3 * d, 3 * d + 128, 3 * d + 256

    def unpack(p):
        return p[:, :o1], p[:, o1 : o1 + FOX_H], p[:, o2 : o2 + SWA_HQ], p[:, o3 : o3 + d], p[:, o3 + d : o3 + 2 * d]

    kinds = []
    for k in range(4):
        b_ada_k, b_f_k, sinks_k, ln_g_k, ln_b_k = unpack(small[k])
        kinds.append(
            [out_w_ada[k][None], b_ada_k, out_w_in[k][None], b_f_k, sinks_k, out_w_bf[k][None], out_w_bs[k][None], out_w_o[k][None], ln_g_k, ln_b_k]
        )
    return (loss, grad_x[None], *kinds[0], *kinds[1], *kinds[2], *kinds[3])
```

```python
import numpy as np
import jax
import jax.numpy as jnp
from jax import lax
from jax.experimental import pallas as pl
from jax.experimental.pallas import tpu as pltpu

F32 = jnp.float32
BF16 = jnp.bfloat16
N_DEV = 8
MESH = pl.DeviceIdType.MESH

FOX_H, FOX_DH, FOX_W = 8, 128, 1024
SWA_HQ, SWA_HKV, SWA_DH, SWA_G = 16, 4, 64, 4
SWA_W, SWA_KVW, WINDOW = 1024, 256, 128
LN_EPS = 1e-5
NEG = -1e30
DEPTH = 1
ALPHA = (2.0 * DEPTH) ** 0.25
FOX_SCALE = FOX_DH ** -0.5
SWA_SCALE = SWA_DH ** -0.5
SLOPES = [2.0 ** (-8.0 * (h + 1.0) / SWA_HQ) for h in range(SWA_HQ)]

ADAM_LR, ADAM_B1, ADAM_B2, ADAM_EPS, ADAM_WD, ADAM_STEP = 0.001, 0.9, 0.999, 1e-08, 0.01, 10

N_FLOG = 8
FLOG_PAD = 512
OFF_FQ, OFF_FK, OFF_FV, OFF_FLOG = 0, 1024, 2048, 3072
OFF_SQ = OFF_FLOG + FLOG_PAD
OFF_SK = OFF_SQ + SWA_W
OFF_SV = OFF_SK + SWA_KVW
OFF_GF = OFF_SV + SWA_KVW
OFF_GS = OFF_GF + FOX_W
OFF_MF = OFF_GS + SWA_W
REAL_FLOG_END = OFF_FLOG + N_FLOG

ATT_BLK = 512
VMEM_LIMIT = 58 * 1024 * 1024


def _pcall(body, **kw):
    return pl.pallas_call(body, **kw)


def _cp(*sem):
    return pltpu.CompilerParams(dimension_semantics=sem, vmem_limit_bytes=VMEM_LIMIT)


def _sigmoid(x):
    return 0.5 * jnp.tanh(0.5 * x) + 0.5


def _all_gather(x, name, space):
    m_per, n = x.shape

    def body(x_ref, out_ref, send_sems, recv_sems, local_sem):
        mx, my, mc = lax.axis_index("x"), lax.axis_index("y"), lax.axis_index("c")
        me, sibling = (mx, my, mc), (mx, my, 1 - mc)
        xn, yn, dg = (1 - mx, my), (mx, 1 - my), (1 - mx, 1 - my)
        south = mc == 0
        src_chip = (jnp.where(south, 1 - mx, mx), jnp.where(south, my, 1 - my))
        dst_chip = (jnp.where(south, mx, 1 - mx), jnp.where(south, 1 - my, my))

        def rows(px, py, pc):
            return out_ref.at[4 * px + 2 * py + pc]

        def copy(k, block, to, src=None):
            return pltpu.make_async_remote_copy(
                src_ref=rows(*block) if src is None else src,
                dst_ref=rows(*block),
                send_sem=send_sems.at[k],
                recv_sem=recv_sems.at[k],
                device_id=to,
                device_id_type=MESH,
            )

        mine = pltpu.make_async_copy(x_ref, rows(*me), local_sem)
        mine.start()
        first = [copy(0, me, sibling, src=x_ref), copy(1, me, (*xn, mc), src=x_ref), copy(2, me, (*yn, mc), src=x_ref)]
        for cp in first:
            cp.start()
        copy(1, (*xn, mc), me).wait_recv()
        copy(2, (*yn, mc), me).wait_recv()
        later = [copy(3, (*src_chip, mc), (*dst_chip, mc)), copy(4, (*xn, mc), sibling), copy(5, (*yn, mc), sibling)]
        for cp in later:
            cp.start()
        copy(3, (*dg, mc), me).wait_recv()
        last = copy(6, (*dg, mc), sibling)
        last.start()
        copy(0, sibling, me).wait_recv()
        for k, chip in ((4, xn), (5, yn), (6, dg)):
            copy(k, (*chip, 1 - mc), me).wait_recv()
        for cp in first + later + [last]:
            cp.wait_send()
        mine.wait()

    return _pcall(
        body,
        name=name,
        out_shape=jax.ShapeDtypeStruct((N_DEV, m_per, n), x.dtype),
        in_specs=[pl.BlockSpec(memory_space=space)],
        out_specs=pl.BlockSpec(memory_space=space),
        scratch_shapes=[pltpu.SemaphoreType.DMA((7,)), pltpu.SemaphoreType.DMA((7,)), pltpu.SemaphoreType.DMA],
    )(x)


def _peer(d, mx, my, mc):
    return (1 - mx if (d >> 2) & 1 else mx, 1 - my if (d >> 1) & 1 else my, 1 - mc if d & 1 else mc)


def _rider_copies(kind, ins, outs, send_sems, recv_sems, local_sems):
    mx, my, mc = lax.axis_index("x"), lax.axis_index("y"), lax.axis_index("c")
    me = 4 * mx + 2 * my + mc
    remote, local = [], []
    for a in range(len(ins)):
        if kind == "gather":
            m_per = ins[a].shape[0]
            mine = outs[a].at[pl.ds(me * m_per, m_per), :]
            local.append(pltpu.make_async_copy(ins[a], mine, local_sems.at[a]))
        else:
            local.append(pltpu.make_async_copy(ins[a].at[me], outs[a].at[0], local_sems.at[a]))
        for d in range(1, N_DEV):
            px, py, pc = _peer(d, mx, my, mc)
            if kind == "gather":
                src, dst = ins[a], mine
            else:
                src, dst = ins[a].at[4 * px + 2 * py + pc], outs[a].at[d]
            remote.append(
                pltpu.make_async_remote_copy(
                    src_ref=src,
                    dst_ref=dst,
                    send_sem=send_sems.at[a * 7 + d - 1],
                    recv_sem=recv_sems.at[a * 7 + d - 1],
                    device_id=(px, py, pc),
                    device_id_type=MESH,
                )
            )
    return remote, local


def _rider_start(*args):
    remote, local = _rider_copies(*args)
    for cp in local + remote:
        cp.start()


def _rider_wait(*args):
    remote, local = _rider_copies(*args)
    for cp in remote:
        cp.wait_recv()
    for cp in remote:
        cp.wait_send()
    for cp in local:
        cp.wait()


def _rider_scratch(n):
    return [pltpu.SemaphoreType.DMA((7 * n,)), pltpu.SemaphoreType.DMA((7 * n,)), pltpu.SemaphoreType.DMA((n,))]


def _gather_rows(v, name):
    n = v.shape[1]
    return _all_gather(jnp.broadcast_to(v, (8, n)), name, pltpu.VMEM)[:, 0, :]


def _adamw(w, g, m, v):
    m = ADAM_B1 * m + (1.0 - ADAM_B1) * g
    v = ADAM_B2 * v + (1.0 - ADAM_B2) * (g * g)
    m_hat = m / (1.0 - ADAM_B1**ADAM_STEP)
    v_hat = v / (1.0 - ADAM_B2**ADAM_STEP)
    delta = -ADAM_LR * (m_hat / (jnp.sqrt(v_hat) + ADAM_EPS) + ADAM_WD * w)
    return delta, m, v


def _sum_adam(recv, w, m, v, name):
    _, r_tot, c = recv.shape
    c_pad = -(-c // 128) * 128
    tr = r_tot
    while 8 * tr * c_pad * 4 > 6 * 1024 * 1024 and tr % 32 == 0:
        tr //= 2

    def body(r_ref, w_ref, m_ref, v_ref, g_ref, d_ref, nm_ref, nv_ref):
        g = r_ref[0].astype(F32)
        for k in range(1, N_DEV):
            g = g + r_ref[k].astype(F32)
        d, nm, nv = _adamw(w_ref[...], g, m_ref[...], v_ref[...])
        g_ref[...] = g
        d_ref[...] = d
        nm_ref[...] = nm
        nv_ref[...] = nv

    blk = pl.BlockSpec((tr, c), lambda i: (i, 0))
    return _pcall(
        body,
        name=name,
        grid=(r_tot // tr,),
        out_shape=[jax.ShapeDtypeStruct((r_tot, c), F32)] * 4,
        in_specs=[pl.BlockSpec((N_DEV, tr, c), lambda i: (0, i, 0)), blk, blk, blk],
        out_specs=[blk] * 4,
        compiler_params=_cp("parallel"),
    )(recv, w, m, v)


def _sum_adam_t(recv, w, m, v, name):
    _, c, r_tot = recv.shape
    tr = min(256, r_tot)

    def body(r_ref, w_ref, m_ref, v_ref, g_ref, d_ref, nm_ref, nv_ref):
        g = r_ref[0].astype(F32)
        for k in range(1, N_DEV):
            g = g + r_ref[k].astype(F32)
        d, nm, nv = _adamw(w_ref[...], g, m_ref[...], v_ref[...])
        g_ref[...] = g
        d_ref[...] = d
        nm_ref[...] = nm
        nv_ref[...] = nv

    blk = pl.BlockSpec((c, tr), lambda i: (0, i))
    return _pcall(
        body,
        name=name,
        grid=(r_tot // tr,),
        out_shape=[jax.ShapeDtypeStruct((c, r_tot), F32)] * 4,
        in_specs=[pl.BlockSpec((N_DEV, c, tr), lambda i: (0, 0, i)), blk, blk, blk],
        out_specs=[blk] * 4,
        compiler_params=_cp("parallel"),
    )(recv, w, m, v)


def _wada_adam(c_t, dada_cols, w, m, v):
    d_model, c = w.shape
    tr = min(256, d_model)

    def body(ct_ref, da_ref, w_ref, m_ref, v_ref, g_ref, d_ref, nm_ref, nv_ref):
        g = jnp.dot(ct_ref[...].astype(BF16), da_ref[...].astype(BF16), preferred_element_type=F32)
        d, nm, nv = _adamw(w_ref[...], g, m_ref[...], v_ref[...])
        g_ref[...] = g
        d_ref[...] = d
        nm_ref[...] = nm
        nv_ref[...] = nv

    blk = pl.BlockSpec((tr, c), lambda i: (i, 0))
    return _pcall(
        body,
        name="wada_adam",
        grid=(d_model // tr,),
        out_shape=[jax.ShapeDtypeStruct((d_model, c), F32)] * 4,
        in_specs=[pl.BlockSpec((tr, N_DEV), lambda i: (i, 0)), pl.BlockSpec((N_DEV, c), lambda i: (0, 0)), blk, blk, blk],
        out_specs=[blk] * 4,
        compiler_params=_cp("parallel"),
    )(c_t, dada_cols, w, m, v)


def _small_adam(gathered, w, m, v):
    p = w.shape[1]

    def body(a_ref, w_ref, m_ref, v_ref, g_ref, d_ref, nm_ref, nv_ref):
        g = a_ref[0:1, :]
        for k in range(1, N_DEV):
            g = g + a_ref[k : k + 1, :]
        d, nm, nv = _adamw(w_ref[...], g, m_ref[...], v_ref[...])
        g_ref[...] = g
        d_ref[...] = d
        nm_ref[...] = nm
        nv_ref[...] = nv

    return _pcall(
        body,
        name="small_adam",
        out_shape=[jax.ShapeDtypeStruct((1, p), F32)] * 4,
    )(gathered, w, m, v)


def _ada_fwd(c_all, w_ada, b_cols):
    c = w_ada.shape[1]

    def body(c_ref, w_ref, b_ref, o_ref):
        o_ref[...] = jnp.dot(c_ref[...].astype(BF16), w_ref[...].astype(BF16), preferred_element_type=F32) + b_ref[...]

    return _pcall(
        body,
        name="ada_fwd",
        out_shape=jax.ShapeDtypeStruct((N_DEV, c), F32),
        compiler_params=_cp(),
    )(c_all, w_ada, b_cols)


def _ln_mod(x, ada):
    s_len, d = x.shape
    tm = min(512, s_len)

    def body(x_ref, sh_ref, sc_ref, h_ref):
        xv = x_ref[...]
        mu = jnp.mean(xv, axis=-1, keepdims=True)
        xc = xv - mu
        var = jnp.mean(xc * xc, axis=-1, keepdims=True)
        xhat = xc * lax.rsqrt(var + LN_EPS)
        h_ref[...] = (xhat * (1.0 + sc_ref[...]) + sh_ref[...]).astype(BF16)

    return _pcall(
        body,
        name="ln_mod",
        grid=(s_len // tm,),
        out_shape=jax.ShapeDtypeStruct((s_len, d), BF16),
        in_specs=[
            pl.BlockSpec((tm, d), lambda i: (i, 0)),
            pl.BlockSpec((1, d), lambda i: (0, 0)),
            pl.BlockSpec((1, d), lambda i: (0, 1)),
        ],
        out_specs=pl.BlockSpec((tm, d), lambda i: (i, 0)),
        compiler_params=_cp("parallel"),
    )(x, ada, ada)


def _mm_cols(a, b, col_off, n_cols, out_dtype, name, ride=()):
    m, k = a.shape
    tm = min(1024, m)
    tn = next(t for t in (1024, 512, 128) if n_cols % t == 0 and col_off % t == 0)
    off = col_off // tn
    ni, nj = m // tm, n_cols // tn
    n = len(ride)

    def body(a_ref, b_ref, *rest):
        ins, o_ref, outs, sems = rest[:n], rest[n], rest[n + 1 : 2 * n + 1], rest[2 * n + 1 :]
        i, j = pl.program_id(0), pl.program_id(1)
        if n:

            @pl.when((i == 0) & (j == 0))
            def _():
                _rider_start("gather", ins, outs, *sems)

        o_ref[...] = lax.dot_general(a_ref[...], b_ref[...], _NT, preferred_element_type=F32).astype(out_dtype)
        if n:

            @pl.when((i == ni - 1) & (j == nj - 1))
            def _():
                _rider_wait("gather", ins, outs, *sems)

    hbm = pl.BlockSpec(memory_space=pltpu.HBM)
    out = _pcall(
        body,
        name=name,
        grid=(ni, nj),
        out_shape=[jax.ShapeDtypeStruct((m, n_cols), out_dtype)]
        + [jax.ShapeDtypeStruct((N_DEV * r.shape[0], r.shape[1]), r.dtype) for r in ride],
        in_specs=[pl.BlockSpec((tm, k), lambda i, j: (i, 0)), pl.BlockSpec((tn, k), lambda i, j: (off + j, 0))] + [hbm] * n,
        out_specs=[pl.BlockSpec((tm, tn), lambda i, j: (i, j))] + [hbm] * n,
        scratch_shapes=_rider_scratch(n) if n else [],
        compiler_params=_cp("arbitrary", "arbitrary") if n else _cp("parallel", "parallel"),
    )(a, b, *ride)
    return out if n else out[0]


def _mm_tn(a, b, name, ride=()):
    s_len, m = a.shape
    n = b.shape[1]
    tm, tn, ts = min(1024, m), min(2048, n), min(2048, s_len)
    ni, nj, ns = m // tm, n // tn, s_len // ts
    nr = len(ride)

    def body(a_ref, b_ref, *rest):
        ins, o_ref, outs = rest[:nr], rest[nr], rest[nr + 1 : 2 * nr + 1]
        sems, acc_s = rest[2 * nr + 1 : -1], rest[-1]
        i, j, kk = pl.program_id(0), pl.program_id(1), pl.program_id(2)
        if nr:

            @pl.when((i == 0) & (j == 0) & (kk == 0))
            def _():
                _rider_start("exchange", ins, outs, *sems)

            @pl.when((i == ni - 1) & (j == nj - 1) & (kk == ns - 1))
            def _():
                _rider_wait("exchange", ins, outs, *sems)

        part = lax.dot_general(a_ref[...], b_ref[...], _TN, preferred_element_type=F32)

        @pl.when(kk == 0)
        def _():
            acc_s[...] = part

        @pl.when(kk > 0)
        def _():
            acc_s[...] += part

        @pl.when(kk == ns - 1)
        def _():
            o_ref[...] = acc_s[...].astype(BF16)

    hbm = pl.BlockSpec(memory_space=pltpu.HBM)
    out = _pcall(
        body,
        name=name,
        grid=(ni, nj, ns),
        out_shape=[jax.ShapeDtypeStruct((m, n), BF16)] + [jax.ShapeDtypeStruct(r.shape, r.dtype) for r in ride],
        in_specs=[pl.BlockSpec((ts, tm), lambda i, j, kk: (kk, i)), pl.BlockSpec((ts, tn), lambda i, j, kk: (kk, j))] + [hbm] * nr,
        out_specs=[pl.BlockSpec((tm, tn), lambda i, j, kk: (i, j))] + [hbm] * nr,
        scratch_shapes=(_rider_scratch(nr) if nr else []) + [pltpu.VMEM((tm, tn), F32)],
        compiler_params=_cp("arbitrary", "arbitrary", "arbitrary") if nr else _cp("parallel", "parallel", "arbitrary"),
    )(a, b, *ride)
    return out if nr else out[0]


def _split3(a):
    hi = a.astype(BF16)
    r1 = a - hi.astype(F32)
    mid = r1.astype(BF16)
    lo = (r1 - mid.astype(F32)).astype(BF16)
    return hi, mid, lo


def _dot_ones(a, tri):
    return sum(jnp.dot(t, tri, preferred_element_type=F32) for t in _split3(a))


def _log_sigmoid(x):
    return jnp.minimum(x, 0.0) - jnp.log1p(jnp.exp(-jnp.abs(x)))


def _fox_cum(flog_t, bf_col):
    s_len = flog_t.shape[1]

    def body(fl_ref, bf_ref, cum_ref):
        r = lax.broadcasted_iota(jnp.int32, (128, 128), 0)
        c = lax.broadcasted_iota(jnp.int32, (128, 128), 1)
        upper = (r <= c).astype(BF16)

        def step(t, carry):
            sl = pl.ds(pl.multiple_of(t * 128, 128), 128)
            lf = _log_sigmoid(fl_ref[:, sl] + bf_ref[...])
            cs = _dot_ones(lf, upper) + carry
            cum_ref[:, sl] = cs
            return cs[:, 127:128]

        lax.fori_loop(0, s_len // 128, step, jnp.zeros((FOX_H, 1), F32))

    return _pcall(body, name="fox_cum", out_shape=jax.ShapeDtypeStruct((FOX_H, s_len), F32))(flog_t, bf_col)


def _fox_gate_bwd(drow, dcol, flog_t, bf_col):
    s_len = flog_t.shape[1]
    n = s_len // 128

    def body(dr_ref, dc_ref, fl_ref, bf_ref, dfl_ref, dbf_ref):
        r = lax.broadcasted_iota(jnp.int32, (128, 128), 0)
        c = lax.broadcasted_iota(jnp.int32, (128, 128), 1)
        lower = (r >= c).astype(BF16)

        def step(t, carry):
            run, tot = carry
            sl = pl.ds(pl.multiple_of((n - 1 - t) * 128, 128), 128)
            rc = _dot_ones(dr_ref[:, sl] - dc_ref[:, sl], lower) + run
            dfl = rc * _sigmoid(-(fl_ref[:, sl] + bf_ref[...]))
            dfl_ref[:, sl] = dfl
            return rc[:, 0:1], tot + jnp.sum(dfl, axis=1, keepdims=True)

        zero = jnp.zeros((FOX_H, 1), F32)
        _, tot = lax.fori_loop(0, n, step, (zero, zero))
        dbf_ref[...] = jnp.broadcast_to(tot, (FOX_H, 128))

    return _pcall(
        body,
        name="fox_gate_bwd",
        out_shape=[jax.ShapeDtypeStruct((FOX_H, s_len), F32), jax.ShapeDtypeStruct((FOX_H, 128), F32)],
    )(drow, dcol, flog_t, bf_col)


def _diag_mask(blk, transposed=False):
    r = lax.broadcasted_iota(jnp.int32, (blk, blk), 0)
    c = lax.broadcasted_iota(jnp.int32, (blk, blk), 1)
    return c >= r if transposed else r >= c


_NT = (((1,), (1,)), ((), ()))
_TN = (((0,), (0,)), ((), ()))


def _fox_fwd(qkv, cum_row):
    s_len = qkv.shape[0]
    blk = min(ATT_BLK, s_len)
    nb = s_len // blk
    log2e = 1.4426950408889634

    def body(q_ref, k_ref, v_ref, c_ref, o_ref, lse_ref, mx_s, acc_s, u_s):
        i = pl.program_id(1)

        def key_cols(j, n):
            return pl.ds(pl.multiple_of(j * blk, blk), n * blk)

        def walk(tile):
            def four_pairs(t, carry):
                for u in range(4):
                    tile(8 * t + 2 * u, 2, False)
                return carry

            lax.fori_loop(0, i // 8, four_pairs, 0)

            @pl.when((i // 4) % 2 == 1)
            def _():
                tile(8 * (i // 8), 2, False)
                tile(8 * (i // 8) + 2, 2, False)

            @pl.when((i // 2) % 2 == 1)
            def _():
                tile(4 * (i // 4), 2, False)

            @pl.when(i % 2 == 1)
            def _():
                tile(i - 1, 1, False)

            tile(i, 1, True)

        def lane_max(j, n, masked):
            cols = key_cols(j, n)
            u = lax.dot_general(q_ref[...], k_ref[cols, :], _NT, preferred_element_type=F32) * (FOX_SCALE * log2e) - c_ref[:, cols] * log2e
            if masked:
                u = jnp.where(_diag_mask(blk), u, NEG)
            u_s[:, cols] = u
            part = u[:, 0:128]
            for t in range(1, n * blk // 128):
                part = jnp.maximum(part, u[:, t * 128 : (t + 1) * 128])
            mx_s[...] = jnp.maximum(mx_s[...], part)

        mx_s[...] = jnp.full(mx_s.shape, NEG, F32)
        walk(lane_max)
        m = jnp.max(mx_s[...], axis=1, keepdims=True)

        def weigh(j, n, masked):
            cols = key_cols(j, n)
            p = jnp.exp2(u_s[:, cols] - m)
            ones_col = (lax.broadcasted_iota(jnp.int32, (n * blk, 128), 1) == 0).astype(BF16)
            v1 = jnp.concatenate([v_ref[cols, :], ones_col], axis=1)
            acc_s[...] += jnp.dot(p.astype(BF16), v1, preferred_element_type=F32)

        acc_s[...] = jnp.zeros(acc_s.shape, F32)
        walk(weigh)
        l = acc_s[:, FOX_DH : FOX_DH + 1]
        o_ref[...] = acc_s[:, :FOX_DH] / l
        lse_ref[...] = m * (1.0 / log2e) + jnp.log(l)

    return _pcall(
        body,
        name="fox_fwd",
        grid=(FOX_H, nb),
        out_shape=[jax.ShapeDtypeStruct((s_len, FOX_W), F32), jax.ShapeDtypeStruct((FOX_H, s_len, 1), F32)],
        in_specs=[
            pl.BlockSpec((blk, FOX_DH), lambda h, i: (i, h)),
            pl.BlockSpec((s_len, FOX_DH), lambda h, i: (0, FOX_H + h)),
            pl.BlockSpec((s_len, FOX_DH), lambda h, i: (0, 2 * FOX_H + h)),
            pl.BlockSpec((None, 1, s_len), lambda h, i: (h, 0, 0)),
        ],
        out_specs=[
            pl.BlockSpec((blk, FOX_DH), lambda h, i: (i, h)),
            pl.BlockSpec((None, blk, 1), lambda h, i: (h, i, 0)),
        ],
        scratch_shapes=[pltpu.VMEM((blk, 128), F32), pltpu.VMEM((blk, 2 * FOX_DH), F32), pltpu.VMEM((blk, s_len), F32)],
        compiler_params=_cp("parallel", "arbitrary"),
    )(qkv, qkv, qkv, cum_row)


def _fox_bwd(qkv, cum_col, lse_row, delta_row, do):
    s_len = qkv.shape[0]
    blk = min(ATT_BLK, s_len)
    nb = s_len // blk

    def body(q_ref, k_ref, v_ref, c_ref, lse_ref, dl_ref, do_ref, dq_ref, dk_ref, dv_ref, dc_ref, dr_ref, dk_s, dv_s, dc_s, cb_s, dq_s):
        j = pl.program_id(1)

        @pl.when(j == 0)
        def _():
            dq_s[...] = jnp.zeros(dq_s.shape, F32)
            dr_ref[...] = jnp.zeros(dr_ref.shape, F32)

        dk_s[...] = jnp.zeros(dk_s.shape, F32)
        dv_s[...] = jnp.zeros(dv_s.shape, F32)
        dc_s[...] = jnp.zeros(dc_s.shape, F32)
        cb_s[...] = jnp.broadcast_to(c_ref[...], cb_s.shape)

        def tile(i, n, diag):
            rows = pl.ds(pl.multiple_of(i * blk, blk), n * blk)
            q, dob = q_ref[rows, :], do_ref[rows, :]
            k, v = k_ref[...], v_ref[...]
            s_t = lax.dot_general(k, q, _NT, preferred_element_type=F32) * FOX_SCALE - cb_s[:, : n * blk]
            p_t = jnp.exp(s_t - lse_ref[:, rows])
            if diag:
                p_t = jnp.where(_diag_mask(blk, transposed=True), p_t, 0.0)
            dp_t = lax.dot_general(v, dob, _NT, preferred_element_type=F32)
            ds_t = p_t * (dp_t - dl_ref[:, rows])
            dsb = ds_t.astype(BF16)
            dv_s[...] += jnp.dot(p_t.astype(BF16), dob, preferred_element_type=F32)
            dk_s[...] += jnp.dot(dsb, q, preferred_element_type=F32)
            dq_c = lax.dot_general(dsb, k, _TN, preferred_element_type=F32)
            part = ds_t[:, 0:128]
            for t in range(1, n * blk // 128):
                part = part + ds_t[:, t * 128 : (t + 1) * 128]
            dc_s[...] += part
            dr_ref[:, rows] += jnp.sum(ds_t, axis=0, keepdims=True)
            if diag:
                dq_s[rows, :] = (dq_s[rows, :] + dq_c) * FOX_SCALE
            else:
                dq_s[rows, :] += dq_c

        tile(j, 1, True)
        below = nb - 1 - j
        b0, b1, b2 = below % 2, (below // 2) % 2, (below // 4) % 2

        @pl.when(b0 == 1)
        def _():
            tile(j + 1, 1, False)

        @pl.when(b1 == 1)
        def _():
            tile(j + 1 + b0, 2, False)

        @pl.when(b2 == 1)
        def _():
            tile(j + 1 + b0 + 2 * b1, 2, False)
            tile(j + 3 + b0 + 2 * b1, 2, False)

        first = j + 1 + b0 + 2 * b1 + 4 * b2

        def four_pairs(t, carry):
            for u in range(4):
                tile(first + 8 * t + 2 * u, 2, False)
            return carry

        lax.fori_loop(0, below // 8, four_pairs, 0)
        dk_ref[...] = (dk_s[...] * FOX_SCALE).astype(BF16)
        dv_ref[...] = dv_s[...].astype(BF16)
        dc_ref[...] = jnp.sum(dc_s[...], axis=1, keepdims=True)

        @pl.when(j == nb - 1)
        def _():
            dq_ref[...] = dq_s[...].astype(BF16)

    head = lambda h, j: (0, h)
    row = pl.BlockSpec((None, 1, s_len), lambda h, j: (h, 0, 0))
    return _pcall(
        body,
        name="fox_bwd",
        grid=(FOX_H, nb),
        out_shape=[
            jax.ShapeDtypeStruct((s_len, FOX_W), BF16),
            jax.ShapeDtypeStruct((s_len, FOX_W), BF16),
            jax.ShapeDtypeStruct((s_len, FOX_W), BF16),
            jax.ShapeDtypeStruct((FOX_H, s_len, 1), F32),
            jax.ShapeDtypeStruct((FOX_H, 1, s_len), F32),
        ],
        in_specs=[
            pl.BlockSpec((s_len, FOX_DH), head),
            pl.BlockSpec((blk, FOX_DH), lambda h, j: (j, FOX_H + h)),
            pl.BlockSpec((blk, FOX_DH), lambda h, j: (j, 2 * FOX_H + h)),
            pl.BlockSpec((None, blk, 1), lambda h, j: (h, j, 0)),
            row,
            row,
            pl.BlockSpec((s_len, FOX_DH), head),
        ],
        out_specs=[
            pl.BlockSpec((s_len, FOX_DH), head),
            pl.BlockSpec((blk, FOX_DH), lambda h, j: (j, h)),
            pl.BlockSpec((blk, FOX_DH), lambda h, j: (j, h)),
            pl.BlockSpec((None, blk, 1), lambda h, j: (h, j, 0)),
            row,
        ],
        scratch_shapes=[
            pltpu.VMEM((blk, FOX_DH), F32),
            pltpu.VMEM((blk, FOX_DH), F32),
            pltpu.VMEM((blk, 128), F32),
            pltpu.VMEM((blk, 2 * blk), F32),
            pltpu.VMEM((s_len, FOX_DH), F32),
        ],
        compiler_params=_cp("parallel", "arbitrary"),
    )(qkv, qkv, qkv, cum_col, lse_row, delta_row, do)


def _swa_bias():
    cols = SWA_G * WINDOW
    k = np.arange(2 * WINDOW)[:, None]
    q = np.arange(cols)[None, :]
    dist = (q % WINDOW) - k + WINDOW
    valid = (dist >= 0) & (dist < WINDOW)
    out = np.empty((2, SWA_HKV, 2 * WINDOW, cols), np.float32)
    for g in range(SWA_HKV):
        slope = np.array([SLOPES[g * SWA_G + t] for t in range(SWA_G)], np.float32)[q // WINDOW]
        bias = -(slope * dist.astype(np.float32))
        out[0, g] = np.where(valid & (k >= WINDOW), bias, np.float32(NEG))
        out[1, g] = np.where(valid, bias, np.float32(NEG))
    return jnp.asarray(out)


def _swa_group(i, q_ref, kk, sinks_ref, bias_ref, g):
    cols = SWA_G * WINDOW
    head = lax.broadcasted_iota(jnp.int32, (1, cols), 1) // WINDOW
    sink = jnp.zeros((1, cols), F32)
    for t in range(SWA_G):
        sink = jnp.where(head == t, sinks_ref[g * SWA_G + t], sink)
    q = jnp.concatenate([q_ref[:, (g * SWA_G + t) * SWA_DH : (g * SWA_G + t + 1) * SWA_DH] for t in range(SWA_G)], axis=0)
    k = kk[:, g * SWA_DH : (g + 1) * SWA_DH]
    s = lax.dot_general(k, q, _NT, preferred_element_type=F32) * SWA_SCALE + bias_ref[jnp.minimum(i, 1), g]
    m = jnp.maximum(jnp.max(s, axis=0, keepdims=True), sink)
    e = jnp.exp(s - m)
    e_sink = jnp.exp(sink - m)
    inv = 1.0 / (jnp.sum(e, axis=0, keepdims=True) + e_sink)
    return q, k, e * inv, e_sink * inv


def _swa_specs(col_q, col_k, col_v, rev, nb):
    def blk(t):
        return nb - 1 - t if rev else t

    return [
        pl.BlockSpec((WINDOW, SWA_W), lambda t: (blk(t), col_q)),
        pl.BlockSpec((WINDOW, SWA_KVW), lambda t: (jnp.maximum(blk(t) - 1, 0), col_k)),
        pl.BlockSpec((WINDOW, SWA_KVW), lambda t: (blk(t), col_k)),
        pl.BlockSpec((WINDOW, SWA_KVW), lambda t: (jnp.maximum(blk(t) - 1, 0), col_v)),
        pl.BlockSpec((WINDOW, SWA_KVW), lambda t: (blk(t), col_v)),
    ]


def _swa_fwd(qkv, sinks):
    s_len = qkv.shape[0]
    nb = s_len // WINDOW
    bias_spec = pl.BlockSpec((2, SWA_HKV, 2 * WINDOW, SWA_G * WINDOW), lambda t: (0, 0, 0, 0))

    def body(q_ref, kp_ref, kc_ref, vp_ref, vc_ref, sinks_ref, bias_ref, o_ref):
        i = pl.program_id(0)
        kk = jnp.concatenate([kp_ref[...], kc_ref[...]], axis=0)
        vv = jnp.concatenate([vp_ref[...], vc_ref[...]], axis=0)
        for g in range(SWA_HKV):
            _, _, p, _ = _swa_group(i, q_ref, kk, sinks_ref, bias_ref, g)
            o = lax.dot_general(p.astype(BF16), vv[:, g * SWA_DH : (g + 1) * SWA_DH], _TN, preferred_element_type=F32)
            for t in range(SWA_G):
                h = g * SWA_G + t
                o_ref[:, h * SWA_DH : (h + 1) * SWA_DH] = o[t * WINDOW : (t + 1) * WINDOW, :]

    return _pcall(
        body,
        name="swa_fwd",
        grid=(nb,),
        out_shape=jax.ShapeDtypeStruct((s_len, SWA_W), F32),
        in_specs=_swa_specs(0, 4, 5, False, nb) + [pl.BlockSpec(memory_space=pltpu.SMEM), bias_spec],
        out_specs=pl.BlockSpec((WINDOW, SWA_W), lambda t: (t, 0)),
        compiler_params=_cp("parallel"),
    )(qkv, qkv, qkv, qkv, qkv, sinks, _swa_bias())


def _swa_bwd(qkv, sinks, do):
    s_len = qkv.shape[0]
    nb = s_len // WINDOW
    bias_spec = pl.BlockSpec((2, SWA_HKV, 2 * WINDOW, SWA_G * WINDOW), lambda t: (0, 0, 0, 0))

    def body(q_ref, kp_ref, kc_ref, vp_ref, vc_ref, sinks_ref, bias_ref, do_ref, dq_ref, dk_ref, dv_ref, dsink_ref, ck_s, cv_s, dkk_s, dvv_s):
        t = pl.program_id(0)
        i = nb - 1 - t

        @pl.when(t == 0)
        def _():
            ck_s[...] = jnp.zeros(ck_s.shape, F32)
            cv_s[...] = jnp.zeros(cv_s.shape, F32)
            dsink_ref[...] = jnp.zeros(dsink_ref.shape, F32)

        kk = jnp.concatenate([kp_ref[...], kc_ref[...]], axis=0)
        vv = jnp.concatenate([vp_ref[...], vc_ref[...]], axis=0)
        lane = lax.broadcasted_iota(jnp.int32, (1, 128), 1)
        dsink = jnp.zeros((1, 128), F32)
        for g in range(SWA_HKV):
            cols = slice(g * SWA_DH, (g + 1) * SWA_DH)
            q, k, p, p_sink = _swa_group(i, q_ref, kk, sinks_ref, bias_ref, g)
            dob = jnp.concatenate([do_ref[:, (g * SWA_G + t) * SWA_DH : (g * SWA_G + t + 1) * SWA_DH] for t in range(SWA_G)], axis=0)
            dp = lax.dot_general(vv[:, cols], dob, _NT, preferred_element_type=F32)
            delta = jnp.sum(p * dp, axis=0, keepdims=True)
            dsb = (p * (dp - delta)).astype(BF16)
            dq = (lax.dot_general(dsb, k, _TN, preferred_element_type=F32) * SWA_SCALE).astype(BF16)
            ps_d = p_sink * delta
            for t in range(SWA_G):
                h = g * SWA_G + t
                dq_ref[:, h * SWA_DH : (h + 1) * SWA_DH] = dq[t * WINDOW : (t + 1) * WINDOW, :]
                dsink = dsink + jnp.where(lane == h, -jnp.sum(ps_d[:, t * WINDOW : (t + 1) * WINDOW], axis=1, keepdims=True), 0.0)
            dkk_s[:, cols] = jnp.dot(dsb, q, preferred_element_type=F32) * SWA_SCALE
            dvv_s[:, cols] = jnp.dot(p.astype(BF16), dob, preferred_element_type=F32)
        dk_ref[...] = (dkk_s[WINDOW:, :] + ck_s[...]).astype(BF16)
        dv_ref[...] = (dvv_s[WINDOW:, :] + cv_s[...]).astype(BF16)
        ck_s[...] = dkk_s[:WINDOW, :]
        cv_s[...] = dvv_s[:WINDOW, :]
        dsink_ref[...] += dsink

    row = lambda t: (nb - 1 - t, 0)
    return _pcall(
        body,
        name="swa_bwd",
        grid=(nb,),
        out_shape=[
            jax.ShapeDtypeStruct((s_len, SWA_W), BF16),
            jax.ShapeDtypeStruct((s_len, SWA_KVW), BF16),
            jax.ShapeDtypeStruct((s_len, SWA_KVW), BF16),
            jax.ShapeDtypeStruct((1, 128), F32),
        ],
        in_specs=_swa_specs(0, 4, 5, True, nb)
        + [pl.BlockSpec(memory_space=pltpu.SMEM), bias_spec, pl.BlockSpec((WINDOW, SWA_W), row)],
        out_specs=[
            pl.BlockSpec((WINDOW, SWA_W), row),
            pl.BlockSpec((WINDOW, SWA_KVW), row),
            pl.BlockSpec((WINDOW, SWA_KVW), row),
            pl.BlockSpec((1, 128), lambda t: (0, 0)),
        ],
        scratch_shapes=[
            pltpu.VMEM((WINDOW, SWA_KVW), F32),
            pltpu.VMEM((WINDOW, SWA_KVW), F32),
            pltpu.VMEM((2 * WINDOW, SWA_KVW), F32),
            pltpu.VMEM((2 * WINDOW, SWA_KVW), F32),
        ],
        compiler_params=_cp("arbitrary"),
    )(qkv, qkv, qkv, qkv, qkv, sinks, _swa_bias(), do)


def _branch_fwd(o, gates, g_blk, w_b, name):
    s_len, wd = o.shape
    d = w_b.shape[1]
    tm = min(1024, s_len)

    def body(o_ref, g_ref, w_ref, y_ref, a_ref):
        g = g_ref[...].astype(F32)
        a = (o_ref[...] * (g * _sigmoid(g))).astype(BF16)
        a_ref[...] = a
        y_ref[...] = jnp.dot(a, w_ref[...], preferred_element_type=F32).astype(BF16)

    return _pcall(
        body,
        name=name,
        grid=(s_len // tm,),
        out_shape=[jax.ShapeDtypeStruct((s_len, d), BF16), jax.ShapeDtypeStruct((s_len, wd), BF16)],
        in_specs=[
            pl.BlockSpec((tm, wd), lambda i: (i, 0)),
            pl.BlockSpec((tm, wd), lambda i: (i, g_blk)),
            pl.BlockSpec((wd, d), lambda i: (0, 0)),
        ],
        out_specs=[pl.BlockSpec((tm, d), lambda i: (i, 0)), pl.BlockSpec((tm, wd), lambda i: (i, 0))],
        compiler_params=_cp("parallel"),
    )(o, gates, w_b)


def _out_stage(gates, mf_blk, y_fox, y_swa, w_out, x, ada, ln_g, ln_b, target):
    s_len, d = x.shape
    tm = min(256, s_len)
    n_steps = s_len // tm

    def body(mf_ref, ms_ref, yf_ref, ys_ref, w_ref, x_ref, gate_ref, lg_ref, lb_ref, t_ref, mg_ref, dza_ref, dsub_ref, red_ref):
        i = pl.program_id(0)
        merged = _sigmoid(mf_ref[...].astype(F32)) * yf_ref[...].astype(F32) + _sigmoid(ms_ref[...].astype(F32)) * ys_ref[...].astype(F32)
        mb = merged.astype(BF16)
        mg_ref[...] = mb
        sub = jnp.dot(mb, w_ref[...], preferred_element_type=F32)
        gate = gate_ref[...]
        z = ALPHA * x_ref[...] + gate * sub
        mu = jnp.mean(z, axis=-1, keepdims=True)
        zc = z - mu
        var = jnp.mean(zc * zc, axis=-1, keepdims=True)
        rstd = lax.rsqrt(var + LN_EPS)
        zhat = zc * rstd
        err = zhat * lg_ref[...] + lb_ref[...] - t_ref[...]
        dout = err * (1.0 / d)
        dzhat = dout * lg_ref[...]
        dz = rstd * (dzhat - jnp.mean(dzhat, axis=-1, keepdims=True) - zhat * jnp.mean(dzhat * zhat, axis=-1, keepdims=True))
        dza_ref[...] = ALPHA * dz
        dsub_ref[...] = (gate * dz).astype(BF16)
        part = jnp.concatenate(
            [
                jnp.sum(dz * sub, axis=0, keepdims=True),
                jnp.sum(dout * zhat, axis=0, keepdims=True),
                jnp.sum(dout, axis=0, keepdims=True),
                jnp.sum(err * err, axis=0, keepdims=True),
                jnp.zeros((4, d), F32),
            ],
            axis=0,
        )

        @pl.when(i == 0)
        def _():
            red_ref[...] = part

        @pl.when(i > 0)
        def _():
            red_ref[...] += part

        @pl.when(i == n_steps - 1)
        def _():
            red_ref[4:5, :] = jnp.broadcast_to(jnp.sum(red_ref[3:4, :], axis=1, keepdims=True), (1, d))

    row = pl.BlockSpec((tm, d), lambda i: (i, 0))
    vec = pl.BlockSpec((1, d), lambda i: (0, 0))
    return _pcall(
        body,
        name="out_stage",
        grid=(n_steps,),
        out_shape=[
            jax.ShapeDtypeStruct((s_len, d), BF16),
            jax.ShapeDtypeStruct((s_len, d), F32),
            jax.ShapeDtypeStruct((s_len, d), BF16),
            jax.ShapeDtypeStruct((8, d), F32),
        ],
        in_specs=[
            pl.BlockSpec((tm, d), lambda i: (i, mf_blk)),
            pl.BlockSpec((tm, d), lambda i: (i, mf_blk + 1)),
            row,
            row,
            pl.BlockSpec((d, d), lambda i: (0, 0), pipeline_mode=pl.Buffered(1)),
            row,
            pl.BlockSpec((1, d), lambda i: (0, 2)),
            vec,
            vec,
            row,
        ],
        out_specs=[row, row, row, pl.BlockSpec((8, d), lambda i: (0, 0))],
        compiler_params=_cp("arbitrary"),
    )(gates, gates, y_fox, y_swa, w_out, x, ada, ln_g, ln_b, target)


def _merge_bwd(dsub, w_out, gates, mf_blk, y_fox, y_swa):
    s_len, d = dsub.shape
    tm = min(512, s_len)

    def body(ds_ref, w_ref, mf_ref, ms_ref, yf_ref, ys_ref, dmf_ref, dms_ref, dyf_ref, dys_ref):
        dm = lax.dot_general(ds_ref[...], w_ref[...], _NT, preferred_element_type=F32)
        sf, ss = _sigmoid(mf_ref[...].astype(F32)), _sigmoid(ms_ref[...].astype(F32))
        dmf_ref[...] = (dm * yf_ref[...].astype(F32) * (sf * (1.0 - sf))).astype(BF16)
        dms_ref[...] = (dm * ys_ref[...].astype(F32) * (ss * (1.0 - ss))).astype(BF16)
        dyf_ref[...] = (dm * sf).astype(BF16)
        dys_ref[...] = (dm * ss).astype(BF16)

    row = pl.BlockSpec((tm, d), lambda i: (i, 0))
    return _pcall(
        body,
        name="merge_bwd",
        grid=(s_len // tm,),
        out_shape=[jax.ShapeDtypeStruct((s_len, d), BF16)] * 4,
        in_specs=[
            row,
            pl.BlockSpec((d, d), lambda i: (0, 0), pipeline_mode=pl.Buffered(1)),
            pl.BlockSpec((tm, d), lambda i: (i, mf_blk)),
            pl.BlockSpec((tm, d), lambda i: (i, mf_blk + 1)),
            row,
            row,
        ],
        out_specs=[row] * 4,
        compiler_params=_cp("parallel"),
    )(dsub, w_out, gates, gates, y_fox, y_swa)


def _branch_bwd(dy, w_b, o, gates, g_blk, name, n_heads):
    s_len, d = dy.shape
    wd = w_b.shape[0]
    tm = min(1024, s_len)

    def body(dy_ref, w_ref, o_ref, g_ref, do_ref, dg_ref, *rest):
        da = lax.dot_general(dy_ref[...], w_ref[...], _NT, preferred_element_type=F32)
        g = g_ref[...].astype(F32)
        sg = _sigmoid(g)
        do = da * (g * sg)
        do_ref[...] = do.astype(BF16)
        o = o_ref[...]
        dg_ref[...] = (da * o * (sg * (1.0 + g * (1.0 - sg)))).astype(BF16)
        if n_heads:
            prod = do.astype(BF16).astype(F32) * o
            lane = lax.broadcasted_iota(jnp.int32, (1, 128), 1)
            delta = jnp.zeros((tm, 128), F32)
            for h in range(n_heads):
                dh = jnp.sum(prod[:, h * 128 : (h + 1) * 128], axis=1, keepdims=True)
                delta = delta + jnp.where(lane == h, dh, 0.0)
            rest[0][...] = delta

    out_shape = [jax.ShapeDtypeStruct((s_len, wd), BF16), jax.ShapeDtypeStruct((s_len, wd), BF16)]
    out_specs = [pl.BlockSpec((tm, wd), lambda i: (i, 0))] * 2
    if n_heads:
        out_shape.append(jax.ShapeDtypeStruct((s_len, 128), F32))
        out_specs.append(pl.BlockSpec((tm, 128), lambda i: (i, 0)))
    return _pcall(
        body,
        name=name,
        grid=(s_len // tm,),
        out_shape=out_shape,
        in_specs=[
            pl.BlockSpec((tm, d), lambda i: (i, 0)),
            pl.BlockSpec((wd, d), lambda i: (0, 0)),
            pl.BlockSpec((tm, wd), lambda i: (i, 0)),
            pl.BlockSpec((tm, wd), lambda i: (i, g_blk)),
        ],
        out_specs=out_specs,
        compiler_params=_cp("parallel"),
    )(dy, w_b, o, gates)


def _in_bwd(dproj, w_in_t, x, ada, dza, ride):
    s_len, d = x.shape
    k_tot = dproj.shape[1]
    tm, tk, dn = min(512, s_len), k_tot // 4, d // 2
    ni, nk = s_len // tm, k_tot // tk
    n = len(ride)

    def body(dp_ref, w_ref, x_ref, sc_ref, dza_ref, *rest):
        ins, (gx_ref, red_ref), outs = rest[:n], rest[n : n + 2], rest[n + 2 : 2 * n + 2]
        sems, acc_s = rest[2 * n + 2 : 2 * n + 5], rest[2 * n + 5]
        i, nh, kk = pl.program_id(0), pl.program_id(1), pl.program_id(2)

        @pl.when((i == 0) & (nh == 0) & (kk == 0))
        def _():
            _rider_start("exchange", ins, outs, *sems)

        @pl.when((i == ni - 1) & (nh == 1) & (kk == nk - 1))
        def _():
            _rider_wait("exchange", ins, outs, *sems)

        part = jnp.dot(dp_ref[...], w_ref[...], preferred_element_type=F32)
        half = pl.ds(pl.multiple_of(nh * dn, dn), dn)

        @pl.when(kk == 0)
        def _():
            acc_s[:, half] = part

        @pl.when(kk > 0)
        def _():
            acc_s[:, half] += part

        @pl.when((nh == 1) & (kk == nk - 1))
        def _():
            dh = acc_s[...]
            xv = x_ref[...]
            mu = jnp.mean(xv, axis=-1, keepdims=True)
            xc = xv - mu
            var = jnp.mean(xc * xc, axis=-1, keepdims=True)
            rstd = lax.rsqrt(var + LN_EPS)
            xhat = xc * rstd
            dxhat = dh * (1.0 + sc_ref[...])
            dx = rstd * (dxhat - jnp.mean(dxhat, axis=-1, keepdims=True) - xhat * jnp.mean(dxhat * xhat, axis=-1, keepdims=True))
            gx_ref[...] = dza_ref[...] + dx
            part_r = jnp.concatenate(
                [jnp.sum(dh, axis=0, keepdims=True), jnp.sum(dh * xhat, axis=0, keepdims=True), jnp.zeros((6, d), F32)], axis=0
            )

            @pl.when(i == 0)
            def _():
                red_ref[...] = part_r

            @pl.when(i > 0)
            def _():
                red_ref[...] += part_r

    row = pl.BlockSpec((tm, d), lambda i, nh, kk: (i, 0))
    hbm = pl.BlockSpec(memory_space=pltpu.HBM)
    return _pcall(
        body,
        name="in_bwd",
        grid=(ni, 2, nk),
        out_shape=[jax.ShapeDtypeStruct((s_len, d), F32), jax.ShapeDtypeStruct((8, d), F32)]
        + [jax.ShapeDtypeStruct(r.shape, r.dtype) for r in ride],
        in_specs=[
            pl.BlockSpec((tm, tk), lambda i, nh, kk: (i, kk)),
            pl.BlockSpec((tk, dn), lambda i, nh, kk: (kk, nh)),
            row,
            pl.BlockSpec((1, d), lambda i, nh, kk: (0, 1)),
            row,
        ]
        + [hbm] * n,
        out_specs=[row, pl.BlockSpec((8, d), lambda i, nh, kk: (0, 0))] + [hbm] * n,
        scratch_shapes=_rider_scratch(n) + [pltpu.VMEM((tm, d), F32)],
        compiler_params=_cp("arbitrary", "arbitrary", "arbitrary"),
    )(dproj, w_in_t, x, ada, dza, *ride)


def _pad_lanes(v, n):
    return jnp.pad(v, ((0, 0), (0, n - v.shape[1])))


def kernel(x, c, w_ada, b_ada, w_in, b_f, attn_sinks, w_br_fox, w_br_swa, w_out, ln_g, ln_b, loss_target, m_w_ada, m_b_ada, m_w_in, m_b_f, m_attn_sinks, m_w_br_fox, m_w_br_swa, m_w_out, m_ln_g, m_ln_b, v_w_ada, v_b_ada, v_w_in, v_b_f, v_attn_sinks, v_w_br_fox, v_w_br_swa, v_w_out, v_ln_g, v_ln_b):
    x2, tgt = x[0], loss_target[0]
    s_len, d = x2.shape
    me = 4 * lax.axis_index("x") + 2 * lax.axis_index("y") + lax.axis_index("c")
    off_ms = OFF_MF + d
    in_pad = off_ms + d
    c_ada = w_ada.shape[2]
    c_in = w_in.shape[2]
    c_br = w_br_fox.shape[2]

    w_in_full = _all_gather(w_in[0].T.astype(BF16), "ag_w_in", pltpu.HBM).reshape(N_DEV * c_in, d)
    w_in_pad = jnp.concatenate(
        [w_in_full[:REAL_FLOG_END], jnp.zeros((FLOG_PAD - N_FLOG, d), BF16), w_in_full[REAL_FLOG_END:]], axis=0
    )
    k_cut = REAL_FLOG_END // c_in

    c_all = _gather_rows(c, "ag_c")
    b_cols = lax.dynamic_slice(b_ada, (0, me * c_ada), (1, c_ada))
    ada_cols = _ada_fwd(c_all, w_ada[0], b_cols)
    ada_g = _all_gather(ada_cols, "ag_ada", pltpu.VMEM)
    ada = lax.dynamic_index_in_dim(ada_g, me, axis=1, keepdims=False).reshape(1, N_DEV * c_ada)

    h = _ln_mod(x2, ada)
    qkv_fox = _mm_cols(h, w_in_pad, OFF_FQ, 3 * FOX_W, BF16, "proj_fox")
    flog = _mm_cols(h, w_in_pad, OFF_FLOG, 128, F32, "proj_flog")
    qkv_swa = _mm_cols(h, w_in_pad, OFF_SQ, SWA_W + 2 * SWA_KVW, BF16, "proj_swa")
    gates, w_bf, w_bs, w_o = _mm_cols(
        h, w_in_pad, OFF_GF, in_pad - OFF_GF, BF16, "proj_gates",
        ride=(w_br_fox[0].astype(BF16), w_br_swa[0].astype(BF16), w_out[0].astype(BF16)),
    )
    w_bf = w_bf.reshape(N_DEV, FOX_W, c_br).transpose(1, 0, 2).reshape(FOX_W, d)
    w_bs = w_bs.reshape(N_DEV, SWA_W, c_br).transpose(1, 0, 2).reshape(SWA_W, d)
    w_o = w_o.reshape(d, d)
    mf_blk = (OFF_MF - OFF_GF) // d

    flog_t = flog[:, :N_FLOG].T
    bf_col = b_f.reshape(FOX_H, 1)
    cum = _fox_cum(flog_t, bf_col)
    cum_row = cum.reshape(FOX_H, 1, s_len)
    o_fox, lse = _fox_fwd(qkv_fox, cum_row)
    sinks = attn_sinks.reshape(SWA_HQ)
    o_swa = _swa_fwd(qkv_swa, sinks)

    y_fox, a_fox = _branch_fwd(o_fox, gates, 0, w_bf, "branch_fox")
    y_swa, a_swa = _branch_fwd(o_swa, gates, 1, w_bs, "branch_swa")
    merged, dza, dsub, red = _out_stage(gates, mf_blk, y_fox, y_swa, w_o, x2, ada, ln_g, ln_b, tgt)
    loss = lax.psum(0.5 * red[4, 0] / d, ("x", "y", "c"))

    dmf, dms, dy_fox, dy_swa = _merge_bwd(dsub, w_o, gates, mf_blk, y_fox, y_swa)
    do_fox, dg_fox, delta = _branch_bwd(dy_fox, w_bf, o_fox, gates, 0, "branch_fox_bwd", FOX_H)
    do_swa, dg_swa = _branch_bwd(dy_swa, w_bs, o_swa, gates, 1, "branch_swa_bwd", 0)
    delta_row = delta[:, :FOX_H].T.reshape(FOX_H, 1, s_len)
    dq_f, dk_f, dv_f, dcol, drow = _fox_bwd(
        qkv_fox, cum.reshape(FOX_H, s_len, 1), lse.reshape(FOX_H, 1, s_len), delta_row, do_fox
    )
    dflog_t, dbf = _fox_gate_bwd(drow.reshape(FOX_H, s_len), dcol.reshape(FOX_H, s_len), flog_t, bf_col)
    dq_s, dk_s, dv_s, dsink = _swa_bwd(qkv_swa, sinks, do_swa)
    dflog = _pad_lanes(dflog_t.T, FLOG_PAD).astype(BF16)
    dproj = jnp.concatenate([dq_f, dk_f, dv_f, dflog, dq_s, dk_s, dv_s, dg_fox, dg_swa, dmf, dms], axis=1)
    g_w_bf = _mm_tn(a_fox, dy_fox, "grad_w_br_fox")
    g_w_bs = _mm_tn(a_swa, dy_swa, "grad_w_br_swa")
    g_w_o = _mm_tn(merged, dsub, "grad_w_out")
    g_w_in, r_bf, r_bs, r_o = _mm_tn(
        dproj, h, "grad_w_in",
        ride=(
            g_w_bf.reshape(FOX_W, N_DEV, c_br).transpose(1, 0, 2),
            g_w_bs.reshape(SWA_W, N_DEV, c_br).transpose(1, 0, 2),
            g_w_o.reshape(N_DEV, d // N_DEV, d),
        ),
    )
    pad = FLOG_PAD - N_FLOG
    g_blocks = jnp.stack(
        [g_w_in[k * c_in : (k + 1) * c_in] for k in range(k_cut)]
        + [jnp.concatenate([g_w_in[k_cut * c_in : REAL_FLOG_END], g_w_in[OFF_SQ : (k_cut + 1) * c_in + pad]], axis=0)]
        + [g_w_in[k * c_in + pad : (k + 1) * c_in + pad] for k in range(k_cut + 1, N_DEV)]
    )

    grad_x, red2, r_in = _in_bwd(dproj, w_in_pad, x2, ada, dza, ride=(g_blocks,))
    out_w_in = _sum_adam_t(r_in, w_in[0].T, m_w_in[0].T, v_w_in[0].T, "adam_w_in")
    out_w_in = [o.T for o in out_w_in]
    out_w_bf = _sum_adam(r_bf, w_br_fox[0], m_w_br_fox[0], v_w_br_fox[0], "adam_w_br_fox")
    out_w_bs = _sum_adam(r_bs, w_br_swa[0], m_w_br_swa[0], v_w_br_swa[0], "adam_w_br_swa")
    out_w_o = _sum_adam(r_o, w_out[0], m_w_out[0], v_w_out[0], "adam_w_out")

    packed = jnp.concatenate([red2[0:1], red2[1:2], red[0:1], _pad_lanes(dbf[:, 0].reshape(1, FOX_H), 128), dsink, red[1:2], red[2:3]], axis=1)
    gathered = _gather_rows(packed, "ag_small")
    pack = lambda a, b, cc, dd, e: jnp.concatenate([a, _pad_lanes(b, 128), _pad_lanes(cc, 128), dd, e], axis=1)
    small = _small_adam(
        gathered,
        pack(b_ada, b_f, attn_sinks, ln_g, ln_b),
        pack(m_b_ada, m_b_f, m_attn_sinks, m_ln_g, m_ln_b),
        pack(v_b_ada, v_b_f, v_attn_sinks, v_ln_g, v_ln_b),
    )
    dada_cols = lax.dynamic_slice(gathered, (0, me * c_ada), (N_DEV, c_ada))
    out_w_ada = _wada_adam(c_all.T, dada_cols, w_ada[0], m_w_ada[0], v_w_ada[0])

    o1, o2, o3 = 3 * d, 3 * d + 128, 3 * d + 256

    def unpack(p):
        return p[:, :o1], p[:, o1 : o1 + FOX_H], p[:, o2 : o2 + SWA_HQ], p[:, o3 : o3 + d], p[:, o3 + d : o3 + 2 * d]

    kinds = []
    for k in range(4):
        b_ada_k, b_f_k, sinks_k, ln_g_k, ln_b_k = unpack(small[k])
        kinds.append(
            [out_w_ada[k][None], b_ada_k, out_w_in[k][None], b_f_k, sinks_k, out_w_bf[k][None], out_w_bs[k][None], out_w_o[k][None], ln_g_k, ln_b_k]
        )
    return (loss, grad_x[None], *kinds[0], *kinds[1], *kinds[2], *kinds[3])
```

```python
import numpy as np
import jax
import jax.numpy as jnp
from jax import lax
from jax.experimental import pallas as pl
from jax.experimental.pallas import tpu as pltpu

F32 = jnp.float32
BF16 = jnp.bfloat16
N_DEV = 8
MESH = pl.DeviceIdType.MESH

FOX_H, FOX_DH, FOX_W = 8, 128, 1024
SWA_HQ, SWA_HKV, SWA_DH, SWA_G = 16, 4, 64, 4
SWA_W, SWA_KVW, WINDOW = 1024, 256, 128
LN_EPS = 1e-5
NEG = -1e30
DEPTH = 1
ALPHA = (2.0 * DEPTH) ** 0.25
FOX_SCALE = FOX_DH ** -0.5
SWA_SCALE = SWA_DH ** -0.5
SLOPES = [2.0 ** (-8.0 * (h + 1.0) / SWA_HQ) for h in range(SWA_HQ)]

ADAM_LR, ADAM_B1, ADAM_B2, ADAM_EPS, ADAM_WD, ADAM_STEP = 0.001, 0.9, 0.999, 1e-08, 0.01, 10

N_FLOG = 8
FLOG_PAD = 512
OFF_FQ, OFF_FK, OFF_FV, OFF_FLOG = 0, 1024, 2048, 3072
OFF_SQ = OFF_FLOG + FLOG_PAD
OFF_SK = OFF_SQ + SWA_W
OFF_SV = OFF_SK + SWA_KVW
OFF_GF = OFF_SV + SWA_KVW
OFF_GS = OFF_GF + FOX_W
OFF_MF = OFF_GS + SWA_W
REAL_FLOG_END = OFF_FLOG + N_FLOG

ATT_BLK = 512
VMEM_LIMIT = 58 * 1024 * 1024


def _pcall(body, **kw):
    return pl.pallas_call(body, **kw)


def _cp(*sem):
    return pltpu.CompilerParams(dimension_semantics=sem, vmem_limit_bytes=VMEM_LIMIT)


def _sigmoid(x):
    return 0.5 * jnp.tanh(0.5 * x) + 0.5


def _all_gather(x, name, space):
    m_per, n = x.shape

    def body(x_ref, out_ref, send_sems, recv_sems, local_sem):
        mx, my, mc = lax.axis_index("x"), lax.axis_index("y"), lax.axis_index("c")
        me, sibling = (mx, my, mc), (mx, my, 1 - mc)
        xn, yn, dg = (1 - mx, my), (mx, 1 - my), (1 - mx, 1 - my)
        south = mc == 0
        src_chip = (jnp.where(south, 1 - mx, mx), jnp.where(south, my, 1 - my))
        dst_chip = (jnp.where(south, mx, 1 - mx), jnp.where(south, 1 - my, my))

        def rows(px, py, pc):
            return out_ref.at[4 * px + 2 * py + pc]

        def copy(k, block, to, src=None):
            return pltpu.make_async_remote_copy(
                src_ref=rows(*block) if src is None else src,
                dst_ref=rows(*block),
                send_sem=send_sems.at[k],
                recv_sem=recv_sems.at[k],
                device_id=to,
                device_id_type=MESH,
            )

        mine = pltpu.make_async_copy(x_ref, rows(*me), local_sem)
        mine.start()
        first = [copy(0, me, sibling, src=x_ref), copy(1, me, (*xn, mc), src=x_ref), copy(2, me, (*yn, mc), src=x_ref)]
        for cp in first:
            cp.start()
        copy(1, (*xn, mc), me).wait_recv()
        copy(2, (*yn, mc), me).wait_recv()
        later = [copy(3, (*src_chip, mc), (*dst_chip, mc)), copy(4, (*xn, mc), sibling), copy(5, (*yn, mc), sibling)]
        for cp in later:
            cp.start()
        copy(3, (*dg, mc), me).wait_recv()
        last = copy(6, (*dg, mc), sibling)
        last.start()
        copy(0, sibling, me).wait_recv()
        for k, chip in ((4, xn), (5, yn), (6, dg)):
            copy(k, (*chip, 1 - mc), me).wait_recv()
        for cp in first + later + [last]:
            cp.wait_send()
        mine.wait()

    return _pcall(
        body,
        name=name,
        out_shape=jax.ShapeDtypeStruct((N_DEV, m_per, n), x.dtype),
        in_specs=[pl.BlockSpec(memory_space=space)],
        out_specs=pl.BlockSpec(memory_space=space),
        scratch_shapes=[pltpu.SemaphoreType.DMA((7,)), pltpu.SemaphoreType.DMA((7,)), pltpu.SemaphoreType.DMA],
    )(x)


def _peer(d, mx, my, mc):
    return (1 - mx if (d >> 2) & 1 else mx, 1 - my if (d >> 1) & 1 else my, 1 - mc if d & 1 else mc)


def _rider_copies(kind, ins, outs, send_sems, recv_sems, local_sems):
    mx, my, mc = lax.axis_index("x"), lax.axis_index("y"), lax.axis_index("c")
    me = 4 * mx + 2 * my + mc
    remote, local = [], []
    for a in range(len(ins)):
        if kind == "gather":
            m_per = ins[a].shape[0]
            mine = outs[a].at[pl.ds(me * m_per, m_per), :]
            local.append(pltpu.make_async_copy(ins[a], mine, local_sems.at[a]))
        else:
            local.append(pltpu.make_async_copy(ins[a].at[me], outs[a].at[0], local_sems.at[a]))
        for d in range(1, N_DEV):
            px, py, pc = _peer(d, mx, my, mc)
            if kind == "gather":
                src, dst = ins[a], mine
            else:
                src, dst = ins[a].at[4 * px + 2 * py + pc], outs[a].at[d]
            remote.append(
                pltpu.make_async_remote_copy(
                    src_ref=src,
                    dst_ref=dst,
                    send_sem=send_sems.at[a * 7 + d - 1],
                    recv_sem=recv_sems.at[a * 7 + d - 1],
                    device_id=(px, py, pc),
                    device_id_type=MESH,
                )
            )
    return remote, local


def _rider_start(*args):
    remote, local = _rider_copies(*args)
    for cp in local + remote:
        cp.start()


def _rider_wait(*args):
    remote, local = _rider_copies(*args)
    for cp in remote:
        cp.wait_recv()
    for cp in remote:
        cp.wait_send()
    for cp in local:
        cp.wait()


def _rider_scratch(n):
    return [pltpu.SemaphoreType.DMA((7 * n,)), pltpu.SemaphoreType.DMA((7 * n,)), pltpu.SemaphoreType.DMA((n,))]


def _gather_rows(v, name):
    n = v.shape[1]
    return _all_gather(jnp.broadcast_to(v, (8, n)), name, pltpu.VMEM)[:, 0, :]


def _adamw(w, g, m, v):
    m = ADAM_B1 * m + (1.0 - ADAM_B1) * g
    v = ADAM_B2 * v + (1.0 - ADAM_B2) * (g * g)
    m_hat = m / (1.0 - ADAM_B1**ADAM_STEP)
    v_hat = v / (1.0 - ADAM_B2**ADAM_STEP)
    delta = -ADAM_LR * (m_hat / (jnp.sqrt(v_hat) + ADAM_EPS) + ADAM_WD * w)
    return delta, m, v


def _sum_adam(recv, w, m, v, name):
    _, r_tot, c = recv.shape
    c_pad = -(-c // 128) * 128
    tr = r_tot
    while 8 * tr * c_pad * 4 > 6 * 1024 * 1024 and tr % 32 == 0:
        tr //= 2

    def body(r_ref, w_ref, m_ref, v_ref, g_ref, d_ref, nm_ref, nv_ref):
        g = r_ref[0].astype(F32)
        for k in range(1, N_DEV):
            g = g + r_ref[k].astype(F32)
        d, nm, nv = _adamw(w_ref[...], g, m_ref[...], v_ref[...])
        g_ref[...] = g
        d_ref[...] = d
        nm_ref[...] = nm
        nv_ref[...] = nv

    blk = pl.BlockSpec((tr, c), lambda i: (i, 0))
    return _pcall(
        body,
        name=name,
        grid=(r_tot // tr,),
        out_shape=[jax.ShapeDtypeStruct((r_tot, c), F32)] * 4,
        in_specs=[pl.BlockSpec((N_DEV, tr, c), lambda i: (0, i, 0)), blk, blk, blk],
        out_specs=[blk] * 4,
        compiler_params=_cp("parallel"),
    )(recv, w, m, v)


def _sum_adam_t(recv, w, m, v, name):
    _, c, r_tot = recv.shape
    tr = min(256, r_tot)

    def body(r_ref, w_ref, m_ref, v_ref, g_ref, d_ref, nm_ref, nv_ref):
        g = r_ref[0].astype(F32)
        for k in range(1, N_DEV):
            g = g + r_ref[k].astype(F32)
        d, nm, nv = _adamw(w_ref[...], g, m_ref[...], v_ref[...])
        g_ref[...] = g
        d_ref[...] = d
        nm_ref[...] = nm
        nv_ref[...] = nv

    blk = pl.BlockSpec((c, tr), lambda i: (0, i))
    return _pcall(
        body,
        name=name,
        grid=(r_tot // tr,),
        out_shape=[jax.ShapeDtypeStruct((c, r_tot), F32)] * 4,
        in_specs=[pl.BlockSpec((N_DEV, c, tr), lambda i: (0, 0, i)), blk, blk, blk],
        out_specs=[blk] * 4,
        compiler_params=_cp("parallel"),
    )(recv, w, m, v)


def _wada_adam(c_t, dada_cols, w, m, v):
    d_model, c = w.shape
    tr = min(256, d_model)

    def body(ct_ref, da_ref, w_ref, m_ref, v_ref, g_ref, d_ref, nm_ref, nv_ref):
        g = jnp.dot(ct_ref[...].astype(BF16), da_ref[...].astype(BF16), preferred_element_type=F32)
        d, nm, nv = _adamw(w_ref[...], g, m_ref[...], v_ref[...])
        g_ref[...] = g
        d_ref[...] = d
        nm_ref[...] = nm
        nv_ref[...] = nv

    blk = pl.BlockSpec((tr, c), lambda i: (i, 0))
    return _pcall(
        body,
        name="wada_adam",
        grid=(d_model // tr,),
        out_shape=[jax.ShapeDtypeStruct((d_model, c), F32)] * 4,
        in_specs=[pl.BlockSpec((tr, N_DEV), lambda i: (i, 0)), pl.BlockSpec((N_DEV, c), lambda i: (0, 0)), blk, blk, blk],
        out_specs=[blk] * 4,
        compiler_params=_cp("parallel"),
    )(c_t, dada_cols, w, m, v)


def _small_adam(gathered, w, m, v):
    p = w.shape[1]

    def body(a_ref, w_ref, m_ref, v_ref, g_ref, d_ref, nm_ref, nv_ref):
        g = a_ref[0:1, :]
        for k in range(1, N_DEV):
            g = g + a_ref[k : k + 1, :]
        d, nm, nv = _adamw(w_ref[...], g, m_ref[...], v_ref[...])
        g_ref[...] = g
        d_ref[...] = d
        nm_ref[...] = nm
        nv_ref[...] = nv

    return _pcall(
        body,
        name="small_adam",
        out_shape=[jax.ShapeDtypeStruct((1, p), F32)] * 4,
    )(gathered, w, m, v)


def _ada_fwd(c_all, w_ada, b_cols):
    c = w_ada.shape[1]

    def body(c_ref, w_ref, b_ref, o_ref):
        o_ref[...] = jnp.dot(c_ref[...].astype(BF16), w_ref[...].astype(BF16), preferred_element_type=F32) + b_ref[...]

    return _pcall(
        body,
        name="ada_fwd",
        out_shape=jax.ShapeDtypeStruct((N_DEV, c), F32),
        compiler_params=_cp(),
    )(c_all, w_ada, b_cols)


def _ln_mod(x, ada):
    s_len, d = x.shape
    tm = min(512, s_len)

    def body(x_ref, sh_ref, sc_ref, h_ref):
        xv = x_ref[...]
        mu = jnp.mean(xv, axis=-1, keepdims=True)
        xc = xv - mu
        var = jnp.mean(xc * xc, axis=-1, keepdims=True)
        xhat = xc * lax.rsqrt(var + LN_EPS)
        h_ref[...] = (xhat * (1.0 + sc_ref[...]) + sh_ref[...]).astype(BF16)

    return _pcall(
        body,
        name="ln_mod",
        grid=(s_len // tm,),
        out_shape=jax.ShapeDtypeStruct((s_len, d), BF16),
        in_specs=[
            pl.BlockSpec((tm, d), lambda i: (i, 0)),
            pl.BlockSpec((1, d), lambda i: (0, 0)),
            pl.BlockSpec((1, d), lambda i: (0, 1)),
        ],
        out_specs=pl.BlockSpec((tm, d), lambda i: (i, 0)),
        compiler_params=_cp("parallel"),
    )(x, ada, ada)


def _mm_cols(a, b, col_off, n_cols, out_dtype, name, ride=()):
    m, k = a.shape
    tm = min(1024, m)
    tn = next(t for t in (1024, 512, 128) if n_cols % t == 0 and col_off % t == 0)
    off = col_off // tn
    ni, nj = m // tm, n_cols // tn
    n = len(ride)

    def body(a_ref, b_ref, *rest):
        ins, o_ref, outs, sems = rest[:n], rest[n], rest[n + 1 : 2 * n + 1], rest[2 * n + 1 :]
        i, j = pl.program_id(0), pl.program_id(1)
        if n:

            @pl.when((i == 0) & (j == 0))
            def _():
                _rider_start("gather", ins, outs, *sems)

        o_ref[...] = lax.dot_general(a_ref[...], b_ref[...], _NT, preferred_element_type=F32).astype(out_dtype)
        if n:

            @pl.when((i == ni - 1) & (j == nj - 1))
            def _():
                _rider_wait("gather", ins, outs, *sems)

    hbm = pl.BlockSpec(memory_space=pltpu.HBM)
    out = _pcall(
        body,
        name=name,
        grid=(ni, nj),
        out_shape=[jax.ShapeDtypeStruct((m, n_cols), out_dtype)]
        + [jax.ShapeDtypeStruct((N_DEV * r.shape[0], r.shape[1]), r.dtype) for r in ride],
        in_specs=[pl.BlockSpec((tm, k), lambda i, j: (i, 0)), pl.BlockSpec((tn, k), lambda i, j: (off + j, 0))] + [hbm] * n,
        out_specs=[pl.BlockSpec((tm, tn), lambda i, j: (i, j))] + [hbm] * n,
        scratch_shapes=_rider_scratch(n) if n else [],
        compiler_params=_cp("arbitrary", "arbitrary") if n else _cp("parallel", "parallel"),
    )(a, b, *ride)
    return out if n else out[0]


def _mm_tn(a, b, name, ride=()):
    s_len, m = a.shape
    n = b.shape[1]
    tm, tn, ts = min(1024, m), min(2048, n), min(2048, s_len)
    ni, nj, ns = m // tm, n // tn, s_len // ts
    nr = len(ride)

    def body(a_ref, b_ref, *rest):
        ins, o_ref, outs = rest[:nr], rest[nr], rest[nr + 1 : 2 * nr + 1]
        sems, acc_s = rest[2 * nr + 1 : -1], rest[-1]
        i, j, kk = pl.program_id(0), pl.program_id(1), pl.program_id(2)
        if nr:

            @pl.when((i == 0) & (j == 0) & (kk == 0))
            def _():
                _rider_start("exchange", ins, outs, *sems)

            @pl.when((i == ni - 1) & (j == nj - 1) & (kk == ns - 1))
            def _():
                _rider_wait("exchange", ins, outs, *sems)

        part = lax.dot_general(a_ref[...], b_ref[...], _TN, preferred_element_type=F32)

        @pl.when(kk == 0)
        def _():
            acc_s[...] = part

        @pl.when(kk > 0)
        def _():
            acc_s[...] += part

        @pl.when(kk == ns - 1)
        def _():
            o_ref[...] = acc_s[...].astype(BF16)

    hbm = pl.BlockSpec(memory_space=pltpu.HBM)
    out = _pcall(
        body,
        name=name,
        grid=(ni, nj, ns),
        out_shape=[jax.ShapeDtypeStruct((m, n), BF16)] + [jax.ShapeDtypeStruct(r.shape, r.dtype) for r in ride],
        in_specs=[pl.BlockSpec((ts, tm), lambda i, j, kk: (kk, i)), pl.BlockSpec((ts, tn), lambda i, j, kk: (kk, j))] + [hbm] * nr,
        out_specs=[pl.BlockSpec((tm, tn), lambda i, j, kk: (i, j))] + [hbm] * nr,
        scratch_shapes=(_rider_scratch(nr) if nr else []) + [pltpu.VMEM((tm, tn), F32)],
        compiler_params=_cp("arbitrary", "arbitrary", "arbitrary") if nr else _cp("parallel", "parallel", "arbitrary"),
    )(a, b, *ride)
    return out if nr else out[0]


def _split3(a):
    hi = a.astype(BF16)
    r1 = a - hi.astype(F32)
    mid = r1.astype(BF16)
    lo = (r1 - mid.astype(F32)).astype(BF16)
    return hi, mid, lo


def _dot_ones(a, tri):
    return sum(jnp.dot(t, tri, preferred_element_type=F32) for t in _split3(a))


def _log_sigmoid(x):
    return jnp.minimum(x, 0.0) - jnp.log1p(jnp.exp(-jnp.abs(x)))


def _fox_cum(flog_t, bf_col):
    s_len = flog_t.shape[1]

    def body(fl_ref, bf_ref, cum_ref):
        r = lax.broadcasted_iota(jnp.int32, (128, 128), 0)
        c = lax.broadcasted_iota(jnp.int32, (128, 128), 1)
        upper = (r <= c).astype(BF16)

        def step(t, carry):
            sl = pl.ds(pl.multiple_of(t * 128, 128), 128)
            lf = _log_sigmoid(fl_ref[:, sl] + bf_ref[...])
            cs = _dot_ones(lf, upper) + carry
            cum_ref[:, sl] = cs
            return cs[:, 127:128]

        lax.fori_loop(0, s_len // 128, step, jnp.zeros((FOX_H, 1), F32))

    return _pcall(body, name="fox_cum", out_shape=jax.ShapeDtypeStruct((FOX_H, s_len), F32))(flog_t, bf_col)


def _fox_gate_bwd(drow, dcol, flog_t, bf_col):
    s_len = flog_t.shape[1]
    n = s_len // 128

    def body(dr_ref, dc_ref, fl_ref, bf_ref, dfl_ref, dbf_ref):
        r = lax.broadcasted_iota(jnp.int32, (128, 128), 0)
        c = lax.broadcasted_iota(jnp.int32, (128, 128), 1)
        lower = (r >= c).astype(BF16)

        def step(t, carry):
            run, tot = carry
            sl = pl.ds(pl.multiple_of((n - 1 - t) * 128, 128), 128)
            rc = _dot_ones(dr_ref[:, sl] - dc_ref[:, sl], lower) + run
            dfl = rc * _sigmoid(-(fl_ref[:, sl] + bf_ref[...]))
            dfl_ref[:, sl] = dfl
            return rc[:, 0:1], tot + jnp.sum(dfl, axis=1, keepdims=True)

        zero = jnp.zeros((FOX_H, 1), F32)
        _, tot = lax.fori_loop(0, n, step, (zero, zero))
        dbf_ref[...] = jnp.broadcast_to(tot, (FOX_H, 128))

    return _pcall(
        body,
        name="fox_gate_bwd",
        out_shape=[jax.ShapeDtypeStruct((FOX_H, s_len), F32), jax.ShapeDtypeStruct((FOX_H, 128), F32)],
    )(drow, dcol, flog_t, bf_col)


def _diag_mask(blk, transposed=False):
    r = lax.broadcasted_iota(jnp.int32, (blk, blk), 0)
    c = lax.broadcasted_iota(jnp.int32, (blk, blk), 1)
    return c >= r if transposed else r >= c


_NT = (((1,), (1,)), ((), ()))
_TN = (((0,), (0,)), ((), ()))


def _fox_fwd(qkv, cum_row):
    s_len = qkv.shape[0]
    blk = min(ATT_BLK, s_len)
    nb = s_len // blk
    log2e = 1.4426950408889634

    def body(q_ref, k_ref, v_ref, c_ref, o_ref, lse_ref, mx_s, acc_s, u_s):
        i = pl.program_id(1)

        def key_cols(j, n):
            return pl.ds(pl.multiple_of(j * blk, blk), n * blk)

        def walk(tile):
            def four_pairs(t, carry):
                for u in range(4):
                    tile(8 * t + 2 * u, 2, False)
                return carry

            lax.fori_loop(0, i // 8, four_pairs, 0)

            @pl.when((i // 4) % 2 == 1)
            def _():
                tile(8 * (i // 8), 2, False)
                tile(8 * (i // 8) + 2, 2, False)

            @pl.when((i // 2) % 2 == 1)
            def _():
                tile(4 * (i // 4), 2, False)

            @pl.when(i % 2 == 1)
            def _():
                tile(i - 1, 1, False)

            tile(i, 1, True)

        def lane_max(j, n, masked):
            cols = key_cols(j, n)
            u = lax.dot_general(q_ref[...], k_ref[cols, :], _NT, preferred_element_type=F32) * (FOX_SCALE * log2e) - c_ref[:, cols] * log2e
            if masked:
                u = jnp.where(_diag_mask(blk), u, NEG)
            u_s[:, cols] = u
            part = u[:, 0:128]
            for t in range(1, n * blk // 128):
                part = jnp.maximum(part, u[:, t * 128 : (t + 1) * 128])
            mx_s[...] = jnp.maximum(mx_s[...], part)

        mx_s[...] = jnp.full(mx_s.shape, NEG, F32)
        walk(lane_max)
        m = jnp.max(mx_s[...], axis=1, keepdims=True)

        def weigh(j, n, masked):
            cols = key_cols(j, n)
            p = jnp.exp2(u_s[:, cols] - m)
            ones_col = (lax.broadcasted_iota(jnp.int32, (n * blk, 128), 1) == 0).astype(BF16)
            v1 = jnp.concatenate([v_ref[cols, :], ones_col], axis=1)
            acc_s[...] += jnp.dot(p.astype(BF16), v1, preferred_element_type=F32)

        acc_s[...] = jnp.zeros(acc_s.shape, F32)
        walk(weigh)
        l = acc_s[:, FOX_DH : FOX_DH + 1]
        o_ref[...] = acc_s[:, :FOX_DH] / l
        lse_ref[...] = m * (1.0 / log2e) + jnp.log(l)

    return _pcall(
        body,
        name="fox_fwd",
        grid=(FOX_H, nb),
        out_shape=[jax.ShapeDtypeStruct((s_len, FOX_W), F32), jax.ShapeDtypeStruct((FOX_H, s_len, 1), F32)],
        in_specs=[
            pl.BlockSpec((blk, FOX_DH), lambda h, i: (i, h)),
            pl.BlockSpec((s_len, FOX_DH), lambda h, i: (0, FOX_H + h)),
            pl.BlockSpec((s_len, FOX_DH), lambda h, i: (0, 2 * FOX_H + h)),
            pl.BlockSpec((None, 1, s_len), lambda h, i: (h, 0, 0)),
        ],
        out_specs=[
            pl.BlockSpec((blk, FOX_DH), lambda h, i: (i, h)),
            pl.BlockSpec((None, blk, 1), lambda h, i: (h, i, 0)),
        ],
        scratch_shapes=[pltpu.VMEM((blk, 128), F32), pltpu.VMEM((blk, 2 * FOX_DH), F32), pltpu.VMEM((blk, s_len), F32)],
        compiler_params=_cp("parallel", "arbitrary"),
    )(qkv, qkv, qkv, cum_row)


def _fox_bwd(qkv, cum_col, lse_row, delta_row, do):
    s_len = qkv.shape[0]
    blk = min(ATT_BLK, s_len)
    nb = s_len // blk

    def body(q_ref, k_ref, v_ref, c_ref, lse_ref, dl_ref, do_ref, dq_ref, dk_ref, dv_ref, dc_ref, dr_ref, dk_s, dv_s, dc_s, cb_s, dq_s):
        j = pl.program_id(1)

        @pl.when(j == 0)
        def _():
            dq_s[...] = jnp.zeros(dq_s.shape, F32)
            dr_ref[...] = jnp.zeros(dr_ref.shape, F32)

        dk_s[...] = jnp.zeros(dk_s.shape, F32)
        dv_s[...] = jnp.zeros(dv_s.shape, F32)
        dc_s[...] = jnp.zeros(dc_s.shape, F32)
        cb_s[...] = jnp.broadcast_to(c_ref[...], cb_s.shape)

        def tile(i, n, diag):
            rows = pl.ds(pl.multiple_of(i * blk, blk), n * blk)
            q, dob = q_ref[rows, :], do_ref[rows, :]
            k, v = k_ref[...], v_ref[...]
            s_t = lax.dot_general(k, q, _NT, preferred_element_type=F32) * FOX_SCALE - cb_s[:, : n * blk]
            p_t = jnp.exp(s_t - lse_ref[:, rows])
            if diag:
                p_t = jnp.where(_diag_mask(blk, transposed=True), p_t, 0.0)
            dp_t = lax.dot_general(v, dob, _NT, preferred_element_type=F32)
            ds_t = p_t * (dp_t - dl_ref[:, rows])
            dsb = ds_t.astype(BF16)
            dv_s[...] += jnp.dot(p_t.astype(BF16), dob, preferred_element_type=F32)
            dk_s[...] += jnp.dot(dsb, q, preferred_element_type=F32)
            dq_c = lax.dot_general(dsb, k, _TN, preferred_element_type=F32)
            part = ds_t[:, 0:128]
            for t in range(1, n * blk // 128):
                part = part + ds_t[:, t * 128 : (t + 1) * 128]
            dc_s[...] += part
            dr_ref[:, rows] += jnp.sum(ds_t, axis=0, keepdims=True)
            if diag:
                dq_s[rows, :] = (dq_s[rows, :] + dq_c) * FOX_SCALE
            else:
                dq_s[rows, :] += dq_c

        tile(j, 1, True)
        below = nb - 1 - j
        b0, b1, b2 = below % 2, (below // 2) % 2, (below // 4) % 2

        @pl.when(b0 == 1)
        def _():
            tile(j + 1, 1, False)

        @pl.when(b1 == 1)
        def _():
            tile(j + 1 + b0, 2, False)

        @pl.when(b2 == 1)
        def _():
            tile(j + 1 + b0 + 2 * b1, 2, False)
            tile(j + 3 + b0 + 2 * b1, 2, False)

        first = j + 1 + b0 + 2 * b1 + 4 * b2

        def four_pairs(t, carry):
            for u in range(4):
                tile(first + 8 * t + 2 * u, 2, False)
            return carry

        lax.fori_loop(0, below // 8, four_pairs, 0)
        dk_ref[...] = (dk_s[...] * FOX_SCALE).astype(BF16)
        dv_ref[...] = dv_s[...].astype(BF16)
        dc_ref[...] = jnp.sum(dc_s[...], axis=1, keepdims=True)

        @pl.when(j == nb - 1)
        def _():
            dq_ref[...] = dq_s[...].astype(BF16)

    head = lambda h, j: (0, h)
    row = pl.BlockSpec((None, 1, s_len), lambda h, j: (h, 0, 0))
    return _pcall(
        body,
        name="fox_bwd",
        grid=(FOX_H, nb),
        out_shape=[
            jax.ShapeDtypeStruct((s_len, FOX_W), BF16),
            jax.ShapeDtypeStruct((s_len, FOX_W), BF16),
            jax.ShapeDtypeStruct((s_len, FOX_W), BF16),
            jax.ShapeDtypeStruct((FOX_H, s_len, 1), F32),
            jax.ShapeDtypeStruct((FOX_H, 1, s_len), F32),
        ],
        in_specs=[
            pl.BlockSpec((s_len, FOX_DH), head),
            pl.BlockSpec((blk, FOX_DH), lambda h, j: (j, FOX_H + h)),
            pl.BlockSpec((blk, FOX_DH), lambda h, j: (j, 2 * FOX_H + h)),
            pl.BlockSpec((None, blk, 1), lambda h, j: (h, j, 0)),
            row,
            row,
            pl.BlockSpec((s_len, FOX_DH), head),
        ],
        out_specs=[
            pl.BlockSpec((s_len, FOX_DH), head),
            pl.BlockSpec((blk, FOX_DH), lambda h, j: (j, h)),
            pl.BlockSpec((blk, FOX_DH), lambda h, j: (j, h)),
            pl.BlockSpec((None, blk, 1), lambda h, j: (h, j, 0)),
            row,
        ],
        scratch_shapes=[
            pltpu.VMEM((blk, FOX_DH), F32),
            pltpu.VMEM((blk, FOX_DH), F32),
            pltpu.VMEM((blk, 128), F32),
            pltpu.VMEM((blk, 2 * blk), F32),
            pltpu.VMEM((s_len, FOX_DH), F32),
        ],
        compiler_params=_cp("parallel", "arbitrary"),
    )(qkv, qkv, qkv, cum_col, lse_row, delta_row, do)


def _swa_bias():
    cols = SWA_G * WINDOW
    k = np.arange(2 * WINDOW)[:, None]
    q = np.arange(cols)[None, :]
    dist = (q % WINDOW) - k + WINDOW
    valid = (dist >= 0) & (dist < WINDOW)
    out = np.empty((2, SWA_HKV, 2 * WINDOW, cols), np.float32)
    for g in range(SWA_HKV):
        slope = np.array([SLOPES[g * SWA_G + t] for t in range(SWA_G)], np.float32)[q // WINDOW]
        bias = -(slope * dist.astype(np.float32))
        out[0, g] = np.where(valid & (k >= WINDOW), bias, np.float32(NEG))
        out[1, g] = np.where(valid, bias, np.float32(NEG))
    return jnp.asarray(out)


def _swa_group(i, q_ref, kk, sinks_ref, bias_ref, g):
    cols = SWA_G * WINDOW
    head = lax.broadcasted_iota(jnp.int32, (1, cols), 1) // WINDOW
    sink = jnp.zeros((1, cols), F32)
    for t in range(SWA_G):
        sink = jnp.where(head == t, sinks_ref[g * SWA_G + t], sink)
    q = jnp.concatenate([q_ref[:, (g * SWA_G + t) * SWA_DH : (g * SWA_G + t + 1) * SWA_DH] for t in range(SWA_G)], axis=0)
    k = kk[:, g * SWA_DH : (g + 1) * SWA_DH]
    s = lax.dot_general(k, q, _NT, preferred_element_type=F32) * SWA_SCALE + bias_ref[jnp.minimum(i, 1), g]
    m = jnp.maximum(jnp.max(s, axis=0, keepdims=True), sink)
    e = jnp.exp(s - m)
    e_sink = jnp.exp(sink - m)
    inv = 1.0 / (jnp.sum(e, axis=0, keepdims=True) + e_sink)
    return q, k, e * inv, e_sink * inv


def _swa_specs(col_q, col_k, col_v, rev, nb):
    def blk(t):
        return nb - 1 - t if rev else t

    return [
        pl.BlockSpec((WINDOW, SWA_W), lambda t: (blk(t), col_q)),
        pl.BlockSpec((WINDOW, SWA_KVW), lambda t: (jnp.maximum(blk(t) - 1, 0), col_k)),
        pl.BlockSpec((WINDOW, SWA_KVW), lambda t: (blk(t), col_k)),
        pl.BlockSpec((WINDOW, SWA_KVW), lambda t: (jnp.maximum(blk(t) - 1, 0), col_v)),
        pl.BlockSpec((WINDOW, SWA_KVW), lambda t: (blk(t), col_v)),
    ]


def _swa_fwd(qkv, sinks):
    s_len = qkv.shape[0]
    nb = s_len // WINDOW
    bias_spec = pl.BlockSpec((2, SWA_HKV, 2 * WINDOW, SWA_G * WINDOW), lambda t: (0, 0, 0, 0))

    def body(q_ref, kp_ref, kc_ref, vp_ref, vc_ref, sinks_ref, bias_ref, o_ref):
        i = pl.program_id(0)
        kk = jnp.concatenate([kp_ref[...], kc_ref[...]], axis=0)
        vv = jnp.concatenate([vp_ref[...], vc_ref[...]], axis=0)
        for g in range(SWA_HKV):
            _, _, p, _ = _swa_group(i, q_ref, kk, sinks_ref, bias_ref, g)
            o = lax.dot_general(p.astype(BF16), vv[:, g * SWA_DH : (g + 1) * SWA_DH], _TN, preferred_element_type=F32)
            for t in range(SWA_G):
                h = g * SWA_G + t
                o_ref[:, h * SWA_DH : (h + 1) * SWA_DH] = o[t * WINDOW : (t + 1) * WINDOW, :]

    return _pcall(
        body,
        name="swa_fwd",
        grid=(nb,),
        out_shape=jax.ShapeDtypeStruct((s_len, SWA_W), F32),
        in_specs=_swa_specs(0, 4, 5, False, nb) + [pl.BlockSpec(memory_space=pltpu.SMEM), bias_spec],
        out_specs=pl.BlockSpec((WINDOW, SWA_W), lambda t: (t, 0)),
        compiler_params=_cp("parallel"),
    )(qkv, qkv, qkv, qkv, qkv, sinks, _swa_bias())


def _swa_bwd(qkv, sinks, do):
    s_len = qkv.shape[0]
    nb = s_len // WINDOW
    bias_spec = pl.BlockSpec((2, SWA_HKV, 2 * WINDOW, SWA_G * WINDOW), lambda t: (0, 0, 0, 0))

    def body(q_ref, kp_ref, kc_ref, vp_ref, vc_ref, sinks_ref, bias_ref, do_ref, dq_ref, dk_ref, dv_ref, dsink_ref, ck_s, cv_s, dkk_s, dvv_s):
        t = pl.program_id(0)
        i = nb - 1 - t

        @pl.when(t == 0)
        def _():
            ck_s[...] = jnp.zeros(ck_s.shape, F32)
            cv_s[...] = jnp.zeros(cv_s.shape, F32)
            dsink_ref[...] = jnp.zeros(dsink_ref.shape, F32)

        kk = jnp.concatenate([kp_ref[...], kc_ref[...]], axis=0)
        vv = jnp.concatenate([vp_ref[...], vc_ref[...]], axis=0)
        lane = lax.broadcasted_iota(jnp.int32, (1, 128), 1)
        dsink = jnp.zeros((1, 128), F32)
        for g in range(SWA_HKV):
            cols = slice(g * SWA_DH, (g + 1) * SWA_DH)
            q, k, p, p_sink = _swa_group(i, q_ref, kk, sinks_ref, bias_ref, g)
            dob = jnp.concatenate([do_ref[:, (g * SWA_G + t) * SWA_DH : (g * SWA_G + t + 1) * SWA_DH] for t in range(SWA_G)], axis=0)
            dp = lax.dot_general(vv[:, cols], dob, _NT, preferred_element_type=F32)
            delta = jnp.sum(p * dp, axis=0, keepdims=True)
            dsb = (p * (dp - delta)).astype(BF16)
            dq = (lax.dot_general(dsb, k, _TN, preferred_element_type=F32) * SWA_SCALE).astype(BF16)
            ps_d = p_sink * delta
            for t in range(SWA_G):
                h = g * SWA_G + t
                dq_ref[:, h * SWA_DH : (h + 1) * SWA_DH] = dq[t * WINDOW : (t + 1) * WINDOW, :]
                dsink = dsink + jnp.where(lane == h, -jnp.sum(ps_d[:, t * WINDOW : (t + 1) * WINDOW], axis=1, keepdims=True), 0.0)
            dkk_s[:, cols] = jnp.dot(dsb, q, preferred_element_type=F32) * SWA_SCALE
            dvv_s[:, cols] = jnp.dot(p.astype(BF16), dob, preferred_element_type=F32)
        dk_ref[...] = (dkk_s[WINDOW:, :] + ck_s[...]).astype(BF16)
        dv_ref[...] = (dvv_s[WINDOW:, :] + cv_s[...]).astype(BF16)
        ck_s[...] = dkk_s[:WINDOW, :]
        cv_s[...] = dvv_s[:WINDOW, :]
        dsink_ref[...] += dsink

    row = lambda t: (nb - 1 - t, 0)
    return _pcall(
        body,
        name="swa_bwd",
        grid=(nb,),
        out_shape=[
            jax.ShapeDtypeStruct((s_len, SWA_W), BF16),
            jax.ShapeDtypeStruct((s_len, SWA_KVW), BF16),
            jax.ShapeDtypeStruct((s_len, SWA_KVW), BF16),
            jax.ShapeDtypeStruct((1, 128), F32),
        ],
        in_specs=_swa_specs(0, 4, 5, True, nb)
        + [pl.BlockSpec(memory_space=pltpu.SMEM), bias_spec, pl.BlockSpec((WINDOW, SWA_W), row)],
        out_specs=[
            pl.BlockSpec((WINDOW, SWA_W), row),
            pl.BlockSpec((WINDOW, SWA_KVW), row),
            pl.BlockSpec((WINDOW, SWA_KVW), row),
            pl.BlockSpec((1, 128), lambda t: (0, 0)),
        ],
        scratch_shapes=[
            pltpu.VMEM((WINDOW, SWA_KVW), F32),
            pltpu.VMEM((WINDOW, SWA_KVW), F32),
            pltpu.VMEM((2 * WINDOW, SWA_KVW), F32),
            pltpu.VMEM((2 * WINDOW, SWA_KVW), F32),
        ],
        compiler_params=_cp("arbitrary"),
    )(qkv, qkv, qkv, qkv, qkv, sinks, _swa_bias(), do)


def _branch_fwd(o, gates, g_blk, w_b, name):
    s_len, wd = o.shape
    d = w_b.shape[1]
    tm = min(1024, s_len)

    def body(o_ref, g_ref, w_ref, y_ref, a_ref):
        g = g_ref[...].astype(F32)
        a = (o_ref[...] * (g * _sigmoid(g))).astype(BF16)
        a_ref[...] = a
        y_ref[...] = jnp.dot(a, w_ref[...], preferred_element_type=F32).astype(BF16)

    return _pcall(
        body,
        name=name,
        grid=(s_len // tm,),
        out_shape=[jax.ShapeDtypeStruct((s_len, d), BF16), jax.ShapeDtypeStruct((s_len, wd), BF16)],
        in_specs=[
            pl.BlockSpec((tm, wd), lambda i: (i, 0)),
            pl.BlockSpec((tm, wd), lambda i: (i, g_blk)),
            pl.BlockSpec((wd, d), lambda i: (0, 0)),
        ],
        out_specs=[pl.BlockSpec((tm, d), lambda i: (i, 0)), pl.BlockSpec((tm, wd), lambda i: (i, 0))],
        compiler_params=_cp("parallel"),
    )(o, gates, w_b)


def _out_stage(gates, mf_blk, y_fox, y_swa, w_out, x, ada, ln_g, ln_b, target):
    s_len, d = x.shape
    tm = min(256, s_len)
    n_steps = s_len // tm

    def body(mf_ref, ms_ref, yf_ref, ys_ref, w_ref, x_ref, gate_ref, lg_ref, lb_ref, t_ref, mg_ref, dza_ref, dsub_ref, red_ref, dmf_ref, dms_ref, dyf_ref, dys_ref):
        i = pl.program_id(0)
        sf, ss = _sigmoid(mf_ref[...].astype(F32)), _sigmoid(ms_ref[...].astype(F32))
        yf, ys = yf_ref[...].astype(F32), ys_ref[...].astype(F32)
        merged = sf * yf + ss * ys
        mb = merged.astype(BF16)
        mg_ref[...] = mb
        sub = jnp.dot(mb, w_ref[...], preferred_element_type=F32)
        gate = gate_ref[...]
        z = ALPHA * x_ref[...] + gate * sub
        mu = jnp.mean(z, axis=-1, keepdims=True)
        zc = z - mu
        var = jnp.mean(zc * zc, axis=-1, keepdims=True)
        rstd = lax.rsqrt(var + LN_EPS)
        zhat = zc * rstd
        err = zhat * lg_ref[...] + lb_ref[...] - t_ref[...]
        dout = err * (1.0 / d)
        dzhat = dout * lg_ref[...]
        dz = rstd * (dzhat - jnp.mean(dzhat, axis=-1, keepdims=True) - zhat * jnp.mean(dzhat * zhat, axis=-1, keepdims=True))
        dza_ref[...] = ALPHA * dz
        dsub = (gate * dz).astype(BF16)
        dsub_ref[...] = dsub
        dm = lax.dot_general(dsub, w_ref[...], _NT, preferred_element_type=F32)
        dmf_ref[...] = (dm * yf * (sf * (1.0 - sf))).astype(BF16)
        dms_ref[...] = (dm * ys * (ss * (1.0 - ss))).astype(BF16)
        dyf_ref[...] = (dm * sf).astype(BF16)
        dys_ref[...] = (dm * ss).astype(BF16)
        part = jnp.concatenate(
            [
                jnp.sum(dz * sub, axis=0, keepdims=True),
                jnp.sum(dout * zhat, axis=0, keepdims=True),
                jnp.sum(dout, axis=0, keepdims=True),
                jnp.sum(err * err, axis=0, keepdims=True),
                jnp.zeros((4, d), F32),
            ],
            axis=0,
        )

        @pl.when(i == 0)
        def _():
            red_ref[...] = part

        @pl.when(i > 0)
        def _():
            red_ref[...] += part

        @pl.when(i == n_steps - 1)
        def _():
            red_ref[4:5, :] = jnp.broadcast_to(jnp.sum(red_ref[3:4, :], axis=1, keepdims=True), (1, d))

    row = pl.BlockSpec((tm, d), lambda i: (i, 0))
    vec = pl.BlockSpec((1, d), lambda i: (0, 0))
    return _pcall(
        body,
        name="out_stage",
        grid=(n_steps,),
        out_shape=[
            jax.ShapeDtypeStruct((s_len, d), BF16),
            jax.ShapeDtypeStruct((s_len, d), F32),
            jax.ShapeDtypeStruct((s_len, d), BF16),
            jax.ShapeDtypeStruct((8, d), F32),
        ]
        + [jax.ShapeDtypeStruct((s_len, d), BF16)] * 4,
        in_specs=[
            pl.BlockSpec((tm, d), lambda i: (i, mf_blk)),
            pl.BlockSpec((tm, d), lambda i: (i, mf_blk + 1)),
            row,
            row,
            pl.BlockSpec((d, d), lambda i: (0, 0), pipeline_mode=pl.Buffered(1)),
            row,
            pl.BlockSpec((1, d), lambda i: (0, 2)),
            vec,
            vec,
            row,
        ],
        out_specs=[row, row, row, pl.BlockSpec((8, d), lambda i: (0, 0))] + [row] * 4,
        compiler_params=_cp("arbitrary"),
    )(gates, gates, y_fox, y_swa, w_out, x, ada, ln_g, ln_b, target)


def _branch_bwd(dy, w_b, o, gates, g_blk, name, n_heads):
    s_len, d = dy.shape
    wd = w_b.shape[0]
    tm = min(1024, s_len)

    def body(dy_ref, w_ref, o_ref, g_ref, do_ref, dg_ref, *rest):
        da = lax.dot_general(dy_ref[...], w_ref[...], _NT, preferred_element_type=F32)
        g = g_ref[...].astype(F32)
        sg = _sigmoid(g)
        do = da * (g * sg)
        do_ref[...] = do.astype(BF16)
        o = o_ref[...]
        dg_ref[...] = (da * o * (sg * (1.0 + g * (1.0 - sg)))).astype(BF16)
        if n_heads:
            prod = do.astype(BF16).astype(F32) * o
            lane = lax.broadcasted_iota(jnp.int32, (1, 128), 1)
            delta = jnp.zeros((tm, 128), F32)
            for h in range(n_heads):
                dh = jnp.sum(prod[:, h * 128 : (h + 1) * 128], axis=1, keepdims=True)
                delta = delta + jnp.where(lane == h, dh, 0.0)
            rest[0][...] = delta

    out_shape = [jax.ShapeDtypeStruct((s_len, wd), BF16), jax.ShapeDtypeStruct((s_len, wd), BF16)]
    out_specs = [pl.BlockSpec((tm, wd), lambda i: (i, 0))] * 2
    if n_heads:
        out_shape.append(jax.ShapeDtypeStruct((s_len, 128), F32))
        out_specs.append(pl.BlockSpec((tm, 128), lambda i: (i, 0)))
    return _pcall(
        body,
        name=name,
        grid=(s_len // tm,),
        out_shape=out_shape,
        in_specs=[
            pl.BlockSpec((tm, d), lambda i: (i, 0)),
            pl.BlockSpec((wd, d), lambda i: (0, 0)),
            pl.BlockSpec((tm, wd), lambda i: (i, 0)),
            pl.BlockSpec((tm, wd), lambda i: (i, g_blk)),
        ],
        out_specs=out_specs,
        compiler_params=_cp("parallel"),
    )(dy, w_b, o, gates)


def _in_bwd(dproj, w_in_t, x, ada, dza, ride):
    s_len, d = x.shape
    k_tot = dproj.shape[1]
    tm, tk, dn = min(512, s_len), k_tot // 4, d // 2
    ni, nk = s_len // tm, k_tot // tk
    n = len(ride)

    def body(dp_ref, w_ref, x_ref, sc_ref, dza_ref, *rest):
        ins, (gx_ref, red_ref), outs = rest[:n], rest[n : n + 2], rest[n + 2 : 2 * n + 2]
        sems, acc_s = rest[2 * n + 2 : 2 * n + 5], rest[2 * n + 5]
        i, nh, kk = pl.program_id(0), pl.program_id(1), pl.program_id(2)

        @pl.when((i == 0) & (nh == 0) & (kk == 0))
        def _():
            _rider_start("exchange", ins, outs, *sems)

        @pl.when((i == ni - 1) & (nh == 1) & (kk == nk - 1))
        def _():
            _rider_wait("exchange", ins, outs, *sems)

        part = jnp.dot(dp_ref[...], w_ref[...], preferred_element_type=F32)
        half = pl.ds(pl.multiple_of(nh * dn, dn), dn)

        @pl.when(kk == 0)
        def _():
            acc_s[:, half] = part

        @pl.when(kk > 0)
        def _():
            acc_s[:, half] += part

        @pl.when((nh == 1) & (kk == nk - 1))
        def _():
            dh = acc_s[...]
            xv = x_ref[...]
            mu = jnp.mean(xv, axis=-1, keepdims=True)
            xc = xv - mu
            var = jnp.mean(xc * xc, axis=-1, keepdims=True)
            rstd = lax.rsqrt(var + LN_EPS)
            xhat = xc * rstd
            dxhat = dh * (1.0 + sc_ref[...])
            dx = rstd * (dxhat - jnp.mean(dxhat, axis=-1, keepdims=True) - xhat * jnp.mean(dxhat * xhat, axis=-1, keepdims=True))
            gx_ref[...] = dza_ref[...] + dx
            part_r = jnp.concatenate(
                [jnp.sum(dh, axis=0, keepdims=True), jnp.sum(dh * xhat, axis=0, keepdims=True), jnp.zeros((6, d), F32)], axis=0
            )

            @pl.when(i == 0)
            def _():
                red_ref[...] = part_r

            @pl.when(i > 0)
            def _():
                red_ref[...] += part_r

    row = pl.BlockSpec((tm, d), lambda i, nh, kk: (i, 0))
    hbm = pl.BlockSpec(memory_space=pltpu.HBM)
    return _pcall(
        body,
        name="in_bwd",
        grid=(ni, 2, nk),
        out_shape=[jax.ShapeDtypeStruct((s_len, d), F32), jax.ShapeDtypeStruct((8, d), F32)]
        + [jax.ShapeDtypeStruct(r.shape, r.dtype) for r in ride],
        in_specs=[
            pl.BlockSpec((tm, tk), lambda i, nh, kk: (i, kk)),
            pl.BlockSpec((tk, dn), lambda i, nh, kk: (kk, nh)),
            row,
            pl.BlockSpec((1, d), lambda i, nh, kk: (0, 1)),
            row,
        ]
        + [hbm] * n,
        out_specs=[row, pl.BlockSpec((8, d), lambda i, nh, kk: (0, 0))] + [hbm] * n,
        scratch_shapes=_rider_scratch(n) + [pltpu.VMEM((tm, d), F32)],
        compiler_params=_cp("arbitrary", "arbitrary", "arbitrary"),
    )(dproj, w_in_t, x, ada, dza, *ride)


def _pad_lanes(v, n):
    return jnp.pad(v, ((0, 0), (0, n - v.shape[1])))


def kernel(x, c, w_ada, b_ada, w_in, b_f, attn_sinks, w_br_fox, w_br_swa, w_out, ln_g, ln_b, loss_target, m_w_ada, m_b_ada, m_w_in, m_b_f, m_attn_sinks, m_w_br_fox, m_w_br_swa, m_w_out, m_ln_g, m_ln_b, v_w_ada, v_b_ada, v_w_in, v_b_f, v_attn_sinks, v_w_br_fox, v_w_br_swa, v_w_out, v_ln_g, v_ln_b):
    x2, tgt = x[0], loss_target[0]
    s_len, d = x2.shape
    me = 4 * lax.axis_index("x") + 2 * lax.axis_index("y") + lax.axis_index("c")
    off_ms = OFF_MF + d
    in_pad = off_ms + d
    c_ada = w_ada.shape[2]
    c_in = w_in.shape[2]
    c_br = w_br_fox.shape[2]

    w_in_full = _all_gather(w_in[0].T.astype(BF16), "ag_w_in", pltpu.HBM).reshape(N_DEV * c_in, d)
    w_in_pad = jnp.concatenate(
        [w_in_full[:REAL_FLOG_END], jnp.zeros((FLOG_PAD - N_FLOG, d), BF16), w_in_full[REAL_FLOG_END:]], axis=0
    )
    k_cut = REAL_FLOG_END // c_in

    c_all = _gather_rows(c, "ag_c")
    b_cols = lax.dynamic_slice(b_ada, (0, me * c_ada), (1, c_ada))
    ada_cols = _ada_fwd(c_all, w_ada[0], b_cols)
    ada_g = _all_gather(ada_cols, "ag_ada", pltpu.VMEM)
    ada = lax.dynamic_index_in_dim(ada_g, me, axis=1, keepdims=False).reshape(1, N_DEV * c_ada)

    h = _ln_mod(x2, ada)
    qkv_fox = _mm_cols(h, w_in_pad, OFF_FQ, 3 * FOX_W, BF16, "proj_fox")
    flog = _mm_cols(h, w_in_pad, OFF_FLOG, 128, F32, "proj_flog")
    qkv_swa = _mm_cols(h, w_in_pad, OFF_SQ, SWA_W + 2 * SWA_KVW, BF16, "proj_swa")
    gates, w_bf, w_bs, w_o = _mm_cols(
        h, w_in_pad, OFF_GF, in_pad - OFF_GF, BF16, "proj_gates",
        ride=(w_br_fox[0].astype(BF16), w_br_swa[0].astype(BF16), w_out[0].astype(BF16)),
    )
    w_bf = w_bf.reshape(N_DEV, FOX_W, c_br).transpose(1, 0, 2).reshape(FOX_W, d)
    w_bs = w_bs.reshape(N_DEV, SWA_W, c_br).transpose(1, 0, 2).reshape(SWA_W, d)
    w_o = w_o.reshape(d, d)
    mf_blk = (OFF_MF - OFF_GF) // d

    flog_t = flog[:, :N_FLOG].T
    bf_col = b_f.reshape(FOX_H, 1)
    cum = _fox_cum(flog_t, bf_col)
    cum_row = cum.reshape(FOX_H, 1, s_len)
    o_fox, lse = _fox_fwd(qkv_fox, cum_row)
    sinks = attn_sinks.reshape(SWA_HQ)
    o_swa = _swa_fwd(qkv_swa, sinks)

    y_fox, a_fox = _branch_fwd(o_fox, gates, 0, w_bf, "branch_fox")
    y_swa, a_swa = _branch_fwd(o_swa, gates, 1, w_bs, "branch_swa")
    merged, dza, dsub, red, dmf, dms, dy_fox, dy_swa = _out_stage(gates, mf_blk, y_fox, y_swa, w_o, x2, ada, ln_g, ln_b, tgt)
    loss = lax.psum(0.5 * red[4, 0] / d, ("x", "y", "c"))

    do_fox, dg_fox, delta = _branch_bwd(dy_fox, w_bf, o_fox, gates, 0, "branch_fox_bwd", FOX_H)
    do_swa, dg_swa = _branch_bwd(dy_swa, w_bs, o_swa, gates, 1, "branch_swa_bwd", 0)
    delta_row = delta[:, :FOX_H].T.reshape(FOX_H, 1, s_len)
    dq_f, dk_f, dv_f, dcol, drow = _fox_bwd(
        qkv_fox, cum.reshape(FOX_H, s_len, 1), lse.reshape(FOX_H, 1, s_len), delta_row, do_fox
    )
    dflog_t, dbf = _fox_gate_bwd(drow.reshape(FOX_H, s_len), dcol.reshape(FOX_H, s_len), flog_t, bf_col)
    dq_s, dk_s, dv_s, dsink = _swa_bwd(qkv_swa, sinks, do_swa)
    dflog = _pad_lanes(dflog_t.T, FLOG_PAD).astype(BF16)
    dproj = jnp.concatenate([dq_f, dk_f, dv_f, dflog, dq_s, dk_s, dv_s, dg_fox, dg_swa, dmf, dms], axis=1)
    g_w_bf = _mm_tn(a_fox, dy_fox, "grad_w_br_fox")
    g_w_bs = _mm_tn(a_swa, dy_swa, "grad_w_br_swa")
    g_w_o = _mm_tn(merged, dsub, "grad_w_out")
    g_w_in, r_bf, r_bs, r_o = _mm_tn(
        dproj, h, "grad_w_in",
        ride=(
            g_w_bf.reshape(FOX_W, N_DEV, c_br).transpose(1, 0, 2),
            g_w_bs.reshape(SWA_W, N_DEV, c_br).transpose(1, 0, 2),
            g_w_o.reshape(N_DEV, d // N_DEV, d),
        ),
    )
    pad = FLOG_PAD - N_FLOG
    g_blocks = jnp.stack(
        [g_w_in[k * c_in : (k + 1) * c_in] for k in range(k_cut)]
        + [jnp.concatenate([g_w_in[k_cut * c_in : REAL_FLOG_END], g_w_in[OFF_SQ : (k_cut + 1) * c_in + pad]], axis=0)]
        + [g_w_in[k * c_in + pad : (k + 1) * c_in + pad] for k in range(k_cut + 1, N_DEV)]
    )

    grad_x, red2, r_in = _in_bwd(dproj, w_in_pad, x2, ada, dza, ride=(g_blocks,))
    out_w_in = _sum_adam_t(r_in, w_in[0].T, m_w_in[0].T, v_w_in[0].T, "adam_w_in")
    out_w_in = [o.T for o in out_w_in]
    out_w_bf = _sum_adam(r_bf, w_br_fox[0], m_w_br_fox[0], v_w_br_fox[0], "adam_w_br_fox")
    out_w_bs = _sum_adam(r_bs, w_br_swa[0], m_w_br_swa[0], v_w_br_swa[0], "adam_w_br_swa")
    out_w_o = _sum_adam(r_o, w_out[0], m_w_out[0], v_w_out[0], "adam_w_out")

    packed = jnp.concatenate([red2[0:1], red2[1:2], red[0:1], _pad_lanes(dbf[:, 0].reshape(1, FOX_H), 128), dsink, red[1:2], red[2:3]], axis=1)
    gathered = _gather_rows(packed, "ag_small")
    pack = lambda a, b, cc, dd, e: jnp.concatenate([a, _pad_lanes(b, 128), _pad_lanes(cc, 128), dd, e], axis=1)
    small = _small_adam(
        gathered,
        pack(b_ada, b_f, attn_sinks, ln_g, ln_b),
        pack(m_b_ada, m_b_f, m_attn_sinks, m_ln_g, m_ln_b),
        pack(v_b_ada, v_b_f, v_attn_sinks, v_ln_g, v_ln_b),
    )
    dada_cols = lax.dynamic_slice(gathered, (0, me * c_ada), (N_DEV, c_ada))
    out_w_ada = _wada_adam(c_all.T, dada_cols, w_ada[0], m_w_ada[0], v_w_ada[0])

    o1, o2, o3 = 3 * d, 3 * d + 128, 3 * d + 256

    def unpack(p):
        return p[:, :o1], p[:, o1 : o1 + FOX_H], p[:, o2 : o2 + SWA_HQ], p[:, o3 : o3 + d], p[:, o3 + d : o3 + 2 * d]

    kinds = []
    for k in range(4):
        b_ada_k, b_f_k, sinks_k, ln_g_k, ln_b_k = unpack(small[k])
        kinds.append(
            [out_w_ada[k][None], b_ada_k, out_w_in[k][None], b_f_k, sinks_k, out_w_bf[k][None], out_w_bs[k][None], out_w_o[k][None], ln_g_k, ln_b_k]
        )
    return (loss, grad_x[None], *kinds[0], *kinds[1], *kinds[2], *kinds[3])
```

```python
import numpy as np
import jax
import jax.numpy as jnp
from jax import lax
from jax.experimental import pallas as pl
from jax.experimental.pallas import tpu as pltpu

F32 = jnp.float32
BF16 = jnp.bfloat16
N_DEV = 8
MESH = pl.DeviceIdType.MESH

FOX_H, FOX_DH, FOX_W = 8, 128, 1024
SWA_HQ, SWA_HKV, SWA_DH, SWA_G = 16, 4, 64, 4
SWA_W, SWA_KVW, WINDOW = 1024, 256, 128
LN_EPS = 1e-5
NEG = -1e30
DEPTH = 1
ALPHA = (2.0 * DEPTH) ** 0.25
FOX_SCALE = FOX_DH ** -0.5
SWA_SCALE = SWA_DH ** -0.5
SLOPES = [2.0 ** (-8.0 * (h + 1.0) / SWA_HQ) for h in range(SWA_HQ)]

ADAM_LR, ADAM_B1, ADAM_B2, ADAM_EPS, ADAM_WD, ADAM_STEP = 0.001, 0.9, 0.999, 1e-08, 0.01, 10

N_FLOG = 8
FLOG_PAD = 512
OFF_FQ, OFF_FK, OFF_FV, OFF_FLOG = 0, 1024, 2048, 3072
OFF_SQ = OFF_FLOG + FLOG_PAD
OFF_SK = OFF_SQ + SWA_W
OFF_SV = OFF_SK + SWA_KVW
OFF_GF = OFF_SV + SWA_KVW
OFF_GS = OFF_GF + FOX_W
OFF_MF = OFF_GS + SWA_W
REAL_FLOG_END = OFF_FLOG + N_FLOG

ATT_BLK = 512
VMEM_LIMIT = 58 * 1024 * 1024


def _pcall(body, **kw):
    return pl.pallas_call(body, **kw)


def _cp(*sem):
    return pltpu.CompilerParams(dimension_semantics=sem, vmem_limit_bytes=VMEM_LIMIT)


def _sigmoid(x):
    return 0.5 * jnp.tanh(0.5 * x) + 0.5


def _all_gather(x, name, space):
    m_per, n = x.shape

    def body(x_ref, out_ref, send_sems, recv_sems, local_sem):
        mx, my, mc = lax.axis_index("x"), lax.axis_index("y"), lax.axis_index("c")
        me, sibling = (mx, my, mc), (mx, my, 1 - mc)
        xn, yn, dg = (1 - mx, my), (mx, 1 - my), (1 - mx, 1 - my)
        south = mc == 0
        src_chip = (jnp.where(south, 1 - mx, mx), jnp.where(south, my, 1 - my))
        dst_chip = (jnp.where(south, mx, 1 - mx), jnp.where(south, 1 - my, my))

        def rows(px, py, pc):
            return out_ref.at[4 * px + 2 * py + pc]

        def copy(k, block, to, src=None):
            return pltpu.make_async_remote_copy(
                src_ref=rows(*block) if src is None else src,
                dst_ref=rows(*block),
                send_sem=send_sems.at[k],
                recv_sem=recv_sems.at[k],
                device_id=to,
                device_id_type=MESH,
            )

        mine = pltpu.make_async_copy(x_ref, rows(*me), local_sem)
        mine.start()
        first = [copy(0, me, sibling, src=x_ref), copy(1, me, (*xn, mc), src=x_ref), copy(2, me, (*yn, mc), src=x_ref)]
        for cp in first:
            cp.start()
        copy(1, (*xn, mc), me).wait_recv()
        copy(2, (*yn, mc), me).wait_recv()
        later = [copy(3, (*src_chip, mc), (*dst_chip, mc)), copy(4, (*xn, mc), sibling), copy(5, (*yn, mc), sibling)]
        for cp in later:
            cp.start()
        copy(3, (*dg, mc), me).wait_recv()
        last = copy(6, (*dg, mc), sibling)
        last.start()
        copy(0, sibling, me).wait_recv()
        for k, chip in ((4, xn), (5, yn), (6, dg)):
            copy(k, (*chip, 1 - mc), me).wait_recv()
        for cp in first + later + [last]:
            cp.wait_send()
        mine.wait()

    return _pcall(
        body,
        name=name,
        out_shape=jax.ShapeDtypeStruct((N_DEV, m_per, n), x.dtype),
        in_specs=[pl.BlockSpec(memory_space=space)],
        out_specs=pl.BlockSpec(memory_space=space),
        scratch_shapes=[pltpu.SemaphoreType.DMA((7,)), pltpu.SemaphoreType.DMA((7,)), pltpu.SemaphoreType.DMA],
    )(x)


def _peer(d, mx, my, mc):
    return (1 - mx if (d >> 2) & 1 else mx, 1 - my if (d >> 1) & 1 else my, 1 - mc if d & 1 else mc)


def _rider_copies(kind, ins, outs, send_sems, recv_sems, local_sems):
    mx, my, mc = lax.axis_index("x"), lax.axis_index("y"), lax.axis_index("c")
    me = 4 * mx + 2 * my + mc
    remote, local = [], []
    for a in range(len(ins)):
        if kind == "gather":
            m_per = ins[a].shape[0]
            mine = outs[a].at[pl.ds(me * m_per, m_per), :]
            local.append(pltpu.make_async_copy(ins[a], mine, local_sems.at[a]))
        else:
            local.append(pltpu.make_async_copy(ins[a].at[me], outs[a].at[0], local_sems.at[a]))
        for d in range(1, N_DEV):
            px, py, pc = _peer(d, mx, my, mc)
            if kind == "gather":
                src, dst = ins[a], mine
            else:
                src, dst = ins[a].at[4 * px + 2 * py + pc], outs[a].at[d]
            remote.append(
                pltpu.make_async_remote_copy(
                    src_ref=src,
                    dst_ref=dst,
                    send_sem=send_sems.at[a * 7 + d - 1],
                    recv_sem=recv_sems.at[a * 7 + d - 1],
                    device_id=(px, py, pc),
                    device_id_type=MESH,
                )
            )
    return remote, local


def _rider_start(*args):
    remote, local = _rider_copies(*args)
    for cp in local + remote:
        cp.start()


def _rider_wait(*args):
    remote, local = _rider_copies(*args)
    for cp in remote:
        cp.wait_recv()
    for cp in remote:
        cp.wait_send()
    for cp in local:
        cp.wait()


def _rider_scratch(n):
    return [pltpu.SemaphoreType.DMA((7 * n,)), pltpu.SemaphoreType.DMA((7 * n,)), pltpu.SemaphoreType.DMA((n,))]


def _gather_rows(v, name):
    n = v.shape[1]
    return _all_gather(jnp.broadcast_to(v, (8, n)), name, pltpu.VMEM)[:, 0, :]


def _adamw(w, g, m, v):
    m = ADAM_B1 * m + (1.0 - ADAM_B1) * g
    v = ADAM_B2 * v + (1.0 - ADAM_B2) * (g * g)
    m_hat = m / (1.0 - ADAM_B1**ADAM_STEP)
    v_hat = v / (1.0 - ADAM_B2**ADAM_STEP)
    delta = -ADAM_LR * (m_hat / (jnp.sqrt(v_hat) + ADAM_EPS) + ADAM_WD * w)
    return delta, m, v


def _sum_adam(recv, w, m, v, name):
    _, r_tot, c = recv.shape
    c_pad = -(-c // 128) * 128
    tr = r_tot
    while 8 * tr * c_pad * 4 > 6 * 1024 * 1024 and tr % 32 == 0:
        tr //= 2

    def body(r_ref, w_ref, m_ref, v_ref, g_ref, d_ref, nm_ref, nv_ref):
        g = r_ref[0].astype(F32)
        for k in range(1, N_DEV):
            g = g + r_ref[k].astype(F32)
        d, nm, nv = _adamw(w_ref[...], g, m_ref[...], v_ref[...])
        g_ref[...] = g
        d_ref[...] = d
        nm_ref[...] = nm
        nv_ref[...] = nv

    blk = pl.BlockSpec((tr, c), lambda i: (i, 0))
    return _pcall(
        body,
        name=name,
        grid=(r_tot // tr,),
        out_shape=[jax.ShapeDtypeStruct((r_tot, c), F32)] * 4,
        in_specs=[pl.BlockSpec((N_DEV, tr, c), lambda i: (0, i, 0)), blk, blk, blk],
        out_specs=[blk] * 4,
        compiler_params=_cp("parallel"),
    )(recv, w, m, v)


def _sum_adam_t(recv, w, m, v, name):
    _, c, r_tot = recv.shape
    tr = min(256, r_tot)

    def body(r_ref, w_ref, m_ref, v_ref, g_ref, d_ref, nm_ref, nv_ref):
        g = r_ref[0].astype(F32)
        for k in range(1, N_DEV):
            g = g + r_ref[k].astype(F32)
        d, nm, nv = _adamw(w_ref[...], g, m_ref[...], v_ref[...])
        g_ref[...] = g
        d_ref[...] = d
        nm_ref[...] = nm
        nv_ref[...] = nv

    blk = pl.BlockSpec((c, tr), lambda i: (0, i))
    return _pcall(
        body,
        name=name,
        grid=(r_tot // tr,),
        out_shape=[jax.ShapeDtypeStruct((c, r_tot), F32)] * 4,
        in_specs=[pl.BlockSpec((N_DEV, c, tr), lambda i: (0, 0, i)), blk, blk, blk],
        out_specs=[blk] * 4,
        compiler_params=_cp("parallel"),
    )(recv, w, m, v)


def _wada_adam(c_t, dada_cols, w, m, v):
    d_model, c = w.shape
    tr = min(256, d_model)

    def body(ct_ref, da_ref, w_ref, m_ref, v_ref, g_ref, d_ref, nm_ref, nv_ref):
        g = jnp.dot(ct_ref[...].astype(BF16), da_ref[...].astype(BF16), preferred_element_type=F32)
        d, nm, nv = _adamw(w_ref[...], g, m_ref[...], v_ref[...])
        g_ref[...] = g
        d_ref[...] = d
        nm_ref[...] = nm
        nv_ref[...] = nv

    blk = pl.BlockSpec((tr, c), lambda i: (i, 0))
    return _pcall(
        body,
        name="wada_adam",
        grid=(d_model // tr,),
        out_shape=[jax.ShapeDtypeStruct((d_model, c), F32)] * 4,
        in_specs=[pl.BlockSpec((tr, N_DEV), lambda i: (i, 0)), pl.BlockSpec((N_DEV, c), lambda i: (0, 0)), blk, blk, blk],
        out_specs=[blk] * 4,
        compiler_params=_cp("parallel"),
    )(c_t, dada_cols, w, m, v)


def _small_adam(gathered, w, m, v):
    p = w.shape[1]

    def body(a_ref, w_ref, m_ref, v_ref, g_ref, d_ref, nm_ref, nv_ref):
        g = a_ref[0:1, :]
        for k in range(1, N_DEV):
            g = g + a_ref[k : k + 1, :]
        d, nm, nv = _adamw(w_ref[...], g, m_ref[...], v_ref[...])
        g_ref[...] = g
        d_ref[...] = d
        nm_ref[...] = nm
        nv_ref[...] = nv

    return _pcall(
        body,
        name="small_adam",
        out_shape=[jax.ShapeDtypeStruct((1, p), F32)] * 4,
    )(gathered, w, m, v)


def _ada_fwd(c_all, w_ada, b_cols):
    c = w_ada.shape[1]

    def body(c_ref, w_ref, b_ref, o_ref):
        o_ref[...] = jnp.dot(c_ref[...].astype(BF16), w_ref[...].astype(BF16), preferred_element_type=F32) + b_ref[...]

    return _pcall(
        body,
        name="ada_fwd",
        out_shape=jax.ShapeDtypeStruct((N_DEV, c), F32),
        compiler_params=_cp(),
    )(c_all, w_ada, b_cols)


def _ln_mod(x, ada):
    s_len, d = x.shape
    tm = min(512, s_len)

    def body(x_ref, sh_ref, sc_ref, h_ref):
        xv = x_ref[...]
        mu = jnp.mean(xv, axis=-1, keepdims=True)
        xc = xv - mu
        var = jnp.mean(xc * xc, axis=-1, keepdims=True)
        xhat = xc * lax.rsqrt(var + LN_EPS)
        h_ref[...] = (xhat * (1.0 + sc_ref[...]) + sh_ref[...]).astype(BF16)

    return _pcall(
        body,
        name="ln_mod",
        grid=(s_len // tm,),
        out_shape=jax.ShapeDtypeStruct((s_len, d), BF16),
        in_specs=[
            pl.BlockSpec((tm, d), lambda i: (i, 0)),
            pl.BlockSpec((1, d), lambda i: (0, 0)),
            pl.BlockSpec((1, d), lambda i: (0, 1)),
        ],
        out_specs=pl.BlockSpec((tm, d), lambda i: (i, 0)),
        compiler_params=_cp("parallel"),
    )(x, ada, ada)


def _mm_cols(a, b, col_off, n_cols, out_dtype, name, ride=()):
    m, k = a.shape
    tm = min(1024, m)
    tn = next(t for t in (1024, 512, 128) if n_cols % t == 0 and col_off % t == 0)
    off = col_off // tn
    ni, nj = m // tm, n_cols // tn
    n = len(ride)

    def body(a_ref, b_ref, *rest):
        ins, o_ref, outs, sems = rest[:n], rest[n], rest[n + 1 : 2 * n + 1], rest[2 * n + 1 :]
        i, j = pl.program_id(0), pl.program_id(1)
        if n:

            @pl.when((i == 0) & (j == 0))
            def _():
                _rider_start("gather", ins, outs, *sems)

        o_ref[...] = lax.dot_general(a_ref[...], b_ref[...], _NT, preferred_element_type=F32).astype(out_dtype)
        if n:

            @pl.when((i == ni - 1) & (j == nj - 1))
            def _():
                _rider_wait("gather", ins, outs, *sems)

    hbm = pl.BlockSpec(memory_space=pltpu.HBM)
    out = _pcall(
        body,
        name=name,
        grid=(ni, nj),
        out_shape=[jax.ShapeDtypeStruct((m, n_cols), out_dtype)]
        + [jax.ShapeDtypeStruct((N_DEV * r.shape[0], r.shape[1]), r.dtype) for r in ride],
        in_specs=[pl.BlockSpec((tm, k), lambda i, j: (i, 0)), pl.BlockSpec((tn, k), lambda i, j: (off + j, 0))] + [hbm] * n,
        out_specs=[pl.BlockSpec((tm, tn), lambda i, j: (i, j))] + [hbm] * n,
        scratch_shapes=_rider_scratch(n) if n else [],
        compiler_params=_cp("arbitrary", "arbitrary") if n else _cp("parallel", "parallel"),
    )(a, b, *ride)
    return out if n else out[0]


def _mm_tn(a, b, name, ride=()):
    s_len, m = a.shape
    n = b.shape[1]
    tm, tn, ts = min(1024, m), min(2048, n), min(2048, s_len)
    ni, nj, ns = m // tm, n // tn, s_len // ts
    nr = len(ride)

    def body(a_ref, b_ref, *rest):
        ins, o_ref, outs = rest[:nr], rest[nr], rest[nr + 1 : 2 * nr + 1]
        sems, acc_s = rest[2 * nr + 1 : -1], rest[-1]
        i, j, kk = pl.program_id(0), pl.program_id(1), pl.program_id(2)
        if nr:

            @pl.when((i == 0) & (j == 0) & (kk == 0))
            def _():
                _rider_start("exchange", ins, outs, *sems)

            @pl.when((i == ni - 1) & (j == nj - 1) & (kk == ns - 1))
            def _():
                _rider_wait("exchange", ins, outs, *sems)

        part = lax.dot_general(a_ref[...], b_ref[...], _TN, preferred_element_type=F32)

        @pl.when(kk == 0)
        def _():
            acc_s[...] = part

        @pl.when(kk > 0)
        def _():
            acc_s[...] += part

        @pl.when(kk == ns - 1)
        def _():
            o_ref[...] = acc_s[...].astype(BF16)

    hbm = pl.BlockSpec(memory_space=pltpu.HBM)
    out = _pcall(
        body,
        name=name,
        grid=(ni, nj, ns),
        out_shape=[jax.ShapeDtypeStruct((m, n), BF16)] + [jax.ShapeDtypeStruct(r.shape, r.dtype) for r in ride],
        in_specs=[pl.BlockSpec((ts, tm), lambda i, j, kk: (kk, i)), pl.BlockSpec((ts, tn), lambda i, j, kk: (kk, j))] + [hbm] * nr,
        out_specs=[pl.BlockSpec((tm, tn), lambda i, j, kk: (i, j))] + [hbm] * nr,
        scratch_shapes=(_rider_scratch(nr) if nr else []) + [pltpu.VMEM((tm, tn), F32)],
        compiler_params=_cp("arbitrary", "arbitrary", "arbitrary") if nr else _cp("parallel", "parallel", "arbitrary"),
    )(a, b, *ride)
    return out if nr else out[0]


def _split3(a):
    hi = a.astype(BF16)
    r1 = a - hi.astype(F32)
    mid = r1.astype(BF16)
    lo = (r1 - mid.astype(F32)).astype(BF16)
    return hi, mid, lo


def _dot_ones(a, tri):
    return sum(jnp.dot(t, tri, preferred_element_type=F32) for t in _split3(a))


def _log_sigmoid(x):
    return jnp.minimum(x, 0.0) - jnp.log1p(jnp.exp(-jnp.abs(x)))


def _fox_cum(flog_t, bf_col):
    s_len = flog_t.shape[1]

    def body(fl_ref, bf_ref, cum_ref):
        r = lax.broadcasted_iota(jnp.int32, (128, 128), 0)
        c = lax.broadcasted_iota(jnp.int32, (128, 128), 1)
        upper = (r <= c).astype(BF16)

        def step(t, carry):
            sl = pl.ds(pl.multiple_of(t * 128, 128), 128)
            lf = _log_sigmoid(fl_ref[:, sl] + bf_ref[...])
            cs = _dot_ones(lf, upper) + carry
            cum_ref[:, sl] = cs
            return cs[:, 127:128]

        lax.fori_loop(0, s_len // 128, step, jnp.zeros((FOX_H, 1), F32))

    return _pcall(body, name="fox_cum", out_shape=jax.ShapeDtypeStruct((FOX_H, s_len), F32))(flog_t, bf_col)


def _fox_gate_bwd(drow, dcol, flog_t, bf_col):
    s_len = flog_t.shape[1]
    n = s_len // 128

    def body(dr_ref, dc_ref, fl_ref, bf_ref, dfl_ref, dbf_ref):
        r = lax.broadcasted_iota(jnp.int32, (128, 128), 0)
        c = lax.broadcasted_iota(jnp.int32, (128, 128), 1)
        lower = (r >= c).astype(BF16)

        def step(t, carry):
            run, tot = carry
            sl = pl.ds(pl.multiple_of((n - 1 - t) * 128, 128), 128)
            rc = _dot_ones(dr_ref[:, sl] - dc_ref[:, sl], lower) + run
            dfl = rc * _sigmoid(-(fl_ref[:, sl] + bf_ref[...]))
            dfl_ref[:, sl] = dfl
            return rc[:, 0:1], tot + jnp.sum(dfl, axis=1, keepdims=True)

        zero = jnp.zeros((FOX_H, 1), F32)
        _, tot = lax.fori_loop(0, n, step, (zero, zero))
        dbf_ref[...] = jnp.broadcast_to(tot, (FOX_H, 128))

    return _pcall(
        body,
        name="fox_gate_bwd",
        out_shape=[jax.ShapeDtypeStruct((FOX_H, s_len), F32), jax.ShapeDtypeStruct((FOX_H, 128), F32)],
    )(drow, dcol, flog_t, bf_col)


def _diag_mask(blk, transposed=False):
    r = lax.broadcasted_iota(jnp.int32, (blk, blk), 0)
    c = lax.broadcasted_iota(jnp.int32, (blk, blk), 1)
    return c >= r if transposed else r >= c


_NT = (((1,), (1,)), ((), ()))
_TN = (((0,), (0,)), ((), ()))


def _fox_fwd(qkv, cum_row):
    s_len = qkv.shape[0]
    blk = min(ATT_BLK, s_len)
    nb = s_len // blk
    log2e = 1.4426950408889634

    def body(q_ref, k_ref, v_ref, c_ref, o_ref, lse_ref, mx_s, acc_s, u_s, v1_s):
        i = pl.program_id(1)

        @pl.when(i == 0)
        def _():
            v1_s[:, :FOX_DH] = v_ref[...]
            v1_s[:, FOX_DH:] = (lax.broadcasted_iota(jnp.int32, (s_len, FOX_DH), 1) == 0).astype(BF16)

        def key_cols(j, n):
            return pl.ds(pl.multiple_of(j * blk, blk), n * blk)

        def walk(tile):
            def four_pairs(t, carry):
                for u in range(4):
                    tile(8 * t + 2 * u, 2, False)
                return carry

            lax.fori_loop(0, i // 8, four_pairs, 0)

            @pl.when((i // 4) % 2 == 1)
            def _():
                tile(8 * (i // 8), 2, False)
                tile(8 * (i // 8) + 2, 2, False)

            @pl.when((i // 2) % 2 == 1)
            def _():
                tile(4 * (i // 4), 2, False)

            @pl.when(i % 2 == 1)
            def _():
                tile(i - 1, 1, False)

            tile(i, 1, True)

        def lane_max(j, n, masked):
            cols = key_cols(j, n)
            u = lax.dot_general(q_ref[...], k_ref[cols, :], _NT, preferred_element_type=F32) * (FOX_SCALE * log2e) - c_ref[:, cols] * log2e
            if masked:
                u = jnp.where(_diag_mask(blk), u, NEG)
            u_s[:, cols] = u
            part = u[:, 0:128]
            for t in range(1, n * blk // 128):
                part = jnp.maximum(part, u[:, t * 128 : (t + 1) * 128])
            mx_s[...] = jnp.maximum(mx_s[...], part)

        mx_s[...] = jnp.full(mx_s.shape, NEG, F32)
        walk(lane_max)
        m = jnp.max(mx_s[...], axis=1, keepdims=True)

        def weigh(j, n, masked):
            cols = key_cols(j, n)
            p = jnp.exp2(u_s[:, cols] - m)
            acc_s[...] += jnp.dot(p.astype(BF16), v1_s[cols, :], preferred_element_type=F32)

        acc_s[...] = jnp.zeros(acc_s.shape, F32)
        walk(weigh)
        l = acc_s[:, FOX_DH : FOX_DH + 1]
        o_ref[...] = acc_s[:, :FOX_DH] / l
        lse_ref[...] = m * (1.0 / log2e) + jnp.log(l)

    return _pcall(
        body,
        name="fox_fwd",
        grid=(FOX_H, nb),
        out_shape=[jax.ShapeDtypeStruct((s_len, FOX_W), F32), jax.ShapeDtypeStruct((FOX_H, s_len, 1), F32)],
        in_specs=[
            pl.BlockSpec((blk, FOX_DH), lambda h, i: (i, h)),
            pl.BlockSpec((s_len, FOX_DH), lambda h, i: (0, FOX_H + h)),
            pl.BlockSpec((s_len, FOX_DH), lambda h, i: (0, 2 * FOX_H + h)),
            pl.BlockSpec((None, 1, s_len), lambda h, i: (h, 0, 0)),
        ],
        out_specs=[
            pl.BlockSpec((blk, FOX_DH), lambda h, i: (i, h)),
            pl.BlockSpec((None, blk, 1), lambda h, i: (h, i, 0)),
        ],
        scratch_shapes=[
            pltpu.VMEM((blk, 128), F32),
            pltpu.VMEM((blk, 2 * FOX_DH), F32),
            pltpu.VMEM((blk, s_len), F32),
            pltpu.VMEM((s_len, 2 * FOX_DH), BF16),
        ],
        compiler_params=_cp("arbitrary", "arbitrary"),
    )(qkv, qkv, qkv, cum_row)


def _fox_bwd(qkv, cum_col, lse_row, delta_row, do):
    s_len = qkv.shape[0]
    blk = min(ATT_BLK, s_len)
    nb = s_len // blk

    def body(q_ref, k_ref, v_ref, c_ref, lse_ref, dl_ref, do_ref, dq_ref, dk_ref, dv_ref, dc_ref, dr_ref, dk_s, dv_s, dc_s, cb_s, dq_s):
        j = pl.program_id(1)

        @pl.when(j == 0)
        def _():
            dq_s[...] = jnp.zeros(dq_s.shape, F32)
            dr_ref[...] = jnp.zeros(dr_ref.shape, F32)

        dk_s[...] = jnp.zeros(dk_s.shape, F32)
        dv_s[...] = jnp.zeros(dv_s.shape, F32)
        dc_s[...] = jnp.zeros(dc_s.shape, F32)
        cb_s[...] = jnp.broadcast_to(c_ref[...], cb_s.shape)

        def tile(i, n, diag):
            rows = pl.ds(pl.multiple_of(i * blk, blk), n * blk)
            q, dob = q_ref[rows, :], do_ref[rows, :]
            k, v = k_ref[...], v_ref[...]
            s_t = lax.dot_general(k, q, _NT, preferred_element_type=F32) * FOX_SCALE - cb_s[:, : n * blk]
            p_t = jnp.exp(s_t - lse_ref[:, rows])
            if diag:
                p_t = jnp.where(_diag_mask(blk, transposed=True), p_t, 0.0)
            dp_t = lax.dot_general(v, dob, _NT, preferred_element_type=F32)
            ds_t = p_t * (dp_t - dl_ref[:, rows])
            dsb = ds_t.astype(BF16)
            dv_s[...] += jnp.dot(p_t.astype(BF16), dob, preferred_element_type=F32)
            dk_s[...] += jnp.dot(dsb, q, preferred_element_type=F32)
            dq_c = lax.dot_general(dsb, k, _TN, preferred_element_type=F32)
            part = ds_t[:, 0:128]
            for t in range(1, n * blk // 128):
                part = part + ds_t[:, t * 128 : (t + 1) * 128]
            dc_s[...] += part
            dr_ref[:, rows] += jnp.sum(ds_t, axis=0, keepdims=True)
            if diag:
                dq_s[rows, :] = (dq_s[rows, :] + dq_c) * FOX_SCALE
            else:
                dq_s[rows, :] += dq_c

        tile(j, 1, True)
        below = nb - 1 - j
        b0, b1, b2 = below % 2, (below // 2) % 2, (below // 4) % 2

        @pl.when(b0 == 1)
        def _():
            tile(j + 1, 1, False)

        @pl.when(b1 == 1)
        def _():
            tile(j + 1 + b0, 2, False)

        @pl.when(b2 == 1)
        def _():
            tile(j + 1 + b0 + 2 * b1, 2, False)
            tile(j + 3 + b0 + 2 * b1, 2, False)

        first = j + 1 + b0 + 2 * b1 + 4 * b2

        def four_pairs(t, carry):
            for u in range(4):
                tile(first + 8 * t + 2 * u, 2, False)
            return carry

        lax.fori_loop(0, below // 8, four_pairs, 0)
        dk_ref[...] = (dk_s[...] * FOX_SCALE).astype(BF16)
        dv_ref[...] = dv_s[...].astype(BF16)
        dc_ref[...] = jnp.sum(dc_s[...], axis=1, keepdims=True)

        @pl.when(j == nb - 1)
        def _():
            dq_ref[...] = dq_s[...].astype(BF16)

    head = lambda h, j: (0, h)
    row = pl.BlockSpec((None, 1, s_len), lambda h, j: (h, 0, 0))
    return _pcall(
        body,
        name="fox_bwd",
        grid=(FOX_H, nb),
        out_shape=[
            jax.ShapeDtypeStruct((s_len, FOX_W), BF16),
            jax.ShapeDtypeStruct((s_len, FOX_W), BF16),
            jax.ShapeDtypeStruct((s_len, FOX_W), BF16),
            jax.ShapeDtypeStruct((FOX_H, s_len, 1), F32),
            jax.ShapeDtypeStruct((FOX_H, 1, s_len), F32),
        ],
        in_specs=[
            pl.BlockSpec((s_len, FOX_DH), head),
            pl.BlockSpec((blk, FOX_DH), lambda h, j: (j, FOX_H + h)),
            pl.BlockSpec((blk, FOX_DH), lambda h, j: (j, 2 * FOX_H + h)),
            pl.BlockSpec((None, blk, 1), lambda h, j: (h, j, 0)),
            row,
            row,
            pl.BlockSpec((s_len, FOX_DH), head),
        ],
        out_specs=[
            pl.BlockSpec((s_len, FOX_DH), head),
            pl.BlockSpec((blk, FOX_DH), lambda h, j: (j, h)),
            pl.BlockSpec((blk, FOX_DH), lambda h, j: (j, h)),
            pl.BlockSpec((None, blk, 1), lambda h, j: (h, j, 0)),
            row,
        ],
        scratch_shapes=[
            pltpu.VMEM((blk, FOX_DH), F32),
            pltpu.VMEM((blk, FOX_DH), F32),
            pltpu.VMEM((blk, 128), F32),
            pltpu.VMEM((blk, 2 * blk), F32),
            pltpu.VMEM((s_len, FOX_DH), F32),
        ],
        compiler_params=_cp("parallel", "arbitrary"),
    )(qkv, qkv, qkv, cum_col, lse_row, delta_row, do)


def _swa_bias():
    cols = SWA_G * WINDOW
    k = np.arange(2 * WINDOW)[:, None]
    q = np.arange(cols)[None, :]
    dist = (q % WINDOW) - k + WINDOW
    valid = (dist >= 0) & (dist < WINDOW)
    out = np.empty((2, SWA_HKV, 2 * WINDOW, cols), np.float32)
    for g in range(SWA_HKV):
        slope = np.array([SLOPES[g * SWA_G + t] for t in range(SWA_G)], np.float32)[q // WINDOW]
        bias = -(slope * dist.astype(np.float32))
        out[0, g] = np.where(valid & (k >= WINDOW), bias, np.float32(NEG))
        out[1, g] = np.where(valid, bias, np.float32(NEG))
    return jnp.asarray(out)


def _swa_group(i, q_ref, kk, sinks_ref, bias_ref, g):
    cols = SWA_G * WINDOW
    head = lax.broadcasted_iota(jnp.int32, (1, cols), 1) // WINDOW
    sink = jnp.zeros((1, cols), F32)
    for t in range(SWA_G):
        sink = jnp.where(head == t, sinks_ref[g * SWA_G + t], sink)
    q = jnp.concatenate([q_ref[:, (g * SWA_G + t) * SWA_DH : (g * SWA_G + t + 1) * SWA_DH] for t in range(SWA_G)], axis=0)
    k = kk[:, g * SWA_DH : (g + 1) * SWA_DH]
    s = lax.dot_general(k, q, _NT, preferred_element_type=F32) * SWA_SCALE + bias_ref[jnp.minimum(i, 1), g]
    m = jnp.maximum(jnp.max(s, axis=0, keepdims=True), sink)
    e = jnp.exp(s - m)
    e_sink = jnp.exp(sink - m)
    inv = 1.0 / (jnp.sum(e, axis=0, keepdims=True) + e_sink)
    return q, k, e * inv, e_sink * inv


def _swa_specs(col_q, col_k, col_v, rev, nb):
    def blk(t):
        return nb - 1 - t if rev else t

    return [
        pl.BlockSpec((WINDOW, SWA_W), lambda t: (blk(t), col_q)),
        pl.BlockSpec((WINDOW, SWA_KVW), lambda t: (jnp.maximum(blk(t) - 1, 0), col_k)),
        pl.BlockSpec((WINDOW, SWA_KVW), lambda t: (blk(t), col_k)),
        pl.BlockSpec((WINDOW, SWA_KVW), lambda t: (jnp.maximum(blk(t) - 1, 0), col_v)),
        pl.BlockSpec((WINDOW, SWA_KVW), lambda t: (blk(t), col_v)),
    ]


def _swa_fwd(qkv, sinks):
    s_len = qkv.shape[0]
    nb = s_len // WINDOW
    bias_spec = pl.BlockSpec((2, SWA_HKV, 2 * WINDOW, SWA_G * WINDOW), lambda t: (0, 0, 0, 0))

    def body(q_ref, kp_ref, kc_ref, vp_ref, vc_ref, sinks_ref, bias_ref, o_ref):
        i = pl.program_id(0)
        kk = jnp.concatenate([kp_ref[...], kc_ref[...]], axis=0)
        vv = jnp.concatenate([vp_ref[...], vc_ref[...]], axis=0)
        for g in range(SWA_HKV):
            _, _, p, _ = _swa_group(i, q_ref, kk, sinks_ref, bias_ref, g)
            o = lax.dot_general(p.astype(BF16), vv[:, g * SWA_DH : (g + 1) * SWA_DH], _TN, preferred_element_type=F32)
            for t in range(SWA_G):
                h = g * SWA_G + t
                o_ref[:, h * SWA_DH : (h + 1) * SWA_DH] = o[t * WINDOW : (t + 1) * WINDOW, :]

    return _pcall(
        body,
        name="swa_fwd",
        grid=(nb,),
        out_shape=jax.ShapeDtypeStruct((s_len, SWA_W), F32),
        in_specs=_swa_specs(0, 4, 5, False, nb) + [pl.BlockSpec(memory_space=pltpu.SMEM), bias_spec],
        out_specs=pl.BlockSpec((WINDOW, SWA_W), lambda t: (t, 0)),
        compiler_params=_cp("parallel"),
    )(qkv, qkv, qkv, qkv, qkv, sinks, _swa_bias())


def _swa_bwd(qkv, sinks, do):
    s_len = qkv.shape[0]
    nb = s_len // WINDOW
    bias_spec = pl.BlockSpec((2, SWA_HKV, 2 * WINDOW, SWA_G * WINDOW), lambda t: (0, 0, 0, 0))

    def body(q_ref, kp_ref, kc_ref, vp_ref, vc_ref, sinks_ref, bias_ref, do_ref, dq_ref, dk_ref, dv_ref, dsink_ref, ck_s, cv_s, dkk_s, dvv_s):
        t = pl.program_id(0)
        i = nb - 1 - t

        @pl.when(t == 0)
        def _():
            ck_s[...] = jnp.zeros(ck_s.shape, F32)
            cv_s[...] = jnp.zeros(cv_s.shape, F32)
            dsink_ref[...] = jnp.zeros(dsink_ref.shape, F32)

        kk = jnp.concatenate([kp_ref[...], kc_ref[...]], axis=0)
        vv = jnp.concatenate([vp_ref[...], vc_ref[...]], axis=0)
        lane = lax.broadcasted_iota(jnp.int32, (1, 128), 1)
        dsink = jnp.zeros((1, 128), F32)
        for g in range(SWA_HKV):
            cols = slice(g * SWA_DH, (g + 1) * SWA_DH)
            q, k, p, p_sink = _swa_group(i, q_ref, kk, sinks_ref, bias_ref, g)
            dob = jnp.concatenate([do_ref[:, (g * SWA_G + t) * SWA_DH : (g * SWA_G + t + 1) * SWA_DH] for t in range(SWA_G)], axis=0)
            dp = lax.dot_general(vv[:, cols], dob, _NT, preferred_element_type=F32)
            delta = jnp.sum(p * dp, axis=0, keepdims=True)
            dsb = (p * (dp - delta)).astype(BF16)
            dq = (lax.dot_general(dsb, k, _TN, preferred_element_type=F32) * SWA_SCALE).astype(BF16)
            ps_d = p_sink * delta
            for t in range(SWA_G):
                h = g * SWA_G + t
                dq_ref[:, h * SWA_DH : (h + 1) * SWA_DH] = dq[t * WINDOW : (t + 1) * WINDOW, :]
                dsink = dsink + jnp.where(lane == h, -jnp.sum(ps_d[:, t * WINDOW : (t + 1) * WINDOW], axis=1, keepdims=True), 0.0)
            dkk_s[:, cols] = jnp.dot(dsb, q, preferred_element_type=F32) * SWA_SCALE
            dvv_s[:, cols] = jnp.dot(p.astype(BF16), dob, preferred_element_type=F32)
        dk_ref[...] = (dkk_s[WINDOW:, :] + ck_s[...]).astype(BF16)
        dv_ref[...] = (dvv_s[WINDOW:, :] + cv_s[...]).astype(BF16)
        ck_s[...] = dkk_s[:WINDOW, :]
        cv_s[...] = dvv_s[:WINDOW, :]
        dsink_ref[...] += dsink

    row = lambda t: (nb - 1 - t, 0)
    return _pcall(
        body,
        name="swa_bwd",
        grid=(nb,),
        out_shape=[
            jax.ShapeDtypeStruct((s_len, SWA_W), BF16),
            jax.ShapeDtypeStruct((s_len, SWA_KVW), BF16),
            jax.ShapeDtypeStruct((s_len, SWA_KVW), BF16),
            jax.ShapeDtypeStruct((1, 128), F32),
        ],
        in_specs=_swa_specs(0, 4, 5, True, nb)
        + [pl.BlockSpec(memory_space=pltpu.SMEM), bias_spec, pl.BlockSpec((WINDOW, SWA_W), row)],
        out_specs=[
            pl.BlockSpec((WINDOW, SWA_W), row),
            pl.BlockSpec((WINDOW, SWA_KVW), row),
            pl.BlockSpec((WINDOW, SWA_KVW), row),
            pl.BlockSpec((1, 128), lambda t: (0, 0)),
        ],
        scratch_shapes=[
            pltpu.VMEM((WINDOW, SWA_KVW), F32),
            pltpu.VMEM((WINDOW, SWA_KVW), F32),
            pltpu.VMEM((2 * WINDOW, SWA_KVW), F32),
            pltpu.VMEM((2 * WINDOW, SWA_KVW), F32),
        ],
        compiler_params=_cp("arbitrary"),
    )(qkv, qkv, qkv, qkv, qkv, sinks, _swa_bias(), do)


def _branch_fwd(o, gates, g_blk, w_b, name):
    s_len, wd = o.shape
    d = w_b.shape[1]
    tm = min(1024, s_len)

    def body(o_ref, g_ref, w_ref, y_ref, a_ref):
        g = g_ref[...].astype(F32)
        a = (o_ref[...] * (g * _sigmoid(g))).astype(BF16)
        a_ref[...] = a
        y_ref[...] = jnp.dot(a, w_ref[...], preferred_element_type=F32).astype(BF16)

    return _pcall(
        body,
        name=name,
        grid=(s_len // tm,),
        out_shape=[jax.ShapeDtypeStruct((s_len, d), BF16), jax.ShapeDtypeStruct((s_len, wd), BF16)],
        in_specs=[
            pl.BlockSpec((tm, wd), lambda i: (i, 0)),
            pl.BlockSpec((tm, wd), lambda i: (i, g_blk)),
            pl.BlockSpec((wd, d), lambda i: (0, 0)),
        ],
        out_specs=[pl.BlockSpec((tm, d), lambda i: (i, 0)), pl.BlockSpec((tm, wd), lambda i: (i, 0))],
        compiler_params=_cp("parallel"),
    )(o, gates, w_b)


def _out_stage(gates, mf_blk, y_fox, y_swa, w_out, x, ada, ln_g, ln_b, target):
    s_len, d = x.shape
    tm = min(256, s_len)
    n_steps = s_len // tm

    def body(mf_ref, ms_ref, yf_ref, ys_ref, w_ref, x_ref, gate_ref, lg_ref, lb_ref, t_ref, mg_ref, dza_ref, dsub_ref, red_ref, dmf_ref, dms_ref, dyf_ref, dys_ref):
        i = pl.program_id(0)
        sf, ss = _sigmoid(mf_ref[...].astype(F32)), _sigmoid(ms_ref[...].astype(F32))
        yf, ys = yf_ref[...].astype(F32), ys_ref[...].astype(F32)
        merged = sf * yf + ss * ys
        mb = merged.astype(BF16)
        mg_ref[...] = mb
        sub = jnp.dot(mb, w_ref[...], preferred_element_type=F32)
        gate = gate_ref[...]
        z = ALPHA * x_ref[...] + gate * sub
        mu = jnp.mean(z, axis=-1, keepdims=True)
        zc = z - mu
        var = jnp.mean(zc * zc, axis=-1, keepdims=True)
        rstd = lax.rsqrt(var + LN_EPS)
        zhat = zc * rstd
        err = zhat * lg_ref[...] + lb_ref[...] - t_ref[...]
        dout = err * (1.0 / d)
        dzhat = dout * lg_ref[...]
        dz = rstd * (dzhat - jnp.mean(dzhat, axis=-1, keepdims=True) - zhat * jnp.mean(dzhat * zhat, axis=-1, keepdims=True))
        dza_ref[...] = ALPHA * dz
        dsub = (gate * dz).astype(BF16)
        dsub_ref[...] = dsub
        dm = lax.dot_general(dsub, w_ref[...], _NT, preferred_element_type=F32)
        dmf_ref[...] = (dm * yf * (sf * (1.0 - sf))).astype(BF16)
        dms_ref[...] = (dm * ys * (ss * (1.0 - ss))).astype(BF16)
        dyf_ref[...] = (dm * sf).astype(BF16)
        dys_ref[...] = (dm * ss).astype(BF16)
        part = jnp.concatenate(
            [
                jnp.sum(dz * sub, axis=0, keepdims=True),
                jnp.sum(dout * zhat, axis=0, keepdims=True),
                jnp.sum(dout, axis=0, keepdims=True),
                jnp.sum(err * err, axis=0, keepdims=True),
                jnp.zeros((4, d), F32),
            ],
            axis=0,
        )

        @pl.when(i == 0)
        def _():
            red_ref[...] = part

        @pl.when(i > 0)
        def _():
            red_ref[...] += part

        @pl.when(i == n_steps - 1)
        def _():
            red_ref[4:5, :] = jnp.broadcast_to(jnp.sum(red_ref[3:4, :], axis=1, keepdims=True), (1, d))

    row = pl.BlockSpec((tm, d), lambda i: (i, 0))
    vec = pl.BlockSpec((1, d), lambda i: (0, 0))
    return _pcall(
        body,
        name="out_stage",
        grid=(n_steps,),
        out_shape=[
            jax.ShapeDtypeStruct((s_len, d), BF16),
            jax.ShapeDtypeStruct((s_len, d), F32),
            jax.ShapeDtypeStruct((s_len, d), BF16),
            jax.ShapeDtypeStruct((8, d), F32),
        ]
        + [jax.ShapeDtypeStruct((s_len, d), BF16)] * 4,
        in_specs=[
            pl.BlockSpec((tm, d), lambda i: (i, mf_blk)),
            pl.BlockSpec((tm, d), lambda i: (i, mf_blk + 1)),
            row,
            row,
            pl.BlockSpec((d, d), lambda i: (0, 0), pipeline_mode=pl.Buffered(1)),
            row,
            pl.BlockSpec((1, d), lambda i: (0, 2)),
            vec,
            vec,
            row,
        ],
        out_specs=[row, row, row, pl.BlockSpec((8, d), lambda i: (0, 0))] + [row] * 4,
        compiler_params=_cp("arbitrary"),
    )(gates, gates, y_fox, y_swa, w_out, x, ada, ln_g, ln_b, target)


def _branch_bwd(dy, w_b, o, gates, g_blk, name, n_heads):
    s_len, d = dy.shape
    wd = w_b.shape[0]
    tm = min(1024, s_len)

    def body(dy_ref, w_ref, o_ref, g_ref, do_ref, dg_ref, *rest):
        da = lax.dot_general(dy_ref[...], w_ref[...], _NT, preferred_element_type=F32)
        g = g_ref[...].astype(F32)
        sg = _sigmoid(g)
        do = da * (g * sg)
        do_ref[...] = do.astype(BF16)
        o = o_ref[...]
        dg_ref[...] = (da * o * (sg * (1.0 + g * (1.0 - sg)))).astype(BF16)
        if n_heads:
            prod = do.astype(BF16).astype(F32) * o
            lane = lax.broadcasted_iota(jnp.int32, (1, 128), 1)
            delta = jnp.zeros((tm, 128), F32)
            for h in range(n_heads):
                dh = jnp.sum(prod[:, h * 128 : (h + 1) * 128], axis=1, keepdims=True)
                delta = delta + jnp.where(lane == h, dh, 0.0)
            rest[0][...] = delta

    out_shape = [jax.ShapeDtypeStruct((s_len, wd), BF16), jax.ShapeDtypeStruct((s_len, wd), BF16)]
    out_specs = [pl.BlockSpec((tm, wd), lambda i: (i, 0))] * 2
    if n_heads:
        out_shape.append(jax.ShapeDtypeStruct((s_len, 128), F32))
        out_specs.append(pl.BlockSpec((tm, 128), lambda i: (i, 0)))
    return _pcall(
        body,
        name=name,
        grid=(s_len // tm,),
        out_shape=out_shape,
        in_specs=[
            pl.BlockSpec((tm, d), lambda i: (i, 0)),
            pl.BlockSpec((wd, d), lambda i: (0, 0)),
            pl.BlockSpec((tm, wd), lambda i: (i, 0)),
            pl.BlockSpec((tm, wd), lambda i: (i, g_blk)),
        ],
        out_specs=out_specs,
        compiler_params=_cp("parallel"),
    )(dy, w_b, o, gates)


def _in_bwd(dproj, w_in_t, x, ada, dza, ride):
    s_len, d = x.shape
    k_tot = dproj.shape[1]
    tm, tk, dn = min(512, s_len), k_tot // 4, d // 2
    ni, nk = s_len // tm, k_tot // tk
    n = len(ride)

    def body(dp_ref, w_ref, x_ref, sc_ref, dza_ref, *rest):
        ins, (gx_ref, red_ref), outs = rest[:n], rest[n : n + 2], rest[n + 2 : 2 * n + 2]
        sems, acc_s = rest[2 * n + 2 : 2 * n + 5], rest[2 * n + 5]
        i, nh, kk = pl.program_id(0), pl.program_id(1), pl.program_id(2)

        @pl.when((i == 0) & (nh == 0) & (kk == 0))
        def _():
            _rider_start("exchange", ins, outs, *sems)

        @pl.when((i == ni - 1) & (nh == 1) & (kk == nk - 1))
        def _():
            _rider_wait("exchange", ins, outs, *sems)

        part = jnp.dot(dp_ref[...], w_ref[...], preferred_element_type=F32)
        half = pl.ds(pl.multiple_of(nh * dn, dn), dn)

        @pl.when(kk == 0)
        def _():
            acc_s[:, half] = part

        @pl.when(kk > 0)
        def _():
            acc_s[:, half] += part

        @pl.when((nh == 1) & (kk == nk - 1))
        def _():
            dh = acc_s[...]
            xv = x_ref[...]
            mu = jnp.mean(xv, axis=-1, keepdims=True)
            xc = xv - mu
            var = jnp.mean(xc * xc, axis=-1, keepdims=True)
            rstd = lax.rsqrt(var + LN_EPS)
            xhat = xc * rstd
            dxhat = dh * (1.0 + sc_ref[...])
            dx = rstd * (dxhat - jnp.mean(dxhat, axis=-1, keepdims=True) - xhat * jnp.mean(dxhat * xhat, axis=-1, keepdims=True))
            gx_ref[...] = dza_ref[...] + dx
            part_r = jnp.concatenate(
                [jnp.sum(dh, axis=0, keepdims=True), jnp.sum(dh * xhat, axis=0, keepdims=True), jnp.zeros((6, d), F32)], axis=0
            )

            @pl.when(i == 0)
            def _():
                red_ref[...] = part_r

            @pl.when(i > 0)
            def _():
                red_ref[...] += part_r

    row = pl.BlockSpec((tm, d), lambda i, nh, kk: (i, 0))
    hbm = pl.BlockSpec(memory_space=pltpu.HBM)
    return _pcall(
        body,
        name="in_bwd",
        grid=(ni, 2, nk),
        out_shape=[jax.ShapeDtypeStruct((s_len, d), F32), jax.ShapeDtypeStruct((8, d), F32)]
        + [jax.ShapeDtypeStruct(r.shape, r.dtype) for r in ride],
        in_specs=[
            pl.BlockSpec((tm, tk), lambda i, nh, kk: (i, kk)),
            pl.BlockSpec((tk, dn), lambda i, nh, kk: (kk, nh)),
            row,
            pl.BlockSpec((1, d), lambda i, nh, kk: (0, 1)),
            row,
        ]
        + [hbm] * n,
        out_specs=[row, pl.BlockSpec((8, d), lambda i, nh, kk: (0, 0))] + [hbm] * n,
        scratch_shapes=_rider_scratch(n) + [pltpu.VMEM((tm, d), F32)],
        compiler_params=_cp("arbitrary", "arbitrary", "arbitrary"),
    )(dproj, w_in_t, x, ada, dza, *ride)


def _pad_lanes(v, n):
    return jnp.pad(v, ((0, 0), (0, n - v.shape[1])))


def kernel(x, c, w_ada, b_ada, w_in, b_f, attn_sinks, w_br_fox, w_br_swa, w_out, ln_g, ln_b, loss_target, m_w_ada, m_b_ada, m_w_in, m_b_f, m_attn_sinks, m_w_br_fox, m_w_br_swa, m_w_out, m_ln_g, m_ln_b, v_w_ada, v_b_ada, v_w_in, v_b_f, v_attn_sinks, v_w_br_fox, v_w_br_swa, v_w_out, v_ln_g, v_ln_b):
    x2, tgt = x[0], loss_target[0]
    s_len, d = x2.shape
    me = 4 * lax.axis_index("x") + 2 * lax.axis_index("y") + lax.axis_index("c")
    off_ms = OFF_MF + d
    in_pad = off_ms + d
    c_ada = w_ada.shape[2]
    c_in = w_in.shape[2]
    c_br = w_br_fox.shape[2]

    w_in_full = _all_gather(w_in[0].T.astype(BF16), "ag_w_in", pltpu.HBM).reshape(N_DEV * c_in, d)
    w_in_pad = jnp.concatenate(
        [w_in_full[:REAL_FLOG_END], jnp.zeros((FLOG_PAD - N_FLOG, d), BF16), w_in_full[REAL_FLOG_END:]], axis=0
    )
    k_cut = REAL_FLOG_END // c_in

    c_all = _gather_rows(c, "ag_c")
    b_cols = lax.dynamic_slice(b_ada, (0, me * c_ada), (1, c_ada))
    ada_cols = _ada_fwd(c_all, w_ada[0], b_cols)
    ada_g = _all_gather(ada_cols, "ag_ada", pltpu.VMEM)
    ada = lax.dynamic_index_in_dim(ada_g, me, axis=1, keepdims=False).reshape(1, N_DEV * c_ada)

    h = _ln_mod(x2, ada)
    qkv_fox = _mm_cols(h, w_in_pad, OFF_FQ, 3 * FOX_W, BF16, "proj_fox")
    flog = _mm_cols(h, w_in_pad, OFF_FLOG, 128, F32, "proj_flog")
    qkv_swa = _mm_cols(h, w_in_pad, OFF_SQ, SWA_W + 2 * SWA_KVW, BF16, "proj_swa")
    gates, w_bf, w_bs, w_o = _mm_cols(
        h, w_in_pad, OFF_GF, in_pad - OFF_GF, BF16, "proj_gates",
        ride=(w_br_fox[0].astype(BF16), w_br_swa[0].astype(BF16), w_out[0].astype(BF16)),
    )
    w_bf = w_bf.reshape(N_DEV, FOX_W, c_br).transpose(1, 0, 2).reshape(FOX_W, d)
    w_bs = w_bs.reshape(N_DEV, SWA_W, c_br).transpose(1, 0, 2).reshape(SWA_W, d)
    w_o = w_o.reshape(d, d)
    mf_blk = (OFF_MF - OFF_GF) // d

    flog_t = flog[:, :N_FLOG].T
    bf_col = b_f.reshape(FOX_H, 1)
    cum = _fox_cum(flog_t, bf_col)
    cum_row = cum.reshape(FOX_H, 1, s_len)
    o_fox, lse = _fox_fwd(qkv_fox, cum_row)
    sinks = attn_sinks.reshape(SWA_HQ)
    o_swa = _swa_fwd(qkv_swa, sinks)

    y_fox, a_fox = _branch_fwd(o_fox, gates, 0, w_bf, "branch_fox")
    y_swa, a_swa = _branch_fwd(o_swa, gates, 1, w_bs, "branch_swa")
    merged, dza, dsub, red, dmf, dms, dy_fox, dy_swa = _out_stage(gates, mf_blk, y_fox, y_swa, w_o, x2, ada, ln_g, ln_b, tgt)
    loss = lax.psum(0.5 * red[4, 0] / d, ("x", "y", "c"))

    do_fox, dg_fox, delta = _branch_bwd(dy_fox, w_bf, o_fox, gates, 0, "branch_fox_bwd", FOX_H)
    do_swa, dg_swa = _branch_bwd(dy_swa, w_bs, o_swa, gates, 1, "branch_swa_bwd", 0)
    delta_row = delta[:, :FOX_H].T.reshape(FOX_H, 1, s_len)
    dq_f, dk_f, dv_f, dcol, drow = _fox_bwd(
        qkv_fox, cum.reshape(FOX_H, s_len, 1), lse.reshape(FOX_H, 1, s_len), delta_row, do_fox
    )
    dflog_t, dbf = _fox_gate_bwd(drow.reshape(FOX_H, s_len), dcol.reshape(FOX_H, s_len), flog_t, bf_col)
    dq_s, dk_s, dv_s, dsink = _swa_bwd(qkv_swa, sinks, do_swa)
    dflog = _pad_lanes(dflog_t.T, FLOG_PAD).astype(BF16)
    dproj = jnp.concatenate([dq_f, dk_f, dv_f, dflog, dq_s, dk_s, dv_s, dg_fox, dg_swa, dmf, dms], axis=1)
    g_w_bf = _mm_tn(a_fox, dy_fox, "grad_w_br_fox")
    g_w_bs = _mm_tn(a_swa, dy_swa, "grad_w_br_swa")
    g_w_o = _mm_tn(merged, dsub, "grad_w_out")
    g_w_in, r_bf, r_bs, r_o = _mm_tn(
        dproj, h, "grad_w_in",
        ride=(
            g_w_bf.reshape(FOX_W, N_DEV, c_br).transpose(1, 0, 2),
            g_w_bs.reshape(SWA_W, N_DEV, c_br).transpose(1, 0, 2),
            g_w_o.reshape(N_DEV, d // N_DEV, d),
        ),
    )
    pad = FLOG_PAD - N_FLOG
    g_blocks = jnp.stack(
        [g_w_in[k * c_in : (k + 1) * c_in] for k in range(k_cut)]
        + [jnp.concatenate([g_w_in[k_cut * c_in : REAL_FLOG_END], g_w_in[OFF_SQ : (k_cut + 1) * c_in + pad]], axis=0)]
        + [g_w_in[k * c_in + pad : (k + 1) * c_in + pad] for k in range(k_cut + 1, N_DEV)]
    )

    grad_x, red2, r_in = _in_bwd(dproj, w_in_pad, x2, ada, dza, ride=(g_blocks,))
    out_w_in = _sum_adam_t(r_in, w_in[0].T, m_w_in[0].T, v_w_in[0].T, "adam_w_in")
    out_w_in = [o.T for o in out_w_in]
    out_w_bf = _sum_adam(r_bf, w_br_fox[0], m_w_br_fox[0], v_w_br_fox[0], "adam_w_br_fox")
    out_w_bs = _sum_adam(r_bs, w_br_swa[0], m_w_br_swa[0], v_w_br_swa[0], "adam_w_br_swa")
    out_w_o = _sum_adam(r_o, w_out[0], m_w_out[0], v_w_out[0], "adam_w_out")

    packed = jnp.concatenate([red2[0:1], red2[1:2], red[0:1], _pad_lanes(dbf[:, 0].reshape(1, FOX_H), 128), dsink, red[1:2], red[2:3]], axis=1)
    gathered = _gather_rows(packed, "ag_small")
    pack = lambda a, b, cc, dd, e: jnp.concatenate([a, _pad_lanes(b, 128), _pad_lanes(cc, 128), dd, e], axis=1)
    small = _small_adam(
        gathered,
        pack(b_ada, b_f, attn_sinks, ln_g, ln_b),
        pack(m_b_ada, m_b_f, m_attn_sinks, m_ln_g, m_ln_b),
        pack(v_b_ada, v_b_f, v_attn_sinks, v_ln_g, v_ln_b),
    )
    dada_cols = lax.dynamic_slice(gathered, (0, me * c_ada), (N_DEV, c_ada))
    out_w_ada = _wada_adam(c_all.T, dada_cols, w_ada[0], m_w_ada[0], v_w_ada[0])

    o1, o2, o3 = 3 * d, 3 * d + 128, 3 * d + 256

    def unpack(p):
        return p[:, :o1], p[:, o1 : o1 + FOX_H], p[:, o2 : o2 + SWA_HQ], p[:, o3 : o3 + d], p[:, o3 + d : o3 + 2 * d]

    kinds = []
    for k in range(4):
        b_ada_k, b_f_k, sinks_k, ln_g_k, ln_b_k = unpack(small[k])
        kinds.append(
            [out_w_ada[k][None], b_ada_k, out_w_in[k][None], b_f_k, sinks_k, out_w_bf[k][None], out_w_bs[k][None], out_w_o[k][None], ln_g_k, ln_b_k]
        )
    return (loss, grad_x[None], *kinds[0], *kinds[1], *kinds[2], *kinds[3])
```

```python
import numpy as np
import jax
import jax.numpy as jnp
from jax import lax
from jax.experimental import pallas as pl
from jax.experimental.pallas import tpu as pltpu

F32 = jnp.float32
BF16 = jnp.bfloat16
N_DEV = 8
MESH = pl.DeviceIdType.MESH

FOX_H, FOX_DH, FOX_W = 8, 128, 1024
SWA_HQ, SWA_HKV, SWA_DH, SWA_G = 16, 4, 64, 4
SWA_W, SWA_KVW, WINDOW = 1024, 256, 128
LN_EPS = 1e-5
NEG = -1e30
DEPTH = 1
ALPHA = (2.0 * DEPTH) ** 0.25
FOX_SCALE = FOX_DH ** -0.5
SWA_SCALE = SWA_DH ** -0.5
SLOPES = [2.0 ** (-8.0 * (h + 1.0) / SWA_HQ) for h in range(SWA_HQ)]

ADAM_LR, ADAM_B1, ADAM_B2, ADAM_EPS, ADAM_WD, ADAM_STEP = 0.001, 0.9, 0.999, 1e-08, 0.01, 10

N_FLOG = 8
FLOG_PAD = 512
OFF_FQ, OFF_FK, OFF_FV, OFF_FLOG = 0, 1024, 2048, 3072
OFF_SQ = OFF_FLOG + FLOG_PAD
OFF_SK = OFF_SQ + SWA_W
OFF_SV = OFF_SK + SWA_KVW
OFF_GF = OFF_SV + SWA_KVW
OFF_GS = OFF_GF + FOX_W
OFF_MF = OFF_GS + SWA_W
REAL_FLOG_END = OFF_FLOG + N_FLOG

ATT_BLK = 512
VMEM_LIMIT = 58 * 1024 * 1024


def _pcall(body, **kw):
    return pl.pallas_call(body, **kw)


def _cp(*sem):
    return pltpu.CompilerParams(dimension_semantics=sem, vmem_limit_bytes=VMEM_LIMIT)


def _sigmoid(x):
    return 0.5 * jnp.tanh(0.5 * x) + 0.5


def _all_gather(x, name, space):
    m_per, n = x.shape

    def body(x_ref, out_ref, send_sems, recv_sems, local_sem):
        mx, my, mc = lax.axis_index("x"), lax.axis_index("y"), lax.axis_index("c")
        me, sibling = (mx, my, mc), (mx, my, 1 - mc)
        xn, yn, dg = (1 - mx, my), (mx, 1 - my), (1 - mx, 1 - my)
        south = mc == 0
        src_chip = (jnp.where(south, 1 - mx, mx), jnp.where(south, my, 1 - my))
        dst_chip = (jnp.where(south, mx, 1 - mx), jnp.where(south, 1 - my, my))

        def rows(px, py, pc):
            return out_ref.at[4 * px + 2 * py + pc]

        def copy(k, block, to, src=None):
            return pltpu.make_async_remote_copy(
                src_ref=rows(*block) if src is None else src,
                dst_ref=rows(*block),
                send_sem=send_sems.at[k],
                recv_sem=recv_sems.at[k],
                device_id=to,
                device_id_type=MESH,
            )

        mine = pltpu.make_async_copy(x_ref, rows(*me), local_sem)
        mine.start()
        first = [copy(0, me, sibling, src=x_ref), copy(1, me, (*xn, mc), src=x_ref), copy(2, me, (*yn, mc), src=x_ref)]
        for cp in first:
            cp.start()
        copy(1, (*xn, mc), me).wait_recv()
        copy(2, (*yn, mc), me).wait_recv()
        later = [copy(3, (*src_chip, mc), (*dst_chip, mc)), copy(4, (*xn, mc), sibling), copy(5, (*yn, mc), sibling)]
        for cp in later:
            cp.start()
        copy(3, (*dg, mc), me).wait_recv()
        last = copy(6, (*dg, mc), sibling)
        last.start()
        copy(0, sibling, me).wait_recv()
        for k, chip in ((4, xn), (5, yn), (6, dg)):
            copy(k, (*chip, 1 - mc), me).wait_recv()
        for cp in first + later + [last]:
            cp.wait_send()
        mine.wait()

    return _pcall(
        body,
        name=name,
        out_shape=jax.ShapeDtypeStruct((N_DEV, m_per, n), x.dtype),
        in_specs=[pl.BlockSpec(memory_space=space)],
        out_specs=pl.BlockSpec(memory_space=space),
        scratch_shapes=[pltpu.SemaphoreType.DMA((7,)), pltpu.SemaphoreType.DMA((7,)), pltpu.SemaphoreType.DMA],
    )(x)


def _peer(d, mx, my, mc):
    return (1 - mx if (d >> 2) & 1 else mx, 1 - my if (d >> 1) & 1 else my, 1 - mc if d & 1 else mc)


def _rider_copies(kind, ins, outs, send_sems, recv_sems, local_sems):
    mx, my, mc = lax.axis_index("x"), lax.axis_index("y"), lax.axis_index("c")
    me = 4 * mx + 2 * my + mc
    remote, local = [], []
    for a in range(len(ins)):
        if kind == "gather":
            m_per = ins[a].shape[0]
            mine = outs[a].at[pl.ds(me * m_per, m_per), :]
            local.append(pltpu.make_async_copy(ins[a], mine, local_sems.at[a]))
        else:
            local.append(pltpu.make_async_copy(ins[a].at[me], outs[a].at[0], local_sems.at[a]))
        for d in range(1, N_DEV):
            px, py, pc = _peer(d, mx, my, mc)
            if kind == "gather":
                src, dst = ins[a], mine
            else:
                src, dst = ins[a].at[4 * px + 2 * py + pc], outs[a].at[d]
            remote.append(
                pltpu.make_async_remote_copy(
                    src_ref=src,
                    dst_ref=dst,
                    send_sem=send_sems.at[a * 7 + d - 1],
                    recv_sem=recv_sems.at[a * 7 + d - 1],
                    device_id=(px, py, pc),
                    device_id_type=MESH,
                )
            )
    return remote, local


def _rider_start(*args):
    remote, local = _rider_copies(*args)
    for cp in local + remote:
        cp.start()


def _rider_wait(*args):
    remote, local = _rider_copies(*args)
    for cp in remote:
        cp.wait_recv()
    for cp in remote:
        cp.wait_send()
    for cp in local:
        cp.wait()


def _rider_scratch(n):
    return [pltpu.SemaphoreType.DMA((7 * n,)), pltpu.SemaphoreType.DMA((7 * n,)), pltpu.SemaphoreType.DMA((n,))]


def _gather_rows(v, name):
    n = v.shape[1]
    return _all_gather(jnp.broadcast_to(v, (8, n)), name, pltpu.VMEM)[:, 0, :]


def _adamw(w, g, m, v):
    m = ADAM_B1 * m + (1.0 - ADAM_B1) * g
    v = ADAM_B2 * v + (1.0 - ADAM_B2) * (g * g)
    m_hat = m / (1.0 - ADAM_B1**ADAM_STEP)
    v_hat = v / (1.0 - ADAM_B2**ADAM_STEP)
    delta = -ADAM_LR * (m_hat / (jnp.sqrt(v_hat) + ADAM_EPS) + ADAM_WD * w)
    return delta, m, v


def _sum_adam(recv, w, m, v, name):
    _, r_tot, c = recv.shape
    c_pad = -(-c // 128) * 128
    tr = r_tot
    while 8 * tr * c_pad * 4 > 6 * 1024 * 1024 and tr % 32 == 0:
        tr //= 2

    def body(r_ref, w_ref, m_ref, v_ref, g_ref, d_ref, nm_ref, nv_ref):
        g = r_ref[0].astype(F32)
        for k in range(1, N_DEV):
            g = g + r_ref[k].astype(F32)
        d, nm, nv = _adamw(w_ref[...], g, m_ref[...], v_ref[...])
        g_ref[...] = g
        d_ref[...] = d
        nm_ref[...] = nm
        nv_ref[...] = nv

    blk = pl.BlockSpec((tr, c), lambda i: (i, 0))
    return _pcall(
        body,
        name=name,
        grid=(r_tot // tr,),
        out_shape=[jax.ShapeDtypeStruct((r_tot, c), F32)] * 4,
        in_specs=[pl.BlockSpec((N_DEV, tr, c), lambda i: (0, i, 0)), blk, blk, blk],
        out_specs=[blk] * 4,
        compiler_params=_cp("parallel"),
    )(recv, w, m, v)


def _sum_adam_t(recv, w, m, v, name):
    _, c, r_tot = recv.shape
    tr = min(256, r_tot)

    def body(r_ref, w_ref, m_ref, v_ref, g_ref, d_ref, nm_ref, nv_ref):
        g = r_ref[0].astype(F32)
        for k in range(1, N_DEV):
            g = g + r_ref[k].astype(F32)
        d, nm, nv = _adamw(w_ref[...], g, m_ref[...], v_ref[...])
        g_ref[...] = g
        d_ref[...] = d
        nm_ref[...] = nm
        nv_ref[...] = nv

    blk = pl.BlockSpec((c, tr), lambda i: (0, i))
    return _pcall(
        body,
        name=name,
        grid=(r_tot // tr,),
        out_shape=[jax.ShapeDtypeStruct((c, r_tot), F32)] * 4,
        in_specs=[pl.BlockSpec((N_DEV, c, tr), lambda i: (0, 0, i)), blk, blk, blk],
        out_specs=[blk] * 4,
        compiler_params=_cp("parallel"),
    )(recv, w, m, v)


def _wada_adam(c_t, dada_cols, w, m, v):
    d_model, c = w.shape
    tr = min(256, d_model)

    def body(ct_ref, da_ref, w_ref, m_ref, v_ref, g_ref, d_ref, nm_ref, nv_ref):
        g = jnp.dot(ct_ref[...].astype(BF16), da_ref[...].astype(BF16), preferred_element_type=F32)
        d, nm, nv = _adamw(w_ref[...], g, m_ref[...], v_ref[...])
        g_ref[...] = g
        d_ref[...] = d
        nm_ref[...] = nm
        nv_ref[...] = nv

    blk = pl.BlockSpec((tr, c), lambda i: (i, 0))
    return _pcall(
        body,
        name="wada_adam",
        grid=(d_model // tr,),
        out_shape=[jax.ShapeDtypeStruct((d_model, c), F32)] * 4,
        in_specs=[pl.BlockSpec((tr, N_DEV), lambda i: (i, 0)), pl.BlockSpec((N_DEV, c), lambda i: (0, 0)), blk, blk, blk],
        out_specs=[blk] * 4,
        compiler_params=_cp("parallel"),
    )(c_t, dada_cols, w, m, v)


def _small_adam(gathered, w, m, v):
    p = w.shape[1]

    def body(a_ref, w_ref, m_ref, v_ref, g_ref, d_ref, nm_ref, nv_ref):
        g = a_ref[0:1, :]
        for k in range(1, N_DEV):
            g = g + a_ref[k : k + 1, :]
        d, nm, nv = _adamw(w_ref[...], g, m_ref[...], v_ref[...])
        g_ref[...] = g
        d_ref[...] = d
        nm_ref[...] = nm
        nv_ref[...] = nv

    return _pcall(
        body,
        name="small_adam",
        out_shape=[jax.ShapeDtypeStruct((1, p), F32)] * 4,
    )(gathered, w, m, v)


def _ada_fwd(c_all, w_ada, b_cols):
    c = w_ada.shape[1]

    def body(c_ref, w_ref, b_ref, o_ref):
        o_ref[...] = jnp.dot(c_ref[...].astype(BF16), w_ref[...].astype(BF16), preferred_element_type=F32) + b_ref[...]

    return _pcall(
        body,
        name="ada_fwd",
        out_shape=jax.ShapeDtypeStruct((N_DEV, c), F32),
        compiler_params=_cp(),
    )(c_all, w_ada, b_cols)


def _ln_mod(x, ada):
    s_len, d = x.shape
    tm = min(512, s_len)

    def body(x_ref, sh_ref, sc_ref, h_ref):
        xv = x_ref[...]
        mu = jnp.mean(xv, axis=-1, keepdims=True)
        xc = xv - mu
        var = jnp.mean(xc * xc, axis=-1, keepdims=True)
        xhat = xc * lax.rsqrt(var + LN_EPS)
        h_ref[...] = (xhat * (1.0 + sc_ref[...]) + sh_ref[...]).astype(BF16)

    return _pcall(
        body,
        name="ln_mod",
        grid=(s_len // tm,),
        out_shape=jax.ShapeDtypeStruct((s_len, d), BF16),
        in_specs=[
            pl.BlockSpec((tm, d), lambda i: (i, 0)),
            pl.BlockSpec((1, d), lambda i: (0, 0)),
            pl.BlockSpec((1, d), lambda i: (0, 1)),
        ],
        out_specs=pl.BlockSpec((tm, d), lambda i: (i, 0)),
        compiler_params=_cp("parallel"),
    )(x, ada, ada)


def _mm_cols(a, b, col_off, n_cols, out_dtype, name, ride=()):
    m, k = a.shape
    tm = min(1024, m)
    tn = next(t for t in (1024, 512, 128) if n_cols % t == 0 and col_off % t == 0)
    off = col_off // tn
    ni, nj = m // tm, n_cols // tn
    n = len(ride)

    def body(a_ref, b_ref, *rest):
        ins, o_ref, outs, sems = rest[:n], rest[n], rest[n + 1 : 2 * n + 1], rest[2 * n + 1 :]
        i, j = pl.program_id(0), pl.program_id(1)
        if n:

            @pl.when((i == 0) & (j == 0))
            def _():
                _rider_start("gather", ins, outs, *sems)

        o_ref[...] = lax.dot_general(a_ref[...], b_ref[...], _NT, preferred_element_type=F32).astype(out_dtype)
        if n:

            @pl.when((i == ni - 1) & (j == nj - 1))
            def _():
                _rider_wait("gather", ins, outs, *sems)

    hbm = pl.BlockSpec(memory_space=pltpu.HBM)
    out = _pcall(
        body,
        name=name,
        grid=(ni, nj),
        out_shape=[jax.ShapeDtypeStruct((m, n_cols), out_dtype)]
        + [jax.ShapeDtypeStruct((N_DEV * r.shape[0], r.shape[1]), r.dtype) for r in ride],
        in_specs=[pl.BlockSpec((tm, k), lambda i, j: (i, 0)), pl.BlockSpec((tn, k), lambda i, j: (off + j, 0))] + [hbm] * n,
        out_specs=[pl.BlockSpec((tm, tn), lambda i, j: (i, j))] + [hbm] * n,
        scratch_shapes=_rider_scratch(n) if n else [],
        compiler_params=_cp("arbitrary", "arbitrary") if n else _cp("parallel", "parallel"),
    )(a, b, *ride)
    return out if n else out[0]


def _mm_tn(a, b, name, ride=()):
    s_len, m = a.shape
    n = b.shape[1]
    tm, tn, ts = min(1024, m), min(2048, n), min(2048, s_len)
    ni, nj, ns = m // tm, n // tn, s_len // ts
    nr = len(ride)

    def body(a_ref, b_ref, *rest):
        ins, o_ref, outs = rest[:nr], rest[nr], rest[nr + 1 : 2 * nr + 1]
        sems, acc_s = rest[2 * nr + 1 : -1], rest[-1]
        i, j, kk = pl.program_id(0), pl.program_id(1), pl.program_id(2)
        if nr:

            @pl.when((i == 0) & (j == 0) & (kk == 0))
            def _():
                _rider_start("exchange", ins, outs, *sems)

            @pl.when((i == ni - 1) & (j == nj - 1) & (kk == ns - 1))
            def _():
                _rider_wait("exchange", ins, outs, *sems)

        part = lax.dot_general(a_ref[...], b_ref[...], _TN, preferred_element_type=F32)

        @pl.when(kk == 0)
        def _():
            acc_s[...] = part

        @pl.when(kk > 0)
        def _():
            acc_s[...] += part

        @pl.when(kk == ns - 1)
        def _():
            o_ref[...] = acc_s[...].astype(BF16)

    hbm = pl.BlockSpec(memory_space=pltpu.HBM)
    out = _pcall(
        body,
        name=name,
        grid=(ni, nj, ns),
        out_shape=[jax.ShapeDtypeStruct((m, n), BF16)] + [jax.ShapeDtypeStruct(r.shape, r.dtype) for r in ride],
        in_specs=[pl.BlockSpec((ts, tm), lambda i, j, kk: (kk, i)), pl.BlockSpec((ts, tn), lambda i, j, kk: (kk, j))] + [hbm] * nr,
        out_specs=[pl.BlockSpec((tm, tn), lambda i, j, kk: (i, j))] + [hbm] * nr,
        scratch_shapes=(_rider_scratch(nr) if nr else []) + [pltpu.VMEM((tm, tn), F32)],
        compiler_params=_cp("arbitrary", "arbitrary", "arbitrary") if nr else _cp("parallel", "parallel", "arbitrary"),
    )(a, b, *ride)
    return out if nr else out[0]


def _split3(a):
    hi = a.astype(BF16)
    r1 = a - hi.astype(F32)
    mid = r1.astype(BF16)
    lo = (r1 - mid.astype(F32)).astype(BF16)
    return hi, mid, lo


def _dot_ones(a, tri):
    return sum(jnp.dot(t, tri, preferred_element_type=F32) for t in _split3(a))


def _log_sigmoid(x):
    return jnp.minimum(x, 0.0) - jnp.log1p(jnp.exp(-jnp.abs(x)))


def _fox_cum(flog_t, bf_col):
    s_len = flog_t.shape[1]

    def body(fl_ref, bf_ref, cum_ref):
        r = lax.broadcasted_iota(jnp.int32, (128, 128), 0)
        c = lax.broadcasted_iota(jnp.int32, (128, 128), 1)
        upper = (r <= c).astype(BF16)

        def step(t, carry):
            sl = pl.ds(pl.multiple_of(t * 128, 128), 128)
            lf = _log_sigmoid(fl_ref[:, sl] + bf_ref[...])
            cs = _dot_ones(lf, upper) + carry
            cum_ref[:, sl] = cs
            return cs[:, 127:128]

        lax.fori_loop(0, s_len // 128, step, jnp.zeros((FOX_H, 1), F32))

    return _pcall(body, name="fox_cum", out_shape=jax.ShapeDtypeStruct((FOX_H, s_len), F32))(flog_t, bf_col)


def _fox_gate_bwd(drow, dcol, flog_t, bf_col):
    s_len = flog_t.shape[1]
    n = s_len // 128

    def body(dr_ref, dc_ref, fl_ref, bf_ref, dfl_ref, dbf_ref):
        r = lax.broadcasted_iota(jnp.int32, (128, 128), 0)
        c = lax.broadcasted_iota(jnp.int32, (128, 128), 1)
        lower = (r >= c).astype(BF16)

        def step(t, carry):
            run, tot = carry
            sl = pl.ds(pl.multiple_of((n - 1 - t) * 128, 128), 128)
            rc = _dot_ones(dr_ref[:, sl] - dc_ref[:, sl], lower) + run
            dfl = rc * _sigmoid(-(fl_ref[:, sl] + bf_ref[...]))
            dfl_ref[:, sl] = dfl
            return rc[:, 0:1], tot + jnp.sum(dfl, axis=1, keepdims=True)

        zero = jnp.zeros((FOX_H, 1), F32)
        _, tot = lax.fori_loop(0, n, step, (zero, zero))
        dbf_ref[...] = jnp.broadcast_to(tot, (FOX_H, 128))

    return _pcall(
        body,
        name="fox_gate_bwd",
        out_shape=[jax.ShapeDtypeStruct((FOX_H, s_len), F32), jax.ShapeDtypeStruct((FOX_H, 128), F32)],
    )(drow, dcol, flog_t, bf_col)


def _diag_mask(blk, transposed=False):
    r = lax.broadcasted_iota(jnp.int32, (blk, blk), 0)
    c = lax.broadcasted_iota(jnp.int32, (blk, blk), 1)
    return c >= r if transposed else r >= c


_NT = (((1,), (1,)), ((), ()))
_TN = (((0,), (0,)), ((), ()))


def _fox_fwd(qkv, cum_row):
    s_len = qkv.shape[0]
    blk = min(ATT_BLK, s_len)
    nb = s_len // blk
    log2e = 1.4426950408889634

    def body(q_ref, k_ref, v_ref, c_ref, o_ref, lse_ref, mx_s, acc_s, u_s, v1_s):
        i = pl.program_id(1)

        @pl.when(i == 0)
        def _():
            v1_s[:, :FOX_DH] = v_ref[...]
            v1_s[:, FOX_DH:] = (lax.broadcasted_iota(jnp.int32, (s_len, FOX_DH), 1) == 0).astype(BF16)

        def key_cols(j, n):
            return pl.ds(pl.multiple_of(j * blk, blk), n * blk)

        def walk(tile):
            def four_pairs(t, carry):
                for u in range(4):
                    tile(8 * t + 2 * u, 2, False)
                return carry

            lax.fori_loop(0, i // 8, four_pairs, 0)

            @pl.when((i // 4) % 2 == 1)
            def _():
                tile(8 * (i // 8), 2, False)
                tile(8 * (i // 8) + 2, 2, False)

            @pl.when((i // 2) % 2 == 1)
            def _():
                tile(4 * (i // 4), 2, False)

            @pl.when(i % 2 == 1)
            def _():
                tile(i - 1, 1, False)

            tile(i, 1, True)

        def lane_max(j, n, masked):
            cols = key_cols(j, n)
            u = lax.dot_general(q_ref[...], k_ref[cols, :], _NT, preferred_element_type=F32) * (FOX_SCALE * log2e) - c_ref[:, cols] * log2e
            if masked:
                u = jnp.where(_diag_mask(blk), u, NEG)
            u_s[:, cols] = u
            part = u[:, 0:128]
            for t in range(1, n * blk // 128):
                part = jnp.maximum(part, u[:, t * 128 : (t + 1) * 128])
            mx_s[...] = jnp.maximum(mx_s[...], part)

        mx_s[...] = jnp.full(mx_s.shape, NEG, F32)
        walk(lane_max)
        m = jnp.max(mx_s[...], axis=1, keepdims=True)

        def weigh(j, n, masked):
            cols = key_cols(j, n)
            p = jnp.exp2(u_s[:, cols] - m)
            acc_s[...] += jnp.dot(p.astype(BF16), v1_s[cols, :], preferred_element_type=F32)

        acc_s[...] = jnp.zeros(acc_s.shape, F32)
        walk(weigh)
        l = acc_s[:, FOX_DH : FOX_DH + 1]
        o_ref[...] = acc_s[:, :FOX_DH] / l
        lse_ref[...] = m * (1.0 / log2e) + jnp.log(l)

    return _pcall(
        body,
        name="fox_fwd",
        grid=(FOX_H, nb),
        out_shape=[jax.ShapeDtypeStruct((s_len, FOX_W), F32), jax.ShapeDtypeStruct((FOX_H, s_len, 1), F32)],
        in_specs=[
            pl.BlockSpec((blk, FOX_DH), lambda h, i: (i, h)),
            pl.BlockSpec((s_len, FOX_DH), lambda h, i: (0, FOX_H + h)),
            pl.BlockSpec((s_len, FOX_DH), lambda h, i: (0, 2 * FOX_H + h)),
            pl.BlockSpec((None, 1, s_len), lambda h, i: (h, 0, 0)),
        ],
        out_specs=[
            pl.BlockSpec((blk, FOX_DH), lambda h, i: (i, h)),
            pl.BlockSpec((None, blk, 1), lambda h, i: (h, i, 0)),
        ],
        scratch_shapes=[
            pltpu.VMEM((blk, 128), F32),
            pltpu.VMEM((blk, 2 * FOX_DH), F32),
            pltpu.VMEM((blk, s_len), F32),
            pltpu.VMEM((s_len, 2 * FOX_DH), BF16),
        ],
        compiler_params=_cp("arbitrary", "arbitrary"),
    )(qkv, qkv, qkv, cum_row)


def _fox_bwd(qkv, cum_col, lse_row, delta_row, do):
    s_len = qkv.shape[0]
    blk = min(ATT_BLK, s_len)
    nb = s_len // blk

    def body(q_ref, k_ref, v_ref, c_ref, lse_ref, dl_ref, do_ref, dq_ref, dk_ref, dv_ref, dc_ref, dr_ref, dk_s, dv_s, dc_s, cb_s, dq_s):
        j = pl.program_id(1)

        @pl.when(j == 0)
        def _():
            dq_s[...] = jnp.zeros(dq_s.shape, F32)
            dr_ref[...] = jnp.zeros(dr_ref.shape, F32)

        dk_s[...] = jnp.zeros(dk_s.shape, F32)
        dv_s[...] = jnp.zeros(dv_s.shape, F32)
        dc_s[...] = jnp.zeros(dc_s.shape, F32)
        cb_s[...] = jnp.broadcast_to(c_ref[...], cb_s.shape)

        def tile(i, n, diag):
            rows = pl.ds(pl.multiple_of(i * blk, blk), n * blk)
            q, dob = q_ref[rows, :], do_ref[rows, :]
            k, v = k_ref[...], v_ref[...]
            s_t = lax.dot_general(k, q, _NT, preferred_element_type=F32) * FOX_SCALE - cb_s[:, : n * blk]
            p_t = jnp.exp(s_t - lse_ref[:, rows])
            if diag:
                p_t = jnp.where(_diag_mask(blk, transposed=True), p_t, 0.0)
            dp_t = lax.dot_general(v, dob, _NT, preferred_element_type=F32)
            ds_t = p_t * (dp_t - dl_ref[:, rows])
            dsb = ds_t.astype(BF16)
            dv_s[...] += jnp.dot(p_t.astype(BF16), dob, preferred_element_type=F32)
            dk_s[...] += jnp.dot(dsb, q, preferred_element_type=F32)
            dq_c = lax.dot_general(dsb, k, _TN, preferred_element_type=F32)
            part = ds_t[:, 0:128]
            for t in range(1, n * blk // 128):
                part = part + ds_t[:, t * 128 : (t + 1) * 128]
            dc_s[...] += part
            dr_ref[:, rows] += jnp.sum(ds_t, axis=0, keepdims=True)
            if diag:
                dq_s[rows, :] = (dq_s[rows, :] + dq_c) * FOX_SCALE
            else:
                dq_s[rows, :] += dq_c

        tile(j, 1, True)
        below = nb - 1 - j
        b0, b1, b2 = below % 2, (below // 2) % 2, (below // 4) % 2

        @pl.when(b0 == 1)
        def _():
            tile(j + 1, 1, False)

        @pl.when(b1 == 1)
        def _():
            tile(j + 1 + b0, 2, False)

        @pl.when(b2 == 1)
        def _():
            tile(j + 1 + b0 + 2 * b1, 2, False)
            tile(j + 3 + b0 + 2 * b1, 2, False)

        first = j + 1 + b0 + 2 * b1 + 4 * b2

        def four_pairs(t, carry):
            for u in range(4):
                tile(first + 8 * t + 2 * u, 2, False)
            return carry

        lax.fori_loop(0, below // 8, four_pairs, 0)
        dk_ref[...] = (dk_s[...] * FOX_SCALE).astype(BF16)
        dv_ref[...] = dv_s[...].astype(BF16)
        dc_ref[...] = jnp.sum(dc_s[...], axis=1, keepdims=True)

        @pl.when(j == nb - 1)
        def _():
            dq_ref[...] = dq_s[...].astype(BF16)

    head = lambda h, j: (0, h)
    row = pl.BlockSpec((None, 1, s_len), lambda h, j: (h, 0, 0))
    return _pcall(
        body,
        name="fox_bwd",
        grid=(FOX_H, nb),
        out_shape=[
            jax.ShapeDtypeStruct((s_len, FOX_W), BF16),
            jax.ShapeDtypeStruct((s_len, FOX_W), BF16),
            jax.ShapeDtypeStruct((s_len, FOX_W), BF16),
            jax.ShapeDtypeStruct((FOX_H, s_len, 1), F32),
            jax.ShapeDtypeStruct((FOX_H, 1, s_len), F32),
        ],
        in_specs=[
            pl.BlockSpec((s_len, FOX_DH), head),
            pl.BlockSpec((blk, FOX_DH), lambda h, j: (j, FOX_H + h)),
            pl.BlockSpec((blk, FOX_DH), lambda h, j: (j, 2 * FOX_H + h)),
            pl.BlockSpec((None, blk, 1), lambda h, j: (h, j, 0)),
            row,
            row,
            pl.BlockSpec((s_len, FOX_DH), head),
        ],
        out_specs=[
            pl.BlockSpec((s_len, FOX_DH), head),
            pl.BlockSpec((blk, FOX_DH), lambda h, j: (j, h)),
            pl.BlockSpec((blk, FOX_DH), lambda h, j: (j, h)),
            pl.BlockSpec((None, blk, 1), lambda h, j: (h, j, 0)),
            row,
        ],
        scratch_shapes=[
            pltpu.VMEM((blk, FOX_DH), F32),
            pltpu.VMEM((blk, FOX_DH), F32),
            pltpu.VMEM((blk, 128), F32),
            pltpu.VMEM((blk, 2 * blk), F32),
            pltpu.VMEM((s_len, FOX_DH), F32),
        ],
        compiler_params=_cp("parallel", "arbitrary"),
    )(qkv, qkv, qkv, cum_col, lse_row, delta_row, do)


def _swa_bias():
    cols = SWA_G * WINDOW
    k = np.arange(2 * WINDOW)[:, None]
    q = np.arange(cols)[None, :]
    dist = (q % WINDOW) - k + WINDOW
    valid = (dist >= 0) & (dist < WINDOW)
    out = np.empty((2, SWA_HKV, 2 * WINDOW, cols), np.float32)
    for g in range(SWA_HKV):
        slope = np.array([SLOPES[g * SWA_G + t] for t in range(SWA_G)], np.float32)[q // WINDOW]
        bias = -(slope * dist.astype(np.float32))
        out[0, g] = np.where(valid & (k >= WINDOW), bias, np.float32(NEG))
        out[1, g] = np.where(valid, bias, np.float32(NEG))
    return jnp.asarray(out)


def _swa_group(i, q_ref, kk, sinks_ref, bias_ref, g):
    cols = SWA_G * WINDOW
    head = lax.broadcasted_iota(jnp.int32, (1, cols), 1) // WINDOW
    sink = jnp.zeros((1, cols), F32)
    for t in range(SWA_G):
        sink = jnp.where(head == t, sinks_ref[g * SWA_G + t], sink)
    q = jnp.concatenate([q_ref[:, (g * SWA_G + t) * SWA_DH : (g * SWA_G + t + 1) * SWA_DH] for t in range(SWA_G)], axis=0)
    k = kk[:, g * SWA_DH : (g + 1) * SWA_DH]
    s = lax.dot_general(k, q, _NT, preferred_element_type=F32) * SWA_SCALE + bias_ref[jnp.minimum(i, 1), g]
    m = jnp.maximum(jnp.max(s, axis=0, keepdims=True), sink)
    e = jnp.exp(s - m)
    e_sink = jnp.exp(sink - m)
    inv = 1.0 / (jnp.sum(e, axis=0, keepdims=True) + e_sink)
    return q, k, e * inv, e_sink * inv


def _swa_fwd(qkv, sinks):
    s_len = qkv.shape[0]
    nb = s_len // WINDOW
    bias_spec = pl.BlockSpec((2, SWA_HKV, 2 * WINDOW, SWA_G * WINDOW), lambda t: (0, 0, 0, 0))

    def body(q_ref, kp_ref, kc_ref, vp_ref, vc_ref, sinks_ref, bias_ref, o_ref):
        step = pl.program_id(0)
        kc, vc = kc_ref[...], vc_ref[...]
        for b in range(2):
            rows = pl.ds(b * WINDOW, WINDOW)
            if b == 0:
                kk = jnp.concatenate([kp_ref[...], kc[:WINDOW]], axis=0)
                vv = jnp.concatenate([vp_ref[...], vc[:WINDOW]], axis=0)
            else:
                kk, vv = kc, vc
            for g in range(SWA_HKV):
                _, _, p, _ = _swa_group(2 * step + b, q_ref.at[rows, :], kk, sinks_ref, bias_ref, g)
                o = lax.dot_general(p.astype(BF16), vv[:, g * SWA_DH : (g + 1) * SWA_DH], _TN, preferred_element_type=F32)
                for t in range(SWA_G):
                    h = g * SWA_G + t
                    o_ref[rows, h * SWA_DH : (h + 1) * SWA_DH] = o[t * WINDOW : (t + 1) * WINDOW, :]

    before = lambda t: jnp.maximum(2 * t - 1, 0)
    return _pcall(
        body,
        name="swa_fwd",
        grid=(nb // 2,),
        out_shape=jax.ShapeDtypeStruct((s_len, SWA_W), F32),
        in_specs=[
            pl.BlockSpec((2 * WINDOW, SWA_W), lambda t: (t, 0)),
            pl.BlockSpec((WINDOW, SWA_KVW), lambda t: (before(t), 4)),
            pl.BlockSpec((2 * WINDOW, SWA_KVW), lambda t: (t, 4)),
            pl.BlockSpec((WINDOW, SWA_KVW), lambda t: (before(t), 5)),
            pl.BlockSpec((2 * WINDOW, SWA_KVW), lambda t: (t, 5)),
            pl.BlockSpec(memory_space=pltpu.SMEM),
            bias_spec,
        ],
        out_specs=pl.BlockSpec((2 * WINDOW, SWA_W), lambda t: (t, 0)),
        compiler_params=_cp("parallel"),
    )(qkv, qkv, qkv, qkv, qkv, sinks, _swa_bias())


def _swa_bwd(qkv, sinks, do):
    s_len = qkv.shape[0]
    nb = s_len // WINDOW
    bias_spec = pl.BlockSpec((2, SWA_HKV, 2 * WINDOW, SWA_G * WINDOW), lambda t: (0, 0, 0, 0))

    def body(q_ref, kp_ref, kc_ref, vp_ref, vc_ref, sinks_ref, bias_ref, do_ref, dq_ref, dk_ref, dv_ref, dsink_ref, ck_s, cv_s, dkk_s, dvv_s, pk_s, pv_s):
        step = pl.program_id(0)
        i_top = nb - 1 - 2 * step

        @pl.when(step == 0)
        def _():
            ck_s[...] = jnp.zeros(ck_s.shape, F32)
            cv_s[...] = jnp.zeros(cv_s.shape, F32)
            dsink_ref[...] = jnp.zeros(dsink_ref.shape, F32)

        kc, vc = kc_ref[...], vc_ref[...]
        lane = lax.broadcasted_iota(jnp.int32, (1, 128), 1)

        def block(i, rows, kk, vv, dsink):
            q_rows, do_rows = q_ref.at[rows, :], do_ref.at[rows, :]
            for g in range(SWA_HKV):
                cols = slice(g * SWA_DH, (g + 1) * SWA_DH)
                q, k, p, p_sink = _swa_group(i, q_rows, kk, sinks_ref, bias_ref, g)
                dob = jnp.concatenate([do_rows[:, (g * SWA_G + t) * SWA_DH : (g * SWA_G + t + 1) * SWA_DH] for t in range(SWA_G)], axis=0)
                dp = lax.dot_general(vv[:, cols], dob, _NT, preferred_element_type=F32)
                delta = jnp.sum(p * dp, axis=0, keepdims=True)
                dsb = (p * (dp - delta)).astype(BF16)
                dq = (lax.dot_general(dsb, k, _TN, preferred_element_type=F32) * SWA_SCALE).astype(BF16)
                ps_d = p_sink * delta
                for t in range(SWA_G):
                    h = g * SWA_G + t
                    dq_ref[rows, h * SWA_DH : (h + 1) * SWA_DH] = dq[t * WINDOW : (t + 1) * WINDOW, :]
                    dsink = dsink + jnp.where(lane == h, -jnp.sum(ps_d[:, t * WINDOW : (t + 1) * WINDOW], axis=1, keepdims=True), 0.0)
                dkk_s[:, cols] = jnp.dot(dsb, q, preferred_element_type=F32) * SWA_SCALE
                dvv_s[:, cols] = jnp.dot(p.astype(BF16), dob, preferred_element_type=F32)
            return dsink

        bottom, top = pl.ds(0, WINDOW), pl.ds(WINDOW, WINDOW)
        dsink = block(i_top, top, kc, vc, jnp.zeros((1, 128), F32))
        dk_ref[top, :] = (dkk_s[WINDOW:, :] + ck_s[...]).astype(BF16)
        dv_ref[top, :] = (dvv_s[WINDOW:, :] + cv_s[...]).astype(BF16)
        pk_s[...] = dkk_s[:WINDOW, :]
        pv_s[...] = dvv_s[:WINDOW, :]
        kk = jnp.concatenate([kp_ref[...], kc[:WINDOW]], axis=0)
        vv = jnp.concatenate([vp_ref[...], vc[:WINDOW]], axis=0)
        dsink = block(i_top - 1, bottom, kk, vv, dsink)
        dk_ref[bottom, :] = (dkk_s[WINDOW:, :] + pk_s[...]).astype(BF16)
        dv_ref[bottom, :] = (dvv_s[WINDOW:, :] + pv_s[...]).astype(BF16)
        ck_s[...] = dkk_s[:WINDOW, :]
        cv_s[...] = dvv_s[:WINDOW, :]
        dsink_ref[...] += dsink

    pair = lambda t: (nb // 2 - 1 - t, 0)
    before = lambda t: jnp.maximum(nb - 3 - 2 * t, 0)
    return _pcall(
        body,
        name="swa_bwd",
        grid=(nb // 2,),
        out_shape=[
            jax.ShapeDtypeStruct((s_len, SWA_W), BF16),
            jax.ShapeDtypeStruct((s_len, SWA_KVW), BF16),
            jax.ShapeDtypeStruct((s_len, SWA_KVW), BF16),
            jax.ShapeDtypeStruct((1, 128), F32),
        ],
        in_specs=[
            pl.BlockSpec((2 * WINDOW, SWA_W), pair),
            pl.BlockSpec((WINDOW, SWA_KVW), lambda t: (before(t), 4)),
            pl.BlockSpec((2 * WINDOW, SWA_KVW), lambda t: (nb // 2 - 1 - t, 4)),
            pl.BlockSpec((WINDOW, SWA_KVW), lambda t: (before(t), 5)),
            pl.BlockSpec((2 * WINDOW, SWA_KVW), lambda t: (nb // 2 - 1 - t, 5)),
            pl.BlockSpec(memory_space=pltpu.SMEM),
            bias_spec,
            pl.BlockSpec((2 * WINDOW, SWA_W), pair),
        ],
        out_specs=[
            pl.BlockSpec((2 * WINDOW, SWA_W), pair),
            pl.BlockSpec((2 * WINDOW, SWA_KVW), pair),
            pl.BlockSpec((2 * WINDOW, SWA_KVW), pair),
            pl.BlockSpec((1, 128), lambda t: (0, 0)),
        ],
        scratch_shapes=[
            pltpu.VMEM((WINDOW, SWA_KVW), F32),
            pltpu.VMEM((WINDOW, SWA_KVW), F32),
            pltpu.VMEM((2 * WINDOW, SWA_KVW), F32),
            pltpu.VMEM((2 * WINDOW, SWA_KVW), F32),
            pltpu.VMEM((WINDOW, SWA_KVW), F32),
            pltpu.VMEM((WINDOW, SWA_KVW), F32),
        ],
        compiler_params=_cp("arbitrary"),
    )(qkv, qkv, qkv, qkv, qkv, sinks, _swa_bias(), do)


def _branch_fwd(o, gates, g_blk, w_b, name):
    s_len, wd = o.shape
    d = w_b.shape[1]
    tm = min(1024, s_len)

    def body(o_ref, g_ref, w_ref, y_ref, a_ref):
        g = g_ref[...].astype(F32)
        a = (o_ref[...] * (g * _sigmoid(g))).astype(BF16)
        a_ref[...] = a
        y_ref[...] = jnp.dot(a, w_ref[...], preferred_element_type=F32).astype(BF16)

    return _pcall(
        body,
        name=name,
        grid=(s_len // tm,),
        out_shape=[jax.ShapeDtypeStruct((s_len, d), BF16), jax.ShapeDtypeStruct((s_len, wd), BF16)],
        in_specs=[
            pl.BlockSpec((tm, wd), lambda i: (i, 0)),
            pl.BlockSpec((tm, wd), lambda i: (i, g_blk)),
            pl.BlockSpec((wd, d), lambda i: (0, 0)),
        ],
        out_specs=[pl.BlockSpec((tm, d), lambda i: (i, 0)), pl.BlockSpec((tm, wd), lambda i: (i, 0))],
        compiler_params=_cp("parallel"),
    )(o, gates, w_b)


def _out_stage(gates, mf_blk, y_fox, y_swa, w_out, x, ada, ln_g, ln_b, target):
    s_len, d = x.shape
    tm = min(256, s_len)
    n_steps = s_len // tm

    def body(mf_ref, ms_ref, yf_ref, ys_ref, w_ref, x_ref, gate_ref, lg_ref, lb_ref, t_ref, mg_ref, dza_ref, dsub_ref, red_ref, dmf_ref, dms_ref, dyf_ref, dys_ref):
        i = pl.program_id(0)
        sf, ss = _sigmoid(mf_ref[...].astype(F32)), _sigmoid(ms_ref[...].astype(F32))
        yf, ys = yf_ref[...].astype(F32), ys_ref[...].astype(F32)
        merged = sf * yf + ss * ys
        mb = merged.astype(BF16)
        mg_ref[...] = mb
        sub = jnp.dot(mb, w_ref[...], preferred_element_type=F32)
        gate = gate_ref[...]
        z = ALPHA * x_ref[...] + gate * sub
        mu = jnp.mean(z, axis=-1, keepdims=True)
        zc = z - mu
        var = jnp.mean(zc * zc, axis=-1, keepdims=True)
        rstd = lax.rsqrt(var + LN_EPS)
        zhat = zc * rstd
        err = zhat * lg_ref[...] + lb_ref[...] - t_ref[...]
        dout = err * (1.0 / d)
        dzhat = dout * lg_ref[...]
        dz = rstd * (dzhat - jnp.mean(dzhat, axis=-1, keepdims=True) - zhat * jnp.mean(dzhat * zhat, axis=-1, keepdims=True))
        dza_ref[...] = ALPHA * dz
        dsub = (gate * dz).astype(BF16)
        dsub_ref[...] = dsub
        dm = lax.dot_general(dsub, w_ref[...], _NT, preferred_element_type=F32)
        dmf_ref[...] = (dm * yf * (sf * (1.0 - sf))).astype(BF16)
        dms_ref[...] = (dm * ys * (ss * (1.0 - ss))).astype(BF16)
        dyf_ref[...] = (dm * sf).astype(BF16)
        dys_ref[...] = (dm * ss).astype(BF16)
        part = jnp.concatenate(
            [
                jnp.sum(dz * sub, axis=0, keepdims=True),
                jnp.sum(dout * zhat, axis=0, keepdims=True),
                jnp.sum(dout, axis=0, keepdims=True),
                jnp.sum(err * err, axis=0, keepdims=True),
                jnp.zeros((4, d), F32),
            ],
            axis=0,
        )

        @pl.when(i == 0)
        def _():
            red_ref[...] = part

        @pl.when(i > 0)
        def _():
            red_ref[...] += part

        @pl.when(i == n_steps - 1)
        def _():
            red_ref[4:5, :] = jnp.broadcast_to(jnp.sum(red_ref[3:4, :], axis=1, keepdims=True), (1, d))

    row = pl.BlockSpec((tm, d), lambda i: (i, 0))
    vec = pl.BlockSpec((1, d), lambda i: (0, 0))
    return _pcall(
        body,
        name="out_stage",
        grid=(n_steps,),
        out_shape=[
            jax.ShapeDtypeStruct((s_len, d), BF16),
            jax.ShapeDtypeStruct((s_len, d), F32),
            jax.ShapeDtypeStruct((s_len, d), BF16),
            jax.ShapeDtypeStruct((8, d), F32),
        ]
        + [jax.ShapeDtypeStruct((s_len, d), BF16)] * 4,
        in_specs=[
            pl.BlockSpec((tm, d), lambda i: (i, mf_blk)),
            pl.BlockSpec((tm, d), lambda i: (i, mf_blk + 1)),
            row,
            row,
            pl.BlockSpec((d, d), lambda i: (0, 0), pipeline_mode=pl.Buffered(1)),
            row,
            pl.BlockSpec((1, d), lambda i: (0, 2)),
            vec,
            vec,
            row,
        ],
        out_specs=[row, row, row, pl.BlockSpec((8, d), lambda i: (0, 0))] + [row] * 4,
        compiler_params=_cp("arbitrary"),
    )(gates, gates, y_fox, y_swa, w_out, x, ada, ln_g, ln_b, target)


def _branch_bwd(dy, w_b, o, gates, g_blk, name, n_heads):
    s_len, d = dy.shape
    wd = w_b.shape[0]
    tm = min(1024, s_len)

    def body(dy_ref, w_ref, o_ref, g_ref, do_ref, dg_ref, *rest):
        da = lax.dot_general(dy_ref[...], w_ref[...], _NT, preferred_element_type=F32)
        g = g_ref[...].astype(F32)
        sg = _sigmoid(g)
        do = da * (g * sg)
        do_ref[...] = do.astype(BF16)
        o = o_ref[...]
        dg_ref[...] = (da * o * (sg * (1.0 + g * (1.0 - sg)))).astype(BF16)
        if n_heads:
            prod = do.astype(BF16).astype(F32) * o
            lane = lax.broadcasted_iota(jnp.int32, (1, 128), 1)
            delta = jnp.zeros((tm, 128), F32)
            for h in range(n_heads):
                dh = jnp.sum(prod[:, h * 128 : (h + 1) * 128], axis=1, keepdims=True)
                delta = delta + jnp.where(lane == h, dh, 0.0)
            rest[0][...] = delta

    out_shape = [jax.ShapeDtypeStruct((s_len, wd), BF16), jax.ShapeDtypeStruct((s_len, wd), BF16)]
    out_specs = [pl.BlockSpec((tm, wd), lambda i: (i, 0))] * 2
    if n_heads:
        out_shape.append(jax.ShapeDtypeStruct((s_len, 128), F32))
        out_specs.append(pl.BlockSpec((tm, 128), lambda i: (i, 0)))
    return _pcall(
        body,
        name=name,
        grid=(s_len // tm,),
        out_shape=out_shape,
        in_specs=[
            pl.BlockSpec((tm, d), lambda i: (i, 0)),
            pl.BlockSpec((wd, d), lambda i: (0, 0)),
            pl.BlockSpec((tm, wd), lambda i: (i, 0)),
            pl.BlockSpec((tm, wd), lambda i: (i, g_blk)),
        ],
        out_specs=out_specs,
        compiler_params=_cp("parallel"),
    )(dy, w_b, o, gates)


def _in_bwd(dproj, w_in_t, x, ada, dza, ride):
    s_len, d = x.shape
    k_tot = dproj.shape[1]
    tm, tk, dn = min(512, s_len), k_tot // 4, d // 2
    ni, nk = s_len // tm, k_tot // tk
    n = len(ride)

    def body(dp_ref, w_ref, x_ref, sc_ref, dza_ref, *rest):
        ins, (gx_ref, red_ref), outs = rest[:n], rest[n : n + 2], rest[n + 2 : 2 * n + 2]
        sems, acc_s = rest[2 * n + 2 : 2 * n + 5], rest[2 * n + 5]
        i, nh, kk = pl.program_id(0), pl.program_id(1), pl.program_id(2)

        @pl.when((i == 0) & (nh == 0) & (kk == 0))
        def _():
            _rider_start("exchange", ins, outs, *sems)

        @pl.when((i == ni - 1) & (nh == 1) & (kk == nk - 1))
        def _():
            _rider_wait("exchange", ins, outs, *sems)

        part = jnp.dot(dp_ref[...], w_ref[...], preferred_element_type=F32)
        half = pl.ds(pl.multiple_of(nh * dn, dn), dn)

        @pl.when(kk == 0)
        def _():
            acc_s[:, half] = part

        @pl.when(kk > 0)
        def _():
            acc_s[:, half] += part

        @pl.when((nh == 1) & (kk == nk - 1))
        def _():
            dh = acc_s[...]
            xv = x_ref[...]
            mu = jnp.mean(xv, axis=-1, keepdims=True)
            xc = xv - mu
            var = jnp.mean(xc * xc, axis=-1, keepdims=True)
            rstd = lax.rsqrt(var + LN_EPS)
            xhat = xc * rstd
            dxhat = dh * (1.0 + sc_ref[...])
            dx = rstd * (dxhat - jnp.mean(dxhat, axis=-1, keepdims=True) - xhat * jnp.mean(dxhat * xhat, axis=-1, keepdims=True))
            gx_ref[...] = dza_ref[...] + dx
            part_r = jnp.concatenate(
                [jnp.sum(dh, axis=0, keepdims=True), jnp.sum(dh * xhat, axis=0, keepdims=True), jnp.zeros((6, d), F32)], axis=0
            )

            @pl.when(i == 0)
            def _():
                red_ref[...] = part_r

            @pl.when(i > 0)
            def _():
                red_ref[...] += part_r

    row = pl.BlockSpec((tm, d), lambda i, nh, kk: (i, 0))
    hbm = pl.BlockSpec(memory_space=pltpu.HBM)
    return _pcall(
        body,
        name="in_bwd",
        grid=(ni, 2, nk),
        out_shape=[jax.ShapeDtypeStruct((s_len, d), F32), jax.ShapeDtypeStruct((8, d), F32)]
        + [jax.ShapeDtypeStruct(r.shape, r.dtype) for r in ride],
        in_specs=[
            pl.BlockSpec((tm, tk), lambda i, nh, kk: (i, kk)),
            pl.BlockSpec((tk, dn), lambda i, nh, kk: (kk, nh)),
            row,
            pl.BlockSpec((1, d), lambda i, nh, kk: (0, 1)),
            row,
        ]
        + [hbm] * n,
        out_specs=[row, pl.BlockSpec((8, d), lambda i, nh, kk: (0, 0))] + [hbm] * n,
        scratch_shapes=_rider_scratch(n) + [pltpu.VMEM((tm, d), F32)],
        compiler_params=_cp("arbitrary", "arbitrary", "arbitrary"),
    )(dproj, w_in_t, x, ada, dza, *ride)


def _pad_lanes(v, n):
    return jnp.pad(v, ((0, 0), (0, n - v.shape[1])))


def kernel(x, c, w_ada, b_ada, w_in, b_f, attn_sinks, w_br_fox, w_br_swa, w_out, ln_g, ln_b, loss_target, m_w_ada, m_b_ada, m_w_in, m_b_f, m_attn_sinks, m_w_br_fox, m_w_br_swa, m_w_out, m_ln_g, m_ln_b, v_w_ada, v_b_ada, v_w_in, v_b_f, v_attn_sinks, v_w_br_fox, v_w_br_swa, v_w_out, v_ln_g, v_ln_b):
    x2, tgt = x[0], loss_target[0]
    s_len, d = x2.shape
    me = 4 * lax.axis_index("x") + 2 * lax.axis_index("y") + lax.axis_index("c")
    off_ms = OFF_MF + d
    in_pad = off_ms + d
    c_ada = w_ada.shape[2]
    c_in = w_in.shape[2]
    c_br = w_br_fox.shape[2]

    w_in_full = _all_gather(w_in[0].T.astype(BF16), "ag_w_in", pltpu.HBM).reshape(N_DEV * c_in, d)
    w_in_pad = jnp.concatenate(
        [w_in_full[:REAL_FLOG_END], jnp.zeros((FLOG_PAD - N_FLOG, d), BF16), w_in_full[REAL_FLOG_END:]], axis=0
    )
    k_cut = REAL_FLOG_END // c_in

    c_all = _gather_rows(c, "ag_c")
    b_cols = lax.dynamic_slice(b_ada, (0, me * c_ada), (1, c_ada))
    ada_cols = _ada_fwd(c_all, w_ada[0], b_cols)
    ada_g = _all_gather(ada_cols, "ag_ada", pltpu.VMEM)
    ada = lax.dynamic_index_in_dim(ada_g, me, axis=1, keepdims=False).reshape(1, N_DEV * c_ada)

    h = _ln_mod(x2, ada)
    qkv_fox = _mm_cols(h, w_in_pad, OFF_FQ, 3 * FOX_W, BF16, "proj_fox")
    flog = _mm_cols(h, w_in_pad, OFF_FLOG, 128, F32, "proj_flog")
    qkv_swa = _mm_cols(h, w_in_pad, OFF_SQ, SWA_W + 2 * SWA_KVW, BF16, "proj_swa")
    gates, w_bf, w_bs, w_o = _mm_cols(
        h, w_in_pad, OFF_GF, in_pad - OFF_GF, BF16, "proj_gates",
        ride=(w_br_fox[0].astype(BF16), w_br_swa[0].astype(BF16), w_out[0].astype(BF16)),
    )
    w_bf = w_bf.reshape(N_DEV, FOX_W, c_br).transpose(1, 0, 2).reshape(FOX_W, d)
    w_bs = w_bs.reshape(N_DEV, SWA_W, c_br).transpose(1, 0, 2).reshape(SWA_W, d)
    w_o = w_o.reshape(d, d)
    mf_blk = (OFF_MF - OFF_GF) // d

    flog_t = flog[:, :N_FLOG].T
    bf_col = b_f.reshape(FOX_H, 1)
    cum = _fox_cum(flog_t, bf_col)
    cum_row = cum.reshape(FOX_H, 1, s_len)
    o_fox, lse = _fox_fwd(qkv_fox, cum_row)
    sinks = attn_sinks.reshape(SWA_HQ)
    o_swa = _swa_fwd(qkv_swa, sinks)

    y_fox, a_fox = _branch_fwd(o_fox, gates, 0, w_bf, "branch_fox")
    y_swa, a_swa = _branch_fwd(o_swa, gates, 1, w_bs, "branch_swa")
    merged, dza, dsub, red, dmf, dms, dy_fox, dy_swa = _out_stage(gates, mf_blk, y_fox, y_swa, w_o, x2, ada, ln_g, ln_b, tgt)
    loss = lax.psum(0.5 * red[4, 0] / d, ("x", "y", "c"))

    do_fox, dg_fox, delta = _branch_bwd(dy_fox, w_bf, o_fox, gates, 0, "branch_fox_bwd", FOX_H)
    do_swa, dg_swa = _branch_bwd(dy_swa, w_bs, o_swa, gates, 1, "branch_swa_bwd", 0)
    delta_row = delta[:, :FOX_H].T.reshape(FOX_H, 1, s_len)
    dq_f, dk_f, dv_f, dcol, drow = _fox_bwd(
        qkv_fox, cum.reshape(FOX_H, s_len, 1), lse.reshape(FOX_H, 1, s_len), delta_row, do_fox
    )
    dflog_t, dbf = _fox_gate_bwd(drow.reshape(FOX_H, s_len), dcol.reshape(FOX_H, s_len), flog_t, bf_col)
    dq_s, dk_s, dv_s, dsink = _swa_bwd(qkv_swa, sinks, do_swa)
    dflog = _pad_lanes(dflog_t.T, FLOG_PAD).astype(BF16)
    dproj = jnp.concatenate([dq_f, dk_f, dv_f, dflog, dq_s, dk_s, dv_s, dg_fox, dg_swa, dmf, dms], axis=1)
    g_w_bf = _mm_tn(a_fox, dy_fox, "grad_w_br_fox")
    g_w_bs = _mm_tn(a_swa, dy_swa, "grad_w_br_swa")
    g_w_o = _mm_tn(merged, dsub, "grad_w_out")
    g_w_in, r_bf, r_bs, r_o = _mm_tn(
        dproj, h, "grad_w_in",
        ride=(
            g_w_bf.reshape(FOX_W, N_DEV, c_br).transpose(1, 0, 2),
            g_w_bs.reshape(SWA_W, N_DEV, c_br).transpose(1, 0, 2),
            g_w_o.reshape(N_DEV, d // N_DEV, d),
        ),
    )
    pad = FLOG_PAD - N_FLOG
    g_blocks = jnp.stack(
        [g_w_in[k * c_in : (k + 1) * c_in] for k in range(k_cut)]
        + [jnp.concatenate([g_w_in[k_cut * c_in : REAL_FLOG_END], g_w_in[OFF_SQ : (k_cut + 1) * c_in + pad]], axis=0)]
        + [g_w_in[k * c_in + pad : (k + 1) * c_in + pad] for k in range(k_cut + 1, N_DEV)]
    )

    grad_x, red2, r_in = _in_bwd(dproj, w_in_pad, x2, ada, dza, ride=(g_blocks,))
    out_w_in = _sum_adam_t(r_in, w_in[0].T, m_w_in[0].T, v_w_in[0].T, "adam_w_in")
    out_w_in = [o.T for o in out_w_in]
    out_w_bf = _sum_adam(r_bf, w_br_fox[0], m_w_br_fox[0], v_w_br_fox[0], "adam_w_br_fox")
    out_w_bs = _sum_adam(r_bs, w_br_swa[0], m_w_br_swa[0], v_w_br_swa[0], "adam_w_br_swa")
    out_w_o = _sum_adam(r_o, w_out[0], m_w_out[0], v_w_out[0], "adam_w_out")

    packed = jnp.concatenate([red2[0:1], red2[1:2], red[0:1], _pad_lanes(dbf[:, 0].reshape(1, FOX_H), 128), dsink, red[1:2], red[2:3]], axis=1)
    gathered = _gather_rows(packed, "ag_small")
    pack = lambda a, b, cc, dd, e: jnp.concatenate([a, _pad_lanes(b, 128), _pad_lanes(cc, 128), dd, e], axis=1)
    small = _small_adam(
        gathered,
        pack(b_ada, b_f, attn_sinks, ln_g, ln_b),
        pack(m_b_ada, m_b_f, m_attn_sinks, m_ln_g, m_ln_b),
        pack(v_b_ada, v_b_f, v_attn_sinks, v_ln_g, v_ln_b),
    )
    dada_cols = lax.dynamic_slice(gathered, (0, me * c_ada), (N_DEV, c_ada))
    out_w_ada = _wada_adam(c_all.T, dada_cols, w_ada[0], m_w_ada[0], v_w_ada[0])

    o1, o2, o3 = 3 * d, 3 * d + 128, 3 * d + 256

    def unpack(p):
        return p[:, :o1], p[:, o1 : o1 + FOX_H], p[:, o2 : o2 + SWA_HQ], p[:, o3 : o3 + d], p[:, o3 + d : o3 + 2 * d]

    kinds = []
    for k in range(4):
        b_ada_k, b_f_k, sinks_k, ln_g_k, ln_b_k = unpack(small[k])
        kinds.append(
            [out_w_ada[k][None], b_ada_k, out_w_in[k][None], b_f_k, sinks_k, out_w_bf[k][None], out_w_bs[k][None], out_w_o[k][None], ln_g_k, ln_b_k]
        )
    return (loss, grad_x[None], *kinds[0], *kinds[1], *kinds[2], *kinds[3])
```

```python
import numpy as np
import jax
import jax.numpy as jnp
from jax import lax
from jax.experimental import pallas as pl
from jax.experimental.pallas import tpu as pltpu

F32 = jnp.float32
BF16 = jnp.bfloat16
N_DEV = 8
MESH = pl.DeviceIdType.MESH

FOX_H, FOX_DH, FOX_W = 8, 128, 1024
SWA_HQ, SWA_HKV, SWA_DH, SWA_G = 16, 4, 64, 4
SWA_W, SWA_KVW, WINDOW = 1024, 256, 128
LN_EPS = 1e-5
NEG = -1e30
DEPTH = 1
ALPHA = (2.0 * DEPTH) ** 0.25
FOX_SCALE = FOX_DH ** -0.5
SWA_SCALE = SWA_DH ** -0.5
SLOPES = [2.0 ** (-8.0 * (h + 1.0) / SWA_HQ) for h in range(SWA_HQ)]

ADAM_LR, ADAM_B1, ADAM_B2, ADAM_EPS, ADAM_WD, ADAM_STEP = 0.001, 0.9, 0.999, 1e-08, 0.01, 10

N_FLOG = 8
FLOG_PAD = 512
OFF_FQ, OFF_FK, OFF_FV, OFF_FLOG = 0, 1024, 2048, 3072
OFF_SQ = OFF_FLOG + FLOG_PAD
OFF_SK = OFF_SQ + SWA_W
OFF_SV = OFF_SK + SWA_KVW
OFF_GF = OFF_SV + SWA_KVW
OFF_GS = OFF_GF + FOX_W
OFF_MF = OFF_GS + SWA_W
REAL_FLOG_END = OFF_FLOG + N_FLOG

ATT_BLK = 512
VMEM_LIMIT = 58 * 1024 * 1024


def _pcall(body, **kw):
    return pl.pallas_call(body, **kw)


def _cp(*sem):
    return pltpu.CompilerParams(dimension_semantics=sem, vmem_limit_bytes=VMEM_LIMIT)


def _sigmoid(x):
    return 0.5 * jnp.tanh(0.5 * x) + 0.5


def _all_gather(x, name, space):
    m_per, n = x.shape

    def body(x_ref, out_ref, send_sems, recv_sems, local_sem):
        mx, my, mc = lax.axis_index("x"), lax.axis_index("y"), lax.axis_index("c")
        me, sibling = (mx, my, mc), (mx, my, 1 - mc)
        xn, yn, dg = (1 - mx, my), (mx, 1 - my), (1 - mx, 1 - my)
        south = mc == 0
        src_chip = (jnp.where(south, 1 - mx, mx), jnp.where(south, my, 1 - my))
        dst_chip = (jnp.where(south, mx, 1 - mx), jnp.where(south, 1 - my, my))

        def rows(px, py, pc):
            return out_ref.at[4 * px + 2 * py + pc]

        def copy(k, block, to, src=None):
            return pltpu.make_async_remote_copy(
                src_ref=rows(*block) if src is None else src,
                dst_ref=rows(*block),
                send_sem=send_sems.at[k],
                recv_sem=recv_sems.at[k],
                device_id=to,
                device_id_type=MESH,
            )

        mine = pltpu.make_async_copy(x_ref, rows(*me), local_sem)
        mine.start()
        first = [copy(0, me, sibling, src=x_ref), copy(1, me, (*xn, mc), src=x_ref), copy(2, me, (*yn, mc), src=x_ref)]
        for cp in first:
            cp.start()
        copy(1, (*xn, mc), me).wait_recv()
        copy(2, (*yn, mc), me).wait_recv()
        later = [copy(3, (*src_chip, mc), (*dst_chip, mc)), copy(4, (*xn, mc), sibling), copy(5, (*yn, mc), sibling)]
        for cp in later:
            cp.start()
        copy(3, (*dg, mc), me).wait_recv()
        last = copy(6, (*dg, mc), sibling)
        last.start()
        copy(0, sibling, me).wait_recv()
        for k, chip in ((4, xn), (5, yn), (6, dg)):
            copy(k, (*chip, 1 - mc), me).wait_recv()
        for cp in first + later + [last]:
            cp.wait_send()
        mine.wait()

    return _pcall(
        body,
        name=name,
        out_shape=jax.ShapeDtypeStruct((N_DEV, m_per, n), x.dtype),
        in_specs=[pl.BlockSpec(memory_space=space)],
        out_specs=pl.BlockSpec(memory_space=space),
        scratch_shapes=[pltpu.SemaphoreType.DMA((7,)), pltpu.SemaphoreType.DMA((7,)), pltpu.SemaphoreType.DMA],
    )(x)


def _peer(d, mx, my, mc):
    return (1 - mx if (d >> 2) & 1 else mx, 1 - my if (d >> 1) & 1 else my, 1 - mc if d & 1 else mc)


def _rider_copies(kind, ins, outs, send_sems, recv_sems, local_sems):
    mx, my, mc = lax.axis_index("x"), lax.axis_index("y"), lax.axis_index("c")
    me = 4 * mx + 2 * my + mc
    remote, local = [], []
    for a in range(len(ins)):
        if kind == "gather":
            m_per = ins[a].shape[0]
            mine = outs[a].at[pl.ds(me * m_per, m_per), :]
            local.append(pltpu.make_async_copy(ins[a], mine, local_sems.at[a]))
        else:
            local.append(pltpu.make_async_copy(ins[a].at[me], outs[a].at[0], local_sems.at[a]))
        for d in range(1, N_DEV):
            px, py, pc = _peer(d, mx, my, mc)
            if kind == "gather":
                src, dst = ins[a], mine
            else:
                src, dst = ins[a].at[4 * px + 2 * py + pc], outs[a].at[d]
            remote.append(
                pltpu.make_async_remote_copy(
                    src_ref=src,
                    dst_ref=dst,
                    send_sem=send_sems.at[a * 7 + d - 1],
                    recv_sem=recv_sems.at[a * 7 + d - 1],
                    device_id=(px, py, pc),
                    device_id_type=MESH,
                )
            )
    return remote, local


def _rider_start(*args):
    remote, local = _rider_copies(*args)
    for cp in local + remote:
        cp.start()


def _rider_wait(*args):
    remote, local = _rider_copies(*args)
    for cp in remote:
        cp.wait_recv()
    for cp in remote:
        cp.wait_send()
    for cp in local:
        cp.wait()


def _rider_scratch(n):
    return [pltpu.SemaphoreType.DMA((7 * n,)), pltpu.SemaphoreType.DMA((7 * n,)), pltpu.SemaphoreType.DMA((n,))]


def _gather_rows(v, name):
    n = v.shape[1]
    return _all_gather(jnp.broadcast_to(v, (8, n)), name, pltpu.VMEM)[:, 0, :]


def _adamw(w, g, m, v):
    m = ADAM_B1 * m + (1.0 - ADAM_B1) * g
    v = ADAM_B2 * v + (1.0 - ADAM_B2) * (g * g)
    m_hat = m / (1.0 - ADAM_B1**ADAM_STEP)
    v_hat = v / (1.0 - ADAM_B2**ADAM_STEP)
    delta = -ADAM_LR * (m_hat / (jnp.sqrt(v_hat) + ADAM_EPS) + ADAM_WD * w)
    return delta, m, v


def _sum_adam(recv, w, m, v, name):
    _, r_tot, c = recv.shape
    c_pad = -(-c // 128) * 128
    tr = r_tot
    while 8 * tr * c_pad * 4 > 6 * 1024 * 1024 and tr % 32 == 0:
        tr //= 2

    def body(r_ref, w_ref, m_ref, v_ref, g_ref, d_ref, nm_ref, nv_ref):
        g = r_ref[0].astype(F32)
        for k in range(1, N_DEV):
            g = g + r_ref[k].astype(F32)
        d, nm, nv = _adamw(w_ref[...], g, m_ref[...], v_ref[...])
        g_ref[...] = g
        d_ref[...] = d
        nm_ref[...] = nm
        nv_ref[...] = nv

    blk = pl.BlockSpec((tr, c), lambda i: (i, 0))
    return _pcall(
        body,
        name=name,
        grid=(r_tot // tr,),
        out_shape=[jax.ShapeDtypeStruct((r_tot, c), F32)] * 4,
        in_specs=[pl.BlockSpec((N_DEV, tr, c), lambda i: (0, i, 0)), blk, blk, blk],
        out_specs=[blk] * 4,
        compiler_params=_cp("parallel"),
    )(recv, w, m, v)


def _sum_adam_t(recv, w, m, v, name):
    _, c, r_tot = recv.shape
    tr = min(256, r_tot)

    def body(r_ref, w_ref, m_ref, v_ref, g_ref, d_ref, nm_ref, nv_ref):
        g = r_ref[0].astype(F32)
        for k in range(1, N_DEV):
            g = g + r_ref[k].astype(F32)
        d, nm, nv = _adamw(w_ref[...], g, m_ref[...], v_ref[...])
        g_ref[...] = g
        d_ref[...] = d
        nm_ref[...] = nm
        nv_ref[...] = nv

    blk = pl.BlockSpec((c, tr), lambda i: (0, i))
    return _pcall(
        body,
        name=name,
        grid=(r_tot // tr,),
        out_shape=[jax.ShapeDtypeStruct((c, r_tot), F32)] * 4,
        in_specs=[pl.BlockSpec((N_DEV, c, tr), lambda i: (0, 0, i)), blk, blk, blk],
        out_specs=[blk] * 4,
        compiler_params=_cp("parallel"),
    )(recv, w, m, v)


def _wada_adam(c_t, dada_cols, w, m, v):
    d_model, c = w.shape
    tr = min(256, d_model)

    def body(ct_ref, da_ref, w_ref, m_ref, v_ref, g_ref, d_ref, nm_ref, nv_ref):
        g = jnp.dot(ct_ref[...].astype(BF16), da_ref[...].astype(BF16), preferred_element_type=F32)
        d, nm, nv = _adamw(w_ref[...], g, m_ref[...], v_ref[...])
        g_ref[...] = g
        d_ref[...] = d
        nm_ref[...] = nm
        nv_ref[...] = nv

    blk = pl.BlockSpec((tr, c), lambda i: (i, 0))
    return _pcall(
        body,
        name="wada_adam",
        grid=(d_model // tr,),
        out_shape=[jax.ShapeDtypeStruct((d_model, c), F32)] * 4,
        in_specs=[pl.BlockSpec((tr, N_DEV), lambda i: (i, 0)), pl.BlockSpec((N_DEV, c), lambda i: (0, 0)), blk, blk, blk],
        out_specs=[blk] * 4,
        compiler_params=_cp("parallel"),
    )(c_t, dada_cols, w, m, v)


def _small_adam(gathered, w, m, v):
    p = w.shape[1]

    def body(a_ref, w_ref, m_ref, v_ref, g_ref, d_ref, nm_ref, nv_ref):
        g = a_ref[0:1, :]
        for k in range(1, N_DEV):
            g = g + a_ref[k : k + 1, :]
        d, nm, nv = _adamw(w_ref[...], g, m_ref[...], v_ref[...])
        g_ref[...] = g
        d_ref[...] = d
        nm_ref[...] = nm
        nv_ref[...] = nv

    return _pcall(
        body,
        name="small_adam",
        out_shape=[jax.ShapeDtypeStruct((1, p), F32)] * 4,
    )(gathered, w, m, v)


def _ada_fwd(c_all, w_ada, b_cols):
    c = w_ada.shape[1]

    def body(c_ref, w_ref, b_ref, o_ref):
        o_ref[...] = jnp.dot(c_ref[...].astype(BF16), w_ref[...].astype(BF16), preferred_element_type=F32) + b_ref[...]

    return _pcall(
        body,
        name="ada_fwd",
        out_shape=jax.ShapeDtypeStruct((N_DEV, c), F32),
        compiler_params=_cp(),
    )(c_all, w_ada, b_cols)


def _ln_mod(x, ada):
    s_len, d = x.shape
    tm = min(512, s_len)

    def body(x_ref, sh_ref, sc_ref, h_ref):
        xv = x_ref[...]
        mu = jnp.mean(xv, axis=-1, keepdims=True)
        xc = xv - mu
        var = jnp.mean(xc * xc, axis=-1, keepdims=True)
        xhat = xc * lax.rsqrt(var + LN_EPS)
        h_ref[...] = (xhat * (1.0 + sc_ref[...]) + sh_ref[...]).astype(BF16)

    return _pcall(
        body,
        name="ln_mod",
        grid=(s_len // tm,),
        out_shape=jax.ShapeDtypeStruct((s_len, d), BF16),
        in_specs=[
            pl.BlockSpec((tm, d), lambda i: (i, 0)),
            pl.BlockSpec((1, d), lambda i: (0, 0)),
            pl.BlockSpec((1, d), lambda i: (0, 1)),
        ],
        out_specs=pl.BlockSpec((tm, d), lambda i: (i, 0)),
        compiler_params=_cp("parallel"),
    )(x, ada, ada)


def _mm_cols(a, b, col_off, n_cols, out_dtype, name, ride=()):
    m, k = a.shape
    tm = min(1024, m)
    tn = next(t for t in (1024, 512, 128) if n_cols % t == 0 and col_off % t == 0)
    off = col_off // tn
    ni, nj = m // tm, n_cols // tn
    n = len(ride)

    def body(a_ref, b_ref, *rest):
        ins, o_ref, outs, sems = rest[:n], rest[n], rest[n + 1 : 2 * n + 1], rest[2 * n + 1 :]
        i, j = pl.program_id(0), pl.program_id(1)
        if n:

            @pl.when((i == 0) & (j == 0))
            def _():
                _rider_start("gather", ins, outs, *sems)

        o_ref[...] = lax.dot_general(a_ref[...], b_ref[...], _NT, preferred_element_type=F32).astype(out_dtype)
        if n:

            @pl.when((i == ni - 1) & (j == nj - 1))
            def _():
                _rider_wait("gather", ins, outs, *sems)

    hbm = pl.BlockSpec(memory_space=pltpu.HBM)
    out = _pcall(
        body,
        name=name,
        grid=(ni, nj),
        out_shape=[jax.ShapeDtypeStruct((m, n_cols), out_dtype)]
        + [jax.ShapeDtypeStruct((N_DEV * r.shape[0], r.shape[1]), r.dtype) for r in ride],
        in_specs=[pl.BlockSpec((tm, k), lambda i, j: (i, 0)), pl.BlockSpec((tn, k), lambda i, j: (off + j, 0))] + [hbm] * n,
        out_specs=[pl.BlockSpec((tm, tn), lambda i, j: (i, j))] + [hbm] * n,
        scratch_shapes=_rider_scratch(n) if n else [],
        compiler_params=_cp("arbitrary", "arbitrary") if n else _cp("parallel", "parallel"),
    )(a, b, *ride)
    return out if n else out[0]


def _mm_tn(a, b, name, ride=()):
    s_len, m = a.shape
    n = b.shape[1]
    tm, tn, ts = min(1024, m), min(2048, n), min(2048, s_len)
    ni, nj, ns = m // tm, n // tn, s_len // ts
    nr = len(ride)

    def body(a_ref, b_ref, *rest):
        ins, o_ref, outs = rest[:nr], rest[nr], rest[nr + 1 : 2 * nr + 1]
        sems, acc_s = rest[2 * nr + 1 : -1], rest[-1]
        i, j, kk = pl.program_id(0), pl.program_id(1), pl.program_id(2)
        if nr:

            @pl.when((i == 0) & (j == 0) & (kk == 0))
            def _():
                _rider_start("exchange", ins, outs, *sems)

            @pl.when((i == ni - 1) & (j == nj - 1) & (kk == ns - 1))
            def _():
                _rider_wait("exchange", ins, outs, *sems)

        part = lax.dot_general(a_ref[...], b_ref[...], _TN, preferred_element_type=F32)

        @pl.when(kk == 0)
        def _():
            acc_s[...] = part

        @pl.when(kk > 0)
        def _():
            acc_s[...] += part

        @pl.when(kk == ns - 1)
        def _():
            o_ref[...] = acc_s[...].astype(BF16)

    hbm = pl.BlockSpec(memory_space=pltpu.HBM)
    out = _pcall(
        body,
        name=name,
        grid=(ni, nj, ns),
        out_shape=[jax.ShapeDtypeStruct((m, n), BF16)] + [jax.ShapeDtypeStruct(r.shape, r.dtype) for r in ride],
        in_specs=[pl.BlockSpec((ts, tm), lambda i, j, kk: (kk, i)), pl.BlockSpec((ts, tn), lambda i, j, kk: (kk, j))] + [hbm] * nr,
        out_specs=[pl.BlockSpec((tm, tn), lambda i, j, kk: (i, j))] + [hbm] * nr,
        scratch_shapes=(_rider_scratch(nr) if nr else []) + [pltpu.VMEM((tm, tn), F32)],
        compiler_params=_cp("arbitrary", "arbitrary", "arbitrary") if nr else _cp("parallel", "parallel", "arbitrary"),
    )(a, b, *ride)
    return out if nr else out[0]


def _split3(a):
    hi = a.astype(BF16)
    r1 = a - hi.astype(F32)
    mid = r1.astype(BF16)
    lo = (r1 - mid.astype(F32)).astype(BF16)
    return hi, mid, lo


def _dot_ones(a, tri):
    return sum(jnp.dot(t, tri, preferred_element_type=F32) for t in _split3(a))


def _log_sigmoid(x):
    return jnp.minimum(x, 0.0) - jnp.log1p(jnp.exp(-jnp.abs(x)))


def _fox_cum(flog_t, bf_col):
    s_len = flog_t.shape[1]

    def body(fl_ref, bf_ref, cum_ref):
        r = lax.broadcasted_iota(jnp.int32, (128, 128), 0)
        c = lax.broadcasted_iota(jnp.int32, (128, 128), 1)
        upper = (r <= c).astype(BF16)

        def step(t, carry):
            sl = pl.ds(pl.multiple_of(t * 128, 128), 128)
            lf = _log_sigmoid(fl_ref[:, sl] + bf_ref[...])
            cs = _dot_ones(lf, upper) + carry
            cum_ref[:, sl] = cs
            return cs[:, 127:128]

        lax.fori_loop(0, s_len // 128, step, jnp.zeros((FOX_H, 1), F32))

    return _pcall(body, name="fox_cum", out_shape=jax.ShapeDtypeStruct((FOX_H, s_len), F32))(flog_t, bf_col)


def _fox_gate_bwd(drow, dcol, flog_t, bf_col):
    s_len = flog_t.shape[1]
    n = s_len // 128

    def body(dr_ref, dc_ref, fl_ref, bf_ref, dfl_ref, dbf_ref):
        r = lax.broadcasted_iota(jnp.int32, (128, 128), 0)
        c = lax.broadcasted_iota(jnp.int32, (128, 128), 1)
        lower = (r >= c).astype(BF16)

        def step(t, carry):
            run, tot = carry
            sl = pl.ds(pl.multiple_of((n - 1 - t) * 128, 128), 128)
            rc = _dot_ones(dr_ref[:, sl] - dc_ref[:, sl], lower) + run
            dfl = rc * _sigmoid(-(fl_ref[:, sl] + bf_ref[...]))
            dfl_ref[:, sl] = dfl
            return rc[:, 0:1], tot + jnp.sum(dfl, axis=1, keepdims=True)

        zero = jnp.zeros((FOX_H, 1), F32)
        _, tot = lax.fori_loop(0, n, step, (zero, zero))
        dbf_ref[...] = jnp.broadcast_to(tot, (FOX_H, 128))

    return _pcall(
        body,
        name="fox_gate_bwd",
        out_shape=[jax.ShapeDtypeStruct((FOX_H, s_len), F32), jax.ShapeDtypeStruct((FOX_H, 128), F32)],
    )(drow, dcol, flog_t, bf_col)


def _diag_mask(blk, transposed=False):
    r = lax.broadcasted_iota(jnp.int32, (blk, blk), 0)
    c = lax.broadcasted_iota(jnp.int32, (blk, blk), 1)
    return c >= r if transposed else r >= c


_NT = (((1,), (1,)), ((), ()))
_TN = (((0,), (0,)), ((), ()))


def _fox_fwd(qkv, cum_row):
    s_len = qkv.shape[0]
    blk = min(ATT_BLK, s_len)
    nb = s_len // blk
    log2e = 1.4426950408889634

    def body(q_ref, k_ref, v_ref, c_ref, o_ref, lse_ref, mx_s, acc_s, u_s, v1_s):
        i = pl.program_id(1)

        @pl.when(i == 0)
        def _():
            v1_s[:, :FOX_DH] = v_ref[...]
            v1_s[:, FOX_DH:] = (lax.broadcasted_iota(jnp.int32, (s_len, FOX_DH), 1) == 0).astype(BF16)

        def key_cols(j, n):
            return pl.ds(pl.multiple_of(j * blk, blk), n * blk)

        def walk(tile):
            def four_pairs(t, carry):
                for u in range(4):
                    tile(8 * t + 2 * u, 2, False)
                return carry

            lax.fori_loop(0, i // 8, four_pairs, 0)

            @pl.when((i // 4) % 2 == 1)
            def _():
                tile(8 * (i // 8), 2, False)
                tile(8 * (i // 8) + 2, 2, False)

            @pl.when((i // 2) % 2 == 1)
            def _():
                tile(4 * (i // 4), 2, False)

            @pl.when(i % 2 == 1)
            def _():
                tile(i - 1, 1, False)

            tile(i, 1, True)

        def lane_max(j, n, masked):
            cols = key_cols(j, n)
            u = lax.dot_general(q_ref[...], k_ref[cols, :], _NT, preferred_element_type=F32) * (FOX_SCALE * log2e) - c_ref[:, cols] * log2e
            if masked:
                u = jnp.where(_diag_mask(blk), u, NEG)
            u_s[:, cols] = u
            part = u[:, 0:128]
            for t in range(1, n * blk // 128):
                part = jnp.maximum(part, u[:, t * 128 : (t + 1) * 128])
            mx_s[...] = jnp.maximum(mx_s[...], part)

        mx_s[...] = jnp.full(mx_s.shape, NEG, F32)
        walk(lane_max)
        m = jnp.max(mx_s[...], axis=1, keepdims=True)

        def weigh(j, n, masked):
            cols = key_cols(j, n)
            p = jnp.exp2(u_s[:, cols] - m)
            acc_s[...] += jnp.dot(p.astype(BF16), v1_s[cols, :], preferred_element_type=F32)

        acc_s[...] = jnp.zeros(acc_s.shape, F32)
        walk(weigh)
        l = acc_s[:, FOX_DH : FOX_DH + 1]
        o_ref[...] = acc_s[:, :FOX_DH] / l
        lse_ref[...] = m * (1.0 / log2e) + jnp.log(l)

    return _pcall(
        body,
        name="fox_fwd",
        grid=(FOX_H, nb),
        out_shape=[jax.ShapeDtypeStruct((s_len, FOX_W), F32), jax.ShapeDtypeStruct((FOX_H, s_len, 1), F32)],
        in_specs=[
            pl.BlockSpec((blk, FOX_DH), lambda h, i: (i, h)),
            pl.BlockSpec((s_len, FOX_DH), lambda h, i: (0, FOX_H + h)),
            pl.BlockSpec((s_len, FOX_DH), lambda h, i: (0, 2 * FOX_H + h)),
            pl.BlockSpec((None, 1, s_len), lambda h, i: (h, 0, 0)),
        ],
        out_specs=[
            pl.BlockSpec((blk, FOX_DH), lambda h, i: (i, h)),
            pl.BlockSpec((None, blk, 1), lambda h, i: (h, i, 0)),
        ],
        scratch_shapes=[
            pltpu.VMEM((blk, 128), F32),
            pltpu.VMEM((blk, 2 * FOX_DH), F32),
            pltpu.VMEM((blk, s_len), F32),
            pltpu.VMEM((s_len, 2 * FOX_DH), BF16),
        ],
        compiler_params=_cp("arbitrary", "arbitrary"),
    )(qkv, qkv, qkv, cum_row)


def _fox_bwd(qkv, cum_col, lse_row, delta_row, do):
    s_len = qkv.shape[0]
    blk = min(ATT_BLK, s_len)
    nb = s_len // blk

    def body(q_ref, k_ref, v_ref, c_ref, lse_ref, dl_ref, do_ref, dq_ref, dk_ref, dv_ref, dc_ref, dr_ref, dk_s, dv_s, dc_s, cb_s, dq_s):
        j = pl.program_id(1)

        @pl.when(j == 0)
        def _():
            dq_s[...] = jnp.zeros(dq_s.shape, F32)
            dr_ref[...] = jnp.zeros(dr_ref.shape, F32)

        dk_s[...] = jnp.zeros(dk_s.shape, F32)
        dv_s[...] = jnp.zeros(dv_s.shape, F32)
        dc_s[...] = jnp.zeros(dc_s.shape, F32)
        cb_s[...] = jnp.broadcast_to(c_ref[...], cb_s.shape)

        def tile(i, n, diag):
            rows = pl.ds(pl.multiple_of(i * blk, blk), n * blk)
            q, dob = q_ref[rows, :], do_ref[rows, :]
            k, v = k_ref[...], v_ref[...]
            s_t = lax.dot_general(k, q, _NT, preferred_element_type=F32) * FOX_SCALE - cb_s[:, : n * blk]
            p_t = jnp.exp(s_t - lse_ref[:, rows])
            if diag:
                p_t = jnp.where(_diag_mask(blk, transposed=True), p_t, 0.0)
            dp_t = lax.dot_general(v, dob, _NT, preferred_element_type=F32)
            ds_t = p_t * (dp_t - dl_ref[:, rows])
            dsb = ds_t.astype(BF16)
            dv_s[...] += jnp.dot(p_t.astype(BF16), dob, preferred_element_type=F32)
            dk_s[...] += jnp.dot(dsb, q, preferred_element_type=F32)
            dq_c = lax.dot_general(dsb, k, _TN, preferred_element_type=F32)
            part = ds_t[:, 0:128]
            for t in range(1, n * blk // 128):
                part = part + ds_t[:, t * 128 : (t + 1) * 128]
            dc_s[...] += part
            dr_ref[:, rows] += jnp.sum(ds_t, axis=0, keepdims=True)
            if diag:
                dq_s[rows, :] = (dq_s[rows, :] + dq_c) * FOX_SCALE
            else:
                dq_s[rows, :] += dq_c

        tile(j, 1, True)
        below = nb - 1 - j
        b0, b1, b2 = below % 2, (below // 2) % 2, (below // 4) % 2

        @pl.when(b0 == 1)
        def _():
            tile(j + 1, 1, False)

        @pl.when(b1 == 1)
        def _():
            tile(j + 1 + b0, 2, False)

        @pl.when(b2 == 1)
        def _():
            tile(j + 1 + b0 + 2 * b1, 2, False)
            tile(j + 3 + b0 + 2 * b1, 2, False)

        first = j + 1 + b0 + 2 * b1 + 4 * b2

        def four_pairs(t, carry):
            for u in range(4):
                tile(first + 8 * t + 2 * u, 2, False)
            return carry

        lax.fori_loop(0, below // 8, four_pairs, 0)
        dk_ref[...] = (dk_s[...] * FOX_SCALE).astype(BF16)
        dv_ref[...] = dv_s[...].astype(BF16)
        dc_ref[...] = jnp.sum(dc_s[...], axis=1, keepdims=True)

        @pl.when(j == nb - 1)
        def _():
            dq_ref[...] = dq_s[...].astype(BF16)

    head = lambda h, j: (0, h)
    row = pl.BlockSpec((None, 1, s_len), lambda h, j: (h, 0, 0))
    return _pcall(
        body,
        name="fox_bwd",
        grid=(FOX_H, nb),
        out_shape=[
            jax.ShapeDtypeStruct((s_len, FOX_W), BF16),
            jax.ShapeDtypeStruct((s_len, FOX_W), BF16),
            jax.ShapeDtypeStruct((s_len, FOX_W), BF16),
            jax.ShapeDtypeStruct((FOX_H, s_len, 1), F32),
            jax.ShapeDtypeStruct((FOX_H, 1, s_len), F32),
        ],
        in_specs=[
            pl.BlockSpec((s_len, FOX_DH), head),
            pl.BlockSpec((blk, FOX_DH), lambda h, j: (j, FOX_H + h)),
            pl.BlockSpec((blk, FOX_DH), lambda h, j: (j, 2 * FOX_H + h)),
            pl.BlockSpec((None, blk, 1), lambda h, j: (h, j, 0)),
            row,
            row,
            pl.BlockSpec((s_len, FOX_DH), head),
        ],
        out_specs=[
            pl.BlockSpec((s_len, FOX_DH), head),
            pl.BlockSpec((blk, FOX_DH), lambda h, j: (j, h)),
            pl.BlockSpec((blk, FOX_DH), lambda h, j: (j, h)),
            pl.BlockSpec((None, blk, 1), lambda h, j: (h, j, 0)),
            row,
        ],
        scratch_shapes=[
            pltpu.VMEM((blk, FOX_DH), F32),
            pltpu.VMEM((blk, FOX_DH), F32),
            pltpu.VMEM((blk, 128), F32),
            pltpu.VMEM((blk, 2 * blk), F32),
            pltpu.VMEM((s_len, FOX_DH), F32),
        ],
        compiler_params=_cp("parallel", "arbitrary"),
    )(qkv, qkv, qkv, cum_col, lse_row, delta_row, do)


def _swa_bias():
    cols = SWA_G * WINDOW
    k = np.arange(2 * WINDOW)[:, None]
    q = np.arange(cols)[None, :]
    dist = (q % WINDOW) - k + WINDOW
    valid = (dist >= 0) & (dist < WINDOW)
    out = np.empty((2, SWA_HKV, 2 * WINDOW, cols), np.float32)
    for g in range(SWA_HKV):
        slope = np.array([SLOPES[g * SWA_G + t] for t in range(SWA_G)], np.float32)[q // WINDOW]
        bias = -(slope * dist.astype(np.float32))
        out[0, g] = np.where(valid & (k >= WINDOW), bias, np.float32(NEG))
        out[1, g] = np.where(valid, bias, np.float32(NEG))
    return jnp.asarray(out)


def _swa_group(i, q_ref, kk, sinks_ref, bias_ref, g):
    cols = SWA_G * WINDOW
    head = lax.broadcasted_iota(jnp.int32, (1, cols), 1) // WINDOW
    sink = jnp.zeros((1, cols), F32)
    for t in range(SWA_G):
        sink = jnp.where(head == t, sinks_ref[g * SWA_G + t], sink)
    q = jnp.concatenate([q_ref[:, (g * SWA_G + t) * SWA_DH : (g * SWA_G + t + 1) * SWA_DH] for t in range(SWA_G)], axis=0)
    k = kk[:, g * SWA_DH : (g + 1) * SWA_DH]
    s = lax.dot_general(k, q, _NT, preferred_element_type=F32) * SWA_SCALE + bias_ref[jnp.minimum(i, 1), g]
    m = jnp.maximum(jnp.max(s, axis=0, keepdims=True), sink)
    e = jnp.exp(s - m)
    e_sink = jnp.exp(sink - m)
    inv = 1.0 / (jnp.sum(e, axis=0, keepdims=True) + e_sink)
    return q, k, e * inv, e_sink * inv


def _swa_fwd(qkv, sinks):
    s_len = qkv.shape[0]
    nb = s_len // WINDOW
    bias_spec = pl.BlockSpec((2, SWA_HKV, 2 * WINDOW, SWA_G * WINDOW), lambda t: (0, 0, 0, 0))

    per_step = 4

    def body(q_ref, kp_ref, kc_ref, vp_ref, vc_ref, sinks_ref, bias_ref, o_ref):
        step = pl.program_id(0)
        kc, vc = kc_ref[...], vc_ref[...]
        for b in range(per_step):
            rows = pl.ds(b * WINDOW, WINDOW)
            if b == 0:
                kk = jnp.concatenate([kp_ref[...], kc[:WINDOW]], axis=0)
                vv = jnp.concatenate([vp_ref[...], vc[:WINDOW]], axis=0)
            else:
                kk, vv = kc[(b - 1) * WINDOW : (b + 1) * WINDOW], vc[(b - 1) * WINDOW : (b + 1) * WINDOW]
            for g in range(SWA_HKV):
                _, _, p, _ = _swa_group(per_step * step + b, q_ref.at[rows, :], kk, sinks_ref, bias_ref, g)
                o = lax.dot_general(p.astype(BF16), vv[:, g * SWA_DH : (g + 1) * SWA_DH], _TN, preferred_element_type=F32)
                for t in range(SWA_G):
                    h = g * SWA_G + t
                    o_ref[rows, h * SWA_DH : (h + 1) * SWA_DH] = o[t * WINDOW : (t + 1) * WINDOW, :]

    before = lambda t: jnp.maximum(per_step * t - 1, 0)
    return _pcall(
        body,
        name="swa_fwd",
        grid=(nb // per_step,),
        out_shape=jax.ShapeDtypeStruct((s_len, SWA_W), F32),
        in_specs=[
            pl.BlockSpec((per_step * WINDOW, SWA_W), lambda t: (t, 0)),
            pl.BlockSpec((WINDOW, SWA_KVW), lambda t: (before(t), 4)),
            pl.BlockSpec((per_step * WINDOW, SWA_KVW), lambda t: (t, 4)),
            pl.BlockSpec((WINDOW, SWA_KVW), lambda t: (before(t), 5)),
            pl.BlockSpec((per_step * WINDOW, SWA_KVW), lambda t: (t, 5)),
            pl.BlockSpec(memory_space=pltpu.SMEM),
            bias_spec,
        ],
        out_specs=pl.BlockSpec((per_step * WINDOW, SWA_W), lambda t: (t, 0)),
        compiler_params=_cp("parallel"),
    )(qkv, qkv, qkv, qkv, qkv, sinks, _swa_bias())


def _swa_bwd(qkv, sinks, do):
    s_len = qkv.shape[0]
    nb = s_len // WINDOW
    bias_spec = pl.BlockSpec((2, SWA_HKV, 2 * WINDOW, SWA_G * WINDOW), lambda t: (0, 0, 0, 0))

    def body(q_ref, kp_ref, kc_ref, vp_ref, vc_ref, sinks_ref, bias_ref, do_ref, dq_ref, dk_ref, dv_ref, dsink_ref, ck_s, cv_s, dkk_s, dvv_s, pk_s, pv_s):
        step = pl.program_id(0)
        i_top = nb - 1 - 2 * step

        @pl.when(step == 0)
        def _():
            ck_s[...] = jnp.zeros(ck_s.shape, F32)
            cv_s[...] = jnp.zeros(cv_s.shape, F32)
            dsink_ref[...] = jnp.zeros(dsink_ref.shape, F32)

        kc, vc = kc_ref[...], vc_ref[...]
        lane = lax.broadcasted_iota(jnp.int32, (1, 128), 1)

        def block(i, rows, kk, vv, dsink):
            q_rows, do_rows = q_ref.at[rows, :], do_ref.at[rows, :]
            for g in range(SWA_HKV):
                cols = slice(g * SWA_DH, (g + 1) * SWA_DH)
                q, k, p, p_sink = _swa_group(i, q_rows, kk, sinks_ref, bias_ref, g)
                dob = jnp.concatenate([do_rows[:, (g * SWA_G + t) * SWA_DH : (g * SWA_G + t + 1) * SWA_DH] for t in range(SWA_G)], axis=0)
                dp = lax.dot_general(vv[:, cols], dob, _NT, preferred_element_type=F32)
                delta = jnp.sum(p * dp, axis=0, keepdims=True)
                dsb = (p * (dp - delta)).astype(BF16)
                dq = (lax.dot_general(dsb, k, _TN, preferred_element_type=F32) * SWA_SCALE).astype(BF16)
                ps_d = p_sink * delta
                for t in range(SWA_G):
                    h = g * SWA_G + t
                    dq_ref[rows, h * SWA_DH : (h + 1) * SWA_DH] = dq[t * WINDOW : (t + 1) * WINDOW, :]
                    dsink = dsink + jnp.where(lane == h, -jnp.sum(ps_d[:, t * WINDOW : (t + 1) * WINDOW], axis=1, keepdims=True), 0.0)
                dkk_s[:, cols] = jnp.dot(dsb, q, preferred_element_type=F32) * SWA_SCALE
                dvv_s[:, cols] = jnp.dot(p.astype(BF16), dob, preferred_element_type=F32)
            return dsink

        bottom, top = pl.ds(0, WINDOW), pl.ds(WINDOW, WINDOW)
        dsink = block(i_top, top, kc, vc, jnp.zeros((1, 128), F32))
        dk_ref[top, :] = (dkk_s[WINDOW:, :] + ck_s[...]).astype(BF16)
        dv_ref[top, :] = (dvv_s[WINDOW:, :] + cv_s[...]).astype(BF16)
        pk_s[...] = dkk_s[:WINDOW, :]
        pv_s[...] = dvv_s[:WINDOW, :]
        kk = jnp.concatenate([kp_ref[...], kc[:WINDOW]], axis=0)
        vv = jnp.concatenate([vp_ref[...], vc[:WINDOW]], axis=0)
        dsink = block(i_top - 1, bottom, kk, vv, dsink)
        dk_ref[bottom, :] = (dkk_s[WINDOW:, :] + pk_s[...]).astype(BF16)
        dv_ref[bottom, :] = (dvv_s[WINDOW:, :] + pv_s[...]).astype(BF16)
        ck_s[...] = dkk_s[:WINDOW, :]
        cv_s[...] = dvv_s[:WINDOW, :]
        dsink_ref[...] += dsink

    pair = lambda t: (nb // 2 - 1 - t, 0)
    before = lambda t: jnp.maximum(nb - 3 - 2 * t, 0)
    return _pcall(
        body,
        name="swa_bwd",
        grid=(nb // 2,),
        out_shape=[
            jax.ShapeDtypeStruct((s_len, SWA_W), BF16),
            jax.ShapeDtypeStruct((s_len, SWA_KVW), BF16),
            jax.ShapeDtypeStruct((s_len, SWA_KVW), BF16),
            jax.ShapeDtypeStruct((1, 128), F32),
        ],
        in_specs=[
            pl.BlockSpec((2 * WINDOW, SWA_W), pair),
            pl.BlockSpec((WINDOW, SWA_KVW), lambda t: (before(t), 4)),
            pl.BlockSpec((2 * WINDOW, SWA_KVW), lambda t: (nb // 2 - 1 - t, 4)),
            pl.BlockSpec((WINDOW, SWA_KVW), lambda t: (before(t), 5)),
            pl.BlockSpec((2 * WINDOW, SWA_KVW), lambda t: (nb // 2 - 1 - t, 5)),
            pl.BlockSpec(memory_space=pltpu.SMEM),
            bias_spec,
            pl.BlockSpec((2 * WINDOW, SWA_W), pair),
        ],
        out_specs=[
            pl.BlockSpec((2 * WINDOW, SWA_W), pair),
            pl.BlockSpec((2 * WINDOW, SWA_KVW), pair),
            pl.BlockSpec((2 * WINDOW, SWA_KVW), pair),
            pl.BlockSpec((1, 128), lambda t: (0, 0)),
        ],
        scratch_shapes=[
            pltpu.VMEM((WINDOW, SWA_KVW), F32),
            pltpu.VMEM((WINDOW, SWA_KVW), F32),
            pltpu.VMEM((2 * WINDOW, SWA_KVW), F32),
            pltpu.VMEM((2 * WINDOW, SWA_KVW), F32),
            pltpu.VMEM((WINDOW, SWA_KVW), F32),
            pltpu.VMEM((WINDOW, SWA_KVW), F32),
        ],
        compiler_params=_cp("arbitrary"),
    )(qkv, qkv, qkv, qkv, qkv, sinks, _swa_bias(), do)


def _branch_fwd(o, gates, g_blk, w_b, name):
    s_len, wd = o.shape
    d = w_b.shape[1]
    tm = min(1024, s_len)

    def body(o_ref, g_ref, w_ref, y_ref, a_ref):
        g = g_ref[...].astype(F32)
        a = (o_ref[...] * (g * _sigmoid(g))).astype(BF16)
        a_ref[...] = a
        y_ref[...] = jnp.dot(a, w_ref[...], preferred_element_type=F32).astype(BF16)

    return _pcall(
        body,
        name=name,
        grid=(s_len // tm,),
        out_shape=[jax.ShapeDtypeStruct((s_len, d), BF16), jax.ShapeDtypeStruct((s_len, wd), BF16)],
        in_specs=[
            pl.BlockSpec((tm, wd), lambda i: (i, 0)),
            pl.BlockSpec((tm, wd), lambda i: (i, g_blk)),
            pl.BlockSpec((wd, d), lambda i: (0, 0)),
        ],
        out_specs=[pl.BlockSpec((tm, d), lambda i: (i, 0)), pl.BlockSpec((tm, wd), lambda i: (i, 0))],
        compiler_params=_cp("parallel"),
    )(o, gates, w_b)


def _out_stage(gates, mf_blk, y_fox, y_swa, w_out, x, ada, ln_g, ln_b, target):
    s_len, d = x.shape
    tm = min(256, s_len)
    n_steps = s_len // tm

    def body(mf_ref, ms_ref, yf_ref, ys_ref, w_ref, x_ref, gate_ref, lg_ref, lb_ref, t_ref, mg_ref, dza_ref, dsub_ref, red_ref, dmf_ref, dms_ref, dyf_ref, dys_ref):
        i = pl.program_id(0)
        sf, ss = _sigmoid(mf_ref[...].astype(F32)), _sigmoid(ms_ref[...].astype(F32))
        yf, ys = yf_ref[...].astype(F32), ys_ref[...].astype(F32)
        merged = sf * yf + ss * ys
        mb = merged.astype(BF16)
        mg_ref[...] = mb
        sub = jnp.dot(mb, w_ref[...], preferred_element_type=F32)
        gate = gate_ref[...]
        z = ALPHA * x_ref[...] + gate * sub
        mu = jnp.mean(z, axis=-1, keepdims=True)
        zc = z - mu
        var = jnp.mean(zc * zc, axis=-1, keepdims=True)
        rstd = lax.rsqrt(var + LN_EPS)
        zhat = zc * rstd
        err = zhat * lg_ref[...] + lb_ref[...] - t_ref[...]
        dout = err * (1.0 / d)
        dzhat = dout * lg_ref[...]
        dz = rstd * (dzhat - jnp.mean(dzhat, axis=-1, keepdims=True) - zhat * jnp.mean(dzhat * zhat, axis=-1, keepdims=True))
        dza_ref[...] = ALPHA * dz
        dsub = (gate * dz).astype(BF16)
        dsub_ref[...] = dsub
        dm = lax.dot_general(dsub, w_ref[...], _NT, preferred_element_type=F32)
        dmf_ref[...] = (dm * yf * (sf * (1.0 - sf))).astype(BF16)
        dms_ref[...] = (dm * ys * (ss * (1.0 - ss))).astype(BF16)
        dyf_ref[...] = (dm * sf).astype(BF16)
        dys_ref[...] = (dm * ss).astype(BF16)
        part = jnp.concatenate(
            [
                jnp.sum(dz * sub, axis=0, keepdims=True),
                jnp.sum(dout * zhat, axis=0, keepdims=True),
                jnp.sum(dout, axis=0, keepdims=True),
                jnp.sum(err * err, axis=0, keepdims=True),
                jnp.zeros((4, d), F32),
            ],
            axis=0,
        )

        @pl.when(i == 0)
        def _():
            red_ref[...] = part

        @pl.when(i > 0)
        def _():
            red_ref[...] += part

        @pl.when(i == n_steps - 1)
        def _():
            red_ref[4:5, :] = jnp.broadcast_to(jnp.sum(red_ref[3:4, :], axis=1, keepdims=True), (1, d))

    row = pl.BlockSpec((tm, d), lambda i: (i, 0))
    vec = pl.BlockSpec((1, d), lambda i: (0, 0))
    return _pcall(
        body,
        name="out_stage",
        grid=(n_steps,),
        out_shape=[
            jax.ShapeDtypeStruct((s_len, d), BF16),
            jax.ShapeDtypeStruct((s_len, d), F32),
            jax.ShapeDtypeStruct((s_len, d), BF16),
            jax.ShapeDtypeStruct((8, d), F32),
        ]
        + [jax.ShapeDtypeStruct((s_len, d), BF16)] * 4,
        in_specs=[
            pl.BlockSpec((tm, d), lambda i: (i, mf_blk)),
            pl.BlockSpec((tm, d), lambda i: (i, mf_blk + 1)),
            row,
            row,
            pl.BlockSpec((d, d), lambda i: (0, 0), pipeline_mode=pl.Buffered(1)),
            row,
            pl.BlockSpec((1, d), lambda i: (0, 2)),
            vec,
            vec,
            row,
        ],
        out_specs=[row, row, row, pl.BlockSpec((8, d), lambda i: (0, 0))] + [row] * 4,
        compiler_params=_cp("arbitrary"),
    )(gates, gates, y_fox, y_swa, w_out, x, ada, ln_g, ln_b, target)


def _branch_bwd(dy, w_b, o, gates, g_blk, name, n_heads):
    s_len, d = dy.shape
    wd = w_b.shape[0]
    tm = min(1024, s_len)

    def body(dy_ref, w_ref, o_ref, g_ref, do_ref, dg_ref, *rest):
        da = lax.dot_general(dy_ref[...], w_ref[...], _NT, preferred_element_type=F32)
        g = g_ref[...].astype(F32)
        sg = _sigmoid(g)
        do = da * (g * sg)
        do_ref[...] = do.astype(BF16)
        o = o_ref[...]
        dg_ref[...] = (da * o * (sg * (1.0 + g * (1.0 - sg)))).astype(BF16)
        if n_heads:
            prod = do.astype(BF16).astype(F32) * o
            lane = lax.broadcasted_iota(jnp.int32, (1, 128), 1)
            delta = jnp.zeros((tm, 128), F32)
            for h in range(n_heads):
                dh = jnp.sum(prod[:, h * 128 : (h + 1) * 128], axis=1, keepdims=True)
                delta = delta + jnp.where(lane == h, dh, 0.0)
            rest[0][...] = delta

    out_shape = [jax.ShapeDtypeStruct((s_len, wd), BF16), jax.ShapeDtypeStruct((s_len, wd), BF16)]
    out_specs = [pl.BlockSpec((tm, wd), lambda i: (i, 0))] * 2
    if n_heads:
        out_shape.append(jax.ShapeDtypeStruct((s_len, 128), F32))
        out_specs.append(pl.BlockSpec((tm, 128), lambda i: (i, 0)))
    return _pcall(
        body,
        name=name,
        grid=(s_len // tm,),
        out_shape=out_shape,
        in_specs=[
            pl.BlockSpec((tm, d), lambda i: (i, 0)),
            pl.BlockSpec((wd, d), lambda i: (0, 0)),
            pl.BlockSpec((tm, wd), lambda i: (i, 0)),
            pl.BlockSpec((tm, wd), lambda i: (i, g_blk)),
        ],
        out_specs=out_specs,
        compiler_params=_cp("parallel"),
    )(dy, w_b, o, gates)


def _in_bwd(dproj, w_in_t, x, ada, dza, ride):
    s_len, d = x.shape
    k_tot = dproj.shape[1]
    tm, tk, dn = min(512, s_len), k_tot // 4, d // 2
    ni, nk = s_len // tm, k_tot // tk
    n = len(ride)

    def body(dp_ref, w_ref, x_ref, sc_ref, dza_ref, *rest):
        ins, (gx_ref, red_ref), outs = rest[:n], rest[n : n + 2], rest[n + 2 : 2 * n + 2]
        sems, acc_s = rest[2 * n + 2 : 2 * n + 5], rest[2 * n + 5]
        i, nh, kk = pl.program_id(0), pl.program_id(1), pl.program_id(2)

        @pl.when((i == 0) & (nh == 0) & (kk == 0))
        def _():
            _rider_start("exchange", ins, outs, *sems)

        @pl.when((i == ni - 1) & (nh == 1) & (kk == nk - 1))
        def _():
            _rider_wait("exchange", ins, outs, *sems)

        part = jnp.dot(dp_ref[...], w_ref[...], preferred_element_type=F32)
        half = pl.ds(pl.multiple_of(nh * dn, dn), dn)

        @pl.when(kk == 0)
        def _():
            acc_s[:, half] = part

        @pl.when(kk > 0)
        def _():
            acc_s[:, half] += part

        @pl.when((nh == 1) & (kk == nk - 1))
        def _():
            dh = acc_s[...]
            xv = x_ref[...]
            mu = jnp.mean(xv, axis=-1, keepdims=True)
            xc = xv - mu
            var = jnp.mean(xc * xc, axis=-1, keepdims=True)
            rstd = lax.rsqrt(var + LN_EPS)
            xhat = xc * rstd
            dxhat = dh * (1.0 + sc_ref[...])
            dx = rstd * (dxhat - jnp.mean(dxhat, axis=-1, keepdims=True) - xhat * jnp.mean(dxhat * xhat, axis=-1, keepdims=True))
            gx_ref[...] = dza_ref[...] + dx
            part_r = jnp.concatenate(
                [jnp.sum(dh, axis=0, keepdims=True), jnp.sum(dh * xhat, axis=0, keepdims=True), jnp.zeros((6, d), F32)], axis=0
            )

            @pl.when(i == 0)
            def _():
                red_ref[...] = part_r

            @pl.when(i > 0)
            def _():
                red_ref[...] += part_r

    row = pl.BlockSpec((tm, d), lambda i, nh, kk: (i, 0))
    hbm = pl.BlockSpec(memory_space=pltpu.HBM)
    return _pcall(
        body,
        name="in_bwd",
        grid=(ni, 2, nk),
        out_shape=[jax.ShapeDtypeStruct((s_len, d), F32), jax.ShapeDtypeStruct((8, d), F32)]
        + [jax.ShapeDtypeStruct(r.shape, r.dtype) for r in ride],
        in_specs=[
            pl.BlockSpec((tm, tk), lambda i, nh, kk: (i, kk)),
            pl.BlockSpec((tk, dn), lambda i, nh, kk: (kk, nh)),
            row,
            pl.BlockSpec((1, d), lambda i, nh, kk: (0, 1)),
            row,
        ]
        + [hbm] * n,
        out_specs=[row, pl.BlockSpec((8, d), lambda i, nh, kk: (0, 0))] + [hbm] * n,
        scratch_shapes=_rider_scratch(n) + [pltpu.VMEM((tm, d), F32)],
        compiler_params=_cp("arbitrary", "arbitrary", "arbitrary"),
    )(dproj, w_in_t, x, ada, dza, *ride)


def _pad_lanes(v, n):
    return jnp.pad(v, ((0, 0), (0, n - v.shape[1])))


def kernel(x, c, w_ada, b_ada, w_in, b_f, attn_sinks, w_br_fox, w_br_swa, w_out, ln_g, ln_b, loss_target, m_w_ada, m_b_ada, m_w_in, m_b_f, m_attn_sinks, m_w_br_fox, m_w_br_swa, m_w_out, m_ln_g, m_ln_b, v_w_ada, v_b_ada, v_w_in, v_b_f, v_attn_sinks, v_w_br_fox, v_w_br_swa, v_w_out, v_ln_g, v_ln_b):
    x2, tgt = x[0], loss_target[0]
    s_len, d = x2.shape
    me = 4 * lax.axis_index("x") + 2 * lax.axis_index("y") + lax.axis_index("c")
    off_ms = OFF_MF + d
    in_pad = off_ms + d
    c_ada = w_ada.shape[2]
    c_in = w_in.shape[2]
    c_br = w_br_fox.shape[2]

    w_in_full = _all_gather(w_in[0].T.astype(BF16), "ag_w_in", pltpu.HBM).reshape(N_DEV * c_in, d)
    w_in_pad = jnp.concatenate(
        [w_in_full[:REAL_FLOG_END], jnp.zeros((FLOG_PAD - N_FLOG, d), BF16), w_in_full[REAL_FLOG_END:]], axis=0
    )
    k_cut = REAL_FLOG_END // c_in

    c_all = _gather_rows(c, "ag_c")
    b_cols = lax.dynamic_slice(b_ada, (0, me * c_ada), (1, c_ada))
    ada_cols = _ada_fwd(c_all, w_ada[0], b_cols)
    ada_g = _all_gather(ada_cols, "ag_ada", pltpu.VMEM)
    ada = lax.dynamic_index_in_dim(ada_g, me, axis=1, keepdims=False).reshape(1, N_DEV * c_ada)

    h = _ln_mod(x2, ada)
    qkv_fox = _mm_cols(h, w_in_pad, OFF_FQ, 3 * FOX_W, BF16, "proj_fox")
    flog = _mm_cols(h, w_in_pad, OFF_FLOG, 128, F32, "proj_flog")
    qkv_swa = _mm_cols(h, w_in_pad, OFF_SQ, SWA_W + 2 * SWA_KVW, BF16, "proj_swa")
    gates, w_bf, w_bs, w_o = _mm_cols(
        h, w_in_pad, OFF_GF, in_pad - OFF_GF, BF16, "proj_gates",
        ride=(w_br_fox[0].astype(BF16), w_br_swa[0].astype(BF16), w_out[0].astype(BF16)),
    )
    w_bf = w_bf.reshape(N_DEV, FOX_W, c_br).transpose(1, 0, 2).reshape(FOX_W, d)
    w_bs = w_bs.reshape(N_DEV, SWA_W, c_br).transpose(1, 0, 2).reshape(SWA_W, d)
    w_o = w_o.reshape(d, d)
    mf_blk = (OFF_MF - OFF_GF) // d

    flog_t = flog[:, :N_FLOG].T
    bf_col = b_f.reshape(FOX_H, 1)
    cum = _fox_cum(flog_t, bf_col)
    cum_row = cum.reshape(FOX_H, 1, s_len)
    o_fox, lse = _fox_fwd(qkv_fox, cum_row)
    sinks = attn_sinks.reshape(SWA_HQ)
    o_swa = _swa_fwd(qkv_swa, sinks)

    y_fox, a_fox = _branch_fwd(o_fox, gates, 0, w_bf, "branch_fox")
    y_swa, a_swa = _branch_fwd(o_swa, gates, 1, w_bs, "branch_swa")
    merged, dza, dsub, red, dmf, dms, dy_fox, dy_swa = _out_stage(gates, mf_blk, y_fox, y_swa, w_o, x2, ada, ln_g, ln_b, tgt)
    loss = lax.psum(0.5 * red[4, 0] / d, ("x", "y", "c"))

    do_fox, dg_fox, delta = _branch_bwd(dy_fox, w_bf, o_fox, gates, 0, "branch_fox_bwd", FOX_H)
    do_swa, dg_swa = _branch_bwd(dy_swa, w_bs, o_swa, gates, 1, "branch_swa_bwd", 0)
    delta_row = delta[:, :FOX_H].T.reshape(FOX_H, 1, s_len)
    dq_f, dk_f, dv_f, dcol, drow = _fox_bwd(
        qkv_fox, cum.reshape(FOX_H, s_len, 1), lse.reshape(FOX_H, 1, s_len), delta_row, do_fox
    )
    dflog_t, dbf = _fox_gate_bwd(drow.reshape(FOX_H, s_len), dcol.reshape(FOX_H, s_len), flog_t, bf_col)
    dq_s, dk_s, dv_s, dsink = _swa_bwd(qkv_swa, sinks, do_swa)
    dflog = _pad_lanes(dflog_t.T, FLOG_PAD).astype(BF16)
    dproj = jnp.concatenate([dq_f, dk_f, dv_f, dflog, dq_s, dk_s, dv_s, dg_fox, dg_swa, dmf, dms], axis=1)
    g_w_bf = _mm_tn(a_fox, dy_fox, "grad_w_br_fox")
    g_w_bs = _mm_tn(a_swa, dy_swa, "grad_w_br_swa")
    g_w_o = _mm_tn(merged, dsub, "grad_w_out")
    g_w_in, r_bf, r_bs, r_o = _mm_tn(
        dproj, h, "grad_w_in",
        ride=(
            g_w_bf.reshape(FOX_W, N_DEV, c_br).transpose(1, 0, 2),
            g_w_bs.reshape(SWA_W, N_DEV, c_br).transpose(1, 0, 2),
            g_w_o.reshape(N_DEV, d // N_DEV, d),
        ),
    )
    pad = FLOG_PAD - N_FLOG
    g_blocks = jnp.stack(
        [g_w_in[k * c_in : (k + 1) * c_in] for k in range(k_cut)]
        + [jnp.concatenate([g_w_in[k_cut * c_in : REAL_FLOG_END], g_w_in[OFF_SQ : (k_cut + 1) * c_in + pad]], axis=0)]
        + [g_w_in[k * c_in + pad : (k + 1) * c_in + pad] for k in range(k_cut + 1, N_DEV)]
    )

    grad_x, red2, r_in = _in_bwd(dproj, w_in_pad, x2, ada, dza, ride=(g_blocks,))
    out_w_in = _sum_adam_t(r_in, w_in[0].T, m_w_in[0].T, v_w_in[0].T, "adam_w_in")
    out_w_in = [o.T for o in out_w_in]
    out_w_bf = _sum_adam(r_bf, w_br_fox[0], m_w_br_fox[0], v_w_br_fox[0], "adam_w_br_fox")
    out_w_bs = _sum_adam(r_bs, w_br_swa[0], m_w_br_swa[0], v_w_br_swa[0], "adam_w_br_swa")
    out_w_o = _sum_adam(r_o, w_out[0], m_w_out[0], v_w_out[0], "adam_w_out")

    packed = jnp.concatenate([red2[0:1], red2[1:2], red[0:1], _pad_lanes(dbf[:, 0].reshape(1, FOX_H), 128), dsink, red[1:2], red[2:3]], axis=1)
    gathered = _gather_rows(packed, "ag_small")
    pack = lambda a, b, cc, dd, e: jnp.concatenate([a, _pad_lanes(b, 128), _pad_lanes(cc, 128), dd, e], axis=1)
    small = _small_adam(
        gathered,
        pack(b_ada, b_f, attn_sinks, ln_g, ln_b),
        pack(m_b_ada, m_b_f, m_attn_sinks, m_ln_g, m_ln_b),
        pack(v_b_ada, v_b_f, v_attn_sinks, v_ln_g, v_ln_b),
    )
    dada_cols = lax.dynamic_slice(gathered, (0, me * c_ada), (N_DEV, c_ada))
    out_w_ada = _wada_adam(c_all.T, dada_cols, w_ada[0], m_w_ada[0], v_w_ada[0])

    o1, o2, o3 = 3 * d, 3 * d + 128, 3 * d + 256

    def unpack(p):
        return p[:, :o1], p[:, o1 : o1 + FOX_H], p[:, o2 : o2 + SWA_HQ], p[:, o3 : o3 + d], p[:, o3 + d : o3 + 2 * d]

    kinds = []
    for k in range(4):
        b_ada_k, b_f_k, sinks_k, ln_g_k, ln_b_k = unpack(small[k])
        kinds.append(
            [out_w_ada[k][None], b_ada_k, out_w_in[k][None], b_f_k, sinks_k, out_w_bf[k][None], out_w_bs[k][None], out_w_o[k][None], ln_g_k, ln_b_k]
        )
    return (loss, grad_x[None], *kinds[0], *kinds[1], *kinds[2], *kinds[3])
```

```python
import numpy as np
import jax
import jax.numpy as jnp
from jax import lax
from jax.experimental import pallas as pl
from jax.experimental.pallas import tpu as pltpu

F32 = jnp.float32
BF16 = jnp.bfloat16
N_DEV = 8
MESH = pl.DeviceIdType.MESH

FOX_H, FOX_DH, FOX_W = 8, 128, 1024
SWA_HQ, SWA_HKV, SWA_DH, SWA_G = 16, 4, 64, 4
SWA_W, SWA_KVW, WINDOW = 1024, 256, 128
LN_EPS = 1e-5
NEG = -1e30
DEPTH = 1
ALPHA = (2.0 * DEPTH) ** 0.25
FOX_SCALE = FOX_DH ** -0.5
SWA_SCALE = SWA_DH ** -0.5
LOG2E = 1.4426950408889634
SLOPES = [2.0 ** (-8.0 * (h + 1.0) / SWA_HQ) for h in range(SWA_HQ)]

ADAM_LR, ADAM_B1, ADAM_B2, ADAM_EPS, ADAM_WD, ADAM_STEP = 0.001, 0.9, 0.999, 1e-08, 0.01, 10

N_FLOG = 8
FLOG_PAD = 512
OFF_FQ, OFF_FK, OFF_FV, OFF_FLOG = 0, 1024, 2048, 3072
OFF_SQ = OFF_FLOG + FLOG_PAD
OFF_SK = OFF_SQ + SWA_W
OFF_SV = OFF_SK + SWA_KVW
OFF_GF = OFF_SV + SWA_KVW
OFF_GS = OFF_GF + FOX_W
OFF_MF = OFF_GS + SWA_W
REAL_FLOG_END = OFF_FLOG + N_FLOG

ATT_BLK = 512
VMEM_LIMIT = 58 * 1024 * 1024


def _pcall(body, **kw):
    return pl.pallas_call(body, **kw)


def _cp(*sem):
    return pltpu.CompilerParams(dimension_semantics=sem, vmem_limit_bytes=VMEM_LIMIT)


def _sigmoid(x):
    return 0.5 * jnp.tanh(0.5 * x) + 0.5


def _all_gather(x, name, space):
    m_per, n = x.shape

    def body(x_ref, out_ref, send_sems, recv_sems, local_sem):
        mx, my, mc = lax.axis_index("x"), lax.axis_index("y"), lax.axis_index("c")
        me, sibling = (mx, my, mc), (mx, my, 1 - mc)
        xn, yn, dg = (1 - mx, my), (mx, 1 - my), (1 - mx, 1 - my)
        south = mc == 0
        src_chip = (jnp.where(south, 1 - mx, mx), jnp.where(south, my, 1 - my))
        dst_chip = (jnp.where(south, mx, 1 - mx), jnp.where(south, 1 - my, my))

        def rows(px, py, pc):
            return out_ref.at[4 * px + 2 * py + pc]

        def copy(k, block, to, src=None):
            return pltpu.make_async_remote_copy(
                src_ref=rows(*block) if src is None else src,
                dst_ref=rows(*block),
                send_sem=send_sems.at[k],
                recv_sem=recv_sems.at[k],
                device_id=to,
                device_id_type=MESH,
            )

        mine = pltpu.make_async_copy(x_ref, rows(*me), local_sem)
        mine.start()
        first = [copy(0, me, sibling, src=x_ref), copy(1, me, (*xn, mc), src=x_ref), copy(2, me, (*yn, mc), src=x_ref)]
        for cp in first:
            cp.start()
        copy(1, (*xn, mc), me).wait_recv()
        copy(2, (*yn, mc), me).wait_recv()
        later = [copy(3, (*src_chip, mc), (*dst_chip, mc)), copy(4, (*xn, mc), sibling), copy(5, (*yn, mc), sibling)]
        for cp in later:
            cp.start()
        copy(3, (*dg, mc), me).wait_recv()
        last = copy(6, (*dg, mc), sibling)
        last.start()
        copy(0, sibling, me).wait_recv()
        for k, chip in ((4, xn), (5, yn), (6, dg)):
            copy(k, (*chip, 1 - mc), me).wait_recv()
        for cp in first + later + [last]:
            cp.wait_send()
        mine.wait()

    return _pcall(
        body,
        name=name,
        out_shape=jax.ShapeDtypeStruct((N_DEV, m_per, n), x.dtype),
        in_specs=[pl.BlockSpec(memory_space=space)],
        out_specs=pl.BlockSpec(memory_space=space),
        scratch_shapes=[pltpu.SemaphoreType.DMA((7,)), pltpu.SemaphoreType.DMA((7,)), pltpu.SemaphoreType.DMA],
    )(x)


def _peer(d, mx, my, mc):
    return (1 - mx if (d >> 2) & 1 else mx, 1 - my if (d >> 1) & 1 else my, 1 - mc if d & 1 else mc)


def _rider_copies(kind, ins, outs, send_sems, recv_sems, local_sems):
    mx, my, mc = lax.axis_index("x"), lax.axis_index("y"), lax.axis_index("c")
    me = 4 * mx + 2 * my + mc
    remote, local = [], []
    for a in range(len(ins)):
        if kind == "gather":
            m_per = ins[a].shape[0]
            mine = outs[a].at[pl.ds(me * m_per, m_per), :]
            local.append(pltpu.make_async_copy(ins[a], mine, local_sems.at[a]))
        else:
            local.append(pltpu.make_async_copy(ins[a].at[me], outs[a].at[0], local_sems.at[a]))
        for d in range(1, N_DEV):
            px, py, pc = _peer(d, mx, my, mc)
            if kind == "gather":
                src, dst = ins[a], mine
            else:
                src, dst = ins[a].at[4 * px + 2 * py + pc], outs[a].at[d]
            remote.append(
                pltpu.make_async_remote_copy(
                    src_ref=src,
                    dst_ref=dst,
                    send_sem=send_sems.at[a * 7 + d - 1],
                    recv_sem=recv_sems.at[a * 7 + d - 1],
                    device_id=(px, py, pc),
                    device_id_type=MESH,
                )
            )
    return remote, local


def _rider_start(*args):
    remote, local = _rider_copies(*args)
    for cp in local + remote:
        cp.start()


def _rider_wait(*args):
    remote, local = _rider_copies(*args)
    for cp in remote:
        cp.wait_recv()
    for cp in remote:
        cp.wait_send()
    for cp in local:
        cp.wait()


def _rider_scratch(n):
    return [pltpu.SemaphoreType.DMA((7 * n,)), pltpu.SemaphoreType.DMA((7 * n,)), pltpu.SemaphoreType.DMA((n,))]


def _gather_rows(v, name):
    n = v.shape[1]
    return _all_gather(jnp.broadcast_to(v, (8, n)), name, pltpu.VMEM)[:, 0, :]


def _adamw(w, g, m, v):
    m = ADAM_B1 * m + (1.0 - ADAM_B1) * g
    v = ADAM_B2 * v + (1.0 - ADAM_B2) * (g * g)
    m_hat = m / (1.0 - ADAM_B1**ADAM_STEP)
    v_hat = v / (1.0 - ADAM_B2**ADAM_STEP)
    delta = -ADAM_LR * (m_hat / (jnp.sqrt(v_hat) + ADAM_EPS) + ADAM_WD * w)
    return delta, m, v


def _sum_adam(recv, w, m, v, name):
    _, r_tot, c = recv.shape
    c_pad = -(-c // 128) * 128
    tr = r_tot
    while 8 * tr * c_pad * 4 > 6 * 1024 * 1024 and tr % 32 == 0:
        tr //= 2

    def body(r_ref, w_ref, m_ref, v_ref, g_ref, d_ref, nm_ref, nv_ref):
        g = r_ref[0].astype(F32)
        for k in range(1, N_DEV):
            g = g + r_ref[k].astype(F32)
        d, nm, nv = _adamw(w_ref[...], g, m_ref[...], v_ref[...])
        g_ref[...] = g
        d_ref[...] = d
        nm_ref[...] = nm
        nv_ref[...] = nv

    blk = pl.BlockSpec((tr, c), lambda i: (i, 0))
    return _pcall(
        body,
        name=name,
        grid=(r_tot // tr,),
        out_shape=[jax.ShapeDtypeStruct((r_tot, c), F32)] * 4,
        in_specs=[pl.BlockSpec((N_DEV, tr, c), lambda i: (0, i, 0)), blk, blk, blk],
        out_specs=[blk] * 4,
        compiler_params=_cp("parallel"),
    )(recv, w, m, v)


def _sum_adam_t(recv, w, m, v, name):
    _, c, r_tot = recv.shape
    tr = min(256, r_tot)

    def body(r_ref, w_ref, m_ref, v_ref, g_ref, d_ref, nm_ref, nv_ref):
        g = r_ref[0].astype(F32)
        for k in range(1, N_DEV):
            g = g + r_ref[k].astype(F32)
        d, nm, nv = _adamw(w_ref[...], g, m_ref[...], v_ref[...])
        g_ref[...] = g
        d_ref[...] = d
        nm_ref[...] = nm
        nv_ref[...] = nv

    blk = pl.BlockSpec((c, tr), lambda i: (0, i))
    return _pcall(
        body,
        name=name,
        grid=(r_tot // tr,),
        out_shape=[jax.ShapeDtypeStruct((c, r_tot), F32)] * 4,
        in_specs=[pl.BlockSpec((N_DEV, c, tr), lambda i: (0, 0, i)), blk, blk, blk],
        out_specs=[blk] * 4,
        compiler_params=_cp("parallel"),
    )(recv, w, m, v)


def _wada_adam(c_t, dada_cols, w, m, v):
    d_model, c = w.shape
    tr = min(256, d_model)

    def body(ct_ref, da_ref, w_ref, m_ref, v_ref, g_ref, d_ref, nm_ref, nv_ref):
        g = jnp.dot(ct_ref[...].astype(BF16), da_ref[...].astype(BF16), preferred_element_type=F32)
        d, nm, nv = _adamw(w_ref[...], g, m_ref[...], v_ref[...])
        g_ref[...] = g
        d_ref[...] = d
        nm_ref[...] = nm
        nv_ref[...] = nv

    blk = pl.BlockSpec((tr, c), lambda i: (i, 0))
    return _pcall(
        body,
        name="wada_adam",
        grid=(d_model // tr,),
        out_shape=[jax.ShapeDtypeStruct((d_model, c), F32)] * 4,
        in_specs=[pl.BlockSpec((tr, N_DEV), lambda i: (i, 0)), pl.BlockSpec((N_DEV, c), lambda i: (0, 0)), blk, blk, blk],
        out_specs=[blk] * 4,
        compiler_params=_cp("parallel"),
    )(c_t, dada_cols, w, m, v)


def _small_adam(gathered, w, m, v):
    p = w.shape[1]

    def body(a_ref, w_ref, m_ref, v_ref, g_ref, d_ref, nm_ref, nv_ref):
        g = a_ref[0:1, :]
        for k in range(1, N_DEV):
            g = g + a_ref[k : k + 1, :]
        d, nm, nv = _adamw(w_ref[...], g, m_ref[...], v_ref[...])
        g_ref[...] = g
        d_ref[...] = d
        nm_ref[...] = nm
        nv_ref[...] = nv

    return _pcall(
        body,
        name="small_adam",
        out_shape=[jax.ShapeDtypeStruct((1, p), F32)] * 4,
    )(gathered, w, m, v)


def _ada_fwd(c_all, w_ada, b_cols):
    c = w_ada.shape[1]

    def body(c_ref, w_ref, b_ref, o_ref):
        o_ref[...] = jnp.dot(c_ref[...].astype(BF16), w_ref[...].astype(BF16), preferred_element_type=F32) + b_ref[...]

    return _pcall(
        body,
        name="ada_fwd",
        out_shape=jax.ShapeDtypeStruct((N_DEV, c), F32),
        compiler_params=_cp(),
    )(c_all, w_ada, b_cols)


def _ln_mod(x, ada):
    s_len, d = x.shape
    tm = min(512, s_len)

    def body(x_ref, sh_ref, sc_ref, h_ref):
        xv = x_ref[...]
        mu = jnp.mean(xv, axis=-1, keepdims=True)
        xc = xv - mu
        var = jnp.mean(xc * xc, axis=-1, keepdims=True)
        xhat = xc * lax.rsqrt(var + LN_EPS)
        h_ref[...] = (xhat * (1.0 + sc_ref[...]) + sh_ref[...]).astype(BF16)

    return _pcall(
        body,
        name="ln_mod",
        grid=(s_len // tm,),
        out_shape=jax.ShapeDtypeStruct((s_len, d), BF16),
        in_specs=[
            pl.BlockSpec((tm, d), lambda i: (i, 0)),
            pl.BlockSpec((1, d), lambda i: (0, 0)),
            pl.BlockSpec((1, d), lambda i: (0, 1)),
        ],
        out_specs=pl.BlockSpec((tm, d), lambda i: (i, 0)),
        compiler_params=_cp("parallel"),
    )(x, ada, ada)


def _mm_cols(a, b, col_off, n_cols, out_dtype, name, ride=()):
    m, k = a.shape
    tm = min(1024, m)
    tn = next(t for t in (1024, 512, 128) if n_cols % t == 0 and col_off % t == 0)
    off = col_off // tn
    ni, nj = m // tm, n_cols // tn
    n = len(ride)

    def body(a_ref, b_ref, *rest):
        ins, o_ref, outs, sems = rest[:n], rest[n], rest[n + 1 : 2 * n + 1], rest[2 * n + 1 :]
        i, j = pl.program_id(0), pl.program_id(1)
        if n:

            @pl.when((i == 0) & (j == 0))
            def _():
                _rider_start("gather", ins, outs, *sems)

        o_ref[...] = lax.dot_general(a_ref[...], b_ref[...], _NT, preferred_element_type=F32).astype(out_dtype)
        if n:

            @pl.when((i == ni - 1) & (j == nj - 1))
            def _():
                _rider_wait("gather", ins, outs, *sems)

    hbm = pl.BlockSpec(memory_space=pltpu.HBM)
    out = _pcall(
        body,
        name=name,
        grid=(ni, nj),
        out_shape=[jax.ShapeDtypeStruct((m, n_cols), out_dtype)]
        + [jax.ShapeDtypeStruct((N_DEV * r.shape[0], r.shape[1]), r.dtype) for r in ride],
        in_specs=[pl.BlockSpec((tm, k), lambda i, j: (i, 0)), pl.BlockSpec((tn, k), lambda i, j: (off + j, 0))] + [hbm] * n,
        out_specs=[pl.BlockSpec((tm, tn), lambda i, j: (i, j))] + [hbm] * n,
        scratch_shapes=_rider_scratch(n) if n else [],
        compiler_params=_cp("arbitrary", "arbitrary") if n else _cp("parallel", "parallel"),
    )(a, b, *ride)
    return out if n else out[0]


def _mm_tn(a, b, name, ride=()):
    s_len, m = a.shape
    n = b.shape[1]
    tm, tn, ts = min(1024, m), min(2048, n), min(2048, s_len)
    ni, nj, ns = m // tm, n // tn, s_len // ts
    nr = len(ride)

    def body(a_ref, b_ref, *rest):
        ins, o_ref, outs = rest[:nr], rest[nr], rest[nr + 1 : 2 * nr + 1]
        sems, acc_s = rest[2 * nr + 1 : -1], rest[-1]
        i, j, kk = pl.program_id(0), pl.program_id(1), pl.program_id(2)
        if nr:

            @pl.when((i == 0) & (j == 0) & (kk == 0))
            def _():
                _rider_start("exchange", ins, outs, *sems)

            @pl.when((i == ni - 1) & (j == nj - 1) & (kk == ns - 1))
            def _():
                _rider_wait("exchange", ins, outs, *sems)

        part = lax.dot_general(a_ref[...], b_ref[...], _TN, preferred_element_type=F32)

        @pl.when(kk == 0)
        def _():
            acc_s[...] = part

        @pl.when(kk > 0)
        def _():
            acc_s[...] += part

        @pl.when(kk == ns - 1)
        def _():
            o_ref[...] = acc_s[...].astype(BF16)

    hbm = pl.BlockSpec(memory_space=pltpu.HBM)
    out = _pcall(
        body,
        name=name,
        grid=(ni, nj, ns),
        out_shape=[jax.ShapeDtypeStruct((m, n), BF16)] + [jax.ShapeDtypeStruct(r.shape, r.dtype) for r in ride],
        in_specs=[pl.BlockSpec((ts, tm), lambda i, j, kk: (kk, i)), pl.BlockSpec((ts, tn), lambda i, j, kk: (kk, j))] + [hbm] * nr,
        out_specs=[pl.BlockSpec((tm, tn), lambda i, j, kk: (i, j))] + [hbm] * nr,
        scratch_shapes=(_rider_scratch(nr) if nr else []) + [pltpu.VMEM((tm, tn), F32)],
        compiler_params=_cp("arbitrary", "arbitrary", "arbitrary") if nr else _cp("parallel", "parallel", "arbitrary"),
    )(a, b, *ride)
    return out if nr else out[0]


def _split3(a):
    hi = a.astype(BF16)
    r1 = a - hi.astype(F32)
    mid = r1.astype(BF16)
    lo = (r1 - mid.astype(F32)).astype(BF16)
    return hi, mid, lo


def _dot_ones(a, tri):
    return sum(jnp.dot(t, tri, preferred_element_type=F32) for t in _split3(a))


def _log_sigmoid(x):
    return jnp.minimum(x, 0.0) - jnp.log1p(jnp.exp(-jnp.abs(x)))


def _fox_cum(flog_t, bf_col):
    s_len = flog_t.shape[1]

    def body(fl_ref, bf_ref, cum_ref):
        r = lax.broadcasted_iota(jnp.int32, (128, 128), 0)
        c = lax.broadcasted_iota(jnp.int32, (128, 128), 1)
        upper = (r <= c).astype(BF16)

        def step(t, carry):
            sl = pl.ds(pl.multiple_of(t * 128, 128), 128)
            lf = _log_sigmoid(fl_ref[:, sl] + bf_ref[...])
            cs = _dot_ones(lf, upper) + carry
            cum_ref[:, sl] = cs
            return cs[:, 127:128]

        lax.fori_loop(0, s_len // 128, step, jnp.zeros((FOX_H, 1), F32))

    return _pcall(body, name="fox_cum", out_shape=jax.ShapeDtypeStruct((FOX_H, s_len), F32))(flog_t, bf_col)


def _fox_gate_bwd(drow, dcol, flog_t, bf_col):
    s_len = flog_t.shape[1]
    n = s_len // 128

    def body(dr_ref, dc_ref, fl_ref, bf_ref, dfl_ref, dbf_ref):
        r = lax.broadcasted_iota(jnp.int32, (128, 128), 0)
        c = lax.broadcasted_iota(jnp.int32, (128, 128), 1)
        lower = (r >= c).astype(BF16)

        def step(t, carry):
            run, tot = carry
            sl = pl.ds(pl.multiple_of((n - 1 - t) * 128, 128), 128)
            rc = _dot_ones(dr_ref[:, sl] - dc_ref[:, sl], lower) + run
            dfl = rc * _sigmoid(-(fl_ref[:, sl] + bf_ref[...]))
            dfl_ref[:, sl] = dfl
            return rc[:, 0:1], tot + jnp.sum(dfl, axis=1, keepdims=True)

        zero = jnp.zeros((FOX_H, 1), F32)
        _, tot = lax.fori_loop(0, n, step, (zero, zero))
        dbf_ref[...] = jnp.broadcast_to(tot, (FOX_H, 128))

    return _pcall(
        body,
        name="fox_gate_bwd",
        out_shape=[jax.ShapeDtypeStruct((FOX_H, s_len), F32), jax.ShapeDtypeStruct((FOX_H, 128), F32)],
    )(drow, dcol, flog_t, bf_col)


def _diag_mask(blk, transposed=False):
    r = lax.broadcasted_iota(jnp.int32, (blk, blk), 0)
    c = lax.broadcasted_iota(jnp.int32, (blk, blk), 1)
    return c >= r if transposed else r >= c


_NT = (((1,), (1,)), ((), ()))
_TN = (((0,), (0,)), ((), ()))


def _fox_fwd(qkv, cum_row):
    s_len = qkv.shape[0]
    blk = min(ATT_BLK, s_len)
    nb = s_len // blk
    log2e = 1.4426950408889634

    def body(q_ref, k_ref, v_ref, c_ref, o_ref, lse_ref, mx_s, acc_s, u_s, v1_s):
        i = pl.program_id(1)

        @pl.when(i == 0)
        def _():
            v1_s[:, :FOX_DH] = v_ref[...]
            v1_s[:, FOX_DH:] = (lax.broadcasted_iota(jnp.int32, (s_len, FOX_DH), 1) == 0).astype(BF16)

        def key_cols(j, n):
            return pl.ds(pl.multiple_of(j * blk, blk), n * blk)

        def walk(tile):
            def four_pairs(t, carry):
                for u in range(4):
                    tile(8 * t + 2 * u, 2, False)
                return carry

            lax.fori_loop(0, i // 8, four_pairs, 0)

            @pl.when((i // 4) % 2 == 1)
            def _():
                tile(8 * (i // 8), 2, False)
                tile(8 * (i // 8) + 2, 2, False)

            @pl.when((i // 2) % 2 == 1)
            def _():
                tile(4 * (i // 4), 2, False)

            @pl.when(i % 2 == 1)
            def _():
                tile(i - 1, 1, False)

            tile(i, 1, True)

        def lane_max(j, n, masked):
            cols = key_cols(j, n)
            u = lax.dot_general(q_ref[...], k_ref[cols, :], _NT, preferred_element_type=F32) * (FOX_SCALE * log2e) - c_ref[:, cols] * log2e
            if masked:
                u = jnp.where(_diag_mask(blk), u, NEG)
            u_s[:, cols] = u
            part = u[:, 0:128]
            for t in range(1, n * blk // 128):
                part = jnp.maximum(part, u[:, t * 128 : (t + 1) * 128])
            mx_s[...] = jnp.maximum(mx_s[...], part)

        mx_s[...] = jnp.full(mx_s.shape, NEG, F32)
        walk(lane_max)
        m = jnp.max(mx_s[...], axis=1, keepdims=True)

        def weigh(j, n, masked):
            cols = key_cols(j, n)
            p = jnp.exp2(u_s[:, cols] - m)
            acc_s[...] += jnp.dot(p.astype(BF16), v1_s[cols, :], preferred_element_type=F32)

        acc_s[...] = jnp.zeros(acc_s.shape, F32)
        walk(weigh)
        l = acc_s[:, FOX_DH : FOX_DH + 1]
        o_ref[...] = acc_s[:, :FOX_DH] / l
        lse_ref[...] = m * (1.0 / log2e) + jnp.log(l)

    return _pcall(
        body,
        name="fox_fwd",
        grid=(FOX_H, nb),
        out_shape=[jax.ShapeDtypeStruct((s_len, FOX_W), F32), jax.ShapeDtypeStruct((FOX_H, s_len, 1), F32)],
        in_specs=[
            pl.BlockSpec((blk, FOX_DH), lambda h, i: (i, h)),
            pl.BlockSpec((s_len, FOX_DH), lambda h, i: (0, FOX_H + h)),
            pl.BlockSpec((s_len, FOX_DH), lambda h, i: (0, 2 * FOX_H + h)),
            pl.BlockSpec((None, 1, s_len), lambda h, i: (h, 0, 0)),
        ],
        out_specs=[
            pl.BlockSpec((blk, FOX_DH), lambda h, i: (i, h)),
            pl.BlockSpec((None, blk, 1), lambda h, i: (h, i, 0)),
        ],
        scratch_shapes=[
            pltpu.VMEM((blk, 128), F32),
            pltpu.VMEM((blk, 2 * FOX_DH), F32),
            pltpu.VMEM((blk, s_len), F32),
            pltpu.VMEM((s_len, 2 * FOX_DH), BF16),
        ],
        compiler_params=_cp("arbitrary", "arbitrary"),
    )(qkv, qkv, qkv, cum_row)


def _fox_bwd(qkv, cum_col, lse_row, delta_row, do):
    s_len = qkv.shape[0]
    blk = min(ATT_BLK, s_len)
    nb = s_len // blk

    def body(q_ref, k_ref, v_ref, c_ref, lse_ref, dl_ref, do_ref, dq_ref, dk_ref, dv_ref, dc_ref, dr_ref, dk_s, dv_s, dc_s, cb_s, dq_s):
        j = pl.program_id(1)

        @pl.when(j == 0)
        def _():
            dq_s[...] = jnp.zeros(dq_s.shape, F32)
            dr_ref[...] = jnp.zeros(dr_ref.shape, F32)

        dk_s[...] = jnp.zeros(dk_s.shape, F32)
        dv_s[...] = jnp.zeros(dv_s.shape, F32)
        dc_s[...] = jnp.zeros(dc_s.shape, F32)
        cb_s[...] = jnp.broadcast_to(c_ref[...] * LOG2E, cb_s.shape)

        def tile(i, n, diag):
            rows = pl.ds(pl.multiple_of(i * blk, blk), n * blk)
            q, dob = q_ref[rows, :], do_ref[rows, :]
            k, v = k_ref[...], v_ref[...]
            s_t = lax.dot_general(k, q, _NT, preferred_element_type=F32) * (FOX_SCALE * LOG2E) - cb_s[:, : n * blk]
            p_t = jnp.exp2(s_t - lse_ref[:, rows] * LOG2E)
            if diag:
                p_t = jnp.where(_diag_mask(blk, transposed=True), p_t, 0.0)
            dp_t = lax.dot_general(v, dob, _NT, preferred_element_type=F32)
            ds_t = p_t * (dp_t - dl_ref[:, rows])
            dsb = ds_t.astype(BF16)
            dv_s[...] += jnp.dot(p_t.astype(BF16), dob, preferred_element_type=F32)
            dk_s[...] += jnp.dot(dsb, q, preferred_element_type=F32)
            dq_c = lax.dot_general(dsb, k, _TN, preferred_element_type=F32)
            part = ds_t[:, 0:128]
            for t in range(1, n * blk // 128):
                part = part + ds_t[:, t * 128 : (t + 1) * 128]
            dc_s[...] += part
            dr_ref[:, rows] += jnp.sum(ds_t, axis=0, keepdims=True)
            if diag:
                dq_s[rows, :] = (dq_s[rows, :] + dq_c) * FOX_SCALE
            else:
                dq_s[rows, :] += dq_c

        tile(j, 1, True)
        below = nb - 1 - j
        b0, b1, b2 = below % 2, (below // 2) % 2, (below // 4) % 2

        @pl.when(b0 == 1)
        def _():
            tile(j + 1, 1, False)

        @pl.when(b1 == 1)
        def _():
            tile(j + 1 + b0, 2, False)

        @pl.when(b2 == 1)
        def _():
            tile(j + 1 + b0 + 2 * b1, 2, False)
            tile(j + 3 + b0 + 2 * b1, 2, False)

        first = j + 1 + b0 + 2 * b1 + 4 * b2

        def four_pairs(t, carry):
            for u in range(4):
                tile(first + 8 * t + 2 * u, 2, False)
            return carry

        lax.fori_loop(0, below // 8, four_pairs, 0)
        dk_ref[...] = (dk_s[...] * FOX_SCALE).astype(BF16)
        dv_ref[...] = dv_s[...].astype(BF16)
        dc_ref[...] = jnp.sum(dc_s[...], axis=1, keepdims=True)

        @pl.when(j == nb - 1)
        def _():
            dq_ref[...] = dq_s[...].astype(BF16)

    head = lambda h, j: (0, h)
    row = pl.BlockSpec((None, 1, s_len), lambda h, j: (h, 0, 0))
    return _pcall(
        body,
        name="fox_bwd",
        grid=(FOX_H, nb),
        out_shape=[
            jax.ShapeDtypeStruct((s_len, FOX_W), BF16),
            jax.ShapeDtypeStruct((s_len, FOX_W), BF16),
            jax.ShapeDtypeStruct((s_len, FOX_W), BF16),
            jax.ShapeDtypeStruct((FOX_H, s_len, 1), F32),
            jax.ShapeDtypeStruct((FOX_H, 1, s_len), F32),
        ],
        in_specs=[
            pl.BlockSpec((s_len, FOX_DH), head),
            pl.BlockSpec((blk, FOX_DH), lambda h, j: (j, FOX_H + h)),
            pl.BlockSpec((blk, FOX_DH), lambda h, j: (j, 2 * FOX_H + h)),
            pl.BlockSpec((None, blk, 1), lambda h, j: (h, j, 0)),
            row,
            row,
            pl.BlockSpec((s_len, FOX_DH), head),
        ],
        out_specs=[
            pl.BlockSpec((s_len, FOX_DH), head),
            pl.BlockSpec((blk, FOX_DH), lambda h, j: (j, h)),
            pl.BlockSpec((blk, FOX_DH), lambda h, j: (j, h)),
            pl.BlockSpec((None, blk, 1), lambda h, j: (h, j, 0)),
            row,
        ],
        scratch_shapes=[
            pltpu.VMEM((blk, FOX_DH), F32),
            pltpu.VMEM((blk, FOX_DH), F32),
            pltpu.VMEM((blk, 128), F32),
            pltpu.VMEM((blk, 2 * blk), F32),
            pltpu.VMEM((s_len, FOX_DH), F32),
        ],
        compiler_params=_cp("parallel", "arbitrary"),
    )(qkv, qkv, qkv, cum_col, lse_row, delta_row, do)


def _swa_bias():
    cols = SWA_G * WINDOW
    k = np.arange(2 * WINDOW)[:, None]
    q = np.arange(cols)[None, :]
    dist = (q % WINDOW) - k + WINDOW
    valid = (dist >= 0) & (dist < WINDOW)
    out = np.empty((2, SWA_HKV, 2 * WINDOW, cols), np.float32)
    for g in range(SWA_HKV):
        slope = np.array([SLOPES[g * SWA_G + t] for t in range(SWA_G)], np.float32)[q // WINDOW]
        bias = -(slope * dist.astype(np.float32))
        out[0, g] = np.where(valid & (k >= WINDOW), bias, np.float32(NEG))
        out[1, g] = np.where(valid, bias, np.float32(NEG))
    return jnp.asarray(out)


def _swa_group(i, q_ref, kk, sinks_ref, bias_ref, g):
    cols = SWA_G * WINDOW
    head = lax.broadcasted_iota(jnp.int32, (1, cols), 1) // WINDOW
    sink = jnp.zeros((1, cols), F32)
    for t in range(SWA_G):
        sink = jnp.where(head == t, sinks_ref[g * SWA_G + t], sink)
    q = jnp.concatenate([q_ref[:, (g * SWA_G + t) * SWA_DH : (g * SWA_G + t + 1) * SWA_DH] for t in range(SWA_G)], axis=0)
    k = kk[:, g * SWA_DH : (g + 1) * SWA_DH]
    s = lax.dot_general(k, q, _NT, preferred_element_type=F32) * SWA_SCALE + bias_ref[jnp.minimum(i, 1), g]
    m = jnp.maximum(jnp.max(s, axis=0, keepdims=True), sink)
    e = jnp.exp(s - m)
    e_sink = jnp.exp(sink - m)
    inv = 1.0 / (jnp.sum(e, axis=0, keepdims=True) + e_sink)
    return q, k, e * inv, e_sink * inv


def _swa_fwd(qkv, sinks):
    s_len = qkv.shape[0]
    nb = s_len // WINDOW
    bias_spec = pl.BlockSpec((2, SWA_HKV, 2 * WINDOW, SWA_G * WINDOW), lambda t: (0, 0, 0, 0))

    def body(q_ref, kp_ref, kc_ref, vp_ref, vc_ref, sinks_ref, bias_ref, o_ref):
        step = pl.program_id(0)
        kc, vc = kc_ref[...], vc_ref[...]
        for b in range(2):
            rows = pl.ds(b * WINDOW, WINDOW)
            if b == 0:
                kk = jnp.concatenate([kp_ref[...], kc[:WINDOW]], axis=0)
                vv = jnp.concatenate([vp_ref[...], vc[:WINDOW]], axis=0)
            else:
                kk, vv = kc, vc
            for g in range(SWA_HKV):
                _, _, p, _ = _swa_group(2 * step + b, q_ref.at[rows, :], kk, sinks_ref, bias_ref, g)
                o = lax.dot_general(p.astype(BF16), vv[:, g * SWA_DH : (g + 1) * SWA_DH], _TN, preferred_element_type=F32)
                for t in range(SWA_G):
                    h = g * SWA_G + t
                    o_ref[rows, h * SWA_DH : (h + 1) * SWA_DH] = o[t * WINDOW : (t + 1) * WINDOW, :]

    before = lambda t: jnp.maximum(2 * t - 1, 0)
    return _pcall(
        body,
        name="swa_fwd",
        grid=(nb // 2,),
        out_shape=jax.ShapeDtypeStruct((s_len, SWA_W), F32),
        in_specs=[
            pl.BlockSpec((2 * WINDOW, SWA_W), lambda t: (t, 0)),
            pl.BlockSpec((WINDOW, SWA_KVW), lambda t: (before(t), 4)),
            pl.BlockSpec((2 * WINDOW, SWA_KVW), lambda t: (t, 4)),
            pl.BlockSpec((WINDOW, SWA_KVW), lambda t: (before(t), 5)),
            pl.BlockSpec((2 * WINDOW, SWA_KVW), lambda t: (t, 5)),
            pl.BlockSpec(memory_space=pltpu.SMEM),
            bias_spec,
        ],
        out_specs=pl.BlockSpec((2 * WINDOW, SWA_W), lambda t: (t, 0)),
        compiler_params=_cp("parallel"),
    )(qkv, qkv, qkv, qkv, qkv, sinks, _swa_bias())


def _swa_bwd(qkv, sinks, do):
    s_len = qkv.shape[0]
    nb = s_len // WINDOW
    bias_spec = pl.BlockSpec((2, SWA_HKV, 2 * WINDOW, SWA_G * WINDOW), lambda t: (0, 0, 0, 0))

    def body(q_ref, kp_ref, kc_ref, vp_ref, vc_ref, sinks_ref, bias_ref, do_ref, dq_ref, dk_ref, dv_ref, dsink_ref, ck_s, cv_s, dkk_s, dvv_s, pk_s, pv_s):
        step = pl.program_id(0)
        i_top = nb - 1 - 2 * step

        @pl.when(step == 0)
        def _():
            ck_s[...] = jnp.zeros(ck_s.shape, F32)
            cv_s[...] = jnp.zeros(cv_s.shape, F32)
            dsink_ref[...] = jnp.zeros(dsink_ref.shape, F32)

        kc, vc = kc_ref[...], vc_ref[...]
        lane = lax.broadcasted_iota(jnp.int32, (1, 128), 1)

        def block(i, rows, kk, vv, dsink):
            q_rows, do_rows = q_ref.at[rows, :], do_ref.at[rows, :]
            for g in range(SWA_HKV):
                cols = slice(g * SWA_DH, (g + 1) * SWA_DH)
                q, k, p, p_sink = _swa_group(i, q_rows, kk, sinks_ref, bias_ref, g)
                dob = jnp.concatenate([do_rows[:, (g * SWA_G + t) * SWA_DH : (g * SWA_G + t + 1) * SWA_DH] for t in range(SWA_G)], axis=0)
                dp = lax.dot_general(vv[:, cols], dob, _NT, preferred_element_type=F32)
                delta = jnp.sum(p * dp, axis=0, keepdims=True)
                dsb = (p * (dp - delta)).astype(BF16)
                dq = (lax.dot_general(dsb, k, _TN, preferred_element_type=F32) * SWA_SCALE).astype(BF16)
                ps_d = p_sink * delta
                for t in range(SWA_G):
                    h = g * SWA_G + t
                    dq_ref[rows, h * SWA_DH : (h + 1) * SWA_DH] = dq[t * WINDOW : (t + 1) * WINDOW, :]
                    dsink = dsink + jnp.where(lane == h, -jnp.sum(ps_d[:, t * WINDOW : (t + 1) * WINDOW], axis=1, keepdims=True), 0.0)
                dkk_s[:, cols] = jnp.dot(dsb, q, preferred_element_type=F32) * SWA_SCALE
                dvv_s[:, cols] = jnp.dot(p.astype(BF16), dob, preferred_element_type=F32)
            return dsink

        bottom, top = pl.ds(0, WINDOW), pl.ds(WINDOW, WINDOW)
        dsink = block(i_top, top, kc, vc, jnp.zeros((1, 128), F32))
        dk_ref[top, :] = (dkk_s[WINDOW:, :] + ck_s[...]).astype(BF16)
        dv_ref[top, :] = (dvv_s[WINDOW:, :] + cv_s[...]).astype(BF16)
        pk_s[...] = dkk_s[:WINDOW, :]
        pv_s[...] = dvv_s[:WINDOW, :]
        kk = jnp.concatenate([kp_ref[...], kc[:WINDOW]], axis=0)
        vv = jnp.concatenate([vp_ref[...], vc[:WINDOW]], axis=0)
        dsink = block(i_top - 1, bottom, kk, vv, dsink)
        dk_ref[bottom, :] = (dkk_s[WINDOW:, :] + pk_s[...]).astype(BF16)
        dv_ref[bottom, :] = (dvv_s[WINDOW:, :] + pv_s[...]).astype(BF16)
        ck_s[...] = dkk_s[:WINDOW, :]
        cv_s[...] = dvv_s[:WINDOW, :]
        dsink_ref[...] += dsink

    pair = lambda t: (nb // 2 - 1 - t, 0)
    before = lambda t: jnp.maximum(nb - 3 - 2 * t, 0)
    return _pcall(
        body,
        name="swa_bwd",
        grid=(nb // 2,),
        out_shape=[
            jax.ShapeDtypeStruct((s_len, SWA_W), BF16),
            jax.ShapeDtypeStruct((s_len, SWA_KVW), BF16),
            jax.ShapeDtypeStruct((s_len, SWA_KVW), BF16),
            jax.ShapeDtypeStruct((1, 128), F32),
        ],
        in_specs=[
            pl.BlockSpec((2 * WINDOW, SWA_W), pair),
            pl.BlockSpec((WINDOW, SWA_KVW), lambda t: (before(t), 4)),
            pl.BlockSpec((2 * WINDOW, SWA_KVW), lambda t: (nb // 2 - 1 - t, 4)),
            pl.BlockSpec((WINDOW, SWA_KVW), lambda t: (before(t), 5)),
            pl.BlockSpec((2 * WINDOW, SWA_KVW), lambda t: (nb // 2 - 1 - t, 5)),
            pl.BlockSpec(memory_space=pltpu.SMEM),
            bias_spec,
            pl.BlockSpec((2 * WINDOW, SWA_W), pair),
        ],
        out_specs=[
            pl.BlockSpec((2 * WINDOW, SWA_W), pair),
            pl.BlockSpec((2 * WINDOW, SWA_KVW), pair),
            pl.BlockSpec((2 * WINDOW, SWA_KVW), pair),
            pl.BlockSpec((1, 128), lambda t: (0, 0)),
        ],
        scratch_shapes=[
            pltpu.VMEM((WINDOW, SWA_KVW), F32),
            pltpu.VMEM((WINDOW, SWA_KVW), F32),
            pltpu.VMEM((2 * WINDOW, SWA_KVW), F32),
            pltpu.VMEM((2 * WINDOW, SWA_KVW), F32),
            pltpu.VMEM((WINDOW, SWA_KVW), F32),
            pltpu.VMEM((WINDOW, SWA_KVW), F32),
        ],
        compiler_params=_cp("arbitrary"),
    )(qkv, qkv, qkv, qkv, qkv, sinks, _swa_bias(), do)


def _branch_fwd(o, gates, g_blk, w_b, name):
    s_len, wd = o.shape
    d = w_b.shape[1]
    tm = min(1024, s_len)

    def body(o_ref, g_ref, w_ref, y_ref, a_ref):
        g = g_ref[...].astype(F32)
        a = (o_ref[...] * (g * _sigmoid(g))).astype(BF16)
        a_ref[...] = a
        y_ref[...] = jnp.dot(a, w_ref[...], preferred_element_type=F32).astype(BF16)

    return _pcall(
        body,
        name=name,
        grid=(s_len // tm,),
        out_shape=[jax.ShapeDtypeStruct((s_len, d), BF16), jax.ShapeDtypeStruct((s_len, wd), BF16)],
        in_specs=[
            pl.BlockSpec((tm, wd), lambda i: (i, 0)),
            pl.BlockSpec((tm, wd), lambda i: (i, g_blk)),
            pl.BlockSpec((wd, d), lambda i: (0, 0)),
        ],
        out_specs=[pl.BlockSpec((tm, d), lambda i: (i, 0)), pl.BlockSpec((tm, wd), lambda i: (i, 0))],
        compiler_params=_cp("parallel"),
    )(o, gates, w_b)


def _out_stage(gates, mf_blk, y_fox, y_swa, w_out, x, ada, ln_g, ln_b, target):
    s_len, d = x.shape
    tm = min(256, s_len)
    n_steps = s_len // tm

    def body(mf_ref, ms_ref, yf_ref, ys_ref, w_ref, x_ref, gate_ref, lg_ref, lb_ref, t_ref, mg_ref, dza_ref, dsub_ref, red_ref, dmf_ref, dms_ref, dyf_ref, dys_ref):
        i = pl.program_id(0)
        sf, ss = _sigmoid(mf_ref[...].astype(F32)), _sigmoid(ms_ref[...].astype(F32))
        yf, ys = yf_ref[...].astype(F32), ys_ref[...].astype(F32)
        merged = sf * yf + ss * ys
        mb = merged.astype(BF16)
        mg_ref[...] = mb
        sub = jnp.dot(mb, w_ref[...], preferred_element_type=F32)
        gate = gate_ref[...]
        z = ALPHA * x_ref[...] + gate * sub
        mu = jnp.mean(z, axis=-1, keepdims=True)
        zc = z - mu
        var = jnp.mean(zc * zc, axis=-1, keepdims=True)
        rstd = lax.rsqrt(var + LN_EPS)
        zhat = zc * rstd
        err = zhat * lg_ref[...] + lb_ref[...] - t_ref[...]
        dout = err * (1.0 / d)
        dzhat = dout * lg_ref[...]
        dz = rstd * (dzhat - jnp.mean(dzhat, axis=-1, keepdims=True) - zhat * jnp.mean(dzhat * zhat, axis=-1, keepdims=True))
        dza_ref[...] = ALPHA * dz
        dsub = (gate * dz).astype(BF16)
        dsub_ref[...] = dsub
        dm = lax.dot_general(dsub, w_ref[...], _NT, preferred_element_type=F32)
        dmf_ref[...] = (dm * yf * (sf * (1.0 - sf))).astype(BF16)
        dms_ref[...] = (dm * ys * (ss * (1.0 - ss))).astype(BF16)
        dyf_ref[...] = (dm * sf).astype(BF16)
        dys_ref[...] = (dm * ss).astype(BF16)
        part = jnp.concatenate(
            [
                jnp.sum(dz * sub, axis=0, keepdims=True),
                jnp.sum(dout * zhat, axis=0, keepdims=True),
                jnp.sum(dout, axis=0, keepdims=True),
                jnp.sum(err * err, axis=0, keepdims=True),
                jnp.zeros((4, d), F32),
            ],
            axis=0,
        )

        @pl.when(i == 0)
        def _():
            red_ref[...] = part

        @pl.when(i > 0)
        def _():
            red_ref[...] += part

        @pl.when(i == n_steps - 1)
        def _():
            red_ref[4:5, :] = jnp.broadcast_to(jnp.sum(red_ref[3:4, :], axis=1, keepdims=True), (1, d))

    row = pl.BlockSpec((tm, d), lambda i: (i, 0))
    vec = pl.BlockSpec((1, d), lambda i: (0, 0))
    return _pcall(
        body,
        name="out_stage",
        grid=(n_steps,),
        out_shape=[
            jax.ShapeDtypeStruct((s_len, d), BF16),
            jax.ShapeDtypeStruct((s_len, d), F32),
            jax.ShapeDtypeStruct((s_len, d), BF16),
            jax.ShapeDtypeStruct((8, d), F32),
        ]
        + [jax.ShapeDtypeStruct((s_len, d), BF16)] * 4,
        in_specs=[
            pl.BlockSpec((tm, d), lambda i: (i, mf_blk)),
            pl.BlockSpec((tm, d), lambda i: (i, mf_blk + 1)),
            row,
            row,
            pl.BlockSpec((d, d), lambda i: (0, 0), pipeline_mode=pl.Buffered(1)),
            row,
            pl.BlockSpec((1, d), lambda i: (0, 2)),
            vec,
            vec,
            row,
        ],
        out_specs=[row, row, row, pl.BlockSpec((8, d), lambda i: (0, 0))] + [row] * 4,
        compiler_params=_cp("arbitrary"),
    )(gates, gates, y_fox, y_swa, w_out, x, ada, ln_g, ln_b, target)


def _branch_bwd(dy, w_b, o, gates, g_blk, name, n_heads):
    s_len, d = dy.shape
    wd = w_b.shape[0]
    tm = min(1024, s_len)

    def body(dy_ref, w_ref, o_ref, g_ref, do_ref, dg_ref, *rest):
        da = lax.dot_general(dy_ref[...], w_ref[...], _NT, preferred_element_type=F32)
        g = g_ref[...].astype(F32)
        sg = _sigmoid(g)
        do = da * (g * sg)
        do_ref[...] = do.astype(BF16)
        o = o_ref[...]
        dg_ref[...] = (da * o * (sg * (1.0 + g * (1.0 - sg)))).astype(BF16)
        if n_heads:
            prod = do.astype(BF16).astype(F32) * o
            lane = lax.broadcasted_iota(jnp.int32, (1, 128), 1)
            delta = jnp.zeros((tm, 128), F32)
            for h in range(n_heads):
                dh = jnp.sum(prod[:, h * 128 : (h + 1) * 128], axis=1, keepdims=True)
                delta = delta + jnp.where(lane == h, dh, 0.0)
            rest[0][...] = delta

    out_shape = [jax.ShapeDtypeStruct((s_len, wd), BF16), jax.ShapeDtypeStruct((s_len, wd), BF16)]
    out_specs = [pl.BlockSpec((tm, wd), lambda i: (i, 0))] * 2
    if n_heads:
        out_shape.append(jax.ShapeDtypeStruct((s_len, 128), F32))
        out_specs.append(pl.BlockSpec((tm, 128), lambda i: (i, 0)))
    return _pcall(
        body,
        name=name,
        grid=(s_len // tm,),
        out_shape=out_shape,
        in_specs=[
            pl.BlockSpec((tm, d), lambda i: (i, 0)),
            pl.BlockSpec((wd, d), lambda i: (0, 0)),
            pl.BlockSpec((tm, wd), lambda i: (i, 0)),
            pl.BlockSpec((tm, wd), lambda i: (i, g_blk)),
        ],
        out_specs=out_specs,
        compiler_params=_cp("parallel"),
    )(dy, w_b, o, gates)


def _in_bwd(dproj, w_in_t, x, ada, dza, ride):
    s_len, d = x.shape
    k_tot = dproj.shape[1]
    tm, tk, dn = min(512, s_len), k_tot // 4, d // 2
    ni, nk = s_len // tm, k_tot // tk
    n = len(ride)

    def body(dp_ref, w_ref, x_ref, sc_ref, dza_ref, *rest):
        ins, (gx_ref, red_ref), outs = rest[:n], rest[n : n + 2], rest[n + 2 : 2 * n + 2]
        sems, acc_s = rest[2 * n + 2 : 2 * n + 5], rest[2 * n + 5]
        i, nh, kk = pl.program_id(0), pl.program_id(1), pl.program_id(2)

        @pl.when((i == 0) & (nh == 0) & (kk == 0))
        def _():
            _rider_start("exchange", ins, outs, *sems)

        @pl.when((i == ni - 1) & (nh == 1) & (kk == nk - 1))
        def _():
            _rider_wait("exchange", ins, outs, *sems)

        part = jnp.dot(dp_ref[...], w_ref[...], preferred_element_type=F32)
        half = pl.ds(pl.multiple_of(nh * dn, dn), dn)

        @pl.when(kk == 0)
        def _():
            acc_s[:, half] = part

        @pl.when(kk > 0)
        def _():
            acc_s[:, half] += part

        @pl.when((nh == 1) & (kk == nk - 1))
        def _():
            dh = acc_s[...]
            xv = x_ref[...]
            mu = jnp.mean(xv, axis=-1, keepdims=True)
            xc = xv - mu
            var = jnp.mean(xc * xc, axis=-1, keepdims=True)
            rstd = lax.rsqrt(var + LN_EPS)
            xhat = xc * rstd
            dxhat = dh * (1.0 + sc_ref[...])
            dx = rstd * (dxhat - jnp.mean(dxhat, axis=-1, keepdims=True) - xhat * jnp.mean(dxhat * xhat, axis=-1, keepdims=True))
            gx_ref[...] = dza_ref[...] + dx
            part_r = jnp.concatenate(
                [jnp.sum(dh, axis=0, keepdims=True), jnp.sum(dh * xhat, axis=0, keepdims=True), jnp.zeros((6, d), F32)], axis=0
            )

            @pl.when(i == 0)
            def _():
                red_ref[...] = part_r

            @pl.when(i > 0)
            def _():
                red_ref[...] += part_r

    row = pl.BlockSpec((tm, d), lambda i, nh, kk: (i, 0))
    hbm = pl.BlockSpec(memory_space=pltpu.HBM)
    return _pcall(
        body,
        name="in_bwd",
        grid=(ni, 2, nk),
        out_shape=[jax.ShapeDtypeStruct((s_len, d), F32), jax.ShapeDtypeStruct((8, d), F32)]
        + [jax.ShapeDtypeStruct(r.shape, r.dtype) for r in ride],
        in_specs=[
            pl.BlockSpec((tm, tk), lambda i, nh, kk: (i, kk)),
            pl.BlockSpec((tk, dn), lambda i, nh, kk: (kk, nh)),
            row,
            pl.BlockSpec((1, d), lambda i, nh, kk: (0, 1)),
            row,
        ]
        + [hbm] * n,
        out_specs=[row, pl.BlockSpec((8, d), lambda i, nh, kk: (0, 0))] + [hbm] * n,
        scratch_shapes=_rider_scratch(n) + [pltpu.VMEM((tm, d), F32)],
        compiler_params=_cp("arbitrary", "arbitrary", "arbitrary"),
    )(dproj, w_in_t, x, ada, dza, *ride)


def _pad_lanes(v, n):
    return jnp.pad(v, ((0, 0), (0, n - v.shape[1])))


def kernel(x, c, w_ada, b_ada, w_in, b_f, attn_sinks, w_br_fox, w_br_swa, w_out, ln_g, ln_b, loss_target, m_w_ada, m_b_ada, m_w_in, m_b_f, m_attn_sinks, m_w_br_fox, m_w_br_swa, m_w_out, m_ln_g, m_ln_b, v_w_ada, v_b_ada, v_w_in, v_b_f, v_attn_sinks, v_w_br_fox, v_w_br_swa, v_w_out, v_ln_g, v_ln_b):
    x2, tgt = x[0], loss_target[0]
    s_len, d = x2.shape
    me = 4 * lax.axis_index("x") + 2 * lax.axis_index("y") + lax.axis_index("c")
    off_ms = OFF_MF + d
    in_pad = off_ms + d
    c_ada = w_ada.shape[2]
    c_in = w_in.shape[2]
    c_br = w_br_fox.shape[2]

    w_in_full = _all_gather(w_in[0].T.astype(BF16), "ag_w_in", pltpu.HBM).reshape(N_DEV * c_in, d)
    w_in_pad = jnp.concatenate(
        [w_in_full[:REAL_FLOG_END], jnp.zeros((FLOG_PAD - N_FLOG, d), BF16), w_in_full[REAL_FLOG_END:]], axis=0
    )
    k_cut = REAL_FLOG_END // c_in

    c_all = _gather_rows(c, "ag_c")
    b_cols = lax.dynamic_slice(b_ada, (0, me * c_ada), (1, c_ada))
    ada_cols = _ada_fwd(c_all, w_ada[0], b_cols)
    ada_g = _all_gather(ada_cols, "ag_ada", pltpu.VMEM)
    ada = lax.dynamic_index_in_dim(ada_g, me, axis=1, keepdims=False).reshape(1, N_DEV * c_ada)

    h = _ln_mod(x2, ada)
    qkv_fox = _mm_cols(h, w_in_pad, OFF_FQ, 3 * FOX_W, BF16, "proj_fox")
    flog = _mm_cols(h, w_in_pad, OFF_FLOG, 128, F32, "proj_flog")
    qkv_swa = _mm_cols(h, w_in_pad, OFF_SQ, SWA_W + 2 * SWA_KVW, BF16, "proj_swa")
    gates, w_bf, w_bs, w_o = _mm_cols(
        h, w_in_pad, OFF_GF, in_pad - OFF_GF, BF16, "proj_gates",
        ride=(w_br_fox[0].astype(BF16), w_br_swa[0].astype(BF16), w_out[0].astype(BF16)),
    )
    w_bf = w_bf.reshape(N_DEV, FOX_W, c_br).transpose(1, 0, 2).reshape(FOX_W, d)
    w_bs = w_bs.reshape(N_DEV, SWA_W, c_br).transpose(1, 0, 2).reshape(SWA_W, d)
    w_o = w_o.reshape(d, d)
    mf_blk = (OFF_MF - OFF_GF) // d

    flog_t = flog[:, :N_FLOG].T
    bf_col = b_f.reshape(FOX_H, 1)
    cum = _fox_cum(flog_t, bf_col)
    cum_row = cum.reshape(FOX_H, 1, s_len)
    o_fox, lse = _fox_fwd(qkv_fox, cum_row)
    sinks = attn_sinks.reshape(SWA_HQ)
    o_swa = _swa_fwd(qkv_swa, sinks)

    y_fox, a_fox = _branch_fwd(o_fox, gates, 0, w_bf, "branch_fox")
    y_swa, a_swa = _branch_fwd(o_swa, gates, 1, w_bs, "branch_swa")
    merged, dza, dsub, red, dmf, dms, dy_fox, dy_swa = _out_stage(gates, mf_blk, y_fox, y_swa, w_o, x2, ada, ln_g, ln_b, tgt)
    loss = lax.psum(0.5 * red[4, 0] / d, ("x", "y", "c"))

    do_fox, dg_fox, delta = _branch_bwd(dy_fox, w_bf, o_fox, gates, 0, "branch_fox_bwd", FOX_H)
    do_swa, dg_swa = _branch_bwd(dy_swa, w_bs, o_swa, gates, 1, "branch_swa_bwd", 0)
    delta_row = delta[:, :FOX_H].T.reshape(FOX_H, 1, s_len)
    dq_f, dk_f, dv_f, dcol, drow = _fox_bwd(
        qkv_fox, cum.reshape(FOX_H, s_len, 1), lse.reshape(FOX_H, 1, s_len), delta_row, do_fox
    )
    dflog_t, dbf = _fox_gate_bwd(drow.reshape(FOX_H, s_len), dcol.reshape(FOX_H, s_len), flog_t, bf_col)
    dq_s, dk_s, dv_s, dsink = _swa_bwd(qkv_swa, sinks, do_swa)
    dflog = _pad_lanes(dflog_t.T, FLOG_PAD).astype(BF16)
    dproj = jnp.concatenate([dq_f, dk_f, dv_f, dflog, dq_s, dk_s, dv_s, dg_fox, dg_swa, dmf, dms], axis=1)
    g_w_bf = _mm_tn(a_fox, dy_fox, "grad_w_br_fox")
    g_w_bs = _mm_tn(a_swa, dy_swa, "grad_w_br_swa")
    g_w_o = _mm_tn(merged, dsub, "grad_w_out")
    g_w_in, r_bf, r_bs, r_o = _mm_tn(
        dproj, h, "grad_w_in",
        ride=(
            g_w_bf.reshape(FOX_W, N_DEV, c_br).transpose(1, 0, 2),
            g_w_bs.reshape(SWA_W, N_DEV, c_br).transpose(1, 0, 2),
            g_w_o.reshape(N_DEV, d // N_DEV, d),
        ),
    )
    pad = FLOG_PAD - N_FLOG
    g_blocks = jnp.stack(
        [g_w_in[k * c_in : (k + 1) * c_in] for k in range(k_cut)]
        + [jnp.concatenate([g_w_in[k_cut * c_in : REAL_FLOG_END], g_w_in[OFF_SQ : (k_cut + 1) * c_in + pad]], axis=0)]
        + [g_w_in[k * c_in + pad : (k + 1) * c_in + pad] for k in range(k_cut + 1, N_DEV)]
    )

    grad_x, red2, r_in = _in_bwd(dproj, w_in_pad, x2, ada, dza, ride=(g_blocks,))
    out_w_in = _sum_adam_t(r_in, w_in[0].T, m_w_in[0].T, v_w_in[0].T, "adam_w_in")
    out_w_in = [o.T for o in out_w_in]
    out_w_bf = _sum_adam(r_bf, w_br_fox[0], m_w_br_fox[0], v_w_br_fox[0], "adam_w_br_fox")
    out_w_bs = _sum_adam(r_bs, w_br_swa[0], m_w_br_swa[0], v_w_br_swa[0], "adam_w_br_swa")
    out_w_o = _sum_adam(r_o, w_out[0], m_w_out[0], v_w_out[0], "adam_w_out")

    packed = jnp.concatenate([red2[0:1], red2[1:2], red[0:1], _pad_lanes(dbf[:, 0].reshape(1, FOX_H), 128), dsink, red[1:2], red[2:3]], axis=1)
    gathered = _gather_rows(packed, "ag_small")
    pack = lambda a, b, cc, dd, e: jnp.concatenate([a, _pad_lanes(b, 128), _pad_lanes(cc, 128), dd, e], axis=1)
    small = _small_adam(
        gathered,
        pack(b_ada, b_f, attn_sinks, ln_g, ln_b),
        pack(m_b_ada, m_b_f, m_attn_sinks, m_ln_g, m_ln_b),
        pack(v_b_ada, v_b_f, v_attn_sinks, v_ln_g, v_ln_b),
    )
    dada_cols = lax.dynamic_slice(gathered, (0, me * c_ada), (N_DEV, c_ada))
    out_w_ada = _wada_adam(c_all.T, dada_cols, w_ada[0], m_w_ada[0], v_w_ada[0])

    o1, o2, o3 = 3 * d, 3 * d + 128, 3 * d + 256

    def unpack(p):
        return p[:, :o1], p[:, o1 : o1 + FOX_H], p[:, o2 : o2 + SWA_HQ], p[:, o3 : o3 + d], p[:, o3 + d : o3 + 2 * d]

    kinds = []
    for k in range(4):
        b_ada_k, b_f_k, sinks_k, ln_g_k, ln_b_k = unpack(small[k])
        kinds.append(
            [out_w_ada[k][None], b_ada_k, out_w_in[k][None], b_f_k, sinks_k, out_w_bf[k][None], out_w_bs[k][None], out_w_o[k][None], ln_g_k, ln_b_k]
        )
    return (loss, grad_x[None], *kinds[0], *kinds[1], *kinds[2], *kinds[3])
```
